```python
import jax, jax.numpy as jnp
from jax import lax
import numpy as np

D_MODEL = 1024
BATCH = 8
SEQ = 2048
DEPTH = 1

GLA_HEADS = 4
GLA_DK = D_MODEL // 8
GLA_DV = D_MODEL // 4
GLA_RANK = 16
GLA_NORMALIZER = 16.0
GLA_CHUNK = 64
MOBA_HEADS = 8
MOBA_DH = D_MODEL // 8
MOBA_BLOCK = 256
MOBA_TOPK = 3
MOBA_QCHUNK = 16
ROT_DIM = MOBA_DH // 4
ROPE_THETA = 500000.0
N_EXPERTS = 256
TOPK_EXPERTS = 8
N_GROUPS = 8
TOPK_GROUPS = 4
D_EXPERT = D_MODEL // 4
D_SHARED = D_MODEL // 4
ROUTED_SCALE = 2.5
DISPATCH_BLOCK = 128
PLE_DIM = 256
LN_EPS = 1e-5
ALPHA = (2.0 * DEPTH) ** 0.25
BETA = (8.0 * DEPTH) ** -0.25
GLA_KDIM = GLA_HEADS * GLA_DK
GLA_VDIM = GLA_HEADS * GLA_DV
MOBA_DIM = MOBA_HEADS * MOBA_DH
IN_COLS = 2 * GLA_KDIM + 2 * GLA_VDIM + GLA_RANK + 3 * MOBA_DIM + 2 * D_MODEL

kernel_name = 'hybrid_gla_moba_moe_deepnorm'

F32 = jnp.float32


def layer_norm(x, g, b):
    xf = x.astype(F32)
    mu = xf.mean(-1, keepdims=True)
    var = jnp.square(xf - mu).mean(-1, keepdims=True)
    return ((xf - mu) * lax.rsqrt(var + LN_EPS) * g.astype(F32) + b.astype(F32)).astype(x.dtype)


def split_in(proj):
    sizes = (GLA_KDIM, GLA_KDIM, GLA_VDIM, GLA_VDIM, GLA_RANK,
             MOBA_DIM, MOBA_DIM, MOBA_DIM, D_MODEL, D_MODEL)
    offs = [int(o) for o in np.cumsum(sizes)[:-1]]
    return jnp.split(proj, offs, axis=-1)


def rope_partial(x, positions):
    inv = ROPE_THETA ** (-jnp.arange(0, ROT_DIM, 2, dtype=F32) / ROT_DIM)
    ang = positions.astype(F32)[..., None] * inv
    cos = jnp.cos(ang)[:, :, None, :]
    sin = jnp.sin(ang)[:, :, None, :]
    xr = x[..., :ROT_DIM].astype(F32)
    x1, x2 = xr[..., :ROT_DIM // 2], xr[..., ROT_DIM // 2:]
    rot = jnp.concatenate([x1 * cos - x2 * sin, x1 * sin + x2 * cos], axis=-1)
    return jnp.concatenate([rot.astype(x.dtype), x[..., ROT_DIM:]], axis=-1)


def gla_mixer(q, k, v, r, lowrank, w_gk2, b_gk, norm_g):
    B, T, _ = q.shape
    dt = q.dtype
    nc = T // GLA_CHUNK
    gk = jax.nn.log_sigmoid((lowrank @ w_gk2 + b_gk).astype(F32)) / GLA_NORMALIZER

    def heads(t, d):
        return t.astype(F32).reshape(B, nc, GLA_CHUNK, GLA_HEADS, d).transpose(0, 3, 1, 2, 4)

    qh = heads(q, GLA_DK) * (GLA_DK ** -0.5)
    kh = heads(k, GLA_DK)
    vh = heads(v, GLA_DV)
    gh = heads(gk, GLA_DK)
    b = jnp.cumsum(gh, axis=3)
    b_end = b[:, :, :, -1:, :]
    q_e = qh * jnp.exp(b)
    k_e = kh * jnp.exp(-b)
    k_d = kh * jnp.exp(b_end - b)
    causal = jnp.tril(jnp.ones((GLA_CHUNK, GLA_CHUNK), bool))
    att = jnp.where(causal, jnp.einsum('bhncd,bhnsd->bhncs', q_e, k_e), 0.0)
    o_intra = jnp.einsum('bhncs,bhnse->bhnce', att, vh)
    upd = jnp.einsum('bhncd,bhnce->bhnde', k_d, vh)
    decay = jnp.exp(b_end[:, :, :, 0, :])

    def step(state, inp):
        dcy, u = inp
        return dcy[..., None] * state + u, state

    s0 = jnp.zeros((B, GLA_HEADS, GLA_DK, GLA_DV), F32)
    _, s_prev = lax.scan(step, s0, (jnp.moveaxis(decay, 2, 0), jnp.moveaxis(upd, 2, 0)))
    s_prev = jnp.moveaxis(s_prev, 0, 2)
    o = o_intra + jnp.einsum('bhncd,bhnde->bhnce', q_e, s_prev)
    o = o.transpose(0, 2, 3, 1, 4).reshape(B, T, GLA_HEADS, GLA_DV)
    o = o * lax.rsqrt(jnp.square(o).mean(-1, keepdims=True) + LN_EPS) * norm_g.astype(F32)
    o = o.reshape(B, T, GLA_VDIM) * jax.nn.silu(r.astype(F32))
    return o.astype(dt)


def moba_mixer(q, k, v, positions):
    B, T, _ = q.shape
    dt = q.dtype
    H, DH, BLK, QC = MOBA_HEADS, MOBA_DH, MOBA_BLOCK, MOBA_QCHUNK
    qh = rope_partial(q.reshape(B, T, H, DH), positions)
    kh = rope_partial(k.reshape(B, T, H, DH), positions)
    vh = v.reshape(B, T, H, DH)
    nb = -(-T // BLK)
    tp = nb * BLK
    K = min(MOBA_TOPK, nb)
    nq = tp // QC
    pad = ((0, 0), (0, tp - T), (0, 0), (0, 0))
    qh, kh, vh = [jnp.pad(t, pad).transpose(0, 2, 1, 3) for t in (qh, kh, vh)]
    k_blocks = kh.reshape(B, H, nb, BLK, DH)
    v_blocks = vh.reshape(B, H, nb, BLK, DH)
    k_mean = k_blocks.astype(F32).mean(axis=3)
    s_blk = jnp.einsum('bhtd,bhnd->bhtn', qh.astype(F32), k_mean)
    q_blk = jnp.arange(tp) // BLK
    past = jnp.arange(nb)[None, :] < q_blk[:, None]
    s_blk = jnp.where(past, s_blk, -jnp.inf)
    _, sel = lax.top_k(s_blk, K)
    q_chunks = qh.reshape(B, H, nq, QC, DH).transpose(2, 0, 1, 3, 4)
    sel_chunks = sel.reshape(B, H, nq, QC, K).transpose(2, 0, 1, 3, 4)
    bi = jnp.arange(B)[:, None, None, None]
    hi = jnp.arange(H)[None, :, None, None]
    scale = DH ** -0.5

    def chunk_attn(args):
        c, q_c, sel_c = args
        start = c * QC
        qb = start // BLK
        k_sel = k_blocks[bi, hi, sel_c]
        v_sel = v_blocks[bi, hi, sel_c]
        s_sel = jnp.einsum('bhqd,bhqkjd->bhqkj', q_c, k_sel).astype(F32) * scale
        valid = (jnp.arange(K) < qb)[:, None]
        s_sel = jnp.where(valid, s_sel, -jnp.inf).reshape(B, H, QC, K * BLK)
        k_own = lax.dynamic_index_in_dim(k_blocks, qb, axis=2, keepdims=False)
        v_own = lax.dynamic_index_in_dim(v_blocks, qb, axis=2, keepdims=False)
        s_own = jnp.einsum('bhqd,bhjd->bhqj', q_c, k_own).astype(F32) * scale
        causal = jnp.arange(BLK)[None, :] <= (start - qb * BLK + jnp.arange(QC))[:, None]
        s_own = jnp.where(causal, s_own, -jnp.inf)
        probs = jax.nn.softmax(jnp.concatenate([s_sel, s_own], axis=-1), axis=-1).astype(dt)
        p_sel = probs[..., :K * BLK].reshape(B, H, QC, K, BLK)
        p_own = probs[..., K * BLK:]
        return (jnp.einsum('bhqkj,bhqkjd->bhqd', p_sel, v_sel)
                + jnp.einsum('bhqj,bhjd->bhqd', p_own, v_own))

    out = lax.map(chunk_attn, (jnp.arange(nq), q_chunks, sel_chunks))
    out = out.transpose(1, 0, 3, 2, 4).reshape(B, tp, H * DH)[:, :T]
    return out.astype(dt)


def moe_ffn(x, w_router, b_router, w_e_gate, w_e_up, w_e_down, w_s_gate, w_s_up, w_s_down):
    B, T, D = x.shape
    n = B * T
    nk = n * TOPK_EXPERTS
    xf = x.reshape(n, D)
    scores = jax.nn.sigmoid((xf @ w_router).astype(F32))
    biased = scores + b_router.astype(F32)
    g_score = lax.top_k(biased.reshape(n, N_GROUPS, N_EXPERTS // N_GROUPS), 2)[0].sum(-1)
    _, g_idx = lax.top_k(g_score, TOPK_GROUPS)
    g_mask = jnp.any(g_idx[:, :, None] == jnp.arange(N_GROUPS)[None, None, :], axis=1)
    e_mask = jnp.repeat(g_mask, N_EXPERTS // N_GROUPS, axis=1)
    _, e_idx = lax.top_k(jnp.where(e_mask, biased, -jnp.inf), TOPK_EXPERTS)
    w_sel = jnp.take_along_axis(scores, e_idx, axis=1)
    w_sel = w_sel / w_sel.sum(-1, keepdims=True) * ROUTED_SCALE
    flat_e = e_idx.reshape(-1)
    flat_tok = jnp.repeat(jnp.arange(n, dtype=jnp.int32), TOPK_EXPERTS)
    flat_w = w_sel.reshape(-1)
    order = jnp.argsort(flat_e)
    s_e, s_tok, s_w = flat_e[order], flat_tok[order], flat_w[order]
    counts = jnp.bincount(flat_e, length=N_EXPERTS)
    starts = jnp.cumsum(counts) - counts
    pcounts = (counts + DISPATCH_BLOCK - 1) // DISPATCH_BLOCK * DISPATCH_BLOCK
    pends = jnp.cumsum(pcounts)
    pstarts = pends - pcounts
    dest = pstarts[s_e] + (jnp.arange(nk) - starts[s_e])
    n_blk = (nk + N_EXPERTS * (DISPATCH_BLOCK - 1) + DISPATCH_BLOCK - 1) // DISPATCH_BLOCK
    rows = n_blk * DISPATCH_BLOCK
    row_tok = jnp.full((rows,), n, jnp.int32).at[dest].set(s_tok)
    row_w = jnp.zeros((rows,), F32).at[dest].set(s_w)
    blk_e = jnp.clip(jnp.searchsorted(pends, jnp.arange(n_blk) * DISPATCH_BLOCK, side='right'),
                     0, N_EXPERTS - 1)
    x_pad = jnp.concatenate([xf, jnp.zeros((1, D), xf.dtype)], axis=0)

    def expert_block(args):
        tok, wr, e = args
        xb = x_pad[tok]
        h = jax.nn.silu(xb @ w_e_gate[e]) * (xb @ w_e_up[e])
        return (h @ w_e_down[e]) * wr[:, None].astype(xb.dtype)

    ys = lax.map(expert_block, (row_tok.reshape(n_blk, DISPATCH_BLOCK),
                                row_w.reshape(n_blk, DISPATCH_BLOCK), blk_e))
    routed = jax.ops.segment_sum(ys.reshape(rows, D), row_tok, num_segments=n + 1)[:n]
    shared = (jax.nn.silu(xf @ w_s_gate) * (xf @ w_s_up)) @ w_s_down
    return (routed + shared).reshape(B, T, D)


def setup_inputs(seed: int = 0) -> dict:
    key = jax.random.key(seed)
    ks = jax.random.split(key, 24)

    def nrm(k, shape, scale):
        return jax.random.normal(k, shape, F32) * scale

    L, D, E = DEPTH, D_MODEL, N_EXPERTS
    return {
        'x': nrm(ks[0], (BATCH, SEQ, D), 1.0),
        'p': nrm(ks[1], (DEPTH, BATCH, SEQ, PLE_DIM), 1.0),
        'positions': jnp.broadcast_to(jnp.arange(SEQ, dtype=jnp.int32), (BATCH, SEQ)),
        'w_in': nrm(ks[2], (L, D, IN_COLS), D ** -0.5),
        'w_gla_gk2': nrm(ks[3], (L, GLA_RANK, GLA_KDIM), GLA_RANK ** -0.5),
        'b_gla_gk': nrm(ks[4], (L, GLA_KDIM), 0.1),
        'gla_norm_g': 1.0 + nrm(ks[5], (L, GLA_DV), 0.02),
        'w_gla_o': nrm(ks[6], (L, GLA_VDIM, D), GLA_VDIM ** -0.5 * BETA),
        'w_moba_o': nrm(ks[7], (L, MOBA_DIM, D), MOBA_DIM ** -0.5 * BETA),
        'w_out': nrm(ks[8], (L, D, D), D ** -0.5 * BETA),
        'ln1_g': 1.0 + nrm(ks[9], (L, D), 0.02),
        'ln1_b': nrm(ks[10], (L, D), 0.02),
        'w_router': nrm(ks[11], (L, D, E), D ** -0.5),
        'b_router': nrm(ks[12], (L, E), 0.01),
        'w_e_gate': nrm(ks[13], (L, E, D, D_EXPERT), D ** -0.5),
        'w_e_up': nrm(ks[14], (L, E, D, D_EXPERT), D ** -0.5),
        'w_e_down': nrm(ks[15], (L, E, D_EXPERT, D), D_EXPERT ** -0.5 * BETA),
        'w_s_gate': nrm(ks[16], (L, D, D_SHARED), D ** -0.5),
        'w_s_up': nrm(ks[17], (L, D, D_SHARED), D ** -0.5),
        'w_s_down': nrm(ks[18], (L, D_SHARED, D), D_SHARED ** -0.5 * BETA),
        'w_ple': nrm(ks[19], (L, PLE_DIM, D), PLE_DIM ** -0.5 * BETA),
        'w_ple_gate': nrm(ks[20], (L, D, D), D ** -0.5),
        'ln2_g': 1.0 + nrm(ks[21], (L, D), 0.02),
        'ln2_b': nrm(ks[22], (L, D), 0.02),
    }


def reference(x, p, positions, w_in, w_gla_gk2, b_gla_gk, gla_norm_g, w_gla_o, w_moba_o,
              w_out, ln1_g, ln1_b, w_router, b_router, w_e_gate, w_e_up, w_e_down,
              w_s_gate, w_s_up, w_s_down, w_ple, w_ple_gate, ln2_g, ln2_b):
    h = x
    for i in range(DEPTH):
        proj = h @ w_in[i]
        (g_q, g_k, g_v, g_r, g_low, m_q, m_k, m_v, gate_a, gate_b) = split_in(proj)
        y_gla = gla_mixer(g_q, g_k, g_v, g_r, g_low, w_gla_gk2[i], b_gla_gk[i], gla_norm_g[i]) @ w_gla_o[i]
        y_moba = moba_mixer(m_q, m_k, m_v, positions) @ w_moba_o[i]
        merged = jax.nn.sigmoid(gate_a) * y_gla + jax.nn.sigmoid(gate_b) * y_moba
        mix = merged @ w_out[i]
        h1 = layer_norm(ALPHA * h + mix, ln1_g[i], ln1_b[i])
        ffn = moe_ffn(h1, w_router[i], b_router[i], w_e_gate[i], w_e_up[i], w_e_down[i],
                      w_s_gate[i], w_s_up[i], w_s_down[i])
        ple = jax.nn.sigmoid(h1 @ w_ple_gate[i]) * (p[i] @ w_ple[i])
        h = layer_norm(ALPHA * h1 + ffn + ple, ln2_g[i], ln2_b[i])
    return h
```

```python
import functools

import jax
import jax.numpy as jnp
from jax import lax
from jax.experimental import pallas as pl
from jax.experimental.pallas import tpu as pltpu

F32 = jnp.float32
BF16 = jnp.bfloat16
I32 = jnp.int32

LANES = 128
SUBLANES = 8
VMEM_LIMIT_BYTES = 48 * 1024 * 1024

D_MODEL = 1024
BATCH = 8
SEQ = 2048
N_TOK = BATCH * SEQ
GLA_HEADS = 4
GLA_DK = 128
GLA_DV = 256
GLA_RANK = 16
GLA_NORMALIZER = 16.0
GLA_CHUNK = 64
MOBA_HEADS = 8
MOBA_DH = 128
MOBA_BLOCK = 256
MOBA_TOPK = 3
ROT_DIM = 32
ROPE_THETA = 500000.0
N_EXPERTS = 256
TOPK_EXPERTS = 8
N_GROUPS = 8
GROUP_SIZE = N_EXPERTS // N_GROUPS
TOPK_GROUPS = 4
D_EXPERT = 256
D_SHARED = 256
ROUTED_SCALE = 2.5
PLE_DIM = 256
LN_EPS = 1e-5
DEPTH = 1
ALPHA = (2.0 * DEPTH) ** 0.25
GLA_KDIM = GLA_HEADS * GLA_DK
GLA_VDIM = GLA_HEADS * GLA_DV
MOBA_DIM = MOBA_HEADS * MOBA_DH
N_KBLK = SEQ // MOBA_BLOCK
CHUNKS_PER_ROW = D_MODEL // LANES

COL_GQ = 0
COL_GK = COL_GQ + GLA_KDIM
COL_GV = COL_GK + GLA_KDIM
COL_GR = COL_GV + GLA_VDIM
COL_MQ = COL_GR + GLA_VDIM
COL_MK = COL_MQ + MOBA_DIM
COL_MV = COL_MK + MOBA_DIM
COL_GA = COL_MV + MOBA_DIM
COL_GB = COL_GA + D_MODEL
COL_LOW = COL_GB + D_MODEL
PROJ_COLS = COL_LOW + LANES
LOW_SRC = 2 * GLA_KDIM + 2 * GLA_VDIM

ROW_TILE = 128
N_ASSIGN = N_TOK * TOPK_EXPERTS
N_ROW_TILES = (N_ASSIGN + N_EXPERTS * (ROW_TILE - 1) + ROW_TILE - 1) // ROW_TILE
N_ROWS = N_ROW_TILES * ROW_TILE
TILE_TABLE = -(-N_ROW_TILES // LANES) * LANES

PROJ_TM = 512
PROJ_TN = PROJ_COLS // 5
MIX_TM = 512
ROUTE_TM = 512
DEST_TM = 2048
DISP_TM = 256
FINAL_TM = 256

_NEG_INF = float("-inf")


def _cparams(*sem):
    return pltpu.CompilerParams(dimension_semantics=sem, vmem_limit_bytes=VMEM_LIMIT_BYTES)


def _dot(a, b):
    return jnp.dot(a, b, preferred_element_type=F32)


def _dot_nt(a, b):
    return lax.dot_general(a, b, (((1,), (1,)), ((), ())), preferred_element_type=F32)


def _dot_tn(a, b):
    return lax.dot_general(a, b, (((0,), (0,)), ((), ())), preferred_element_type=F32)


def _split_bf16(x):
    hi = x.astype(BF16)
    lo = (x - hi.astype(F32)).astype(BF16)
    return hi, lo


def _sigmoid(x):
    return 1.0 / (1.0 + jnp.exp(-x))


def _proj_body(x_ref, w_ref, o_ref):
    o_ref[...] = _dot(x_ref[...].astype(BF16), w_ref[...]).astype(BF16)


def _in_proj(x2d, w_cat):
    return pl.pallas_call(
        _proj_body,
        grid=(N_TOK // PROJ_TM, PROJ_COLS // PROJ_TN),
        in_specs=[
            pl.BlockSpec((PROJ_TM, D_MODEL), lambda i, j: (i, 0)),
            pl.BlockSpec((D_MODEL, PROJ_TN), lambda i, j: (0, j)),
        ],
        out_specs=pl.BlockSpec((PROJ_TM, PROJ_TN), lambda i, j: (i, j)),
        out_shape=jax.ShapeDtypeStruct((N_TOK, PROJ_COLS), BF16),
        compiler_params=_cparams("parallel", "arbitrary"),
        name="in_proj",
    )(x2d, w_cat)


def _gla_body(q_ref, k_ref, v_ref, r_ref, low_ref, wg_ref, bg_ref, ng_ref, o_ref, st_ref, gk_ref):
    w_hi, w_lo = _split_bf16(wg_ref[...])
    low = low_ref[...]
    lin = _dot(low, w_hi) + _dot(low, w_lo) + bg_ref[...]
    gk_ref[...] = (jnp.minimum(lin, 0.0) - jnp.log1p(jnp.exp(-jnp.abs(lin)))) * (1.0 / GLA_NORMALIZER)
    st_ref[...] = jnp.zeros_like(st_ref)

    ri = lax.broadcasted_iota(I32, (GLA_CHUNK, GLA_CHUNK), 0)
    ci = lax.broadcasted_iota(I32, (GLA_CHUNK, GLA_CHUNK), 1)
    causal = ri >= ci
    tri = jnp.where(causal, 1.0, 0.0).astype(BF16)
    gain = ng_ref[...]

    def chunk(c, carry):
        rows = pl.ds(pl.multiple_of(c * GLA_CHUNK, GLA_CHUNK), GLA_CHUNK)
        g_hi, g_lo = _split_bf16(gk_ref[rows, :])
        b = _dot(tri, g_hi) + _dot(tri, g_lo)
        b_end = b[GLA_CHUNK - 1:GLA_CHUNK, :]
        q = q_ref[rows, :].astype(F32) * (GLA_DK ** -0.5)
        k = k_ref[rows, :].astype(F32)
        v = v_ref[rows, :]
        q_e = (q * jnp.exp(b)).astype(BF16)
        k_e = (k * jnp.exp(-b)).astype(BF16)
        k_d = (k * jnp.exp(b_end - b)).astype(BF16)
        att = jnp.where(causal, _dot_nt(q_e, k_e), 0.0)
        st = st_ref[...]
        o = _dot(att.astype(BF16), v) + _dot_nt(q_e, st.astype(BF16))
        st_ref[...] = st * jnp.exp(b_end) + _dot_tn(v, k_d)
        o = o * lax.rsqrt(jnp.mean(o * o, axis=-1, keepdims=True) + LN_EPS) * gain
        r = r_ref[rows, :].astype(F32)
        o_ref[rows, :] = (o * (r * _sigmoid(r))).astype(BF16)
        return carry

    lax.fori_loop(0, SEQ // GLA_CHUNK, chunk, 0)


def _gla(proj, w_gk2_pad, b_gk, norm_g):
    kb, vb = GLA_DK, GLA_DV
    return pl.pallas_call(
        _gla_body,
        grid=(BATCH, GLA_HEADS),
        in_specs=[
            pl.BlockSpec((SEQ, kb), lambda b, h: (b, COL_GQ // kb + h)),
            pl.BlockSpec((SEQ, kb), lambda b, h: (b, COL_GK // kb + h)),
            pl.BlockSpec((SEQ, vb), lambda b, h: (b, COL_GV // vb + h)),
            pl.BlockSpec((SEQ, vb), lambda b, h: (b, COL_GR // vb + h)),
            pl.BlockSpec((SEQ, LANES), lambda b, h: (b, COL_LOW // LANES)),
            pl.BlockSpec((LANES, kb), lambda b, h: (0, h)),
            pl.BlockSpec((1, kb), lambda b, h: (0, h)),
            pl.BlockSpec((1, vb), lambda b, h: (0, 0)),
        ],
        out_specs=pl.BlockSpec((SEQ, vb), lambda b, h: (b, h)),
        out_shape=jax.ShapeDtypeStruct((N_TOK, GLA_VDIM), BF16),
        scratch_shapes=[pltpu.VMEM((vb, kb), F32), pltpu.VMEM((SEQ, kb), F32)],
        compiler_params=_cparams("parallel", "parallel"),
        name="gla",
    )(proj, proj, proj, proj, proj, w_gk2_pad, b_gk, norm_g)


def _moba_body(q_ref, k_ref, v_ref, c_ref, s_ref, o_ref, qs_ref, ks_ref, vt_ref, bias_ref):
    cos_t = c_ref[0]
    sin_t = s_ref[0]
    lane = lax.broadcasted_iota(I32, (SEQ, MOBA_DH), 1)
    half = ROT_DIM // 2

    def rope(x):
        partner = jnp.where(lane < half, pltpu.roll(x, MOBA_DH - half, 1), pltpu.roll(x, half, 1))
        return x * cos_t + partner * sin_t

    q = rope(q_ref[...].astype(F32))
    k = rope(k_ref[...].astype(F32))
    q_hi, q_lo = _split_bf16(q)
    qs_ref[...] = q_hi
    ks_ref[...] = k.astype(BF16)
    vt_ref[...] = v_ref[...].astype(F32).T.astype(BF16)

    k_mean = jnp.concatenate(
        [jnp.mean(k[j * MOBA_BLOCK:(j + 1) * MOBA_BLOCK], axis=0, keepdims=True) for j in range(N_KBLK)], axis=0)
    m_hi, m_lo = _split_bf16(k_mean)
    s_blk = _dot_nt(m_hi, q_hi) + _dot_nt(m_hi, q_lo) + _dot_nt(m_lo, q_hi)
    blk = lax.broadcasted_iota(I32, (N_KBLK, SEQ), 0)
    q_blk = lax.shift_right_logical(lax.broadcasted_iota(I32, (N_KBLK, SEQ), 1), MOBA_BLOCK.bit_length() - 1)
    past = blk < q_blk
    s_blk = jnp.where(past, s_blk, _NEG_INF)
    beaten = jnp.zeros((N_KBLK, SEQ), I32)
    for j in range(N_KBLK):
        row = s_blk[j:j + 1, :]
        beaten += jnp.where((row > s_blk) | ((row == s_blk) & (j < blk)), 1, 0)
    bias_ref[...] = jnp.where(past & (beaten < MOBA_TOPK), 0.0, _NEG_INF)

    kr = lax.broadcasted_iota(I32, (MOBA_BLOCK, MOBA_BLOCK), 0)
    qc = lax.broadcasted_iota(I32, (MOBA_BLOCK, MOBA_BLOCK), 1)
    own_bias = jnp.where(kr <= qc, 0.0, _NEG_INF)
    scale = MOBA_DH ** -0.5

    for i in range(N_KBLK):
        cols = slice(i * MOBA_BLOCK, (i + 1) * MOBA_BLOCK)
        n_keys = (i + 1) * MOBA_BLOCK
        s = _dot_nt(ks_ref[0:n_keys, :], qs_ref[cols, :]) * scale
        parts = []
        for j in range(i + 1):
            bias = own_bias if j == i else bias_ref[j:j + 1, cols]
            parts.append(s[j * MOBA_BLOCK:(j + 1) * MOBA_BLOCK] + bias)
        m = functools.reduce(jnp.maximum, [jnp.max(p, axis=0, keepdims=True) for p in parts])
        probs = [jnp.exp(p - m) for p in parts]
        denom = functools.reduce(jnp.add, [jnp.sum(p, axis=0, keepdims=True) for p in probs])
        p_all = jnp.concatenate(probs, axis=0).astype(BF16)
        o_t = _dot(vt_ref[:, 0:n_keys], p_all) * (1.0 / denom)
        o_ref[cols, :] = o_t.T.astype(BF16)


def _moba(proj, cos_t, sin_t):
    dh = MOBA_DH
    return pl.pallas_call(
        _moba_body,
        grid=(BATCH, MOBA_HEADS),
        in_specs=[
            pl.BlockSpec((SEQ, dh), lambda b, h: (b, COL_MQ // dh + h)),
            pl.BlockSpec((SEQ, dh), lambda b, h: (b, COL_MK // dh + h)),
            pl.BlockSpec((SEQ, dh), lambda b, h: (b, COL_MV // dh + h)),
            pl.BlockSpec((1, SEQ, dh), lambda b, h: (b, 0, 0)),
            pl.BlockSpec((1, SEQ, dh), lambda b, h: (b, 0, 0)),
        ],
        out_specs=pl.BlockSpec((SEQ, dh), lambda b, h: (b, h)),
        out_shape=jax.ShapeDtypeStruct((N_TOK, MOBA_DIM), BF16),
        scratch_shapes=[
            pltpu.VMEM((SEQ, dh), BF16),
            pltpu.VMEM((SEQ, dh), BF16),
            pltpu.VMEM((dh, SEQ), BF16),
            pltpu.VMEM((N_KBLK, SEQ), F32),
        ],
        compiler_params=_cparams("parallel", "parallel"),
        name="moba",
    )(proj, proj, proj, cos_t, sin_t)


def _layer_norm(z, g, b):
    mu = jnp.mean(z, axis=-1, keepdims=True)
    zc = z - mu
    var = jnp.mean(zc * zc, axis=-1, keepdims=True)
    return zc * lax.rsqrt(var + LN_EPS) * g + b


def _mix_body(gla_ref, moba_ref, ga_ref, gb_ref, x_ref, wgo_ref, wmo_ref, wo_ref, g_ref, b_ref, h3_ref, hb_ref):
    y_gla = _dot(gla_ref[...], wgo_ref[...])
    y_moba = _dot(moba_ref[...], wmo_ref[...])
    merged = _sigmoid(ga_ref[...].astype(F32)) * y_gla + _sigmoid(gb_ref[...].astype(F32)) * y_moba
    mix = _dot(merged.astype(BF16), wo_ref[...])
    h = _layer_norm(ALPHA * x_ref[...] + mix, g_ref[...], b_ref[...])
    hb_ref[...] = h.astype(BF16)
    for c in range(CHUNKS_PER_ROW):
        h3_ref[:, c, :] = h[:, c * LANES:(c + 1) * LANES]


def _mix(gla_out, moba_out, proj, x2d, w_gla_o, w_moba_o, w_out, ln_g, ln_b):
    d = D_MODEL
    row = lambda i: (i, 0)
    full = lambda i: (0, 0)
    return pl.pallas_call(
        _mix_body,
        grid=(N_TOK // MIX_TM,),
        in_specs=[
            pl.BlockSpec((MIX_TM, d), row),
            pl.BlockSpec((MIX_TM, d), row),
            pl.BlockSpec((MIX_TM, d), lambda i: (i, COL_GA // d)),
            pl.BlockSpec((MIX_TM, d), lambda i: (i, COL_GB // d)),
            pl.BlockSpec((MIX_TM, d), row),
            pl.BlockSpec((d, d), full),
            pl.BlockSpec((d, d), full),
            pl.BlockSpec((d, d), full),
            pl.BlockSpec((1, d), full),
            pl.BlockSpec((1, d), full),
        ],
        out_specs=[
            pl.BlockSpec((MIX_TM, CHUNKS_PER_ROW, LANES), lambda i: (i, 0, 0)),
            pl.BlockSpec((MIX_TM, d), row),
        ],
        out_shape=[
            jax.ShapeDtypeStruct((N_TOK, CHUNKS_PER_ROW, LANES), F32),
            jax.ShapeDtypeStruct((N_TOK, d), BF16),
        ],
        compiler_params=_cparams("parallel"),
        name="mix_ln1",
    )(gla_out, moba_out, proj, proj, x2d, w_gla_o, w_moba_o, w_out, ln_g, ln_b)


def _route_body(h_ref, wr_ref, br_ref, e_ref, w_ref, rk_ref, cnt_ref, carry_ref):
    tm = ROUTE_TM

    @pl.when(pl.program_id(0) == 0)
    def _():
        carry_ref[...] = jnp.zeros_like(carry_ref)

    scores = _sigmoid(_dot_nt(wr_ref[...], h_ref[...]))
    biased = scores + br_ref[...]
    row = lax.broadcasted_iota(I32, (N_EXPERTS, tm), 0).astype(F32)
    row_g = lax.broadcasted_iota(I32, (GROUP_SIZE, tm), 0).astype(F32)

    g_scores = []
    for g in range(N_GROUPS):
        grp = biased[g * GROUP_SIZE:(g + 1) * GROUP_SIZE]
        m1 = jnp.max(grp, axis=0, keepdims=True)
        first = jnp.min(jnp.where(grp == m1, row_g, float(GROUP_SIZE)), axis=0, keepdims=True)
        m2 = jnp.max(jnp.where(row_g == first, _NEG_INF, grp), axis=0, keepdims=True)
        g_scores.append(m1 + m2)
    g_score = jnp.concatenate(g_scores, axis=0)
    g_row = lax.broadcasted_iota(I32, (N_GROUPS, tm), 0)
    g_beaten = jnp.zeros((N_GROUPS, tm), I32)
    for g in range(N_GROUPS):
        r = g_score[g:g + 1, :]
        g_beaten += jnp.where((r > g_score) | ((r == g_score) & (g < g_row)), 1, 0)
    g_keep = g_beaten < TOPK_GROUPS
    masked = jnp.concatenate(
        [jnp.where(g_keep[g:g + 1, :], biased[g * GROUP_SIZE:(g + 1) * GROUP_SIZE], _NEG_INF)
         for g in range(N_GROUPS)], axis=0)

    onehot = jnp.zeros((N_EXPERTS, tm), F32)
    picks, pick_scores = [], []
    for _ in range(TOPK_EXPERTS):
        m = jnp.max(masked, axis=0, keepdims=True)
        idx = jnp.min(jnp.where(masked == m, row, float(N_EXPERTS)), axis=0, keepdims=True)
        hit = row == idx
        picks.append(idx)
        pick_scores.append(jnp.sum(jnp.where(hit, scores, 0.0), axis=0, keepdims=True))
        onehot = onehot + jnp.where(hit, 1.0, 0.0)
        masked = jnp.where(hit, _NEG_INF, masked)
    sel = jnp.concatenate(pick_scores, axis=0)
    e_ref[...] = jnp.concatenate(picks, axis=0).astype(I32)
    w_ref[...] = sel / jnp.sum(sel, axis=0, keepdims=True) * ROUTED_SCALE

    t_r = lax.broadcasted_iota(I32, (tm, tm), 0)
    t_c = lax.broadcasted_iota(I32, (tm, tm), 1)
    earlier = jnp.where(t_r < t_c, 1.0, 0.0).astype(BF16)
    seen = _dot(onehot.astype(BF16), earlier) + carry_ref[...]
    rk_ref[...] = jnp.concatenate(
        [jnp.sum(jnp.where(row == idx, seen, 0.0), axis=0, keepdims=True) for idx in picks], axis=0).astype(I32)
    carry_ref[...] += jnp.sum(onehot, axis=1, keepdims=True)
    cnt_ref[...] = carry_ref[...]


def _route(h_bf, w_router_t, b_router_col):
    k = TOPK_EXPERTS
    tok = lambda i: (0, i)
    return pl.pallas_call(
        _route_body,
        grid=(N_TOK // ROUTE_TM,),
        in_specs=[
            pl.BlockSpec((ROUTE_TM, D_MODEL), lambda i: (i, 0)),
            pl.BlockSpec((N_EXPERTS, D_MODEL), lambda i: (0, 0)),
            pl.BlockSpec((N_EXPERTS, 1), lambda i: (0, 0)),
        ],
        out_specs=[
            pl.BlockSpec((k, ROUTE_TM), tok),
            pl.BlockSpec((k, ROUTE_TM), tok),
            pl.BlockSpec((k, ROUTE_TM), tok),
            pl.BlockSpec((N_EXPERTS, 1), lambda i: (0, 0)),
        ],
        out_shape=[
            jax.ShapeDtypeStruct((k, N_TOK), I32),
            jax.ShapeDtypeStruct((k, N_TOK), F32),
            jax.ShapeDtypeStruct((k, N_TOK), I32),
            jax.ShapeDtypeStruct((N_EXPERTS, 1), F32),
        ],
        scratch_shapes=[pltpu.VMEM((N_EXPERTS, 1), F32)],
        compiler_params=_cparams("arbitrary"),
        name="route",
    )(h_bf, w_router_t, b_router_col)


def _dest_body(cnt_ref, e_ref, rk_ref, d_ref, te_ref, nu_ref):
    cnt = cnt_ref[...]
    tiles = jnp.floor((cnt + (ROW_TILE - 1)) * (1.0 / ROW_TILE))
    er = lax.broadcasted_iota(I32, (N_EXPERTS, N_EXPERTS), 0)
    ec = lax.broadcasted_iota(I32, (N_EXPERTS, N_EXPERTS), 1)
    before = jnp.where(ec < er, 1.0, 0.0).astype(BF16)
    tiles_b = jnp.broadcast_to(tiles, (N_EXPERTS, LANES)).astype(BF16)
    t_start = _dot(before, tiles_b)[:, 0:1]
    t_end = t_start + tiles
    p_start = t_start * float(ROW_TILE)

    row = lax.broadcasted_iota(I32, (N_EXPERTS, DEST_TM), 0)
    d_ref[...] = jnp.concatenate(
        [jnp.sum(jnp.where(row == e_ref[k:k + 1, :], p_start, 0.0), axis=0, keepdims=True)
         for k in range(TOPK_EXPERTS)], axis=0).astype(I32) + rk_ref[...]

    tile_id = lax.broadcasted_iota(I32, (N_EXPERTS, TILE_TABLE), 1).astype(F32)
    owner = jnp.sum(jnp.where(t_end <= tile_id, 1, 0), axis=0, keepdims=True)
    te_ref[...] = jnp.minimum(owner, N_EXPERTS - 1)
    nu_ref[...] = jnp.broadcast_to(t_end[N_EXPERTS - 1:N_EXPERTS, :], (1, LANES)).astype(I32)


def _dest(cnt, e_t, rk_t):
    k = TOPK_EXPERTS
    tok = lambda i: (0, i)
    return pl.pallas_call(
        _dest_body,
        grid=(N_TOK // DEST_TM,),
        in_specs=[
            pl.BlockSpec((N_EXPERTS, 1), lambda i: (0, 0)),
            pl.BlockSpec((k, DEST_TM), tok),
            pl.BlockSpec((k, DEST_TM), tok),
        ],
        out_specs=[
            pl.BlockSpec((k, DEST_TM), tok),
            pl.BlockSpec((1, TILE_TABLE), lambda i: (0, 0)),
            pl.BlockSpec((1, LANES), lambda i: (0, 0)),
        ],
        out_shape=[
            jax.ShapeDtypeStruct((k, N_TOK), I32),
            jax.ShapeDtypeStruct((1, TILE_TABLE), I32),
            jax.ShapeDtypeStruct((1, LANES), I32),
        ],
        compiler_params=_cparams("arbitrary"),
        name="dest",
    )(cnt, e_t, rk_t)


def _row_copy(src, dst, sem):
    return pltpu.make_async_copy(src, dst, sem)


def _dispatch_body(d_ref, h3_ref, xs_in_ref, xs_ref, sem):
    del xs_in_ref

    def issue(t, carry):
        for k in range(TOPK_EXPERTS):
            _row_copy(h3_ref.at[t], xs_ref.at[d_ref[k, t]], sem).start()
        return carry

    lax.fori_loop(0, DISP_TM, issue, 0)

    def drain(t, carry):
        for k in range(TOPK_EXPERTS):
            _row_copy(h3_ref.at[t], xs_ref.at[d_ref[k, t]], sem).wait()
        return carry

    lax.fori_loop(0, DISP_TM, drain, 0)


def _dispatch(dest_t, h3, xs_zero):
    return pl.pallas_call(
        _dispatch_body,
        grid=(N_TOK // DISP_TM,),
        in_specs=[
            pl.BlockSpec((TOPK_EXPERTS, DISP_TM), lambda i: (0, i), memory_space=pltpu.SMEM),
            pl.BlockSpec((DISP_TM, CHUNKS_PER_ROW, LANES), lambda i: (i, 0, 0)),
            pl.BlockSpec(memory_space=pl.ANY),
        ],
        out_specs=pl.BlockSpec(memory_space=pl.ANY),
        out_shape=jax.ShapeDtypeStruct((N_ROWS, CHUNKS_PER_ROW, LANES), F32),
        scratch_shapes=[pltpu.SemaphoreType.DMA],
        input_output_aliases={2: 0},
        compiler_params=_cparams("arbitrary"),
        name="dispatch",
    )(dest_t, h3, xs_zero)


def _experts_body(te_ref, nu_ref, xs_ref, wg_ref, wu_ref, wd_ref, ys_ref, wg_b, wu_b, wd_b):
    i = pl.program_id(0)

    @pl.when(i < nu_ref[0])
    def _():
        prev = te_ref[jnp.maximum(i - 1, 0)]

        @pl.when((i == 0) | (te_ref[i] != prev))
        def _():
            wg_b[...] = wg_ref[0].astype(BF16)
            wu_b[...] = wu_ref[0].astype(BF16)
            wd_b[...] = wd_ref[0].astype(BF16)

        x = jnp.concatenate([xs_ref[:, c, :] for c in range(CHUNKS_PER_ROW)], axis=1).astype(BF16)
        g = _dot(x, wg_b[...])
        u = _dot(x, wu_b[...])
        h = (g * _sigmoid(g)) * u
        y = _dot(h.astype(BF16), wd_b[...])
        for c in range(CHUNKS_PER_ROW):
            ys_ref[:, c, :] = y[:, c * LANES:(c + 1) * LANES]


def _experts(tile_expert, n_used, xs, w_gate, w_up, w_down):
    def tile(i, te, nu):
        return (jnp.minimum(i, nu[0] - 1), 0, 0)

    def expert(i, te, nu):
        return (te[jnp.minimum(i, nu[0] - 1)], 0, 0)

    grid_spec = pltpu.PrefetchScalarGridSpec(
        num_scalar_prefetch=2,
        grid=(N_ROW_TILES,),
        in_specs=[
            pl.BlockSpec((ROW_TILE, CHUNKS_PER_ROW, LANES), tile),
            pl.BlockSpec((1, D_MODEL, D_EXPERT), expert),
            pl.BlockSpec((1, D_MODEL, D_EXPERT), expert),
            pl.BlockSpec((1, D_EXPERT, D_MODEL), expert),
        ],
        out_specs=pl.BlockSpec((ROW_TILE, CHUNKS_PER_ROW, LANES), tile),
        scratch_shapes=[
            pltpu.VMEM((D_MODEL, D_EXPERT), BF16),
            pltpu.VMEM((D_MODEL, D_EXPERT), BF16),
            pltpu.VMEM((D_EXPERT, D_MODEL), BF16),
        ],
    )
    return pl.pallas_call(
        _experts_body,
        grid_spec=grid_spec,
        out_shape=jax.ShapeDtypeStruct((N_ROWS, CHUNKS_PER_ROW, LANES), F32),
        compiler_params=_cparams("arbitrary"),
        name="experts",
    )(tile_expert, n_used, xs, w_gate, w_up, w_down)


def _final_body(d_ref, w_ref, ys_ref, h3_ref, hb_ref, p_ref, wsg_ref, wsu_ref, wsd_ref, wpl_ref, wpg_ref,
                g_ref, b_ref, o_ref, buf_ref, acc_ref, sem):
    def issue(t, carry):
        for k in range(TOPK_EXPERTS):
            _row_copy(ys_ref.at[d_ref[k, t]], buf_ref.at[k, t], sem).start()
        return carry

    lax.fori_loop(0, FINAL_TM, issue, 0)

    hb = hb_ref[...]
    sg = _dot(hb, wsg_ref[...])
    shared = _dot(((sg * _sigmoid(sg)) * _dot(hb, wsu_ref[...])).astype(BF16), wsd_ref[...])
    ple = _sigmoid(_dot(hb, wpg_ref[...])) * _dot(p_ref[...].astype(BF16), wpl_ref[...])

    def drain(t, carry):
        for k in range(TOPK_EXPERTS):
            _row_copy(ys_ref.at[d_ref[k, t]], buf_ref.at[k, t], sem).wait()
        return carry

    lax.fori_loop(0, FINAL_TM, drain, 0)

    def combine(t, carry):
        acc = w_ref[0, t] * buf_ref[0, t]
        for k in range(1, TOPK_EXPERTS):
            acc = acc + w_ref[k, t] * buf_ref[k, t]
        acc_ref[t] = acc
        return carry

    lax.fori_loop(0, FINAL_TM, combine, 0)

    routed = jnp.concatenate([acc_ref[:, c, :] for c in range(CHUNKS_PER_ROW)], axis=1)
    h1 = jnp.concatenate([h3_ref[:, c, :] for c in range(CHUNKS_PER_ROW)], axis=1)
    o_ref[...] = _layer_norm(ALPHA * h1 + (routed + shared) + ple, g_ref[...], b_ref[...])


def _final(dest_t, w_t, ys, h3, h_bf, p2d, w_s_gate, w_s_up, w_s_down, w_ple, w_ple_gate, ln_g, ln_b):
    d = D_MODEL
    full = lambda i: (0, 0)
    smem_tok = lambda i: (0, i)
    return pl.pallas_call(
        _final_body,
        grid=(N_TOK // FINAL_TM,),
        in_specs=[
            pl.BlockSpec((TOPK_EXPERTS, FINAL_TM), smem_tok, memory_space=pltpu.SMEM),
            pl.BlockSpec((TOPK_EXPERTS, FINAL_TM), smem_tok, memory_space=pltpu.SMEM),
            pl.BlockSpec(memory_space=pl.ANY),
            pl.BlockSpec((FINAL_TM, CHUNKS_PER_ROW, LANES), lambda i: (i, 0, 0)),
            pl.BlockSpec((FINAL_TM, d), lambda i: (i, 0)),
            pl.BlockSpec((FINAL_TM, PLE_DIM), lambda i: (i, 0)),
            pl.BlockSpec((d, D_SHARED), full),
            pl.BlockSpec((d, D_SHARED), full),
            pl.BlockSpec((D_SHARED, d), full),
            pl.BlockSpec((PLE_DIM, d), full),
            pl.BlockSpec((d, d), full),
            pl.BlockSpec((1, d), full),
            pl.BlockSpec((1, d), full),
        ],
        out_specs=pl.BlockSpec((FINAL_TM, d), lambda i: (i, 0)),
        out_shape=jax.ShapeDtypeStruct((N_TOK, d), F32),
        scratch_shapes=[
            pltpu.VMEM((TOPK_EXPERTS, FINAL_TM, CHUNKS_PER_ROW, LANES), F32),
            pltpu.VMEM((FINAL_TM, CHUNKS_PER_ROW, LANES), F32),
            pltpu.SemaphoreType.DMA,
        ],
        compiler_params=_cparams("arbitrary"),
        name="combine_ln2",
    )(dest_t, w_t, ys, h3, h_bf, p2d, w_s_gate, w_s_up, w_s_down, w_ple, w_ple_gate, ln_g, ln_b)


def _rope_tables(positions):
    half = ROT_DIM // 2
    inv = ROPE_THETA ** (-jnp.arange(0, ROT_DIM, 2, dtype=F32) / ROT_DIM)
    ang = positions.astype(F32)[..., None] * inv
    cos, sin = jnp.cos(ang), jnp.sin(ang)
    rest = MOBA_DH - ROT_DIM
    ones = jnp.ones(cos.shape[:-1] + (rest,), F32)
    zeros = jnp.zeros(cos.shape[:-1] + (rest,), F32)
    return (jnp.concatenate([cos, cos, ones], axis=-1), jnp.concatenate([-sin, sin, zeros], axis=-1))


def _layer(h2d, p2d, cos_t, sin_t, w_in, w_gk2, b_gk, norm_g, w_gla_o, w_moba_o, w_out, ln1_g, ln1_b,
           w_router, b_router, w_e_gate, w_e_up, w_e_down, w_s_gate, w_s_up, w_s_down, w_ple, w_ple_gate,
           ln2_g, ln2_b):
    low_end = LOW_SRC + GLA_RANK
    w_cat = jnp.concatenate(
        [w_in[:, :LOW_SRC], w_in[:, low_end:], w_in[:, LOW_SRC:low_end],
         jnp.zeros((D_MODEL, LANES - GLA_RANK), w_in.dtype)], axis=1).astype(BF16)
    w_gk2_pad = jnp.concatenate([w_gk2, jnp.zeros((LANES - GLA_RANK, GLA_KDIM), w_gk2.dtype)], axis=0)

    proj = _in_proj(h2d, w_cat)
    gla_out = _gla(proj, w_gk2_pad, b_gk[None, :], norm_g[None, :])
    moba_out = _moba(proj, cos_t, sin_t)
    h3, h_bf = _mix(gla_out, moba_out, proj, h2d, w_gla_o.astype(BF16), w_moba_o.astype(BF16),
                    w_out.astype(BF16), ln1_g[None, :], ln1_b[None, :])
    e_t, w_t, rk_t, cnt = _route(h_bf, w_router.T.astype(BF16), b_router[:, None])
    dest_t, tile_expert, n_used = _dest(cnt, e_t, rk_t)
    xs = _dispatch(dest_t, h3, jnp.zeros((N_ROWS, CHUNKS_PER_ROW, LANES), F32))
    ys = _experts(tile_expert.reshape(TILE_TABLE), n_used[0, 0:1], xs, w_e_gate, w_e_up, w_e_down)
    return _final(dest_t, w_t, ys, h3, h_bf, p2d, w_s_gate.astype(BF16), w_s_up.astype(BF16),
                  w_s_down.astype(BF16), w_ple.astype(BF16), w_ple_gate.astype(BF16),
                  ln2_g[None, :], ln2_b[None, :])


def kernel(x, p, positions, w_in, w_gla_gk2, b_gla_gk, gla_norm_g, w_gla_o, w_moba_o, w_out, ln1_g, ln1_b,
           w_router, b_router, w_e_gate, w_e_up, w_e_down, w_s_gate, w_s_up, w_s_down, w_ple, w_ple_gate,
           ln2_g, ln2_b):
    cos_t, sin_t = _rope_tables(positions)
    h = x.reshape(N_TOK, D_MODEL)
    for i in range(DEPTH):
        h = _layer(h, p[i].reshape(N_TOK, PLE_DIM), cos_t, sin_t, w_in[i], w_gla_gk2[i], b_gla_gk[i],
                   gla_norm_g[i], w_gla_o[i], w_moba_o[i], w_out[i], ln1_g[i], ln1_b[i], w_router[i],
                   b_router[i], w_e_gate[i], w_e_up[i], w_e_down[i], w_s_gate[i], w_s_up[i], w_s_down[i],
                   w_ple[i], w_ple_gate[i], ln2_g[i], ln2_b[i])
    return h.reshape(BATCH, SEQ, D_MODEL)
```

```python
import functools

import jax
import jax.numpy as jnp
from jax import lax
from jax.experimental import pallas as pl
from jax.experimental.pallas import tpu as pltpu

F32 = jnp.float32
BF16 = jnp.bfloat16
I32 = jnp.int32

LANES = 128
SUBLANES = 8
VMEM_LIMIT_BYTES = 48 * 1024 * 1024

D_MODEL = 1024
BATCH = 8
SEQ = 2048
N_TOK = BATCH * SEQ
GLA_HEADS = 4
GLA_DK = 128
GLA_DV = 256
GLA_RANK = 16
GLA_NORMALIZER = 16.0
GLA_CHUNK = 64
MOBA_HEADS = 8
MOBA_DH = 128
MOBA_BLOCK = 256
MOBA_TOPK = 3
ROT_DIM = 32
ROPE_THETA = 500000.0
N_EXPERTS = 256
TOPK_EXPERTS = 8
N_GROUPS = 8
GROUP_SIZE = N_EXPERTS // N_GROUPS
TOPK_GROUPS = 4
D_EXPERT = 256
D_SHARED = 256
ROUTED_SCALE = 2.5
PLE_DIM = 256
LN_EPS = 1e-5
DEPTH = 1
ALPHA = (2.0 * DEPTH) ** 0.25
GLA_KDIM = GLA_HEADS * GLA_DK
GLA_VDIM = GLA_HEADS * GLA_DV
MOBA_DIM = MOBA_HEADS * MOBA_DH
N_KBLK = SEQ // MOBA_BLOCK

COL_GQ = 0
COL_GK = COL_GQ + GLA_KDIM
COL_GV = COL_GK + GLA_KDIM
COL_GR = COL_GV + GLA_VDIM
COL_MQ = COL_GR + GLA_VDIM
COL_MK = COL_MQ + MOBA_DIM
COL_MV = COL_MK + MOBA_DIM
COL_GA = COL_MV + MOBA_DIM
COL_GB = COL_GA + D_MODEL
COL_LOW = COL_GB + D_MODEL
PROJ_COLS = COL_LOW + LANES
LOW_SRC = 2 * GLA_KDIM + 2 * GLA_VDIM

ROW_TILE = 256
N_ASSIGN = N_TOK * TOPK_EXPERTS
N_ROW_TILES = (N_ASSIGN + N_EXPERTS * (ROW_TILE - 1) + ROW_TILE - 1) // ROW_TILE
N_ROWS = N_ROW_TILES * ROW_TILE
TILE_TABLE = -(-N_ROW_TILES // LANES) * LANES

PROJ_TM = 512
PROJ_TN = PROJ_COLS // 5
MIX_TM = 512
ROUTE_TM = 512
DEST_TM = 2048
DISP_TM = 256
FINAL_TM = 256

_NEG_INF = float("-inf")


def _cparams(*sem):
    return pltpu.CompilerParams(dimension_semantics=sem, vmem_limit_bytes=VMEM_LIMIT_BYTES)


def _dot(a, b):
    return jnp.dot(a, b, preferred_element_type=F32)


def _dot_nt(a, b):
    return lax.dot_general(a, b, (((1,), (1,)), ((), ())), preferred_element_type=F32)


def _dot_tn(a, b):
    return lax.dot_general(a, b, (((0,), (0,)), ((), ())), preferred_element_type=F32)


def _split_bf16(x):
    hi = x.astype(BF16)
    lo = (x - hi.astype(F32)).astype(BF16)
    return hi, lo


def _sigmoid(x):
    return 1.0 / (1.0 + jnp.exp(-x))


def _proj_body(x_ref, w_ref, o_ref):
    o_ref[...] = _dot(x_ref[...].astype(BF16), w_ref[...]).astype(BF16)


def _in_proj(x2d, w_cat):
    return pl.pallas_call(
        _proj_body,
        grid=(N_TOK // PROJ_TM, PROJ_COLS // PROJ_TN),
        in_specs=[
            pl.BlockSpec((PROJ_TM, D_MODEL), lambda i, j: (i, 0)),
            pl.BlockSpec((D_MODEL, PROJ_TN), lambda i, j: (0, j)),
        ],
        out_specs=pl.BlockSpec((PROJ_TM, PROJ_TN), lambda i, j: (i, j)),
        out_shape=jax.ShapeDtypeStruct((N_TOK, PROJ_COLS), BF16),
        compiler_params=_cparams("parallel", "arbitrary"),
        name="in_proj",
    )(x2d, w_cat)


def _gla_body(q_ref, k_ref, v_ref, r_ref, low_ref, wg_ref, bg_ref, ng_ref, o_ref, st_ref, gk_ref):
    w_hi, w_lo = _split_bf16(wg_ref[...])
    low = low_ref[...]
    lin = _dot(low, w_hi) + _dot(low, w_lo) + bg_ref[...]
    gk_ref[...] = (jnp.minimum(lin, 0.0) - jnp.log1p(jnp.exp(-jnp.abs(lin)))) * (1.0 / GLA_NORMALIZER)
    st_ref[...] = jnp.zeros_like(st_ref)

    ri = lax.broadcasted_iota(I32, (GLA_CHUNK, GLA_CHUNK), 0)
    ci = lax.broadcasted_iota(I32, (GLA_CHUNK, GLA_CHUNK), 1)
    causal = ri >= ci
    tri = jnp.where(causal, 1.0, 0.0).astype(BF16)
    gain = ng_ref[...]

    def chunk(c, carry):
        rows = pl.ds(pl.multiple_of(c * GLA_CHUNK, GLA_CHUNK), GLA_CHUNK)
        g_hi, g_lo = _split_bf16(gk_ref[rows, :])
        b = _dot(tri, g_hi) + _dot(tri, g_lo)
        b_end = b[GLA_CHUNK - 1:GLA_CHUNK, :]
        q = q_ref[rows, :].astype(F32) * (GLA_DK ** -0.5)
        k = k_ref[rows, :].astype(F32)
        v = v_ref[rows, :]
        q_e = (q * jnp.exp(b)).astype(BF16)
        k_e = (k * jnp.exp(-b)).astype(BF16)
        k_d = (k * jnp.exp(b_end - b)).astype(BF16)
        att = jnp.where(causal, _dot_nt(q_e, k_e), 0.0)
        st = st_ref[...]
        o = _dot(att.astype(BF16), v) + _dot_nt(q_e, st.astype(BF16))
        st_ref[...] = st * jnp.exp(b_end) + _dot_tn(v, k_d)
        o = o * lax.rsqrt(jnp.mean(o * o, axis=-1, keepdims=True) + LN_EPS) * gain
        r = r_ref[rows, :].astype(F32)
        o_ref[rows, :] = (o * (r * _sigmoid(r))).astype(BF16)
        return carry

    lax.fori_loop(0, SEQ // GLA_CHUNK, chunk, 0)


def _gla(proj, w_gk2_pad, b_gk, norm_g):
    kb, vb = GLA_DK, GLA_DV
    return pl.pallas_call(
        _gla_body,
        grid=(BATCH, GLA_HEADS),
        in_specs=[
            pl.BlockSpec((SEQ, kb), lambda b, h: (b, COL_GQ // kb + h)),
            pl.BlockSpec((SEQ, kb), lambda b, h: (b, COL_GK // kb + h)),
            pl.BlockSpec((SEQ, vb), lambda b, h: (b, COL_GV // vb + h)),
            pl.BlockSpec((SEQ, vb), lambda b, h: (b, COL_GR // vb + h)),
            pl.BlockSpec((SEQ, LANES), lambda b, h: (b, COL_LOW // LANES)),
            pl.BlockSpec((LANES, kb), lambda b, h: (0, h)),
            pl.BlockSpec((1, kb), lambda b, h: (0, h)),
            pl.BlockSpec((1, vb), lambda b, h: (0, 0)),
        ],
        out_specs=pl.BlockSpec((SEQ, vb), lambda b, h: (b, h)),
        out_shape=jax.ShapeDtypeStruct((N_TOK, GLA_VDIM), BF16),
        scratch_shapes=[pltpu.VMEM((vb, kb), F32), pltpu.VMEM((SEQ, kb), F32)],
        compiler_params=_cparams("parallel", "parallel"),
        name="gla",
    )(proj, proj, proj, proj, proj, w_gk2_pad, b_gk, norm_g)


def _moba_body(q_ref, k_ref, v_ref, c_ref, s_ref, o_ref, qs_ref, ks_ref, vt_ref, bias_ref):
    cos_t = c_ref[0]
    sin_t = s_ref[0]
    lane = lax.broadcasted_iota(I32, (SEQ, MOBA_DH), 1)
    half = ROT_DIM // 2

    def rope(x):
        partner = jnp.where(lane < half, pltpu.roll(x, MOBA_DH - half, 1), pltpu.roll(x, half, 1))
        return x * cos_t + partner * sin_t

    q = rope(q_ref[...].astype(F32))
    k = rope(k_ref[...].astype(F32))
    q_hi, q_lo = _split_bf16(q)
    qs_ref[...] = q_hi
    ks_ref[...] = k.astype(BF16)
    vt_ref[...] = v_ref[...].astype(F32).T.astype(BF16)

    k_mean = jnp.concatenate(
        [jnp.mean(k[j * MOBA_BLOCK:(j + 1) * MOBA_BLOCK], axis=0, keepdims=True) for j in range(N_KBLK)], axis=0)
    m_hi, m_lo = _split_bf16(k_mean)
    s_blk = _dot_nt(m_hi, q_hi) + _dot_nt(m_hi, q_lo) + _dot_nt(m_lo, q_hi)
    blk = lax.broadcasted_iota(I32, (N_KBLK, SEQ), 0)
    q_blk = lax.shift_right_logical(lax.broadcasted_iota(I32, (N_KBLK, SEQ), 1), MOBA_BLOCK.bit_length() - 1)
    past = blk < q_blk
    s_blk = jnp.where(past, s_blk, _NEG_INF)
    beaten = jnp.zeros((N_KBLK, SEQ), I32)
    for j in range(N_KBLK):
        row = s_blk[j:j + 1, :]
        beaten += jnp.where((row > s_blk) | ((row == s_blk) & (j < blk)), 1, 0)
    bias_ref[...] = jnp.where(past & (beaten < MOBA_TOPK), 0.0, _NEG_INF)

    kr = lax.broadcasted_iota(I32, (MOBA_BLOCK, MOBA_BLOCK), 0)
    qc = lax.broadcasted_iota(I32, (MOBA_BLOCK, MOBA_BLOCK), 1)
    own_bias = jnp.where(kr <= qc, 0.0, _NEG_INF)
    scale = MOBA_DH ** -0.5

    for i in range(N_KBLK):
        cols = slice(i * MOBA_BLOCK, (i + 1) * MOBA_BLOCK)
        n_keys = (i + 1) * MOBA_BLOCK
        s = _dot_nt(ks_ref[0:n_keys, :], qs_ref[cols, :]) * scale
        parts = []
        for j in range(i + 1):
            bias = own_bias if j == i else bias_ref[j:j + 1, cols]
            parts.append(s[j * MOBA_BLOCK:(j + 1) * MOBA_BLOCK] + bias)
        m = functools.reduce(jnp.maximum, [jnp.max(p, axis=0, keepdims=True) for p in parts])
        probs = [jnp.exp(p - m) for p in parts]
        denom = functools.reduce(jnp.add, [jnp.sum(p, axis=0, keepdims=True) for p in probs])
        p_all = jnp.concatenate(probs, axis=0).astype(BF16)
        o_t = _dot(vt_ref[:, 0:n_keys], p_all) * (1.0 / denom)
        o_ref[cols, :] = o_t.T.astype(BF16)


def _moba(proj, cos_t, sin_t):
    dh = MOBA_DH
    return pl.pallas_call(
        _moba_body,
        grid=(BATCH, MOBA_HEADS),
        in_specs=[
            pl.BlockSpec((SEQ, dh), lambda b, h: (b, COL_MQ // dh + h)),
            pl.BlockSpec((SEQ, dh), lambda b, h: (b, COL_MK // dh + h)),
            pl.BlockSpec((SEQ, dh), lambda b, h: (b, COL_MV // dh + h)),
            pl.BlockSpec((1, SEQ, dh), lambda b, h: (b, 0, 0)),
            pl.BlockSpec((1, SEQ, dh), lambda b, h: (b, 0, 0)),
        ],
        out_specs=pl.BlockSpec((SEQ, dh), lambda b, h: (b, h)),
        out_shape=jax.ShapeDtypeStruct((N_TOK, MOBA_DIM), BF16),
        scratch_shapes=[
            pltpu.VMEM((SEQ, dh), BF16),
            pltpu.VMEM((SEQ, dh), BF16),
            pltpu.VMEM((dh, SEQ), BF16),
            pltpu.VMEM((N_KBLK, SEQ), F32),
        ],
        compiler_params=_cparams("parallel", "parallel"),
        name="moba",
    )(proj, proj, proj, cos_t, sin_t)


def _layer_norm(z, g, b):
    mu = jnp.mean(z, axis=-1, keepdims=True)
    zc = z - mu
    var = jnp.mean(zc * zc, axis=-1, keepdims=True)
    return zc * lax.rsqrt(var + LN_EPS) * g + b


def _mix_body(gla_ref, moba_ref, ga_ref, gb_ref, x_ref, wgo_ref, wmo_ref, wo_ref, g_ref, b_ref, h_ref, hb_ref):
    y_gla = _dot(gla_ref[...], wgo_ref[...])
    y_moba = _dot(moba_ref[...], wmo_ref[...])
    merged = _sigmoid(ga_ref[...].astype(F32)) * y_gla + _sigmoid(gb_ref[...].astype(F32)) * y_moba
    mix = _dot(merged.astype(BF16), wo_ref[...])
    h = _layer_norm(ALPHA * x_ref[...] + mix, g_ref[...], b_ref[...])
    h_ref[...] = h
    hb_ref[...] = h.astype(BF16)


def _mix(gla_out, moba_out, proj, x2d, w_gla_o, w_moba_o, w_out, ln_g, ln_b):
    d = D_MODEL
    row = lambda i: (i, 0)
    full = lambda i: (0, 0)
    return pl.pallas_call(
        _mix_body,
        grid=(N_TOK // MIX_TM,),
        in_specs=[
            pl.BlockSpec((MIX_TM, d), row),
            pl.BlockSpec((MIX_TM, d), row),
            pl.BlockSpec((MIX_TM, d), lambda i: (i, COL_GA // d)),
            pl.BlockSpec((MIX_TM, d), lambda i: (i, COL_GB // d)),
            pl.BlockSpec((MIX_TM, d), row),
            pl.BlockSpec((d, d), full),
            pl.BlockSpec((d, d), full),
            pl.BlockSpec((d, d), full),
            pl.BlockSpec((1, d), full),
            pl.BlockSpec((1, d), full),
        ],
        out_specs=[
            pl.BlockSpec((MIX_TM, d), row),
            pl.BlockSpec((MIX_TM, d), row),
        ],
        out_shape=[
            jax.ShapeDtypeStruct((N_TOK, d), F32),
            jax.ShapeDtypeStruct((N_TOK, d), BF16),
        ],
        compiler_params=_cparams("parallel"),
        name="mix_ln1",
    )(gla_out, moba_out, proj, proj, x2d, w_gla_o, w_moba_o, w_out, ln_g, ln_b)


def _route_body(h_ref, wr_ref, br_ref, e_ref, w_ref, rk_ref, cnt_ref, carry_ref):
    tm = ROUTE_TM

    @pl.when(pl.program_id(0) == 0)
    def _():
        carry_ref[...] = jnp.zeros_like(carry_ref)

    scores = _sigmoid(_dot_nt(wr_ref[...], h_ref[...]))
    biased = scores + br_ref[...]
    row = lax.broadcasted_iota(I32, (N_EXPERTS, tm), 0).astype(F32)
    row_g = lax.broadcasted_iota(I32, (GROUP_SIZE, tm), 0).astype(F32)

    g_scores = []
    for g in range(N_GROUPS):
        grp = biased[g * GROUP_SIZE:(g + 1) * GROUP_SIZE]
        m1 = jnp.max(grp, axis=0, keepdims=True)
        first = jnp.min(jnp.where(grp == m1, row_g, float(GROUP_SIZE)), axis=0, keepdims=True)
        m2 = jnp.max(jnp.where(row_g == first, _NEG_INF, grp), axis=0, keepdims=True)
        g_scores.append(m1 + m2)
    g_score = jnp.concatenate(g_scores, axis=0)
    g_row = lax.broadcasted_iota(I32, (N_GROUPS, tm), 0)
    g_beaten = jnp.zeros((N_GROUPS, tm), I32)
    for g in range(N_GROUPS):
        r = g_score[g:g + 1, :]
        g_beaten += jnp.where((r > g_score) | ((r == g_score) & (g < g_row)), 1, 0)
    g_keep = g_beaten < TOPK_GROUPS
    masked = jnp.concatenate(
        [jnp.where(g_keep[g:g + 1, :], biased[g * GROUP_SIZE:(g + 1) * GROUP_SIZE], _NEG_INF)
         for g in range(N_GROUPS)], axis=0)

    onehot = jnp.zeros((N_EXPERTS, tm), F32)
    picks, pick_scores = [], []
    for _ in range(TOPK_EXPERTS):
        m = jnp.max(masked, axis=0, keepdims=True)
        idx = jnp.min(jnp.where(masked == m, row, float(N_EXPERTS)), axis=0, keepdims=True)
        hit = row == idx
        picks.append(idx)
        pick_scores.append(jnp.sum(jnp.where(hit, scores, 0.0), axis=0, keepdims=True))
        onehot = onehot + jnp.where(hit, 1.0, 0.0)
        masked = jnp.where(hit, _NEG_INF, masked)
    sel = jnp.concatenate(pick_scores, axis=0)
    e_ref[...] = jnp.concatenate(picks, axis=0).astype(I32)
    w_ref[...] = sel / jnp.sum(sel, axis=0, keepdims=True) * ROUTED_SCALE

    t_r = lax.broadcasted_iota(I32, (tm, tm), 0)
    t_c = lax.broadcasted_iota(I32, (tm, tm), 1)
    earlier = jnp.where(t_r < t_c, 1.0, 0.0).astype(BF16)
    seen = _dot(onehot.astype(BF16), earlier) + carry_ref[...]
    rk_ref[...] = jnp.concatenate(
        [jnp.sum(jnp.where(row == idx, seen, 0.0), axis=0, keepdims=True) for idx in picks], axis=0).astype(I32)
    carry_ref[...] += jnp.sum(onehot, axis=1, keepdims=True)
    cnt_ref[...] = carry_ref[...]


def _route(h_bf, w_router_t, b_router_col):
    k = TOPK_EXPERTS
    tok = lambda i: (0, i)
    return pl.pallas_call(
        _route_body,
        grid=(N_TOK // ROUTE_TM,),
        in_specs=[
            pl.BlockSpec((ROUTE_TM, D_MODEL), lambda i: (i, 0)),
            pl.BlockSpec((N_EXPERTS, D_MODEL), lambda i: (0, 0)),
            pl.BlockSpec((N_EXPERTS, 1), lambda i: (0, 0)),
        ],
        out_specs=[
            pl.BlockSpec((k, ROUTE_TM), tok),
            pl.BlockSpec((k, ROUTE_TM), tok),
            pl.BlockSpec((k, ROUTE_TM), tok),
            pl.BlockSpec((N_EXPERTS, 1), lambda i: (0, 0)),
        ],
        out_shape=[
            jax.ShapeDtypeStruct((k, N_TOK), I32),
            jax.ShapeDtypeStruct((k, N_TOK), F32),
            jax.ShapeDtypeStruct((k, N_TOK), I32),
            jax.ShapeDtypeStruct((N_EXPERTS, 1), F32),
        ],
        scratch_shapes=[pltpu.VMEM((N_EXPERTS, 1), F32)],
        compiler_params=_cparams("arbitrary"),
        name="route",
    )(h_bf, w_router_t, b_router_col)


def _dest_body(cnt_ref, e_ref, rk_ref, d_ref, te_ref, nu_ref, lt_ref):
    cnt = cnt_ref[...]
    tiles = jnp.floor((cnt + (ROW_TILE - 1)) * (1.0 / ROW_TILE))
    er = lax.broadcasted_iota(I32, (N_EXPERTS, N_EXPERTS), 0)
    ec = lax.broadcasted_iota(I32, (N_EXPERTS, N_EXPERTS), 1)
    before = jnp.where(ec < er, 1.0, 0.0).astype(BF16)
    tiles_b = jnp.broadcast_to(tiles, (N_EXPERTS, LANES)).astype(BF16)
    t_start = _dot(before, tiles_b)[:, 0:1]
    t_end = t_start + tiles
    p_start = t_start * float(ROW_TILE)
    lt_ref[...] = jnp.where(tiles > 0.0, (t_end - 1.0) * float(ROW_TILE), -1.0).astype(I32)

    row = lax.broadcasted_iota(I32, (N_EXPERTS, DEST_TM), 0)
    d_ref[...] = jnp.concatenate(
        [jnp.sum(jnp.where(row == e_ref[k:k + 1, :], p_start, 0.0), axis=0, keepdims=True)
         for k in range(TOPK_EXPERTS)], axis=0).astype(I32) + rk_ref[...]

    tile_id = lax.broadcasted_iota(I32, (N_EXPERTS, TILE_TABLE), 1).astype(F32)
    owner = jnp.sum(jnp.where(t_end <= tile_id, 1, 0), axis=0, keepdims=True)
    te_ref[...] = jnp.minimum(owner, N_EXPERTS - 1)
    nu_ref[...] = jnp.broadcast_to(t_end[N_EXPERTS - 1:N_EXPERTS, :], (1, LANES)).astype(I32)


def _dest(cnt, e_t, rk_t):
    k = TOPK_EXPERTS
    tok = lambda i: (0, i)
    return pl.pallas_call(
        _dest_body,
        grid=(N_TOK // DEST_TM,),
        in_specs=[
            pl.BlockSpec((N_EXPERTS, 1), lambda i: (0, 0)),
            pl.BlockSpec((k, DEST_TM), tok),
            pl.BlockSpec((k, DEST_TM), tok),
        ],
        out_specs=[
            pl.BlockSpec((k, DEST_TM), tok),
            pl.BlockSpec((1, TILE_TABLE), lambda i: (0, 0)),
            pl.BlockSpec((1, LANES), lambda i: (0, 0)),
            pl.BlockSpec((N_EXPERTS, 1), lambda i: (0, 0)),
        ],
        out_shape=[
            jax.ShapeDtypeStruct((k, N_TOK), I32),
            jax.ShapeDtypeStruct((1, TILE_TABLE), I32),
            jax.ShapeDtypeStruct((1, LANES), I32),
            jax.ShapeDtypeStruct((N_EXPERTS, 1), I32),
        ],
        compiler_params=_cparams("arbitrary"),
        name="dest",
    )(cnt, e_t, rk_t)


def _row_copy(src, dst, sem):
    return pltpu.make_async_copy(src, dst, sem)


def _dispatch_body(lt_ref, d_ref, h_ref, xs_ref, zero_ref, sem):
    @pl.when(pl.program_id(0) == 0)
    def _():
        zero_ref[...] = jnp.zeros_like(zero_ref)

        def tile_copy(e):
            r0 = pl.multiple_of(jnp.maximum(lt_ref[e], 0), ROW_TILE)
            return pltpu.make_async_copy(zero_ref, xs_ref.at[pl.ds(r0, ROW_TILE), :], sem)

        def z_start(e, carry):
            @pl.when(lt_ref[e] >= 0)
            def _():
                tile_copy(e).start()
            return carry

        def z_wait(e, carry):
            @pl.when(lt_ref[e] >= 0)
            def _():
                tile_copy(e).wait()
            return carry

        lax.fori_loop(0, N_EXPERTS, z_start, 0)
        lax.fori_loop(0, N_EXPERTS, z_wait, 0)

    def row(t, k):
        return _row_copy(h_ref.at[pl.ds(t, 1), :], xs_ref.at[pl.ds(d_ref[k, t], 1), :], sem)

    def issue(t, carry):
        for k in range(TOPK_EXPERTS):
            row(t, k).start(priority=k % 2)
        return carry

    def drain(t, carry):
        for k in range(TOPK_EXPERTS):
            row(t, k).wait()
        return carry

    lax.fori_loop(0, DISP_TM, issue, 0)
    lax.fori_loop(0, DISP_TM, drain, 0)


def _dispatch(last_tile_row, dest_t, h1):
    grid_spec = pltpu.PrefetchScalarGridSpec(
        num_scalar_prefetch=1,
        grid=(N_TOK // DISP_TM,),
        in_specs=[
            pl.BlockSpec((TOPK_EXPERTS, DISP_TM), lambda i, lt: (0, i), memory_space=pltpu.SMEM),
            pl.BlockSpec((DISP_TM, D_MODEL), lambda i, lt: (i, 0)),
        ],
        out_specs=pl.BlockSpec(memory_space=pl.ANY),
        scratch_shapes=[pltpu.VMEM((ROW_TILE, D_MODEL), F32), pltpu.SemaphoreType.DMA],
    )
    return pl.pallas_call(
        _dispatch_body,
        grid_spec=grid_spec,
        out_shape=jax.ShapeDtypeStruct((N_ROWS, D_MODEL), F32),
        compiler_params=_cparams("arbitrary"),
        name="dispatch",
    )(last_tile_row, dest_t, h1)


def _experts_body(te_ref, nu_ref, xs_ref, wg_ref, wu_ref, wd_ref, ys_ref, wg_b, wu_b, wd_b):
    i = pl.program_id(0)

    @pl.when(i < nu_ref[0])
    def _():
        prev = te_ref[jnp.maximum(i - 1, 0)]

        @pl.when((i == 0) | (te_ref[i] != prev))
        def _():
            wg_b[...] = wg_ref[0].astype(BF16)
            wu_b[...] = wu_ref[0].astype(BF16)
            wd_b[...] = wd_ref[0].astype(BF16)

        x = xs_ref[...].astype(BF16)
        g = _dot(x, wg_b[...])
        u = _dot(x, wu_b[...])
        h = (g * _sigmoid(g)) * u
        ys_ref[...] = _dot(h.astype(BF16), wd_b[...])


def _experts(tile_expert, n_used, xs, w_gate, w_up, w_down):
    def tile(i, te, nu):
        return (jnp.minimum(i, nu[0] - 1), 0)

    def expert(i, te, nu):
        return (te[jnp.minimum(i, nu[0] - 1)], 0, 0)

    grid_spec = pltpu.PrefetchScalarGridSpec(
        num_scalar_prefetch=2,
        grid=(N_ROW_TILES,),
        in_specs=[
            pl.BlockSpec((ROW_TILE, D_MODEL), tile),
            pl.BlockSpec((1, D_MODEL, D_EXPERT), expert),
            pl.BlockSpec((1, D_MODEL, D_EXPERT), expert),
            pl.BlockSpec((1, D_EXPERT, D_MODEL), expert),
        ],
        out_specs=pl.BlockSpec((ROW_TILE, D_MODEL), tile),
        scratch_shapes=[
            pltpu.VMEM((D_MODEL, D_EXPERT), BF16),
            pltpu.VMEM((D_MODEL, D_EXPERT), BF16),
            pltpu.VMEM((D_EXPERT, D_MODEL), BF16),
        ],
    )
    return pl.pallas_call(
        _experts_body,
        grid_spec=grid_spec,
        out_shape=jax.ShapeDtypeStruct((N_ROWS, D_MODEL), F32),
        compiler_params=_cparams("arbitrary"),
        name="experts",
    )(tile_expert, n_used, xs, w_gate, w_up, w_down)


def _final_body(d_ref, dn_ref, w_ref, ys_ref, h_ref, hb_ref, p_ref, wsg_ref, wsu_ref, wsd_ref, wpl_ref, wpg_ref,
                g_ref, b_ref, o_ref, buf_ref, sem):
    i = pl.program_id(0)
    slot = lax.rem(i, 2)

    def row(idx_ref, s, t, k):
        return _row_copy(ys_ref.at[pl.ds(idx_ref[k, t], 1), :], buf_ref.at[s, k, pl.ds(t, 1), :], sem.at[s])

    def issue(idx_ref, s):
        def body(t, carry):
            for k in range(TOPK_EXPERTS):
                row(idx_ref, s, t, k).start(priority=k % 2)
            return carry
        lax.fori_loop(0, FINAL_TM, body, 0)

    @pl.when(i == 0)
    def _():
        issue(d_ref, 0)

    @pl.when(i + 1 < pl.num_programs(0))
    def _():
        issue(dn_ref, 1 - slot)

    hb = hb_ref[...]
    sg = _dot(hb, wsg_ref[...])
    shared = _dot(((sg * _sigmoid(sg)) * _dot(hb, wsu_ref[...])).astype(BF16), wsd_ref[...])
    ple = _sigmoid(_dot(hb, wpg_ref[...])) * _dot(p_ref[...].astype(BF16), wpl_ref[...])

    def drain(t, carry):
        for k in range(TOPK_EXPERTS):
            row(d_ref, slot, t, k).wait()
        return carry

    lax.fori_loop(0, FINAL_TM, drain, 0)

    w_col = w_ref[...].T
    routed = buf_ref[slot, 0] * w_col[:, 0:1]
    for k in range(1, TOPK_EXPERTS):
        routed = routed + buf_ref[slot, k] * w_col[:, k:k + 1]
    o_ref[...] = _layer_norm(ALPHA * h_ref[...] + (routed + shared) + ple, g_ref[...], b_ref[...])


def _final(dest_t, w_t, ys, h1, h_bf, p2d, w_s_gate, w_s_up, w_s_down, w_ple, w_ple_gate, ln_g, ln_b):
    d = D_MODEL
    n_steps = N_TOK // FINAL_TM
    full = lambda i: (0, 0)
    tok = lambda i: (0, i)
    tok_next = lambda i: (0, jnp.minimum(i + 1, n_steps - 1))
    return pl.pallas_call(
        _final_body,
        grid=(n_steps,),
        in_specs=[
            pl.BlockSpec((TOPK_EXPERTS, FINAL_TM), tok, memory_space=pltpu.SMEM),
            pl.BlockSpec((TOPK_EXPERTS, FINAL_TM), tok_next, memory_space=pltpu.SMEM),
            pl.BlockSpec((TOPK_EXPERTS, FINAL_TM), tok),
            pl.BlockSpec(memory_space=pl.ANY),
            pl.BlockSpec((FINAL_TM, d), lambda i: (i, 0)),
            pl.BlockSpec((FINAL_TM, d), lambda i: (i, 0)),
            pl.BlockSpec((FINAL_TM, PLE_DIM), lambda i: (i, 0)),
            pl.BlockSpec((d, D_SHARED), full),
            pl.BlockSpec((d, D_SHARED), full),
            pl.BlockSpec((D_SHARED, d), full),
            pl.BlockSpec((PLE_DIM, d), full),
            pl.BlockSpec((d, d), full),
            pl.BlockSpec((1, d), full),
            pl.BlockSpec((1, d), full),
        ],
        out_specs=pl.BlockSpec((FINAL_TM, d), lambda i: (i, 0)),
        out_shape=jax.ShapeDtypeStruct((N_TOK, d), F32),
        scratch_shapes=[
            pltpu.VMEM((2, TOPK_EXPERTS, FINAL_TM, d), F32),
            pltpu.SemaphoreType.DMA((2,)),
        ],
        compiler_params=_cparams("arbitrary"),
        name="combine_ln2",
    )(dest_t, dest_t, w_t, ys, h1, h_bf, p2d, w_s_gate, w_s_up, w_s_down, w_ple, w_ple_gate, ln_g, ln_b)


def _rope_tables(positions):
    half = ROT_DIM // 2
    inv = ROPE_THETA ** (-jnp.arange(0, ROT_DIM, 2, dtype=F32) / ROT_DIM)
    ang = positions.astype(F32)[..., None] * inv
    cos, sin = jnp.cos(ang), jnp.sin(ang)
    rest = MOBA_DH - ROT_DIM
    ones = jnp.ones(cos.shape[:-1] + (rest,), F32)
    zeros = jnp.zeros(cos.shape[:-1] + (rest,), F32)
    return (jnp.concatenate([cos, cos, ones], axis=-1), jnp.concatenate([-sin, sin, zeros], axis=-1))


def _layer(h2d, p2d, cos_t, sin_t, w_in, w_gk2, b_gk, norm_g, w_gla_o, w_moba_o, w_out, ln1_g, ln1_b,
           w_router, b_router, w_e_gate, w_e_up, w_e_down, w_s_gate, w_s_up, w_s_down, w_ple, w_ple_gate,
           ln2_g, ln2_b):
    low_end = LOW_SRC + GLA_RANK
    w_cat = jnp.concatenate(
        [w_in[:, :LOW_SRC], w_in[:, low_end:], w_in[:, LOW_SRC:low_end],
         jnp.zeros((D_MODEL, LANES - GLA_RANK), w_in.dtype)], axis=1).astype(BF16)
    w_gk2_pad = jnp.concatenate([w_gk2, jnp.zeros((LANES - GLA_RANK, GLA_KDIM), w_gk2.dtype)], axis=0)

    proj = _in_proj(h2d, w_cat)
    gla_out = _gla(proj, w_gk2_pad, b_gk[None, :], norm_g[None, :])
    moba_out = _moba(proj, cos_t, sin_t)
    h1, h_bf = _mix(gla_out, moba_out, proj, h2d, w_gla_o.astype(BF16), w_moba_o.astype(BF16),
                    w_out.astype(BF16), ln1_g[None, :], ln1_b[None, :])
    e_t, w_t, rk_t, cnt = _route(h_bf, w_router.T.astype(BF16), b_router[:, None])
    dest_t, tile_expert, n_used, last_tile_row = _dest(cnt, e_t, rk_t)
    xs = _dispatch(last_tile_row.reshape(N_EXPERTS), dest_t, h1)
    ys = _experts(tile_expert.reshape(TILE_TABLE), n_used[0, 0:1], xs, w_e_gate, w_e_up, w_e_down)
    return _final(dest_t, w_t, ys, h1, h_bf, p2d, w_s_gate.astype(BF16), w_s_up.astype(BF16),
                  w_s_down.astype(BF16), w_ple.astype(BF16), w_ple_gate.astype(BF16),
                  ln2_g[None, :], ln2_b[None, :])


def kernel(x, p, positions, w_in, w_gla_gk2, b_gla_gk, gla_norm_g, w_gla_o, w_moba_o, w_out, ln1_g, ln1_b,
           w_router, b_router, w_e_gate, w_e_up, w_e_down, w_s_gate, w_s_up, w_s_down, w_ple, w_ple_gate,
           ln2_g, ln2_b):
    cos_t, sin_t = _rope_tables(positions)
    h = x.reshape(N_TOK, D_MODEL)
    for i in range(DEPTH):
        h = _layer(h, p[i].reshape(N_TOK, PLE_DIM), cos_t, sin_t, w_in[i], w_gla_gk2[i], b_gla_gk[i],
                   gla_norm_g[i], w_gla_o[i], w_moba_o[i], w_out[i], ln1_g[i], ln1_b[i], w_router[i],
                   b_router[i], w_e_gate[i], w_e_up[i], w_e_down[i], w_s_gate[i], w_s_up[i], w_s_down[i],
                   w_ple[i], w_ple_gate[i], ln2_g[i], ln2_b[i])
    return h.reshape(BATCH, SEQ, D_MODEL)
```

```python
import functools

import jax
import jax.numpy as jnp
from jax import lax
from jax.experimental import pallas as pl
from jax.experimental.pallas import tpu as pltpu

F32 = jnp.float32
BF16 = jnp.bfloat16
I32 = jnp.int32

LANES = 128
SUBLANES = 8
VMEM_LIMIT_BYTES = 48 * 1024 * 1024

D_MODEL = 1024
BATCH = 8
SEQ = 2048
N_TOK = BATCH * SEQ
GLA_HEADS = 4
GLA_DK = 128
GLA_DV = 256
GLA_RANK = 16
GLA_NORMALIZER = 16.0
GLA_CHUNK = 64
GLA_GROUP = 4 * GLA_CHUNK
MOBA_HEADS = 8
MOBA_DH = 128
MOBA_BLOCK = 256
MOBA_TOPK = 3
ROT_DIM = 32
ROPE_THETA = 500000.0
N_EXPERTS = 256
TOPK_EXPERTS = 8
N_GROUPS = 8
GROUP_SIZE = N_EXPERTS // N_GROUPS
TOPK_GROUPS = 4
D_EXPERT = 256
D_SHARED = 256
ROUTED_SCALE = 2.5
PLE_DIM = 256
LN_EPS = 1e-5
DEPTH = 1
ALPHA = (2.0 * DEPTH) ** 0.25
GLA_KDIM = GLA_HEADS * GLA_DK
GLA_VDIM = GLA_HEADS * GLA_DV
MOBA_DIM = MOBA_HEADS * MOBA_DH
N_KBLK = SEQ // MOBA_BLOCK

COL_GQ = 0
COL_GK = COL_GQ + GLA_KDIM
COL_GV = COL_GK + GLA_KDIM
COL_GR = COL_GV + GLA_VDIM
COL_MQ = COL_GR + GLA_VDIM
COL_MK = COL_MQ + MOBA_DIM
COL_MV = COL_MK + MOBA_DIM
COL_GA = COL_MV + MOBA_DIM
COL_GB = COL_GA + D_MODEL
COL_LOW = COL_GB + D_MODEL
PROJ_COLS = COL_LOW + LANES
LOW_SRC = 2 * GLA_KDIM + 2 * GLA_VDIM

ROW_TILE = 256
N_ASSIGN = N_TOK * TOPK_EXPERTS
N_ROW_TILES = (N_ASSIGN + N_EXPERTS * (ROW_TILE - 1) + ROW_TILE - 1) // ROW_TILE
N_ROWS = N_ROW_TILES * ROW_TILE
TILE_TABLE = -(-N_ROW_TILES // LANES) * LANES

PROJ_TM = 512
PROJ_TN = PROJ_COLS // 5
MIX_TM = 512
ROUTE_TM = 512
DEST_TM = 2048
DISP_TM = 256
FINAL_TM = 256

_NEG_INF = float("-inf")
LOG2_E = 1.4426950408889634


def _cparams(*sem):
    return pltpu.CompilerParams(dimension_semantics=sem, vmem_limit_bytes=VMEM_LIMIT_BYTES)


def _dot(a, b):
    return jnp.dot(a, b, preferred_element_type=F32)


def _dot_nt(a, b):
    return lax.dot_general(a, b, (((1,), (1,)), ((), ())), preferred_element_type=F32)


def _dot_tn(a, b):
    return lax.dot_general(a, b, (((0,), (0,)), ((), ())), preferred_element_type=F32)


def _split_bf16(x):
    hi = x.astype(BF16)
    lo = (x - hi.astype(F32)).astype(BF16)
    return hi, lo


def _sigmoid(x):
    return 1.0 / (1.0 + jnp.exp(-x))


def _proj_body(x_ref, w_ref, o_ref):
    o_ref[...] = _dot(x_ref[...].astype(BF16), w_ref[...]).astype(BF16)


def _in_proj(x2d, w_cat):
    return pl.pallas_call(
        _proj_body,
        grid=(PROJ_COLS // PROJ_TN, N_TOK // PROJ_TM),
        in_specs=[
            pl.BlockSpec((PROJ_TM, D_MODEL), lambda j, i: (i, 0)),
            pl.BlockSpec((D_MODEL, PROJ_TN), lambda j, i: (0, j)),
        ],
        out_specs=pl.BlockSpec((PROJ_TM, PROJ_TN), lambda j, i: (i, j)),
        out_shape=jax.ShapeDtypeStruct((N_TOK, PROJ_COLS), BF16),
        compiler_params=_cparams("parallel", "parallel"),
        name="in_proj",
    )(x2d, w_cat)


def _gla_body(q_ref, k_ref, v_ref, r_ref, low_ref, wg_ref, bg_ref, ng_ref, o_ref, st_ref, gk_ref):
    w_hi, w_lo = _split_bf16(wg_ref[...])
    low = low_ref[...]
    lin = _dot(low, w_hi) + _dot(low, w_lo) + bg_ref[...]
    gk_ref[...] = (jnp.minimum(lin, 0.0) - jnp.log1p(jnp.exp(-jnp.abs(lin)))) * (1.0 / GLA_NORMALIZER)
    st_ref[...] = jnp.zeros_like(st_ref)

    ri = lax.broadcasted_iota(I32, (GLA_GROUP, GLA_GROUP), 0)
    ci = lax.broadcasted_iota(I32, (GLA_GROUP, GLA_GROUP), 1)
    same_chunk = lax.shift_right_logical(ri, GLA_CHUNK.bit_length() - 1) == lax.shift_right_logical(
        ci, GLA_CHUNK.bit_length() - 1)
    causal = same_chunk & (ri >= ci)
    sums = jnp.concatenate([jnp.where(causal, 1.0, 0.0), jnp.where(same_chunk, 1.0, 0.0)], axis=0).astype(BF16)
    gain = ng_ref[...]

    def group(c, carry):
        rows = pl.ds(pl.multiple_of(c * GLA_GROUP, GLA_GROUP), GLA_GROUP)
        g_hi, g_lo = _split_bf16(gk_ref[rows, :])
        bb = _dot(sums, g_hi) + _dot(sums, g_lo)
        b = bb[0:GLA_GROUP]
        b_end = bb[GLA_GROUP:2 * GLA_GROUP]
        q = q_ref[rows, :].astype(F32) * (GLA_DK ** -0.5)
        k = k_ref[rows, :].astype(F32)
        v = v_ref[rows, :]
        q_e = (q * jnp.exp(b)).astype(BF16)
        k_e = (k * jnp.exp(-b)).astype(BF16)
        k_d = (k * jnp.exp(b_end - b)).astype(BF16)
        att = jnp.where(causal, _dot_nt(q_e, k_e), 0.0)
        o = _dot(att.astype(BF16), v)
        st = st_ref[...]
        inter = []
        for j in range(GLA_GROUP // GLA_CHUNK):
            cr = slice(j * GLA_CHUNK, (j + 1) * GLA_CHUNK)
            inter.append(_dot_nt(q_e[cr], st.astype(BF16)))
            st = st * jnp.exp(b_end[j * GLA_CHUNK:j * GLA_CHUNK + 1, :]) + _dot_tn(v[cr], k_d[cr])
        st_ref[...] = st
        o = o + jnp.concatenate(inter, axis=0)
        o = o * lax.rsqrt(jnp.mean(o * o, axis=-1, keepdims=True) + LN_EPS) * gain
        r = r_ref[rows, :].astype(F32)
        o_ref[rows, :] = (o * (r * _sigmoid(r))).astype(BF16)
        return carry

    lax.fori_loop(0, SEQ // GLA_GROUP, group, 0)


def _gla(proj, w_gk2_pad, b_gk, norm_g):
    kb, vb = GLA_DK, GLA_DV
    return pl.pallas_call(
        _gla_body,
        grid=(BATCH, GLA_HEADS),
        in_specs=[
            pl.BlockSpec((SEQ, kb), lambda b, h: (b, COL_GQ // kb + h)),
            pl.BlockSpec((SEQ, kb), lambda b, h: (b, COL_GK // kb + h)),
            pl.BlockSpec((SEQ, vb), lambda b, h: (b, COL_GV // vb + h)),
            pl.BlockSpec((SEQ, vb), lambda b, h: (b, COL_GR // vb + h)),
            pl.BlockSpec((SEQ, LANES), lambda b, h: (b, COL_LOW // LANES)),
            pl.BlockSpec((LANES, kb), lambda b, h: (0, h)),
            pl.BlockSpec((1, kb), lambda b, h: (0, h)),
            pl.BlockSpec((1, vb), lambda b, h: (0, 0)),
        ],
        out_specs=pl.BlockSpec((SEQ, vb), lambda b, h: (b, h)),
        out_shape=jax.ShapeDtypeStruct((N_TOK, GLA_VDIM), BF16),
        scratch_shapes=[pltpu.VMEM((vb, kb), F32), pltpu.VMEM((SEQ, kb), F32)],
        compiler_params=_cparams("parallel", "parallel"),
        name="gla",
    )(proj, proj, proj, proj, proj, w_gk2_pad, b_gk, norm_g)


def _moba_body(q_ref, k_ref, v_ref, c_ref, s_ref, o_ref, qs_ref, ks_ref, vt_ref, bias_ref,
               sc_a_ref, sc_b_ref, pr_a_ref, pr_b_ref):
    cos_t = c_ref[0]
    sin_t = s_ref[0]
    lane = lax.broadcasted_iota(I32, (SEQ, MOBA_DH), 1)
    half = ROT_DIM // 2

    def rope(x):
        partner = jnp.where(lane < half, pltpu.roll(x, MOBA_DH - half, 1), pltpu.roll(x, half, 1))
        return x * cos_t + partner * sin_t

    q = rope(q_ref[...].astype(F32))
    k = rope(k_ref[...].astype(F32))
    q_hi, q_lo = _split_bf16(q)
    qs_ref[...] = (q * (MOBA_DH ** -0.5 * LOG2_E)).astype(BF16)
    ks_ref[...] = k.astype(BF16)
    vt_ref[...] = v_ref[...].astype(F32).T.astype(BF16)

    k_mean = jnp.concatenate(
        [jnp.mean(k[j * MOBA_BLOCK:(j + 1) * MOBA_BLOCK], axis=0, keepdims=True) for j in range(N_KBLK)], axis=0)
    m_hi, m_lo = _split_bf16(k_mean)
    s_blk = _dot_nt(m_hi, q_hi) + _dot_nt(m_hi, q_lo) + _dot_nt(m_lo, q_hi)
    blk = lax.broadcasted_iota(I32, (N_KBLK, SEQ), 0)
    q_blk = lax.shift_right_logical(lax.broadcasted_iota(I32, (N_KBLK, SEQ), 1), MOBA_BLOCK.bit_length() - 1)
    past = blk < q_blk
    s_blk = jnp.where(past, s_blk, _NEG_INF)
    beaten = jnp.zeros((N_KBLK, SEQ), I32)
    for j in range(N_KBLK):
        row = s_blk[j:j + 1, :]
        beaten += jnp.where((row > s_blk) | ((row == s_blk) & (j < blk)), 1, 0)
    bias_ref[...] = jnp.where(past & (beaten < MOBA_TOPK), 0.0, _NEG_INF)

    kr = lax.broadcasted_iota(I32, (MOBA_BLOCK, MOBA_BLOCK), 0)
    qc = lax.broadcasted_iota(I32, (MOBA_BLOCK, MOBA_BLOCK), 1)
    own_bias = jnp.where(kr <= qc, 0.0, _NEG_INF)

    sc_bufs = (sc_a_ref, sc_b_ref)
    pr_bufs = (pr_a_ref, pr_b_ref)
    zeros_row = jnp.zeros((1, MOBA_BLOCK), F32)
    future_row = jnp.full((1, MOBA_BLOCK), _NEG_INF, F32)
    for pair in range(N_KBLK // 2):
        q_blocks = (2 * pair, 2 * pair + 1)
        cols = slice(q_blocks[0] * MOBA_BLOCK, (q_blocks[1] + 1) * MOBA_BLOCK)
        n_kblk = q_blocks[1] + 1
        n_keys = n_kblk * MOBA_BLOCK
        sc, pr = sc_bufs[pair % 2], pr_bufs[pair % 2]
        q_pair = qs_ref[cols, :]

        def query_bias(j):
            halves = []
            for qb in q_blocks:
                q_cols = slice(qb * MOBA_BLOCK, (qb + 1) * MOBA_BLOCK)
                halves.append(bias_ref[j:j + 1, q_cols] if j < qb else zeros_row if j == qb else future_row)
            return jnp.concatenate(halves, axis=1)

        biases = [query_bias(j) for j in range(n_kblk)]
        col_max = []
        for j in range(n_kblk):
            rows = slice(j * MOBA_BLOCK, (j + 1) * MOBA_BLOCK)
            s = _dot_nt(ks_ref[rows, :], q_pair)
            if j in q_blocks:
                h = q_blocks.index(j)
                own = s[:, h * MOBA_BLOCK:(h + 1) * MOBA_BLOCK] + own_bias
                s = jnp.concatenate([own, s[:, MOBA_BLOCK:]] if h == 0 else [s[:, :MOBA_BLOCK], own], axis=1)
            sc[rows, :] = s
            col_max.append(jnp.max(s, axis=0, keepdims=True) + biases[j])
        m = functools.reduce(jnp.maximum, col_max)
        denom = jnp.zeros((1, 2 * MOBA_BLOCK), F32)
        for j in range(n_kblk):
            rows = slice(j * MOBA_BLOCK, (j + 1) * MOBA_BLOCK)
            p = jnp.exp2(sc[rows, :] - (m - biases[j]))
            denom = denom + jnp.sum(p, axis=0, keepdims=True)
            pr[rows, :] = p.astype(BF16)
        o_t = _dot(vt_ref[:, 0:n_keys], pr[0:n_keys, :]) * (1.0 / denom)
        o_ref[cols, :] = o_t.T.astype(BF16)


def _moba(proj, cos_t, sin_t):
    dh = MOBA_DH
    return pl.pallas_call(
        _moba_body,
        grid=(BATCH, MOBA_HEADS),
        in_specs=[
            pl.BlockSpec((SEQ, dh), lambda b, h: (b, COL_MQ // dh + h)),
            pl.BlockSpec((SEQ, dh), lambda b, h: (b, COL_MK // dh + h)),
            pl.BlockSpec((SEQ, dh), lambda b, h: (b, COL_MV // dh + h)),
            pl.BlockSpec((1, SEQ, dh), lambda b, h: (b, 0, 0)),
            pl.BlockSpec((1, SEQ, dh), lambda b, h: (b, 0, 0)),
        ],
        out_specs=pl.BlockSpec((SEQ, dh), lambda b, h: (b, h)),
        out_shape=jax.ShapeDtypeStruct((N_TOK, MOBA_DIM), BF16),
        scratch_shapes=[
            pltpu.VMEM((SEQ, dh), BF16),
            pltpu.VMEM((SEQ, dh), BF16),
            pltpu.VMEM((dh, SEQ), BF16),
            pltpu.VMEM((N_KBLK, SEQ), F32),
            pltpu.VMEM((SEQ, 2 * MOBA_BLOCK), F32),
            pltpu.VMEM((SEQ, 2 * MOBA_BLOCK), F32),
            pltpu.VMEM((SEQ, 2 * MOBA_BLOCK), BF16),
            pltpu.VMEM((SEQ, 2 * MOBA_BLOCK), BF16),
        ],
        compiler_params=_cparams("parallel", "parallel"),
        name="moba",
    )(proj, proj, proj, cos_t, sin_t)


def _layer_norm(z, g, b):
    mu = jnp.mean(z, axis=-1, keepdims=True)
    zc = z - mu
    var = jnp.mean(zc * zc, axis=-1, keepdims=True)
    return zc * lax.rsqrt(var + LN_EPS) * g + b


def _mix_body(gla_ref, moba_ref, ga_ref, gb_ref, x_ref, wgo_ref, wmo_ref, wo_ref, g_ref, b_ref, h_ref, hb_ref):
    y_gla = _dot(gla_ref[...], wgo_ref[...])
    y_moba = _dot(moba_ref[...], wmo_ref[...])
    merged = _sigmoid(ga_ref[...].astype(F32)) * y_gla + _sigmoid(gb_ref[...].astype(F32)) * y_moba
    mix = _dot(merged.astype(BF16), wo_ref[...])
    h = _layer_norm(ALPHA * x_ref[...] + mix, g_ref[...], b_ref[...])
    h_ref[...] = h
    hb_ref[...] = h.astype(BF16)


def _mix(gla_out, moba_out, proj, x2d, w_gla_o, w_moba_o, w_out, ln_g, ln_b):
    d = D_MODEL
    row = lambda i: (i, 0)
    full = lambda i: (0, 0)
    return pl.pallas_call(
        _mix_body,
        grid=(N_TOK // MIX_TM,),
        in_specs=[
            pl.BlockSpec((MIX_TM, d), row),
            pl.BlockSpec((MIX_TM, d), row),
            pl.BlockSpec((MIX_TM, d), lambda i: (i, COL_GA // d)),
            pl.BlockSpec((MIX_TM, d), lambda i: (i, COL_GB // d)),
            pl.BlockSpec((MIX_TM, d), row),
            pl.BlockSpec((d, d), full),
            pl.BlockSpec((d, d), full),
            pl.BlockSpec((d, d), full),
            pl.BlockSpec((1, d), full),
            pl.BlockSpec((1, d), full),
        ],
        out_specs=[
            pl.BlockSpec((MIX_TM, d), row),
            pl.BlockSpec((MIX_TM, d), row),
        ],
        out_shape=[
            jax.ShapeDtypeStruct((N_TOK, d), F32),
            jax.ShapeDtypeStruct((N_TOK, d), BF16),
        ],
        compiler_params=_cparams("parallel"),
        name="mix_ln1",
    )(gla_out, moba_out, proj, proj, x2d, w_gla_o, w_moba_o, w_out, ln_g, ln_b)


def _route_body(h_ref, wr_ref, br_ref, e_ref, w_ref, rk_ref, cnt_ref, carry_ref):
    tm = ROUTE_TM

    @pl.when(pl.program_id(0) == 0)
    def _():
        carry_ref[...] = jnp.zeros_like(carry_ref)

    scores = _sigmoid(_dot_nt(wr_ref[...], h_ref[...]))
    biased = scores + br_ref[...]
    row = lax.broadcasted_iota(I32, (N_EXPERTS, tm), 0).astype(F32)
    row_g = lax.broadcasted_iota(I32, (GROUP_SIZE, tm), 0).astype(F32)

    g_scores = []
    for g in range(N_GROUPS):
        grp = biased[g * GROUP_SIZE:(g + 1) * GROUP_SIZE]
        m1 = jnp.max(grp, axis=0, keepdims=True)
        first = jnp.min(jnp.where(grp == m1, row_g, float(GROUP_SIZE)), axis=0, keepdims=True)
        m2 = jnp.max(jnp.where(row_g == first, _NEG_INF, grp), axis=0, keepdims=True)
        g_scores.append(m1 + m2)
    g_score = jnp.concatenate(g_scores, axis=0)
    g_row = lax.broadcasted_iota(I32, (N_GROUPS, tm), 0)
    g_beaten = jnp.zeros((N_GROUPS, tm), I32)
    for g in range(N_GROUPS):
        r = g_score[g:g + 1, :]
        g_beaten += jnp.where((r > g_score) | ((r == g_score) & (g < g_row)), 1, 0)
    g_keep = g_beaten < TOPK_GROUPS
    masked = jnp.concatenate(
        [jnp.where(g_keep[g:g + 1, :], biased[g * GROUP_SIZE:(g + 1) * GROUP_SIZE], _NEG_INF)
         for g in range(N_GROUPS)], axis=0)

    onehot = jnp.zeros((N_EXPERTS, tm), F32)
    picks, pick_scores = [], []
    for _ in range(TOPK_EXPERTS):
        m = jnp.max(masked, axis=0, keepdims=True)
        idx = jnp.min(jnp.where(masked == m, row, float(N_EXPERTS)), axis=0, keepdims=True)
        hit = row == idx
        picks.append(idx)
        pick_scores.append(jnp.sum(jnp.where(hit, scores, 0.0), axis=0, keepdims=True))
        onehot = onehot + jnp.where(hit, 1.0, 0.0)
        masked = jnp.where(hit, _NEG_INF, masked)
    sel = jnp.concatenate(pick_scores, axis=0)
    e_ref[...] = jnp.concatenate(picks, axis=0).astype(I32)
    w_ref[...] = sel / jnp.sum(sel, axis=0, keepdims=True) * ROUTED_SCALE

    t_r = lax.broadcasted_iota(I32, (tm, tm), 0)
    t_c = lax.broadcasted_iota(I32, (tm, tm), 1)
    earlier = jnp.where(t_r < t_c, 1.0, 0.0).astype(BF16)
    seen = _dot(onehot.astype(BF16), earlier) + carry_ref[...]
    rk_ref[...] = jnp.concatenate(
        [jnp.sum(jnp.where(row == idx, seen, 0.0), axis=0, keepdims=True) for idx in picks], axis=0).astype(I32)
    carry_ref[...] += jnp.sum(onehot, axis=1, keepdims=True)
    cnt_ref[...] = carry_ref[...]


def _route(h_bf, w_router_t, b_router_col):
    k = TOPK_EXPERTS
    tok = lambda i: (0, i)
    return pl.pallas_call(
        _route_body,
        grid=(N_TOK // ROUTE_TM,),
        in_specs=[
            pl.BlockSpec((ROUTE_TM, D_MODEL), lambda i: (i, 0)),
            pl.BlockSpec((N_EXPERTS, D_MODEL), lambda i: (0, 0)),
            pl.BlockSpec((N_EXPERTS, 1), lambda i: (0, 0)),
        ],
        out_specs=[
            pl.BlockSpec((k, ROUTE_TM), tok),
            pl.BlockSpec((k, ROUTE_TM), tok),
            pl.BlockSpec((k, ROUTE_TM), tok),
            pl.BlockSpec((N_EXPERTS, 1), lambda i: (0, 0)),
        ],
        out_shape=[
            jax.ShapeDtypeStruct((k, N_TOK), I32),
            jax.ShapeDtypeStruct((k, N_TOK), F32),
            jax.ShapeDtypeStruct((k, N_TOK), I32),
            jax.ShapeDtypeStruct((N_EXPERTS, 1), F32),
        ],
        scratch_shapes=[pltpu.VMEM((N_EXPERTS, 1), F32)],
        compiler_params=_cparams("arbitrary"),
        name="route",
    )(h_bf, w_router_t, b_router_col)


def _dest_body(cnt_ref, e_ref, rk_ref, d_ref, te_ref, nu_ref, lt_ref):
    cnt = cnt_ref[...]
    tiles = jnp.floor((cnt + (ROW_TILE - 1)) * (1.0 / ROW_TILE))
    er = lax.broadcasted_iota(I32, (N_EXPERTS, N_EXPERTS), 0)
    ec = lax.broadcasted_iota(I32, (N_EXPERTS, N_EXPERTS), 1)
    before = jnp.where(ec < er, 1.0, 0.0).astype(BF16)
    tiles_b = jnp.broadcast_to(tiles, (N_EXPERTS, LANES)).astype(BF16)
    t_start = _dot(before, tiles_b)[:, 0:1]
    t_end = t_start + tiles
    p_start = t_start * float(ROW_TILE)
    lt_ref[...] = jnp.where(tiles > 0.0, (t_end - 1.0) * float(ROW_TILE), -1.0).astype(I32)

    row = lax.broadcasted_iota(I32, (N_EXPERTS, DEST_TM), 0)
    d_ref[...] = jnp.concatenate(
        [jnp.sum(jnp.where(row == e_ref[k:k + 1, :], p_start, 0.0), axis=0, keepdims=True)
         for k in range(TOPK_EXPERTS)], axis=0).astype(I32) + rk_ref[...]

    tile_id = lax.broadcasted_iota(I32, (N_EXPERTS, TILE_TABLE), 1).astype(F32)
    owner = jnp.sum(jnp.where(t_end <= tile_id, 1, 0), axis=0, keepdims=True)
    te_ref[...] = jnp.minimum(owner, N_EXPERTS - 1)
    nu_ref[...] = jnp.broadcast_to(t_end[N_EXPERTS - 1:N_EXPERTS, :], (1, LANES)).astype(I32)


def _dest(cnt, e_t, rk_t):
    k = TOPK_EXPERTS
    tok = lambda i: (0, i)
    return pl.pallas_call(
        _dest_body,
        grid=(N_TOK // DEST_TM,),
        in_specs=[
            pl.BlockSpec((N_EXPERTS, 1), lambda i: (0, 0)),
            pl.BlockSpec((k, DEST_TM), tok),
            pl.BlockSpec((k, DEST_TM), tok),
        ],
        out_specs=[
            pl.BlockSpec((k, DEST_TM), tok),
            pl.BlockSpec((1, TILE_TABLE), lambda i: (0, 0)),
            pl.BlockSpec((1, LANES), lambda i: (0, 0)),
            pl.BlockSpec((N_EXPERTS, 1), lambda i: (0, 0)),
        ],
        out_shape=[
            jax.ShapeDtypeStruct((k, N_TOK), I32),
            jax.ShapeDtypeStruct((1, TILE_TABLE), I32),
            jax.ShapeDtypeStruct((1, LANES), I32),
            jax.ShapeDtypeStruct((N_EXPERTS, 1), I32),
        ],
        compiler_params=_cparams("arbitrary"),
        name="dest",
    )(cnt, e_t, rk_t)


def _row_copy(src, dst, sem):
    return pltpu.make_async_copy(src, dst, sem)


def _dispatch_body(lt_ref, d_ref, h_ref, xs_ref, zero_ref, sem):
    @pl.when(pl.program_id(0) == 0)
    def _():
        zero_ref[...] = jnp.zeros_like(zero_ref)

        def tile_copy(e):
            r0 = pl.multiple_of(jnp.maximum(lt_ref[e], 0), ROW_TILE)
            return pltpu.make_async_copy(zero_ref, xs_ref.at[pl.ds(r0, ROW_TILE), :], sem)

        def z_start(e, carry):
            @pl.when(lt_ref[e] >= 0)
            def _():
                tile_copy(e).start()
            return carry

        def z_wait(e, carry):
            @pl.when(lt_ref[e] >= 0)
            def _():
                tile_copy(e).wait()
            return carry

        lax.fori_loop(0, N_EXPERTS, z_start, 0)
        lax.fori_loop(0, N_EXPERTS, z_wait, 0)

    def row(t, k):
        return _row_copy(h_ref.at[pl.ds(t, 1), :], xs_ref.at[pl.ds(d_ref[k, t], 1), :], sem)

    def issue(t, carry):
        for k in range(TOPK_EXPERTS):
            row(t, k).start(priority=k % 2)
        return carry

    def drain(t, carry):
        for k in range(TOPK_EXPERTS):
            row(t, k).wait()
        return carry

    lax.fori_loop(0, DISP_TM, issue, 0)
    lax.fori_loop(0, DISP_TM, drain, 0)


def _dispatch(last_tile_row, dest_t, h1):
    grid_spec = pltpu.PrefetchScalarGridSpec(
        num_scalar_prefetch=1,
        grid=(N_TOK // DISP_TM,),
        in_specs=[
            pl.BlockSpec((TOPK_EXPERTS, DISP_TM), lambda i, lt: (0, i), memory_space=pltpu.SMEM),
            pl.BlockSpec((DISP_TM, D_MODEL), lambda i, lt: (i, 0)),
        ],
        out_specs=pl.BlockSpec(memory_space=pl.ANY),
        scratch_shapes=[pltpu.VMEM((ROW_TILE, D_MODEL), F32), pltpu.SemaphoreType.DMA],
    )
    return pl.pallas_call(
        _dispatch_body,
        grid_spec=grid_spec,
        out_shape=jax.ShapeDtypeStruct((N_ROWS, D_MODEL), F32),
        compiler_params=_cparams("arbitrary"),
        name="dispatch",
    )(last_tile_row, dest_t, h1)


def _experts_body(te_ref, nu_ref, xs_ref, wg_ref, wu_ref, wd_ref, ys_ref, wg_b, wu_b, wd_b):
    i = pl.program_id(0)

    @pl.when(i < nu_ref[0])
    def _():
        prev = te_ref[jnp.maximum(i - 1, 0)]

        @pl.when((i == 0) | (te_ref[i] != prev))
        def _():
            wg_b[...] = wg_ref[0].astype(BF16)
            wu_b[...] = wu_ref[0].astype(BF16)
            wd_b[...] = wd_ref[0].astype(BF16)

        x = xs_ref[...].astype(BF16)
        g = _dot(x, wg_b[...])
        u = _dot(x, wu_b[...])
        h = (g * _sigmoid(g)) * u
        ys_ref[...] = _dot(h.astype(BF16), wd_b[...])


def _experts(tile_expert, n_used, xs, w_gate, w_up, w_down):
    def tile(i, te, nu):
        return (jnp.minimum(i, nu[0] - 1), 0)

    def expert(i, te, nu):
        return (te[jnp.minimum(i, nu[0] - 1)], 0, 0)

    grid_spec = pltpu.PrefetchScalarGridSpec(
        num_scalar_prefetch=2,
        grid=(N_ROW_TILES,),
        in_specs=[
            pl.BlockSpec((ROW_TILE, D_MODEL), tile),
            pl.BlockSpec((1, D_MODEL, D_EXPERT), expert),
            pl.BlockSpec((1, D_MODEL, D_EXPERT), expert),
            pl.BlockSpec((1, D_EXPERT, D_MODEL), expert),
        ],
        out_specs=pl.BlockSpec((ROW_TILE, D_MODEL), tile),
        scratch_shapes=[
            pltpu.VMEM((D_MODEL, D_EXPERT), BF16),
            pltpu.VMEM((D_MODEL, D_EXPERT), BF16),
            pltpu.VMEM((D_EXPERT, D_MODEL), BF16),
        ],
    )
    return pl.pallas_call(
        _experts_body,
        grid_spec=grid_spec,
        out_shape=jax.ShapeDtypeStruct((N_ROWS, D_MODEL), F32),
        compiler_params=_cparams("arbitrary"),
        name="experts",
    )(tile_expert, n_used, xs, w_gate, w_up, w_down)


def _final_body(d_ref, dn_ref, w_ref, ys_ref, h_ref, hb_ref, p_ref, wsg_ref, wsu_ref, wsd_ref, wpl_ref, wpg_ref,
                g_ref, b_ref, o_ref, buf_ref, sem):
    i = pl.program_id(0)
    slot = lax.rem(i, 2)

    def row(idx_ref, s, t, k):
        return _row_copy(ys_ref.at[pl.ds(idx_ref[k, t], 1), :], buf_ref.at[s, k, pl.ds(t, 1), :], sem.at[s])

    def issue(idx_ref, s):
        def body(t, carry):
            for k in range(TOPK_EXPERTS):
                row(idx_ref, s, t, k).start(priority=k % 2)
            return carry
        lax.fori_loop(0, FINAL_TM, body, 0)

    @pl.when(i == 0)
    def _():
        issue(d_ref, 0)

    @pl.when(i + 1 < pl.num_programs(0))
    def _():
        issue(dn_ref, 1 - slot)

    hb = hb_ref[...]
    sg = _dot(hb, wsg_ref[...])
    shared = _dot(((sg * _sigmoid(sg)) * _dot(hb, wsu_ref[...])).astype(BF16), wsd_ref[...])
    ple = _sigmoid(_dot(hb, wpg_ref[...])) * _dot(p_ref[...].astype(BF16), wpl_ref[...])

    def drain(t, carry):
        for k in range(TOPK_EXPERTS):
            row(d_ref, slot, t, k).wait()
        return carry

    lax.fori_loop(0, FINAL_TM, drain, 0)

    w_col = w_ref[...].T
    routed = buf_ref[slot, 0] * w_col[:, 0:1]
    for k in range(1, TOPK_EXPERTS):
        routed = routed + buf_ref[slot, k] * w_col[:, k:k + 1]
    o_ref[...] = _layer_norm(ALPHA * h_ref[...] + (routed + shared) + ple, g_ref[...], b_ref[...])


def _final(dest_t, w_t, ys, h1, h_bf, p2d, w_s_gate, w_s_up, w_s_down, w_ple, w_ple_gate, ln_g, ln_b):
    d = D_MODEL
    n_steps = N_TOK // FINAL_TM
    full = lambda i: (0, 0)
    tok = lambda i: (0, i)
    tok_next = lambda i: (0, jnp.minimum(i + 1, n_steps - 1))
    return pl.pallas_call(
        _final_body,
        grid=(n_steps,),
        in_specs=[
            pl.BlockSpec((TOPK_EXPERTS, FINAL_TM), tok, memory_space=pltpu.SMEM),
            pl.BlockSpec((TOPK_EXPERTS, FINAL_TM), tok_next, memory_space=pltpu.SMEM),
            pl.BlockSpec((TOPK_EXPERTS, FINAL_TM), tok),
            pl.BlockSpec(memory_space=pl.ANY),
            pl.BlockSpec((FINAL_TM, d), lambda i: (i, 0)),
            pl.BlockSpec((FINAL_TM, d), lambda i: (i, 0)),
            pl.BlockSpec((FINAL_TM, PLE_DIM), lambda i: (i, 0)),
            pl.BlockSpec((d, D_SHARED), full),
            pl.BlockSpec((d, D_SHARED), full),
            pl.BlockSpec((D_SHARED, d), full),
            pl.BlockSpec((PLE_DIM, d), full),
            pl.BlockSpec((d, d), full),
            pl.BlockSpec((1, d), full),
            pl.BlockSpec((1, d), full),
        ],
        out_specs=pl.BlockSpec((FINAL_TM, d), lambda i: (i, 0)),
        out_shape=jax.ShapeDtypeStruct((N_TOK, d), F32),
        scratch_shapes=[
            pltpu.VMEM((2, TOPK_EXPERTS, FINAL_TM, d), F32),
            pltpu.SemaphoreType.DMA((2,)),
        ],
        compiler_params=_cparams("arbitrary"),
        name="combine_ln2",
    )(dest_t, dest_t, w_t, ys, h1, h_bf, p2d, w_s_gate, w_s_up, w_s_down, w_ple, w_ple_gate, ln_g, ln_b)


def _rope_tables(positions):
    half = ROT_DIM // 2
    inv = ROPE_THETA ** (-jnp.arange(0, ROT_DIM, 2, dtype=F32) / ROT_DIM)
    ang = positions.astype(F32)[..., None] * inv
    cos, sin = jnp.cos(ang), jnp.sin(ang)
    rest = MOBA_DH - ROT_DIM
    ones = jnp.ones(cos.shape[:-1] + (rest,), F32)
    zeros = jnp.zeros(cos.shape[:-1] + (rest,), F32)
    return (jnp.concatenate([cos, cos, ones], axis=-1), jnp.concatenate([-sin, sin, zeros], axis=-1))


def _layer(h2d, p2d, cos_t, sin_t, w_in, w_gk2, b_gk, norm_g, w_gla_o, w_moba_o, w_out, ln1_g, ln1_b,
           w_router, b_router, w_e_gate, w_e_up, w_e_down, w_s_gate, w_s_up, w_s_down, w_ple, w_ple_gate,
           ln2_g, ln2_b):
    low_end = LOW_SRC + GLA_RANK
    w_cat = jnp.concatenate(
        [w_in[:, :LOW_SRC], w_in[:, low_end:], w_in[:, LOW_SRC:low_end],
         jnp.zeros((D_MODEL, LANES - GLA_RANK), w_in.dtype)], axis=1).astype(BF16)
    w_gk2_pad = jnp.concatenate([w_gk2, jnp.zeros((LANES - GLA_RANK, GLA_KDIM), w_gk2.dtype)], axis=0)

    proj = _in_proj(h2d, w_cat)
    gla_out = _gla(proj, w_gk2_pad, b_gk[None, :], norm_g[None, :])
    moba_out = _moba(proj, cos_t, sin_t)
    h1, h_bf = _mix(gla_out, moba_out, proj, h2d, w_gla_o.astype(BF16), w_moba_o.astype(BF16),
                    w_out.astype(BF16), ln1_g[None, :], ln1_b[None, :])
    e_t, w_t, rk_t, cnt = _route(h_bf, w_router.T.astype(BF16), b_router[:, None])
    dest_t, tile_expert, n_used, last_tile_row = _dest(cnt, e_t, rk_t)
    xs = _dispatch(last_tile_row.reshape(N_EXPERTS), dest_t, h1)
    ys = _experts(tile_expert.reshape(TILE_TABLE), n_used[0, 0:1], xs, w_e_gate, w_e_up, w_e_down)
    return _final(dest_t, w_t, ys, h1, h_bf, p2d, w_s_gate.astype(BF16), w_s_up.astype(BF16),
                  w_s_down.astype(BF16), w_ple.astype(BF16), w_ple_gate.astype(BF16),
                  ln2_g[None, :], ln2_b[None, :])


def kernel(x, p, positions, w_in, w_gla_gk2, b_gla_gk, gla_norm_g, w_gla_o, w_moba_o, w_out, ln1_g, ln1_b,
           w_router, b_router, w_e_gate, w_e_up, w_e_down, w_s_gate, w_s_up, w_s_down, w_ple, w_ple_gate,
           ln2_g, ln2_b):
    cos_t, sin_t = _rope_tables(positions)
    h = x.reshape(N_TOK, D_MODEL)
    for i in range(DEPTH):
        h = _layer(h, p[i].reshape(N_TOK, PLE_DIM), cos_t, sin_t, w_in[i], w_gla_gk2[i], b_gla_gk[i],
                   gla_norm_g[i], w_gla_o[i], w_moba_o[i], w_out[i], ln1_g[i], ln1_b[i], w_router[i],
                   b_router[i], w_e_gate[i], w_e_up[i], w_e_down[i], w_s_gate[i], w_s_up[i], w_s_down[i],
                   w_ple[i], w_ple_gate[i], ln2_g[i], ln2_b[i])
    return h.reshape(BATCH, SEQ, D_MODEL)
```

```python
import functools

import jax
import jax.numpy as jnp
from jax import lax
from jax.experimental import pallas as pl
from jax.experimental.pallas import tpu as pltpu

F32 = jnp.float32
BF16 = jnp.bfloat16
I32 = jnp.int32
U32 = jnp.uint32

LANES = 128
SUBLANES = 8
VMEM_LIMIT_BYTES = 48 * 1024 * 1024

D_MODEL = 1024
BATCH = 8
SEQ = 2048
N_TOK = BATCH * SEQ
GLA_HEADS = 4
GLA_DK = 128
GLA_DV = 256
GLA_RANK = 16
GLA_NORMALIZER = 16.0
GLA_CHUNK = 64
GLA_GROUP = 4 * GLA_CHUNK
MOBA_HEADS = 8
MOBA_DH = 128
MOBA_BLOCK = 256
MOBA_TOPK = 3
ROT_DIM = 32
ROPE_THETA = 500000.0
N_EXPERTS = 256
TOPK_EXPERTS = 8
N_GROUPS = 8
GROUP_SIZE = N_EXPERTS // N_GROUPS
TOPK_GROUPS = 4
D_EXPERT = 256
D_SHARED = 256
ROUTED_SCALE = 2.5
PLE_DIM = 256
LN_EPS = 1e-5
DEPTH = 1
ALPHA = (2.0 * DEPTH) ** 0.25
GLA_KDIM = GLA_HEADS * GLA_DK
GLA_VDIM = GLA_HEADS * GLA_DV
MOBA_DIM = MOBA_HEADS * MOBA_DH
N_KBLK = SEQ // MOBA_BLOCK

COL_GQ = 0
COL_GK = COL_GQ + GLA_KDIM
COL_GV = COL_GK + GLA_KDIM
COL_GR = COL_GV + GLA_VDIM
COL_MQ = COL_GR + GLA_VDIM
COL_MK = COL_MQ + MOBA_DIM
COL_MV = COL_MK + MOBA_DIM
COL_GA = COL_MV + MOBA_DIM
COL_GB = COL_GA + D_MODEL
COL_LOW = COL_GB + D_MODEL
PROJ_COLS = COL_LOW + LANES
LOW_SRC = 2 * GLA_KDIM + 2 * GLA_VDIM

ROW_TILE = 256
PACKED_COLS = D_MODEL // 2
N_ASSIGN = N_TOK * TOPK_EXPERTS
N_ROW_TILES = (N_ASSIGN + N_EXPERTS * (ROW_TILE - 1) + ROW_TILE - 1) // ROW_TILE
N_ROWS = N_ROW_TILES * ROW_TILE
TILE_TABLE = -(-N_ROW_TILES // LANES) * LANES

PROJ_TM = 512
PROJ_TN = PROJ_COLS // 5
MIX_TM = 512
ROUTE_TM = 512
DEST_TM = 2048
DISP_TM = 256
FINAL_TM = 256

_NEG_INF = float("-inf")
LOG2_E = 1.4426950408889634


def _cparams(*sem):
    return pltpu.CompilerParams(dimension_semantics=sem, vmem_limit_bytes=VMEM_LIMIT_BYTES)


def _dot(a, b):
    return jnp.dot(a, b, preferred_element_type=F32)


def _dot_nt(a, b):
    return lax.dot_general(a, b, (((1,), (1,)), ((), ())), preferred_element_type=F32)


def _dot_tn(a, b):
    return lax.dot_general(a, b, (((0,), (0,)), ((), ())), preferred_element_type=F32)


def _split_bf16(x):
    hi = x.astype(BF16)
    lo = (x - hi.astype(F32)).astype(BF16)
    return hi, lo


def _sigmoid(x):
    return 1.0 / (1.0 + jnp.exp(-x))


def _pack_bf16_pair(x):
    lo = lax.bitcast_convert_type(x[:, :PACKED_COLS].astype(BF16).astype(F32), U32)
    hi = lax.bitcast_convert_type(x[:, PACKED_COLS:].astype(BF16).astype(F32), U32)
    return hi | lax.shift_right_logical(lo, jnp.uint32(16))


def _unpack_bf16_pair(w):
    lo = lax.bitcast_convert_type(lax.shift_left(w, jnp.uint32(16)), F32)
    hi = lax.bitcast_convert_type(w & jnp.uint32(0xFFFF0000), F32)
    return lo, hi


def _proj_body(x_ref, w_ref, o_ref):
    o_ref[...] = _dot(x_ref[...].astype(BF16), w_ref[...]).astype(BF16)


def _in_proj(x2d, w_cat):
    return pl.pallas_call(
        _proj_body,
        grid=(PROJ_COLS // PROJ_TN, N_TOK // PROJ_TM),
        in_specs=[
            pl.BlockSpec((PROJ_TM, D_MODEL), lambda j, i: (i, 0)),
            pl.BlockSpec((D_MODEL, PROJ_TN), lambda j, i: (0, j)),
        ],
        out_specs=pl.BlockSpec((PROJ_TM, PROJ_TN), lambda j, i: (i, j)),
        out_shape=jax.ShapeDtypeStruct((N_TOK, PROJ_COLS), BF16),
        compiler_params=_cparams("parallel", "parallel"),
        name="in_proj",
    )(x2d, w_cat)


def _gla_body(q_ref, k_ref, v_ref, r_ref, low_ref, wg_ref, bg_ref, ng_ref, o_ref, st_ref, gk_ref):
    w_hi, w_lo = _split_bf16(wg_ref[...])
    low = low_ref[...]
    lin = _dot(low, w_hi) + _dot(low, w_lo) + bg_ref[...]
    gk_ref[...] = (jnp.minimum(lin, 0.0) - jnp.log1p(jnp.exp(-jnp.abs(lin)))) * (1.0 / GLA_NORMALIZER)
    st_ref[...] = jnp.zeros_like(st_ref)

    ri = lax.broadcasted_iota(I32, (GLA_GROUP, GLA_GROUP), 0)
    ci = lax.broadcasted_iota(I32, (GLA_GROUP, GLA_GROUP), 1)
    same_chunk = lax.shift_right_logical(ri, GLA_CHUNK.bit_length() - 1) == lax.shift_right_logical(
        ci, GLA_CHUNK.bit_length() - 1)
    causal = same_chunk & (ri >= ci)
    sums = jnp.concatenate([jnp.where(causal, 1.0, 0.0), jnp.where(same_chunk, 1.0, 0.0)], axis=0).astype(BF16)
    gain = ng_ref[...]

    def group(c, carry):
        rows = pl.ds(pl.multiple_of(c * GLA_GROUP, GLA_GROUP), GLA_GROUP)
        g_hi, g_lo = _split_bf16(gk_ref[rows, :])
        bb = _dot(sums, g_hi) + _dot(sums, g_lo)
        b = bb[0:GLA_GROUP]
        b_end = bb[GLA_GROUP:2 * GLA_GROUP]
        q = q_ref[rows, :].astype(F32) * (GLA_DK ** -0.5)
        k = k_ref[rows, :].astype(F32)
        v = v_ref[rows, :]
        q_e = (q * jnp.exp(b)).astype(BF16)
        k_e = (k * jnp.exp(-b)).astype(BF16)
        k_d = (k * jnp.exp(b_end - b)).astype(BF16)
        att = jnp.where(causal, _dot_nt(q_e, k_e), 0.0)
        o = _dot(att.astype(BF16), v)
        st = st_ref[...]
        inter = []
        for j in range(GLA_GROUP // GLA_CHUNK):
            cr = slice(j * GLA_CHUNK, (j + 1) * GLA_CHUNK)
            inter.append(_dot_nt(q_e[cr], st.astype(BF16)))
            st = st * jnp.exp(b_end[j * GLA_CHUNK:j * GLA_CHUNK + 1, :]) + _dot_tn(v[cr], k_d[cr])
        st_ref[...] = st
        o = o + jnp.concatenate(inter, axis=0)
        o = o * lax.rsqrt(jnp.mean(o * o, axis=-1, keepdims=True) + LN_EPS) * gain
        r = r_ref[rows, :].astype(F32)
        o_ref[rows, :] = (o * (r * _sigmoid(r))).astype(BF16)
        return carry

    lax.fori_loop(0, SEQ // GLA_GROUP, group, 0)


def _gla(proj, w_gk2_pad, b_gk, norm_g):
    kb, vb = GLA_DK, GLA_DV
    return pl.pallas_call(
        _gla_body,
        grid=(BATCH, GLA_HEADS),
        in_specs=[
            pl.BlockSpec((SEQ, kb), lambda b, h: (b, COL_GQ // kb + h)),
            pl.BlockSpec((SEQ, kb), lambda b, h: (b, COL_GK // kb + h)),
            pl.BlockSpec((SEQ, vb), lambda b, h: (b, COL_GV // vb + h)),
            pl.BlockSpec((SEQ, vb), lambda b, h: (b, COL_GR // vb + h)),
            pl.BlockSpec((SEQ, LANES), lambda b, h: (b, COL_LOW // LANES)),
            pl.BlockSpec((LANES, kb), lambda b, h: (0, h)),
            pl.BlockSpec((1, kb), lambda b, h: (0, h)),
            pl.BlockSpec((1, vb), lambda b, h: (0, 0)),
        ],
        out_specs=pl.BlockSpec((SEQ, vb), lambda b, h: (b, h)),
        out_shape=jax.ShapeDtypeStruct((N_TOK, GLA_VDIM), BF16),
        scratch_shapes=[pltpu.VMEM((vb, kb), F32), pltpu.VMEM((SEQ, kb), F32)],
        compiler_params=_cparams("parallel", "parallel"),
        name="gla",
    )(proj, proj, proj, proj, proj, w_gk2_pad, b_gk, norm_g)


def _moba_body(q_ref, k_ref, v_ref, c_ref, s_ref, o_ref, qs_ref, ks_ref, vt_ref, bias_ref,
               sc_a_ref, sc_b_ref, pr_a_ref, pr_b_ref):
    cos_t = c_ref[0]
    sin_t = s_ref[0]
    lane = lax.broadcasted_iota(I32, (SEQ, MOBA_DH), 1)
    half = ROT_DIM // 2

    def rope(x):
        partner = jnp.where(lane < half, pltpu.roll(x, MOBA_DH - half, 1), pltpu.roll(x, half, 1))
        return x * cos_t + partner * sin_t

    q = rope(q_ref[...].astype(F32))
    k = rope(k_ref[...].astype(F32))
    q_hi, q_lo = _split_bf16(q)
    qs_ref[...] = (q * (MOBA_DH ** -0.5 * LOG2_E)).astype(BF16)
    ks_ref[...] = k.astype(BF16)
    vt_ref[...] = v_ref[...].astype(F32).T.astype(BF16)

    k_mean = jnp.concatenate(
        [jnp.mean(k[j * MOBA_BLOCK:(j + 1) * MOBA_BLOCK], axis=0, keepdims=True) for j in range(N_KBLK)], axis=0)
    m_hi, m_lo = _split_bf16(k_mean)
    s_blk = _dot_nt(m_hi, q_hi) + _dot_nt(m_hi, q_lo) + _dot_nt(m_lo, q_hi)
    blk = lax.broadcasted_iota(I32, (N_KBLK, SEQ), 0)
    q_blk = lax.shift_right_logical(lax.broadcasted_iota(I32, (N_KBLK, SEQ), 1), MOBA_BLOCK.bit_length() - 1)
    past = blk < q_blk
    s_blk = jnp.where(past, s_blk, _NEG_INF)
    beaten = jnp.zeros((N_KBLK, SEQ), I32)
    for j in range(N_KBLK):
        row = s_blk[j:j + 1, :]
        beaten += jnp.where((row > s_blk) | ((row == s_blk) & (j < blk)), 1, 0)
    bias_ref[...] = jnp.where(past & (beaten < MOBA_TOPK), 0.0, _NEG_INF)

    kr = lax.broadcasted_iota(I32, (MOBA_BLOCK, MOBA_BLOCK), 0)
    qc = lax.broadcasted_iota(I32, (MOBA_BLOCK, MOBA_BLOCK), 1)
    own_bias = jnp.where(kr <= qc, 0.0, _NEG_INF)

    sc_bufs = (sc_a_ref, sc_b_ref)
    pr_bufs = (pr_a_ref, pr_b_ref)
    zeros_row = jnp.zeros((1, MOBA_BLOCK), F32)
    future_row = jnp.full((1, MOBA_BLOCK), _NEG_INF, F32)
    for pair in range(N_KBLK // 2):
        q_blocks = (2 * pair, 2 * pair + 1)
        cols = slice(q_blocks[0] * MOBA_BLOCK, (q_blocks[1] + 1) * MOBA_BLOCK)
        n_kblk = q_blocks[1] + 1
        n_keys = n_kblk * MOBA_BLOCK
        sc, pr = sc_bufs[pair % 2], pr_bufs[pair % 2]
        q_pair = qs_ref[cols, :]

        def query_bias(j):
            halves = []
            for qb in q_blocks:
                q_cols = slice(qb * MOBA_BLOCK, (qb + 1) * MOBA_BLOCK)
                halves.append(bias_ref[j:j + 1, q_cols] if j < qb else zeros_row if j == qb else future_row)
            return jnp.concatenate(halves, axis=1)

        biases = [query_bias(j) for j in range(n_kblk)]
        col_max = []
        for j in range(n_kblk):
            rows = slice(j * MOBA_BLOCK, (j + 1) * MOBA_BLOCK)
            s = _dot_nt(ks_ref[rows, :], q_pair)
            if j in q_blocks:
                h = q_blocks.index(j)
                own = s[:, h * MOBA_BLOCK:(h + 1) * MOBA_BLOCK] + own_bias
                s = jnp.concatenate([own, s[:, MOBA_BLOCK:]] if h == 0 else [s[:, :MOBA_BLOCK], own], axis=1)
            sc[rows, :] = s
            col_max.append(jnp.max(s, axis=0, keepdims=True) + biases[j])
        m = functools.reduce(jnp.maximum, col_max)
        denom = jnp.zeros((1, 2 * MOBA_BLOCK), F32)
        for j in range(n_kblk):
            rows = slice(j * MOBA_BLOCK, (j + 1) * MOBA_BLOCK)
            p = jnp.exp2(sc[rows, :] - (m - biases[j]))
            denom = denom + jnp.sum(p, axis=0, keepdims=True)
            pr[rows, :] = p.astype(BF16)
        o_t = _dot(vt_ref[:, 0:n_keys], pr[0:n_keys, :]) * (1.0 / denom)
        o_ref[cols, :] = o_t.T.astype(BF16)


def _moba(proj, cos_t, sin_t):
    dh = MOBA_DH
    return pl.pallas_call(
        _moba_body,
        grid=(BATCH, MOBA_HEADS),
        in_specs=[
            pl.BlockSpec((SEQ, dh), lambda b, h: (b, COL_MQ // dh + h)),
            pl.BlockSpec((SEQ, dh), lambda b, h: (b, COL_MK // dh + h)),
            pl.BlockSpec((SEQ, dh), lambda b, h: (b, COL_MV // dh + h)),
            pl.BlockSpec((1, SEQ, dh), lambda b, h: (b, 0, 0)),
            pl.BlockSpec((1, SEQ, dh), lambda b, h: (b, 0, 0)),
        ],
        out_specs=pl.BlockSpec((SEQ, dh), lambda b, h: (b, h)),
        out_shape=jax.ShapeDtypeStruct((N_TOK, MOBA_DIM), BF16),
        scratch_shapes=[
            pltpu.VMEM((SEQ, dh), BF16),
            pltpu.VMEM((SEQ, dh), BF16),
            pltpu.VMEM((dh, SEQ), BF16),
            pltpu.VMEM((N_KBLK, SEQ), F32),
            pltpu.VMEM((SEQ, 2 * MOBA_BLOCK), F32),
            pltpu.VMEM((SEQ, 2 * MOBA_BLOCK), F32),
            pltpu.VMEM((SEQ, 2 * MOBA_BLOCK), BF16),
            pltpu.VMEM((SEQ, 2 * MOBA_BLOCK), BF16),
        ],
        compiler_params=_cparams("parallel", "parallel"),
        name="moba",
    )(proj, proj, proj, cos_t, sin_t)


def _layer_norm(z, g, b):
    mu = jnp.mean(z, axis=-1, keepdims=True)
    zc = z - mu
    var = jnp.mean(zc * zc, axis=-1, keepdims=True)
    return zc * lax.rsqrt(var + LN_EPS) * g + b


def _mix_body(gla_ref, moba_ref, ga_ref, gb_ref, x_ref, wgo_ref, wmo_ref, wo_ref, g_ref, b_ref,
              h_ref, hb_ref, hp_ref):
    y_gla = _dot(gla_ref[...], wgo_ref[...])
    y_moba = _dot(moba_ref[...], wmo_ref[...])
    merged = _sigmoid(ga_ref[...].astype(F32)) * y_gla + _sigmoid(gb_ref[...].astype(F32)) * y_moba
    mix = _dot(merged.astype(BF16), wo_ref[...])
    h = _layer_norm(ALPHA * x_ref[...] + mix, g_ref[...], b_ref[...])
    h_ref[...] = h
    hb_ref[...] = h.astype(BF16)
    hp_ref[...] = _pack_bf16_pair(h)


def _mix(gla_out, moba_out, proj, x2d, w_gla_o, w_moba_o, w_out, ln_g, ln_b):
    d = D_MODEL
    row = lambda i: (i, 0)
    full = lambda i: (0, 0)
    return pl.pallas_call(
        _mix_body,
        grid=(N_TOK // MIX_TM,),
        in_specs=[
            pl.BlockSpec((MIX_TM, d), row),
            pl.BlockSpec((MIX_TM, d), row),
            pl.BlockSpec((MIX_TM, d), lambda i: (i, COL_GA // d)),
            pl.BlockSpec((MIX_TM, d), lambda i: (i, COL_GB // d)),
            pl.BlockSpec((MIX_TM, d), row),
            pl.BlockSpec((d, d), full),
            pl.BlockSpec((d, d), full),
            pl.BlockSpec((d, d), full),
            pl.BlockSpec((1, d), full),
            pl.BlockSpec((1, d), full),
        ],
        out_specs=[
            pl.BlockSpec((MIX_TM, d), row),
            pl.BlockSpec((MIX_TM, d), row),
            pl.BlockSpec((MIX_TM, PACKED_COLS), row),
        ],
        out_shape=[
            jax.ShapeDtypeStruct((N_TOK, d), F32),
            jax.ShapeDtypeStruct((N_TOK, d), BF16),
            jax.ShapeDtypeStruct((N_TOK, PACKED_COLS), U32),
        ],
        compiler_params=_cparams("parallel"),
        name="mix_ln1",
    )(gla_out, moba_out, proj, proj, x2d, w_gla_o, w_moba_o, w_out, ln_g, ln_b)


def _route_body(h_ref, wr_ref, br_ref, e_ref, w_ref, rk_ref, cnt_ref, carry_ref):
    tm = ROUTE_TM

    @pl.when(pl.program_id(0) == 0)
    def _():
        carry_ref[...] = jnp.zeros_like(carry_ref)

    scores = _sigmoid(_dot_nt(wr_ref[...], h_ref[...]))
    biased = scores + br_ref[...]
    row = lax.broadcasted_iota(I32, (N_EXPERTS, tm), 0).astype(F32)
    row_g = lax.broadcasted_iota(I32, (GROUP_SIZE, tm), 0).astype(F32)

    g_scores = []
    for g in range(N_GROUPS):
        grp = biased[g * GROUP_SIZE:(g + 1) * GROUP_SIZE]
        m1 = jnp.max(grp, axis=0, keepdims=True)
        first = jnp.min(jnp.where(grp == m1, row_g, float(GROUP_SIZE)), axis=0, keepdims=True)
        m2 = jnp.max(jnp.where(row_g == first, _NEG_INF, grp), axis=0, keepdims=True)
        g_scores.append(m1 + m2)
    g_score = jnp.concatenate(g_scores, axis=0)
    g_row = lax.broadcasted_iota(I32, (N_GROUPS, tm), 0)
    g_beaten = jnp.zeros((N_GROUPS, tm), I32)
    for g in range(N_GROUPS):
        r = g_score[g:g + 1, :]
        g_beaten += jnp.where((r > g_score) | ((r == g_score) & (g < g_row)), 1, 0)
    g_keep = g_beaten < TOPK_GROUPS
    masked = jnp.concatenate(
        [jnp.where(g_keep[g:g + 1, :], biased[g * GROUP_SIZE:(g + 1) * GROUP_SIZE], _NEG_INF)
         for g in range(N_GROUPS)], axis=0)

    onehot = jnp.zeros((N_EXPERTS, tm), F32)
    picks, pick_scores = [], []
    for _ in range(TOPK_EXPERTS):
        m = jnp.max(masked, axis=0, keepdims=True)
        idx = jnp.min(jnp.where(masked == m, row, float(N_EXPERTS)), axis=0, keepdims=True)
        hit = row == idx
        picks.append(idx)
        pick_scores.append(jnp.sum(jnp.where(hit, scores, 0.0), axis=0, keepdims=True))
        onehot = onehot + jnp.where(hit, 1.0, 0.0)
        masked = jnp.where(hit, _NEG_INF, masked)
    sel = jnp.concatenate(pick_scores, axis=0)
    e_ref[...] = jnp.concatenate(picks, axis=0).astype(I32)
    w_ref[...] = sel / jnp.sum(sel, axis=0, keepdims=True) * ROUTED_SCALE

    t_r = lax.broadcasted_iota(I32, (tm, tm), 0)
    t_c = lax.broadcasted_iota(I32, (tm, tm), 1)
    earlier = jnp.where(t_r < t_c, 1.0, 0.0).astype(BF16)
    seen = _dot(onehot.astype(BF16), earlier) + carry_ref[...]
    rk_ref[...] = jnp.concatenate(
        [jnp.sum(jnp.where(row == idx, seen, 0.0), axis=0, keepdims=True) for idx in picks], axis=0).astype(I32)
    carry_ref[...] += jnp.sum(onehot, axis=1, keepdims=True)
    cnt_ref[...] = carry_ref[...]


def _route(h_bf, w_router_t, b_router_col):
    k = TOPK_EXPERTS
    tok = lambda i: (0, i)
    return pl.pallas_call(
        _route_body,
        grid=(N_TOK // ROUTE_TM,),
        in_specs=[
            pl.BlockSpec((ROUTE_TM, D_MODEL), lambda i: (i, 0)),
            pl.BlockSpec((N_EXPERTS, D_MODEL), lambda i: (0, 0)),
            pl.BlockSpec((N_EXPERTS, 1), lambda i: (0, 0)),
        ],
        out_specs=[
            pl.BlockSpec((k, ROUTE_TM), tok),
            pl.BlockSpec((k, ROUTE_TM), tok),
            pl.BlockSpec((k, ROUTE_TM), tok),
            pl.BlockSpec((N_EXPERTS, 1), lambda i: (0, 0)),
        ],
        out_shape=[
            jax.ShapeDtypeStruct((k, N_TOK), I32),
            jax.ShapeDtypeStruct((k, N_TOK), F32),
            jax.ShapeDtypeStruct((k, N_TOK), I32),
            jax.ShapeDtypeStruct((N_EXPERTS, 1), F32),
        ],
        scratch_shapes=[pltpu.VMEM((N_EXPERTS, 1), F32)],
        compiler_params=_cparams("arbitrary"),
        name="route",
    )(h_bf, w_router_t, b_router_col)


def _dest_body(cnt_ref, e_ref, rk_ref, d_ref, te_ref, nu_ref, lt_ref):
    cnt = cnt_ref[...]
    tiles = jnp.floor((cnt + (ROW_TILE - 1)) * (1.0 / ROW_TILE))
    er = lax.broadcasted_iota(I32, (N_EXPERTS, N_EXPERTS), 0)
    ec = lax.broadcasted_iota(I32, (N_EXPERTS, N_EXPERTS), 1)
    before = jnp.where(ec < er, 1.0, 0.0).astype(BF16)
    tiles_b = jnp.broadcast_to(tiles, (N_EXPERTS, LANES)).astype(BF16)
    t_start = _dot(before, tiles_b)[:, 0:1]
    t_end = t_start + tiles
    p_start = t_start * float(ROW_TILE)
    lt_ref[...] = jnp.where(tiles > 0.0, (t_end - 1.0) * float(ROW_TILE), -1.0).astype(I32)

    row = lax.broadcasted_iota(I32, (N_EXPERTS, DEST_TM), 0)
    d_ref[...] = jnp.concatenate(
        [jnp.sum(jnp.where(row == e_ref[k:k + 1, :], p_start, 0.0), axis=0, keepdims=True)
         for k in range(TOPK_EXPERTS)], axis=0).astype(I32) + rk_ref[...]

    tile_id = lax.broadcasted_iota(I32, (N_EXPERTS, TILE_TABLE), 1).astype(F32)
    owner = jnp.sum(jnp.where(t_end <= tile_id, 1, 0), axis=0, keepdims=True)
    te_ref[...] = jnp.minimum(owner, N_EXPERTS - 1)
    nu_ref[...] = jnp.broadcast_to(t_end[N_EXPERTS - 1:N_EXPERTS, :], (1, LANES)).astype(I32)


def _dest(cnt, e_t, rk_t):
    k = TOPK_EXPERTS
    tok = lambda i: (0, i)
    return pl.pallas_call(
        _dest_body,
        grid=(N_TOK // DEST_TM,),
        in_specs=[
            pl.BlockSpec((N_EXPERTS, 1), lambda i: (0, 0)),
            pl.BlockSpec((k, DEST_TM), tok),
            pl.BlockSpec((k, DEST_TM), tok),
        ],
        out_specs=[
            pl.BlockSpec((k, DEST_TM), tok),
            pl.BlockSpec((1, TILE_TABLE), lambda i: (0, 0)),
            pl.BlockSpec((1, LANES), lambda i: (0, 0)),
            pl.BlockSpec((N_EXPERTS, 1), lambda i: (0, 0)),
        ],
        out_shape=[
            jax.ShapeDtypeStruct((k, N_TOK), I32),
            jax.ShapeDtypeStruct((1, TILE_TABLE), I32),
            jax.ShapeDtypeStruct((1, LANES), I32),
            jax.ShapeDtypeStruct((N_EXPERTS, 1), I32),
        ],
        compiler_params=_cparams("arbitrary"),
        name="dest",
    )(cnt, e_t, rk_t)


def _row_copy(src, dst, sem):
    return pltpu.make_async_copy(src, dst, sem)


def _dispatch_body(lt_ref, d_ref, h_ref, xs_ref, zero_ref, sem):
    @pl.when(pl.program_id(0) == 0)
    def _():
        zero_ref[...] = jnp.zeros_like(zero_ref)

        def tile_copy(e):
            r0 = pl.multiple_of(jnp.maximum(lt_ref[e], 0), ROW_TILE)
            return pltpu.make_async_copy(zero_ref, xs_ref.at[pl.ds(r0, ROW_TILE), :], sem)

        def z_start(e, carry):
            @pl.when(lt_ref[e] >= 0)
            def _():
                tile_copy(e).start()
            return carry

        def z_wait(e, carry):
            @pl.when(lt_ref[e] >= 0)
            def _():
                tile_copy(e).wait()
            return carry

        lax.fori_loop(0, N_EXPERTS, z_start, 0)
        lax.fori_loop(0, N_EXPERTS, z_wait, 0)

    def row(t, k):
        return _row_copy(h_ref.at[pl.ds(t, 1), :], xs_ref.at[pl.ds(d_ref[k, t], 1), :], sem)

    def issue(t, carry):
        for k in range(TOPK_EXPERTS):
            row(t, k).start(priority=k % 2)
        return carry

    def drain(t, carry):
        for k in range(TOPK_EXPERTS):
            row(t, k).wait()
        return carry

    lax.fori_loop(0, DISP_TM, issue, 0)
    lax.fori_loop(0, DISP_TM, drain, 0)


def _dispatch(last_tile_row, dest_t, h1):
    grid_spec = pltpu.PrefetchScalarGridSpec(
        num_scalar_prefetch=1,
        grid=(N_TOK // DISP_TM,),
        in_specs=[
            pl.BlockSpec((TOPK_EXPERTS, DISP_TM), lambda i, lt: (0, i), memory_space=pltpu.SMEM),
            pl.BlockSpec((DISP_TM, PACKED_COLS), lambda i, lt: (i, 0)),
        ],
        out_specs=pl.BlockSpec(memory_space=pl.ANY),
        scratch_shapes=[pltpu.VMEM((ROW_TILE, PACKED_COLS), U32), pltpu.SemaphoreType.DMA],
    )
    return pl.pallas_call(
        _dispatch_body,
        grid_spec=grid_spec,
        out_shape=jax.ShapeDtypeStruct((N_ROWS, PACKED_COLS), U32),
        compiler_params=_cparams("arbitrary"),
        name="dispatch",
    )(last_tile_row, dest_t, h1)


def _experts_body(te_ref, nu_ref, xs_ref, wg_ref, wu_ref, wd_ref, ys_ref, wg_b, wu_b, wd_b):
    i = pl.program_id(0)

    @pl.when(i < nu_ref[0])
    def _():
        prev = te_ref[jnp.maximum(i - 1, 0)]

        @pl.when((i == 0) | (te_ref[i] != prev))
        def _():
            wg_b[...] = wg_ref[0].astype(BF16)
            wu_b[...] = wu_ref[0].astype(BF16)
            wd_b[...] = wd_ref[0].astype(BF16)

        x_lo, x_hi = _unpack_bf16_pair(xs_ref[...])
        x = jnp.concatenate([x_lo.astype(BF16), x_hi.astype(BF16)], axis=1)
        g = _dot(x, wg_b[...])
        u = _dot(x, wu_b[...])
        h = (g * _sigmoid(g)) * u
        ys_ref[...] = _pack_bf16_pair(_dot(h.astype(BF16), wd_b[...]))


def _experts(tile_expert, n_used, xs, w_gate, w_up, w_down):
    def tile(i, te, nu):
        return (jnp.minimum(i, nu[0] - 1), 0)

    def expert(i, te, nu):
        return (te[jnp.minimum(i, nu[0] - 1)], 0, 0)

    grid_spec = pltpu.PrefetchScalarGridSpec(
        num_scalar_prefetch=2,
        grid=(N_ROW_TILES,),
        in_specs=[
            pl.BlockSpec((ROW_TILE, PACKED_COLS), tile),
            pl.BlockSpec((1, D_MODEL, D_EXPERT), expert),
            pl.BlockSpec((1, D_MODEL, D_EXPERT), expert),
            pl.BlockSpec((1, D_EXPERT, D_MODEL), expert),
        ],
        out_specs=pl.BlockSpec((ROW_TILE, PACKED_COLS), tile),
        scratch_shapes=[
            pltpu.VMEM((D_MODEL, D_EXPERT), BF16),
            pltpu.VMEM((D_MODEL, D_EXPERT), BF16),
            pltpu.VMEM((D_EXPERT, D_MODEL), BF16),
        ],
    )
    return pl.pallas_call(
        _experts_body,
        grid_spec=grid_spec,
        out_shape=jax.ShapeDtypeStruct((N_ROWS, PACKED_COLS), U32),
        compiler_params=_cparams("arbitrary"),
        name="experts",
    )(tile_expert, n_used, xs, w_gate, w_up, w_down)


def _final_body(d_ref, dn_ref, w_ref, ys_ref, h_ref, hb_ref, p_ref, wsg_ref, wsu_ref, wsd_ref, wpl_ref, wpg_ref,
                g_ref, b_ref, o_ref, buf_ref, sem):
    i = pl.program_id(0)
    slot = lax.rem(i, 2)

    def row(idx_ref, s, t, k):
        return _row_copy(ys_ref.at[pl.ds(idx_ref[k, t], 1), :], buf_ref.at[s, k, pl.ds(t, 1), :], sem.at[s])

    def issue(idx_ref, s):
        def body(t, carry):
            for k in range(TOPK_EXPERTS):
                row(idx_ref, s, t, k).start(priority=k % 2)
            return carry
        lax.fori_loop(0, FINAL_TM, body, 0)

    @pl.when(i == 0)
    def _():
        issue(d_ref, 0)

    @pl.when(i + 1 < pl.num_programs(0))
    def _():
        issue(dn_ref, 1 - slot)

    hb = hb_ref[...]
    sg = _dot(hb, wsg_ref[...])
    shared = _dot(((sg * _sigmoid(sg)) * _dot(hb, wsu_ref[...])).astype(BF16), wsd_ref[...])
    ple = _sigmoid(_dot(hb, wpg_ref[...])) * _dot(p_ref[...].astype(BF16), wpl_ref[...])

    def drain(t, carry):
        for k in range(TOPK_EXPERTS):
            row(d_ref, slot, t, k).wait()
        return carry

    lax.fori_loop(0, FINAL_TM, drain, 0)

    w_col = w_ref[...].T
    y_lo, y_hi = _unpack_bf16_pair(buf_ref[slot, 0])
    r_lo, r_hi = y_lo * w_col[:, 0:1], y_hi * w_col[:, 0:1]
    for k in range(1, TOPK_EXPERTS):
        y_lo, y_hi = _unpack_bf16_pair(buf_ref[slot, k])
        r_lo, r_hi = r_lo + y_lo * w_col[:, k:k + 1], r_hi + y_hi * w_col[:, k:k + 1]
    routed = jnp.concatenate([r_lo, r_hi], axis=1)
    o_ref[...] = _layer_norm(ALPHA * h_ref[...] + (routed + shared) + ple, g_ref[...], b_ref[...])


def _final(dest_t, w_t, ys, h1, h_bf, p2d, w_s_gate, w_s_up, w_s_down, w_ple, w_ple_gate, ln_g, ln_b):
    d = D_MODEL
    n_steps = N_TOK // FINAL_TM
    full = lambda i: (0, 0)
    tok = lambda i: (0, i)
    tok_next = lambda i: (0, jnp.minimum(i + 1, n_steps - 1))
    return pl.pallas_call(
        _final_body,
        grid=(n_steps,),
        in_specs=[
            pl.BlockSpec((TOPK_EXPERTS, FINAL_TM), tok, memory_space=pltpu.SMEM),
            pl.BlockSpec((TOPK_EXPERTS, FINAL_TM), tok_next, memory_space=pltpu.SMEM),
            pl.BlockSpec((TOPK_EXPERTS, FINAL_TM), tok),
            pl.BlockSpec(memory_space=pl.ANY),
            pl.BlockSpec((FINAL_TM, d), lambda i: (i, 0)),
            pl.BlockSpec((FINAL_TM, d), lambda i: (i, 0)),
            pl.BlockSpec((FINAL_TM, PLE_DIM), lambda i: (i, 0)),
            pl.BlockSpec((d, D_SHARED), full),
            pl.BlockSpec((d, D_SHARED), full),
            pl.BlockSpec((D_SHARED, d), full),
            pl.BlockSpec((PLE_DIM, d), full),
            pl.BlockSpec((d, d), full),
            pl.BlockSpec((1, d), full),
            pl.BlockSpec((1, d), full),
        ],
        out_specs=pl.BlockSpec((FINAL_TM, d), lambda i: (i, 0)),
        out_shape=jax.ShapeDtypeStruct((N_TOK, d), F32),
        scratch_shapes=[
            pltpu.VMEM((2, TOPK_EXPERTS, FINAL_TM, PACKED_COLS), U32),
            pltpu.SemaphoreType.DMA((2,)),
        ],
        compiler_params=_cparams("arbitrary"),
        name="combine_ln2",
    )(dest_t, dest_t, w_t, ys, h1, h_bf, p2d, w_s_gate, w_s_up, w_s_down, w_ple, w_ple_gate, ln_g, ln_b)


def _rope_tables(positions):
    half = ROT_DIM // 2
    inv = ROPE_THETA ** (-jnp.arange(0, ROT_DIM, 2, dtype=F32) / ROT_DIM)
    ang = positions.astype(F32)[..., None] * inv
    cos, sin = jnp.cos(ang), jnp.sin(ang)
    rest = MOBA_DH - ROT_DIM
    ones = jnp.ones(cos.shape[:-1] + (rest,), F32)
    zeros = jnp.zeros(cos.shape[:-1] + (rest,), F32)
    return (jnp.concatenate([cos, cos, ones], axis=-1), jnp.concatenate([-sin, sin, zeros], axis=-1))


def _layer(h2d, p2d, cos_t, sin_t, w_in, w_gk2, b_gk, norm_g, w_gla_o, w_moba_o, w_out, ln1_g, ln1_b,
           w_router, b_router, w_e_gate, w_e_up, w_e_down, w_s_gate, w_s_up, w_s_down, w_ple, w_ple_gate,
           ln2_g, ln2_b):
    low_end = LOW_SRC + GLA_RANK
    w_cat = jnp.concatenate(
        [w_in[:, :LOW_SRC], w_in[:, low_end:], w_in[:, LOW_SRC:low_end],
         jnp.zeros((D_MODEL, LANES - GLA_RANK), w_in.dtype)], axis=1).astype(BF16)
    w_gk2_pad = jnp.concatenate([w_gk2, jnp.zeros((LANES - GLA_RANK, GLA_KDIM), w_gk2.dtype)], axis=0)

    proj = _in_proj(h2d, w_cat)
    gla_out = _gla(proj, w_gk2_pad, b_gk[None, :], norm_g[None, :])
    moba_out = _moba(proj, cos_t, sin_t)
    h1, h_bf, h_packed = _mix(gla_out, moba_out, proj, h2d, w_gla_o.astype(BF16), w_moba_o.astype(BF16),
                              w_out.astype(BF16), ln1_g[None, :], ln1_b[None, :])
    e_t, w_t, rk_t, cnt = _route(h_bf, w_router.T.astype(BF16), b_router[:, None])
    dest_t, tile_expert, n_used, last_tile_row = _dest(cnt, e_t, rk_t)
    xs = _dispatch(last_tile_row.reshape(N_EXPERTS), dest_t, h_packed)
    ys = _experts(tile_expert.reshape(TILE_TABLE), n_used[0, 0:1], xs, w_e_gate, w_e_up, w_e_down)
    return _final(dest_t, w_t, ys, h1, h_bf, p2d, w_s_gate.astype(BF16), w_s_up.astype(BF16),
                  w_s_down.astype(BF16), w_ple.astype(BF16), w_ple_gate.astype(BF16),
                  ln2_g[None, :], ln2_b[None, :])


def kernel(x, p, positions, w_in, w_gla_gk2, b_gla_gk, gla_norm_g, w_gla_o, w_moba_o, w_out, ln1_g, ln1_b,
           w_router, b_router, w_e_gate, w_e_up, w_e_down, w_s_gate, w_s_up, w_s_down, w_ple, w_ple_gate,
           ln2_g, ln2_b):
    cos_t, sin_t = _rope_tables(positions)
    h = x.reshape(N_TOK, D_MODEL)
    for i in range(DEPTH):
        h = _layer(h, p[i].reshape(N_TOK, PLE_DIM), cos_t, sin_t, w_in[i], w_gla_gk2[i], b_gla_gk[i],
                   gla_norm_g[i], w_gla_o[i], w_moba_o[i], w_out[i], ln1_g[i], ln1_b[i], w_router[i],
                   b_router[i], w_e_gate[i], w_e_up[i], w_e_down[i], w_s_gate[i], w_s_up[i], w_s_down[i],
                   w_ple[i], w_ple_gate[i], ln2_g[i], ln2_b[i])
    return h.reshape(BATCH, SEQ, D_MODEL)
```

```python
import functools

import jax
import jax.numpy as jnp
from jax import lax
from jax.experimental import pallas as pl
from jax.experimental.pallas import tpu as pltpu
from jax.experimental.pallas import tpu_sc as plsc

F32 = jnp.float32
BF16 = jnp.bfloat16
I32 = jnp.int32
U32 = jnp.uint32

LANES = 128
SUBLANES = 8
VMEM_LIMIT_BYTES = 48 * 1024 * 1024

D_MODEL = 1024
BATCH = 8
SEQ = 2048
N_TOK = BATCH * SEQ
GLA_HEADS = 4
GLA_DK = 128
GLA_DV = 256
GLA_RANK = 16
GLA_NORMALIZER = 16.0
GLA_CHUNK = 64
GLA_GROUP = 4 * GLA_CHUNK
MOBA_HEADS = 8
MOBA_DH = 128
MOBA_BLOCK = 256
MOBA_TOPK = 3
ROT_DIM = 32
ROPE_THETA = 500000.0
N_EXPERTS = 256
TOPK_EXPERTS = 8
N_GROUPS = 8
GROUP_SIZE = N_EXPERTS // N_GROUPS
TOPK_GROUPS = 4
D_EXPERT = 256
D_SHARED = 256
ROUTED_SCALE = 2.5
PLE_DIM = 256
LN_EPS = 1e-5
DEPTH = 1
ALPHA = (2.0 * DEPTH) ** 0.25
GLA_KDIM = GLA_HEADS * GLA_DK
GLA_VDIM = GLA_HEADS * GLA_DV
MOBA_DIM = MOBA_HEADS * MOBA_DH
N_KBLK = SEQ // MOBA_BLOCK

COL_GQ = 0
COL_GK = COL_GQ + GLA_KDIM
COL_GV = COL_GK + GLA_KDIM
COL_GR = COL_GV + GLA_VDIM
COL_MQ = COL_GR + GLA_VDIM
COL_MK = COL_MQ + MOBA_DIM
COL_MV = COL_MK + MOBA_DIM
COL_GA = COL_MV + MOBA_DIM
COL_GB = COL_GA + D_MODEL
COL_LOW = COL_GB + D_MODEL
PROJ_COLS = COL_LOW + LANES
LOW_SRC = 2 * GLA_KDIM + 2 * GLA_VDIM

ROW_TILE = 256
PACKED_COLS = D_MODEL // 2
EXPERT_BUFS = 3
YK_ROWS = (TOPK_EXPERTS + 1) * N_TOK
SLOT_CHUNK = 4096
assert EXPERT_BUFS * ROW_TILE <= N_TOK
N_ASSIGN = N_TOK * TOPK_EXPERTS
N_ROW_TILES = (N_ASSIGN + N_EXPERTS * (ROW_TILE - 1) + ROW_TILE - 1) // ROW_TILE
N_ROWS = N_ROW_TILES * ROW_TILE
TILE_TABLE = -(-N_ROW_TILES // LANES) * LANES

PROJ_TM = 512
PROJ_TN = PROJ_COLS // 5
MIX_TM = 512
ROUTE_TM = 512
DEST_TM = 2048
DISP_TM = 256
FINAL_TM = 256

_NEG_INF = float("-inf")
LOG2_E = 1.4426950408889634


def _cparams(*sem):
    return pltpu.CompilerParams(dimension_semantics=sem, vmem_limit_bytes=VMEM_LIMIT_BYTES)


def _dot(a, b):
    return jnp.dot(a, b, preferred_element_type=F32)


def _dot_nt(a, b):
    return lax.dot_general(a, b, (((1,), (1,)), ((), ())), preferred_element_type=F32)


def _dot_tn(a, b):
    return lax.dot_general(a, b, (((0,), (0,)), ((), ())), preferred_element_type=F32)


def _split_bf16(x):
    hi = x.astype(BF16)
    lo = (x - hi.astype(F32)).astype(BF16)
    return hi, lo


def _sigmoid(x):
    return 1.0 / (1.0 + jnp.exp(-x))


def _pack_bf16_pair(x):
    lo = lax.bitcast_convert_type(x[:, :PACKED_COLS].astype(BF16).astype(F32), U32)
    hi = lax.bitcast_convert_type(x[:, PACKED_COLS:].astype(BF16).astype(F32), U32)
    return hi | lax.shift_right_logical(lo, jnp.uint32(16))


def _unpack_bf16_pair(w):
    lo = lax.bitcast_convert_type(lax.shift_left(w, jnp.uint32(16)), F32)
    hi = lax.bitcast_convert_type(w & jnp.uint32(0xFFFF0000), F32)
    return lo, hi


def _proj_body(x_ref, w_ref, o_ref):
    o_ref[...] = _dot(x_ref[...].astype(BF16), w_ref[...]).astype(BF16)


def _in_proj(x2d, w_cat):
    return pl.pallas_call(
        _proj_body,
        grid=(PROJ_COLS // PROJ_TN, N_TOK // PROJ_TM),
        in_specs=[
            pl.BlockSpec((PROJ_TM, D_MODEL), lambda j, i: (i, 0)),
            pl.BlockSpec((D_MODEL, PROJ_TN), lambda j, i: (0, j)),
        ],
        out_specs=pl.BlockSpec((PROJ_TM, PROJ_TN), lambda j, i: (i, j)),
        out_shape=jax.ShapeDtypeStruct((N_TOK, PROJ_COLS), BF16),
        compiler_params=_cparams("parallel", "parallel"),
        name="in_proj",
    )(x2d, w_cat)


def _gla_body(q_ref, k_ref, v_ref, r_ref, low_ref, wg_ref, bg_ref, ng_ref, o_ref, st_ref, gk_ref):
    w_hi, w_lo = _split_bf16(wg_ref[...])
    low = low_ref[...]
    lin = _dot(low, w_hi) + _dot(low, w_lo) + bg_ref[...]
    gk_ref[...] = (jnp.minimum(lin, 0.0) - jnp.log1p(jnp.exp(-jnp.abs(lin)))) * (1.0 / GLA_NORMALIZER)
    st_ref[...] = jnp.zeros_like(st_ref)

    ri = lax.broadcasted_iota(I32, (GLA_GROUP, GLA_GROUP), 0)
    ci = lax.broadcasted_iota(I32, (GLA_GROUP, GLA_GROUP), 1)
    same_chunk = lax.shift_right_logical(ri, GLA_CHUNK.bit_length() - 1) == lax.shift_right_logical(
        ci, GLA_CHUNK.bit_length() - 1)
    causal = same_chunk & (ri >= ci)
    sums = jnp.concatenate([jnp.where(causal, 1.0, 0.0), jnp.where(same_chunk, 1.0, 0.0)], axis=0).astype(BF16)
    gain = ng_ref[...]

    def group(c, carry):
        rows = pl.ds(pl.multiple_of(c * GLA_GROUP, GLA_GROUP), GLA_GROUP)
        g_hi, g_lo = _split_bf16(gk_ref[rows, :])
        bb = _dot(sums, g_hi) + _dot(sums, g_lo)
        b = bb[0:GLA_GROUP]
        b_end = bb[GLA_GROUP:2 * GLA_GROUP]
        q = q_ref[rows, :].astype(F32) * (GLA_DK ** -0.5)
        k = k_ref[rows, :].astype(F32)
        v = v_ref[rows, :]
        q_e = (q * jnp.exp(b)).astype(BF16)
        k_e = (k * jnp.exp(-b)).astype(BF16)
        k_d = (k * jnp.exp(b_end - b)).astype(BF16)
        att = jnp.where(causal, _dot_nt(q_e, k_e), 0.0)
        o = _dot(att.astype(BF16), v)
        st = st_ref[...]
        inter = []
        for j in range(GLA_GROUP // GLA_CHUNK):
            cr = slice(j * GLA_CHUNK, (j + 1) * GLA_CHUNK)
            inter.append(_dot_nt(q_e[cr], st.astype(BF16)))
            st = st * jnp.exp(b_end[j * GLA_CHUNK:j * GLA_CHUNK + 1, :]) + _dot_tn(v[cr], k_d[cr])
        st_ref[...] = st
        o = o + jnp.concatenate(inter, axis=0)
        o = o * lax.rsqrt(jnp.mean(o * o, axis=-1, keepdims=True) + LN_EPS) * gain
        r = r_ref[rows, :].astype(F32)
        o_ref[rows, :] = (o * (r * _sigmoid(r))).astype(BF16)
        return carry

    lax.fori_loop(0, SEQ // GLA_GROUP, group, 0)


def _gla(proj, w_gk2_pad, b_gk, norm_g):
    kb, vb = GLA_DK, GLA_DV
    return pl.pallas_call(
        _gla_body,
        grid=(BATCH, GLA_HEADS),
        in_specs=[
            pl.BlockSpec((SEQ, kb), lambda b, h: (b, COL_GQ // kb + h)),
            pl.BlockSpec((SEQ, kb), lambda b, h: (b, COL_GK // kb + h)),
            pl.BlockSpec((SEQ, vb), lambda b, h: (b, COL_GV // vb + h)),
            pl.BlockSpec((SEQ, vb), lambda b, h: (b, COL_GR // vb + h)),
            pl.BlockSpec((SEQ, LANES), lambda b, h: (b, COL_LOW // LANES)),
            pl.BlockSpec((LANES, kb), lambda b, h: (0, h)),
            pl.BlockSpec((1, kb), lambda b, h: (0, h)),
            pl.BlockSpec((1, vb), lambda b, h: (0, 0)),
        ],
        out_specs=pl.BlockSpec((SEQ, vb), lambda b, h: (b, h)),
        out_shape=jax.ShapeDtypeStruct((N_TOK, GLA_VDIM), BF16),
        scratch_shapes=[pltpu.VMEM((vb, kb), F32), pltpu.VMEM((SEQ, kb), F32)],
        compiler_params=_cparams("parallel", "parallel"),
        name="gla",
    )(proj, proj, proj, proj, proj, w_gk2_pad, b_gk, norm_g)


def _moba_body(q_ref, k_ref, v_ref, c_ref, s_ref, o_ref, qs_ref, ks_ref, vt_ref, bias_ref,
               sc_a_ref, sc_b_ref, pr_a_ref, pr_b_ref):
    cos_t = c_ref[0]
    sin_t = s_ref[0]
    lane = lax.broadcasted_iota(I32, (SEQ, MOBA_DH), 1)
    half = ROT_DIM // 2

    def rope(x):
        partner = jnp.where(lane < half, pltpu.roll(x, MOBA_DH - half, 1), pltpu.roll(x, half, 1))
        return x * cos_t + partner * sin_t

    q = rope(q_ref[...].astype(F32))
    k = rope(k_ref[...].astype(F32))
    q_hi, q_lo = _split_bf16(q)
    qs_ref[...] = (q * (MOBA_DH ** -0.5 * LOG2_E)).astype(BF16)
    ks_ref[...] = k.astype(BF16)
    vt_ref[...] = v_ref[...].astype(F32).T.astype(BF16)

    k_mean = jnp.concatenate(
        [jnp.mean(k[j * MOBA_BLOCK:(j + 1) * MOBA_BLOCK], axis=0, keepdims=True) for j in range(N_KBLK)], axis=0)
    m_hi, m_lo = _split_bf16(k_mean)
    s_blk = _dot_nt(m_hi, q_hi) + _dot_nt(m_hi, q_lo) + _dot_nt(m_lo, q_hi)
    blk = lax.broadcasted_iota(I32, (N_KBLK, SEQ), 0)
    q_blk = lax.shift_right_logical(lax.broadcasted_iota(I32, (N_KBLK, SEQ), 1), MOBA_BLOCK.bit_length() - 1)
    past = blk < q_blk
    s_blk = jnp.where(past, s_blk, _NEG_INF)
    beaten = jnp.zeros((N_KBLK, SEQ), I32)
    for j in range(N_KBLK):
        row = s_blk[j:j + 1, :]
        beaten += jnp.where((row > s_blk) | ((row == s_blk) & (j < blk)), 1, 0)
    bias_ref[...] = jnp.where(past & (beaten < MOBA_TOPK), 0.0, _NEG_INF)

    kr = lax.broadcasted_iota(I32, (MOBA_BLOCK, MOBA_BLOCK), 0)
    qc = lax.broadcasted_iota(I32, (MOBA_BLOCK, MOBA_BLOCK), 1)
    own_bias = jnp.where(kr <= qc, 0.0, _NEG_INF)

    sc_bufs = (sc_a_ref, sc_b_ref)
    pr_bufs = (pr_a_ref, pr_b_ref)
    zeros_row = jnp.zeros((1, MOBA_BLOCK), F32)
    future_row = jnp.full((1, MOBA_BLOCK), _NEG_INF, F32)
    for pair in range(N_KBLK // 2):
        q_blocks = (2 * pair, 2 * pair + 1)
        cols = slice(q_blocks[0] * MOBA_BLOCK, (q_blocks[1] + 1) * MOBA_BLOCK)
        n_kblk = q_blocks[1] + 1
        n_keys = n_kblk * MOBA_BLOCK
        sc, pr = sc_bufs[pair % 2], pr_bufs[pair % 2]
        q_pair = qs_ref[cols, :]

        def query_bias(j):
            halves = []
            for qb in q_blocks:
                q_cols = slice(qb * MOBA_BLOCK, (qb + 1) * MOBA_BLOCK)
                halves.append(bias_ref[j:j + 1, q_cols] if j < qb else zeros_row if j == qb else future_row)
            return jnp.concatenate(halves, axis=1)

        biases = [query_bias(j) for j in range(n_kblk)]
        col_max = []
        for j in range(n_kblk):
            rows = slice(j * MOBA_BLOCK, (j + 1) * MOBA_BLOCK)
            s = _dot_nt(ks_ref[rows, :], q_pair)
            if j in q_blocks:
                h = q_blocks.index(j)
                own = s[:, h * MOBA_BLOCK:(h + 1) * MOBA_BLOCK] + own_bias
                s = jnp.concatenate([own, s[:, MOBA_BLOCK:]] if h == 0 else [s[:, :MOBA_BLOCK], own], axis=1)
            sc[rows, :] = s
            col_max.append(jnp.max(s, axis=0, keepdims=True) + biases[j])
        m = functools.reduce(jnp.maximum, col_max)
        denom = jnp.zeros((1, 2 * MOBA_BLOCK), F32)
        for j in range(n_kblk):
            rows = slice(j * MOBA_BLOCK, (j + 1) * MOBA_BLOCK)
            p = jnp.exp2(sc[rows, :] - (m - biases[j]))
            denom = denom + jnp.sum(p, axis=0, keepdims=True)
            pr[rows, :] = p.astype(BF16)
        o_t = _dot(vt_ref[:, 0:n_keys], pr[0:n_keys, :]) * (1.0 / denom)
        o_ref[cols, :] = o_t.T.astype(BF16)


def _moba(proj, cos_t, sin_t):
    dh = MOBA_DH
    return pl.pallas_call(
        _moba_body,
        grid=(BATCH, MOBA_HEADS),
        in_specs=[
            pl.BlockSpec((SEQ, dh), lambda b, h: (b, COL_MQ // dh + h)),
            pl.BlockSpec((SEQ, dh), lambda b, h: (b, COL_MK // dh + h)),
            pl.BlockSpec((SEQ, dh), lambda b, h: (b, COL_MV // dh + h)),
            pl.BlockSpec((1, SEQ, dh), lambda b, h: (b, 0, 0)),
            pl.BlockSpec((1, SEQ, dh), lambda b, h: (b, 0, 0)),
        ],
        out_specs=pl.BlockSpec((SEQ, dh), lambda b, h: (b, h)),
        out_shape=jax.ShapeDtypeStruct((N_TOK, MOBA_DIM), BF16),
        scratch_shapes=[
            pltpu.VMEM((SEQ, dh), BF16),
            pltpu.VMEM((SEQ, dh), BF16),
            pltpu.VMEM((dh, SEQ), BF16),
            pltpu.VMEM((N_KBLK, SEQ), F32),
            pltpu.VMEM((SEQ, 2 * MOBA_BLOCK), F32),
            pltpu.VMEM((SEQ, 2 * MOBA_BLOCK), F32),
            pltpu.VMEM((SEQ, 2 * MOBA_BLOCK), BF16),
            pltpu.VMEM((SEQ, 2 * MOBA_BLOCK), BF16),
        ],
        compiler_params=_cparams("parallel", "parallel"),
        name="moba",
    )(proj, proj, proj, cos_t, sin_t)


def _layer_norm(z, g, b):
    mu = jnp.mean(z, axis=-1, keepdims=True)
    zc = z - mu
    var = jnp.mean(zc * zc, axis=-1, keepdims=True)
    return zc * lax.rsqrt(var + LN_EPS) * g + b


def _mix_body(gla_ref, moba_ref, ga_ref, gb_ref, x_ref, wgo_ref, wmo_ref, wo_ref, g_ref, b_ref,
              h_ref, hb_ref, hp_ref):
    y_gla = _dot(gla_ref[...], wgo_ref[...])
    y_moba = _dot(moba_ref[...], wmo_ref[...])
    merged = _sigmoid(ga_ref[...].astype(F32)) * y_gla + _sigmoid(gb_ref[...].astype(F32)) * y_moba
    mix = _dot(merged.astype(BF16), wo_ref[...])
    h = _layer_norm(ALPHA * x_ref[...] + mix, g_ref[...], b_ref[...])
    h_ref[...] = h
    hb_ref[...] = h.astype(BF16)
    hp_ref[...] = _pack_bf16_pair(h)


def _mix(gla_out, moba_out, proj, x2d, w_gla_o, w_moba_o, w_out, ln_g, ln_b):
    d = D_MODEL
    row = lambda i: (i, 0)
    full = lambda i: (0, 0)
    return pl.pallas_call(
        _mix_body,
        grid=(N_TOK // MIX_TM,),
        in_specs=[
            pl.BlockSpec((MIX_TM, d), row),
            pl.BlockSpec((MIX_TM, d), row),
            pl.BlockSpec((MIX_TM, d), lambda i: (i, COL_GA // d)),
            pl.BlockSpec((MIX_TM, d), lambda i: (i, COL_GB // d)),
            pl.BlockSpec((MIX_TM, d), row),
            pl.BlockSpec((d, d), full),
            pl.BlockSpec((d, d), full),
            pl.BlockSpec((d, d), full),
            pl.BlockSpec((1, d), full),
            pl.BlockSpec((1, d), full),
        ],
        out_specs=[
            pl.BlockSpec((MIX_TM, d), row),
            pl.BlockSpec((MIX_TM, d), row),
            pl.BlockSpec((MIX_TM, PACKED_COLS), row),
        ],
        out_shape=[
            jax.ShapeDtypeStruct((N_TOK, d), F32),
            jax.ShapeDtypeStruct((N_TOK, d), BF16),
            jax.ShapeDtypeStruct((N_TOK, PACKED_COLS), U32),
        ],
        compiler_params=_cparams("parallel"),
        name="mix_ln1",
    )(gla_out, moba_out, proj, proj, x2d, w_gla_o, w_moba_o, w_out, ln_g, ln_b)


def _route_body(h_ref, wr_ref, br_ref, e_ref, w_ref, rk_ref, cnt_ref, carry_ref):
    tm = ROUTE_TM

    @pl.when(pl.program_id(0) == 0)
    def _():
        carry_ref[...] = jnp.zeros_like(carry_ref)

    scores = _sigmoid(_dot_nt(wr_ref[...], h_ref[...]))
    biased = scores + br_ref[...]
    row = lax.broadcasted_iota(I32, (N_EXPERTS, tm), 0).astype(F32)
    row_g = lax.broadcasted_iota(I32, (GROUP_SIZE, tm), 0).astype(F32)

    g_scores = []
    for g in range(N_GROUPS):
        grp = biased[g * GROUP_SIZE:(g + 1) * GROUP_SIZE]
        m1 = jnp.max(grp, axis=0, keepdims=True)
        first = jnp.min(jnp.where(grp == m1, row_g, float(GROUP_SIZE)), axis=0, keepdims=True)
        m2 = jnp.max(jnp.where(row_g == first, _NEG_INF, grp), axis=0, keepdims=True)
        g_scores.append(m1 + m2)
    g_score = jnp.concatenate(g_scores, axis=0)
    g_row = lax.broadcasted_iota(I32, (N_GROUPS, tm), 0)
    g_beaten = jnp.zeros((N_GROUPS, tm), I32)
    for g in range(N_GROUPS):
        r = g_score[g:g + 1, :]
        g_beaten += jnp.where((r > g_score) | ((r == g_score) & (g < g_row)), 1, 0)
    g_keep = g_beaten < TOPK_GROUPS
    masked = jnp.concatenate(
        [jnp.where(g_keep[g:g + 1, :], biased[g * GROUP_SIZE:(g + 1) * GROUP_SIZE], _NEG_INF)
         for g in range(N_GROUPS)], axis=0)

    onehot = jnp.zeros((N_EXPERTS, tm), F32)
    picks, pick_scores = [], []
    for _ in range(TOPK_EXPERTS):
        m = jnp.max(masked, axis=0, keepdims=True)
        idx = jnp.min(jnp.where(masked == m, row, float(N_EXPERTS)), axis=0, keepdims=True)
        hit = row == idx
        picks.append(idx)
        pick_scores.append(jnp.sum(jnp.where(hit, scores, 0.0), axis=0, keepdims=True))
        onehot = onehot + jnp.where(hit, 1.0, 0.0)
        masked = jnp.where(hit, _NEG_INF, masked)
    sel = jnp.concatenate(pick_scores, axis=0)
    e_ref[...] = jnp.concatenate(picks, axis=0).astype(I32)
    w_ref[...] = sel / jnp.sum(sel, axis=0, keepdims=True) * ROUTED_SCALE

    t_r = lax.broadcasted_iota(I32, (tm, tm), 0)
    t_c = lax.broadcasted_iota(I32, (tm, tm), 1)
    earlier = jnp.where(t_r < t_c, 1.0, 0.0).astype(BF16)
    seen = _dot(onehot.astype(BF16), earlier) + carry_ref[...]
    rk_ref[...] = jnp.concatenate(
        [jnp.sum(jnp.where(row == idx, seen, 0.0), axis=0, keepdims=True) for idx in picks], axis=0).astype(I32)
    carry_ref[...] += jnp.sum(onehot, axis=1, keepdims=True)
    cnt_ref[...] = carry_ref[...]


def _route(h_bf, w_router_t, b_router_col):
    k = TOPK_EXPERTS
    tok = lambda i: (0, i)
    return pl.pallas_call(
        _route_body,
        grid=(N_TOK // ROUTE_TM,),
        in_specs=[
            pl.BlockSpec((ROUTE_TM, D_MODEL), lambda i: (i, 0)),
            pl.BlockSpec((N_EXPERTS, D_MODEL), lambda i: (0, 0)),
            pl.BlockSpec((N_EXPERTS, 1), lambda i: (0, 0)),
        ],
        out_specs=[
            pl.BlockSpec((k, ROUTE_TM), tok),
            pl.BlockSpec((k, ROUTE_TM), tok),
            pl.BlockSpec((k, ROUTE_TM), tok),
            pl.BlockSpec((N_EXPERTS, 1), lambda i: (0, 0)),
        ],
        out_shape=[
            jax.ShapeDtypeStruct((k, N_TOK), I32),
            jax.ShapeDtypeStruct((k, N_TOK), F32),
            jax.ShapeDtypeStruct((k, N_TOK), I32),
            jax.ShapeDtypeStruct((N_EXPERTS, 1), F32),
        ],
        scratch_shapes=[pltpu.VMEM((N_EXPERTS, 1), F32)],
        compiler_params=_cparams("arbitrary"),
        name="route",
    )(h_bf, w_router_t, b_router_col)


def _dest_body(cnt_ref, e_ref, rk_ref, d_ref, te_ref, nu_ref, lt_ref):
    cnt = cnt_ref[...]
    tiles = jnp.floor((cnt + (ROW_TILE - 1)) * (1.0 / ROW_TILE))
    er = lax.broadcasted_iota(I32, (N_EXPERTS, N_EXPERTS), 0)
    ec = lax.broadcasted_iota(I32, (N_EXPERTS, N_EXPERTS), 1)
    before = jnp.where(ec < er, 1.0, 0.0).astype(BF16)
    tiles_b = jnp.broadcast_to(tiles, (N_EXPERTS, LANES)).astype(BF16)
    t_start = _dot(before, tiles_b)[:, 0:1]
    t_end = t_start + tiles
    p_start = t_start * float(ROW_TILE)
    lt_ref[...] = jnp.where(tiles > 0.0, (t_end - 1.0) * float(ROW_TILE), -1.0).astype(I32)

    row = lax.broadcasted_iota(I32, (N_EXPERTS, DEST_TM), 0)
    d_ref[...] = jnp.concatenate(
        [jnp.sum(jnp.where(row == e_ref[k:k + 1, :], p_start, 0.0), axis=0, keepdims=True)
         for k in range(TOPK_EXPERTS)], axis=0).astype(I32) + rk_ref[...]

    tile_id = lax.broadcasted_iota(I32, (N_EXPERTS, TILE_TABLE), 1).astype(F32)
    owner = jnp.sum(jnp.where(t_end <= tile_id, 1, 0), axis=0, keepdims=True)
    te_ref[...] = jnp.minimum(owner, N_EXPERTS - 1)
    nu_ref[...] = jnp.broadcast_to(t_end[N_EXPERTS - 1:N_EXPERTS, :], (1, LANES)).astype(I32)


def _dest(cnt, e_t, rk_t):
    k = TOPK_EXPERTS
    tok = lambda i: (0, i)
    return pl.pallas_call(
        _dest_body,
        grid=(N_TOK // DEST_TM,),
        in_specs=[
            pl.BlockSpec((N_EXPERTS, 1), lambda i: (0, 0)),
            pl.BlockSpec((k, DEST_TM), tok),
            pl.BlockSpec((k, DEST_TM), tok),
        ],
        out_specs=[
            pl.BlockSpec((k, DEST_TM), tok),
            pl.BlockSpec((1, TILE_TABLE), lambda i: (0, 0)),
            pl.BlockSpec((1, LANES), lambda i: (0, 0)),
            pl.BlockSpec((N_EXPERTS, 1), lambda i: (0, 0)),
        ],
        out_shape=[
            jax.ShapeDtypeStruct((k, N_TOK), I32),
            jax.ShapeDtypeStruct((1, TILE_TABLE), I32),
            jax.ShapeDtypeStruct((1, LANES), I32),
            jax.ShapeDtypeStruct((N_EXPERTS, 1), I32),
        ],
        compiler_params=_cparams("arbitrary"),
        name="dest",
    )(cnt, e_t, rk_t)


def _row_copy(src, dst, sem):
    return pltpu.make_async_copy(src, dst, sem)


def _dispatch_body(lt_ref, d_ref, h_ref, xs_ref, zero_ref, sem):
    @pl.when(pl.program_id(0) == 0)
    def _():
        zero_ref[...] = jnp.zeros_like(zero_ref)

        def tile_copy(e):
            r0 = pl.multiple_of(jnp.maximum(lt_ref[e], 0), ROW_TILE)
            return pltpu.make_async_copy(zero_ref, xs_ref.at[pl.ds(r0, ROW_TILE), :], sem)

        def z_start(e, carry):
            @pl.when(lt_ref[e] >= 0)
            def _():
                tile_copy(e).start()
            return carry

        def z_wait(e, carry):
            @pl.when(lt_ref[e] >= 0)
            def _():
                tile_copy(e).wait()
            return carry

        lax.fori_loop(0, N_EXPERTS, z_start, 0)
        lax.fori_loop(0, N_EXPERTS, z_wait, 0)

    def row(t, k):
        return _row_copy(h_ref.at[pl.ds(t, 1), :], xs_ref.at[pl.ds(d_ref[k, t], 1), :], sem)

    def issue(t, carry):
        for k in range(TOPK_EXPERTS):
            row(t, k).start(priority=k % 2)
        return carry

    def drain(t, carry):
        for k in range(TOPK_EXPERTS):
            row(t, k).wait()
        return carry

    lax.fori_loop(0, DISP_TM, issue, 0)
    lax.fori_loop(0, DISP_TM, drain, 0)


def _dispatch(last_tile_row, dest_t, h1):
    grid_spec = pltpu.PrefetchScalarGridSpec(
        num_scalar_prefetch=1,
        grid=(N_TOK // DISP_TM,),
        in_specs=[
            pl.BlockSpec((TOPK_EXPERTS, DISP_TM), lambda i, lt: (0, i), memory_space=pltpu.SMEM),
            pl.BlockSpec((DISP_TM, PACKED_COLS), lambda i, lt: (i, 0)),
        ],
        out_specs=pl.BlockSpec(memory_space=pl.ANY),
        scratch_shapes=[pltpu.VMEM((ROW_TILE, PACKED_COLS), U32), pltpu.SemaphoreType.DMA],
    )
    return pl.pallas_call(
        _dispatch_body,
        grid_spec=grid_spec,
        out_shape=jax.ShapeDtypeStruct((N_ROWS, PACKED_COLS), U32),
        compiler_params=_cparams("arbitrary"),
        name="dispatch",
    )(last_tile_row, dest_t, h1)


def _row_destinations(dest_flat):
    info = plsc.get_sparse_core_info()
    n_cores, lanes = info.num_cores, info.num_lanes
    n_workers = n_cores * info.num_subcores
    rows_per_worker = -(-N_ROWS // (n_workers * ROW_TILE)) * ROW_TILE
    assert rows_per_worker % lanes == 0
    mesh = plsc.VectorSubcoreMesh(core_axis_name="c", subcore_axis_name="s")

    @functools.partial(
        pl.kernel, mesh=mesh, out_type=jax.ShapeDtypeStruct((n_workers * rows_per_worker,), I32),
        scratch_types=[pltpu.VMEM((SLOT_CHUNK,), I32), pltpu.VMEM((rows_per_worker,), I32)],
        compiler_params=pltpu.CompilerParams(needs_layout_passes=False), name="row_destinations")
    def invert(dest_hbm, out_hbm, dest_v, table_v):
        first_row = (lax.axis_index("s") * n_cores + lax.axis_index("c")) * rows_per_worker
        lane = lax.iota(I32, lanes)
        tile_shift = ROW_TILE.bit_length() - 1

        def spare(i, carry):
            row = first_row + i * lanes + lane
            buf = lax.rem(lax.shift_right_logical(row, tile_shift), EXPERT_BUFS)
            table_v[pl.ds(i * lanes, lanes)] = TOPK_EXPERTS * N_TOK + buf * ROW_TILE + (row & (ROW_TILE - 1))
            return carry

        lax.fori_loop(0, rows_per_worker // lanes, spare, 0)

        def chunk(c, carry):
            pltpu.sync_copy(dest_hbm.at[pl.ds(c * SLOT_CHUNK, SLOT_CHUNK)], dest_v)

            def vec(i, inner):
                local = dest_v[pl.ds(i * lanes, lanes)] - first_row
                mine = (local >= 0) & (local < rows_per_worker)
                plsc.store_scatter(table_v, [jnp.where(mine, local, 0)], lane + (c * SLOT_CHUNK + i * lanes),
                                   mask=mine)
                return inner

            lax.fori_loop(0, SLOT_CHUNK // lanes, vec, 0)
            return carry

        lax.fori_loop(0, N_ASSIGN // SLOT_CHUNK, chunk, 0)
        pltpu.sync_copy(table_v, out_hbm.at[pl.ds(first_row, rows_per_worker)])

    return invert(dest_flat)


def _experts_body(te_ref, nu_ref, slot_ref, xs_ref, wg_ref, wu_ref, wd_ref, yk_ref, wg_b, wu_b, wd_b, y_buf, sem):
    i = pl.program_id(0)
    n_used = nu_ref[0]

    def wait_tile(b):
        pltpu.make_async_copy(y_buf.at[b], yk_ref.at[pl.ds(0, ROW_TILE), :], sem.at[b]).wait()

    def send_tile(b):
        for r in range(ROW_TILE):
            dst = slot_ref[0, 0, r]
            _row_copy(y_buf.at[b, pl.ds(r, 1), :], yk_ref.at[pl.ds(dst, 1), :], sem.at[b]).start(priority=r % 2)

    @pl.when(i == 0)
    def _():
        y_buf[EXPERT_BUFS - 1] = jnp.zeros((ROW_TILE, PACKED_COLS), U32)

    @pl.when((i >= 2) & (i <= n_used))
    def _():
        wait_tile(lax.rem(i, EXPERT_BUFS))

    @pl.when(i < n_used)
    def _():
        prev = te_ref[jnp.maximum(i - 1, 0)]

        @pl.when((i == 0) | (te_ref[i] != prev))
        def _():
            wg_b[...] = wg_ref[0].astype(BF16)
            wu_b[...] = wu_ref[0].astype(BF16)
            wd_b[...] = wd_ref[0].astype(BF16)

        send_tile(lax.rem(i + EXPERT_BUFS - 1, EXPERT_BUFS))
        x_lo, x_hi = _unpack_bf16_pair(xs_ref[...])
        x = jnp.concatenate([x_lo.astype(BF16), x_hi.astype(BF16)], axis=1)
        g = _dot(x, wg_b[...])
        u = _dot(x, wu_b[...])
        h = (g * _sigmoid(g)) * u
        y_buf[lax.rem(i, EXPERT_BUFS)] = _pack_bf16_pair(_dot(h.astype(BF16), wd_b[...]))

    @pl.when(i == n_used)
    def _():
        send_tile(lax.rem(i + EXPERT_BUFS - 1, EXPERT_BUFS))
        wait_tile(lax.rem(i + EXPERT_BUFS - 2, EXPERT_BUFS))
        wait_tile(lax.rem(i + EXPERT_BUFS - 1, EXPERT_BUFS))


def _experts(tile_expert, n_used, row_dst, xs, w_gate, w_up, w_down):
    def tile(i, te, nu):
        return (jnp.minimum(i, nu[0] - 1), 0)

    n_table_tiles = row_dst.shape[0] // ROW_TILE
    placeholder = n_table_tiles - 1
    assert placeholder * ROW_TILE >= N_ROWS and placeholder % EXPERT_BUFS == EXPERT_BUFS - 1

    def prev_tile(i, te, nu):
        return (jnp.where(i == 0, placeholder, jnp.minimum(i, nu[0]) - 1), 0, 0)

    def expert(i, te, nu):
        return (te[jnp.minimum(i, nu[0] - 1)], 0, 0)

    grid_spec = pltpu.PrefetchScalarGridSpec(
        num_scalar_prefetch=2,
        grid=(N_ROW_TILES + 1,),
        in_specs=[
            pl.BlockSpec((1, 1, ROW_TILE), prev_tile, memory_space=pltpu.SMEM),
            pl.BlockSpec((ROW_TILE, PACKED_COLS), tile),
            pl.BlockSpec((1, D_MODEL, D_EXPERT), expert),
            pl.BlockSpec((1, D_MODEL, D_EXPERT), expert),
            pl.BlockSpec((1, D_EXPERT, D_MODEL), expert),
        ],
        out_specs=pl.BlockSpec(memory_space=pl.ANY),
        scratch_shapes=[
            pltpu.VMEM((D_MODEL, D_EXPERT), BF16),
            pltpu.VMEM((D_MODEL, D_EXPERT), BF16),
            pltpu.VMEM((D_EXPERT, D_MODEL), BF16),
            pltpu.VMEM((EXPERT_BUFS, ROW_TILE, PACKED_COLS), U32),
            pltpu.SemaphoreType.DMA((EXPERT_BUFS,)),
        ],
    )
    return pl.pallas_call(
        _experts_body,
        grid_spec=grid_spec,
        out_shape=jax.ShapeDtypeStruct((YK_ROWS, PACKED_COLS), U32),
        compiler_params=_cparams("arbitrary"),
        name="experts",
    )(tile_expert, n_used, row_dst.reshape(n_table_tiles, 1, ROW_TILE), xs, w_gate, w_up, w_down)


def _final_body(w_ref, yk_ref, h_ref, hb_ref, p_ref, wsg_ref, wsu_ref, wsd_ref, wpl_ref, wpg_ref,
                g_ref, b_ref, o_ref):
    hb = hb_ref[...]
    sg = _dot(hb, wsg_ref[...])
    shared = _dot(((sg * _sigmoid(sg)) * _dot(hb, wsu_ref[...])).astype(BF16), wsd_ref[...])
    ple = _sigmoid(_dot(hb, wpg_ref[...])) * _dot(p_ref[...].astype(BF16), wpl_ref[...])

    w_col = w_ref[...].T
    y_lo, y_hi = _unpack_bf16_pair(yk_ref[0])
    r_lo, r_hi = y_lo * w_col[:, 0:1], y_hi * w_col[:, 0:1]
    for k in range(1, TOPK_EXPERTS):
        y_lo, y_hi = _unpack_bf16_pair(yk_ref[k])
        r_lo, r_hi = r_lo + y_lo * w_col[:, k:k + 1], r_hi + y_hi * w_col[:, k:k + 1]
    routed = jnp.concatenate([r_lo, r_hi], axis=1)
    o_ref[...] = _layer_norm(ALPHA * h_ref[...] + (routed + shared) + ple, g_ref[...], b_ref[...])


def _final(w_t, yk, h1, h_bf, p2d, w_s_gate, w_s_up, w_s_down, w_ple, w_ple_gate, ln_g, ln_b):
    d = D_MODEL
    full = lambda i: (0, 0)
    return pl.pallas_call(
        _final_body,
        grid=(N_TOK // FINAL_TM,),
        in_specs=[
            pl.BlockSpec((TOPK_EXPERTS, FINAL_TM), lambda i: (0, i)),
            pl.BlockSpec((TOPK_EXPERTS, FINAL_TM, PACKED_COLS), lambda i: (0, i, 0)),
            pl.BlockSpec((FINAL_TM, d), lambda i: (i, 0)),
            pl.BlockSpec((FINAL_TM, d), lambda i: (i, 0)),
            pl.BlockSpec((FINAL_TM, PLE_DIM), lambda i: (i, 0)),
            pl.BlockSpec((d, D_SHARED), full),
            pl.BlockSpec((d, D_SHARED), full),
            pl.BlockSpec((D_SHARED, d), full),
            pl.BlockSpec((PLE_DIM, d), full),
            pl.BlockSpec((d, d), full),
            pl.BlockSpec((1, d), full),
            pl.BlockSpec((1, d), full),
        ],
        out_specs=pl.BlockSpec((FINAL_TM, d), lambda i: (i, 0)),
        out_shape=jax.ShapeDtypeStruct((N_TOK, d), F32),
        compiler_params=_cparams("parallel"),
        name="combine_ln2",
    )(w_t, yk.reshape(TOPK_EXPERTS + 1, N_TOK, PACKED_COLS), h1, h_bf, p2d, w_s_gate, w_s_up, w_s_down,
      w_ple, w_ple_gate, ln_g, ln_b)


def _rope_tables(positions):
    half = ROT_DIM // 2
    inv = ROPE_THETA ** (-jnp.arange(0, ROT_DIM, 2, dtype=F32) / ROT_DIM)
    ang = positions.astype(F32)[..., None] * inv
    cos, sin = jnp.cos(ang), jnp.sin(ang)
    rest = MOBA_DH - ROT_DIM
    ones = jnp.ones(cos.shape[:-1] + (rest,), F32)
    zeros = jnp.zeros(cos.shape[:-1] + (rest,), F32)
    return (jnp.concatenate([cos, cos, ones], axis=-1), jnp.concatenate([-sin, sin, zeros], axis=-1))


def _layer(h2d, p2d, cos_t, sin_t, w_in, w_gk2, b_gk, norm_g, w_gla_o, w_moba_o, w_out, ln1_g, ln1_b,
           w_router, b_router, w_e_gate, w_e_up, w_e_down, w_s_gate, w_s_up, w_s_down, w_ple, w_ple_gate,
           ln2_g, ln2_b):
    low_end = LOW_SRC + GLA_RANK
    w_cat = jnp.concatenate(
        [w_in[:, :LOW_SRC], w_in[:, low_end:], w_in[:, LOW_SRC:low_end],
         jnp.zeros((D_MODEL, LANES - GLA_RANK), w_in.dtype)], axis=1).astype(BF16)
    w_gk2_pad = jnp.concatenate([w_gk2, jnp.zeros((LANES - GLA_RANK, GLA_KDIM), w_gk2.dtype)], axis=0)

    proj = _in_proj(h2d, w_cat)
    gla_out = _gla(proj, w_gk2_pad, b_gk[None, :], norm_g[None, :])
    moba_out = _moba(proj, cos_t, sin_t)
    h1, h_bf, h_packed = _mix(gla_out, moba_out, proj, h2d, w_gla_o.astype(BF16), w_moba_o.astype(BF16),
                              w_out.astype(BF16), ln1_g[None, :], ln1_b[None, :])
    e_t, w_t, rk_t, cnt = _route(h_bf, w_router.T.astype(BF16), b_router[:, None])
    dest_t, tile_expert, n_used, last_tile_row = _dest(cnt, e_t, rk_t)
    xs = _dispatch(last_tile_row.reshape(N_EXPERTS), dest_t, h_packed)
    row_dst = _row_destinations(dest_t.reshape(N_ASSIGN))
    yk = _experts(tile_expert.reshape(TILE_TABLE), n_used[0, 0:1], row_dst, xs, w_e_gate, w_e_up, w_e_down)
    return _final(w_t, yk, h1, h_bf, p2d, w_s_gate.astype(BF16), w_s_up.astype(BF16),
                  w_s_down.astype(BF16), w_ple.astype(BF16), w_ple_gate.astype(BF16),
                  ln2_g[None, :], ln2_b[None, :])


def kernel(x, p, positions, w_in, w_gla_gk2, b_gla_gk, gla_norm_g, w_gla_o, w_moba_o, w_out, ln1_g, ln1_b,
           w_router, b_router, w_e_gate, w_e_up, w_e_down, w_s_gate, w_s_up, w_s_down, w_ple, w_ple_gate,
           ln2_g, ln2_b):
    cos_t, sin_t = _rope_tables(positions)
    h = x.reshape(N_TOK, D_MODEL)
    for i in range(DEPTH):
        h = _layer(h, p[i].reshape(N_TOK, PLE_DIM), cos_t, sin_t, w_in[i], w_gla_gk2[i], b_gla_gk[i],
                   gla_norm_g[i], w_gla_o[i], w_moba_o[i], w_out[i], ln1_g[i], ln1_b[i], w_router[i],
                   b_router[i], w_e_gate[i], w_e_up[i], w_e_down[i], w_s_gate[i], w_s_up[i], w_s_down[i],
                   w_ple[i], w_ple_gate[i], ln2_g[i], ln2_b[i])
    return h.reshape(BATCH, SEQ, D_MODEL)
```

```python
import functools

import jax
import jax.numpy as jnp
from jax import lax
from jax.experimental import pallas as pl
from jax.experimental.pallas import tpu as pltpu
from jax.experimental.pallas import tpu_sc as plsc

F32 = jnp.float32
BF16 = jnp.bfloat16
I32 = jnp.int32
U32 = jnp.uint32

LANES = 128
SUBLANES = 8
VMEM_LIMIT_BYTES = 48 * 1024 * 1024

D_MODEL = 1024
BATCH = 8
SEQ = 2048
N_TOK = BATCH * SEQ
GLA_HEADS = 4
GLA_DK = 128
GLA_DV = 256
GLA_RANK = 16
GLA_NORMALIZER = 16.0
GLA_CHUNK = 64
GLA_GROUP = 4 * GLA_CHUNK
MOBA_HEADS = 8
MOBA_DH = 128
MOBA_BLOCK = 256
MOBA_TOPK = 3
ROT_DIM = 32
ROPE_THETA = 500000.0
N_EXPERTS = 256
TOPK_EXPERTS = 8
N_GROUPS = 8
GROUP_SIZE = N_EXPERTS // N_GROUPS
TOPK_GROUPS = 4
D_EXPERT = 256
D_SHARED = 256
ROUTED_SCALE = 2.5
PLE_DIM = 256
LN_EPS = 1e-5
DEPTH = 1
ALPHA = (2.0 * DEPTH) ** 0.25
GLA_KDIM = GLA_HEADS * GLA_DK
GLA_VDIM = GLA_HEADS * GLA_DV
MOBA_DIM = MOBA_HEADS * MOBA_DH
N_KBLK = SEQ // MOBA_BLOCK

COL_GQ = 0
COL_GK = COL_GQ + GLA_KDIM
COL_GV = COL_GK + GLA_KDIM
COL_GR = COL_GV + GLA_VDIM
COL_MQ = COL_GR + GLA_VDIM
COL_MK = COL_MQ + MOBA_DIM
COL_MV = COL_MK + MOBA_DIM
COL_GA = COL_MV + MOBA_DIM
COL_GB = COL_GA + D_MODEL
COL_LOW = COL_GB + D_MODEL
PROJ_COLS = COL_LOW + LANES
LOW_SRC = 2 * GLA_KDIM + 2 * GLA_VDIM

ROW_TILE = 256
PACKED_COLS = D_MODEL // 2
PACKED_CHUNKS = PACKED_COLS // LANES
EXPERT_BUFS = 3
YK_ROWS = (TOPK_EXPERTS + 1) * N_TOK
SLOT_CHUNK = 4096
assert EXPERT_BUFS * ROW_TILE <= N_TOK
N_ASSIGN = N_TOK * TOPK_EXPERTS
N_ROW_TILES = (N_ASSIGN + N_EXPERTS * (ROW_TILE - 1) + ROW_TILE - 1) // ROW_TILE
N_ROWS = N_ROW_TILES * ROW_TILE
TILE_TABLE = -(-N_ROW_TILES // LANES) * LANES

PROJ_TM = 512
PROJ_TN = PROJ_COLS // 5
MIX_TM = 512
ROUTE_TM = 512
DEST_TM = 2048
DISP_TM = 256
DISPATCH_UNROLL = 4
FINAL_TM = 256

_NEG_INF = float("-inf")
LOG2_E = 1.4426950408889634


def _cparams(*sem):
    return pltpu.CompilerParams(dimension_semantics=sem, vmem_limit_bytes=VMEM_LIMIT_BYTES)


def _dot(a, b):
    return jnp.dot(a, b, preferred_element_type=F32)


def _dot_nt(a, b):
    return lax.dot_general(a, b, (((1,), (1,)), ((), ())), preferred_element_type=F32)


def _dot_tn(a, b):
    return lax.dot_general(a, b, (((0,), (0,)), ((), ())), preferred_element_type=F32)


def _split_bf16(x):
    hi = x.astype(BF16)
    lo = (x - hi.astype(F32)).astype(BF16)
    return hi, lo


def _sigmoid(x):
    return 1.0 / (1.0 + jnp.exp(-x))


def _pack_bf16_pair(x):
    lo = lax.bitcast_convert_type(x[:, :PACKED_COLS].astype(BF16).astype(F32), U32)
    hi = lax.bitcast_convert_type(x[:, PACKED_COLS:].astype(BF16).astype(F32), U32)
    return hi | lax.shift_right_logical(lo, jnp.uint32(16))


def _unpack_bf16_pair(w):
    lo = lax.bitcast_convert_type(lax.shift_left(w, jnp.uint32(16)), F32)
    hi = lax.bitcast_convert_type(w & jnp.uint32(0xFFFF0000), F32)
    return lo, hi


def _proj_body(x_ref, w_ref, o_ref):
    o_ref[...] = _dot(x_ref[...].astype(BF16), w_ref[...]).astype(BF16)


def _in_proj(x2d, w_cat):
    return pl.pallas_call(
        _proj_body,
        grid=(PROJ_COLS // PROJ_TN, N_TOK // PROJ_TM),
        in_specs=[
            pl.BlockSpec((PROJ_TM, D_MODEL), lambda j, i: (i, 0)),
            pl.BlockSpec((D_MODEL, PROJ_TN), lambda j, i: (0, j)),
        ],
        out_specs=pl.BlockSpec((PROJ_TM, PROJ_TN), lambda j, i: (i, j)),
        out_shape=jax.ShapeDtypeStruct((N_TOK, PROJ_COLS), BF16),
        compiler_params=_cparams("parallel", "parallel"),
        name="in_proj",
    )(x2d, w_cat)


def _gla_body(q_ref, k_ref, v_ref, r_ref, low_ref, wg_ref, bg_ref, ng_ref, o_ref, st_ref, gk_ref):
    w_hi, w_lo = _split_bf16(wg_ref[...])
    low = low_ref[...]
    lin = _dot(low, w_hi) + _dot(low, w_lo) + bg_ref[...]
    gk_ref[...] = (jnp.minimum(lin, 0.0) - jnp.log1p(jnp.exp(-jnp.abs(lin)))) * (1.0 / GLA_NORMALIZER)
    st_ref[...] = jnp.zeros_like(st_ref)

    ri = lax.broadcasted_iota(I32, (GLA_GROUP, GLA_GROUP), 0)
    ci = lax.broadcasted_iota(I32, (GLA_GROUP, GLA_GROUP), 1)
    same_chunk = lax.shift_right_logical(ri, GLA_CHUNK.bit_length() - 1) == lax.shift_right_logical(
        ci, GLA_CHUNK.bit_length() - 1)
    causal = same_chunk & (ri >= ci)
    sums = jnp.concatenate([jnp.where(causal, 1.0, 0.0), jnp.where(same_chunk, 1.0, 0.0)], axis=0).astype(BF16)
    gain = ng_ref[...]

    def group(c, carry):
        rows = pl.ds(pl.multiple_of(c * GLA_GROUP, GLA_GROUP), GLA_GROUP)
        g_hi, g_lo = _split_bf16(gk_ref[rows, :])
        bb = _dot(sums, g_hi) + _dot(sums, g_lo)
        b = bb[0:GLA_GROUP]
        b_end = bb[GLA_GROUP:2 * GLA_GROUP]
        q = q_ref[rows, :].astype(F32) * (GLA_DK ** -0.5)
        k = k_ref[rows, :].astype(F32)
        v = v_ref[rows, :]
        q_e = (q * jnp.exp(b)).astype(BF16)
        k_e = (k * jnp.exp(-b)).astype(BF16)
        k_d = (k * jnp.exp(b_end - b)).astype(BF16)
        att = jnp.where(causal, _dot_nt(q_e, k_e), 0.0)
        o = _dot(att.astype(BF16), v)
        st = st_ref[...]
        inter = []
        for j in range(GLA_GROUP // GLA_CHUNK):
            cr = slice(j * GLA_CHUNK, (j + 1) * GLA_CHUNK)
            inter.append(_dot_nt(q_e[cr], st.astype(BF16)))
            st = st * jnp.exp(b_end[j * GLA_CHUNK:j * GLA_CHUNK + 1, :]) + _dot_tn(v[cr], k_d[cr])
        st_ref[...] = st
        o = o + jnp.concatenate(inter, axis=0)
        o = o * lax.rsqrt(jnp.mean(o * o, axis=-1, keepdims=True) + LN_EPS) * gain
        r = r_ref[rows, :].astype(F32)
        o_ref[rows, :] = (o * (r * _sigmoid(r))).astype(BF16)
        return carry

    lax.fori_loop(0, SEQ // GLA_GROUP, group, 0)


def _gla(proj, w_gk2_pad, b_gk, norm_g):
    kb, vb = GLA_DK, GLA_DV
    return pl.pallas_call(
        _gla_body,
        grid=(BATCH, GLA_HEADS),
        in_specs=[
            pl.BlockSpec((SEQ, kb), lambda b, h: (b, COL_GQ // kb + h)),
            pl.BlockSpec((SEQ, kb), lambda b, h: (b, COL_GK // kb + h)),
            pl.BlockSpec((SEQ, vb), lambda b, h: (b, COL_GV // vb + h)),
            pl.BlockSpec((SEQ, vb), lambda b, h: (b, COL_GR // vb + h)),
            pl.BlockSpec((SEQ, LANES), lambda b, h: (b, COL_LOW // LANES)),
            pl.BlockSpec((LANES, kb), lambda b, h: (0, h)),
            pl.BlockSpec((1, kb), lambda b, h: (0, h)),
            pl.BlockSpec((1, vb), lambda b, h: (0, 0)),
        ],
        out_specs=pl.BlockSpec((SEQ, vb), lambda b, h: (b, h)),
        out_shape=jax.ShapeDtypeStruct((N_TOK, GLA_VDIM), BF16),
        scratch_shapes=[pltpu.VMEM((vb, kb), F32), pltpu.VMEM((SEQ, kb), F32)],
        compiler_params=_cparams("parallel", "parallel"),
        name="gla",
    )(proj, proj, proj, proj, proj, w_gk2_pad, b_gk, norm_g)


def _moba_body(q_ref, k_ref, v_ref, c_ref, s_ref, o_ref, qs_ref, ks_ref, vt_ref, bias_ref,
               sc_a_ref, sc_b_ref, pr_a_ref, pr_b_ref):
    cos_t = c_ref[0]
    sin_t = s_ref[0]
    lane = lax.broadcasted_iota(I32, (SEQ, MOBA_DH), 1)
    half = ROT_DIM // 2

    def rope(x):
        partner = jnp.where(lane < half, pltpu.roll(x, MOBA_DH - half, 1), pltpu.roll(x, half, 1))
        return x * cos_t + partner * sin_t

    q = rope(q_ref[...].astype(F32))
    k = rope(k_ref[...].astype(F32))
    q_hi, q_lo = _split_bf16(q)
    qs_ref[...] = (q * (MOBA_DH ** -0.5 * LOG2_E)).astype(BF16)
    ks_ref[...] = k.astype(BF16)
    vt_ref[...] = v_ref[...].astype(F32).T.astype(BF16)

    k_mean = jnp.concatenate(
        [jnp.mean(k[j * MOBA_BLOCK:(j + 1) * MOBA_BLOCK], axis=0, keepdims=True) for j in range(N_KBLK)], axis=0)
    m_hi, m_lo = _split_bf16(k_mean)
    s_blk = _dot_nt(m_hi, q_hi) + _dot_nt(m_hi, q_lo) + _dot_nt(m_lo, q_hi)
    blk = lax.broadcasted_iota(I32, (N_KBLK, SEQ), 0)
    q_blk = lax.shift_right_logical(lax.broadcasted_iota(I32, (N_KBLK, SEQ), 1), MOBA_BLOCK.bit_length() - 1)
    past = blk < q_blk
    s_blk = jnp.where(past, s_blk, _NEG_INF)
    beaten = jnp.zeros((N_KBLK, SEQ), I32)
    for j in range(N_KBLK):
        row = s_blk[j:j + 1, :]
        beaten += jnp.where((row > s_blk) | ((row == s_blk) & (j < blk)), 1, 0)
    bias_ref[...] = jnp.where(past & (beaten < MOBA_TOPK), 0.0, _NEG_INF)

    kr = lax.broadcasted_iota(I32, (MOBA_BLOCK, MOBA_BLOCK), 0)
    qc = lax.broadcasted_iota(I32, (MOBA_BLOCK, MOBA_BLOCK), 1)
    own_bias = jnp.where(kr <= qc, 0.0, _NEG_INF)

    sc_bufs = (sc_a_ref, sc_b_ref)
    pr_bufs = (pr_a_ref, pr_b_ref)
    zeros_row = jnp.zeros((1, MOBA_BLOCK), F32)
    future_row = jnp.full((1, MOBA_BLOCK), _NEG_INF, F32)
    for pair in range(N_KBLK // 2):
        q_blocks = (2 * pair, 2 * pair + 1)
        cols = slice(q_blocks[0] * MOBA_BLOCK, (q_blocks[1] + 1) * MOBA_BLOCK)
        n_kblk = q_blocks[1] + 1
        n_keys = n_kblk * MOBA_BLOCK
        sc, pr = sc_bufs[pair % 2], pr_bufs[pair % 2]
        q_pair = qs_ref[cols, :]

        def query_bias(j):
            halves = []
            for qb in q_blocks:
                q_cols = slice(qb * MOBA_BLOCK, (qb + 1) * MOBA_BLOCK)
                halves.append(bias_ref[j:j + 1, q_cols] if j < qb else zeros_row if j == qb else future_row)
            return jnp.concatenate(halves, axis=1)

        biases = [query_bias(j) for j in range(n_kblk)]
        col_max = []
        for j in range(n_kblk):
            rows = slice(j * MOBA_BLOCK, (j + 1) * MOBA_BLOCK)
            s = _dot_nt(ks_ref[rows, :], q_pair)
            if j in q_blocks:
                h = q_blocks.index(j)
                own = s[:, h * MOBA_BLOCK:(h + 1) * MOBA_BLOCK] + own_bias
                s = jnp.concatenate([own, s[:, MOBA_BLOCK:]] if h == 0 else [s[:, :MOBA_BLOCK], own], axis=1)
            sc[rows, :] = s
            col_max.append(jnp.max(s, axis=0, keepdims=True) + biases[j])
        m = functools.reduce(jnp.maximum, col_max)
        denom = jnp.zeros((1, 2 * MOBA_BLOCK), F32)
        for j in range(n_kblk):
            rows = slice(j * MOBA_BLOCK, (j + 1) * MOBA_BLOCK)
            p = jnp.exp2(sc[rows, :] - (m - biases[j]))
            denom = denom + jnp.sum(p, axis=0, keepdims=True)
            pr[rows, :] = p.astype(BF16)
        o_t = _dot(vt_ref[:, 0:n_keys], pr[0:n_keys, :]) * (1.0 / denom)
        o_ref[cols, :] = o_t.T.astype(BF16)


def _moba(proj, cos_t, sin_t):
    dh = MOBA_DH
    return pl.pallas_call(
        _moba_body,
        grid=(BATCH, MOBA_HEADS),
        in_specs=[
            pl.BlockSpec((SEQ, dh), lambda b, h: (b, COL_MQ // dh + h)),
            pl.BlockSpec((SEQ, dh), lambda b, h: (b, COL_MK // dh + h)),
            pl.BlockSpec((SEQ, dh), lambda b, h: (b, COL_MV // dh + h)),
            pl.BlockSpec((1, SEQ, dh), lambda b, h: (b, 0, 0)),
            pl.BlockSpec((1, SEQ, dh), lambda b, h: (b, 0, 0)),
        ],
        out_specs=pl.BlockSpec((SEQ, dh), lambda b, h: (b, h)),
        out_shape=jax.ShapeDtypeStruct((N_TOK, MOBA_DIM), BF16),
        scratch_shapes=[
            pltpu.VMEM((SEQ, dh), BF16),
            pltpu.VMEM((SEQ, dh), BF16),
            pltpu.VMEM((dh, SEQ), BF16),
            pltpu.VMEM((N_KBLK, SEQ), F32),
            pltpu.VMEM((SEQ, 2 * MOBA_BLOCK), F32),
            pltpu.VMEM((SEQ, 2 * MOBA_BLOCK), F32),
            pltpu.VMEM((SEQ, 2 * MOBA_BLOCK), BF16),
            pltpu.VMEM((SEQ, 2 * MOBA_BLOCK), BF16),
        ],
        compiler_params=_cparams("parallel", "parallel"),
        name="moba",
    )(proj, proj, proj, cos_t, sin_t)


def _layer_norm(z, g, b):
    mu = jnp.mean(z, axis=-1, keepdims=True)
    zc = z - mu
    var = jnp.mean(zc * zc, axis=-1, keepdims=True)
    return zc * lax.rsqrt(var + LN_EPS) * g + b


def _mix_body(gla_ref, moba_ref, ga_ref, gb_ref, x_ref, wgo_ref, wmo_ref, wo_ref, g_ref, b_ref,
              h_ref, hb_ref, hp_ref):
    y_gla = _dot(gla_ref[...], wgo_ref[...])
    y_moba = _dot(moba_ref[...], wmo_ref[...])
    merged = _sigmoid(ga_ref[...].astype(F32)) * y_gla + _sigmoid(gb_ref[...].astype(F32)) * y_moba
    mix = _dot(merged.astype(BF16), wo_ref[...])
    h = _layer_norm(ALPHA * x_ref[...] + mix, g_ref[...], b_ref[...])
    h_ref[...] = h
    hb_ref[...] = h.astype(BF16)
    packed = _pack_bf16_pair(h)
    for c in range(PACKED_CHUNKS):
        hp_ref[:, c, :] = packed[:, c * LANES:(c + 1) * LANES]


def _mix(gla_out, moba_out, proj, x2d, w_gla_o, w_moba_o, w_out, ln_g, ln_b):
    d = D_MODEL
    row = lambda i: (i, 0)
    full = lambda i: (0, 0)
    return pl.pallas_call(
        _mix_body,
        grid=(N_TOK // MIX_TM,),
        in_specs=[
            pl.BlockSpec((MIX_TM, d), row),
            pl.BlockSpec((MIX_TM, d), row),
            pl.BlockSpec((MIX_TM, d), lambda i: (i, COL_GA // d)),
            pl.BlockSpec((MIX_TM, d), lambda i: (i, COL_GB // d)),
            pl.BlockSpec((MIX_TM, d), row),
            pl.BlockSpec((d, d), full),
            pl.BlockSpec((d, d), full),
            pl.BlockSpec((d, d), full),
            pl.BlockSpec((1, d), full),
            pl.BlockSpec((1, d), full),
        ],
        out_specs=[
            pl.BlockSpec((MIX_TM, d), row),
            pl.BlockSpec((MIX_TM, d), row),
            pl.BlockSpec((MIX_TM, PACKED_CHUNKS, LANES), lambda i: (i, 0, 0)),
        ],
        out_shape=[
            jax.ShapeDtypeStruct((N_TOK, d), F32),
            jax.ShapeDtypeStruct((N_TOK, d), BF16),
            jax.ShapeDtypeStruct((N_TOK, PACKED_CHUNKS, LANES), U32),
        ],
        compiler_params=_cparams("parallel"),
        name="mix_ln1",
    )(gla_out, moba_out, proj, proj, x2d, w_gla_o, w_moba_o, w_out, ln_g, ln_b)


def _route_body(h_ref, wr_ref, br_ref, e_ref, w_ref, rk_ref, cnt_ref, carry_ref):
    tm = ROUTE_TM

    @pl.when(pl.program_id(0) == 0)
    def _():
        carry_ref[...] = jnp.zeros_like(carry_ref)

    scores = _sigmoid(_dot_nt(wr_ref[...], h_ref[...]))
    biased = scores + br_ref[...]
    row = lax.broadcasted_iota(I32, (N_EXPERTS, tm), 0).astype(F32)
    row_g = lax.broadcasted_iota(I32, (GROUP_SIZE, tm), 0).astype(F32)

    g_scores = []
    for g in range(N_GROUPS):
        grp = biased[g * GROUP_SIZE:(g + 1) * GROUP_SIZE]
        m1 = jnp.max(grp, axis=0, keepdims=True)
        first = jnp.min(jnp.where(grp == m1, row_g, float(GROUP_SIZE)), axis=0, keepdims=True)
        m2 = jnp.max(jnp.where(row_g == first, _NEG_INF, grp), axis=0, keepdims=True)
        g_scores.append(m1 + m2)
    g_score = jnp.concatenate(g_scores, axis=0)
    g_row = lax.broadcasted_iota(I32, (N_GROUPS, tm), 0)
    g_beaten = jnp.zeros((N_GROUPS, tm), I32)
    for g in range(N_GROUPS):
        r = g_score[g:g + 1, :]
        g_beaten += jnp.where((r > g_score) | ((r == g_score) & (g < g_row)), 1, 0)
    g_keep = g_beaten < TOPK_GROUPS
    masked = jnp.concatenate(
        [jnp.where(g_keep[g:g + 1, :], biased[g * GROUP_SIZE:(g + 1) * GROUP_SIZE], _NEG_INF)
         for g in range(N_GROUPS)], axis=0)

    onehot = jnp.zeros((N_EXPERTS, tm), F32)
    picks, pick_scores = [], []
    for _ in range(TOPK_EXPERTS):
        m = jnp.max(masked, axis=0, keepdims=True)
        idx = jnp.min(jnp.where(masked == m, row, float(N_EXPERTS)), axis=0, keepdims=True)
        hit = row == idx
        picks.append(idx)
        pick_scores.append(jnp.sum(jnp.where(hit, scores, 0.0), axis=0, keepdims=True))
        onehot = onehot + jnp.where(hit, 1.0, 0.0)
        masked = jnp.where(hit, _NEG_INF, masked)
    sel = jnp.concatenate(pick_scores, axis=0)
    e_ref[...] = jnp.concatenate(picks, axis=0).astype(I32)
    w_ref[...] = sel / jnp.sum(sel, axis=0, keepdims=True) * ROUTED_SCALE

    t_r = lax.broadcasted_iota(I32, (tm, tm), 0)
    t_c = lax.broadcasted_iota(I32, (tm, tm), 1)
    earlier = jnp.where(t_r < t_c, 1.0, 0.0).astype(BF16)
    seen = _dot(onehot.astype(BF16), earlier) + carry_ref[...]
    rk_ref[...] = jnp.concatenate(
        [jnp.sum(jnp.where(row == idx, seen, 0.0), axis=0, keepdims=True) for idx in picks], axis=0).astype(I32)
    carry_ref[...] += jnp.sum(onehot, axis=1, keepdims=True)
    cnt_ref[...] = carry_ref[...]


def _route(h_bf, w_router_t, b_router_col):
    k = TOPK_EXPERTS
    tok = lambda i: (0, i)
    return pl.pallas_call(
        _route_body,
        grid=(N_TOK // ROUTE_TM,),
        in_specs=[
            pl.BlockSpec((ROUTE_TM, D_MODEL), lambda i: (i, 0)),
            pl.BlockSpec((N_EXPERTS, D_MODEL), lambda i: (0, 0)),
            pl.BlockSpec((N_EXPERTS, 1), lambda i: (0, 0)),
        ],
        out_specs=[
            pl.BlockSpec((k, ROUTE_TM), tok),
            pl.BlockSpec((k, ROUTE_TM), tok),
            pl.BlockSpec((k, ROUTE_TM), tok),
            pl.BlockSpec((N_EXPERTS, 1), lambda i: (0, 0)),
        ],
        out_shape=[
            jax.ShapeDtypeStruct((k, N_TOK), I32),
            jax.ShapeDtypeStruct((k, N_TOK), F32),
            jax.ShapeDtypeStruct((k, N_TOK), I32),
            jax.ShapeDtypeStruct((N_EXPERTS, 1), F32),
        ],
        scratch_shapes=[pltpu.VMEM((N_EXPERTS, 1), F32)],
        compiler_params=_cparams("arbitrary"),
        name="route",
    )(h_bf, w_router_t, b_router_col)


def _dest_body(cnt_ref, e_ref, rk_ref, d_ref, te_ref, nu_ref, lt_ref):
    cnt = cnt_ref[...]
    tiles = jnp.floor((cnt + (ROW_TILE - 1)) * (1.0 / ROW_TILE))
    er = lax.broadcasted_iota(I32, (N_EXPERTS, N_EXPERTS), 0)
    ec = lax.broadcasted_iota(I32, (N_EXPERTS, N_EXPERTS), 1)
    before = jnp.where(ec < er, 1.0, 0.0).astype(BF16)
    tiles_b = jnp.broadcast_to(tiles, (N_EXPERTS, LANES)).astype(BF16)
    t_start = _dot(before, tiles_b)[:, 0:1]
    t_end = t_start + tiles
    p_start = t_start * float(ROW_TILE)
    lt_ref[...] = jnp.where(tiles > 0.0, (t_end - 1.0) * float(ROW_TILE), -1.0).astype(I32)

    row = lax.broadcasted_iota(I32, (N_EXPERTS, DEST_TM), 0)
    d_ref[...] = jnp.concatenate(
        [jnp.sum(jnp.where(row == e_ref[k:k + 1, :], p_start, 0.0), axis=0, keepdims=True)
         for k in range(TOPK_EXPERTS)], axis=0).astype(I32) + rk_ref[...]

    tile_id = lax.broadcasted_iota(I32, (N_EXPERTS, TILE_TABLE), 1).astype(F32)
    owner = jnp.sum(jnp.where(t_end <= tile_id, 1, 0), axis=0, keepdims=True)
    te_ref[...] = jnp.minimum(owner, N_EXPERTS - 1)
    nu_ref[...] = jnp.broadcast_to(t_end[N_EXPERTS - 1:N_EXPERTS, :], (1, LANES)).astype(I32)


def _dest(cnt, e_t, rk_t):
    k = TOPK_EXPERTS
    tok = lambda i: (0, i)
    return pl.pallas_call(
        _dest_body,
        grid=(N_TOK // DEST_TM,),
        in_specs=[
            pl.BlockSpec((N_EXPERTS, 1), lambda i: (0, 0)),
            pl.BlockSpec((k, DEST_TM), tok),
            pl.BlockSpec((k, DEST_TM), tok),
        ],
        out_specs=[
            pl.BlockSpec((k, DEST_TM), tok),
            pl.BlockSpec((1, TILE_TABLE), lambda i: (0, 0)),
            pl.BlockSpec((1, LANES), lambda i: (0, 0)),
            pl.BlockSpec((N_EXPERTS, 1), lambda i: (0, 0)),
        ],
        out_shape=[
            jax.ShapeDtypeStruct((k, N_TOK), I32),
            jax.ShapeDtypeStruct((1, TILE_TABLE), I32),
            jax.ShapeDtypeStruct((1, LANES), I32),
            jax.ShapeDtypeStruct((N_EXPERTS, 1), I32),
        ],
        compiler_params=_cparams("arbitrary"),
        name="dest",
    )(cnt, e_t, rk_t)


def _row_copy(src, dst, sem):
    return pltpu.make_async_copy(src, dst, sem)


def _dispatch_body(lt_ref, d_ref, h_ref, xs_ref, zero_ref, sem):
    @pl.when(pl.program_id(0) == 0)
    def _():
        zero_ref[...] = jnp.zeros_like(zero_ref)

        def tile_copy(e):
            r0 = pl.multiple_of(jnp.maximum(lt_ref[e], 0), ROW_TILE)
            return pltpu.make_async_copy(zero_ref, xs_ref.at[pl.ds(r0, ROW_TILE)], sem)

        def z_start(e, carry):
            @pl.when(lt_ref[e] >= 0)
            def _():
                tile_copy(e).start()
            return carry

        def z_wait(e, carry):
            @pl.when(lt_ref[e] >= 0)
            def _():
                tile_copy(e).wait()
            return carry

        lax.fori_loop(0, N_EXPERTS, z_start, 0)
        lax.fori_loop(0, N_EXPERTS, z_wait, 0)

    def issue(t, carry):
        for k in range(TOPK_EXPERTS):
            _row_copy(h_ref.at[t], xs_ref.at[d_ref[k, t]], sem).start(priority=k % 2)
        return carry

    lax.fori_loop(0, DISP_TM, issue, 0, unroll=DISPATCH_UNROLL)
    for k in range(TOPK_EXPERTS):
        pltpu.make_async_copy(h_ref, xs_ref.at[pl.ds(0, DISP_TM)], sem).wait()


def _dispatch(last_tile_row, dest_t, h1):
    grid_spec = pltpu.PrefetchScalarGridSpec(
        num_scalar_prefetch=1,
        grid=(N_TOK // DISP_TM,),
        in_specs=[
            pl.BlockSpec((TOPK_EXPERTS, DISP_TM), lambda i, lt: (0, i), memory_space=pltpu.SMEM),
            pl.BlockSpec((DISP_TM, PACKED_CHUNKS, LANES), lambda i, lt: (i, 0, 0)),
        ],
        out_specs=pl.BlockSpec(memory_space=pl.ANY),
        scratch_shapes=[pltpu.VMEM((ROW_TILE, PACKED_CHUNKS, LANES), U32), pltpu.SemaphoreType.DMA],
    )
    return pl.pallas_call(
        _dispatch_body,
        grid_spec=grid_spec,
        out_shape=jax.ShapeDtypeStruct((N_ROWS, PACKED_CHUNKS, LANES), U32),
        compiler_params=_cparams("arbitrary"),
        name="dispatch",
    )(last_tile_row, dest_t, h1)


def _row_destinations(dest_flat):
    info = plsc.get_sparse_core_info()
    n_cores, lanes = info.num_cores, info.num_lanes
    n_workers = n_cores * info.num_subcores
    rows_per_worker = -(-N_ROWS // (n_workers * ROW_TILE)) * ROW_TILE
    assert rows_per_worker % lanes == 0
    mesh = plsc.VectorSubcoreMesh(core_axis_name="c", subcore_axis_name="s")

    @functools.partial(
        pl.kernel, mesh=mesh, out_type=jax.ShapeDtypeStruct((n_workers * rows_per_worker,), I32),
        scratch_types=[pltpu.VMEM((SLOT_CHUNK,), I32), pltpu.VMEM((rows_per_worker,), I32)],
        compiler_params=pltpu.CompilerParams(needs_layout_passes=False), name="row_destinations")
    def invert(dest_hbm, out_hbm, dest_v, table_v):
        first_row = (lax.axis_index("s") * n_cores + lax.axis_index("c")) * rows_per_worker
        lane = lax.iota(I32, lanes)
        tile_shift = ROW_TILE.bit_length() - 1

        def spare(i, carry):
            row = first_row + i * lanes + lane
            buf = lax.rem(lax.shift_right_logical(row, tile_shift), EXPERT_BUFS)
            table_v[pl.ds(i * lanes, lanes)] = TOPK_EXPERTS * N_TOK + buf * ROW_TILE + (row & (ROW_TILE - 1))
            return carry

        lax.fori_loop(0, rows_per_worker // lanes, spare, 0)

        def chunk(c, carry):
            pltpu.sync_copy(dest_hbm.at[pl.ds(c * SLOT_CHUNK, SLOT_CHUNK)], dest_v)

            def vec(i, inner):
                local = dest_v[pl.ds(i * lanes, lanes)] - first_row
                mine = (local >= 0) & (local < rows_per_worker)
                plsc.store_scatter(table_v, [jnp.where(mine, local, 0)], lane + (c * SLOT_CHUNK + i * lanes),
                                   mask=mine)
                return inner

            lax.fori_loop(0, SLOT_CHUNK // lanes, vec, 0)
            return carry

        lax.fori_loop(0, N_ASSIGN // SLOT_CHUNK, chunk, 0)
        pltpu.sync_copy(table_v, out_hbm.at[pl.ds(first_row, rows_per_worker)])

    return invert(dest_flat)


def _experts_body(te_ref, nu_ref, slot_ref, xs_ref, wg_ref, wu_ref, wd_ref, yk_ref, wg_b, wu_b, wd_b, y_buf, sem):
    i = pl.program_id(0)
    n_used = nu_ref[0]

    def wait_tile(b):
        pltpu.make_async_copy(y_buf.at[b], yk_ref.at[pl.ds(0, ROW_TILE), :], sem.at[b]).wait()

    def send_tile(b):
        for r in range(ROW_TILE):
            dst = slot_ref[0, 0, r]
            _row_copy(y_buf.at[b, pl.ds(r, 1), :], yk_ref.at[pl.ds(dst, 1), :], sem.at[b]).start(priority=r % 2)

    @pl.when(i == 0)
    def _():
        y_buf[EXPERT_BUFS - 1] = jnp.zeros((ROW_TILE, PACKED_COLS), U32)

    @pl.when((i >= 2) & (i <= n_used))
    def _():
        wait_tile(lax.rem(i, EXPERT_BUFS))

    @pl.when(i < n_used)
    def _():
        prev = te_ref[jnp.maximum(i - 1, 0)]

        @pl.when((i == 0) | (te_ref[i] != prev))
        def _():
            wg_b[...] = wg_ref[0].astype(BF16)
            wu_b[...] = wu_ref[0].astype(BF16)
            wd_b[...] = wd_ref[0].astype(BF16)

        send_tile(lax.rem(i + EXPERT_BUFS - 1, EXPERT_BUFS))
        packed = jnp.concatenate([xs_ref[:, c, :] for c in range(PACKED_CHUNKS)], axis=1)
        x_lo, x_hi = _unpack_bf16_pair(packed)
        x = jnp.concatenate([x_lo.astype(BF16), x_hi.astype(BF16)], axis=1)
        g = _dot(x, wg_b[...])
        u = _dot(x, wu_b[...])
        h = (g * _sigmoid(g)) * u
        y_buf[lax.rem(i, EXPERT_BUFS)] = _pack_bf16_pair(_dot(h.astype(BF16), wd_b[...]))

    @pl.when(i == n_used)
    def _():
        send_tile(lax.rem(i + EXPERT_BUFS - 1, EXPERT_BUFS))
        wait_tile(lax.rem(i + EXPERT_BUFS - 2, EXPERT_BUFS))
        wait_tile(lax.rem(i + EXPERT_BUFS - 1, EXPERT_BUFS))


def _experts(tile_expert, n_used, row_dst, xs, w_gate, w_up, w_down):
    def tile(i, te, nu):
        return (jnp.minimum(i, nu[0] - 1), 0, 0)

    n_table_tiles = row_dst.shape[0] // ROW_TILE
    placeholder = n_table_tiles - 1
    assert placeholder * ROW_TILE >= N_ROWS and placeholder % EXPERT_BUFS == EXPERT_BUFS - 1

    def prev_tile(i, te, nu):
        return (jnp.where(i == 0, placeholder, jnp.minimum(i, nu[0]) - 1), 0, 0)

    def expert(i, te, nu):
        return (te[jnp.minimum(i, nu[0] - 1)], 0, 0)

    grid_spec = pltpu.PrefetchScalarGridSpec(
        num_scalar_prefetch=2,
        grid=(N_ROW_TILES + 1,),
        in_specs=[
            pl.BlockSpec((1, 1, ROW_TILE), prev_tile, memory_space=pltpu.SMEM),
            pl.BlockSpec((ROW_TILE, PACKED_CHUNKS, LANES), tile),
            pl.BlockSpec((1, D_MODEL, D_EXPERT), expert),
            pl.BlockSpec((1, D_MODEL, D_EXPERT), expert),
            pl.BlockSpec((1, D_EXPERT, D_MODEL), expert),
        ],
        out_specs=pl.BlockSpec(memory_space=pl.ANY),
        scratch_shapes=[
            pltpu.VMEM((D_MODEL, D_EXPERT), BF16),
            pltpu.VMEM((D_MODEL, D_EXPERT), BF16),
            pltpu.VMEM((D_EXPERT, D_MODEL), BF16),
            pltpu.VMEM((EXPERT_BUFS, ROW_TILE, PACKED_COLS), U32),
            pltpu.SemaphoreType.DMA((EXPERT_BUFS,)),
        ],
    )
    return pl.pallas_call(
        _experts_body,
        grid_spec=grid_spec,
        out_shape=jax.ShapeDtypeStruct((YK_ROWS, PACKED_COLS), U32),
        compiler_params=_cparams("arbitrary"),
        name="experts",
    )(tile_expert, n_used, row_dst.reshape(n_table_tiles, 1, ROW_TILE), xs, w_gate, w_up, w_down)


def _final_body(w_ref, yk_ref, h_ref, hb_ref, p_ref, wsg_ref, wsu_ref, wsd_ref, wpl_ref, wpg_ref,
                g_ref, b_ref, o_ref):
    hb = hb_ref[...]
    sg = _dot(hb, wsg_ref[...])
    shared = _dot(((sg * _sigmoid(sg)) * _dot(hb, wsu_ref[...])).astype(BF16), wsd_ref[...])
    ple = _sigmoid(_dot(hb, wpg_ref[...])) * _dot(p_ref[...].astype(BF16), wpl_ref[...])

    w_col = w_ref[...].T
    y_lo, y_hi = _unpack_bf16_pair(yk_ref[0])
    r_lo, r_hi = y_lo * w_col[:, 0:1], y_hi * w_col[:, 0:1]
    for k in range(1, TOPK_EXPERTS):
        y_lo, y_hi = _unpack_bf16_pair(yk_ref[k])
        r_lo, r_hi = r_lo + y_lo * w_col[:, k:k + 1], r_hi + y_hi * w_col[:, k:k + 1]
    routed = jnp.concatenate([r_lo, r_hi], axis=1)
    o_ref[...] = _layer_norm(ALPHA * h_ref[...] + (routed + shared) + ple, g_ref[...], b_ref[...])


def _final(w_t, yk, h1, h_bf, p2d, w_s_gate, w_s_up, w_s_down, w_ple, w_ple_gate, ln_g, ln_b):
    d = D_MODEL
    full = lambda i: (0, 0)
    return pl.pallas_call(
        _final_body,
        grid=(N_TOK // FINAL_TM,),
        in_specs=[
            pl.BlockSpec((TOPK_EXPERTS, FINAL_TM), lambda i: (0, i)),
            pl.BlockSpec((TOPK_EXPERTS, FINAL_TM, PACKED_COLS), lambda i: (0, i, 0)),
            pl.BlockSpec((FINAL_TM, d), lambda i: (i, 0)),
            pl.BlockSpec((FINAL_TM, d), lambda i: (i, 0)),
            pl.BlockSpec((FINAL_TM, PLE_DIM), lambda i: (i, 0)),
            pl.BlockSpec((d, D_SHARED), full),
            pl.BlockSpec((d, D_SHARED), full),
            pl.BlockSpec((D_SHARED, d), full),
            pl.BlockSpec((PLE_DIM, d), full),
            pl.BlockSpec((d, d), full),
            pl.BlockSpec((1, d), full),
            pl.BlockSpec((1, d), full),
        ],
        out_specs=pl.BlockSpec((FINAL_TM, d), lambda i: (i, 0)),
        out_shape=jax.ShapeDtypeStruct((N_TOK, d), F32),
        compiler_params=_cparams("parallel"),
        name="combine_ln2",
    )(w_t, yk.reshape(TOPK_EXPERTS + 1, N_TOK, PACKED_COLS), h1, h_bf, p2d, w_s_gate, w_s_up, w_s_down,
      w_ple, w_ple_gate, ln_g, ln_b)


def _rope_tables(positions):
    half = ROT_DIM // 2
    inv = ROPE_THETA ** (-jnp.arange(0, ROT_DIM, 2, dtype=F32) / ROT_DIM)
    ang = positions.astype(F32)[..., None] * inv
    cos, sin = jnp.cos(ang), jnp.sin(ang)
    rest = MOBA_DH - ROT_DIM
    ones = jnp.ones(cos.shape[:-1] + (rest,), F32)
    zeros = jnp.zeros(cos.shape[:-1] + (rest,), F32)
    return (jnp.concatenate([cos, cos, ones], axis=-1), jnp.concatenate([-sin, sin, zeros], axis=-1))


def _layer(h2d, p2d, cos_t, sin_t, w_in, w_gk2, b_gk, norm_g, w_gla_o, w_moba_o, w_out, ln1_g, ln1_b,
           w_router, b_router, w_e_gate, w_e_up, w_e_down, w_s_gate, w_s_up, w_s_down, w_ple, w_ple_gate,
           ln2_g, ln2_b):
    low_end = LOW_SRC + GLA_RANK
    w_cat = jnp.concatenate(
        [w_in[:, :LOW_SRC], w_in[:, low_end:], w_in[:, LOW_SRC:low_end],
         jnp.zeros((D_MODEL, LANES - GLA_RANK), w_in.dtype)], axis=1).astype(BF16)
    w_gk2_pad = jnp.concatenate([w_gk2, jnp.zeros((LANES - GLA_RANK, GLA_KDIM), w_gk2.dtype)], axis=0)

    proj = _in_proj(h2d, w_cat)
    gla_out = _gla(proj, w_gk2_pad, b_gk[None, :], norm_g[None, :])
    moba_out = _moba(proj, cos_t, sin_t)
    h1, h_bf, h_packed = _mix(gla_out, moba_out, proj, h2d, w_gla_o.astype(BF16), w_moba_o.astype(BF16),
                              w_out.astype(BF16), ln1_g[None, :], ln1_b[None, :])
    e_t, w_t, rk_t, cnt = _route(h_bf, w_router.T.astype(BF16), b_router[:, None])
    dest_t, tile_expert, n_used, last_tile_row = _dest(cnt, e_t, rk_t)
    xs = _dispatch(last_tile_row.reshape(N_EXPERTS), dest_t, h_packed)
    row_dst = _row_destinations(dest_t.reshape(N_ASSIGN))
    yk = _experts(tile_expert.reshape(TILE_TABLE), n_used[0, 0:1], row_dst, xs, w_e_gate, w_e_up, w_e_down)
    return _final(w_t, yk, h1, h_bf, p2d, w_s_gate.astype(BF16), w_s_up.astype(BF16),
                  w_s_down.astype(BF16), w_ple.astype(BF16), w_ple_gate.astype(BF16),
                  ln2_g[None, :], ln2_b[None, :])


def kernel(x, p, positions, w_in, w_gla_gk2, b_gla_gk, gla_norm_g, w_gla_o, w_moba_o, w_out, ln1_g, ln1_b,
           w_router, b_router, w_e_gate, w_e_up, w_e_down, w_s_gate, w_s_up, w_s_down, w_ple, w_ple_gate,
           ln2_g, ln2_b):
    cos_t, sin_t = _rope_tables(positions)
    h = x.reshape(N_TOK, D_MODEL)
    for i in range(DEPTH):
        h = _layer(h, p[i].reshape(N_TOK, PLE_DIM), cos_t, sin_t, w_in[i], w_gla_gk2[i], b_gla_gk[i],
                   gla_norm_g[i], w_gla_o[i], w_moba_o[i], w_out[i], ln1_g[i], ln1_b[i], w_router[i],
                   b_router[i], w_e_gate[i], w_e_up[i], w_e_down[i], w_s_gate[i], w_s_up[i], w_s_down[i],
                   w_ple[i], w_ple_gate[i], ln2_g[i], ln2_b[i])
    return h.reshape(BATCH, SEQ, D_MODEL)
```

```python
import functools

import jax
import jax.numpy as jnp
from jax import lax
from jax.experimental import pallas as pl
from jax.experimental.pallas import tpu as pltpu
from jax.experimental.pallas import tpu_sc as plsc

F32 = jnp.float32
BF16 = jnp.bfloat16
I32 = jnp.int32
U32 = jnp.uint32

LANES = 128
SUBLANES = 8
VMEM_LIMIT_BYTES = 48 * 1024 * 1024

D_MODEL = 1024
BATCH = 8
SEQ = 2048
N_TOK = BATCH * SEQ
GLA_HEADS = 4
GLA_DK = 128
GLA_DV = 256
GLA_RANK = 16
GLA_NORMALIZER = 16.0
GLA_CHUNK = 64
GLA_GROUP = 4 * GLA_CHUNK
MOBA_HEADS = 8
MOBA_DH = 128
MOBA_BLOCK = 256
MOBA_TOPK = 3
ROT_DIM = 32
ROPE_THETA = 500000.0
N_EXPERTS = 256
TOPK_EXPERTS = 8
N_GROUPS = 8
GROUP_SIZE = N_EXPERTS // N_GROUPS
TOPK_GROUPS = 4
D_EXPERT = 256
D_SHARED = 256
ROUTED_SCALE = 2.5
PLE_DIM = 256
LN_EPS = 1e-5
DEPTH = 1
ALPHA = (2.0 * DEPTH) ** 0.25
GLA_KDIM = GLA_HEADS * GLA_DK
GLA_VDIM = GLA_HEADS * GLA_DV
MOBA_DIM = MOBA_HEADS * MOBA_DH
N_KBLK = SEQ // MOBA_BLOCK

COL_GQ = 0
COL_GK = COL_GQ + GLA_KDIM
COL_GV = COL_GK + GLA_KDIM
COL_GR = COL_GV + GLA_VDIM
COL_MQ = COL_GR + GLA_VDIM
COL_MK = COL_MQ + MOBA_DIM
COL_MV = COL_MK + MOBA_DIM
COL_GA = COL_MV + MOBA_DIM
COL_GB = COL_GA + D_MODEL
COL_LOW = COL_GB + D_MODEL
PROJ_COLS = COL_LOW + LANES
LOW_SRC = 2 * GLA_KDIM + 2 * GLA_VDIM

ROW_TILE = 256
PACKED_COLS = D_MODEL // 2
PACKED_CHUNKS = PACKED_COLS // LANES
EXPERT_BUFS = 3
YK_ROWS = (TOPK_EXPERTS + 1) * N_TOK
SLOT_CHUNK = 4096
assert EXPERT_BUFS * ROW_TILE <= N_TOK
N_ASSIGN = N_TOK * TOPK_EXPERTS
N_ROW_TILES = (N_ASSIGN + N_EXPERTS * (ROW_TILE - 1) + ROW_TILE - 1) // ROW_TILE
N_ROWS = N_ROW_TILES * ROW_TILE
TILE_TABLE = -(-N_ROW_TILES // LANES) * LANES

PROJ_TM = 1024
PROJ_TN = PROJ_COLS // 5
MIX_TM = 512
ROUTE_TM = 512
DEST_TM = 2048
DISP_TM = 256
DISPATCH_UNROLL = 4
FINAL_TM = 256

_NEG_INF = float("-inf")
LOG2_E = 1.4426950408889634


def _cparams(*sem):
    return pltpu.CompilerParams(dimension_semantics=sem, vmem_limit_bytes=VMEM_LIMIT_BYTES)


def _dot(a, b):
    return jnp.dot(a, b, preferred_element_type=F32)


def _dot_nt(a, b):
    return lax.dot_general(a, b, (((1,), (1,)), ((), ())), preferred_element_type=F32)


def _dot_tn(a, b):
    return lax.dot_general(a, b, (((0,), (0,)), ((), ())), preferred_element_type=F32)


def _split_bf16(x):
    hi = x.astype(BF16)
    lo = (x - hi.astype(F32)).astype(BF16)
    return hi, lo


def _sigmoid(x):
    return 1.0 / (1.0 + jnp.exp(-x))


def _pack_bf16_pair(x):
    lo = lax.bitcast_convert_type(x[:, :PACKED_COLS].astype(BF16).astype(F32), U32)
    hi = lax.bitcast_convert_type(x[:, PACKED_COLS:].astype(BF16).astype(F32), U32)
    return hi | lax.shift_right_logical(lo, jnp.uint32(16))


def _unpack_bf16_pair(w):
    lo = lax.bitcast_convert_type(lax.shift_left(w, jnp.uint32(16)), F32)
    hi = lax.bitcast_convert_type(w & jnp.uint32(0xFFFF0000), F32)
    return lo, hi


def _proj_body(x_ref, w_ref, o_ref):
    o_ref[...] = _dot(x_ref[...].astype(BF16), w_ref[...]).astype(BF16)


def _in_proj(x2d, w_cat):
    return pl.pallas_call(
        _proj_body,
        grid=(PROJ_COLS // PROJ_TN, N_TOK // PROJ_TM),
        in_specs=[
            pl.BlockSpec((PROJ_TM, D_MODEL), lambda j, i: (i, 0)),
            pl.BlockSpec((D_MODEL, PROJ_TN), lambda j, i: (0, j)),
        ],
        out_specs=pl.BlockSpec((PROJ_TM, PROJ_TN), lambda j, i: (i, j)),
        out_shape=jax.ShapeDtypeStruct((N_TOK, PROJ_COLS), BF16),
        compiler_params=_cparams("parallel", "parallel"),
        name="in_proj",
    )(x2d, w_cat)


def _gla_body(q_ref, k_ref, v_ref, r_ref, low_ref, wg_ref, bg_ref, ng_ref, o_ref, st_ref, gk_ref):
    w_hi, w_lo = _split_bf16(wg_ref[...])
    low = low_ref[...]
    lin = _dot(low, w_hi) + _dot(low, w_lo) + bg_ref[...]
    gk_ref[...] = (jnp.minimum(lin, 0.0) - jnp.log1p(jnp.exp(-jnp.abs(lin)))) * (1.0 / GLA_NORMALIZER)
    st_ref[...] = jnp.zeros_like(st_ref)

    ri = lax.broadcasted_iota(I32, (GLA_GROUP, GLA_GROUP), 0)
    ci = lax.broadcasted_iota(I32, (GLA_GROUP, GLA_GROUP), 1)
    same_chunk = lax.shift_right_logical(ri, GLA_CHUNK.bit_length() - 1) == lax.shift_right_logical(
        ci, GLA_CHUNK.bit_length() - 1)
    causal = same_chunk & (ri >= ci)
    sums = jnp.concatenate([jnp.where(causal, 1.0, 0.0), jnp.where(same_chunk, 1.0, 0.0)], axis=0).astype(BF16)
    gain = ng_ref[...]

    def group(c, carry):
        rows = pl.ds(pl.multiple_of(c * GLA_GROUP, GLA_GROUP), GLA_GROUP)
        g_hi, g_lo = _split_bf16(gk_ref[rows, :])
        bb = _dot(sums, g_hi) + _dot(sums, g_lo)
        b = bb[0:GLA_GROUP]
        b_end = bb[GLA_GROUP:2 * GLA_GROUP]
        q = q_ref[rows, :].astype(F32) * (GLA_DK ** -0.5)
        k = k_ref[rows, :].astype(F32)
        v = v_ref[rows, :]
        q_e = (q * jnp.exp(b)).astype(BF16)
        k_e = (k * jnp.exp(-b)).astype(BF16)
        k_d = (k * jnp.exp(b_end - b)).astype(BF16)
        att = jnp.where(causal, _dot_nt(q_e, k_e), 0.0)
        o = _dot(att.astype(BF16), v)
        st = st_ref[...]
        inter = []
        for j in range(GLA_GROUP // GLA_CHUNK):
            cr = slice(j * GLA_CHUNK, (j + 1) * GLA_CHUNK)
            inter.append(_dot_nt(q_e[cr], st.astype(BF16)))
            st = st * jnp.exp(b_end[j * GLA_CHUNK:j * GLA_CHUNK + 1, :]) + _dot_tn(v[cr], k_d[cr])
        st_ref[...] = st
        o = o + jnp.concatenate(inter, axis=0)
        o = o * lax.rsqrt(jnp.mean(o * o, axis=-1, keepdims=True) + LN_EPS) * gain
        r = r_ref[rows, :].astype(F32)
        o_ref[rows, :] = (o * (r * _sigmoid(r))).astype(BF16)
        return carry

    lax.fori_loop(0, SEQ // GLA_GROUP, group, 0)


def _gla(proj, w_gk2_pad, b_gk, norm_g):
    kb, vb = GLA_DK, GLA_DV
    return pl.pallas_call(
        _gla_body,
        grid=(BATCH, GLA_HEADS),
        in_specs=[
            pl.BlockSpec((SEQ, kb), lambda b, h: (b, COL_GQ // kb + h)),
            pl.BlockSpec((SEQ, kb), lambda b, h: (b, COL_GK // kb + h)),
            pl.BlockSpec((SEQ, vb), lambda b, h: (b, COL_GV // vb + h)),
            pl.BlockSpec((SEQ, vb), lambda b, h: (b, COL_GR // vb + h)),
            pl.BlockSpec((SEQ, LANES), lambda b, h: (b, COL_LOW // LANES)),
            pl.BlockSpec((LANES, kb), lambda b, h: (0, h)),
            pl.BlockSpec((1, kb), lambda b, h: (0, h)),
            pl.BlockSpec((1, vb), lambda b, h: (0, 0)),
        ],
        out_specs=pl.BlockSpec((SEQ, vb), lambda b, h: (b, h)),
        out_shape=jax.ShapeDtypeStruct((N_TOK, GLA_VDIM), BF16),
        scratch_shapes=[pltpu.VMEM((vb, kb), F32), pltpu.VMEM((SEQ, kb), F32)],
        compiler_params=_cparams("parallel", "parallel"),
        name="gla",
    )(proj, proj, proj, proj, proj, w_gk2_pad, b_gk, norm_g)


def _moba_body(q_ref, k_ref, v_ref, c_ref, s_ref, o_ref, qs_ref, ks_ref, vt_ref, bias_ref,
               sc_a_ref, sc_b_ref, pr_a_ref, pr_b_ref):
    cos_t = c_ref[0]
    sin_t = s_ref[0]
    lane = lax.broadcasted_iota(I32, (SEQ, MOBA_DH), 1)
    half = ROT_DIM // 2

    def rope(x):
        partner = jnp.where(lane < half, pltpu.roll(x, MOBA_DH - half, 1), pltpu.roll(x, half, 1))
        return x * cos_t + partner * sin_t

    q = rope(q_ref[...].astype(F32))
    k = rope(k_ref[...].astype(F32))
    q_hi, q_lo = _split_bf16(q)
    qs_ref[...] = (q * (MOBA_DH ** -0.5 * LOG2_E)).astype(BF16)
    ks_ref[...] = k.astype(BF16)
    vt_ref[...] = v_ref[...].astype(F32).T.astype(BF16)

    k_mean = jnp.concatenate(
        [jnp.mean(k[j * MOBA_BLOCK:(j + 1) * MOBA_BLOCK], axis=0, keepdims=True) for j in range(N_KBLK)], axis=0)
    m_hi, m_lo = _split_bf16(k_mean)
    s_blk = _dot_nt(m_hi, q_hi) + _dot_nt(m_hi, q_lo) + _dot_nt(m_lo, q_hi)
    blk = lax.broadcasted_iota(I32, (N_KBLK, SEQ), 0)
    q_blk = lax.shift_right_logical(lax.broadcasted_iota(I32, (N_KBLK, SEQ), 1), MOBA_BLOCK.bit_length() - 1)
    past = blk < q_blk
    s_blk = jnp.where(past, s_blk, _NEG_INF)
    beaten = jnp.zeros((N_KBLK, SEQ), I32)
    for j in range(N_KBLK):
        row = s_blk[j:j + 1, :]
        beaten += jnp.where((row > s_blk) | ((row == s_blk) & (j < blk)), 1, 0)
    bias_ref[...] = jnp.where(past & (beaten < MOBA_TOPK), 0.0, _NEG_INF)

    kr = lax.broadcasted_iota(I32, (MOBA_BLOCK, MOBA_BLOCK), 0)
    qc = lax.broadcasted_iota(I32, (MOBA_BLOCK, MOBA_BLOCK), 1)
    own_bias = jnp.where(kr <= qc, 0.0, _NEG_INF)

    sc_bufs = (sc_a_ref, sc_b_ref)
    pr_bufs = (pr_a_ref, pr_b_ref)
    zeros_row = jnp.zeros((1, MOBA_BLOCK), F32)
    future_row = jnp.full((1, MOBA_BLOCK), _NEG_INF, F32)
    for pair in range(N_KBLK // 2):
        q_blocks = (2 * pair, 2 * pair + 1)
        cols = slice(q_blocks[0] * MOBA_BLOCK, (q_blocks[1] + 1) * MOBA_BLOCK)
        n_kblk = q_blocks[1] + 1
        n_keys = n_kblk * MOBA_BLOCK
        sc, pr = sc_bufs[pair % 2], pr_bufs[pair % 2]
        q_pair = qs_ref[cols, :]

        def query_bias(j):
            halves = []
            for qb in q_blocks:
                q_cols = slice(qb * MOBA_BLOCK, (qb + 1) * MOBA_BLOCK)
                halves.append(bias_ref[j:j + 1, q_cols] if j < qb else zeros_row if j == qb else future_row)
            return jnp.concatenate(halves, axis=1)

        biases = [query_bias(j) for j in range(n_kblk)]
        col_max = []
        for j in range(n_kblk):
            rows = slice(j * MOBA_BLOCK, (j + 1) * MOBA_BLOCK)
            s = _dot_nt(ks_ref[rows, :], q_pair)
            if j in q_blocks:
                h = q_blocks.index(j)
                own = s[:, h * MOBA_BLOCK:(h + 1) * MOBA_BLOCK] + own_bias
                s = jnp.concatenate([own, s[:, MOBA_BLOCK:]] if h == 0 else [s[:, :MOBA_BLOCK], own], axis=1)
            sc[rows, :] = s
            col_max.append(jnp.max(s, axis=0, keepdims=True) + biases[j])
        m = functools.reduce(jnp.maximum, col_max)
        denom = jnp.zeros((1, 2 * MOBA_BLOCK), F32)
        for j in range(n_kblk):
            rows = slice(j * MOBA_BLOCK, (j + 1) * MOBA_BLOCK)
            p = jnp.exp2(sc[rows, :] - (m - biases[j]))
            denom = denom + jnp.sum(p, axis=0, keepdims=True)
            pr[rows, :] = p.astype(BF16)
        o_t = _dot(vt_ref[:, 0:n_keys], pr[0:n_keys, :]) * (1.0 / denom)
        o_ref[cols, :] = o_t.T.astype(BF16)


def _moba(proj, cos_t, sin_t):
    dh = MOBA_DH
    return pl.pallas_call(
        _moba_body,
        grid=(BATCH, MOBA_HEADS),
        in_specs=[
            pl.BlockSpec((SEQ, dh), lambda b, h: (b, COL_MQ // dh + h)),
            pl.BlockSpec((SEQ, dh), lambda b, h: (b, COL_MK // dh + h)),
            pl.BlockSpec((SEQ, dh), lambda b, h: (b, COL_MV // dh + h)),
            pl.BlockSpec((1, SEQ, dh), lambda b, h: (b, 0, 0)),
            pl.BlockSpec((1, SEQ, dh), lambda b, h: (b, 0, 0)),
        ],
        out_specs=pl.BlockSpec((SEQ, dh), lambda b, h: (b, h)),
        out_shape=jax.ShapeDtypeStruct((N_TOK, MOBA_DIM), BF16),
        scratch_shapes=[
            pltpu.VMEM((SEQ, dh), BF16),
            pltpu.VMEM((SEQ, dh), BF16),
            pltpu.VMEM((dh, SEQ), BF16),
            pltpu.VMEM((N_KBLK, SEQ), F32),
            pltpu.VMEM((SEQ, 2 * MOBA_BLOCK), F32),
            pltpu.VMEM((SEQ, 2 * MOBA_BLOCK), F32),
            pltpu.VMEM((SEQ, 2 * MOBA_BLOCK), BF16),
            pltpu.VMEM((SEQ, 2 * MOBA_BLOCK), BF16),
        ],
        compiler_params=_cparams("parallel", "parallel"),
        name="moba",
    )(proj, proj, proj, cos_t, sin_t)


def _layer_norm(z, g, b):
    mu = jnp.mean(z, axis=-1, keepdims=True)
    zc = z - mu
    var = jnp.mean(zc * zc, axis=-1, keepdims=True)
    return zc * lax.rsqrt(var + LN_EPS) * g + b


def _mix_body(gla_ref, moba_ref, ga_ref, gb_ref, x_ref, wgo_ref, wmo_ref, wo_ref, g_ref, b_ref,
              h_ref, hb_ref, hp_ref):
    y_gla = _dot(gla_ref[...], wgo_ref[...])
    y_moba = _dot(moba_ref[...], wmo_ref[...])
    merged = _sigmoid(ga_ref[...].astype(F32)) * y_gla + _sigmoid(gb_ref[...].astype(F32)) * y_moba
    mix = _dot(merged.astype(BF16), wo_ref[...])
    h = _layer_norm(ALPHA * x_ref[...] + mix, g_ref[...], b_ref[...])
    h_ref[...] = h
    hb_ref[...] = h.astype(BF16)
    packed = _pack_bf16_pair(h)
    for c in range(PACKED_CHUNKS):
        hp_ref[:, c, :] = packed[:, c * LANES:(c + 1) * LANES]


def _mix(gla_out, moba_out, proj, x2d, w_gla_o, w_moba_o, w_out, ln_g, ln_b):
    d = D_MODEL
    row = lambda i: (i, 0)
    full = lambda i: (0, 0)
    return pl.pallas_call(
        _mix_body,
        grid=(N_TOK // MIX_TM,),
        in_specs=[
            pl.BlockSpec((MIX_TM, d), row),
            pl.BlockSpec((MIX_TM, d), row),
            pl.BlockSpec((MIX_TM, d), lambda i: (i, COL_GA // d)),
            pl.BlockSpec((MIX_TM, d), lambda i: (i, COL_GB // d)),
            pl.BlockSpec((MIX_TM, d), row),
            pl.BlockSpec((d, d), full),
            pl.BlockSpec((d, d), full),
            pl.BlockSpec((d, d), full),
            pl.BlockSpec((1, d), full),
            pl.BlockSpec((1, d), full),
        ],
        out_specs=[
            pl.BlockSpec((MIX_TM, d), row),
            pl.BlockSpec((MIX_TM, d), row),
            pl.BlockSpec((MIX_TM, PACKED_CHUNKS, LANES), lambda i: (i, 0, 0)),
        ],
        out_shape=[
            jax.ShapeDtypeStruct((N_TOK, d), F32),
            jax.ShapeDtypeStruct((N_TOK, d), BF16),
            jax.ShapeDtypeStruct((N_TOK, PACKED_CHUNKS, LANES), U32),
        ],
        compiler_params=_cparams("parallel"),
        name="mix_ln1",
    )(gla_out, moba_out, proj, proj, x2d, w_gla_o, w_moba_o, w_out, ln_g, ln_b)


def _route_body(h_ref, wr_ref, br_ref, e_ref, w_ref, rk_ref, cnt_ref, carry_ref):
    tm = ROUTE_TM

    @pl.when(pl.program_id(0) == 0)
    def _():
        carry_ref[...] = jnp.zeros_like(carry_ref)

    scores = _sigmoid(_dot_nt(wr_ref[...], h_ref[...]))
    biased = scores + br_ref[...]
    row = lax.broadcasted_iota(I32, (N_EXPERTS, tm), 0).astype(F32)
    row_g = lax.broadcasted_iota(I32, (GROUP_SIZE, tm), 0).astype(F32)

    g_scores = []
    for g in range(N_GROUPS):
        grp = biased[g * GROUP_SIZE:(g + 1) * GROUP_SIZE]
        m1 = jnp.max(grp, axis=0, keepdims=True)
        first = jnp.min(jnp.where(grp == m1, row_g, float(GROUP_SIZE)), axis=0, keepdims=True)
        m2 = jnp.max(jnp.where(row_g == first, _NEG_INF, grp), axis=0, keepdims=True)
        g_scores.append(m1 + m2)
    g_score = jnp.concatenate(g_scores, axis=0)
    g_row = lax.broadcasted_iota(I32, (N_GROUPS, tm), 0)
    g_beaten = jnp.zeros((N_GROUPS, tm), I32)
    for g in range(N_GROUPS):
        r = g_score[g:g + 1, :]
        g_beaten += jnp.where((r > g_score) | ((r == g_score) & (g < g_row)), 1, 0)
    g_keep = g_beaten < TOPK_GROUPS
    masked = jnp.concatenate(
        [jnp.where(g_keep[g:g + 1, :], biased[g * GROUP_SIZE:(g + 1) * GROUP_SIZE], _NEG_INF)
         for g in range(N_GROUPS)], axis=0)

    onehot = jnp.zeros((N_EXPERTS, tm), F32)
    picks, pick_scores = [], []
    for _ in range(TOPK_EXPERTS):
        m = jnp.max(masked, axis=0, keepdims=True)
        idx = jnp.min(jnp.where(masked == m, row, float(N_EXPERTS)), axis=0, keepdims=True)
        hit = row == idx
        picks.append(idx)
        pick_scores.append(jnp.sum(jnp.where(hit, scores, 0.0), axis=0, keepdims=True))
        onehot = onehot + jnp.where(hit, 1.0, 0.0)
        masked = jnp.where(hit, _NEG_INF, masked)
    sel = jnp.concatenate(pick_scores, axis=0)
    e_ref[...] = jnp.concatenate(picks, axis=0).astype(I32)
    w_ref[...] = sel / jnp.sum(sel, axis=0, keepdims=True) * ROUTED_SCALE

    t_r = lax.broadcasted_iota(I32, (tm, tm), 0)
    t_c = lax.broadcasted_iota(I32, (tm, tm), 1)
    earlier = jnp.where(t_r < t_c, 1.0, 0.0).astype(BF16)
    seen = _dot(onehot.astype(BF16), earlier) + carry_ref[...]
    rk_ref[...] = jnp.concatenate(
        [jnp.sum(jnp.where(row == idx, seen, 0.0), axis=0, keepdims=True) for idx in picks], axis=0).astype(I32)
    carry_ref[...] += jnp.sum(onehot, axis=1, keepdims=True)
    cnt_ref[...] = carry_ref[...]


def _route(h_bf, w_router_t, b_router_col):
    k = TOPK_EXPERTS
    tok = lambda i: (0, i)
    return pl.pallas_call(
        _route_body,
        grid=(N_TOK // ROUTE_TM,),
        in_specs=[
            pl.BlockSpec((ROUTE_TM, D_MODEL), lambda i: (i, 0)),
            pl.BlockSpec((N_EXPERTS, D_MODEL), lambda i: (0, 0)),
            pl.BlockSpec((N_EXPERTS, 1), lambda i: (0, 0)),
        ],
        out_specs=[
            pl.BlockSpec((k, ROUTE_TM), tok),
            pl.BlockSpec((k, ROUTE_TM), tok),
            pl.BlockSpec((k, ROUTE_TM), tok),
            pl.BlockSpec((N_EXPERTS, 1), lambda i: (0, 0)),
        ],
        out_shape=[
            jax.ShapeDtypeStruct((k, N_TOK), I32),
            jax.ShapeDtypeStruct((k, N_TOK), F32),
            jax.ShapeDtypeStruct((k, N_TOK), I32),
            jax.ShapeDtypeStruct((N_EXPERTS, 1), F32),
        ],
        scratch_shapes=[pltpu.VMEM((N_EXPERTS, 1), F32)],
        compiler_params=_cparams("arbitrary"),
        name="route",
    )(h_bf, w_router_t, b_router_col)


def _dest_body(cnt_ref, e_ref, rk_ref, d_ref, te_ref, nu_ref, lt_ref):
    cnt = cnt_ref[...]
    tiles = jnp.floor((cnt + (ROW_TILE - 1)) * (1.0 / ROW_TILE))
    er = lax.broadcasted_iota(I32, (N_EXPERTS, N_EXPERTS), 0)
    ec = lax.broadcasted_iota(I32, (N_EXPERTS, N_EXPERTS), 1)
    before = jnp.where(ec < er, 1.0, 0.0).astype(BF16)
    tiles_b = jnp.broadcast_to(tiles, (N_EXPERTS, LANES)).astype(BF16)
    t_start = _dot(before, tiles_b)[:, 0:1]
    t_end = t_start + tiles
    p_start = t_start * float(ROW_TILE)
    lt_ref[...] = jnp.where(tiles > 0.0, (t_end - 1.0) * float(ROW_TILE), -1.0).astype(I32)

    row = lax.broadcasted_iota(I32, (N_EXPERTS, DEST_TM), 0)
    d_ref[...] = jnp.concatenate(
        [jnp.sum(jnp.where(row == e_ref[k:k + 1, :], p_start, 0.0), axis=0, keepdims=True)
         for k in range(TOPK_EXPERTS)], axis=0).astype(I32) + rk_ref[...]

    tile_id = lax.broadcasted_iota(I32, (N_EXPERTS, TILE_TABLE), 1).astype(F32)
    owner = jnp.sum(jnp.where(t_end <= tile_id, 1, 0), axis=0, keepdims=True)
    te_ref[...] = jnp.minimum(owner, N_EXPERTS - 1)
    nu_ref[...] = jnp.broadcast_to(t_end[N_EXPERTS - 1:N_EXPERTS, :], (1, LANES)).astype(I32)


def _dest(cnt, e_t, rk_t):
    k = TOPK_EXPERTS
    tok = lambda i: (0, i)
    return pl.pallas_call(
        _dest_body,
        grid=(N_TOK // DEST_TM,),
        in_specs=[
            pl.BlockSpec((N_EXPERTS, 1), lambda i: (0, 0)),
            pl.BlockSpec((k, DEST_TM), tok),
            pl.BlockSpec((k, DEST_TM), tok),
        ],
        out_specs=[
            pl.BlockSpec((k, DEST_TM), tok),
            pl.BlockSpec((1, TILE_TABLE), lambda i: (0, 0)),
            pl.BlockSpec((1, LANES), lambda i: (0, 0)),
            pl.BlockSpec((N_EXPERTS, 1), lambda i: (0, 0)),
        ],
        out_shape=[
            jax.ShapeDtypeStruct((k, N_TOK), I32),
            jax.ShapeDtypeStruct((1, TILE_TABLE), I32),
            jax.ShapeDtypeStruct((1, LANES), I32),
            jax.ShapeDtypeStruct((N_EXPERTS, 1), I32),
        ],
        compiler_params=_cparams("arbitrary"),
        name="dest",
    )(cnt, e_t, rk_t)


def _row_copy(src, dst, sem):
    return pltpu.make_async_copy(src, dst, sem)


def _dispatch_body(lt_ref, d_ref, h_ref, xs_ref, zero_ref, sem):
    @pl.when(pl.program_id(0) == 0)
    def _():
        zero_ref[...] = jnp.zeros_like(zero_ref)

        def tile_copy(e):
            r0 = pl.multiple_of(jnp.maximum(lt_ref[e], 0), ROW_TILE)
            return pltpu.make_async_copy(zero_ref, xs_ref.at[pl.ds(r0, ROW_TILE)], sem)

        def z_start(e, carry):
            @pl.when(lt_ref[e] >= 0)
            def _():
                tile_copy(e).start()
            return carry

        def z_wait(e, carry):
            @pl.when(lt_ref[e] >= 0)
            def _():
                tile_copy(e).wait()
            return carry

        lax.fori_loop(0, N_EXPERTS, z_start, 0)
        lax.fori_loop(0, N_EXPERTS, z_wait, 0)

    def issue(t, carry):
        for k in range(TOPK_EXPERTS):
            _row_copy(h_ref.at[t], xs_ref.at[d_ref[k, t]], sem).start(priority=k % 2)
        return carry

    lax.fori_loop(0, DISP_TM, issue, 0, unroll=DISPATCH_UNROLL)
    for k in range(TOPK_EXPERTS):
        pltpu.make_async_copy(h_ref, xs_ref.at[pl.ds(0, DISP_TM)], sem).wait()


def _dispatch(last_tile_row, dest_t, h1):
    grid_spec = pltpu.PrefetchScalarGridSpec(
        num_scalar_prefetch=1,
        grid=(N_TOK // DISP_TM,),
        in_specs=[
            pl.BlockSpec((TOPK_EXPERTS, DISP_TM), lambda i, lt: (0, i), memory_space=pltpu.SMEM),
            pl.BlockSpec((DISP_TM, PACKED_CHUNKS, LANES), lambda i, lt: (i, 0, 0)),
        ],
        out_specs=pl.BlockSpec(memory_space=pl.ANY),
        scratch_shapes=[pltpu.VMEM((ROW_TILE, PACKED_CHUNKS, LANES), U32), pltpu.SemaphoreType.DMA],
    )
    return pl.pallas_call(
        _dispatch_body,
        grid_spec=grid_spec,
        out_shape=jax.ShapeDtypeStruct((N_ROWS, PACKED_CHUNKS, LANES), U32),
        compiler_params=_cparams("arbitrary"),
        name="dispatch",
    )(last_tile_row, dest_t, h1)


def _row_destinations(dest_flat):
    info = plsc.get_sparse_core_info()
    n_cores, lanes = info.num_cores, info.num_lanes
    n_workers = n_cores * info.num_subcores
    rows_per_worker = -(-N_ROWS // (n_workers * ROW_TILE)) * ROW_TILE
    assert rows_per_worker % lanes == 0
    mesh = plsc.VectorSubcoreMesh(core_axis_name="c", subcore_axis_name="s")

    @functools.partial(
        pl.kernel, mesh=mesh, out_type=jax.ShapeDtypeStruct((n_workers * rows_per_worker,), I32),
        scratch_types=[pltpu.VMEM((SLOT_CHUNK,), I32), pltpu.VMEM((rows_per_worker,), I32)],
        compiler_params=pltpu.CompilerParams(needs_layout_passes=False), name="row_destinations")
    def invert(dest_hbm, out_hbm, dest_v, table_v):
        first_row = (lax.axis_index("s") * n_cores + lax.axis_index("c")) * rows_per_worker
        lane = lax.iota(I32, lanes)
        tile_shift = ROW_TILE.bit_length() - 1

        def spare(i, carry):
            row = first_row + i * lanes + lane
            buf = lax.rem(lax.shift_right_logical(row, tile_shift), EXPERT_BUFS)
            table_v[pl.ds(i * lanes, lanes)] = TOPK_EXPERTS * N_TOK + buf * ROW_TILE + (row & (ROW_TILE - 1))
            return carry

        lax.fori_loop(0, rows_per_worker // lanes, spare, 0)

        def chunk(c, carry):
            pltpu.sync_copy(dest_hbm.at[pl.ds(c * SLOT_CHUNK, SLOT_CHUNK)], dest_v)

            def vec(i, inner):
                local = dest_v[pl.ds(i * lanes, lanes)] - first_row
                mine = (local >= 0) & (local < rows_per_worker)
                plsc.store_scatter(table_v, [jnp.where(mine, local, 0)], lane + (c * SLOT_CHUNK + i * lanes),
                                   mask=mine)
                return inner

            lax.fori_loop(0, SLOT_CHUNK // lanes, vec, 0)
            return carry

        lax.fori_loop(0, N_ASSIGN // SLOT_CHUNK, chunk, 0)
        pltpu.sync_copy(table_v, out_hbm.at[pl.ds(first_row, rows_per_worker)])

    return invert(dest_flat)


def _experts_body(te_ref, nu_ref, lt_ref, slot_ref, xs_ref, wg_ref, wu_ref, wd_ref, yk_ref,
                  wg_f, wu_f, wd_f, wg_b, wu_b, wd_b, y_buf, n_loaded, sem, w_sem):
    i = pl.program_id(0)
    n_used = nu_ref[0]

    def weight_copies(e, s):
        return (pltpu.make_async_copy(wg_ref.at[e], wg_f.at[s], w_sem.at[s]),
                pltpu.make_async_copy(wu_ref.at[e], wu_f.at[s], w_sem.at[s]),
                pltpu.make_async_copy(wd_ref.at[e], wd_f.at[s], w_sem.at[s]))

    def wait_tile(b):
        pltpu.make_async_copy(y_buf.at[b], yk_ref.at[pl.ds(0, ROW_TILE), :], sem.at[b]).wait()

    def send_tile(b):
        for r in range(ROW_TILE):
            dst = slot_ref[0, 0, r]
            _row_copy(y_buf.at[b, pl.ds(r, 1), :], yk_ref.at[pl.ds(dst, 1), :], sem.at[b]).start(priority=r % 2)

    @pl.when(i == 0)
    def _():
        y_buf[EXPERT_BUFS - 1] = jnp.zeros((ROW_TILE, PACKED_COLS), U32)
        n_loaded[0] = 0
        for c in weight_copies(te_ref[0], 0):
            c.start()

    @pl.when((i >= 2) & (i <= n_used))
    def _():
        wait_tile(lax.rem(i, EXPERT_BUFS))

    @pl.when(i < n_used)
    def _():
        e = te_ref[i]
        prev = te_ref[jnp.maximum(i - 1, 0)]

        @pl.when((i == 0) | (e != prev))
        def _():
            s = lax.rem(n_loaded[0], 2)
            for c in weight_copies(e, s):
                c.wait()
            wg_b[...] = wg_f[s].astype(BF16)
            wu_b[...] = wu_f[s].astype(BF16)
            wd_b[...] = wd_f[s].astype(BF16)
            n_loaded[0] = n_loaded[0] + 1
            nxt = lax.shift_right_logical(lt_ref[e], ROW_TILE.bit_length() - 1) + 1

            @pl.when(nxt < n_used)
            def _():
                for c in weight_copies(te_ref[nxt], 1 - s):
                    c.start()

        send_tile(lax.rem(i + EXPERT_BUFS - 1, EXPERT_BUFS))
        packed = jnp.concatenate([xs_ref[:, c, :] for c in range(PACKED_CHUNKS)], axis=1)
        x_lo, x_hi = _unpack_bf16_pair(packed)
        x = jnp.concatenate([x_lo.astype(BF16), x_hi.astype(BF16)], axis=1)
        g = _dot(x, wg_b[...])
        u = _dot(x, wu_b[...])
        h = (g * _sigmoid(g)) * u
        y_buf[lax.rem(i, EXPERT_BUFS)] = _pack_bf16_pair(_dot(h.astype(BF16), wd_b[...]))

    @pl.when(i == n_used)
    def _():
        send_tile(lax.rem(i + EXPERT_BUFS - 1, EXPERT_BUFS))
        wait_tile(lax.rem(i + EXPERT_BUFS - 2, EXPERT_BUFS))
        wait_tile(lax.rem(i + EXPERT_BUFS - 1, EXPERT_BUFS))


def _experts(tile_expert, n_used, last_tile_row, row_dst, xs, w_gate, w_up, w_down):
    def tile(i, te, nu, lt):
        return (jnp.minimum(i, nu[0] - 1), 0, 0)

    n_table_tiles = row_dst.shape[0] // ROW_TILE
    placeholder = n_table_tiles - 1
    assert placeholder * ROW_TILE >= N_ROWS and placeholder % EXPERT_BUFS == EXPERT_BUFS - 1

    def prev_tile(i, te, nu, lt):
        return (jnp.where(i == 0, placeholder, jnp.minimum(i, nu[0]) - 1), 0, 0)

    grid_spec = pltpu.PrefetchScalarGridSpec(
        num_scalar_prefetch=3,
        grid=(N_ROW_TILES + 1,),
        in_specs=[
            pl.BlockSpec((1, 1, ROW_TILE), prev_tile, memory_space=pltpu.SMEM),
            pl.BlockSpec((ROW_TILE, PACKED_CHUNKS, LANES), tile),
            pl.BlockSpec(memory_space=pl.ANY),
            pl.BlockSpec(memory_space=pl.ANY),
            pl.BlockSpec(memory_space=pl.ANY),
        ],
        out_specs=pl.BlockSpec(memory_space=pl.ANY),
        scratch_shapes=[
            pltpu.VMEM((2, D_MODEL, D_EXPERT), F32),
            pltpu.VMEM((2, D_MODEL, D_EXPERT), F32),
            pltpu.VMEM((2, D_EXPERT, D_MODEL), F32),
            pltpu.VMEM((D_MODEL, D_EXPERT), BF16),
            pltpu.VMEM((D_MODEL, D_EXPERT), BF16),
            pltpu.VMEM((D_EXPERT, D_MODEL), BF16),
            pltpu.VMEM((EXPERT_BUFS, ROW_TILE, PACKED_COLS), U32),
            pltpu.SMEM((1,), I32),
            pltpu.SemaphoreType.DMA((EXPERT_BUFS,)),
            pltpu.SemaphoreType.DMA((2,)),
        ],
    )
    return pl.pallas_call(
        _experts_body,
        grid_spec=grid_spec,
        out_shape=jax.ShapeDtypeStruct((YK_ROWS, PACKED_COLS), U32),
        compiler_params=_cparams("arbitrary"),
        name="experts",
    )(tile_expert, n_used, last_tile_row, row_dst.reshape(n_table_tiles, 1, ROW_TILE), xs, w_gate, w_up, w_down)


def _final_body(w_ref, yk_ref, h_ref, hb_ref, p_ref, wsg_ref, wsu_ref, wsd_ref, wpl_ref, wpg_ref,
                g_ref, b_ref, o_ref):
    hb = hb_ref[...]
    sg = _dot(hb, wsg_ref[...])
    shared = _dot(((sg * _sigmoid(sg)) * _dot(hb, wsu_ref[...])).astype(BF16), wsd_ref[...])
    ple = _sigmoid(_dot(hb, wpg_ref[...])) * _dot(p_ref[...].astype(BF16), wpl_ref[...])

    w_col = w_ref[...].T
    y_lo, y_hi = _unpack_bf16_pair(yk_ref[0])
    r_lo, r_hi = y_lo * w_col[:, 0:1], y_hi * w_col[:, 0:1]
    for k in range(1, TOPK_EXPERTS):
        y_lo, y_hi = _unpack_bf16_pair(yk_ref[k])
        r_lo, r_hi = r_lo + y_lo * w_col[:, k:k + 1], r_hi + y_hi * w_col[:, k:k + 1]
    routed = jnp.concatenate([r_lo, r_hi], axis=1)
    o_ref[...] = _layer_norm(ALPHA * h_ref[...] + (routed + shared) + ple, g_ref[...], b_ref[...])


def _final(w_t, yk, h1, h_bf, p2d, w_s_gate, w_s_up, w_s_down, w_ple, w_ple_gate, ln_g, ln_b):
    d = D_MODEL
    full = lambda i: (0, 0)
    return pl.pallas_call(
        _final_body,
        grid=(N_TOK // FINAL_TM,),
        in_specs=[
            pl.BlockSpec((TOPK_EXPERTS, FINAL_TM), lambda i: (0, i)),
            pl.BlockSpec((TOPK_EXPERTS, FINAL_TM, PACKED_COLS), lambda i: (0, i, 0)),
            pl.BlockSpec((FINAL_TM, d), lambda i: (i, 0)),
            pl.BlockSpec((FINAL_TM, d), lambda i: (i, 0)),
            pl.BlockSpec((FINAL_TM, PLE_DIM), lambda i: (i, 0)),
            pl.BlockSpec((d, D_SHARED), full),
            pl.BlockSpec((d, D_SHARED), full),
            pl.BlockSpec((D_SHARED, d), full),
            pl.BlockSpec((PLE_DIM, d), full),
            pl.BlockSpec((d, d), full),
            pl.BlockSpec((1, d), full),
            pl.BlockSpec((1, d), full),
        ],
        out_specs=pl.BlockSpec((FINAL_TM, d), lambda i: (i, 0)),
        out_shape=jax.ShapeDtypeStruct((N_TOK, d), F32),
        compiler_params=_cparams("parallel"),
        name="combine_ln2",
    )(w_t, yk.reshape(TOPK_EXPERTS + 1, N_TOK, PACKED_COLS), h1, h_bf, p2d, w_s_gate, w_s_up, w_s_down,
      w_ple, w_ple_gate, ln_g, ln_b)


def _rope_tables(positions):
    half = ROT_DIM // 2
    inv = ROPE_THETA ** (-jnp.arange(0, ROT_DIM, 2, dtype=F32) / ROT_DIM)
    ang = positions.astype(F32)[..., None] * inv
    cos, sin = jnp.cos(ang), jnp.sin(ang)
    rest = MOBA_DH - ROT_DIM
    ones = jnp.ones(cos.shape[:-1] + (rest,), F32)
    zeros = jnp.zeros(cos.shape[:-1] + (rest,), F32)
    return (jnp.concatenate([cos, cos, ones], axis=-1), jnp.concatenate([-sin, sin, zeros], axis=-1))


def _layer(h2d, p2d, cos_t, sin_t, w_in, w_gk2, b_gk, norm_g, w_gla_o, w_moba_o, w_out, ln1_g, ln1_b,
           w_router, b_router, w_e_gate, w_e_up, w_e_down, w_s_gate, w_s_up, w_s_down, w_ple, w_ple_gate,
           ln2_g, ln2_b):
    low_end = LOW_SRC + GLA_RANK
    w_cat = jnp.concatenate(
        [w_in[:, :LOW_SRC], w_in[:, low_end:], w_in[:, LOW_SRC:low_end],
         jnp.zeros((D_MODEL, LANES - GLA_RANK), w_in.dtype)], axis=1).astype(BF16)
    w_gk2_pad = jnp.concatenate([w_gk2, jnp.zeros((LANES - GLA_RANK, GLA_KDIM), w_gk2.dtype)], axis=0)

    proj = _in_proj(h2d, w_cat)
    gla_out = _gla(proj, w_gk2_pad, b_gk[None, :], norm_g[None, :])
    moba_out = _moba(proj, cos_t, sin_t)
    h1, h_bf, h_packed = _mix(gla_out, moba_out, proj, h2d, w_gla_o.astype(BF16), w_moba_o.astype(BF16),
                              w_out.astype(BF16), ln1_g[None, :], ln1_b[None, :])
    e_t, w_t, rk_t, cnt = _route(h_bf, w_router.T.astype(BF16), b_router[:, None])
    dest_t, tile_expert, n_used, last_tile_row = _dest(cnt, e_t, rk_t)
    last_tile_row = last_tile_row.reshape(N_EXPERTS)
    xs = _dispatch(last_tile_row, dest_t, h_packed)
    row_dst = _row_destinations(dest_t.reshape(N_ASSIGN))
    yk = _experts(tile_expert.reshape(TILE_TABLE), n_used[0, 0:1], last_tile_row, row_dst, xs,
                  w_e_gate, w_e_up, w_e_down)
    return _final(w_t, yk, h1, h_bf, p2d, w_s_gate.astype(BF16), w_s_up.astype(BF16),
                  w_s_down.astype(BF16), w_ple.astype(BF16), w_ple_gate.astype(BF16),
                  ln2_g[None, :], ln2_b[None, :])


def kernel(x, p, positions, w_in, w_gla_gk2, b_gla_gk, gla_norm_g, w_gla_o, w_moba_o, w_out, ln1_g, ln1_b,
           w_router, b_router, w_e_gate, w_e_up, w_e_down, w_s_gate, w_s_up, w_s_down, w_ple, w_ple_gate,
           ln2_g, ln2_b):
    cos_t, sin_t = _rope_tables(positions)
    h = x.reshape(N_TOK, D_MODEL)
    for i in range(DEPTH):
        h = _layer(h, p[i].reshape(N_TOK, PLE_DIM), cos_t, sin_t, w_in[i], w_gla_gk2[i], b_gla_gk[i],
                   gla_norm_g[i], w_gla_o[i], w_moba_o[i], w_out[i], ln1_g[i], ln1_b[i], w_router[i],
                   b_router[i], w_e_gate[i], w_e_up[i], w_e_down[i], w_s_gate[i], w_s_up[i], w_s_down[i],
                   w_ple[i], w_ple_gate[i], ln2_g[i], ln2_b[i])
    return h.reshape(BATCH, SEQ, D_MODEL)
```

```python
import functools

import jax
import jax.numpy as jnp
from jax import lax
from jax.experimental import pallas as pl
from jax.experimental.pallas import tpu as pltpu
from jax.experimental.pallas import tpu_sc as plsc

F32 = jnp.float32
BF16 = jnp.bfloat16
I32 = jnp.int32
U32 = jnp.uint32

LANES = 128
SUBLANES = 8
VMEM_LIMIT_BYTES = 48 * 1024 * 1024

D_MODEL = 1024
BATCH = 8
SEQ = 2048
N_TOK = BATCH * SEQ
GLA_HEADS = 4
GLA_DK = 128
GLA_DV = 256
GLA_RANK = 16
GLA_NORMALIZER = 16.0
GLA_CHUNK = 64
GLA_GROUP = 4 * GLA_CHUNK
MOBA_HEADS = 8
MOBA_DH = 128
MOBA_BLOCK = 256
MOBA_TOPK = 3
ROT_DIM = 32
ROPE_THETA = 500000.0
N_EXPERTS = 256
TOPK_EXPERTS = 8
N_GROUPS = 8
GROUP_SIZE = N_EXPERTS // N_GROUPS
TOPK_GROUPS = 4
D_EXPERT = 256
D_SHARED = 256
ROUTED_SCALE = 2.5
PLE_DIM = 256
LN_EPS = 1e-5
DEPTH = 1
ALPHA = (2.0 * DEPTH) ** 0.25
GLA_KDIM = GLA_HEADS * GLA_DK
GLA_VDIM = GLA_HEADS * GLA_DV
MOBA_DIM = MOBA_HEADS * MOBA_DH
N_KBLK = SEQ // MOBA_BLOCK

COL_GQ = 0
COL_GK = COL_GQ + GLA_KDIM
COL_GV = COL_GK + GLA_KDIM
COL_GR = COL_GV + GLA_VDIM
COL_MQ = COL_GR + GLA_VDIM
COL_MK = COL_MQ + MOBA_DIM
COL_MV = COL_MK + MOBA_DIM
COL_GA = COL_MV + MOBA_DIM
COL_GB = COL_GA + D_MODEL
COL_LOW = COL_GB + D_MODEL
PROJ_COLS = COL_LOW + LANES
LOW_SRC = 2 * GLA_KDIM + 2 * GLA_VDIM

ROW_TILE = 256
PACKED_COLS = D_MODEL // 2
PACKED_CHUNKS = PACKED_COLS // LANES
EXPERT_BUFS = 3
YK_ROWS = (TOPK_EXPERTS + 1) * N_TOK
SLOT_CHUNK = 4096
assert EXPERT_BUFS * ROW_TILE <= N_TOK
N_ASSIGN = N_TOK * TOPK_EXPERTS
N_ROW_TILES = (N_ASSIGN + N_EXPERTS * (ROW_TILE - 1) + ROW_TILE - 1) // ROW_TILE
N_ROWS = N_ROW_TILES * ROW_TILE
TILE_TABLE = -(-N_ROW_TILES // LANES) * LANES

PROJ_TM = 1024
PROJ_TN = PROJ_COLS // 5
MIX_TM = 512
ROUTE_TM = 512
DEST_TM = 2048
DISP_TM = 256
DISPATCH_UNROLL = 4
FINAL_TM = 256

_NEG_INF = float("-inf")
LOG2_E = 1.4426950408889634


def _cparams(*sem):
    return pltpu.CompilerParams(dimension_semantics=sem, vmem_limit_bytes=VMEM_LIMIT_BYTES)


def _dot(a, b):
    return jnp.dot(a, b, preferred_element_type=F32)


def _dot_nt(a, b):
    return lax.dot_general(a, b, (((1,), (1,)), ((), ())), preferred_element_type=F32)


def _dot_tn(a, b):
    return lax.dot_general(a, b, (((0,), (0,)), ((), ())), preferred_element_type=F32)


def _split_bf16(x):
    hi = x.astype(BF16)
    lo = (x - hi.astype(F32)).astype(BF16)
    return hi, lo


def _sigmoid(x):
    return 1.0 / (1.0 + jnp.exp(-x))


def _pack_bf16_pair(x):
    lo = lax.bitcast_convert_type(x[:, :PACKED_COLS].astype(BF16).astype(F32), U32)
    hi = lax.bitcast_convert_type(x[:, PACKED_COLS:].astype(BF16).astype(F32), U32)
    return hi | lax.shift_right_logical(lo, jnp.uint32(16))


def _unpack_bf16_pair(w):
    lo = lax.bitcast_convert_type(lax.shift_left(w, jnp.uint32(16)), F32)
    hi = lax.bitcast_convert_type(w & jnp.uint32(0xFFFF0000), F32)
    return lo, hi


def _repack_body(a_ref, b_ref, o_ref):
    t = pl.program_id(0)
    a = a_ref[...].astype(BF16)

    @pl.when(t < LOW_SRC // LANES)
    def _():
        o_ref[...] = a

    @pl.when((t >= LOW_SRC // LANES) & (t < COL_LOW // LANES))
    def _():
        o_ref[...] = jnp.concatenate([a[:, GLA_RANK:], b_ref[:, :GLA_RANK].astype(BF16)], axis=1)

    @pl.when(t == COL_LOW // LANES)
    def _():
        o_ref[...] = jnp.concatenate([a[:, :GLA_RANK], jnp.zeros((D_MODEL, LANES - GLA_RANK), BF16)], axis=1)


def _repack_w_in(w_in):
    low_tile = LOW_SRC // LANES
    last_out = COL_LOW // LANES
    return pl.pallas_call(
        _repack_body,
        grid=(PROJ_COLS // LANES,),
        in_specs=[
            pl.BlockSpec((D_MODEL, LANES), lambda t: (0, jnp.where(t == last_out, low_tile, t))),
            pl.BlockSpec((D_MODEL, LANES), lambda t: (0, jnp.minimum(t + 1, last_out))),
        ],
        out_specs=pl.BlockSpec((D_MODEL, LANES), lambda t: (0, t)),
        out_shape=jax.ShapeDtypeStruct((D_MODEL, PROJ_COLS), BF16),
        compiler_params=_cparams("parallel"),
        name="repack_w_in",
    )(w_in, w_in)


def _proj_body(x_ref, w_ref, o_ref):
    o_ref[...] = _dot(x_ref[...].astype(BF16), w_ref[...]).astype(BF16)


def _in_proj(x2d, w_cat):
    return pl.pallas_call(
        _proj_body,
        grid=(PROJ_COLS // PROJ_TN, N_TOK // PROJ_TM),
        in_specs=[
            pl.BlockSpec((PROJ_TM, D_MODEL), lambda j, i: (i, 0)),
            pl.BlockSpec((D_MODEL, PROJ_TN), lambda j, i: (0, j)),
        ],
        out_specs=pl.BlockSpec((PROJ_TM, PROJ_TN), lambda j, i: (i, j)),
        out_shape=jax.ShapeDtypeStruct((N_TOK, PROJ_COLS), BF16),
        compiler_params=_cparams("parallel", "parallel"),
        name="in_proj",
    )(x2d, w_cat)


def _gla_body(q_ref, k_ref, v_ref, r_ref, low_ref, wg_ref, bg_ref, ng_ref, o_ref, st_ref, gk_ref):
    w_hi, w_lo = _split_bf16(wg_ref[...])
    low = low_ref[...]
    lin = _dot(low, w_hi) + _dot(low, w_lo) + bg_ref[...]
    gk_ref[...] = (jnp.minimum(lin, 0.0) - jnp.log1p(jnp.exp(-jnp.abs(lin)))) * (1.0 / GLA_NORMALIZER)
    st_ref[...] = jnp.zeros_like(st_ref)

    ri = lax.broadcasted_iota(I32, (GLA_GROUP, GLA_GROUP), 0)
    ci = lax.broadcasted_iota(I32, (GLA_GROUP, GLA_GROUP), 1)
    same_chunk = lax.shift_right_logical(ri, GLA_CHUNK.bit_length() - 1) == lax.shift_right_logical(
        ci, GLA_CHUNK.bit_length() - 1)
    causal = same_chunk & (ri >= ci)
    sums = jnp.concatenate([jnp.where(causal, 1.0, 0.0), jnp.where(same_chunk, 1.0, 0.0)], axis=0).astype(BF16)
    gain = ng_ref[...]

    def group(c, carry):
        rows = pl.ds(pl.multiple_of(c * GLA_GROUP, GLA_GROUP), GLA_GROUP)
        g_hi, g_lo = _split_bf16(gk_ref[rows, :])
        bb = _dot(sums, g_hi) + _dot(sums, g_lo)
        b = bb[0:GLA_GROUP]
        b_end = bb[GLA_GROUP:2 * GLA_GROUP]
        q = q_ref[rows, :].astype(F32) * (GLA_DK ** -0.5)
        k = k_ref[rows, :].astype(F32)
        v = v_ref[rows, :]
        q_e = (q * jnp.exp(b)).astype(BF16)
        k_e = (k * jnp.exp(-b)).astype(BF16)
        k_d = (k * jnp.exp(b_end - b)).astype(BF16)
        att = jnp.where(causal, _dot_nt(q_e, k_e), 0.0)
        o = _dot(att.astype(BF16), v)
        st = st_ref[...]
        inter = []
        for j in range(GLA_GROUP // GLA_CHUNK):
            cr = slice(j * GLA_CHUNK, (j + 1) * GLA_CHUNK)
            inter.append(_dot_nt(q_e[cr], st.astype(BF16)))
            st = st * jnp.exp(b_end[j * GLA_CHUNK:j * GLA_CHUNK + 1, :]) + _dot_tn(v[cr], k_d[cr])
        st_ref[...] = st
        o = o + jnp.concatenate(inter, axis=0)
        o = o * lax.rsqrt(jnp.mean(o * o, axis=-1, keepdims=True) + LN_EPS) * gain
        r = r_ref[rows, :].astype(F32)
        o_ref[rows, :] = (o * (r * _sigmoid(r))).astype(BF16)
        return carry

    lax.fori_loop(0, SEQ // GLA_GROUP, group, 0)


def _gla(proj, w_gk2_pad, b_gk, norm_g):
    kb, vb = GLA_DK, GLA_DV
    return pl.pallas_call(
        _gla_body,
        grid=(BATCH, GLA_HEADS),
        in_specs=[
            pl.BlockSpec((SEQ, kb), lambda b, h: (b, COL_GQ // kb + h)),
            pl.BlockSpec((SEQ, kb), lambda b, h: (b, COL_GK // kb + h)),
            pl.BlockSpec((SEQ, vb), lambda b, h: (b, COL_GV // vb + h)),
            pl.BlockSpec((SEQ, vb), lambda b, h: (b, COL_GR // vb + h)),
            pl.BlockSpec((SEQ, LANES), lambda b, h: (b, COL_LOW // LANES)),
            pl.BlockSpec((LANES, kb), lambda b, h: (0, h)),
            pl.BlockSpec((1, kb), lambda b, h: (0, h)),
            pl.BlockSpec((1, vb), lambda b, h: (0, 0)),
        ],
        out_specs=pl.BlockSpec((SEQ, vb), lambda b, h: (b, h)),
        out_shape=jax.ShapeDtypeStruct((N_TOK, GLA_VDIM), BF16),
        scratch_shapes=[pltpu.VMEM((vb, kb), F32), pltpu.VMEM((SEQ, kb), F32)],
        compiler_params=_cparams("parallel", "parallel"),
        name="gla",
    )(proj, proj, proj, proj, proj, w_gk2_pad, b_gk, norm_g)


def _moba_body(q_ref, k_ref, v_ref, c_ref, s_ref, o_ref, qs_ref, ks_ref, vt_ref, bias_ref,
               sc_a_ref, sc_b_ref, pr_a_ref, pr_b_ref):
    cos_t = c_ref[0]
    sin_t = s_ref[0]
    lane = lax.broadcasted_iota(I32, (SEQ, MOBA_DH), 1)
    half = ROT_DIM // 2

    def rope(x):
        partner = jnp.where(lane < half, pltpu.roll(x, MOBA_DH - half, 1), pltpu.roll(x, half, 1))
        return x * cos_t + partner * sin_t

    q = rope(q_ref[...].astype(F32))
    k = rope(k_ref[...].astype(F32))
    q_hi, q_lo = _split_bf16(q)
    qs_ref[...] = (q * (MOBA_DH ** -0.5 * LOG2_E)).astype(BF16)
    ks_ref[...] = k.astype(BF16)
    vt_ref[...] = v_ref[...].astype(F32).T.astype(BF16)

    k_mean = jnp.concatenate(
        [jnp.mean(k[j * MOBA_BLOCK:(j + 1) * MOBA_BLOCK], axis=0, keepdims=True) for j in range(N_KBLK)], axis=0)
    m_hi, m_lo = _split_bf16(k_mean)
    s_blk = _dot_nt(m_hi, q_hi) + _dot_nt(m_hi, q_lo) + _dot_nt(m_lo, q_hi)
    blk = lax.broadcasted_iota(I32, (N_KBLK, SEQ), 0)
    q_blk = lax.shift_right_logical(lax.broadcasted_iota(I32, (N_KBLK, SEQ), 1), MOBA_BLOCK.bit_length() - 1)
    past = blk < q_blk
    s_blk = jnp.where(past, s_blk, _NEG_INF)
    beaten = jnp.zeros((N_KBLK, SEQ), I32)
    for j in range(N_KBLK):
        row = s_blk[j:j + 1, :]
        beaten += jnp.where((row > s_blk) | ((row == s_blk) & (j < blk)), 1, 0)
    bias_ref[...] = jnp.where(past & (beaten < MOBA_TOPK), 0.0, _NEG_INF)

    kr = lax.broadcasted_iota(I32, (MOBA_BLOCK, MOBA_BLOCK), 0)
    qc = lax.broadcasted_iota(I32, (MOBA_BLOCK, MOBA_BLOCK), 1)
    own_bias = jnp.where(kr <= qc, 0.0, _NEG_INF)

    sc_bufs = (sc_a_ref, sc_b_ref)
    pr_bufs = (pr_a_ref, pr_b_ref)
    zeros_row = jnp.zeros((1, MOBA_BLOCK), F32)
    future_row = jnp.full((1, MOBA_BLOCK), _NEG_INF, F32)
    for pair in range(N_KBLK // 2):
        q_blocks = (2 * pair, 2 * pair + 1)
        cols = slice(q_blocks[0] * MOBA_BLOCK, (q_blocks[1] + 1) * MOBA_BLOCK)
        n_kblk = q_blocks[1] + 1
        n_keys = n_kblk * MOBA_BLOCK
        sc, pr = sc_bufs[pair % 2], pr_bufs[pair % 2]
        q_pair = qs_ref[cols, :]

        def query_bias(j):
            halves = []
            for qb in q_blocks:
                q_cols = slice(qb * MOBA_BLOCK, (qb + 1) * MOBA_BLOCK)
                halves.append(bias_ref[j:j + 1, q_cols] if j < qb else zeros_row if j == qb else future_row)
            return jnp.concatenate(halves, axis=1)

        biases = [query_bias(j) for j in range(n_kblk)]
        col_max = []
        for j in range(n_kblk):
            rows = slice(j * MOBA_BLOCK, (j + 1) * MOBA_BLOCK)
            s = _dot_nt(ks_ref[rows, :], q_pair)
            if j in q_blocks:
                h = q_blocks.index(j)
                own = s[:, h * MOBA_BLOCK:(h + 1) * MOBA_BLOCK] + own_bias
                s = jnp.concatenate([own, s[:, MOBA_BLOCK:]] if h == 0 else [s[:, :MOBA_BLOCK], own], axis=1)
            sc[rows, :] = s
            col_max.append(jnp.max(s, axis=0, keepdims=True) + biases[j])
        m = functools.reduce(jnp.maximum, col_max)
        denom = jnp.zeros((1, 2 * MOBA_BLOCK), F32)
        for j in range(n_kblk):
            rows = slice(j * MOBA_BLOCK, (j + 1) * MOBA_BLOCK)
            p = jnp.exp2(sc[rows, :] - (m - biases[j]))
            denom = denom + jnp.sum(p, axis=0, keepdims=True)
            pr[rows, :] = p.astype(BF16)
        o_t = _dot(vt_ref[:, 0:n_keys], pr[0:n_keys, :]) * (1.0 / denom)
        o_ref[cols, :] = o_t.T.astype(BF16)


def _moba(proj, cos_t, sin_t):
    dh = MOBA_DH
    return pl.pallas_call(
        _moba_body,
        grid=(BATCH, MOBA_HEADS),
        in_specs=[
            pl.BlockSpec((SEQ, dh), lambda b, h: (b, COL_MQ // dh + h)),
            pl.BlockSpec((SEQ, dh), lambda b, h: (b, COL_MK // dh + h)),
            pl.BlockSpec((SEQ, dh), lambda b, h: (b, COL_MV // dh + h)),
            pl.BlockSpec((1, SEQ, dh), lambda b, h: (b, 0, 0)),
            pl.BlockSpec((1, SEQ, dh), lambda b, h: (b, 0, 0)),
        ],
        out_specs=pl.BlockSpec((SEQ, dh), lambda b, h: (b, h)),
        out_shape=jax.ShapeDtypeStruct((N_TOK, MOBA_DIM), BF16),
        scratch_shapes=[
            pltpu.VMEM((SEQ, dh), BF16),
            pltpu.VMEM((SEQ, dh), BF16),
            pltpu.VMEM((dh, SEQ), BF16),
            pltpu.VMEM((N_KBLK, SEQ), F32),
            pltpu.VMEM((SEQ, 2 * MOBA_BLOCK), F32),
            pltpu.VMEM((SEQ, 2 * MOBA_BLOCK), F32),
            pltpu.VMEM((SEQ, 2 * MOBA_BLOCK), BF16),
            pltpu.VMEM((SEQ, 2 * MOBA_BLOCK), BF16),
        ],
        compiler_params=_cparams("parallel", "parallel"),
        name="moba",
    )(proj, proj, proj, cos_t, sin_t)


def _layer_norm(z, g, b):
    mu = jnp.mean(z, axis=-1, keepdims=True)
    zc = z - mu
    var = jnp.mean(zc * zc, axis=-1, keepdims=True)
    return zc * lax.rsqrt(var + LN_EPS) * g + b


def _mix_body(gla_ref, moba_ref, ga_ref, gb_ref, x_ref, wgo_ref, wmo_ref, wo_ref, g_ref, b_ref,
              h_ref, hb_ref, hp_ref):
    y_gla = _dot(gla_ref[...], wgo_ref[...])
    y_moba = _dot(moba_ref[...], wmo_ref[...])
    merged = _sigmoid(ga_ref[...].astype(F32)) * y_gla + _sigmoid(gb_ref[...].astype(F32)) * y_moba
    mix = _dot(merged.astype(BF16), wo_ref[...])
    h = _layer_norm(ALPHA * x_ref[...] + mix, g_ref[...], b_ref[...])
    h_ref[...] = h
    hb_ref[...] = h.astype(BF16)
    packed = _pack_bf16_pair(h)
    for c in range(PACKED_CHUNKS):
        hp_ref[:, c, :] = packed[:, c * LANES:(c + 1) * LANES]


def _mix(gla_out, moba_out, proj, x2d, w_gla_o, w_moba_o, w_out, ln_g, ln_b):
    d = D_MODEL
    row = lambda i: (i, 0)
    full = lambda i: (0, 0)
    return pl.pallas_call(
        _mix_body,
        grid=(N_TOK // MIX_TM,),
        in_specs=[
            pl.BlockSpec((MIX_TM, d), row),
            pl.BlockSpec((MIX_TM, d), row),
            pl.BlockSpec((MIX_TM, d), lambda i: (i, COL_GA // d)),
            pl.BlockSpec((MIX_TM, d), lambda i: (i, COL_GB // d)),
            pl.BlockSpec((MIX_TM, d), row),
            pl.BlockSpec((d, d), full),
            pl.BlockSpec((d, d), full),
            pl.BlockSpec((d, d), full),
            pl.BlockSpec((1, d), full),
            pl.BlockSpec((1, d), full),
        ],
        out_specs=[
            pl.BlockSpec((MIX_TM, d), row),
            pl.BlockSpec((MIX_TM, d), row),
            pl.BlockSpec((MIX_TM, PACKED_CHUNKS, LANES), lambda i: (i, 0, 0)),
        ],
        out_shape=[
            jax.ShapeDtypeStruct((N_TOK, d), F32),
            jax.ShapeDtypeStruct((N_TOK, d), BF16),
            jax.ShapeDtypeStruct((N_TOK, PACKED_CHUNKS, LANES), U32),
        ],
        compiler_params=_cparams("parallel"),
        name="mix_ln1",
    )(gla_out, moba_out, proj, proj, x2d, w_gla_o, w_moba_o, w_out, ln_g, ln_b)


def _route_body(h_ref, wr_ref, br_ref, e_ref, w_ref, rk_ref, cnt_ref, carry_ref):
    tm = ROUTE_TM

    @pl.when(pl.program_id(0) == 0)
    def _():
        carry_ref[...] = jnp.zeros_like(carry_ref)

    scores = _sigmoid(_dot_nt(wr_ref[...], h_ref[...]))
    biased = scores + br_ref[...]
    row = lax.broadcasted_iota(I32, (N_EXPERTS, tm), 0).astype(F32)
    row_g = lax.broadcasted_iota(I32, (GROUP_SIZE, tm), 0).astype(F32)

    g_scores = []
    for g in range(N_GROUPS):
        grp = biased[g * GROUP_SIZE:(g + 1) * GROUP_SIZE]
        m1 = jnp.max(grp, axis=0, keepdims=True)
        first = jnp.min(jnp.where(grp == m1, row_g, float(GROUP_SIZE)), axis=0, keepdims=True)
        m2 = jnp.max(jnp.where(row_g == first, _NEG_INF, grp), axis=0, keepdims=True)
        g_scores.append(m1 + m2)
    g_score = jnp.concatenate(g_scores, axis=0)
    g_row = lax.broadcasted_iota(I32, (N_GROUPS, tm), 0)
    g_beaten = jnp.zeros((N_GROUPS, tm), I32)
    for g in range(N_GROUPS):
        r = g_score[g:g + 1, :]
        g_beaten += jnp.where((r > g_score) | ((r == g_score) & (g < g_row)), 1, 0)
    g_keep = g_beaten < TOPK_GROUPS
    masked = jnp.concatenate(
        [jnp.where(g_keep[g:g + 1, :], biased[g * GROUP_SIZE:(g + 1) * GROUP_SIZE], _NEG_INF)
         for g in range(N_GROUPS)], axis=0)

    onehot = jnp.zeros((N_EXPERTS, tm), F32)
    picks, pick_scores = [], []
    for _ in range(TOPK_EXPERTS):
        m = jnp.max(masked, axis=0, keepdims=True)
        idx = jnp.min(jnp.where(masked == m, row, float(N_EXPERTS)), axis=0, keepdims=True)
        hit = row == idx
        picks.append(idx)
        pick_scores.append(jnp.sum(jnp.where(hit, scores, 0.0), axis=0, keepdims=True))
        onehot = onehot + jnp.where(hit, 1.0, 0.0)
        masked = jnp.where(hit, _NEG_INF, masked)
    sel = jnp.concatenate(pick_scores, axis=0)
    e_ref[...] = jnp.concatenate(picks, axis=0).astype(I32)
    w_ref[...] = sel / jnp.sum(sel, axis=0, keepdims=True) * ROUTED_SCALE

    t_r = lax.broadcasted_iota(I32, (tm, tm), 0)
    t_c = lax.broadcasted_iota(I32, (tm, tm), 1)
    earlier = jnp.where(t_r < t_c, 1.0, 0.0).astype(BF16)
    seen = _dot(onehot.astype(BF16), earlier) + carry_ref[...]
    rk_ref[...] = jnp.concatenate(
        [jnp.sum(jnp.where(row == idx, seen, 0.0), axis=0, keepdims=True) for idx in picks], axis=0).astype(I32)
    carry_ref[...] += jnp.sum(onehot, axis=1, keepdims=True)
    cnt_ref[...] = carry_ref[...]


def _route(h_bf, w_router_t, b_router_col):
    k = TOPK_EXPERTS
    tok = lambda i: (0, i)
    return pl.pallas_call(
        _route_body,
        grid=(N_TOK // ROUTE_TM,),
        in_specs=[
            pl.BlockSpec((ROUTE_TM, D_MODEL), lambda i: (i, 0)),
            pl.BlockSpec((N_EXPERTS, D_MODEL), lambda i: (0, 0)),
            pl.BlockSpec((N_EXPERTS, 1), lambda i: (0, 0)),
        ],
        out_specs=[
            pl.BlockSpec((k, ROUTE_TM), tok),
            pl.BlockSpec((k, ROUTE_TM), tok),
            pl.BlockSpec((k, ROUTE_TM), tok),
            pl.BlockSpec((N_EXPERTS, 1), lambda i: (0, 0)),
        ],
        out_shape=[
            jax.ShapeDtypeStruct((k, N_TOK), I32),
            jax.ShapeDtypeStruct((k, N_TOK), F32),
            jax.ShapeDtypeStruct((k, N_TOK), I32),
            jax.ShapeDtypeStruct((N_EXPERTS, 1), F32),
        ],
        scratch_shapes=[pltpu.VMEM((N_EXPERTS, 1), F32)],
        compiler_params=_cparams("arbitrary"),
        name="route",
    )(h_bf, w_router_t, b_router_col)


def _dest_body(cnt_ref, e_ref, rk_ref, d_ref, te_ref, nu_ref, lt_ref):
    cnt = cnt_ref[...]
    tiles = jnp.floor((cnt + (ROW_TILE - 1)) * (1.0 / ROW_TILE))
    er = lax.broadcasted_iota(I32, (N_EXPERTS, N_EXPERTS), 0)
    ec = lax.broadcasted_iota(I32, (N_EXPERTS, N_EXPERTS), 1)
    before = jnp.where(ec < er, 1.0, 0.0).astype(BF16)
    tiles_b = jnp.broadcast_to(tiles, (N_EXPERTS, LANES)).astype(BF16)
    t_start = _dot(before, tiles_b)[:, 0:1]
    t_end = t_start + tiles
    p_start = t_start * float(ROW_TILE)
    lt_ref[...] = jnp.where(tiles > 0.0, (t_end - 1.0) * float(ROW_TILE), -1.0).astype(I32)

    row = lax.broadcasted_iota(I32, (N_EXPERTS, DEST_TM), 0)
    d_ref[...] = jnp.concatenate(
        [jnp.sum(jnp.where(row == e_ref[k:k + 1, :], p_start, 0.0), axis=0, keepdims=True)
         for k in range(TOPK_EXPERTS)], axis=0).astype(I32) + rk_ref[...]

    tile_id = lax.broadcasted_iota(I32, (N_EXPERTS, TILE_TABLE), 1).astype(F32)
    owner = jnp.sum(jnp.where(t_end <= tile_id, 1, 0), axis=0, keepdims=True)
    te_ref[...] = jnp.minimum(owner, N_EXPERTS - 1)
    nu_ref[...] = jnp.broadcast_to(t_end[N_EXPERTS - 1:N_EXPERTS, :], (1, LANES)).astype(I32)


def _dest(cnt, e_t, rk_t):
    k = TOPK_EXPERTS
    tok = lambda i: (0, i)
    return pl.pallas_call(
        _dest_body,
        grid=(N_TOK // DEST_TM,),
        in_specs=[
            pl.BlockSpec((N_EXPERTS, 1), lambda i: (0, 0)),
            pl.BlockSpec((k, DEST_TM), tok),
            pl.BlockSpec((k, DEST_TM), tok),
        ],
        out_specs=[
            pl.BlockSpec((k, DEST_TM), tok),
            pl.BlockSpec((1, TILE_TABLE), lambda i: (0, 0)),
            pl.BlockSpec((1, LANES), lambda i: (0, 0)),
            pl.BlockSpec((N_EXPERTS, 1), lambda i: (0, 0)),
        ],
        out_shape=[
            jax.ShapeDtypeStruct((k, N_TOK), I32),
            jax.ShapeDtypeStruct((1, TILE_TABLE), I32),
            jax.ShapeDtypeStruct((1, LANES), I32),
            jax.ShapeDtypeStruct((N_EXPERTS, 1), I32),
        ],
        compiler_params=_cparams("arbitrary"),
        name="dest",
    )(cnt, e_t, rk_t)


def _row_copy(src, dst, sem):
    return pltpu.make_async_copy(src, dst, sem)


def _dispatch_body(lt_ref, d_ref, h_ref, xs_ref, zero_ref, sem):
    @pl.when(pl.program_id(0) == 0)
    def _():
        zero_ref[...] = jnp.zeros_like(zero_ref)

        def tile_copy(e):
            r0 = pl.multiple_of(jnp.maximum(lt_ref[e], 0), ROW_TILE)
            return pltpu.make_async_copy(zero_ref, xs_ref.at[pl.ds(r0, ROW_TILE)], sem)

        def z_start(e, carry):
            @pl.when(lt_ref[e] >= 0)
            def _():
                tile_copy(e).start()
            return carry

        def z_wait(e, carry):
            @pl.when(lt_ref[e] >= 0)
            def _():
                tile_copy(e).wait()
            return carry

        lax.fori_loop(0, N_EXPERTS, z_start, 0)
        lax.fori_loop(0, N_EXPERTS, z_wait, 0)

    def issue(t, carry):
        for k in range(TOPK_EXPERTS):
            _row_copy(h_ref.at[t], xs_ref.at[d_ref[k, t]], sem).start(priority=k % 2)
        return carry

    lax.fori_loop(0, DISP_TM, issue, 0, unroll=DISPATCH_UNROLL)
    for k in range(TOPK_EXPERTS):
        pltpu.make_async_copy(h_ref, xs_ref.at[pl.ds(0, DISP_TM)], sem).wait()


def _dispatch(last_tile_row, dest_t, h1):
    grid_spec = pltpu.PrefetchScalarGridSpec(
        num_scalar_prefetch=1,
        grid=(N_TOK // DISP_TM,),
        in_specs=[
            pl.BlockSpec((TOPK_EXPERTS, DISP_TM), lambda i, lt: (0, i), memory_space=pltpu.SMEM),
            pl.BlockSpec((DISP_TM, PACKED_CHUNKS, LANES), lambda i, lt: (i, 0, 0)),
        ],
        out_specs=pl.BlockSpec(memory_space=pl.ANY),
        scratch_shapes=[pltpu.VMEM((ROW_TILE, PACKED_CHUNKS, LANES), U32), pltpu.SemaphoreType.DMA],
    )
    return pl.pallas_call(
        _dispatch_body,
        grid_spec=grid_spec,
        out_shape=jax.ShapeDtypeStruct((N_ROWS, PACKED_CHUNKS, LANES), U32),
        compiler_params=_cparams("arbitrary"),
        name="dispatch",
    )(last_tile_row, dest_t, h1)


def _row_destinations(dest_flat):
    info = plsc.get_sparse_core_info()
    n_cores, lanes = info.num_cores, info.num_lanes
    n_workers = n_cores * info.num_subcores
    rows_per_worker = -(-N_ROWS // (n_workers * ROW_TILE)) * ROW_TILE
    assert rows_per_worker % lanes == 0
    mesh = plsc.VectorSubcoreMesh(core_axis_name="c", subcore_axis_name="s")

    @functools.partial(
        pl.kernel, mesh=mesh, out_type=jax.ShapeDtypeStruct((n_workers * rows_per_worker,), I32),
        scratch_types=[pltpu.VMEM((SLOT_CHUNK,), I32), pltpu.VMEM((rows_per_worker,), I32)],
        compiler_params=pltpu.CompilerParams(needs_layout_passes=False), name="row_destinations")
    def invert(dest_hbm, out_hbm, dest_v, table_v):
        first_row = (lax.axis_index("s") * n_cores + lax.axis_index("c")) * rows_per_worker
        lane = lax.iota(I32, lanes)
        tile_shift = ROW_TILE.bit_length() - 1

        def spare(i, carry):
            row = first_row + i * lanes + lane
            buf = lax.rem(lax.shift_right_logical(row, tile_shift), EXPERT_BUFS)
            table_v[pl.ds(i * lanes, lanes)] = TOPK_EXPERTS * N_TOK + buf * ROW_TILE + (row & (ROW_TILE - 1))
            return carry

        lax.fori_loop(0, rows_per_worker // lanes, spare, 0)

        def chunk(c, carry):
            pltpu.sync_copy(dest_hbm.at[pl.ds(c * SLOT_CHUNK, SLOT_CHUNK)], dest_v)

            def vec(i, inner):
                local = dest_v[pl.ds(i * lanes, lanes)] - first_row
                mine = (local >= 0) & (local < rows_per_worker)
                plsc.store_scatter(table_v, [jnp.where(mine, local, 0)], lane + (c * SLOT_CHUNK + i * lanes),
                                   mask=mine)
                return inner

            lax.fori_loop(0, SLOT_CHUNK // lanes, vec, 0)
            return carry

        lax.fori_loop(0, N_ASSIGN // SLOT_CHUNK, chunk, 0)
        pltpu.sync_copy(table_v, out_hbm.at[pl.ds(first_row, rows_per_worker)])

    return invert(dest_flat)


def _experts_body(te_ref, nu_ref, lt_ref, slot_ref, xs_ref, wg_ref, wu_ref, wd_ref, yk_ref,
                  wg_f, wu_f, wd_f, wg_b, wu_b, wd_b, y_buf, n_loaded, sem, w_sem):
    i = pl.program_id(0)
    n_used = nu_ref[0]

    def weight_copies(e, s):
        return (pltpu.make_async_copy(wg_ref.at[e], wg_f.at[s], w_sem.at[s]),
                pltpu.make_async_copy(wu_ref.at[e], wu_f.at[s], w_sem.at[s]),
                pltpu.make_async_copy(wd_ref.at[e], wd_f.at[s], w_sem.at[s]))

    def wait_tile(b):
        pltpu.make_async_copy(y_buf.at[b], yk_ref.at[pl.ds(0, ROW_TILE), :], sem.at[b]).wait()

    def send_tile(b):
        for r in range(ROW_TILE):
            dst = slot_ref[0, 0, r]
            _row_copy(y_buf.at[b, pl.ds(r, 1), :], yk_ref.at[pl.ds(dst, 1), :], sem.at[b]).start(priority=r % 2)

    @pl.when(i == 0)
    def _():
        y_buf[EXPERT_BUFS - 1] = jnp.zeros((ROW_TILE, PACKED_COLS), U32)
        n_loaded[0] = 0
        for c in weight_copies(te_ref[0], 0):
            c.start()

    @pl.when((i >= 2) & (i <= n_used))
    def _():
        wait_tile(lax.rem(i, EXPERT_BUFS))

    @pl.when(i < n_used)
    def _():
        e = te_ref[i]
        prev = te_ref[jnp.maximum(i - 1, 0)]

        @pl.when((i == 0) | (e != prev))
        def _():
            s = lax.rem(n_loaded[0], 2)
            for c in weight_copies(e, s):
                c.wait()
            wg_b[...] = wg_f[s].astype(BF16)
            wu_b[...] = wu_f[s].astype(BF16)
            wd_b[...] = wd_f[s].astype(BF16)
            n_loaded[0] = n_loaded[0] + 1
            nxt = lax.shift_right_logical(lt_ref[e], ROW_TILE.bit_length() - 1) + 1

            @pl.when(nxt < n_used)
            def _():
                for c in weight_copies(te_ref[nxt], 1 - s):
                    c.start()

        send_tile(lax.rem(i + EXPERT_BUFS - 1, EXPERT_BUFS))
        packed = jnp.concatenate([xs_ref[:, c, :] for c in range(PACKED_CHUNKS)], axis=1)
        x_lo, x_hi = _unpack_bf16_pair(packed)
        x = jnp.concatenate([x_lo.astype(BF16), x_hi.astype(BF16)], axis=1)
        g = _dot(x, wg_b[...])
        u = _dot(x, wu_b[...])
        h = (g * _sigmoid(g)) * u
        y_buf[lax.rem(i, EXPERT_BUFS)] = _pack_bf16_pair(_dot(h.astype(BF16), wd_b[...]))

    @pl.when(i == n_used)
    def _():
        send_tile(lax.rem(i + EXPERT_BUFS - 1, EXPERT_BUFS))
        wait_tile(lax.rem(i + EXPERT_BUFS - 2, EXPERT_BUFS))
        wait_tile(lax.rem(i + EXPERT_BUFS - 1, EXPERT_BUFS))


def _experts(tile_expert, n_used, last_tile_row, row_dst, xs, w_gate, w_up, w_down):
    def tile(i, te, nu, lt):
        return (jnp.minimum(i, nu[0] - 1), 0, 0)

    n_table_tiles = row_dst.shape[0] // ROW_TILE
    placeholder = n_table_tiles - 1
    assert placeholder * ROW_TILE >= N_ROWS and placeholder % EXPERT_BUFS == EXPERT_BUFS - 1

    def prev_tile(i, te, nu, lt):
        return (jnp.where(i == 0, placeholder, jnp.minimum(i, nu[0]) - 1), 0, 0)

    grid_spec = pltpu.PrefetchScalarGridSpec(
        num_scalar_prefetch=3,
        grid=(N_ROW_TILES + 1,),
        in_specs=[
            pl.BlockSpec((1, 1, ROW_TILE), prev_tile, memory_space=pltpu.SMEM),
            pl.BlockSpec((ROW_TILE, PACKED_CHUNKS, LANES), tile),
            pl.BlockSpec(memory_space=pl.ANY),
            pl.BlockSpec(memory_space=pl.ANY),
            pl.BlockSpec(memory_space=pl.ANY),
        ],
        out_specs=pl.BlockSpec(memory_space=pl.ANY),
        scratch_shapes=[
            pltpu.VMEM((2, D_MODEL, D_EXPERT), F32),
            pltpu.VMEM((2, D_MODEL, D_EXPERT), F32),
            pltpu.VMEM((2, D_EXPERT, D_MODEL), F32),
            pltpu.VMEM((D_MODEL, D_EXPERT), BF16),
            pltpu.VMEM((D_MODEL, D_EXPERT), BF16),
            pltpu.VMEM((D_EXPERT, D_MODEL), BF16),
            pltpu.VMEM((EXPERT_BUFS, ROW_TILE, PACKED_COLS), U32),
            pltpu.SMEM((1,), I32),
            pltpu.SemaphoreType.DMA((EXPERT_BUFS,)),
            pltpu.SemaphoreType.DMA((2,)),
        ],
    )
    return pl.pallas_call(
        _experts_body,
        grid_spec=grid_spec,
        out_shape=jax.ShapeDtypeStruct((YK_ROWS, PACKED_COLS), U32),
        compiler_params=_cparams("arbitrary"),
        name="experts",
    )(tile_expert, n_used, last_tile_row, row_dst.reshape(n_table_tiles, 1, ROW_TILE), xs, w_gate, w_up, w_down)


def _final_body(w_ref, yk_ref, h_ref, hb_ref, p_ref, wsg_ref, wsu_ref, wsd_ref, wpl_ref, wpg_ref,
                g_ref, b_ref, o_ref):
    hb = hb_ref[...]
    sg = _dot(hb, wsg_ref[...])
    shared = _dot(((sg * _sigmoid(sg)) * _dot(hb, wsu_ref[...])).astype(BF16), wsd_ref[...])
    ple = _sigmoid(_dot(hb, wpg_ref[...])) * _dot(p_ref[...].astype(BF16), wpl_ref[...])

    w_col = w_ref[...].T
    y_lo, y_hi = _unpack_bf16_pair(yk_ref[0])
    r_lo, r_hi = y_lo * w_col[:, 0:1], y_hi * w_col[:, 0:1]
    for k in range(1, TOPK_EXPERTS):
        y_lo, y_hi = _unpack_bf16_pair(yk_ref[k])
        r_lo, r_hi = r_lo + y_lo * w_col[:, k:k + 1], r_hi + y_hi * w_col[:, k:k + 1]
    routed = jnp.concatenate([r_lo, r_hi], axis=1)
    o_ref[...] = _layer_norm(ALPHA * h_ref[...] + (routed + shared) + ple, g_ref[...], b_ref[...])


def _final(w_t, yk, h1, h_bf, p2d, w_s_gate, w_s_up, w_s_down, w_ple, w_ple_gate, ln_g, ln_b):
    d = D_MODEL
    full = lambda i: (0, 0)
    return pl.pallas_call(
        _final_body,
        grid=(N_TOK // FINAL_TM,),
        in_specs=[
            pl.BlockSpec((TOPK_EXPERTS, FINAL_TM), lambda i: (0, i)),
            pl.BlockSpec((TOPK_EXPERTS, FINAL_TM, PACKED_COLS), lambda i: (0, i, 0)),
            pl.BlockSpec((FINAL_TM, d), lambda i: (i, 0)),
            pl.BlockSpec((FINAL_TM, d), lambda i: (i, 0)),
            pl.BlockSpec((FINAL_TM, PLE_DIM), lambda i: (i, 0)),
            pl.BlockSpec((d, D_SHARED), full),
            pl.BlockSpec((d, D_SHARED), full),
            pl.BlockSpec((D_SHARED, d), full),
            pl.BlockSpec((PLE_DIM, d), full),
            pl.BlockSpec((d, d), full),
            pl.BlockSpec((1, d), full),
            pl.BlockSpec((1, d), full),
        ],
        out_specs=pl.BlockSpec((FINAL_TM, d), lambda i: (i, 0)),
        out_shape=jax.ShapeDtypeStruct((N_TOK, d), F32),
        compiler_params=_cparams("parallel"),
        name="combine_ln2",
    )(w_t, yk.reshape(TOPK_EXPERTS + 1, N_TOK, PACKED_COLS), h1, h_bf, p2d, w_s_gate, w_s_up, w_s_down,
      w_ple, w_ple_gate, ln_g, ln_b)


def _rope_tables(positions):
    half = ROT_DIM // 2
    inv = ROPE_THETA ** (-jnp.arange(0, ROT_DIM, 2, dtype=F32) / ROT_DIM)
    per_row = LANES // half
    pos = jnp.repeat(positions.reshape(-1, per_row), half, axis=1).astype(F32)
    ang = pos * jnp.tile(inv, per_row)[None, :]
    cos = jnp.cos(ang).reshape(positions.shape + (half,))
    sin = jnp.sin(ang).reshape(positions.shape + (half,))
    rest = MOBA_DH - ROT_DIM
    ones = jnp.ones(cos.shape[:-1] + (rest,), F32)
    zeros = jnp.zeros(cos.shape[:-1] + (rest,), F32)
    return (jnp.concatenate([cos, cos, ones], axis=-1), jnp.concatenate([-sin, sin, zeros], axis=-1))


def _layer(h2d, p2d, cos_t, sin_t, w_in, w_gk2, b_gk, norm_g, w_gla_o, w_moba_o, w_out, ln1_g, ln1_b,
           w_router, b_router, w_e_gate, w_e_up, w_e_down, w_s_gate, w_s_up, w_s_down, w_ple, w_ple_gate,
           ln2_g, ln2_b):
    w_gk2_pad = jnp.concatenate([w_gk2, jnp.zeros((LANES - GLA_RANK, GLA_KDIM), w_gk2.dtype)], axis=0)

    proj = _in_proj(h2d, _repack_w_in(w_in))
    gla_out = _gla(proj, w_gk2_pad, b_gk[None, :], norm_g[None, :])
    moba_out = _moba(proj, cos_t, sin_t)
    h1, h_bf, h_packed = _mix(gla_out, moba_out, proj, h2d, w_gla_o.astype(BF16), w_moba_o.astype(BF16),
                              w_out.astype(BF16), ln1_g[None, :], ln1_b[None, :])
    e_t, w_t, rk_t, cnt = _route(h_bf, w_router.T.astype(BF16), b_router[:, None])
    dest_t, tile_expert, n_used, last_tile_row = _dest(cnt, e_t, rk_t)
    last_tile_row = last_tile_row.reshape(N_EXPERTS)
    xs = _dispatch(last_tile_row, dest_t, h_packed)
    row_dst = _row_destinations(dest_t.reshape(N_ASSIGN))
    yk = _experts(tile_expert.reshape(TILE_TABLE), n_used[0, 0:1], last_tile_row, row_dst, xs,
                  w_e_gate, w_e_up, w_e_down)
    return _final(w_t, yk, h1, h_bf, p2d, w_s_gate.astype(BF16), w_s_up.astype(BF16),
                  w_s_down.astype(BF16), w_ple.astype(BF16), w_ple_gate.astype(BF16),
                  ln2_g[None, :], ln2_b[None, :])


def kernel(x, p, positions, w_in, w_gla_gk2, b_gla_gk, gla_norm_g, w_gla_o, w_moba_o, w_out, ln1_g, ln1_b,
           w_router, b_router, w_e_gate, w_e_up, w_e_down, w_s_gate, w_s_up, w_s_down, w_ple, w_ple_gate,
           ln2_g, ln2_b):
    cos_t, sin_t = _rope_tables(positions)
    h = x.reshape(N_TOK, D_MODEL)
    for i in range(DEPTH):
        h = _layer(h, p[i].reshape(N_TOK, PLE_DIM), cos_t, sin_t, w_in[i], w_gla_gk2[i], b_gla_gk[i],
                   gla_norm_g[i], w_gla_o[i], w_moba_o[i], w_out[i], ln1_g[i], ln1_b[i], w_router[i],
                   b_router[i], w_e_gate[i], w_e_up[i], w_e_down[i], w_s_gate[i], w_s_up[i], w_s_down[i],
                   w_ple[i], w_ple_gate[i], ln2_g[i], ln2_b[i])
    return h.reshape(BATCH, SEQ, D_MODEL)
```

```python
import functools

import jax
import jax.numpy as jnp
from jax import lax
from jax.experimental import pallas as pl
from jax.experimental.pallas import tpu as pltpu
from jax.experimental.pallas import tpu_sc as plsc

F32 = jnp.float32
BF16 = jnp.bfloat16
I32 = jnp.int32
U32 = jnp.uint32

LANES = 128
SUBLANES = 8
VMEM_LIMIT_BYTES = 48 * 1024 * 1024

D_MODEL = 1024
BATCH = 8
SEQ = 2048
N_TOK = BATCH * SEQ
GLA_HEADS = 4
GLA_DK = 128
GLA_DV = 256
GLA_RANK = 16
GLA_NORMALIZER = 16.0
GLA_CHUNK = 64
GLA_GROUP = 4 * GLA_CHUNK
GLA_UNROLL = 4
MOBA_HEADS = 8
MOBA_DH = 128
MOBA_BLOCK = 256
MOBA_TOPK = 3
ROT_DIM = 32
ROPE_THETA = 500000.0
N_EXPERTS = 256
TOPK_EXPERTS = 8
N_GROUPS = 8
GROUP_SIZE = N_EXPERTS // N_GROUPS
TOPK_GROUPS = 4
D_EXPERT = 256
D_SHARED = 256
ROUTED_SCALE = 2.5
PLE_DIM = 256
LN_EPS = 1e-5
DEPTH = 1
ALPHA = (2.0 * DEPTH) ** 0.25
GLA_KDIM = GLA_HEADS * GLA_DK
GLA_VDIM = GLA_HEADS * GLA_DV
MOBA_DIM = MOBA_HEADS * MOBA_DH
N_KBLK = SEQ // MOBA_BLOCK

COL_GQ = 0
COL_GK = COL_GQ + GLA_KDIM
COL_GV = COL_GK + GLA_KDIM
COL_GR = COL_GV + GLA_VDIM
COL_MQ = COL_GR + GLA_VDIM
COL_MK = COL_MQ + MOBA_DIM
COL_MV = COL_MK + MOBA_DIM
COL_GA = COL_MV + MOBA_DIM
COL_GB = COL_GA + D_MODEL
COL_LOW = COL_GB + D_MODEL
PROJ_COLS = COL_LOW + LANES
LOW_SRC = 2 * GLA_KDIM + 2 * GLA_VDIM

ROW_TILE = 256
PACKED_COLS = D_MODEL // 2
PACKED_CHUNKS = PACKED_COLS // LANES
EXPERT_BUFS = 3
YK_ROWS = (TOPK_EXPERTS + 1) * N_TOK
SLOT_CHUNK = 4096
assert EXPERT_BUFS * ROW_TILE <= N_TOK
N_ASSIGN = N_TOK * TOPK_EXPERTS
N_ROW_TILES = (N_ASSIGN + N_EXPERTS * (ROW_TILE - 1) + ROW_TILE - 1) // ROW_TILE
N_ROWS = N_ROW_TILES * ROW_TILE
TILE_TABLE = -(-N_ROW_TILES // LANES) * LANES

PROJ_TM = 1024
REPACK_TN = 512
PROJ_TN = PROJ_COLS // 5
MIX_TM = 512
ROUTE_TM = 512
DEST_TM = 2048
DISP_TM = 256
DISPATCH_UNROLL = 4
FINAL_TM = 256

_NEG_INF = float("-inf")
LOG2_E = 1.4426950408889634


def _cparams(*sem):
    return pltpu.CompilerParams(dimension_semantics=sem, vmem_limit_bytes=VMEM_LIMIT_BYTES)


def _dot(a, b):
    return jnp.dot(a, b, preferred_element_type=F32)


def _dot_nt(a, b):
    return lax.dot_general(a, b, (((1,), (1,)), ((), ())), preferred_element_type=F32)


def _dot_tn(a, b):
    return lax.dot_general(a, b, (((0,), (0,)), ((), ())), preferred_element_type=F32)


def _split_bf16(x):
    hi = x.astype(BF16)
    lo = (x - hi.astype(F32)).astype(BF16)
    return hi, lo


def _sigmoid(x):
    return 1.0 / (1.0 + jnp.exp(-x))


def _pack_bf16_pair(x):
    lo = lax.bitcast_convert_type(x[:, :PACKED_COLS].astype(BF16).astype(F32), U32)
    hi = lax.bitcast_convert_type(x[:, PACKED_COLS:].astype(BF16).astype(F32), U32)
    return hi | lax.shift_right_logical(lo, jnp.uint32(16))


def _unpack_bf16_pair(w):
    lo = lax.bitcast_convert_type(lax.shift_left(w, jnp.uint32(16)), F32)
    hi = lax.bitcast_convert_type(w & jnp.uint32(0xFFFF0000), F32)
    return lo, hi


def _repack_body(a_ref, b_ref, o_ref):
    t = pl.program_id(0)
    a = a_ref[...].astype(BF16)

    @pl.when(t < LOW_SRC // REPACK_TN)
    def _():
        o_ref[...] = a

    @pl.when((t >= LOW_SRC // REPACK_TN) & (t < COL_LOW // REPACK_TN))
    def _():
        o_ref[...] = jnp.concatenate([a[:, GLA_RANK:], b_ref[:, :GLA_RANK].astype(BF16)], axis=1)

    @pl.when(t == COL_LOW // REPACK_TN)
    def _():
        o_ref[...] = jnp.concatenate([a[:, :GLA_RANK], jnp.zeros((D_MODEL, REPACK_TN - GLA_RANK), BF16)], axis=1)


def _repack_w_in(w_in, layer):
    assert LOW_SRC % REPACK_TN == 0 and COL_LOW % REPACK_TN == 0
    low_tile = LOW_SRC // REPACK_TN
    last_out = COL_LOW // REPACK_TN
    lanes_per_tile = REPACK_TN // LANES
    return pl.pallas_call(
        _repack_body,
        grid=(last_out + 1,),
        in_specs=[
            pl.BlockSpec((None, D_MODEL, REPACK_TN), lambda t: (layer, 0, jnp.where(t == last_out, low_tile, t))),
            pl.BlockSpec((None, D_MODEL, LANES), lambda t: (layer, 0, jnp.minimum(t + 1, last_out) * lanes_per_tile)),
        ],
        out_specs=pl.BlockSpec((D_MODEL, REPACK_TN), lambda t: (0, t)),
        out_shape=jax.ShapeDtypeStruct((D_MODEL, PROJ_COLS), BF16),
        compiler_params=_cparams("parallel"),
        name="repack_w_in",
    )(w_in, w_in)


def _proj_body(x_ref, w_ref, o_ref):
    o_ref[...] = _dot(x_ref[...].astype(BF16), w_ref[...]).astype(BF16)


def _in_proj(x2d, w_cat):
    return pl.pallas_call(
        _proj_body,
        grid=(PROJ_COLS // PROJ_TN, N_TOK // PROJ_TM),
        in_specs=[
            pl.BlockSpec((PROJ_TM, D_MODEL), lambda j, i: (i, 0)),
            pl.BlockSpec((D_MODEL, PROJ_TN), lambda j, i: (0, j)),
        ],
        out_specs=pl.BlockSpec((PROJ_TM, PROJ_TN), lambda j, i: (i, j)),
        out_shape=jax.ShapeDtypeStruct((N_TOK, PROJ_COLS), BF16),
        compiler_params=_cparams("parallel", "parallel"),
        name="in_proj",
    )(x2d, w_cat)


def _gla_body(q_ref, k_ref, v_ref, r_ref, low_ref, wg_ref, bg_ref, ng_ref, o_ref, st_ref, gk_ref):
    w_hi, w_lo = _split_bf16(wg_ref[...])
    low = low_ref[...]
    lin = _dot(low, w_hi) + _dot(low, w_lo) + bg_ref[...]
    gk_ref[...] = (jnp.minimum(lin, 0.0) - jnp.log1p(jnp.exp(-jnp.abs(lin)))) * (1.0 / GLA_NORMALIZER)
    st_ref[...] = jnp.zeros_like(st_ref)

    ri = lax.broadcasted_iota(I32, (GLA_GROUP, GLA_GROUP), 0)
    ci = lax.broadcasted_iota(I32, (GLA_GROUP, GLA_GROUP), 1)
    same_chunk = lax.shift_right_logical(ri, GLA_CHUNK.bit_length() - 1) == lax.shift_right_logical(
        ci, GLA_CHUNK.bit_length() - 1)
    causal = same_chunk & (ri >= ci)
    sums = jnp.concatenate([jnp.where(causal, 1.0, 0.0), jnp.where(same_chunk, 1.0, 0.0)], axis=0).astype(BF16)
    gain = ng_ref[...]

    def group(c, carry):
        rows = pl.ds(pl.multiple_of(c * GLA_GROUP, GLA_GROUP), GLA_GROUP)
        g_hi, g_lo = _split_bf16(gk_ref[rows, :])
        bb = _dot(sums, g_hi) + _dot(sums, g_lo)
        b = bb[0:GLA_GROUP]
        b_end = bb[GLA_GROUP:2 * GLA_GROUP]
        q = q_ref[rows, :].astype(F32) * (GLA_DK ** -0.5)
        k = k_ref[rows, :].astype(F32)
        v = v_ref[rows, :]
        q_e = (q * jnp.exp(b)).astype(BF16)
        k_e = (k * jnp.exp(-b)).astype(BF16)
        k_d = (k * jnp.exp(b_end - b)).astype(BF16)
        att = jnp.where(causal, _dot_nt(q_e, k_e), 0.0)
        o = _dot(att.astype(BF16), v)
        st = st_ref[...]
        inter = []
        for j in range(GLA_GROUP // GLA_CHUNK):
            cr = slice(j * GLA_CHUNK, (j + 1) * GLA_CHUNK)
            inter.append(_dot_nt(q_e[cr], st.astype(BF16)))
            st = st * jnp.exp(b_end[j * GLA_CHUNK:j * GLA_CHUNK + 1, :]) + _dot_tn(v[cr], k_d[cr])
        st_ref[...] = st
        o = o + jnp.concatenate(inter, axis=0)
        o = o * lax.rsqrt(jnp.mean(o * o, axis=-1, keepdims=True) + LN_EPS) * gain
        r = r_ref[rows, :].astype(F32)
        o_ref[rows, :] = (o * (r * _sigmoid(r))).astype(BF16)
        return carry

    lax.fori_loop(0, SEQ // GLA_GROUP, group, 0, unroll=GLA_UNROLL)


def _gla(proj, w_gk2_pad, b_gk, norm_g):
    kb, vb = GLA_DK, GLA_DV
    return pl.pallas_call(
        _gla_body,
        grid=(BATCH, GLA_HEADS),
        in_specs=[
            pl.BlockSpec((SEQ, kb), lambda b, h: (b, COL_GQ // kb + h)),
            pl.BlockSpec((SEQ, kb), lambda b, h: (b, COL_GK // kb + h)),
            pl.BlockSpec((SEQ, vb), lambda b, h: (b, COL_GV // vb + h)),
            pl.BlockSpec((SEQ, vb), lambda b, h: (b, COL_GR // vb + h)),
            pl.BlockSpec((SEQ, LANES), lambda b, h: (b, COL_LOW // LANES)),
            pl.BlockSpec((LANES, kb), lambda b, h: (0, h)),
            pl.BlockSpec((1, kb), lambda b, h: (0, h)),
            pl.BlockSpec((1, vb), lambda b, h: (0, 0)),
        ],
        out_specs=pl.BlockSpec((SEQ, vb), lambda b, h: (b, h)),
        out_shape=jax.ShapeDtypeStruct((N_TOK, GLA_VDIM), BF16),
        scratch_shapes=[pltpu.VMEM((vb, kb), F32), pltpu.VMEM((SEQ, kb), F32)],
        compiler_params=_cparams("parallel", "parallel"),
        name="gla",
    )(proj, proj, proj, proj, proj, w_gk2_pad, b_gk, norm_g)


def _moba_body(q_ref, k_ref, v_ref, c_ref, s_ref, o_ref, qs_ref, ks_ref, vt_ref, bias_ref,
               sc_a_ref, sc_b_ref, pr_a_ref, pr_b_ref):
    cos_t = c_ref[0]
    sin_t = s_ref[0]
    lane = lax.broadcasted_iota(I32, (SEQ, MOBA_DH), 1)
    half = ROT_DIM // 2

    def rope(x):
        partner = jnp.where(lane < half, pltpu.roll(x, MOBA_DH - half, 1), pltpu.roll(x, half, 1))
        return x * cos_t + partner * sin_t

    q = rope(q_ref[...].astype(F32))
    k = rope(k_ref[...].astype(F32))
    q_hi, q_lo = _split_bf16(q)
    qs_ref[...] = (q * (MOBA_DH ** -0.5 * LOG2_E)).astype(BF16)
    ks_ref[...] = k.astype(BF16)
    vt_ref[...] = v_ref[...].astype(F32).T.astype(BF16)

    k_mean = jnp.concatenate(
        [jnp.mean(k[j * MOBA_BLOCK:(j + 1) * MOBA_BLOCK], axis=0, keepdims=True) for j in range(N_KBLK)], axis=0)
    m_hi, m_lo = _split_bf16(k_mean)
    s_blk = _dot_nt(m_hi, q_hi) + _dot_nt(m_hi, q_lo) + _dot_nt(m_lo, q_hi)
    blk = lax.broadcasted_iota(I32, (N_KBLK, SEQ), 0)
    q_blk = lax.shift_right_logical(lax.broadcasted_iota(I32, (N_KBLK, SEQ), 1), MOBA_BLOCK.bit_length() - 1)
    past = blk < q_blk
    s_blk = jnp.where(past, s_blk, _NEG_INF)
    beaten = jnp.zeros((N_KBLK, SEQ), I32)
    for j in range(N_KBLK):
        row = s_blk[j:j + 1, :]
        beaten += jnp.where((row > s_blk) | ((row == s_blk) & (j < blk)), 1, 0)
    bias_ref[...] = jnp.where(past & (beaten < MOBA_TOPK), 0.0, _NEG_INF)

    kr = lax.broadcasted_iota(I32, (MOBA_BLOCK, MOBA_BLOCK), 0)
    qc = lax.broadcasted_iota(I32, (MOBA_BLOCK, MOBA_BLOCK), 1)
    own_bias = jnp.where(kr <= qc, 0.0, _NEG_INF)

    sc_bufs = (sc_a_ref, sc_b_ref)
    pr_bufs = (pr_a_ref, pr_b_ref)
    zeros_row = jnp.zeros((1, MOBA_BLOCK), F32)
    future_row = jnp.full((1, MOBA_BLOCK), _NEG_INF, F32)
    for pair in range(N_KBLK // 2):
        q_blocks = (2 * pair, 2 * pair + 1)
        cols = slice(q_blocks[0] * MOBA_BLOCK, (q_blocks[1] + 1) * MOBA_BLOCK)
        n_kblk = q_blocks[1] + 1
        n_keys = n_kblk * MOBA_BLOCK
        sc, pr = sc_bufs[pair % 2], pr_bufs[pair % 2]
        q_pair = qs_ref[cols, :]

        def query_bias(j):
            halves = []
            for qb in q_blocks:
                q_cols = slice(qb * MOBA_BLOCK, (qb + 1) * MOBA_BLOCK)
                halves.append(bias_ref[j:j + 1, q_cols] if j < qb else zeros_row if j == qb else future_row)
            return jnp.concatenate(halves, axis=1)

        biases = [query_bias(j) for j in range(n_kblk)]
        col_max = []
        for j in range(n_kblk):
            rows = slice(j * MOBA_BLOCK, (j + 1) * MOBA_BLOCK)
            s = _dot_nt(ks_ref[rows, :], q_pair)
            if j in q_blocks:
                h = q_blocks.index(j)
                own = s[:, h * MOBA_BLOCK:(h + 1) * MOBA_BLOCK] + own_bias
                s = jnp.concatenate([own, s[:, MOBA_BLOCK:]] if h == 0 else [s[:, :MOBA_BLOCK], own], axis=1)
            sc[rows, :] = s
            col_max.append(jnp.max(s, axis=0, keepdims=True) + biases[j])
        m = functools.reduce(jnp.maximum, col_max)
        denom = jnp.zeros((1, 2 * MOBA_BLOCK), F32)
        for j in range(n_kblk):
            rows = slice(j * MOBA_BLOCK, (j + 1) * MOBA_BLOCK)
            p = jnp.exp2(sc[rows, :] - (m - biases[j]))
            denom = denom + jnp.sum(p, axis=0, keepdims=True)
            pr[rows, :] = p.astype(BF16)
        o_t = _dot(vt_ref[:, 0:n_keys], pr[0:n_keys, :]) * (1.0 / denom)
        o_ref[cols, :] = o_t.T.astype(BF16)


def _moba(proj, cos_t, sin_t):
    dh = MOBA_DH
    return pl.pallas_call(
        _moba_body,
        grid=(BATCH, MOBA_HEADS),
        in_specs=[
            pl.BlockSpec((SEQ, dh), lambda b, h: (b, COL_MQ // dh + h)),
            pl.BlockSpec((SEQ, dh), lambda b, h: (b, COL_MK // dh + h)),
            pl.BlockSpec((SEQ, dh), lambda b, h: (b, COL_MV // dh + h)),
            pl.BlockSpec((1, SEQ, dh), lambda b, h: (b, 0, 0)),
            pl.BlockSpec((1, SEQ, dh), lambda b, h: (b, 0, 0)),
        ],
        out_specs=pl.BlockSpec((SEQ, dh), lambda b, h: (b, h)),
        out_shape=jax.ShapeDtypeStruct((N_TOK, MOBA_DIM), BF16),
        scratch_shapes=[
            pltpu.VMEM((SEQ, dh), BF16),
            pltpu.VMEM((SEQ, dh), BF16),
            pltpu.VMEM((dh, SEQ), BF16),
            pltpu.VMEM((N_KBLK, SEQ), F32),
            pltpu.VMEM((SEQ, 2 * MOBA_BLOCK), F32),
            pltpu.VMEM((SEQ, 2 * MOBA_BLOCK), F32),
            pltpu.VMEM((SEQ, 2 * MOBA_BLOCK), BF16),
            pltpu.VMEM((SEQ, 2 * MOBA_BLOCK), BF16),
        ],
        compiler_params=_cparams("parallel", "parallel"),
        name="moba",
    )(proj, proj, proj, cos_t, sin_t)


def _layer_norm(z, g, b):
    mu = jnp.mean(z, axis=-1, keepdims=True)
    zc = z - mu
    var = jnp.mean(zc * zc, axis=-1, keepdims=True)
    return zc * lax.rsqrt(var + LN_EPS) * g + b


def _mix_body(gla_ref, moba_ref, ga_ref, gb_ref, x_ref, wgo_ref, wmo_ref, wo_ref, g_ref, b_ref,
              h_ref, hb_ref, hp_ref):
    y_gla = _dot(gla_ref[...], wgo_ref[...])
    y_moba = _dot(moba_ref[...], wmo_ref[...])
    merged = _sigmoid(ga_ref[...].astype(F32)) * y_gla + _sigmoid(gb_ref[...].astype(F32)) * y_moba
    mix = _dot(merged.astype(BF16), wo_ref[...])
    h = _layer_norm(ALPHA * x_ref[...] + mix, g_ref[...], b_ref[...])
    h_ref[...] = h
    hb_ref[...] = h.astype(BF16)
    packed = _pack_bf16_pair(h)
    for c in range(PACKED_CHUNKS):
        hp_ref[:, c, :] = packed[:, c * LANES:(c + 1) * LANES]


def _mix(gla_out, moba_out, proj, x2d, w_gla_o, w_moba_o, w_out, ln_g, ln_b):
    d = D_MODEL
    row = lambda i: (i, 0)
    full = lambda i: (0, 0)
    return pl.pallas_call(
        _mix_body,
        grid=(N_TOK // MIX_TM,),
        in_specs=[
            pl.BlockSpec((MIX_TM, d), row),
            pl.BlockSpec((MIX_TM, d), row),
            pl.BlockSpec((MIX_TM, d), lambda i: (i, COL_GA // d)),
            pl.BlockSpec((MIX_TM, d), lambda i: (i, COL_GB // d)),
            pl.BlockSpec((MIX_TM, d), row),
            pl.BlockSpec((d, d), full),
            pl.BlockSpec((d, d), full),
            pl.BlockSpec((d, d), full),
            pl.BlockSpec((1, d), full),
            pl.BlockSpec((1, d), full),
        ],
        out_specs=[
            pl.BlockSpec((MIX_TM, d), row),
            pl.BlockSpec((MIX_TM, d), row),
            pl.BlockSpec((MIX_TM, PACKED_CHUNKS, LANES), lambda i: (i, 0, 0)),
        ],
        out_shape=[
            jax.ShapeDtypeStruct((N_TOK, d), F32),
            jax.ShapeDtypeStruct((N_TOK, d), BF16),
            jax.ShapeDtypeStruct((N_TOK, PACKED_CHUNKS, LANES), U32),
        ],
        compiler_params=_cparams("parallel"),
        name="mix_ln1",
    )(gla_out, moba_out, proj, proj, x2d, w_gla_o, w_moba_o, w_out, ln_g, ln_b)


def _route_body(h_ref, wr_ref, br_ref, e_ref, w_ref, rk_ref, cnt_ref, carry_ref):
    tm = ROUTE_TM

    @pl.when(pl.program_id(0) == 0)
    def _():
        carry_ref[...] = jnp.zeros_like(carry_ref)

    scores = _sigmoid(_dot_nt(wr_ref[...], h_ref[...]))
    biased = scores + br_ref[...]
    row = lax.broadcasted_iota(I32, (N_EXPERTS, tm), 0).astype(F32)
    row_g = lax.broadcasted_iota(I32, (GROUP_SIZE, tm), 0).astype(F32)

    g_scores = []
    for g in range(N_GROUPS):
        grp = biased[g * GROUP_SIZE:(g + 1) * GROUP_SIZE]
        m1 = jnp.max(grp, axis=0, keepdims=True)
        first = jnp.min(jnp.where(grp == m1, row_g, float(GROUP_SIZE)), axis=0, keepdims=True)
        m2 = jnp.max(jnp.where(row_g == first, _NEG_INF, grp), axis=0, keepdims=True)
        g_scores.append(m1 + m2)
    g_score = jnp.concatenate(g_scores, axis=0)
    g_row = lax.broadcasted_iota(I32, (N_GROUPS, tm), 0)
    g_beaten = jnp.zeros((N_GROUPS, tm), I32)
    for g in range(N_GROUPS):
        r = g_score[g:g + 1, :]
        g_beaten += jnp.where((r > g_score) | ((r == g_score) & (g < g_row)), 1, 0)
    g_keep = g_beaten < TOPK_GROUPS
    masked = jnp.concatenate(
        [jnp.where(g_keep[g:g + 1, :], biased[g * GROUP_SIZE:(g + 1) * GROUP_SIZE], _NEG_INF)
         for g in range(N_GROUPS)], axis=0)

    onehot = jnp.zeros((N_EXPERTS, tm), F32)
    picks, pick_scores = [], []
    for _ in range(TOPK_EXPERTS):
        m = jnp.max(masked, axis=0, keepdims=True)
        idx = jnp.min(jnp.where(masked == m, row, float(N_EXPERTS)), axis=0, keepdims=True)
        hit = row == idx
        picks.append(idx)
        pick_scores.append(jnp.sum(jnp.where(hit, scores, 0.0), axis=0, keepdims=True))
        onehot = onehot + jnp.where(hit, 1.0, 0.0)
        masked = jnp.where(hit, _NEG_INF, masked)
    sel = jnp.concatenate(pick_scores, axis=0)
    e_ref[...] = jnp.concatenate(picks, axis=0).astype(I32)
    w_ref[...] = sel / jnp.sum(sel, axis=0, keepdims=True) * ROUTED_SCALE

    t_r = lax.broadcasted_iota(I32, (tm, tm), 0)
    t_c = lax.broadcasted_iota(I32, (tm, tm), 1)
    earlier = jnp.where(t_r < t_c, 1.0, 0.0).astype(BF16)
    seen = _dot(onehot.astype(BF16), earlier) + carry_ref[...]
    rk_ref[...] = jnp.concatenate(
        [jnp.sum(jnp.where(row == idx, seen, 0.0), axis=0, keepdims=True) for idx in picks], axis=0).astype(I32)
    carry_ref[...] += jnp.sum(onehot, axis=1, keepdims=True)
    cnt_ref[...] = carry_ref[...]


def _route(h_bf, w_router_t, b_router_col):
    k = TOPK_EXPERTS
    tok = lambda i: (0, i)
    return pl.pallas_call(
        _route_body,
        grid=(N_TOK // ROUTE_TM,),
        in_specs=[
            pl.BlockSpec((ROUTE_TM, D_MODEL), lambda i: (i, 0)),
            pl.BlockSpec((N_EXPERTS, D_MODEL), lambda i: (0, 0)),
            pl.BlockSpec((N_EXPERTS, 1), lambda i: (0, 0)),
        ],
        out_specs=[
            pl.BlockSpec((k, ROUTE_TM), tok),
            pl.BlockSpec((k, ROUTE_TM), tok),
            pl.BlockSpec((k, ROUTE_TM), tok),
            pl.BlockSpec((N_EXPERTS, 1), lambda i: (0, 0)),
        ],
        out_shape=[
            jax.ShapeDtypeStruct((k, N_TOK), I32),
            jax.ShapeDtypeStruct((k, N_TOK), F32),
            jax.ShapeDtypeStruct((k, N_TOK), I32),
            jax.ShapeDtypeStruct((N_EXPERTS, 1), F32),
        ],
        scratch_shapes=[pltpu.VMEM((N_EXPERTS, 1), F32)],
        compiler_params=_cparams("arbitrary"),
        name="route",
    )(h_bf, w_router_t, b_router_col)


def _dest_body(cnt_ref, e_ref, rk_ref, d_ref, te_ref, nu_ref, lt_ref):
    cnt = cnt_ref[...]
    tiles = jnp.floor((cnt + (ROW_TILE - 1)) * (1.0 / ROW_TILE))
    er = lax.broadcasted_iota(I32, (N_EXPERTS, N_EXPERTS), 0)
    ec = lax.broadcasted_iota(I32, (N_EXPERTS, N_EXPERTS), 1)
    before = jnp.where(ec < er, 1.0, 0.0).astype(BF16)
    tiles_b = jnp.broadcast_to(tiles, (N_EXPERTS, LANES)).astype(BF16)
    t_start = _dot(before, tiles_b)[:, 0:1]
    t_end = t_start + tiles
    p_start = t_start * float(ROW_TILE)
    lt_ref[...] = jnp.where(tiles > 0.0, (t_end - 1.0) * float(ROW_TILE), -1.0).astype(I32)

    row = lax.broadcasted_iota(I32, (N_EXPERTS, DEST_TM), 0)
    d_ref[...] = jnp.concatenate(
        [jnp.sum(jnp.where(row == e_ref[k:k + 1, :], p_start, 0.0), axis=0, keepdims=True)
         for k in range(TOPK_EXPERTS)], axis=0).astype(I32) + rk_ref[...]

    tile_id = lax.broadcasted_iota(I32, (N_EXPERTS, TILE_TABLE), 1).astype(F32)
    owner = jnp.sum(jnp.where(t_end <= tile_id, 1, 0), axis=0, keepdims=True)
    te_ref[...] = jnp.minimum(owner, N_EXPERTS - 1)
    nu_ref[...] = jnp.broadcast_to(t_end[N_EXPERTS - 1:N_EXPERTS, :], (1, LANES)).astype(I32)


def _dest(cnt, e_t, rk_t):
    k = TOPK_EXPERTS
    tok = lambda i: (0, i)
    return pl.pallas_call(
        _dest_body,
        grid=(N_TOK // DEST_TM,),
        in_specs=[
            pl.BlockSpec((N_EXPERTS, 1), lambda i: (0, 0)),
            pl.BlockSpec((k, DEST_TM), tok),
            pl.BlockSpec((k, DEST_TM), tok),
        ],
        out_specs=[
            pl.BlockSpec((k, DEST_TM), tok),
            pl.BlockSpec((1, TILE_TABLE), lambda i: (0, 0)),
            pl.BlockSpec((1, LANES), lambda i: (0, 0)),
            pl.BlockSpec((N_EXPERTS, 1), lambda i: (0, 0)),
        ],
        out_shape=[
            jax.ShapeDtypeStruct((k, N_TOK), I32),
            jax.ShapeDtypeStruct((1, TILE_TABLE), I32),
            jax.ShapeDtypeStruct((1, LANES), I32),
            jax.ShapeDtypeStruct((N_EXPERTS, 1), I32),
        ],
        compiler_params=_cparams("arbitrary"),
        name="dest",
    )(cnt, e_t, rk_t)


def _row_copy(src, dst, sem):
    return pltpu.make_async_copy(src, dst, sem)


def _dispatch_body(lt_ref, d_ref, h_ref, xs_ref, zero_ref, sem):
    @pl.when(pl.program_id(0) == 0)
    def _():
        zero_ref[...] = jnp.zeros_like(zero_ref)

        def tile_copy(e):
            r0 = pl.multiple_of(jnp.maximum(lt_ref[e], 0), ROW_TILE)
            return pltpu.make_async_copy(zero_ref, xs_ref.at[pl.ds(r0, ROW_TILE)], sem)

        def z_start(e, carry):
            @pl.when(lt_ref[e] >= 0)
            def _():
                tile_copy(e).start()
            return carry

        def z_wait(e, carry):
            @pl.when(lt_ref[e] >= 0)
            def _():
                tile_copy(e).wait()
            return carry

        lax.fori_loop(0, N_EXPERTS, z_start, 0)
        lax.fori_loop(0, N_EXPERTS, z_wait, 0)

    def issue(t, carry):
        for k in range(TOPK_EXPERTS):
            _row_copy(h_ref.at[t], xs_ref.at[d_ref[k, t]], sem).start(priority=k % 2)
        return carry

    lax.fori_loop(0, DISP_TM, issue, 0, unroll=DISPATCH_UNROLL)
    for k in range(TOPK_EXPERTS):
        pltpu.make_async_copy(h_ref, xs_ref.at[pl.ds(0, DISP_TM)], sem).wait()


def _dispatch(last_tile_row, dest_t, h1):
    grid_spec = pltpu.PrefetchScalarGridSpec(
        num_scalar_prefetch=1,
        grid=(N_TOK // DISP_TM,),
        in_specs=[
            pl.BlockSpec((TOPK_EXPERTS, DISP_TM), lambda i, lt: (0, i), memory_space=pltpu.SMEM),
            pl.BlockSpec((DISP_TM, PACKED_CHUNKS, LANES), lambda i, lt: (i, 0, 0)),
        ],
        out_specs=pl.BlockSpec(memory_space=pl.ANY),
        scratch_shapes=[pltpu.VMEM((ROW_TILE, PACKED_CHUNKS, LANES), U32), pltpu.SemaphoreType.DMA],
    )
    return pl.pallas_call(
        _dispatch_body,
        grid_spec=grid_spec,
        out_shape=jax.ShapeDtypeStruct((N_ROWS, PACKED_CHUNKS, LANES), U32),
        compiler_params=_cparams("arbitrary"),
        name="dispatch",
    )(last_tile_row, dest_t, h1)


def _row_destinations(dest_flat):
    info = plsc.get_sparse_core_info()
    n_cores, lanes = info.num_cores, info.num_lanes
    n_workers = n_cores * info.num_subcores
    rows_per_worker = -(-N_ROWS // (n_workers * ROW_TILE)) * ROW_TILE
    assert rows_per_worker % lanes == 0
    mesh = plsc.VectorSubcoreMesh(core_axis_name="c", subcore_axis_name="s")

    @functools.partial(
        pl.kernel, mesh=mesh, out_type=jax.ShapeDtypeStruct((n_workers * rows_per_worker,), I32),
        scratch_types=[pltpu.VMEM((SLOT_CHUNK,), I32), pltpu.VMEM((rows_per_worker,), I32)],
        compiler_params=pltpu.CompilerParams(needs_layout_passes=False), name="row_destinations")
    def invert(dest_hbm, out_hbm, dest_v, table_v):
        first_row = (lax.axis_index("s") * n_cores + lax.axis_index("c")) * rows_per_worker
        lane = lax.iota(I32, lanes)
        tile_shift = ROW_TILE.bit_length() - 1

        def spare(i, carry):
            row = first_row + i * lanes + lane
            buf = lax.rem(lax.shift_right_logical(row, tile_shift), EXPERT_BUFS)
            table_v[pl.ds(i * lanes, lanes)] = TOPK_EXPERTS * N_TOK + buf * ROW_TILE + (row & (ROW_TILE - 1))
            return carry

        lax.fori_loop(0, rows_per_worker // lanes, spare, 0)

        def chunk(c, carry):
            pltpu.sync_copy(dest_hbm.at[pl.ds(c * SLOT_CHUNK, SLOT_CHUNK)], dest_v)

            def vec(i, inner):
                local = dest_v[pl.ds(i * lanes, lanes)] - first_row
                mine = (local >= 0) & (local < rows_per_worker)
                plsc.store_scatter(table_v, [jnp.where(mine, local, 0)], lane + (c * SLOT_CHUNK + i * lanes),
                                   mask=mine)
                return inner

            lax.fori_loop(0, SLOT_CHUNK // lanes, vec, 0)
            return carry

        lax.fori_loop(0, N_ASSIGN // SLOT_CHUNK, chunk, 0)
        pltpu.sync_copy(table_v, out_hbm.at[pl.ds(first_row, rows_per_worker)])

    return invert(dest_flat)


def _experts_body(te_ref, nu_ref, lt_ref, slot_ref, xs_ref, wg_ref, wu_ref, wd_ref, yk_ref,
                  wg_f, wu_f, wd_f, wg_b, wu_b, wd_b, y_buf, n_loaded, sem, w_sem):
    i = pl.program_id(0)
    n_used = nu_ref[0]

    def weight_copies(e, s):
        return (pltpu.make_async_copy(wg_ref.at[e], wg_f.at[s], w_sem.at[s]),
                pltpu.make_async_copy(wu_ref.at[e], wu_f.at[s], w_sem.at[s]),
                pltpu.make_async_copy(wd_ref.at[e], wd_f.at[s], w_sem.at[s]))

    def wait_tile(b):
        pltpu.make_async_copy(y_buf.at[b], yk_ref.at[pl.ds(0, ROW_TILE), :], sem.at[b]).wait()

    def send_tile(b):
        for r in range(ROW_TILE):
            dst = slot_ref[0, 0, r]
            _row_copy(y_buf.at[b, pl.ds(r, 1), :], yk_ref.at[pl.ds(dst, 1), :], sem.at[b]).start(priority=r % 2)

    @pl.when(i == 0)
    def _():
        y_buf[EXPERT_BUFS - 1] = jnp.zeros((ROW_TILE, PACKED_COLS), U32)
        n_loaded[0] = 0
        for c in weight_copies(te_ref[0], 0):
            c.start()

    @pl.when((i >= 2) & (i <= n_used))
    def _():
        wait_tile(lax.rem(i, EXPERT_BUFS))

    @pl.when(i < n_used)
    def _():
        e = te_ref[i]
        prev = te_ref[jnp.maximum(i - 1, 0)]

        @pl.when((i == 0) | (e != prev))
        def _():
            s = lax.rem(n_loaded[0], 2)
            for c in weight_copies(e, s):
                c.wait()
            wg_b[...] = wg_f[s].astype(BF16)
            wu_b[...] = wu_f[s].astype(BF16)
            wd_b[...] = wd_f[s].astype(BF16)
            n_loaded[0] = n_loaded[0] + 1
            nxt = lax.shift_right_logical(lt_ref[e], ROW_TILE.bit_length() - 1) + 1

            @pl.when(nxt < n_used)
            def _():
                for c in weight_copies(te_ref[nxt], 1 - s):
                    c.start()

        send_tile(lax.rem(i + EXPERT_BUFS - 1, EXPERT_BUFS))
        packed = jnp.concatenate([xs_ref[:, c, :] for c in range(PACKED_CHUNKS)], axis=1)
        x_lo, x_hi = _unpack_bf16_pair(packed)
        x = jnp.concatenate([x_lo.astype(BF16), x_hi.astype(BF16)], axis=1)
        g = _dot(x, wg_b[...])
        u = _dot(x, wu_b[...])
        h = (g * _sigmoid(g)) * u
        y_buf[lax.rem(i, EXPERT_BUFS)] = _pack_bf16_pair(_dot(h.astype(BF16), wd_b[...]))

    @pl.when(i == n_used)
    def _():
        send_tile(lax.rem(i + EXPERT_BUFS - 1, EXPERT_BUFS))
        wait_tile(lax.rem(i + EXPERT_BUFS - 2, EXPERT_BUFS))
        wait_tile(lax.rem(i + EXPERT_BUFS - 1, EXPERT_BUFS))


def _experts(tile_expert, n_used, last_tile_row, row_dst, xs, w_gate, w_up, w_down):
    def tile(i, te, nu, lt):
        return (jnp.minimum(i, nu[0] - 1), 0, 0)

    n_table_tiles = row_dst.shape[0] // ROW_TILE
    placeholder = n_table_tiles - 1
    assert placeholder * ROW_TILE >= N_ROWS and placeholder % EXPERT_BUFS == EXPERT_BUFS - 1

    def prev_tile(i, te, nu, lt):
        return (jnp.where(i == 0, placeholder, jnp.minimum(i, nu[0]) - 1), 0, 0)

    grid_spec = pltpu.PrefetchScalarGridSpec(
        num_scalar_prefetch=3,
        grid=(N_ROW_TILES + 1,),
        in_specs=[
            pl.BlockSpec((1, 1, ROW_TILE), prev_tile, memory_space=pltpu.SMEM),
            pl.BlockSpec((ROW_TILE, PACKED_CHUNKS, LANES), tile),
            pl.BlockSpec(memory_space=pl.ANY),
            pl.BlockSpec(memory_space=pl.ANY),
            pl.BlockSpec(memory_space=pl.ANY),
        ],
        out_specs=pl.BlockSpec(memory_space=pl.ANY),
        scratch_shapes=[
            pltpu.VMEM((2, D_MODEL, D_EXPERT), F32),
            pltpu.VMEM((2, D_MODEL, D_EXPERT), F32),
            pltpu.VMEM((2, D_EXPERT, D_MODEL), F32),
            pltpu.VMEM((D_MODEL, D_EXPERT), BF16),
            pltpu.VMEM((D_MODEL, D_EXPERT), BF16),
            pltpu.VMEM((D_EXPERT, D_MODEL), BF16),
            pltpu.VMEM((EXPERT_BUFS, ROW_TILE, PACKED_COLS), U32),
            pltpu.SMEM((1,), I32),
            pltpu.SemaphoreType.DMA((EXPERT_BUFS,)),
            pltpu.SemaphoreType.DMA((2,)),
        ],
    )
    return pl.pallas_call(
        _experts_body,
        grid_spec=grid_spec,
        out_shape=jax.ShapeDtypeStruct((YK_ROWS, PACKED_COLS), U32),
        compiler_params=_cparams("arbitrary"),
        name="experts",
    )(tile_expert, n_used, last_tile_row, row_dst.reshape(n_table_tiles, 1, ROW_TILE), xs, w_gate, w_up, w_down)


def _final_body(w_ref, yk_ref, h_ref, hb_ref, p_ref, wsg_ref, wsu_ref, wsd_ref, wpl_ref, wpg_ref,
                g_ref, b_ref, o_ref):
    hb = hb_ref[...]
    sg = _dot(hb, wsg_ref[...])
    shared = _dot(((sg * _sigmoid(sg)) * _dot(hb, wsu_ref[...])).astype(BF16), wsd_ref[...])
    ple = _sigmoid(_dot(hb, wpg_ref[...])) * _dot(p_ref[...].astype(BF16), wpl_ref[...])

    w_col = w_ref[...].T
    y_lo, y_hi = _unpack_bf16_pair(yk_ref[0])
    r_lo, r_hi = y_lo * w_col[:, 0:1], y_hi * w_col[:, 0:1]
    for k in range(1, TOPK_EXPERTS):
        y_lo, y_hi = _unpack_bf16_pair(yk_ref[k])
        r_lo, r_hi = r_lo + y_lo * w_col[:, k:k + 1], r_hi + y_hi * w_col[:, k:k + 1]
    routed = jnp.concatenate([r_lo, r_hi], axis=1)
    o_ref[...] = _layer_norm(ALPHA * h_ref[...] + (routed + shared) + ple, g_ref[...], b_ref[...])


def _final(w_t, yk, h1, h_bf, p2d, w_s_gate, w_s_up, w_s_down, w_ple, w_ple_gate, ln_g, ln_b):
    d = D_MODEL
    full = lambda i: (0, 0)
    return pl.pallas_call(
        _final_body,
        grid=(N_TOK // FINAL_TM,),
        in_specs=[
            pl.BlockSpec((TOPK_EXPERTS, FINAL_TM), lambda i: (0, i)),
            pl.BlockSpec((TOPK_EXPERTS, FINAL_TM, PACKED_COLS), lambda i: (0, i, 0)),
            pl.BlockSpec((FINAL_TM, d), lambda i: (i, 0)),
            pl.BlockSpec((FINAL_TM, d), lambda i: (i, 0)),
            pl.BlockSpec((FINAL_TM, PLE_DIM), lambda i: (i, 0)),
            pl.BlockSpec((d, D_SHARED), full),
            pl.BlockSpec((d, D_SHARED), full),
            pl.BlockSpec((D_SHARED, d), full),
            pl.BlockSpec((PLE_DIM, d), full),
            pl.BlockSpec((d, d), full),
            pl.BlockSpec((1, d), full),
            pl.BlockSpec((1, d), full),
        ],
        out_specs=pl.BlockSpec((FINAL_TM, d), lambda i: (i, 0)),
        out_shape=jax.ShapeDtypeStruct((N_TOK, d), F32),
        compiler_params=_cparams("parallel"),
        name="combine_ln2",
    )(w_t, yk.reshape(TOPK_EXPERTS + 1, N_TOK, PACKED_COLS), h1, h_bf, p2d, w_s_gate, w_s_up, w_s_down,
      w_ple, w_ple_gate, ln_g, ln_b)


def _rope_tables(positions):
    half = ROT_DIM // 2
    inv = ROPE_THETA ** (-jnp.arange(0, ROT_DIM, 2, dtype=F32) / ROT_DIM)
    per_row = LANES // half
    pos = jnp.repeat(positions.reshape(-1, per_row), half, axis=1).astype(F32)
    ang = pos * jnp.tile(inv, per_row)[None, :]
    cos, sin = lax.optimization_barrier((jnp.cos(ang), jnp.sin(ang)))
    cos = cos.reshape(positions.shape + (half,))
    sin = sin.reshape(positions.shape + (half,))
    rest = MOBA_DH - ROT_DIM
    ones = jnp.ones(cos.shape[:-1] + (rest,), F32)
    zeros = jnp.zeros(cos.shape[:-1] + (rest,), F32)
    return (jnp.concatenate([cos, cos, ones], axis=-1), jnp.concatenate([-sin, sin, zeros], axis=-1))


def _layer(h2d, p2d, cos_t, sin_t, w_cat, w_gk2, b_gk, norm_g, w_gla_o, w_moba_o, w_out, ln1_g, ln1_b,
           w_router, b_router, w_e_gate, w_e_up, w_e_down, w_s_gate, w_s_up, w_s_down, w_ple, w_ple_gate,
           ln2_g, ln2_b):
    w_gk2_pad = jnp.concatenate([w_gk2, jnp.zeros((LANES - GLA_RANK, GLA_KDIM), w_gk2.dtype)], axis=0)

    proj = _in_proj(h2d, w_cat)
    gla_out = _gla(proj, w_gk2_pad, b_gk[None, :], norm_g[None, :])
    moba_out = _moba(proj, cos_t, sin_t)
    h1, h_bf, h_packed = _mix(gla_out, moba_out, proj, h2d, w_gla_o.astype(BF16), w_moba_o.astype(BF16),
                              w_out.astype(BF16), ln1_g[None, :], ln1_b[None, :])
    e_t, w_t, rk_t, cnt = _route(h_bf, w_router.T.astype(BF16), b_router[:, None])
    dest_t, tile_expert, n_used, last_tile_row = _dest(cnt, e_t, rk_t)
    last_tile_row = last_tile_row.reshape(N_EXPERTS)
    xs = _dispatch(last_tile_row, dest_t, h_packed)
    row_dst = _row_destinations(dest_t.reshape(N_ASSIGN))
    yk = _experts(tile_expert.reshape(TILE_TABLE), n_used[0, 0:1], last_tile_row, row_dst, xs,
                  w_e_gate, w_e_up, w_e_down)
    return _final(w_t, yk, h1, h_bf, p2d, w_s_gate.astype(BF16), w_s_up.astype(BF16),
                  w_s_down.astype(BF16), w_ple.astype(BF16), w_ple_gate.astype(BF16),
                  ln2_g[None, :], ln2_b[None, :])


def kernel(x, p, positions, w_in, w_gla_gk2, b_gla_gk, gla_norm_g, w_gla_o, w_moba_o, w_out, ln1_g, ln1_b,
           w_router, b_router, w_e_gate, w_e_up, w_e_down, w_s_gate, w_s_up, w_s_down, w_ple, w_ple_gate,
           ln2_g, ln2_b):
    cos_t, sin_t = _rope_tables(positions)
    h = x.reshape(N_TOK, D_MODEL)
    for i in range(DEPTH):
        h = _layer(h, p[i].reshape(N_TOK, PLE_DIM), cos_t, sin_t, _repack_w_in(w_in, i), w_gla_gk2[i], b_gla_gk[i],
                   gla_norm_g[i], w_gla_o[i], w_moba_o[i], w_out[i], ln1_g[i], ln1_b[i], w_router[i],
                   b_router[i], w_e_gate[i], w_e_up[i], w_e_down[i], w_s_gate[i], w_s_up[i], w_s_down[i],
                   w_ple[i], w_ple_gate[i], ln2_g[i], ln2_b[i])
    return h.reshape(BATCH, SEQ, D_MODEL)
```

```python
import functools

import jax
import jax.numpy as jnp
from jax import lax
from jax.experimental import pallas as pl
from jax.experimental.pallas import tpu as pltpu
from jax.experimental.pallas import tpu_sc as plsc

F32 = jnp.float32
BF16 = jnp.bfloat16
I32 = jnp.int32
U32 = jnp.uint32

LANES = 128
SUBLANES = 8
VMEM_LIMIT_BYTES = 48 * 1024 * 1024

D_MODEL = 1024
BATCH = 8
SEQ = 2048
N_TOK = BATCH * SEQ
GLA_HEADS = 4
GLA_DK = 128
GLA_DV = 256
GLA_RANK = 16
GLA_NORMALIZER = 16.0
GLA_CHUNK = 64
GLA_GROUP = 4 * GLA_CHUNK
GLA_UNROLL = 4
MOBA_HEADS = 8
MOBA_DH = 128
MOBA_BLOCK = 256
MOBA_TOPK = 3
ROT_DIM = 32
ROPE_THETA = 500000.0
N_EXPERTS = 256
TOPK_EXPERTS = 8
N_GROUPS = 8
GROUP_SIZE = N_EXPERTS // N_GROUPS
TOPK_GROUPS = 4
D_EXPERT = 256
D_SHARED = 256
ROUTED_SCALE = 2.5
PLE_DIM = 256
LN_EPS = 1e-5
DEPTH = 1
ALPHA = (2.0 * DEPTH) ** 0.25
GLA_KDIM = GLA_HEADS * GLA_DK
GLA_VDIM = GLA_HEADS * GLA_DV
MOBA_DIM = MOBA_HEADS * MOBA_DH
N_KBLK = SEQ // MOBA_BLOCK

COL_GQ = 0
COL_GK = COL_GQ + GLA_KDIM
COL_GV = COL_GK + GLA_KDIM
COL_GR = COL_GV + GLA_VDIM
COL_MQ = COL_GR + GLA_VDIM
COL_MK = COL_MQ + MOBA_DIM
COL_MV = COL_MK + MOBA_DIM
COL_GA = COL_MV + MOBA_DIM
COL_GB = COL_GA + D_MODEL
COL_LOW = COL_GB + D_MODEL
PROJ_COLS = COL_LOW + LANES
LOW_SRC = 2 * GLA_KDIM + 2 * GLA_VDIM

ROW_TILE = 256
PACKED_COLS = D_MODEL // 2
PACKED_CHUNKS = PACKED_COLS // LANES
EXPERT_BUFS = 3
YK_ROWS = (TOPK_EXPERTS + 1) * N_TOK
SLOT_CHUNK = 4096
assert EXPERT_BUFS * ROW_TILE <= N_TOK
N_ASSIGN = N_TOK * TOPK_EXPERTS
N_ROW_TILES = (N_ASSIGN + N_EXPERTS * (ROW_TILE - 1) + ROW_TILE - 1) // ROW_TILE
N_ROWS = N_ROW_TILES * ROW_TILE
TILE_TABLE = -(-N_ROW_TILES // LANES) * LANES

PROJ_TM = 1024
REPACK_TN = 512
PROJ_TN = PROJ_COLS // 5
MIX_TM = 512
ROUTE_TM = 512
DEST_TM = 2048
DISP_TM = 256
DISPATCH_UNROLL = 4
FINAL_TM = 256

_NEG_INF = float("-inf")
LOG2_E = 1.4426950408889634


def _cparams(*sem):
    return pltpu.CompilerParams(dimension_semantics=sem, vmem_limit_bytes=VMEM_LIMIT_BYTES)


def _dot(a, b):
    return jnp.dot(a, b, preferred_element_type=F32)


def _dot_nt(a, b):
    return lax.dot_general(a, b, (((1,), (1,)), ((), ())), preferred_element_type=F32)


def _dot_tn(a, b):
    return lax.dot_general(a, b, (((0,), (0,)), ((), ())), preferred_element_type=F32)


def _split_bf16(x):
    hi = x.astype(BF16)
    lo = (x - hi.astype(F32)).astype(BF16)
    return hi, lo


def _sigmoid(x):
    return 1.0 / (1.0 + jnp.exp(-x))


def _pack_bf16_pair(x):
    lo = lax.bitcast_convert_type(x[:, :PACKED_COLS].astype(BF16).astype(F32), U32)
    hi = lax.bitcast_convert_type(x[:, PACKED_COLS:].astype(BF16).astype(F32), U32)
    return hi | lax.shift_right_logical(lo, jnp.uint32(16))


def _unpack_bf16_pair(w):
    lo = lax.bitcast_convert_type(lax.shift_left(w, jnp.uint32(16)), F32)
    hi = lax.bitcast_convert_type(w & jnp.uint32(0xFFFF0000), F32)
    return lo, hi


def _repack_body(a_ref, b_ref, o_ref):
    t = pl.program_id(0)
    a = a_ref[...].astype(BF16)

    @pl.when(t < LOW_SRC // REPACK_TN)
    def _():
        o_ref[...] = a

    @pl.when((t >= LOW_SRC // REPACK_TN) & (t < COL_LOW // REPACK_TN))
    def _():
        o_ref[...] = jnp.concatenate([a[:, GLA_RANK:], b_ref[:, :GLA_RANK].astype(BF16)], axis=1)

    @pl.when(t == COL_LOW // REPACK_TN)
    def _():
        o_ref[...] = jnp.concatenate([a[:, :GLA_RANK], jnp.zeros((D_MODEL, REPACK_TN - GLA_RANK), BF16)], axis=1)


def _repack_w_in(w_in, layer):
    assert LOW_SRC % REPACK_TN == 0 and COL_LOW % REPACK_TN == 0
    low_tile = LOW_SRC // REPACK_TN
    last_out = COL_LOW // REPACK_TN
    lanes_per_tile = REPACK_TN // LANES
    return pl.pallas_call(
        _repack_body,
        grid=(last_out + 1,),
        in_specs=[
            pl.BlockSpec((None, D_MODEL, REPACK_TN), lambda t: (layer, 0, jnp.where(t == last_out, low_tile, t))),
            pl.BlockSpec((None, D_MODEL, LANES), lambda t: (layer, 0, jnp.minimum(t + 1, last_out) * lanes_per_tile)),
        ],
        out_specs=pl.BlockSpec((D_MODEL, REPACK_TN), lambda t: (0, t)),
        out_shape=jax.ShapeDtypeStruct((D_MODEL, PROJ_COLS), BF16),
        compiler_params=_cparams("parallel"),
        name="repack_w_in",
    )(w_in, w_in)


def _proj_body(x_ref, w_ref, o_ref):
    o_ref[...] = _dot(x_ref[...].astype(BF16), w_ref[...]).astype(BF16)


def _in_proj(x2d, w_cat):
    return pl.pallas_call(
        _proj_body,
        grid=(PROJ_COLS // PROJ_TN, N_TOK // PROJ_TM),
        in_specs=[
            pl.BlockSpec((PROJ_TM, D_MODEL), lambda j, i: (i, 0)),
            pl.BlockSpec((D_MODEL, PROJ_TN), lambda j, i: (0, j)),
        ],
        out_specs=pl.BlockSpec((PROJ_TM, PROJ_TN), lambda j, i: (i, j)),
        out_shape=jax.ShapeDtypeStruct((N_TOK, PROJ_COLS), BF16),
        compiler_params=_cparams("parallel", "parallel"),
        name="in_proj",
    )(x2d, w_cat)


def _gla_body(q_ref, k_ref, v_ref, r_ref, low_ref, wg_ref, bg_ref, ng_ref, o_ref, st_ref, gk_ref):
    w_hi, w_lo = _split_bf16(wg_ref[...])
    low = low_ref[...]
    lin = _dot(low, w_hi) + _dot(low, w_lo) + bg_ref[...]
    gk_ref[...] = (jnp.minimum(lin, 0.0) - jnp.log1p(jnp.exp(-jnp.abs(lin)))) * (1.0 / GLA_NORMALIZER)
    st_ref[...] = jnp.zeros_like(st_ref)

    ri = lax.broadcasted_iota(I32, (GLA_GROUP, GLA_GROUP), 0)
    ci = lax.broadcasted_iota(I32, (GLA_GROUP, GLA_GROUP), 1)
    same_chunk = lax.shift_right_logical(ri, GLA_CHUNK.bit_length() - 1) == lax.shift_right_logical(
        ci, GLA_CHUNK.bit_length() - 1)
    causal = same_chunk & (ri >= ci)
    sums = jnp.concatenate([jnp.where(causal, 1.0, 0.0), jnp.where(same_chunk, 1.0, 0.0)], axis=0).astype(BF16)
    gain = ng_ref[...]

    def group(c, carry):
        rows = pl.ds(pl.multiple_of(c * GLA_GROUP, GLA_GROUP), GLA_GROUP)
        g_hi, g_lo = _split_bf16(gk_ref[rows, :])
        bb = _dot(sums, g_hi) + _dot(sums, g_lo)
        b = bb[0:GLA_GROUP]
        b_end = bb[GLA_GROUP:2 * GLA_GROUP]
        q = q_ref[rows, :].astype(F32) * (GLA_DK ** -0.5)
        k = k_ref[rows, :].astype(F32)
        v = v_ref[rows, :]
        q_e = (q * jnp.exp(b)).astype(BF16)
        k_e = (k * jnp.exp(-b)).astype(BF16)
        k_d = (k * jnp.exp(b_end - b)).astype(BF16)
        att = jnp.where(causal, _dot_nt(q_e, k_e), 0.0)
        o = _dot(att.astype(BF16), v)
        st = st_ref[...]
        inter = []
        for j in range(GLA_GROUP // GLA_CHUNK):
            cr = slice(j * GLA_CHUNK, (j + 1) * GLA_CHUNK)
            inter.append(_dot_nt(q_e[cr], st.astype(BF16)))
            st = st * jnp.exp(b_end[j * GLA_CHUNK:j * GLA_CHUNK + 1, :]) + _dot_tn(v[cr], k_d[cr])
        st_ref[...] = st
        o = o + jnp.concatenate(inter, axis=0)
        o = o * lax.rsqrt(jnp.mean(o * o, axis=-1, keepdims=True) + LN_EPS) * gain
        r = r_ref[rows, :].astype(F32)
        o_ref[rows, :] = (o * (r * _sigmoid(r))).astype(BF16)
        return carry

    lax.fori_loop(0, SEQ // GLA_GROUP, group, 0, unroll=GLA_UNROLL)


def _gla(proj, w_gk2_pad, b_gk, norm_g):
    kb, vb = GLA_DK, GLA_DV
    return pl.pallas_call(
        _gla_body,
        grid=(BATCH, GLA_HEADS),
        in_specs=[
            pl.BlockSpec((SEQ, kb), lambda b, h: (b, COL_GQ // kb + h)),
            pl.BlockSpec((SEQ, kb), lambda b, h: (b, COL_GK // kb + h)),
            pl.BlockSpec((SEQ, vb), lambda b, h: (b, COL_GV // vb + h)),
            pl.BlockSpec((SEQ, vb), lambda b, h: (b, COL_GR // vb + h)),
            pl.BlockSpec((SEQ, LANES), lambda b, h: (b, COL_LOW // LANES)),
            pl.BlockSpec((LANES, kb), lambda b, h: (0, h)),
            pl.BlockSpec((1, kb), lambda b, h: (0, h)),
            pl.BlockSpec((1, vb), lambda b, h: (0, 0)),
        ],
        out_specs=pl.BlockSpec((SEQ, vb), lambda b, h: (b, h)),
        out_shape=jax.ShapeDtypeStruct((N_TOK, GLA_VDIM), BF16),
        scratch_shapes=[pltpu.VMEM((vb, kb), F32), pltpu.VMEM((SEQ, kb), F32)],
        compiler_params=_cparams("parallel", "parallel"),
        name="gla",
    )(proj, proj, proj, proj, proj, w_gk2_pad, b_gk, norm_g)


def _moba_body(q_ref, k_ref, v_ref, c_ref, s_ref, o_ref, qs_ref, ks_ref, vt_ref, bias_ref,
               sc_a_ref, sc_b_ref, pr_a_ref, pr_b_ref):
    cos_t = c_ref[0]
    sin_t = s_ref[0]
    lane = lax.broadcasted_iota(I32, (SEQ, MOBA_DH), 1)
    half = ROT_DIM // 2

    def rope(x):
        partner = jnp.where(lane < half, pltpu.roll(x, MOBA_DH - half, 1), pltpu.roll(x, half, 1))
        return x * cos_t + partner * sin_t

    q = rope(q_ref[...].astype(F32))
    k = rope(k_ref[...].astype(F32))
    q_hi, q_lo = _split_bf16(q)
    qs_ref[...] = (q * (MOBA_DH ** -0.5 * LOG2_E)).astype(BF16)
    ks_ref[...] = k.astype(BF16)
    vt_ref[...] = v_ref[...].astype(F32).T.astype(BF16)

    k_mean = jnp.concatenate(
        [jnp.mean(k[j * MOBA_BLOCK:(j + 1) * MOBA_BLOCK], axis=0, keepdims=True) for j in range(N_KBLK)], axis=0)
    m_hi, m_lo = _split_bf16(k_mean)
    s_blk = _dot_nt(m_hi, q_hi) + _dot_nt(m_hi, q_lo) + _dot_nt(m_lo, q_hi)
    blk = lax.broadcasted_iota(I32, (N_KBLK, SEQ), 0)
    q_blk = lax.shift_right_logical(lax.broadcasted_iota(I32, (N_KBLK, SEQ), 1), MOBA_BLOCK.bit_length() - 1)
    past = blk < q_blk
    s_blk = jnp.where(past, s_blk, _NEG_INF)
    beaten = jnp.zeros((N_KBLK, SEQ), I32)
    for j in range(N_KBLK):
        row = s_blk[j:j + 1, :]
        beaten += jnp.where((row > s_blk) | ((row == s_blk) & (j < blk)), 1, 0)
    bias_ref[...] = jnp.where(past & (beaten < MOBA_TOPK), 0.0, _NEG_INF)

    kr = lax.broadcasted_iota(I32, (MOBA_BLOCK, MOBA_BLOCK), 0)
    qc = lax.broadcasted_iota(I32, (MOBA_BLOCK, MOBA_BLOCK), 1)
    own_bias = jnp.where(kr <= qc, 0.0, _NEG_INF)

    sc_bufs = (sc_a_ref, sc_b_ref)
    pr_bufs = (pr_a_ref, pr_b_ref)
    zeros_row = jnp.zeros((1, MOBA_BLOCK), F32)
    future_row = jnp.full((1, MOBA_BLOCK), _NEG_INF, F32)
    for pair in range(N_KBLK // 2):
        q_blocks = (2 * pair, 2 * pair + 1)
        cols = slice(q_blocks[0] * MOBA_BLOCK, (q_blocks[1] + 1) * MOBA_BLOCK)
        n_kblk = q_blocks[1] + 1
        n_keys = n_kblk * MOBA_BLOCK
        sc, pr = sc_bufs[pair % 2], pr_bufs[pair % 2]
        q_pair = qs_ref[cols, :]

        def query_bias(j):
            halves = []
            for qb in q_blocks:
                q_cols = slice(qb * MOBA_BLOCK, (qb + 1) * MOBA_BLOCK)
                halves.append(bias_ref[j:j + 1, q_cols] if j < qb else zeros_row if j == qb else future_row)
            return jnp.concatenate(halves, axis=1)

        biases = [query_bias(j) for j in range(n_kblk)]
        col_max = []
        for j in range(n_kblk):
            rows = slice(j * MOBA_BLOCK, (j + 1) * MOBA_BLOCK)
            s = _dot_nt(ks_ref[rows, :], q_pair)
            if j in q_blocks:
                h = q_blocks.index(j)
                own = s[:, h * MOBA_BLOCK:(h + 1) * MOBA_BLOCK] + own_bias
                s = jnp.concatenate([own, s[:, MOBA_BLOCK:]] if h == 0 else [s[:, :MOBA_BLOCK], own], axis=1)
            sc[rows, :] = s
            col_max.append(jnp.max(s, axis=0, keepdims=True) + biases[j])
        m = functools.reduce(jnp.maximum, col_max)
        denom = jnp.zeros((1, 2 * MOBA_BLOCK), F32)
        for j in range(n_kblk):
            rows = slice(j * MOBA_BLOCK, (j + 1) * MOBA_BLOCK)
            p = jnp.exp2(sc[rows, :] - (m - biases[j]))
            denom = denom + jnp.sum(p, axis=0, keepdims=True)
            pr[rows, :] = p.astype(BF16)
        o_t = _dot(vt_ref[:, 0:n_keys], pr[0:n_keys, :]) * (1.0 / denom)
        o_ref[cols, :] = o_t.T.astype(BF16)


def _moba(proj, cos_t, sin_t):
    dh = MOBA_DH
    return pl.pallas_call(
        _moba_body,
        grid=(BATCH, MOBA_HEADS),
        in_specs=[
            pl.BlockSpec((SEQ, dh), lambda b, h: (b, COL_MQ // dh + h)),
            pl.BlockSpec((SEQ, dh), lambda b, h: (b, COL_MK // dh + h)),
            pl.BlockSpec((SEQ, dh), lambda b, h: (b, COL_MV // dh + h)),
            pl.BlockSpec((1, SEQ, dh), lambda b, h: (b, 0, 0)),
            pl.BlockSpec((1, SEQ, dh), lambda b, h: (b, 0, 0)),
        ],
        out_specs=pl.BlockSpec((SEQ, dh), lambda b, h: (b, h)),
        out_shape=jax.ShapeDtypeStruct((N_TOK, MOBA_DIM), BF16),
        scratch_shapes=[
            pltpu.VMEM((SEQ, dh), BF16),
            pltpu.VMEM((SEQ, dh), BF16),
            pltpu.VMEM((dh, SEQ), BF16),
            pltpu.VMEM((N_KBLK, SEQ), F32),
            pltpu.VMEM((SEQ, 2 * MOBA_BLOCK), F32),
            pltpu.VMEM((SEQ, 2 * MOBA_BLOCK), F32),
            pltpu.VMEM((SEQ, 2 * MOBA_BLOCK), BF16),
            pltpu.VMEM((SEQ, 2 * MOBA_BLOCK), BF16),
        ],
        compiler_params=_cparams("parallel", "parallel"),
        name="moba",
    )(proj, proj, proj, cos_t, sin_t)


def _layer_norm(z, g, b):
    mu = jnp.mean(z, axis=-1, keepdims=True)
    zc = z - mu
    var = jnp.mean(zc * zc, axis=-1, keepdims=True)
    return zc * lax.rsqrt(var + LN_EPS) * g + b


def _mix_body(gla_ref, moba_ref, ga_ref, gb_ref, x_ref, wgo_ref, wmo_ref, wo_ref, g_ref, b_ref,
              h_ref, hb_ref, hp_ref):
    y_gla = _dot(gla_ref[...], wgo_ref[...])
    y_moba = _dot(moba_ref[...], wmo_ref[...])
    merged = _sigmoid(ga_ref[...].astype(F32)) * y_gla + _sigmoid(gb_ref[...].astype(F32)) * y_moba
    mix = _dot(merged.astype(BF16), wo_ref[...])
    h = _layer_norm(ALPHA * x_ref[...] + mix, g_ref[...], b_ref[...])
    h_ref[...] = h
    hb_ref[...] = h.astype(BF16)
    packed = _pack_bf16_pair(h)
    for c in range(PACKED_CHUNKS):
        hp_ref[:, c, :] = packed[:, c * LANES:(c + 1) * LANES]


def _mix(gla_out, moba_out, proj, x2d, w_gla_o, w_moba_o, w_out, ln_g, ln_b):
    d = D_MODEL
    row = lambda i: (i, 0)
    full = lambda i: (0, 0)
    return pl.pallas_call(
        _mix_body,
        grid=(N_TOK // MIX_TM,),
        in_specs=[
            pl.BlockSpec((MIX_TM, d), row),
            pl.BlockSpec((MIX_TM, d), row),
            pl.BlockSpec((MIX_TM, d), lambda i: (i, COL_GA // d)),
            pl.BlockSpec((MIX_TM, d), lambda i: (i, COL_GB // d)),
            pl.BlockSpec((MIX_TM, d), row),
            pl.BlockSpec((d, d), full),
            pl.BlockSpec((d, d), full),
            pl.BlockSpec((d, d), full),
            pl.BlockSpec((1, d), full),
            pl.BlockSpec((1, d), full),
        ],
        out_specs=[
            pl.BlockSpec((MIX_TM, d), row),
            pl.BlockSpec((MIX_TM, d), row),
            pl.BlockSpec((MIX_TM, PACKED_CHUNKS, LANES), lambda i: (i, 0, 0)),
        ],
        out_shape=[
            jax.ShapeDtypeStruct((N_TOK, d), F32),
            jax.ShapeDtypeStruct((N_TOK, d), BF16),
            jax.ShapeDtypeStruct((N_TOK, PACKED_CHUNKS, LANES), U32),
        ],
        compiler_params=_cparams("parallel"),
        name="mix_ln1",
    )(gla_out, moba_out, proj, proj, x2d, w_gla_o, w_moba_o, w_out, ln_g, ln_b)


def _route_body(h_ref, wr_ref, br_ref, e_ref, w_ref, rk_ref, cnt_ref, carry_ref):
    tm = ROUTE_TM

    @pl.when(pl.program_id(0) == 0)
    def _():
        carry_ref[...] = jnp.zeros_like(carry_ref)

    scores = _sigmoid(_dot_nt(wr_ref[...], h_ref[...]))
    biased = scores + br_ref[...]
    row = lax.broadcasted_iota(I32, (N_EXPERTS, tm), 0).astype(F32)
    row_g = lax.broadcasted_iota(I32, (GROUP_SIZE, tm), 0).astype(F32)

    g_scores = []
    for g in range(N_GROUPS):
        grp = biased[g * GROUP_SIZE:(g + 1) * GROUP_SIZE]
        m1 = jnp.max(grp, axis=0, keepdims=True)
        first = jnp.min(jnp.where(grp == m1, row_g, float(GROUP_SIZE)), axis=0, keepdims=True)
        m2 = jnp.max(jnp.where(row_g == first, _NEG_INF, grp), axis=0, keepdims=True)
        g_scores.append(m1 + m2)
    g_score = jnp.concatenate(g_scores, axis=0)
    g_row = lax.broadcasted_iota(I32, (N_GROUPS, tm), 0)
    g_beaten = jnp.zeros((N_GROUPS, tm), I32)
    for g in range(N_GROUPS):
        r = g_score[g:g + 1, :]
        g_beaten += jnp.where((r > g_score) | ((r == g_score) & (g < g_row)), 1, 0)
    g_keep = g_beaten < TOPK_GROUPS
    masked = jnp.concatenate(
        [jnp.where(g_keep[g:g + 1, :], biased[g * GROUP_SIZE:(g + 1) * GROUP_SIZE], _NEG_INF)
         for g in range(N_GROUPS)], axis=0)

    onehot = jnp.zeros((N_EXPERTS, tm), F32)
    picks, pick_scores = [], []
    for _ in range(TOPK_EXPERTS):
        m = jnp.max(masked, axis=0, keepdims=True)
        idx = jnp.min(jnp.where(masked == m, row, float(N_EXPERTS)), axis=0, keepdims=True)
        hit = row == idx
        picks.append(idx)
        pick_scores.append(jnp.sum(jnp.where(hit, scores, 0.0), axis=0, keepdims=True))
        onehot = onehot + jnp.where(hit, 1.0, 0.0)
        masked = jnp.where(hit, _NEG_INF, masked)
    sel = jnp.concatenate(pick_scores, axis=0)
    e_ref[...] = jnp.concatenate(picks, axis=0).astype(I32)
    w_ref[...] = sel / jnp.sum(sel, axis=0, keepdims=True) * ROUTED_SCALE

    t_r = lax.broadcasted_iota(I32, (tm, tm), 0)
    t_c = lax.broadcasted_iota(I32, (tm, tm), 1)
    earlier = jnp.where(t_r < t_c, 1.0, 0.0).astype(BF16)
    seen = _dot(onehot.astype(BF16), earlier) + carry_ref[...]
    rk_ref[...] = jnp.concatenate(
        [jnp.sum(jnp.where(row == idx, seen, 0.0), axis=0, keepdims=True) for idx in picks], axis=0).astype(I32)
    carry_ref[...] += jnp.sum(onehot, axis=1, keepdims=True)
    cnt_ref[...] = carry_ref[...]


def _route(h_bf, w_router_t, b_router_col):
    k = TOPK_EXPERTS
    tok = lambda i: (0, i)
    return pl.pallas_call(
        _route_body,
        grid=(N_TOK // ROUTE_TM,),
        in_specs=[
            pl.BlockSpec((ROUTE_TM, D_MODEL), lambda i: (i, 0)),
            pl.BlockSpec((N_EXPERTS, D_MODEL), lambda i: (0, 0)),
            pl.BlockSpec((N_EXPERTS, 1), lambda i: (0, 0)),
        ],
        out_specs=[
            pl.BlockSpec((k, ROUTE_TM), tok),
            pl.BlockSpec((k, ROUTE_TM), tok),
            pl.BlockSpec((k, ROUTE_TM), tok),
            pl.BlockSpec((N_EXPERTS, 1), lambda i: (0, 0)),
        ],
        out_shape=[
            jax.ShapeDtypeStruct((k, N_TOK), I32),
            jax.ShapeDtypeStruct((k, N_TOK), F32),
            jax.ShapeDtypeStruct((k, N_TOK), I32),
            jax.ShapeDtypeStruct((N_EXPERTS, 1), F32),
        ],
        scratch_shapes=[pltpu.VMEM((N_EXPERTS, 1), F32)],
        compiler_params=_cparams("arbitrary"),
        name="route",
    )(h_bf, w_router_t, b_router_col)


def _dest_body(cnt_ref, e_ref, rk_ref, d_ref, te_ref, nu_ref, lt_ref):
    cnt = cnt_ref[...]
    tiles = jnp.floor((cnt + (ROW_TILE - 1)) * (1.0 / ROW_TILE))
    er = lax.broadcasted_iota(I32, (N_EXPERTS, N_EXPERTS), 0)
    ec = lax.broadcasted_iota(I32, (N_EXPERTS, N_EXPERTS), 1)
    before = jnp.where(ec < er, 1.0, 0.0).astype(BF16)
    tiles_b = jnp.broadcast_to(tiles, (N_EXPERTS, LANES)).astype(BF16)
    t_start = _dot(before, tiles_b)[:, 0:1]
    t_end = t_start + tiles
    p_start = t_start * float(ROW_TILE)
    lt_ref[...] = jnp.where(tiles > 0.0, (t_end - 1.0) * float(ROW_TILE), -1.0).astype(I32)

    row = lax.broadcasted_iota(I32, (N_EXPERTS, DEST_TM), 0)
    d_ref[...] = jnp.concatenate(
        [jnp.sum(jnp.where(row == e_ref[k:k + 1, :], p_start, 0.0), axis=0, keepdims=True)
         for k in range(TOPK_EXPERTS)], axis=0).astype(I32) + rk_ref[...]

    tile_id = lax.broadcasted_iota(I32, (N_EXPERTS, TILE_TABLE), 1).astype(F32)
    owner = jnp.sum(jnp.where(t_end <= tile_id, 1, 0), axis=0, keepdims=True)
    te_ref[...] = jnp.minimum(owner, N_EXPERTS - 1)
    nu_ref[...] = jnp.broadcast_to(t_end[N_EXPERTS - 1:N_EXPERTS, :], (1, LANES)).astype(I32)


def _dest(cnt, e_t, rk_t):
    k = TOPK_EXPERTS
    tok = lambda i: (0, i)
    return pl.pallas_call(
        _dest_body,
        grid=(N_TOK // DEST_TM,),
        in_specs=[
            pl.BlockSpec((N_EXPERTS, 1), lambda i: (0, 0)),
            pl.BlockSpec((k, DEST_TM), tok),
            pl.BlockSpec((k, DEST_TM), tok),
        ],
        out_specs=[
            pl.BlockSpec((k, DEST_TM), tok),
            pl.BlockSpec((1, TILE_TABLE), lambda i: (0, 0)),
            pl.BlockSpec((1, LANES), lambda i: (0, 0)),
            pl.BlockSpec((N_EXPERTS, 1), lambda i: (0, 0)),
        ],
        out_shape=[
            jax.ShapeDtypeStruct((k, N_TOK), I32),
            jax.ShapeDtypeStruct((1, TILE_TABLE), I32),
            jax.ShapeDtypeStruct((1, LANES), I32),
            jax.ShapeDtypeStruct((N_EXPERTS, 1), I32),
        ],
        compiler_params=_cparams("arbitrary"),
        name="dest",
    )(cnt, e_t, rk_t)


def _row_copy(src, dst, sem):
    return pltpu.make_async_copy(src, dst, sem)


def _dispatch_body(lt_ref, d_ref, h_ref, xs_ref, zero_ref, sem):
    @pl.when(pl.program_id(0) == 0)
    def _():
        zero_ref[...] = jnp.zeros_like(zero_ref)

        def tile_copy(e):
            r0 = pl.multiple_of(jnp.maximum(lt_ref[e], 0), ROW_TILE)
            return pltpu.make_async_copy(zero_ref, xs_ref.at[pl.ds(r0, ROW_TILE)], sem)

        def z_start(e, carry):
            @pl.when(lt_ref[e] >= 0)
            def _():
                tile_copy(e).start()
            return carry

        def z_wait(e, carry):
            @pl.when(lt_ref[e] >= 0)
            def _():
                tile_copy(e).wait()
            return carry

        lax.fori_loop(0, N_EXPERTS, z_start, 0)
        lax.fori_loop(0, N_EXPERTS, z_wait, 0)

    def issue(t, carry):
        for k in range(TOPK_EXPERTS):
            _row_copy(h_ref.at[t], xs_ref.at[d_ref[k, t]], sem).start(priority=k % 2)
        return carry

    lax.fori_loop(0, DISP_TM, issue, 0, unroll=DISPATCH_UNROLL)
    for k in range(TOPK_EXPERTS):
        pltpu.make_async_copy(h_ref, xs_ref.at[pl.ds(0, DISP_TM)], sem).wait()


def _dispatch(last_tile_row, dest_t, h1):
    grid_spec = pltpu.PrefetchScalarGridSpec(
        num_scalar_prefetch=1,
        grid=(N_TOK // DISP_TM,),
        in_specs=[
            pl.BlockSpec((TOPK_EXPERTS, DISP_TM), lambda i, lt: (0, i), memory_space=pltpu.SMEM),
            pl.BlockSpec((DISP_TM, PACKED_CHUNKS, LANES), lambda i, lt: (i, 0, 0)),
        ],
        out_specs=pl.BlockSpec(memory_space=pl.ANY),
        scratch_shapes=[pltpu.VMEM((ROW_TILE, PACKED_CHUNKS, LANES), U32), pltpu.SemaphoreType.DMA],
    )
    return pl.pallas_call(
        _dispatch_body,
        grid_spec=grid_spec,
        out_shape=jax.ShapeDtypeStruct((N_ROWS, PACKED_CHUNKS, LANES), U32),
        compiler_params=_cparams("arbitrary"),
        name="dispatch",
    )(last_tile_row, dest_t, h1)


def _row_destinations(dest_flat):
    info = plsc.get_sparse_core_info()
    n_cores, lanes = info.num_cores, info.num_lanes
    n_workers = n_cores * info.num_subcores
    rows_per_worker = -(-N_ROWS // (n_workers * ROW_TILE)) * ROW_TILE
    assert rows_per_worker % lanes == 0
    mesh = plsc.VectorSubcoreMesh(core_axis_name="c", subcore_axis_name="s")

    @functools.partial(
        pl.kernel, mesh=mesh, out_type=jax.ShapeDtypeStruct((n_workers * rows_per_worker,), I32),
        scratch_types=[pltpu.VMEM((SLOT_CHUNK,), I32), pltpu.VMEM((rows_per_worker,), I32)],
        compiler_params=pltpu.CompilerParams(needs_layout_passes=False), name="row_destinations")
    def invert(dest_hbm, out_hbm, dest_v, table_v):
        first_row = (lax.axis_index("s") * n_cores + lax.axis_index("c")) * rows_per_worker
        lane = lax.iota(I32, lanes)
        tile_shift = ROW_TILE.bit_length() - 1

        def spare(i, carry):
            row = first_row + i * lanes + lane
            buf = lax.rem(lax.shift_right_logical(row, tile_shift), EXPERT_BUFS)
            table_v[pl.ds(i * lanes, lanes)] = TOPK_EXPERTS * N_TOK + buf * ROW_TILE + (row & (ROW_TILE - 1))
            return carry

        lax.fori_loop(0, rows_per_worker // lanes, spare, 0)

        def chunk(c, carry):
            pltpu.sync_copy(dest_hbm.at[pl.ds(c * SLOT_CHUNK, SLOT_CHUNK)], dest_v)

            def vec(i, inner):
                local = dest_v[pl.ds(i * lanes, lanes)] - first_row
                mine = (local >= 0) & (local < rows_per_worker)
                plsc.store_scatter(table_v, [jnp.where(mine, local, 0)], lane + (c * SLOT_CHUNK + i * lanes),
                                   mask=mine)
                return inner

            lax.fori_loop(0, SLOT_CHUNK // lanes, vec, 0)
            return carry

        lax.fori_loop(0, N_ASSIGN // SLOT_CHUNK, chunk, 0)
        pltpu.sync_copy(table_v, out_hbm.at[pl.ds(first_row, rows_per_worker)])

    return invert(dest_flat)


def _experts_body(te_ref, nu_ref, lt_ref, slot_ref, xs_ref, wg_ref, wu_ref, wd_ref, yk_ref,
                  wg_f, wu_f, wd_f, wg_b, wu_b, wd_b, y_buf, n_loaded, sem, w_sem):
    i = pl.program_id(0)
    n_used = nu_ref[0]

    def weight_copies(e, s):
        return (pltpu.make_async_copy(wg_ref.at[e], wg_f.at[s], w_sem.at[s]),
                pltpu.make_async_copy(wu_ref.at[e], wu_f.at[s], w_sem.at[s]),
                pltpu.make_async_copy(wd_ref.at[e], wd_f.at[s], w_sem.at[s]))

    def wait_tile(b):
        pltpu.make_async_copy(y_buf.at[b], yk_ref.at[pl.ds(0, ROW_TILE), :], sem.at[b]).wait()

    def send_tile(b):
        for r in range(ROW_TILE):
            dst = slot_ref[0, 0, r]
            _row_copy(y_buf.at[b, pl.ds(r, 1), :], yk_ref.at[pl.ds(dst, 1), :], sem.at[b]).start(priority=r % 2)

    @pl.when(i == 0)
    def _():
        y_buf[EXPERT_BUFS - 1] = jnp.zeros((ROW_TILE, PACKED_COLS), U32)
        n_loaded[0] = 0
        for c in weight_copies(te_ref[0], 0):
            c.start()

    @pl.when((i >= 2) & (i <= n_used))
    def _():
        wait_tile(lax.rem(i, EXPERT_BUFS))

    @pl.when(i < n_used)
    def _():
        e = te_ref[i]
        prev = te_ref[jnp.maximum(i - 1, 0)]

        @pl.when((i == 0) | (e != prev))
        def _():
            s = lax.rem(n_loaded[0], 2)
            for c in weight_copies(e, s):
                c.wait()
            wg_b[...] = wg_f[s].astype(BF16)
            wu_b[...] = wu_f[s].astype(BF16)
            wd_b[...] = wd_f[s].astype(BF16)
            n_loaded[0] = n_loaded[0] + 1
            nxt = lax.shift_right_logical(lt_ref[e], ROW_TILE.bit_length() - 1) + 1

            @pl.when(nxt < n_used)
            def _():
                for c in weight_copies(te_ref[nxt], 1 - s):
                    c.start()

        for phase in range(EXPERT_BUFS):
            @pl.when(lax.rem(i, EXPERT_BUFS) == phase)
            def _(phase=phase):
                send_tile((phase + EXPERT_BUFS - 1) % EXPERT_BUFS)
                packed = jnp.concatenate([xs_ref[:, c, :] for c in range(PACKED_CHUNKS)], axis=1)
                x_lo, x_hi = _unpack_bf16_pair(packed)
                x = jnp.concatenate([x_lo.astype(BF16), x_hi.astype(BF16)], axis=1)
                g = _dot(x, wg_b[...])
                u = _dot(x, wu_b[...])
                h = (g * _sigmoid(g)) * u
                y_buf[phase] = _pack_bf16_pair(_dot(h.astype(BF16), wd_b[...]))

    @pl.when(i == n_used)
    def _():
        send_tile(lax.rem(i + EXPERT_BUFS - 1, EXPERT_BUFS))
        wait_tile(lax.rem(i + EXPERT_BUFS - 2, EXPERT_BUFS))
        wait_tile(lax.rem(i + EXPERT_BUFS - 1, EXPERT_BUFS))


def _experts(tile_expert, n_used, last_tile_row, row_dst, xs, w_gate, w_up, w_down):
    def tile(i, te, nu, lt):
        return (jnp.minimum(i, nu[0] - 1), 0, 0)

    n_table_tiles = row_dst.shape[0] // ROW_TILE
    placeholder = n_table_tiles - 1
    assert placeholder * ROW_TILE >= N_ROWS and placeholder % EXPERT_BUFS == EXPERT_BUFS - 1

    def prev_tile(i, te, nu, lt):
        return (jnp.where(i == 0, placeholder, jnp.minimum(i, nu[0]) - 1), 0, 0)

    grid_spec = pltpu.PrefetchScalarGridSpec(
        num_scalar_prefetch=3,
        grid=(N_ROW_TILES + 1,),
        in_specs=[
            pl.BlockSpec((1, 1, ROW_TILE), prev_tile, memory_space=pltpu.SMEM),
            pl.BlockSpec((ROW_TILE, PACKED_CHUNKS, LANES), tile),
            pl.BlockSpec(memory_space=pl.ANY),
            pl.BlockSpec(memory_space=pl.ANY),
            pl.BlockSpec(memory_space=pl.ANY),
        ],
        out_specs=pl.BlockSpec(memory_space=pl.ANY),
        scratch_shapes=[
            pltpu.VMEM((2, D_MODEL, D_EXPERT), F32),
            pltpu.VMEM((2, D_MODEL, D_EXPERT), F32),
            pltpu.VMEM((2, D_EXPERT, D_MODEL), F32),
            pltpu.VMEM((D_MODEL, D_EXPERT), BF16),
            pltpu.VMEM((D_MODEL, D_EXPERT), BF16),
            pltpu.VMEM((D_EXPERT, D_MODEL), BF16),
            pltpu.VMEM((EXPERT_BUFS, ROW_TILE, PACKED_COLS), U32),
            pltpu.SMEM((1,), I32),
            pltpu.SemaphoreType.DMA((EXPERT_BUFS,)),
            pltpu.SemaphoreType.DMA((2,)),
        ],
    )
    return pl.pallas_call(
        _experts_body,
        grid_spec=grid_spec,
        out_shape=jax.ShapeDtypeStruct((YK_ROWS, PACKED_COLS), U32),
        compiler_params=_cparams("arbitrary"),
        name="experts",
    )(tile_expert, n_used, last_tile_row, row_dst.reshape(n_table_tiles, 1, ROW_TILE), xs, w_gate, w_up, w_down)


def _final_body(w_ref, yk_ref, h_ref, hb_ref, p_ref, wsg_ref, wsu_ref, wsd_ref, wpl_ref, wpg_ref,
                g_ref, b_ref, o_ref):
    hb = hb_ref[...]
    sg = _dot(hb, wsg_ref[...])
    shared = _dot(((sg * _sigmoid(sg)) * _dot(hb, wsu_ref[...])).astype(BF16), wsd_ref[...])
    ple = _sigmoid(_dot(hb, wpg_ref[...])) * _dot(p_ref[...].astype(BF16), wpl_ref[...])

    w_col = w_ref[...].T
    y_lo, y_hi = _unpack_bf16_pair(yk_ref[0])
    r_lo, r_hi = y_lo * w_col[:, 0:1], y_hi * w_col[:, 0:1]
    for k in range(1, TOPK_EXPERTS):
        y_lo, y_hi = _unpack_bf16_pair(yk_ref[k])
        r_lo, r_hi = r_lo + y_lo * w_col[:, k:k + 1], r_hi + y_hi * w_col[:, k:k + 1]
    routed = jnp.concatenate([r_lo, r_hi], axis=1)
    o_ref[...] = _layer_norm(ALPHA * h_ref[...] + (routed + shared) + ple, g_ref[...], b_ref[...])


def _final(w_t, yk, h1, h_bf, p2d, w_s_gate, w_s_up, w_s_down, w_ple, w_ple_gate, ln_g, ln_b):
    d = D_MODEL
    full = lambda i: (0, 0)
    return pl.pallas_call(
        _final_body,
        grid=(N_TOK // FINAL_TM,),
        in_specs=[
            pl.BlockSpec((TOPK_EXPERTS, FINAL_TM), lambda i: (0, i)),
            pl.BlockSpec((TOPK_EXPERTS, FINAL_TM, PACKED_COLS), lambda i: (0, i, 0)),
            pl.BlockSpec((FINAL_TM, d), lambda i: (i, 0)),
            pl.BlockSpec((FINAL_TM, d), lambda i: (i, 0)),
            pl.BlockSpec((FINAL_TM, PLE_DIM), lambda i: (i, 0)),
            pl.BlockSpec((d, D_SHARED), full),
            pl.BlockSpec((d, D_SHARED), full),
            pl.BlockSpec((D_SHARED, d), full),
            pl.BlockSpec((PLE_DIM, d), full),
            pl.BlockSpec((d, d), full),
            pl.BlockSpec((1, d), full),
            pl.BlockSpec((1, d), full),
        ],
        out_specs=pl.BlockSpec((FINAL_TM, d), lambda i: (i, 0)),
        out_shape=jax.ShapeDtypeStruct((N_TOK, d), F32),
        compiler_params=_cparams("parallel"),
        name="combine_ln2",
    )(w_t, yk.reshape(TOPK_EXPERTS + 1, N_TOK, PACKED_COLS), h1, h_bf, p2d, w_s_gate, w_s_up, w_s_down,
      w_ple, w_ple_gate, ln_g, ln_b)


def _rope_tables(positions):
    half = ROT_DIM // 2
    inv = ROPE_THETA ** (-jnp.arange(0, ROT_DIM, 2, dtype=F32) / ROT_DIM)
    per_row = LANES // half
    pos = jnp.repeat(positions.reshape(-1, per_row), half, axis=1).astype(F32)
    ang = pos * jnp.tile(inv, per_row)[None, :]
    cos, sin = lax.optimization_barrier((jnp.cos(ang), jnp.sin(ang)))
    cos = cos.reshape(positions.shape + (half,))
    sin = sin.reshape(positions.shape + (half,))
    rest = MOBA_DH - ROT_DIM
    ones = jnp.ones(cos.shape[:-1] + (rest,), F32)
    zeros = jnp.zeros(cos.shape[:-1] + (rest,), F32)
    return (jnp.concatenate([cos, cos, ones], axis=-1), jnp.concatenate([-sin, sin, zeros], axis=-1))


def _layer(h2d, p2d, cos_t, sin_t, w_cat, w_gk2, b_gk, norm_g, w_gla_o, w_moba_o, w_out, ln1_g, ln1_b,
           w_router, b_router, w_e_gate, w_e_up, w_e_down, w_s_gate, w_s_up, w_s_down, w_ple, w_ple_gate,
           ln2_g, ln2_b):
    w_gk2_pad = jnp.concatenate([w_gk2, jnp.zeros((LANES - GLA_RANK, GLA_KDIM), w_gk2.dtype)], axis=0)

    proj = _in_proj(h2d, w_cat)
    gla_out = _gla(proj, w_gk2_pad, b_gk[None, :], norm_g[None, :])
    moba_out = _moba(proj, cos_t, sin_t)
    h1, h_bf, h_packed = _mix(gla_out, moba_out, proj, h2d, w_gla_o.astype(BF16), w_moba_o.astype(BF16),
                              w_out.astype(BF16), ln1_g[None, :], ln1_b[None, :])
    e_t, w_t, rk_t, cnt = _route(h_bf, w_router.T.astype(BF16), b_router[:, None])
    dest_t, tile_expert, n_used, last_tile_row = _dest(cnt, e_t, rk_t)
    last_tile_row = last_tile_row.reshape(N_EXPERTS)
    xs = _dispatch(last_tile_row, dest_t, h_packed)
    row_dst = _row_destinations(dest_t.reshape(N_ASSIGN))
    yk = _experts(tile_expert.reshape(TILE_TABLE), n_used[0, 0:1], last_tile_row, row_dst, xs,
                  w_e_gate, w_e_up, w_e_down)
    return _final(w_t, yk, h1, h_bf, p2d, w_s_gate.astype(BF16), w_s_up.astype(BF16),
                  w_s_down.astype(BF16), w_ple.astype(BF16), w_ple_gate.astype(BF16),
                  ln2_g[None, :], ln2_b[None, :])


def kernel(x, p, positions, w_in, w_gla_gk2, b_gla_gk, gla_norm_g, w_gla_o, w_moba_o, w_out, ln1_g, ln1_b,
           w_router, b_router, w_e_gate, w_e_up, w_e_down, w_s_gate, w_s_up, w_s_down, w_ple, w_ple_gate,
           ln2_g, ln2_b):
    cos_t, sin_t = _rope_tables(positions)
    h = x.reshape(N_TOK, D_MODEL)
    for i in range(DEPTH):
        h = _layer(h, p[i].reshape(N_TOK, PLE_DIM), cos_t, sin_t, _repack_w_in(w_in, i), w_gla_gk2[i], b_gla_gk[i],
                   gla_norm_g[i], w_gla_o[i], w_moba_o[i], w_out[i], ln1_g[i], ln1_b[i], w_router[i],
                   b_router[i], w_e_gate[i], w_e_up[i], w_e_down[i], w_s_gate[i], w_s_up[i], w_s_down[i],
                   w_ple[i], w_ple_gate[i], ln2_g[i], ln2_b[i])
    return h.reshape(BATCH, SEQ, D_MODEL)
```

```python
import functools

import jax
import jax.numpy as jnp
from jax import lax
from jax.experimental import pallas as pl
from jax.experimental.pallas import tpu as pltpu
from jax.experimental.pallas import tpu_sc as plsc

F32 = jnp.float32
BF16 = jnp.bfloat16
I32 = jnp.int32
U32 = jnp.uint32

LANES = 128
SUBLANES = 8
VMEM_LIMIT_BYTES = 48 * 1024 * 1024

D_MODEL = 1024
BATCH = 8
SEQ = 2048
N_TOK = BATCH * SEQ
GLA_HEADS = 4
GLA_DK = 128
GLA_DV = 256
GLA_RANK = 16
GLA_NORMALIZER = 16.0
GLA_CHUNK = 64
GLA_GROUP = 4 * GLA_CHUNK
GLA_UNROLL = 4
MOBA_HEADS = 8
MOBA_DH = 128
MOBA_BLOCK = 256
MOBA_TOPK = 3
ROT_DIM = 32
ROPE_THETA = 500000.0
N_EXPERTS = 256
TOPK_EXPERTS = 8
N_GROUPS = 8
GROUP_SIZE = N_EXPERTS // N_GROUPS
TOPK_GROUPS = 4
D_EXPERT = 256
D_SHARED = 256
ROUTED_SCALE = 2.5
PLE_DIM = 256
LN_EPS = 1e-5
DEPTH = 1
ALPHA = (2.0 * DEPTH) ** 0.25
GLA_KDIM = GLA_HEADS * GLA_DK
GLA_VDIM = GLA_HEADS * GLA_DV
MOBA_DIM = MOBA_HEADS * MOBA_DH
N_KBLK = SEQ // MOBA_BLOCK

COL_GQ = 0
COL_GK = COL_GQ + GLA_KDIM
COL_GV = COL_GK + GLA_KDIM
COL_GR = COL_GV + GLA_VDIM
COL_MQ = COL_GR + GLA_VDIM
COL_MK = COL_MQ + MOBA_DIM
COL_MV = COL_MK + MOBA_DIM
COL_GA = COL_MV + MOBA_DIM
COL_GB = COL_GA + D_MODEL
COL_LOW = COL_GB + D_MODEL
PROJ_COLS = COL_LOW + LANES
LOW_SRC = 2 * GLA_KDIM + 2 * GLA_VDIM

ROW_TILE = 256
PACKED_COLS = D_MODEL // 2
PACKED_CHUNKS = PACKED_COLS // LANES
EXPERT_BUFS = 3
YK_ROWS = (TOPK_EXPERTS + 1) * N_TOK
SLOT_CHUNK = 4096
assert EXPERT_BUFS * ROW_TILE <= N_TOK
N_ASSIGN = N_TOK * TOPK_EXPERTS
N_ROW_TILES = (N_ASSIGN + N_EXPERTS * (ROW_TILE - 1) + ROW_TILE - 1) // ROW_TILE
N_ROWS = N_ROW_TILES * ROW_TILE
TILE_TABLE = -(-N_ROW_TILES // LANES) * LANES

PROJ_TM = 1024
REPACK_TN = 512
PROJ_TN = PROJ_COLS // 5
MIX_TM = 512
ROUTE_TM = 512
DEST_TM = 2048
DISP_TM = 256
FINAL_TM = 256

_NEG_INF = float("-inf")
LOG2_E = 1.4426950408889634


def _cparams(*sem):
    return pltpu.CompilerParams(dimension_semantics=sem, vmem_limit_bytes=VMEM_LIMIT_BYTES)


def _dot(a, b):
    return jnp.dot(a, b, preferred_element_type=F32)


def _dot_nt(a, b):
    return lax.dot_general(a, b, (((1,), (1,)), ((), ())), preferred_element_type=F32)


def _dot_tn(a, b):
    return lax.dot_general(a, b, (((0,), (0,)), ((), ())), preferred_element_type=F32)


def _split_bf16(x):
    hi = x.astype(BF16)
    lo = (x - hi.astype(F32)).astype(BF16)
    return hi, lo


def _sigmoid(x):
    return 1.0 / (1.0 + jnp.exp(-x))


def _pack_bf16_pair(x):
    lo = lax.bitcast_convert_type(x[:, :PACKED_COLS].astype(BF16).astype(F32), U32)
    hi = lax.bitcast_convert_type(x[:, PACKED_COLS:].astype(BF16).astype(F32), U32)
    return hi | lax.shift_right_logical(lo, jnp.uint32(16))


def _unpack_bf16_pair(w):
    lo = lax.bitcast_convert_type(lax.shift_left(w, jnp.uint32(16)), F32)
    hi = lax.bitcast_convert_type(w & jnp.uint32(0xFFFF0000), F32)
    return lo, hi


def _repack_body(a_ref, b_ref, o_ref):
    t = pl.program_id(0)
    a = a_ref[...].astype(BF16)

    @pl.when(t < LOW_SRC // REPACK_TN)
    def _():
        o_ref[...] = a

    @pl.when((t >= LOW_SRC // REPACK_TN) & (t < COL_LOW // REPACK_TN))
    def _():
        o_ref[...] = jnp.concatenate([a[:, GLA_RANK:], b_ref[:, :GLA_RANK].astype(BF16)], axis=1)

    @pl.when(t == COL_LOW // REPACK_TN)
    def _():
        o_ref[...] = jnp.concatenate([a[:, :GLA_RANK], jnp.zeros((D_MODEL, REPACK_TN - GLA_RANK), BF16)], axis=1)


def _repack_w_in(w_in, layer):
    assert LOW_SRC % REPACK_TN == 0 and COL_LOW % REPACK_TN == 0
    low_tile = LOW_SRC // REPACK_TN
    last_out = COL_LOW // REPACK_TN
    lanes_per_tile = REPACK_TN // LANES
    return pl.pallas_call(
        _repack_body,
        grid=(last_out + 1,),
        in_specs=[
            pl.BlockSpec((None, D_MODEL, REPACK_TN), lambda t: (layer, 0, jnp.where(t == last_out, low_tile, t))),
            pl.BlockSpec((None, D_MODEL, LANES), lambda t: (layer, 0, jnp.minimum(t + 1, last_out) * lanes_per_tile)),
        ],
        out_specs=pl.BlockSpec((D_MODEL, REPACK_TN), lambda t: (0, t)),
        out_shape=jax.ShapeDtypeStruct((D_MODEL, PROJ_COLS), BF16),
        compiler_params=_cparams("parallel"),
        name="repack_w_in",
    )(w_in, w_in)


def _proj_body(x_ref, w_ref, o_ref):
    o_ref[...] = _dot(x_ref[...].astype(BF16), w_ref[...]).astype(BF16)


def _in_proj(x2d, w_cat):
    return pl.pallas_call(
        _proj_body,
        grid=(PROJ_COLS // PROJ_TN, N_TOK // PROJ_TM),
        in_specs=[
            pl.BlockSpec((PROJ_TM, D_MODEL), lambda j, i: (i, 0)),
            pl.BlockSpec((D_MODEL, PROJ_TN), lambda j, i: (0, j)),
        ],
        out_specs=pl.BlockSpec((PROJ_TM, PROJ_TN), lambda j, i: (i, j)),
        out_shape=jax.ShapeDtypeStruct((N_TOK, PROJ_COLS), BF16),
        compiler_params=_cparams("parallel", "parallel"),
        name="in_proj",
    )(x2d, w_cat)


def _gla_body(q_ref, k_ref, v_ref, r_ref, low_ref, wg_ref, bg_ref, ng_ref, o_ref, st_ref, gk_ref):
    w_hi, w_lo = _split_bf16(wg_ref[...])
    low = low_ref[...]
    lin = _dot(low, w_hi) + _dot(low, w_lo) + bg_ref[...]
    gk_ref[...] = (jnp.minimum(lin, 0.0) - jnp.log1p(jnp.exp(-jnp.abs(lin)))) * (1.0 / GLA_NORMALIZER)
    st_ref[...] = jnp.zeros_like(st_ref)

    ri = lax.broadcasted_iota(I32, (GLA_GROUP, GLA_GROUP), 0)
    ci = lax.broadcasted_iota(I32, (GLA_GROUP, GLA_GROUP), 1)
    same_chunk = lax.shift_right_logical(ri, GLA_CHUNK.bit_length() - 1) == lax.shift_right_logical(
        ci, GLA_CHUNK.bit_length() - 1)
    causal = same_chunk & (ri >= ci)
    sums = jnp.concatenate([jnp.where(causal, 1.0, 0.0), jnp.where(same_chunk, 1.0, 0.0)], axis=0).astype(BF16)
    gain = ng_ref[...]

    def group(c, carry):
        rows = pl.ds(pl.multiple_of(c * GLA_GROUP, GLA_GROUP), GLA_GROUP)
        g_hi, g_lo = _split_bf16(gk_ref[rows, :])
        bb = _dot(sums, g_hi) + _dot(sums, g_lo)
        b = bb[0:GLA_GROUP]
        b_end = bb[GLA_GROUP:2 * GLA_GROUP]
        q = q_ref[rows, :].astype(F32) * (GLA_DK ** -0.5)
        k = k_ref[rows, :].astype(F32)
        v = v_ref[rows, :]
        q_e = (q * jnp.exp(b)).astype(BF16)
        k_e = (k * jnp.exp(-b)).astype(BF16)
        k_d = (k * jnp.exp(b_end - b)).astype(BF16)
        att = jnp.where(causal, _dot_nt(q_e, k_e), 0.0)
        o = _dot(att.astype(BF16), v)
        st = st_ref[...]
        inter = []
        for j in range(GLA_GROUP // GLA_CHUNK):
            cr = slice(j * GLA_CHUNK, (j + 1) * GLA_CHUNK)
            inter.append(_dot_nt(q_e[cr], st.astype(BF16)))
            st = st * jnp.exp(b_end[j * GLA_CHUNK:j * GLA_CHUNK + 1, :]) + _dot_tn(v[cr], k_d[cr])
        st_ref[...] = st
        o = o + jnp.concatenate(inter, axis=0)
        o = o * lax.rsqrt(jnp.mean(o * o, axis=-1, keepdims=True) + LN_EPS) * gain
        r = r_ref[rows, :].astype(F32)
        o_ref[rows, :] = (o * (r * _sigmoid(r))).astype(BF16)
        return carry

    lax.fori_loop(0, SEQ // GLA_GROUP, group, 0, unroll=GLA_UNROLL)


def _gla(proj, w_gk2_pad, b_gk, norm_g):
    kb, vb = GLA_DK, GLA_DV
    return pl.pallas_call(
        _gla_body,
        grid=(BATCH, GLA_HEADS),
        in_specs=[
            pl.BlockSpec((SEQ, kb), lambda b, h: (b, COL_GQ // kb + h)),
            pl.BlockSpec((SEQ, kb), lambda b, h: (b, COL_GK // kb + h)),
            pl.BlockSpec((SEQ, vb), lambda b, h: (b, COL_GV // vb + h)),
            pl.BlockSpec((SEQ, vb), lambda b, h: (b, COL_GR // vb + h)),
            pl.BlockSpec((SEQ, LANES), lambda b, h: (b, COL_LOW // LANES)),
            pl.BlockSpec((LANES, kb), lambda b, h: (0, h)),
            pl.BlockSpec((1, kb), lambda b, h: (0, h)),
            pl.BlockSpec((1, vb), lambda b, h: (0, 0)),
        ],
        out_specs=pl.BlockSpec((SEQ, vb), lambda b, h: (b, h)),
        out_shape=jax.ShapeDtypeStruct((N_TOK, GLA_VDIM), BF16),
        scratch_shapes=[pltpu.VMEM((vb, kb), F32), pltpu.VMEM((SEQ, kb), F32)],
        compiler_params=_cparams("parallel", "parallel"),
        name="gla",
    )(proj, proj, proj, proj, proj, w_gk2_pad, b_gk, norm_g)


def _moba_body(q_ref, k_ref, v_ref, c_ref, s_ref, o_ref, qs_ref, ks_ref, vt_ref, bias_ref,
               sc_a_ref, sc_b_ref, pr_a_ref, pr_b_ref):
    cos_t = c_ref[0]
    sin_t = s_ref[0]
    lane = lax.broadcasted_iota(I32, (SEQ, MOBA_DH), 1)
    half = ROT_DIM // 2

    def rope(x):
        partner = jnp.where(lane < half, pltpu.roll(x, MOBA_DH - half, 1), pltpu.roll(x, half, 1))
        return x * cos_t + partner * sin_t

    q = rope(q_ref[...].astype(F32))
    k = rope(k_ref[...].astype(F32))
    q_hi, q_lo = _split_bf16(q)
    qs_ref[...] = (q * (MOBA_DH ** -0.5 * LOG2_E)).astype(BF16)
    ks_ref[...] = k.astype(BF16)
    vt_ref[...] = v_ref[...].astype(F32).T.astype(BF16)

    k_mean = jnp.concatenate(
        [jnp.mean(k[j * MOBA_BLOCK:(j + 1) * MOBA_BLOCK], axis=0, keepdims=True) for j in range(N_KBLK)], axis=0)
    m_hi, m_lo = _split_bf16(k_mean)
    s_blk = _dot_nt(m_hi, q_hi) + _dot_nt(m_hi, q_lo) + _dot_nt(m_lo, q_hi)
    blk = lax.broadcasted_iota(I32, (N_KBLK, SEQ), 0)
    q_blk = lax.shift_right_logical(lax.broadcasted_iota(I32, (N_KBLK, SEQ), 1), MOBA_BLOCK.bit_length() - 1)
    past = blk < q_blk
    s_blk = jnp.where(past, s_blk, _NEG_INF)
    beaten = jnp.zeros((N_KBLK, SEQ), I32)
    for j in range(N_KBLK):
        row = s_blk[j:j + 1, :]
        beaten += jnp.where((row > s_blk) | ((row == s_blk) & (j < blk)), 1, 0)
    bias_ref[...] = jnp.where(past & (beaten < MOBA_TOPK), 0.0, _NEG_INF)

    kr = lax.broadcasted_iota(I32, (MOBA_BLOCK, MOBA_BLOCK), 0)
    qc = lax.broadcasted_iota(I32, (MOBA_BLOCK, MOBA_BLOCK), 1)
    own_bias = jnp.where(kr <= qc, 0.0, _NEG_INF)

    sc_bufs = (sc_a_ref, sc_b_ref)
    pr_bufs = (pr_a_ref, pr_b_ref)
    zeros_row = jnp.zeros((1, MOBA_BLOCK), F32)
    future_row = jnp.full((1, MOBA_BLOCK), _NEG_INF, F32)
    for pair in range(N_KBLK // 2):
        q_blocks = (2 * pair, 2 * pair + 1)
        cols = slice(q_blocks[0] * MOBA_BLOCK, (q_blocks[1] + 1) * MOBA_BLOCK)
        n_kblk = q_blocks[1] + 1
        n_keys = n_kblk * MOBA_BLOCK
        sc, pr = sc_bufs[pair % 2], pr_bufs[pair % 2]
        q_pair = qs_ref[cols, :]

        def query_bias(j):
            halves = []
            for qb in q_blocks:
                q_cols = slice(qb * MOBA_BLOCK, (qb + 1) * MOBA_BLOCK)
                halves.append(bias_ref[j:j + 1, q_cols] if j < qb else zeros_row if j == qb else future_row)
            return jnp.concatenate(halves, axis=1)

        biases = [query_bias(j) for j in range(n_kblk)]
        col_max = []
        for j in range(n_kblk):
            rows = slice(j * MOBA_BLOCK, (j + 1) * MOBA_BLOCK)
            s = _dot_nt(ks_ref[rows, :], q_pair)
            if j in q_blocks:
                h = q_blocks.index(j)
                own = s[:, h * MOBA_BLOCK:(h + 1) * MOBA_BLOCK] + own_bias
                s = jnp.concatenate([own, s[:, MOBA_BLOCK:]] if h == 0 else [s[:, :MOBA_BLOCK], own], axis=1)
            sc[rows, :] = s
            col_max.append(jnp.max(s, axis=0, keepdims=True) + biases[j])
        m = functools.reduce(jnp.maximum, col_max)
        denom = jnp.zeros((1, 2 * MOBA_BLOCK), F32)
        for j in range(n_kblk):
            rows = slice(j * MOBA_BLOCK, (j + 1) * MOBA_BLOCK)
            p = jnp.exp2(sc[rows, :] - (m - biases[j]))
            denom = denom + jnp.sum(p, axis=0, keepdims=True)
            pr[rows, :] = p.astype(BF16)
        o_t = _dot(vt_ref[:, 0:n_keys], pr[0:n_keys, :]) * (1.0 / denom)
        o_ref[cols, :] = o_t.T.astype(BF16)


def _moba(proj, cos_t, sin_t):
    dh = MOBA_DH
    return pl.pallas_call(
        _moba_body,
        grid=(BATCH, MOBA_HEADS),
        in_specs=[
            pl.BlockSpec((SEQ, dh), lambda b, h: (b, COL_MQ // dh + h)),
            pl.BlockSpec((SEQ, dh), lambda b, h: (b, COL_MK // dh + h)),
            pl.BlockSpec((SEQ, dh), lambda b, h: (b, COL_MV // dh + h)),
            pl.BlockSpec((1, SEQ, dh), lambda b, h: (b, 0, 0)),
            pl.BlockSpec((1, SEQ, dh), lambda b, h: (b, 0, 0)),
        ],
        out_specs=pl.BlockSpec((SEQ, dh), lambda b, h: (b, h)),
        out_shape=jax.ShapeDtypeStruct((N_TOK, MOBA_DIM), BF16),
        scratch_shapes=[
            pltpu.VMEM((SEQ, dh), BF16),
            pltpu.VMEM((SEQ, dh), BF16),
            pltpu.VMEM((dh, SEQ), BF16),
            pltpu.VMEM((N_KBLK, SEQ), F32),
            pltpu.VMEM((SEQ, 2 * MOBA_BLOCK), F32),
            pltpu.VMEM((SEQ, 2 * MOBA_BLOCK), F32),
            pltpu.VMEM((SEQ, 2 * MOBA_BLOCK), BF16),
            pltpu.VMEM((SEQ, 2 * MOBA_BLOCK), BF16),
        ],
        compiler_params=_cparams("parallel", "parallel"),
        name="moba",
    )(proj, proj, proj, cos_t, sin_t)


def _layer_norm(z, g, b):
    mu = jnp.mean(z, axis=-1, keepdims=True)
    zc = z - mu
    var = jnp.mean(zc * zc, axis=-1, keepdims=True)
    return zc * lax.rsqrt(var + LN_EPS) * g + b


def _mix_body(gla_ref, moba_ref, ga_ref, gb_ref, x_ref, wgo_ref, wmo_ref, wo_ref, g_ref, b_ref,
              h_ref, hb_ref, hp_ref):
    y_gla = _dot(gla_ref[...], wgo_ref[...])
    y_moba = _dot(moba_ref[...], wmo_ref[...])
    merged = _sigmoid(ga_ref[...].astype(F32)) * y_gla + _sigmoid(gb_ref[...].astype(F32)) * y_moba
    mix = _dot(merged.astype(BF16), wo_ref[...])
    h = _layer_norm(ALPHA * x_ref[...] + mix, g_ref[...], b_ref[...])
    h_ref[...] = h
    hb_ref[...] = h.astype(BF16)
    packed = _pack_bf16_pair(h)
    for c in range(PACKED_CHUNKS):
        hp_ref[:, c, :] = packed[:, c * LANES:(c + 1) * LANES]


def _mix(gla_out, moba_out, proj, x2d, w_gla_o, w_moba_o, w_out, ln_g, ln_b):
    d = D_MODEL
    row = lambda i: (i, 0)
    full = lambda i: (0, 0)
    return pl.pallas_call(
        _mix_body,
        grid=(N_TOK // MIX_TM,),
        in_specs=[
            pl.BlockSpec((MIX_TM, d), row),
            pl.BlockSpec((MIX_TM, d), row),
            pl.BlockSpec((MIX_TM, d), lambda i: (i, COL_GA // d)),
            pl.BlockSpec((MIX_TM, d), lambda i: (i, COL_GB // d)),
            pl.BlockSpec((MIX_TM, d), row),
            pl.BlockSpec((d, d), full),
            pl.BlockSpec((d, d), full),
            pl.BlockSpec((d, d), full),
            pl.BlockSpec((1, d), full),
            pl.BlockSpec((1, d), full),
        ],
        out_specs=[
            pl.BlockSpec((MIX_TM, d), row),
            pl.BlockSpec((MIX_TM, d), row),
            pl.BlockSpec((MIX_TM, PACKED_CHUNKS, LANES), lambda i: (i, 0, 0)),
        ],
        out_shape=[
            jax.ShapeDtypeStruct((N_TOK, d), F32),
            jax.ShapeDtypeStruct((N_TOK, d), BF16),
            jax.ShapeDtypeStruct((N_TOK, PACKED_CHUNKS, LANES), U32),
        ],
        compiler_params=_cparams("parallel"),
        name="mix_ln1",
    )(gla_out, moba_out, proj, proj, x2d, w_gla_o, w_moba_o, w_out, ln_g, ln_b)


def _route_body(h_ref, wr_ref, br_ref, e_ref, w_ref, rk_ref, cnt_ref, carry_ref):
    tm = ROUTE_TM

    @pl.when(pl.program_id(0) == 0)
    def _():
        carry_ref[...] = jnp.zeros_like(carry_ref)

    scores = _sigmoid(_dot_nt(wr_ref[...], h_ref[...]))
    biased = scores + br_ref[...]
    row = lax.broadcasted_iota(I32, (N_EXPERTS, tm), 0).astype(F32)
    row_g = lax.broadcasted_iota(I32, (GROUP_SIZE, tm), 0).astype(F32)

    g_scores = []
    for g in range(N_GROUPS):
        grp = biased[g * GROUP_SIZE:(g + 1) * GROUP_SIZE]
        m1 = jnp.max(grp, axis=0, keepdims=True)
        first = jnp.min(jnp.where(grp == m1, row_g, float(GROUP_SIZE)), axis=0, keepdims=True)
        m2 = jnp.max(jnp.where(row_g == first, _NEG_INF, grp), axis=0, keepdims=True)
        g_scores.append(m1 + m2)
    g_score = jnp.concatenate(g_scores, axis=0)
    g_row = lax.broadcasted_iota(I32, (N_GROUPS, tm), 0)
    g_beaten = jnp.zeros((N_GROUPS, tm), I32)
    for g in range(N_GROUPS):
        r = g_score[g:g + 1, :]
        g_beaten += jnp.where((r > g_score) | ((r == g_score) & (g < g_row)), 1, 0)
    g_keep = g_beaten < TOPK_GROUPS
    masked = jnp.concatenate(
        [jnp.where(g_keep[g:g + 1, :], biased[g * GROUP_SIZE:(g + 1) * GROUP_SIZE], _NEG_INF)
         for g in range(N_GROUPS)], axis=0)

    onehot = jnp.zeros((N_EXPERTS, tm), F32)
    picks, pick_scores = [], []
    for _ in range(TOPK_EXPERTS):
        m = jnp.max(masked, axis=0, keepdims=True)
        idx = jnp.min(jnp.where(masked == m, row, float(N_EXPERTS)), axis=0, keepdims=True)
        hit = row == idx
        picks.append(idx)
        pick_scores.append(jnp.sum(jnp.where(hit, scores, 0.0), axis=0, keepdims=True))
        onehot = onehot + jnp.where(hit, 1.0, 0.0)
        masked = jnp.where(hit, _NEG_INF, masked)
    sel = jnp.concatenate(pick_scores, axis=0)
    e_ref[...] = jnp.concatenate(picks, axis=0).astype(I32)
    w_ref[...] = sel / jnp.sum(sel, axis=0, keepdims=True) * ROUTED_SCALE

    t_r = lax.broadcasted_iota(I32, (tm, tm), 0)
    t_c = lax.broadcasted_iota(I32, (tm, tm), 1)
    earlier = jnp.where(t_r < t_c, 1.0, 0.0).astype(BF16)
    seen = _dot(onehot.astype(BF16), earlier) + carry_ref[...]
    rk_ref[...] = jnp.concatenate(
        [jnp.sum(jnp.where(row == idx, seen, 0.0), axis=0, keepdims=True) for idx in picks], axis=0).astype(I32)
    carry_ref[...] += jnp.sum(onehot, axis=1, keepdims=True)
    cnt_ref[...] = carry_ref[...]


def _route(h_bf, w_router_t, b_router_col):
    k = TOPK_EXPERTS
    tok = lambda i: (0, i)
    return pl.pallas_call(
        _route_body,
        grid=(N_TOK // ROUTE_TM,),
        in_specs=[
            pl.BlockSpec((ROUTE_TM, D_MODEL), lambda i: (i, 0)),
            pl.BlockSpec((N_EXPERTS, D_MODEL), lambda i: (0, 0)),
            pl.BlockSpec((N_EXPERTS, 1), lambda i: (0, 0)),
        ],
        out_specs=[
            pl.BlockSpec((k, ROUTE_TM), tok),
            pl.BlockSpec((k, ROUTE_TM), tok),
            pl.BlockSpec((k, ROUTE_TM), tok),
            pl.BlockSpec((N_EXPERTS, 1), lambda i: (0, 0)),
        ],
        out_shape=[
            jax.ShapeDtypeStruct((k, N_TOK), I32),
            jax.ShapeDtypeStruct((k, N_TOK), F32),
            jax.ShapeDtypeStruct((k, N_TOK), I32),
            jax.ShapeDtypeStruct((N_EXPERTS, 1), F32),
        ],
        scratch_shapes=[pltpu.VMEM((N_EXPERTS, 1), F32)],
        compiler_params=_cparams("arbitrary"),
        name="route",
    )(h_bf, w_router_t, b_router_col)


def _dest_body(cnt_ref, e_ref, rk_ref, d_ref, te_ref, nu_ref, lt_ref):
    cnt = cnt_ref[...]
    tiles = jnp.floor((cnt + (ROW_TILE - 1)) * (1.0 / ROW_TILE))
    er = lax.broadcasted_iota(I32, (N_EXPERTS, N_EXPERTS), 0)
    ec = lax.broadcasted_iota(I32, (N_EXPERTS, N_EXPERTS), 1)
    before = jnp.where(ec < er, 1.0, 0.0).astype(BF16)
    tiles_b = jnp.broadcast_to(tiles, (N_EXPERTS, LANES)).astype(BF16)
    t_start = _dot(before, tiles_b)[:, 0:1]
    t_end = t_start + tiles
    p_start = t_start * float(ROW_TILE)
    lt_ref[...] = jnp.where(tiles > 0.0, (t_end - 1.0) * float(ROW_TILE), -1.0).astype(I32)

    row = lax.broadcasted_iota(I32, (N_EXPERTS, DEST_TM), 0)
    d_ref[...] = jnp.concatenate(
        [jnp.sum(jnp.where(row == e_ref[k:k + 1, :], p_start, 0.0), axis=0, keepdims=True)
         for k in range(TOPK_EXPERTS)], axis=0).astype(I32) + rk_ref[...]

    tile_id = lax.broadcasted_iota(I32, (N_EXPERTS, TILE_TABLE), 1).astype(F32)
    owner = jnp.sum(jnp.where(t_end <= tile_id, 1, 0), axis=0, keepdims=True)
    te_ref[...] = jnp.minimum(owner, N_EXPERTS - 1)
    nu_ref[...] = jnp.broadcast_to(t_end[N_EXPERTS - 1:N_EXPERTS, :], (1, LANES)).astype(I32)


def _dest(cnt, e_t, rk_t):
    k = TOPK_EXPERTS
    tok = lambda i: (0, i)
    return pl.pallas_call(
        _dest_body,
        grid=(N_TOK // DEST_TM,),
        in_specs=[
            pl.BlockSpec((N_EXPERTS, 1), lambda i: (0, 0)),
            pl.BlockSpec((k, DEST_TM), tok),
            pl.BlockSpec((k, DEST_TM), tok),
        ],
        out_specs=[
            pl.BlockSpec((k, DEST_TM), tok),
            pl.BlockSpec((1, TILE_TABLE), lambda i: (0, 0)),
            pl.BlockSpec((1, LANES), lambda i: (0, 0)),
            pl.BlockSpec((N_EXPERTS, 1), lambda i: (0, 0)),
        ],
        out_shape=[
            jax.ShapeDtypeStruct((k, N_TOK), I32),
            jax.ShapeDtypeStruct((1, TILE_TABLE), I32),
            jax.ShapeDtypeStruct((1, LANES), I32),
            jax.ShapeDtypeStruct((N_EXPERTS, 1), I32),
        ],
        compiler_params=_cparams("arbitrary"),
        name="dest",
    )(cnt, e_t, rk_t)


def _row_copy(src, dst, sem):
    return pltpu.make_async_copy(src, dst, sem)


def _dispatch_body(lt_ref, d_ref, h_ref, xs_ref, zero_ref, sem):
    @pl.when(pl.program_id(0) == 0)
    def _():
        zero_ref[...] = jnp.zeros_like(zero_ref)

        def tile_copy(e):
            r0 = pl.multiple_of(jnp.maximum(lt_ref[e], 0), ROW_TILE)
            return pltpu.make_async_copy(zero_ref, xs_ref.at[pl.ds(r0, ROW_TILE)], sem)

        def z_start(e, carry):
            @pl.when(lt_ref[e] >= 0)
            def _():
                tile_copy(e).start()
            return carry

        def z_wait(e, carry):
            @pl.when(lt_ref[e] >= 0)
            def _():
                tile_copy(e).wait()
            return carry

        lax.fori_loop(0, N_EXPERTS, z_start, 0)
        lax.fori_loop(0, N_EXPERTS, z_wait, 0)

    for t in range(DISP_TM):
        for k in range(TOPK_EXPERTS):
            _row_copy(h_ref.at[t], xs_ref.at[d_ref[k, t]], sem).start(priority=k % 2)
    for k in range(TOPK_EXPERTS):
        pltpu.make_async_copy(h_ref, xs_ref.at[pl.ds(0, DISP_TM)], sem).wait()


def _dispatch(last_tile_row, dest_t, h1):
    grid_spec = pltpu.PrefetchScalarGridSpec(
        num_scalar_prefetch=1,
        grid=(N_TOK // DISP_TM,),
        in_specs=[
            pl.BlockSpec((TOPK_EXPERTS, DISP_TM), lambda i, lt: (0, i), memory_space=pltpu.SMEM),
            pl.BlockSpec((DISP_TM, PACKED_CHUNKS, LANES), lambda i, lt: (i, 0, 0)),
        ],
        out_specs=pl.BlockSpec(memory_space=pl.ANY),
        scratch_shapes=[pltpu.VMEM((ROW_TILE, PACKED_CHUNKS, LANES), U32), pltpu.SemaphoreType.DMA],
    )
    return pl.pallas_call(
        _dispatch_body,
        grid_spec=grid_spec,
        out_shape=jax.ShapeDtypeStruct((N_ROWS, PACKED_CHUNKS, LANES), U32),
        compiler_params=_cparams("arbitrary"),
        name="dispatch",
    )(last_tile_row, dest_t, h1)


def _row_destinations(dest_flat):
    info = plsc.get_sparse_core_info()
    n_cores, lanes = info.num_cores, info.num_lanes
    n_workers = n_cores * info.num_subcores
    rows_per_worker = -(-N_ROWS // (n_workers * ROW_TILE)) * ROW_TILE
    assert rows_per_worker % lanes == 0
    mesh = plsc.VectorSubcoreMesh(core_axis_name="c", subcore_axis_name="s")

    @functools.partial(
        pl.kernel, mesh=mesh, out_type=jax.ShapeDtypeStruct((n_workers * rows_per_worker,), I32),
        scratch_types=[pltpu.VMEM((SLOT_CHUNK,), I32), pltpu.VMEM((rows_per_worker,), I32)],
        compiler_params=pltpu.CompilerParams(needs_layout_passes=False), name="row_destinations")
    def invert(dest_hbm, out_hbm, dest_v, table_v):
        first_row = (lax.axis_index("s") * n_cores + lax.axis_index("c")) * rows_per_worker
        lane = lax.iota(I32, lanes)
        tile_shift = ROW_TILE.bit_length() - 1

        def spare(i, carry):
            row = first_row + i * lanes + lane
            buf = lax.rem(lax.shift_right_logical(row, tile_shift), EXPERT_BUFS)
            table_v[pl.ds(i * lanes, lanes)] = TOPK_EXPERTS * N_TOK + buf * ROW_TILE + (row & (ROW_TILE - 1))
            return carry

        lax.fori_loop(0, rows_per_worker // lanes, spare, 0)

        def chunk(c, carry):
            pltpu.sync_copy(dest_hbm.at[pl.ds(c * SLOT_CHUNK, SLOT_CHUNK)], dest_v)

            def vec(i, inner):
                local = dest_v[pl.ds(i * lanes, lanes)] - first_row
                mine = (local >= 0) & (local < rows_per_worker)
                plsc.store_scatter(table_v, [jnp.where(mine, local, 0)], lane + (c * SLOT_CHUNK + i * lanes),
                                   mask=mine)
                return inner

            lax.fori_loop(0, SLOT_CHUNK // lanes, vec, 0)
            return carry

        lax.fori_loop(0, N_ASSIGN // SLOT_CHUNK, chunk, 0)
        pltpu.sync_copy(table_v, out_hbm.at[pl.ds(first_row, rows_per_worker)])

    return invert(dest_flat)


def _experts_body(te_ref, nu_ref, lt_ref, slot_ref, xs_ref, wg_ref, wu_ref, wd_ref, yk_ref,
                  wg_f, wu_f, wd_f, wg_b, wu_b, wd_b, y_buf, n_loaded, sem, w_sem):
    i = pl.program_id(0)
    n_used = nu_ref[0]

    def weight_copies(e, s):
        return (pltpu.make_async_copy(wg_ref.at[e], wg_f.at[s], w_sem.at[s]),
                pltpu.make_async_copy(wu_ref.at[e], wu_f.at[s], w_sem.at[s]),
                pltpu.make_async_copy(wd_ref.at[e], wd_f.at[s], w_sem.at[s]))

    def wait_tile(b):
        pltpu.make_async_copy(y_buf.at[b], yk_ref.at[pl.ds(0, ROW_TILE), :], sem.at[b]).wait()

    def send_tile(b):
        for r in range(ROW_TILE):
            dst = slot_ref[0, 0, r]
            _row_copy(y_buf.at[b, pl.ds(r, 1), :], yk_ref.at[pl.ds(dst, 1), :], sem.at[b]).start(priority=r % 2)

    @pl.when(i == 0)
    def _():
        y_buf[EXPERT_BUFS - 1] = jnp.zeros((ROW_TILE, PACKED_COLS), U32)
        n_loaded[0] = 0
        for c in weight_copies(te_ref[0], 0):
            c.start()

    @pl.when((i >= 2) & (i <= n_used))
    def _():
        wait_tile(lax.rem(i, EXPERT_BUFS))

    @pl.when(i < n_used)
    def _():
        e = te_ref[i]
        prev = te_ref[jnp.maximum(i - 1, 0)]

        @pl.when((i == 0) | (e != prev))
        def _():
            s = lax.rem(n_loaded[0], 2)
            for c in weight_copies(e, s):
                c.wait()
            wg_b[...] = wg_f[s].astype(BF16)
            wu_b[...] = wu_f[s].astype(BF16)
            wd_b[...] = wd_f[s].astype(BF16)
            n_loaded[0] = n_loaded[0] + 1
            nxt = lax.shift_right_logical(lt_ref[e], ROW_TILE.bit_length() - 1) + 1

            @pl.when(nxt < n_used)
            def _():
                for c in weight_copies(te_ref[nxt], 1 - s):
                    c.start()

        for phase in range(EXPERT_BUFS):
            @pl.when(lax.rem(i, EXPERT_BUFS) == phase)
            def _(phase=phase):
                send_tile((phase + EXPERT_BUFS - 1) % EXPERT_BUFS)
                packed = jnp.concatenate([xs_ref[:, c, :] for c in range(PACKED_CHUNKS)], axis=1)
                x_lo, x_hi = _unpack_bf16_pair(packed)
                x = jnp.concatenate([x_lo.astype(BF16), x_hi.astype(BF16)], axis=1)
                g = _dot(x, wg_b[...])
                u = _dot(x, wu_b[...])
                h = (g * _sigmoid(g)) * u
                y_buf[phase] = _pack_bf16_pair(_dot(h.astype(BF16), wd_b[...]))

    @pl.when(i == n_used)
    def _():
        send_tile(lax.rem(i + EXPERT_BUFS - 1, EXPERT_BUFS))
        wait_tile(lax.rem(i + EXPERT_BUFS - 2, EXPERT_BUFS))
        wait_tile(lax.rem(i + EXPERT_BUFS - 1, EXPERT_BUFS))


def _experts(tile_expert, n_used, last_tile_row, row_dst, xs, w_gate, w_up, w_down):
    def tile(i, te, nu, lt):
        return (jnp.minimum(i, nu[0] - 1), 0, 0)

    n_table_tiles = row_dst.shape[0] // ROW_TILE
    placeholder = n_table_tiles - 1
    assert placeholder * ROW_TILE >= N_ROWS and placeholder % EXPERT_BUFS == EXPERT_BUFS - 1

    def prev_tile(i, te, nu, lt):
        return (jnp.where(i == 0, placeholder, jnp.minimum(i, nu[0]) - 1), 0, 0)

    grid_spec = pltpu.PrefetchScalarGridSpec(
        num_scalar_prefetch=3,
        grid=(N_ROW_TILES + 1,),
        in_specs=[
            pl.BlockSpec((1, 1, ROW_TILE), prev_tile, memory_space=pltpu.SMEM),
            pl.BlockSpec((ROW_TILE, PACKED_CHUNKS, LANES), tile),
            pl.BlockSpec(memory_space=pl.ANY),
            pl.BlockSpec(memory_space=pl.ANY),
            pl.BlockSpec(memory_space=pl.ANY),
        ],
        out_specs=pl.BlockSpec(memory_space=pl.ANY),
        scratch_shapes=[
            pltpu.VMEM((2, D_MODEL, D_EXPERT), F32),
            pltpu.VMEM((2, D_MODEL, D_EXPERT), F32),
            pltpu.VMEM((2, D_EXPERT, D_MODEL), F32),
            pltpu.VMEM((D_MODEL, D_EXPERT), BF16),
            pltpu.VMEM((D_MODEL, D_EXPERT), BF16),
            pltpu.VMEM((D_EXPERT, D_MODEL), BF16),
            pltpu.VMEM((EXPERT_BUFS, ROW_TILE, PACKED_COLS), U32),
            pltpu.SMEM((1,), I32),
            pltpu.SemaphoreType.DMA((EXPERT_BUFS,)),
            pltpu.SemaphoreType.DMA((2,)),
        ],
    )
    return pl.pallas_call(
        _experts_body,
        grid_spec=grid_spec,
        out_shape=jax.ShapeDtypeStruct((YK_ROWS, PACKED_COLS), U32),
        compiler_params=_cparams("arbitrary"),
        name="experts",
    )(tile_expert, n_used, last_tile_row, row_dst.reshape(n_table_tiles, 1, ROW_TILE), xs, w_gate, w_up, w_down)


def _final_body(w_ref, yk_ref, h_ref, hb_ref, p_ref, wsg_ref, wsu_ref, wsd_ref, wpl_ref, wpg_ref,
                g_ref, b_ref, o_ref):
    hb = hb_ref[...]
    sg = _dot(hb, wsg_ref[...])
    shared = _dot(((sg * _sigmoid(sg)) * _dot(hb, wsu_ref[...])).astype(BF16), wsd_ref[...])
    ple = _sigmoid(_dot(hb, wpg_ref[...])) * _dot(p_ref[...].astype(BF16), wpl_ref[...])

    w_col = w_ref[...].T
    y_lo, y_hi = _unpack_bf16_pair(yk_ref[0])
    r_lo, r_hi = y_lo * w_col[:, 0:1], y_hi * w_col[:, 0:1]
    for k in range(1, TOPK_EXPERTS):
        y_lo, y_hi = _unpack_bf16_pair(yk_ref[k])
        r_lo, r_hi = r_lo + y_lo * w_col[:, k:k + 1], r_hi + y_hi * w_col[:, k:k + 1]
    routed = jnp.concatenate([r_lo, r_hi], axis=1)
    o_ref[...] = _layer_norm(ALPHA * h_ref[...] + (routed + shared) + ple, g_ref[...], b_ref[...])


def _final(w_t, yk, h1, h_bf, p2d, w_s_gate, w_s_up, w_s_down, w_ple, w_ple_gate, ln_g, ln_b):
    d = D_MODEL
    full = lambda i: (0, 0)
    return pl.pallas_call(
        _final_body,
        grid=(N_TOK // FINAL_TM,),
        in_specs=[
            pl.BlockSpec((TOPK_EXPERTS, FINAL_TM), lambda i: (0, i)),
            pl.BlockSpec((TOPK_EXPERTS, FINAL_TM, PACKED_COLS), lambda i: (0, i, 0)),
            pl.BlockSpec((FINAL_TM, d), lambda i: (i, 0)),
            pl.BlockSpec((FINAL_TM, d), lambda i: (i, 0)),
            pl.BlockSpec((FINAL_TM, PLE_DIM), lambda i: (i, 0)),
            pl.BlockSpec((d, D_SHARED), full),
            pl.BlockSpec((d, D_SHARED), full),
            pl.BlockSpec((D_SHARED, d), full),
            pl.BlockSpec((PLE_DIM, d), full),
            pl.BlockSpec((d, d), full),
            pl.BlockSpec((1, d), full),
            pl.BlockSpec((1, d), full),
        ],
        out_specs=pl.BlockSpec((FINAL_TM, d), lambda i: (i, 0)),
        out_shape=jax.ShapeDtypeStruct((N_TOK, d), F32),
        compiler_params=_cparams("parallel"),
        name="combine_ln2",
    )(w_t, yk.reshape(TOPK_EXPERTS + 1, N_TOK, PACKED_COLS), h1, h_bf, p2d, w_s_gate, w_s_up, w_s_down,
      w_ple, w_ple_gate, ln_g, ln_b)


def _rope_tables(positions):
    half = ROT_DIM // 2
    inv = ROPE_THETA ** (-jnp.arange(0, ROT_DIM, 2, dtype=F32) / ROT_DIM)
    per_row = LANES // half
    pos = jnp.repeat(positions.reshape(-1, per_row), half, axis=1).astype(F32)
    ang = pos * jnp.tile(inv, per_row)[None, :]
    cos, sin = lax.optimization_barrier((jnp.cos(ang), jnp.sin(ang)))
    cos = cos.reshape(positions.shape + (half,))
    sin = sin.reshape(positions.shape + (half,))
    rest = MOBA_DH - ROT_DIM
    ones = jnp.ones(cos.shape[:-1] + (rest,), F32)
    zeros = jnp.zeros(cos.shape[:-1] + (rest,), F32)
    return (jnp.concatenate([cos, cos, ones], axis=-1), jnp.concatenate([-sin, sin, zeros], axis=-1))


def _layer(h2d, p2d, cos_t, sin_t, w_cat, w_gk2, b_gk, norm_g, w_gla_o, w_moba_o, w_out, ln1_g, ln1_b,
           w_router, b_router, w_e_gate, w_e_up, w_e_down, w_s_gate, w_s_up, w_s_down, w_ple, w_ple_gate,
           ln2_g, ln2_b):
    w_gk2_pad = jnp.concatenate([w_gk2, jnp.zeros((LANES - GLA_RANK, GLA_KDIM), w_gk2.dtype)], axis=0)

    proj = _in_proj(h2d, w_cat)
    gla_out = _gla(proj, w_gk2_pad, b_gk[None, :], norm_g[None, :])
    moba_out = _moba(proj, cos_t, sin_t)
    h1, h_bf, h_packed = _mix(gla_out, moba_out, proj, h2d, w_gla_o.astype(BF16), w_moba_o.astype(BF16),
                              w_out.astype(BF16), ln1_g[None, :], ln1_b[None, :])
    e_t, w_t, rk_t, cnt = _route(h_bf, w_router.T.astype(BF16), b_router[:, None])
    dest_t, tile_expert, n_used, last_tile_row = _dest(cnt, e_t, rk_t)
    last_tile_row = last_tile_row.reshape(N_EXPERTS)
    xs = _dispatch(last_tile_row, dest_t, h_packed)
    row_dst = _row_destinations(dest_t.reshape(N_ASSIGN))
    yk = _experts(tile_expert.reshape(TILE_TABLE), n_used[0, 0:1], last_tile_row, row_dst, xs,
                  w_e_gate, w_e_up, w_e_down)
    return _final(w_t, yk, h1, h_bf, p2d, w_s_gate.astype(BF16), w_s_up.astype(BF16),
                  w_s_down.astype(BF16), w_ple.astype(BF16), w_ple_gate.astype(BF16),
                  ln2_g[None, :], ln2_b[None, :])


def kernel(x, p, positions, w_in, w_gla_gk2, b_gla_gk, gla_norm_g, w_gla_o, w_moba_o, w_out, ln1_g, ln1_b,
           w_router, b_router, w_e_gate, w_e_up, w_e_down, w_s_gate, w_s_up, w_s_down, w_ple, w_ple_gate,
           ln2_g, ln2_b):
    cos_t, sin_t = _rope_tables(positions)
    h = x.reshape(N_TOK, D_MODEL)
    for i in range(DEPTH):
        h = _layer(h, p[i].reshape(N_TOK, PLE_DIM), cos_t, sin_t, _repack_w_in(w_in, i), w_gla_gk2[i], b_gla_gk[i],
                   gla_norm_g[i], w_gla_o[i], w_moba_o[i], w_out[i], ln1_g[i], ln1_b[i], w_router[i],
                   b_router[i], w_e_gate[i], w_e_up[i], w_e_down[i], w_s_gate[i], w_s_up[i], w_s_down[i],
                   w_ple[i], w_ple_gate[i], ln2_g[i], ln2_b[i])
    return h.reshape(BATCH, SEQ, D_MODEL)
```

```python
import functools

import jax
import jax.numpy as jnp
from jax import lax
from jax.experimental import pallas as pl
from jax.experimental.pallas import tpu as pltpu
from jax.experimental.pallas import tpu_sc as plsc

F32 = jnp.float32
BF16 = jnp.bfloat16
I32 = jnp.int32
U32 = jnp.uint32

LANES = 128
SUBLANES = 8
VMEM_LIMIT_BYTES = 48 * 1024 * 1024

D_MODEL = 1024
BATCH = 8
SEQ = 2048
N_TOK = BATCH * SEQ
GLA_HEADS = 4
GLA_DK = 128
GLA_DV = 256
GLA_RANK = 16
GLA_NORMALIZER = 16.0
GLA_CHUNK = 64
GLA_GROUP = 4 * GLA_CHUNK
GLA_UNROLL = 4
MOBA_HEADS = 8
MOBA_DH = 128
MOBA_BLOCK = 256
MOBA_TOPK = 3
ROT_DIM = 32
ROPE_THETA = 500000.0
N_EXPERTS = 256
TOPK_EXPERTS = 8
N_GROUPS = 8
GROUP_SIZE = N_EXPERTS // N_GROUPS
TOPK_GROUPS = 4
D_EXPERT = 256
D_SHARED = 256
ROUTED_SCALE = 2.5
PLE_DIM = 256
LN_EPS = 1e-5
DEPTH = 1
ALPHA = (2.0 * DEPTH) ** 0.25
GLA_KDIM = GLA_HEADS * GLA_DK
GLA_VDIM = GLA_HEADS * GLA_DV
MOBA_DIM = MOBA_HEADS * MOBA_DH
N_KBLK = SEQ // MOBA_BLOCK

COL_GQ = 0
COL_GK = COL_GQ + GLA_KDIM
COL_GV = COL_GK + GLA_KDIM
COL_GR = COL_GV + GLA_VDIM
COL_MQ = COL_GR + GLA_VDIM
COL_MK = COL_MQ + MOBA_DIM
COL_MV = COL_MK + MOBA_DIM
COL_GA = COL_MV + MOBA_DIM
COL_GB = COL_GA + D_MODEL
COL_LOW = COL_GB + D_MODEL
PROJ_COLS = COL_LOW + LANES
LOW_SRC = 2 * GLA_KDIM + 2 * GLA_VDIM

ROW_TILE = 256
PACKED_COLS = D_MODEL // 2
PACKED_CHUNKS = PACKED_COLS // LANES
EXPERT_BUFS = 3
YK_ROWS = (TOPK_EXPERTS + 1) * N_TOK
SLOT_CHUNK = 4096
assert EXPERT_BUFS * ROW_TILE <= N_TOK
N_ASSIGN = N_TOK * TOPK_EXPERTS
N_ROW_TILES = (N_ASSIGN + N_EXPERTS * (ROW_TILE - 1) + ROW_TILE - 1) // ROW_TILE
N_ROWS = N_ROW_TILES * ROW_TILE
TILE_TABLE = -(-N_ROW_TILES // LANES) * LANES

PROJ_TM = 1024
REPACK_TN = 512
PROJ_TN = PROJ_COLS // 5
MIX_TM = 512
ROUTE_TM = 512
DEST_TM = 2048
DISP_TM = 256
FINAL_TM = 256

_NEG_INF = float("-inf")
LOG2_E = 1.4426950408889634


def _cparams(*sem):
    return pltpu.CompilerParams(dimension_semantics=sem, vmem_limit_bytes=VMEM_LIMIT_BYTES)


def _dot(a, b):
    return jnp.dot(a, b, preferred_element_type=F32)


def _dot_nt(a, b):
    return lax.dot_general(a, b, (((1,), (1,)), ((), ())), preferred_element_type=F32)


def _dot_tn(a, b):
    return lax.dot_general(a, b, (((0,), (0,)), ((), ())), preferred_element_type=F32)


def _split_bf16(x):
    hi = x.astype(BF16)
    lo = (x - hi.astype(F32)).astype(BF16)
    return hi, lo


def _sigmoid(x):
    return 1.0 / (1.0 + jnp.exp(-x))


def _pack_bf16_pair(x):
    lo = lax.bitcast_convert_type(x[:, :PACKED_COLS].astype(BF16).astype(F32), U32)
    hi = lax.bitcast_convert_type(x[:, PACKED_COLS:].astype(BF16).astype(F32), U32)
    return hi | lax.shift_right_logical(lo, jnp.uint32(16))


def _unpack_bf16_pair(w):
    lo = lax.bitcast_convert_type(lax.shift_left(w, jnp.uint32(16)), F32)
    hi = lax.bitcast_convert_type(w & jnp.uint32(0xFFFF0000), F32)
    return lo, hi


def _repack_body(a_ref, b_ref, o_ref):
    t = pl.program_id(0)
    a = a_ref[...].astype(BF16)

    @pl.when(t < LOW_SRC // REPACK_TN)
    def _():
        o_ref[...] = a

    @pl.when((t >= LOW_SRC // REPACK_TN) & (t < COL_LOW // REPACK_TN))
    def _():
        o_ref[...] = jnp.concatenate([a[:, GLA_RANK:], b_ref[:, :GLA_RANK].astype(BF16)], axis=1)

    @pl.when(t == COL_LOW // REPACK_TN)
    def _():
        o_ref[...] = jnp.concatenate([a[:, :GLA_RANK], jnp.zeros((D_MODEL, REPACK_TN - GLA_RANK), BF16)], axis=1)


def _repack_w_in(w_in, layer):
    assert LOW_SRC % REPACK_TN == 0 and COL_LOW % REPACK_TN == 0
    low_tile = LOW_SRC // REPACK_TN
    last_out = COL_LOW // REPACK_TN
    lanes_per_tile = REPACK_TN // LANES
    w_rows = w_in.reshape(-1, w_in.shape[-1])
    return pl.pallas_call(
        _repack_body,
        grid=(last_out + 1,),
        in_specs=[
            pl.BlockSpec((D_MODEL, REPACK_TN), lambda t: (layer, jnp.where(t == last_out, low_tile, t))),
            pl.BlockSpec((D_MODEL, LANES), lambda t: (layer, jnp.minimum(t + 1, last_out) * lanes_per_tile)),
        ],
        out_specs=pl.BlockSpec((D_MODEL, REPACK_TN), lambda t: (0, t)),
        out_shape=jax.ShapeDtypeStruct((D_MODEL, PROJ_COLS), BF16),
        compiler_params=_cparams("parallel"),
        name="repack_w_in",
    )(w_rows, w_rows)


def _proj_body(x_ref, w_ref, o_ref):
    o_ref[...] = _dot(x_ref[...].astype(BF16), w_ref[...]).astype(BF16)


def _in_proj(x2d, w_cat):
    return pl.pallas_call(
        _proj_body,
        grid=(PROJ_COLS // PROJ_TN, N_TOK // PROJ_TM),
        in_specs=[
            pl.BlockSpec((PROJ_TM, D_MODEL), lambda j, i: (i, 0)),
            pl.BlockSpec((D_MODEL, PROJ_TN), lambda j, i: (0, j)),
        ],
        out_specs=pl.BlockSpec((PROJ_TM, PROJ_TN), lambda j, i: (i, j)),
        out_shape=jax.ShapeDtypeStruct((N_TOK, PROJ_COLS), BF16),
        compiler_params=_cparams("parallel", "parallel"),
        name="in_proj",
    )(x2d, w_cat)


def _gla_body(q_ref, k_ref, v_ref, r_ref, low_ref, wg_ref, bg_ref, ng_ref, o_ref, st_ref, gk_ref):
    w_hi, w_lo = _split_bf16(wg_ref[...])
    low = low_ref[...]
    lin = _dot(low, w_hi) + _dot(low, w_lo) + bg_ref[...]
    gk_ref[...] = (jnp.minimum(lin, 0.0) - jnp.log1p(jnp.exp(-jnp.abs(lin)))) * (1.0 / GLA_NORMALIZER)
    st_ref[...] = jnp.zeros_like(st_ref)

    ri = lax.broadcasted_iota(I32, (GLA_GROUP, GLA_GROUP), 0)
    ci = lax.broadcasted_iota(I32, (GLA_GROUP, GLA_GROUP), 1)
    same_chunk = lax.shift_right_logical(ri, GLA_CHUNK.bit_length() - 1) == lax.shift_right_logical(
        ci, GLA_CHUNK.bit_length() - 1)
    causal = same_chunk & (ri >= ci)
    sums = jnp.concatenate([jnp.where(causal, 1.0, 0.0), jnp.where(same_chunk, 1.0, 0.0)], axis=0).astype(BF16)
    gain = ng_ref[...]

    def group(c, carry):
        rows = pl.ds(pl.multiple_of(c * GLA_GROUP, GLA_GROUP), GLA_GROUP)
        g_hi, g_lo = _split_bf16(gk_ref[rows, :])
        bb = _dot(sums, g_hi) + _dot(sums, g_lo)
        b = bb[0:GLA_GROUP]
        b_end = bb[GLA_GROUP:2 * GLA_GROUP]
        q = q_ref[rows, :].astype(F32) * (GLA_DK ** -0.5)
        k = k_ref[rows, :].astype(F32)
        v = v_ref[rows, :]
        q_e = (q * jnp.exp(b)).astype(BF16)
        k_e = (k * jnp.exp(-b)).astype(BF16)
        k_d = (k * jnp.exp(b_end - b)).astype(BF16)
        att = jnp.where(causal, _dot_nt(q_e, k_e), 0.0)
        o = _dot(att.astype(BF16), v)
        st = st_ref[...]
        inter = []
        for j in range(GLA_GROUP // GLA_CHUNK):
            cr = slice(j * GLA_CHUNK, (j + 1) * GLA_CHUNK)
            inter.append(_dot_nt(q_e[cr], st.astype(BF16)))
            st = st * jnp.exp(b_end[j * GLA_CHUNK:j * GLA_CHUNK + 1, :]) + _dot_tn(v[cr], k_d[cr])
        st_ref[...] = st
        o = o + jnp.concatenate(inter, axis=0)
        o = o * lax.rsqrt(jnp.mean(o * o, axis=-1, keepdims=True) + LN_EPS) * gain
        r = r_ref[rows, :].astype(F32)
        o_ref[rows, :] = (o * (r * _sigmoid(r))).astype(BF16)
        return carry

    lax.fori_loop(0, SEQ // GLA_GROUP, group, 0, unroll=GLA_UNROLL)


def _gla(proj, w_gk2_pad, b_gk, norm_g):
    kb, vb = GLA_DK, GLA_DV
    return pl.pallas_call(
        _gla_body,
        grid=(BATCH, GLA_HEADS),
        in_specs=[
            pl.BlockSpec((SEQ, kb), lambda b, h: (b, COL_GQ // kb + h)),
            pl.BlockSpec((SEQ, kb), lambda b, h: (b, COL_GK // kb + h)),
            pl.BlockSpec((SEQ, vb), lambda b, h: (b, COL_GV // vb + h)),
            pl.BlockSpec((SEQ, vb), lambda b, h: (b, COL_GR // vb + h)),
            pl.BlockSpec((SEQ, LANES), lambda b, h: (b, COL_LOW // LANES)),
            pl.BlockSpec((LANES, kb), lambda b, h: (0, h)),
            pl.BlockSpec((1, kb), lambda b, h: (0, h)),
            pl.BlockSpec((1, vb), lambda b, h: (0, 0)),
        ],
        out_specs=pl.BlockSpec((SEQ, vb), lambda b, h: (b, h)),
        out_shape=jax.ShapeDtypeStruct((N_TOK, GLA_VDIM), BF16),
        scratch_shapes=[pltpu.VMEM((vb, kb), F32), pltpu.VMEM((SEQ, kb), F32)],
        compiler_params=_cparams("parallel", "parallel"),
        name="gla",
    )(proj, proj, proj, proj, proj, w_gk2_pad, b_gk, norm_g)


def _moba_body(q_ref, k_ref, v_ref, c_ref, s_ref, o_ref, qs_ref, ks_ref, vt_ref, bias_ref,
               sc_a_ref, sc_b_ref, pr_a_ref, pr_b_ref):
    rest = MOBA_DH - ROT_DIM
    cos_t = jnp.concatenate([c_ref[0], jnp.ones((SEQ, rest), F32)], axis=1)
    sin_t = jnp.concatenate([s_ref[0], jnp.zeros((SEQ, rest), F32)], axis=1)
    lane = lax.broadcasted_iota(I32, (SEQ, MOBA_DH), 1)
    half = ROT_DIM // 2

    def rope(x):
        partner = jnp.where(lane < half, pltpu.roll(x, MOBA_DH - half, 1), pltpu.roll(x, half, 1))
        return x * cos_t + partner * sin_t

    q = rope(q_ref[...].astype(F32))
    k = rope(k_ref[...].astype(F32))
    q_hi, q_lo = _split_bf16(q)
    qs_ref[...] = (q * (MOBA_DH ** -0.5 * LOG2_E)).astype(BF16)
    ks_ref[...] = k.astype(BF16)
    vt_ref[...] = v_ref[...].astype(F32).T.astype(BF16)

    k_mean = jnp.concatenate(
        [jnp.mean(k[j * MOBA_BLOCK:(j + 1) * MOBA_BLOCK], axis=0, keepdims=True) for j in range(N_KBLK)], axis=0)
    m_hi, m_lo = _split_bf16(k_mean)
    s_blk = _dot_nt(m_hi, q_hi) + _dot_nt(m_hi, q_lo) + _dot_nt(m_lo, q_hi)
    blk = lax.broadcasted_iota(I32, (N_KBLK, SEQ), 0)
    q_blk = lax.shift_right_logical(lax.broadcasted_iota(I32, (N_KBLK, SEQ), 1), MOBA_BLOCK.bit_length() - 1)
    past = blk < q_blk
    s_blk = jnp.where(past, s_blk, _NEG_INF)
    beaten = jnp.zeros((N_KBLK, SEQ), I32)
    for j in range(N_KBLK):
        row = s_blk[j:j + 1, :]
        beaten += jnp.where((row > s_blk) | ((row == s_blk) & (j < blk)), 1, 0)
    bias_ref[...] = jnp.where(past & (beaten < MOBA_TOPK), 0.0, _NEG_INF)

    kr = lax.broadcasted_iota(I32, (MOBA_BLOCK, MOBA_BLOCK), 0)
    qc = lax.broadcasted_iota(I32, (MOBA_BLOCK, MOBA_BLOCK), 1)
    own_bias = jnp.where(kr <= qc, 0.0, _NEG_INF)

    sc_bufs = (sc_a_ref, sc_b_ref)
    pr_bufs = (pr_a_ref, pr_b_ref)
    zeros_row = jnp.zeros((1, MOBA_BLOCK), F32)
    future_row = jnp.full((1, MOBA_BLOCK), _NEG_INF, F32)
    for pair in range(N_KBLK // 2):
        q_blocks = (2 * pair, 2 * pair + 1)
        cols = slice(q_blocks[0] * MOBA_BLOCK, (q_blocks[1] + 1) * MOBA_BLOCK)
        n_kblk = q_blocks[1] + 1
        n_keys = n_kblk * MOBA_BLOCK
        sc, pr = sc_bufs[pair % 2], pr_bufs[pair % 2]
        q_pair = qs_ref[cols, :]

        def query_bias(j):
            halves = []
            for qb in q_blocks:
                q_cols = slice(qb * MOBA_BLOCK, (qb + 1) * MOBA_BLOCK)
                halves.append(bias_ref[j:j + 1, q_cols] if j < qb else zeros_row if j == qb else future_row)
            return jnp.concatenate(halves, axis=1)

        biases = [query_bias(j) for j in range(n_kblk)]
        col_max = []
        for j in range(n_kblk):
            rows = slice(j * MOBA_BLOCK, (j + 1) * MOBA_BLOCK)
            s = _dot_nt(ks_ref[rows, :], q_pair)
            if j in q_blocks:
                h = q_blocks.index(j)
                own = s[:, h * MOBA_BLOCK:(h + 1) * MOBA_BLOCK] + own_bias
                s = jnp.concatenate([own, s[:, MOBA_BLOCK:]] if h == 0 else [s[:, :MOBA_BLOCK], own], axis=1)
            sc[rows, :] = s
            col_max.append(jnp.max(s, axis=0, keepdims=True) + biases[j])
        m = functools.reduce(jnp.maximum, col_max)
        denom = jnp.zeros((1, 2 * MOBA_BLOCK), F32)
        for j in range(n_kblk):
            rows = slice(j * MOBA_BLOCK, (j + 1) * MOBA_BLOCK)
            p = jnp.exp2(sc[rows, :] - (m - biases[j]))
            denom = denom + jnp.sum(p, axis=0, keepdims=True)
            pr[rows, :] = p.astype(BF16)
        o_t = _dot(vt_ref[:, 0:n_keys], pr[0:n_keys, :]) * (1.0 / denom)
        o_ref[cols, :] = o_t.T.astype(BF16)


def _moba(proj, cos_t, sin_t):
    dh = MOBA_DH
    return pl.pallas_call(
        _moba_body,
        grid=(BATCH, MOBA_HEADS),
        in_specs=[
            pl.BlockSpec((SEQ, dh), lambda b, h: (b, COL_MQ // dh + h)),
            pl.BlockSpec((SEQ, dh), lambda b, h: (b, COL_MK // dh + h)),
            pl.BlockSpec((SEQ, dh), lambda b, h: (b, COL_MV // dh + h)),
            pl.BlockSpec((1, SEQ, ROT_DIM), lambda b, h: (b, 0, 0)),
            pl.BlockSpec((1, SEQ, ROT_DIM), lambda b, h: (b, 0, 0)),
        ],
        out_specs=pl.BlockSpec((SEQ, dh), lambda b, h: (b, h)),
        out_shape=jax.ShapeDtypeStruct((N_TOK, MOBA_DIM), BF16),
        scratch_shapes=[
            pltpu.VMEM((SEQ, dh), BF16),
            pltpu.VMEM((SEQ, dh), BF16),
            pltpu.VMEM((dh, SEQ), BF16),
            pltpu.VMEM((N_KBLK, SEQ), F32),
            pltpu.VMEM((SEQ, 2 * MOBA_BLOCK), F32),
            pltpu.VMEM((SEQ, 2 * MOBA_BLOCK), F32),
            pltpu.VMEM((SEQ, 2 * MOBA_BLOCK), BF16),
            pltpu.VMEM((SEQ, 2 * MOBA_BLOCK), BF16),
        ],
        compiler_params=_cparams("parallel", "parallel"),
        name="moba",
    )(proj, proj, proj, cos_t, sin_t)


def _layer_norm(z, g, b):
    mu = jnp.mean(z, axis=-1, keepdims=True)
    zc = z - mu
    var = jnp.mean(zc * zc, axis=-1, keepdims=True)
    return zc * lax.rsqrt(var + LN_EPS) * g + b


def _mix_body(gla_ref, moba_ref, ga_ref, gb_ref, x_ref, wgo_ref, wmo_ref, wo_ref, g_ref, b_ref,
              h_ref, hb_ref, hp_ref):
    y_gla = _dot(gla_ref[...], wgo_ref[...])
    y_moba = _dot(moba_ref[...], wmo_ref[...])
    merged = _sigmoid(ga_ref[...].astype(F32)) * y_gla + _sigmoid(gb_ref[...].astype(F32)) * y_moba
    mix = _dot(merged.astype(BF16), wo_ref[...])
    h = _layer_norm(ALPHA * x_ref[...] + mix, g_ref[...], b_ref[...])
    h_ref[...] = h
    hb_ref[...] = h.astype(BF16)
    packed = _pack_bf16_pair(h)
    for c in range(PACKED_CHUNKS):
        hp_ref[:, c, :] = packed[:, c * LANES:(c + 1) * LANES]


def _mix(gla_out, moba_out, proj, x2d, w_gla_o, w_moba_o, w_out, ln_g, ln_b):
    d = D_MODEL
    row = lambda i: (i, 0)
    full = lambda i: (0, 0)
    return pl.pallas_call(
        _mix_body,
        grid=(N_TOK // MIX_TM,),
        in_specs=[
            pl.BlockSpec((MIX_TM, d), row),
            pl.BlockSpec((MIX_TM, d), row),
            pl.BlockSpec((MIX_TM, d), lambda i: (i, COL_GA // d)),
            pl.BlockSpec((MIX_TM, d), lambda i: (i, COL_GB // d)),
            pl.BlockSpec((MIX_TM, d), row),
            pl.BlockSpec((d, d), full),
            pl.BlockSpec((d, d), full),
            pl.BlockSpec((d, d), full),
            pl.BlockSpec((1, d), full),
            pl.BlockSpec((1, d), full),
        ],
        out_specs=[
            pl.BlockSpec((MIX_TM, d), row),
            pl.BlockSpec((MIX_TM, d), row),
            pl.BlockSpec((MIX_TM, PACKED_CHUNKS, LANES), lambda i: (i, 0, 0)),
        ],
        out_shape=[
            jax.ShapeDtypeStruct((N_TOK, d), F32),
            jax.ShapeDtypeStruct((N_TOK, d), BF16),
            jax.ShapeDtypeStruct((N_TOK, PACKED_CHUNKS, LANES), U32),
        ],
        compiler_params=_cparams("parallel"),
        name="mix_ln1",
    )(gla_out, moba_out, proj, proj, x2d, w_gla_o, w_moba_o, w_out, ln_g, ln_b)


def _route_body(h_ref, wr_ref, br_ref, e_ref, w_ref, rk_ref, cnt_ref, carry_ref):
    tm = ROUTE_TM

    @pl.when(pl.program_id(0) == 0)
    def _():
        carry_ref[...] = jnp.zeros_like(carry_ref)

    scores = _sigmoid(_dot_nt(wr_ref[...], h_ref[...]))
    biased = scores + br_ref[...]
    row = lax.broadcasted_iota(I32, (N_EXPERTS, tm), 0).astype(F32)
    row_g = lax.broadcasted_iota(I32, (GROUP_SIZE, tm), 0).astype(F32)

    g_scores = []
    for g in range(N_GROUPS):
        grp = biased[g * GROUP_SIZE:(g + 1) * GROUP_SIZE]
        m1 = jnp.max(grp, axis=0, keepdims=True)
        first = jnp.min(jnp.where(grp == m1, row_g, float(GROUP_SIZE)), axis=0, keepdims=True)
        m2 = jnp.max(jnp.where(row_g == first, _NEG_INF, grp), axis=0, keepdims=True)
        g_scores.append(m1 + m2)
    g_score = jnp.concatenate(g_scores, axis=0)
    g_row = lax.broadcasted_iota(I32, (N_GROUPS, tm), 0)
    g_beaten = jnp.zeros((N_GROUPS, tm), I32)
    for g in range(N_GROUPS):
        r = g_score[g:g + 1, :]
        g_beaten += jnp.where((r > g_score) | ((r == g_score) & (g < g_row)), 1, 0)
    g_keep = g_beaten < TOPK_GROUPS
    masked = jnp.concatenate(
        [jnp.where(g_keep[g:g + 1, :], biased[g * GROUP_SIZE:(g + 1) * GROUP_SIZE], _NEG_INF)
         for g in range(N_GROUPS)], axis=0)

    onehot = jnp.zeros((N_EXPERTS, tm), F32)
    picks, pick_scores = [], []
    for _ in range(TOPK_EXPERTS):
        m = jnp.max(masked, axis=0, keepdims=True)
        idx = jnp.min(jnp.where(masked == m, row, float(N_EXPERTS)), axis=0, keepdims=True)
        hit = row == idx
        picks.append(idx)
        pick_scores.append(jnp.sum(jnp.where(hit, scores, 0.0), axis=0, keepdims=True))
        onehot = onehot + jnp.where(hit, 1.0, 0.0)
        masked = jnp.where(hit, _NEG_INF, masked)
    sel = jnp.concatenate(pick_scores, axis=0)
    e_ref[...] = jnp.concatenate(picks, axis=0).astype(I32)
    w_ref[...] = sel / jnp.sum(sel, axis=0, keepdims=True) * ROUTED_SCALE

    t_r = lax.broadcasted_iota(I32, (tm, tm), 0)
    t_c = lax.broadcasted_iota(I32, (tm, tm), 1)
    earlier = jnp.where(t_r < t_c, 1.0, 0.0).astype(BF16)
    seen = _dot(onehot.astype(BF16), earlier) + carry_ref[...]
    rk_ref[...] = jnp.concatenate(
        [jnp.sum(jnp.where(row == idx, seen, 0.0), axis=0, keepdims=True) for idx in picks], axis=0).astype(I32)
    carry_ref[...] += jnp.sum(onehot, axis=1, keepdims=True)
    cnt_ref[...] = carry_ref[...]


def _route(h_bf, w_router_t, b_router_col):
    k = TOPK_EXPERTS
    tok = lambda i: (0, i)
    return pl.pallas_call(
        _route_body,
        grid=(N_TOK // ROUTE_TM,),
        in_specs=[
            pl.BlockSpec((ROUTE_TM, D_MODEL), lambda i: (i, 0)),
            pl.BlockSpec((N_EXPERTS, D_MODEL), lambda i: (0, 0)),
            pl.BlockSpec((N_EXPERTS, 1), lambda i: (0, 0)),
        ],
        out_specs=[
            pl.BlockSpec((k, ROUTE_TM), tok),
            pl.BlockSpec((k, ROUTE_TM), tok),
            pl.BlockSpec((k, ROUTE_TM), tok),
            pl.BlockSpec((N_EXPERTS, 1), lambda i: (0, 0)),
        ],
        out_shape=[
            jax.ShapeDtypeStruct((k, N_TOK), I32),
            jax.ShapeDtypeStruct((k, N_TOK), F32),
            jax.ShapeDtypeStruct((k, N_TOK), I32),
            jax.ShapeDtypeStruct((N_EXPERTS, 1), F32),
        ],
        scratch_shapes=[pltpu.VMEM((N_EXPERTS, 1), F32)],
        compiler_params=_cparams("arbitrary"),
        name="route",
    )(h_bf, w_router_t, b_router_col)


def _dest_body(cnt_ref, e_ref, rk_ref, d_ref, te_ref, nu_ref, lt_ref):
    cnt = cnt_ref[...]
    tiles = jnp.floor((cnt + (ROW_TILE - 1)) * (1.0 / ROW_TILE))
    er = lax.broadcasted_iota(I32, (N_EXPERTS, N_EXPERTS), 0)
    ec = lax.broadcasted_iota(I32, (N_EXPERTS, N_EXPERTS), 1)
    before = jnp.where(ec < er, 1.0, 0.0).astype(BF16)
    tiles_b = jnp.broadcast_to(tiles, (N_EXPERTS, LANES)).astype(BF16)
    t_start = _dot(before, tiles_b)[:, 0:1]
    t_end = t_start + tiles
    p_start = t_start * float(ROW_TILE)
    lt_ref[...] = jnp.where(tiles > 0.0, (t_end - 1.0) * float(ROW_TILE), -1.0).astype(I32)

    row = lax.broadcasted_iota(I32, (N_EXPERTS, DEST_TM), 0)
    d_ref[...] = jnp.concatenate(
        [jnp.sum(jnp.where(row == e_ref[k:k + 1, :], p_start, 0.0), axis=0, keepdims=True)
         for k in range(TOPK_EXPERTS)], axis=0).astype(I32) + rk_ref[...]

    tile_id = lax.broadcasted_iota(I32, (N_EXPERTS, TILE_TABLE), 1).astype(F32)
    owner = jnp.sum(jnp.where(t_end <= tile_id, 1, 0), axis=0, keepdims=True)
    te_ref[...] = jnp.minimum(owner, N_EXPERTS - 1)
    nu_ref[...] = jnp.broadcast_to(t_end[N_EXPERTS - 1:N_EXPERTS, :], (1, LANES)).astype(I32)


def _dest(cnt, e_t, rk_t):
    k = TOPK_EXPERTS
    tok = lambda i: (0, i)
    return pl.pallas_call(
        _dest_body,
        grid=(N_TOK // DEST_TM,),
        in_specs=[
            pl.BlockSpec((N_EXPERTS, 1), lambda i: (0, 0)),
            pl.BlockSpec((k, DEST_TM), tok),
            pl.BlockSpec((k, DEST_TM), tok),
        ],
        out_specs=[
            pl.BlockSpec((k, DEST_TM), tok),
            pl.BlockSpec((1, TILE_TABLE), lambda i: (0, 0)),
            pl.BlockSpec((1, LANES), lambda i: (0, 0)),
            pl.BlockSpec((N_EXPERTS, 1), lambda i: (0, 0)),
        ],
        out_shape=[
            jax.ShapeDtypeStruct((k, N_TOK), I32),
            jax.ShapeDtypeStruct((1, TILE_TABLE), I32),
            jax.ShapeDtypeStruct((1, LANES), I32),
            jax.ShapeDtypeStruct((N_EXPERTS, 1), I32),
        ],
        compiler_params=_cparams("arbitrary"),
        name="dest",
    )(cnt, e_t, rk_t)


def _row_copy(src, dst, sem):
    return pltpu.make_async_copy(src, dst, sem)


def _dispatch_body(lt_ref, d_ref, h_ref, xs_ref, zero_ref, sem):
    @pl.when(pl.program_id(0) == 0)
    def _():
        zero_ref[...] = jnp.zeros_like(zero_ref)

        def tile_copy(e):
            r0 = pl.multiple_of(jnp.maximum(lt_ref[e], 0), ROW_TILE)
            return pltpu.make_async_copy(zero_ref, xs_ref.at[pl.ds(r0, ROW_TILE)], sem)

        def z_start(e, carry):
            @pl.when(lt_ref[e] >= 0)
            def _():
                tile_copy(e).start()
            return carry

        def z_wait(e, carry):
            @pl.when(lt_ref[e] >= 0)
            def _():
                tile_copy(e).wait()
            return carry

        lax.fori_loop(0, N_EXPERTS, z_start, 0)
        lax.fori_loop(0, N_EXPERTS, z_wait, 0)

    for t in range(DISP_TM):
        for k in range(TOPK_EXPERTS):
            _row_copy(h_ref.at[t], xs_ref.at[d_ref[k, t]], sem).start(priority=k % 2)
    for k in range(TOPK_EXPERTS):
        pltpu.make_async_copy(h_ref, xs_ref.at[pl.ds(0, DISP_TM)], sem).wait()


def _dispatch(last_tile_row, dest_t, h1):
    grid_spec = pltpu.PrefetchScalarGridSpec(
        num_scalar_prefetch=1,
        grid=(N_TOK // DISP_TM,),
        in_specs=[
            pl.BlockSpec((TOPK_EXPERTS, DISP_TM), lambda i, lt: (0, i), memory_space=pltpu.SMEM),
            pl.BlockSpec((DISP_TM, PACKED_CHUNKS, LANES), lambda i, lt: (i, 0, 0)),
        ],
        out_specs=pl.BlockSpec(memory_space=pl.ANY),
        scratch_shapes=[pltpu.VMEM((ROW_TILE, PACKED_CHUNKS, LANES), U32), pltpu.SemaphoreType.DMA],
    )
    return pl.pallas_call(
        _dispatch_body,
        grid_spec=grid_spec,
        out_shape=jax.ShapeDtypeStruct((N_ROWS, PACKED_CHUNKS, LANES), U32),
        compiler_params=_cparams("arbitrary"),
        name="dispatch",
    )(last_tile_row, dest_t, h1)


def _row_destinations(dest_flat):
    info = plsc.get_sparse_core_info()
    n_cores, lanes = info.num_cores, info.num_lanes
    n_workers = n_cores * info.num_subcores
    rows_per_worker = -(-N_ROWS // (n_workers * ROW_TILE)) * ROW_TILE
    assert rows_per_worker % lanes == 0
    mesh = plsc.VectorSubcoreMesh(core_axis_name="c", subcore_axis_name="s")

    @functools.partial(
        pl.kernel, mesh=mesh, out_type=jax.ShapeDtypeStruct((n_workers * rows_per_worker,), I32),
        scratch_types=[pltpu.VMEM((SLOT_CHUNK,), I32), pltpu.VMEM((rows_per_worker,), I32)],
        compiler_params=pltpu.CompilerParams(needs_layout_passes=False), name="row_destinations")
    def invert(dest_hbm, out_hbm, dest_v, table_v):
        first_row = (lax.axis_index("s") * n_cores + lax.axis_index("c")) * rows_per_worker
        lane = lax.iota(I32, lanes)
        tile_shift = ROW_TILE.bit_length() - 1

        def spare(i, carry):
            row = first_row + i * lanes + lane
            buf = lax.rem(lax.shift_right_logical(row, tile_shift), EXPERT_BUFS)
            table_v[pl.ds(i * lanes, lanes)] = TOPK_EXPERTS * N_TOK + buf * ROW_TILE + (row & (ROW_TILE - 1))
            return carry

        lax.fori_loop(0, rows_per_worker // lanes, spare, 0)

        def chunk(c, carry):
            pltpu.sync_copy(dest_hbm.at[pl.ds(c * SLOT_CHUNK, SLOT_CHUNK)], dest_v)

            def vec(i, inner):
                local = dest_v[pl.ds(i * lanes, lanes)] - first_row
                mine = (local >= 0) & (local < rows_per_worker)
                plsc.store_scatter(table_v, [jnp.where(mine, local, 0)], lane + (c * SLOT_CHUNK + i * lanes),
                                   mask=mine)
                return inner

            lax.fori_loop(0, SLOT_CHUNK // lanes, vec, 0)
            return carry

        lax.fori_loop(0, N_ASSIGN // SLOT_CHUNK, chunk, 0)
        pltpu.sync_copy(table_v, out_hbm.at[pl.ds(first_row, rows_per_worker)])

    return invert(dest_flat)


def _experts_body(te_ref, nu_ref, lt_ref, slot_ref, xs_ref, wg_ref, wu_ref, wd_ref, yk_ref,
                  wg_f, wu_f, wd_f, wg_b, wu_b, wd_b, y_buf, n_loaded, sem, w_sem):
    i = pl.program_id(0)
    n_used = nu_ref[0]

    def weight_copies(e, s):
        return (pltpu.make_async_copy(wg_ref.at[e], wg_f.at[s], w_sem.at[s]),
                pltpu.make_async_copy(wu_ref.at[e], wu_f.at[s], w_sem.at[s]),
                pltpu.make_async_copy(wd_ref.at[e], wd_f.at[s], w_sem.at[s]))

    def wait_tile(b):
        pltpu.make_async_copy(y_buf.at[b], yk_ref.at[pl.ds(0, ROW_TILE), :], sem.at[b]).wait()

    def send_tile(b):
        for r in range(ROW_TILE):
            dst = slot_ref[0, 0, r]
            _row_copy(y_buf.at[b, pl.ds(r, 1), :], yk_ref.at[pl.ds(dst, 1), :], sem.at[b]).start(priority=r % 2)

    @pl.when(i == 0)
    def _():
        y_buf[EXPERT_BUFS - 1] = jnp.zeros((ROW_TILE, PACKED_COLS), U32)
        n_loaded[0] = 0
        for c in weight_copies(te_ref[0], 0):
            c.start()

    @pl.when((i >= 2) & (i <= n_used))
    def _():
        wait_tile(lax.rem(i, EXPERT_BUFS))

    @pl.when(i < n_used)
    def _():
        e = te_ref[i]
        prev = te_ref[jnp.maximum(i - 1, 0)]

        @pl.when((i == 0) | (e != prev))
        def _():
            s = lax.rem(n_loaded[0], 2)
            for c in weight_copies(e, s):
                c.wait()
            wg_b[...] = wg_f[s].astype(BF16)
            wu_b[...] = wu_f[s].astype(BF16)
            wd_b[...] = wd_f[s].astype(BF16)
            n_loaded[0] = n_loaded[0] + 1
            nxt = lax.shift_right_logical(lt_ref[e], ROW_TILE.bit_length() - 1) + 1

            @pl.when(nxt < n_used)
            def _():
                for c in weight_copies(te_ref[nxt], 1 - s):
                    c.start()

        for phase in range(EXPERT_BUFS):
            @pl.when(lax.rem(i, EXPERT_BUFS) == phase)
            def _(phase=phase):
                send_tile((phase + EXPERT_BUFS - 1) % EXPERT_BUFS)
                packed = jnp.concatenate([xs_ref[:, c, :] for c in range(PACKED_CHUNKS)], axis=1)
                x_lo, x_hi = _unpack_bf16_pair(packed)
                x = jnp.concatenate([x_lo.astype(BF16), x_hi.astype(BF16)], axis=1)
                g = _dot(x, wg_b[...])
                u = _dot(x, wu_b[...])
                h = (g * _sigmoid(g)) * u
                y_buf[phase] = _pack_bf16_pair(_dot(h.astype(BF16), wd_b[...]))

    @pl.when(i == n_used)
    def _():
        send_tile(lax.rem(i + EXPERT_BUFS - 1, EXPERT_BUFS))
        wait_tile(lax.rem(i + EXPERT_BUFS - 2, EXPERT_BUFS))
        wait_tile(lax.rem(i + EXPERT_BUFS - 1, EXPERT_BUFS))


def _experts(tile_expert, n_used, last_tile_row, row_dst, xs, w_gate, w_up, w_down):
    def tile(i, te, nu, lt):
        return (jnp.minimum(i, nu[0] - 1), 0, 0)

    n_table_tiles = row_dst.shape[0] // ROW_TILE
    placeholder = n_table_tiles - 1
    assert placeholder * ROW_TILE >= N_ROWS and placeholder % EXPERT_BUFS == EXPERT_BUFS - 1

    def prev_tile(i, te, nu, lt):
        return (jnp.where(i == 0, placeholder, jnp.minimum(i, nu[0]) - 1), 0, 0)

    grid_spec = pltpu.PrefetchScalarGridSpec(
        num_scalar_prefetch=3,
        grid=(N_ROW_TILES + 1,),
        in_specs=[
            pl.BlockSpec((1, 1, ROW_TILE), prev_tile, memory_space=pltpu.SMEM),
            pl.BlockSpec((ROW_TILE, PACKED_CHUNKS, LANES), tile),
            pl.BlockSpec(memory_space=pl.ANY),
            pl.BlockSpec(memory_space=pl.ANY),
            pl.BlockSpec(memory_space=pl.ANY),
        ],
        out_specs=pl.BlockSpec(memory_space=pl.ANY),
        scratch_shapes=[
            pltpu.VMEM((2, D_MODEL, D_EXPERT), F32),
            pltpu.VMEM((2, D_MODEL, D_EXPERT), F32),
            pltpu.VMEM((2, D_EXPERT, D_MODEL), F32),
            pltpu.VMEM((D_MODEL, D_EXPERT), BF16),
            pltpu.VMEM((D_MODEL, D_EXPERT), BF16),
            pltpu.VMEM((D_EXPERT, D_MODEL), BF16),
            pltpu.VMEM((EXPERT_BUFS, ROW_TILE, PACKED_COLS), U32),
            pltpu.SMEM((1,), I32),
            pltpu.SemaphoreType.DMA((EXPERT_BUFS,)),
            pltpu.SemaphoreType.DMA((2,)),
        ],
    )
    return pl.pallas_call(
        _experts_body,
        grid_spec=grid_spec,
        out_shape=jax.ShapeDtypeStruct((YK_ROWS, PACKED_COLS), U32),
        compiler_params=_cparams("arbitrary"),
        name="experts",
    )(tile_expert, n_used, last_tile_row, row_dst.reshape(n_table_tiles, 1, ROW_TILE), xs, w_gate, w_up, w_down)


def _final_body(w_ref, yk_ref, h_ref, hb_ref, p_ref, wsg_ref, wsu_ref, wsd_ref, wpl_ref, wpg_ref,
                g_ref, b_ref, o_ref):
    hb = hb_ref[...]
    sg = _dot(hb, wsg_ref[...])
    shared = _dot(((sg * _sigmoid(sg)) * _dot(hb, wsu_ref[...])).astype(BF16), wsd_ref[...])
    ple = _sigmoid(_dot(hb, wpg_ref[...])) * _dot(p_ref[...].astype(BF16), wpl_ref[...])

    w_col = w_ref[...].T
    y_lo, y_hi = _unpack_bf16_pair(yk_ref[0])
    r_lo, r_hi = y_lo * w_col[:, 0:1], y_hi * w_col[:, 0:1]
    for k in range(1, TOPK_EXPERTS):
        y_lo, y_hi = _unpack_bf16_pair(yk_ref[k])
        r_lo, r_hi = r_lo + y_lo * w_col[:, k:k + 1], r_hi + y_hi * w_col[:, k:k + 1]
    routed = jnp.concatenate([r_lo, r_hi], axis=1)
    o_ref[...] = _layer_norm(ALPHA * h_ref[...] + (routed + shared) + ple, g_ref[...], b_ref[...])


def _final(w_t, yk, h1, h_bf, p2d, w_s_gate, w_s_up, w_s_down, w_ple, w_ple_gate, ln_g, ln_b):
    d = D_MODEL
    full = lambda i: (0, 0)
    return pl.pallas_call(
        _final_body,
        grid=(N_TOK // FINAL_TM,),
        in_specs=[
            pl.BlockSpec((TOPK_EXPERTS, FINAL_TM), lambda i: (0, i)),
            pl.BlockSpec((TOPK_EXPERTS, FINAL_TM, PACKED_COLS), lambda i: (0, i, 0)),
            pl.BlockSpec((FINAL_TM, d), lambda i: (i, 0)),
            pl.BlockSpec((FINAL_TM, d), lambda i: (i, 0)),
            pl.BlockSpec((FINAL_TM, PLE_DIM), lambda i: (i, 0)),
            pl.BlockSpec((d, D_SHARED), full),
            pl.BlockSpec((d, D_SHARED), full),
            pl.BlockSpec((D_SHARED, d), full),
            pl.BlockSpec((PLE_DIM, d), full),
            pl.BlockSpec((d, d), full),
            pl.BlockSpec((1, d), full),
            pl.BlockSpec((1, d), full),
        ],
        out_specs=pl.BlockSpec((FINAL_TM, d), lambda i: (i, 0)),
        out_shape=jax.ShapeDtypeStruct((N_TOK, d), F32),
        compiler_params=_cparams("parallel"),
        name="combine_ln2",
    )(w_t, yk.reshape(TOPK_EXPERTS + 1, N_TOK, PACKED_COLS), h1, h_bf, p2d, w_s_gate, w_s_up, w_s_down,
      w_ple, w_ple_gate, ln_g, ln_b)


def _rope_tables(positions):
    half = ROT_DIM // 2
    inv = ROPE_THETA ** (-jnp.arange(0, ROT_DIM, 2, dtype=F32) / ROT_DIM)
    per_row = LANES // half
    pos = jnp.repeat(positions.reshape(-1, per_row), half, axis=1).astype(F32)
    ang = pos * jnp.tile(inv, per_row)[None, :]
    cos, sin = lax.optimization_barrier((jnp.cos(ang), jnp.sin(ang)))
    cos = cos.reshape(positions.shape + (half,))
    sin = sin.reshape(positions.shape + (half,))
    return jnp.concatenate([cos, cos], axis=-1), jnp.concatenate([-sin, sin], axis=-1)


def _layer(h2d, p2d, cos_t, sin_t, w_cat, w_gk2, b_gk, norm_g, w_gla_o, w_moba_o, w_out, ln1_g, ln1_b,
           w_router, b_router, w_e_gate, w_e_up, w_e_down, w_s_gate, w_s_up, w_s_down, w_ple, w_ple_gate,
           ln2_g, ln2_b):
    w_gk2_pad = jnp.concatenate([w_gk2, jnp.zeros((LANES - GLA_RANK, GLA_KDIM), w_gk2.dtype)], axis=0)

    proj = _in_proj(h2d, w_cat)
    gla_out = _gla(proj, w_gk2_pad, b_gk[None, :], norm_g[None, :])
    moba_out = _moba(proj, cos_t, sin_t)
    h1, h_bf, h_packed = _mix(gla_out, moba_out, proj, h2d, w_gla_o.astype(BF16), w_moba_o.astype(BF16),
                              w_out.astype(BF16), ln1_g[None, :], ln1_b[None, :])
    e_t, w_t, rk_t, cnt = _route(h_bf, w_router.T.astype(BF16), b_router[:, None])
    dest_t, tile_expert, n_used, last_tile_row = _dest(cnt, e_t, rk_t)
    last_tile_row = last_tile_row.reshape(N_EXPERTS)
    xs = _dispatch(last_tile_row, dest_t, h_packed)
    row_dst = _row_destinations(dest_t.reshape(N_ASSIGN))
    yk = _experts(tile_expert.reshape(TILE_TABLE), n_used[0, 0:1], last_tile_row, row_dst, xs,
                  w_e_gate, w_e_up, w_e_down)
    return _final(w_t, yk, h1, h_bf, p2d, w_s_gate.astype(BF16), w_s_up.astype(BF16),
                  w_s_down.astype(BF16), w_ple.astype(BF16), w_ple_gate.astype(BF16),
                  ln2_g[None, :], ln2_b[None, :])


def kernel(x, p, positions, w_in, w_gla_gk2, b_gla_gk, gla_norm_g, w_gla_o, w_moba_o, w_out, ln1_g, ln1_b,
           w_router, b_router, w_e_gate, w_e_up, w_e_down, w_s_gate, w_s_up, w_s_down, w_ple, w_ple_gate,
           ln2_g, ln2_b):
    cos_t, sin_t = _rope_tables(positions)
    h = x.reshape(N_TOK, D_MODEL)
    for i in range(DEPTH):
        h = _layer(h, p[i].reshape(N_TOK, PLE_DIM), cos_t, sin_t, _repack_w_in(w_in, i), w_gla_gk2[i], b_gla_gk[i],
                   gla_norm_g[i], w_gla_o[i], w_moba_o[i], w_out[i], ln1_g[i], ln1_b[i], w_router[i],
                   b_router[i], w_e_gate[i], w_e_up[i], w_e_down[i], w_s_gate[i], w_s_up[i], w_s_down[i],
                   w_ple[i], w_ple_gate[i], ln2_g[i], ln2_b[i])
    return h.reshape(BATCH, SEQ, D_MODEL)
```

```python
import functools

import jax
import jax.numpy as jnp
from jax import lax
from jax.experimental import pallas as pl
from jax.experimental.pallas import tpu as pltpu
from jax.experimental.pallas import tpu_sc as plsc

F32 = jnp.float32
BF16 = jnp.bfloat16
I32 = jnp.int32
U32 = jnp.uint32

LANES = 128
SUBLANES = 8
VMEM_LIMIT_BYTES = 48 * 1024 * 1024

D_MODEL = 1024
BATCH = 8
SEQ = 2048
N_TOK = BATCH * SEQ
GLA_HEADS = 4
GLA_DK = 128
GLA_DV = 256
GLA_RANK = 16
GLA_NORMALIZER = 16.0
GLA_CHUNK = 64
GLA_GROUP = 4 * GLA_CHUNK
GLA_UNROLL = 4
MOBA_HEADS = 8
MOBA_DH = 128
MOBA_BLOCK = 256
MOBA_TOPK = 3
ROT_DIM = 32
ROPE_THETA = 500000.0
N_EXPERTS = 256
TOPK_EXPERTS = 8
N_GROUPS = 8
GROUP_SIZE = N_EXPERTS // N_GROUPS
TOPK_GROUPS = 4
D_EXPERT = 256
D_SHARED = 256
ROUTED_SCALE = 2.5
PLE_DIM = 256
LN_EPS = 1e-5
DEPTH = 1
ALPHA = (2.0 * DEPTH) ** 0.25
GLA_KDIM = GLA_HEADS * GLA_DK
GLA_VDIM = GLA_HEADS * GLA_DV
MOBA_DIM = MOBA_HEADS * MOBA_DH
N_KBLK = SEQ // MOBA_BLOCK

COL_GQ = 0
COL_GK = COL_GQ + GLA_KDIM
COL_GV = COL_GK + GLA_KDIM
COL_GR = COL_GV + GLA_VDIM
COL_MQ = COL_GR + GLA_VDIM
COL_MK = COL_MQ + MOBA_DIM
COL_MV = COL_MK + MOBA_DIM
COL_GA = COL_MV + MOBA_DIM
COL_GB = COL_GA + D_MODEL
COL_LOW = COL_GB + D_MODEL
PROJ_COLS = COL_LOW + LANES
LOW_SRC = 2 * GLA_KDIM + 2 * GLA_VDIM

ROW_TILE = 256
PACKED_COLS = D_MODEL // 2
PACKED_CHUNKS = PACKED_COLS // LANES
EXPERT_BUFS = 3
YK_ROWS = (TOPK_EXPERTS + 1) * N_TOK
SLOT_CHUNK = 4096
assert EXPERT_BUFS * ROW_TILE <= N_TOK
N_ASSIGN = N_TOK * TOPK_EXPERTS
N_ROW_TILES = (N_ASSIGN + N_EXPERTS * (ROW_TILE - 1) + ROW_TILE - 1) // ROW_TILE
N_ROWS = N_ROW_TILES * ROW_TILE
TILE_TABLE = -(-N_ROW_TILES // LANES) * LANES

PROJ_TM = 1024
REPACK_TN = 512
PROJ_TN = PROJ_COLS // 5
MIX_TM = 512
ROUTE_TM = 512
DEST_TM = 2048
DISP_TM = 256
FINAL_TM = 256

_NEG_INF = float("-inf")
LOG2_E = 1.4426950408889634


def _cparams(*sem):
    return pltpu.CompilerParams(dimension_semantics=sem, vmem_limit_bytes=VMEM_LIMIT_BYTES)


def _dot(a, b):
    return jnp.dot(a, b, preferred_element_type=F32)


def _dot_nt(a, b):
    return lax.dot_general(a, b, (((1,), (1,)), ((), ())), preferred_element_type=F32)


def _dot_tn(a, b):
    return lax.dot_general(a, b, (((0,), (0,)), ((), ())), preferred_element_type=F32)


def _split_bf16(x):
    hi = x.astype(BF16)
    lo = (x - hi.astype(F32)).astype(BF16)
    return hi, lo


def _sigmoid(x):
    return 1.0 / (1.0 + jnp.exp(-x))


def _pack_bf16_pair(x):
    lo = lax.bitcast_convert_type(x[:, :PACKED_COLS].astype(BF16).astype(F32), U32)
    hi = lax.bitcast_convert_type(x[:, PACKED_COLS:].astype(BF16).astype(F32), U32)
    return hi | lax.shift_right_logical(lo, jnp.uint32(16))


def _unpack_bf16_pair(w):
    lo = lax.bitcast_convert_type(lax.shift_left(w, jnp.uint32(16)), F32)
    hi = lax.bitcast_convert_type(w & jnp.uint32(0xFFFF0000), F32)
    return lo, hi


def _repack_body(a_ref, b_ref, o_ref):
    t = pl.program_id(0)
    a = a_ref[...].astype(BF16)

    @pl.when(t < LOW_SRC // REPACK_TN)
    def _():
        o_ref[...] = a

    @pl.when((t >= LOW_SRC // REPACK_TN) & (t < COL_LOW // REPACK_TN))
    def _():
        o_ref[...] = jnp.concatenate([a[:, GLA_RANK:], b_ref[:, :GLA_RANK].astype(BF16)], axis=1)

    @pl.when(t == COL_LOW // REPACK_TN)
    def _():
        o_ref[...] = jnp.concatenate([a[:, :GLA_RANK], jnp.zeros((D_MODEL, REPACK_TN - GLA_RANK), BF16)], axis=1)


def _repack_w_in(w_in, layer):
    assert LOW_SRC % REPACK_TN == 0 and COL_LOW % REPACK_TN == 0
    low_tile = LOW_SRC // REPACK_TN
    last_out = COL_LOW // REPACK_TN
    lanes_per_tile = REPACK_TN // LANES
    return pl.pallas_call(
        _repack_body,
        grid=(last_out + 1,),
        in_specs=[
            pl.BlockSpec((None, D_MODEL, REPACK_TN), lambda t: (layer, 0, jnp.where(t == last_out, low_tile, t))),
            pl.BlockSpec((None, D_MODEL, LANES), lambda t: (layer, 0, jnp.minimum(t + 1, last_out) * lanes_per_tile)),
        ],
        out_specs=pl.BlockSpec((D_MODEL, REPACK_TN), lambda t: (0, t)),
        out_shape=jax.ShapeDtypeStruct((D_MODEL, PROJ_COLS), BF16),
        compiler_params=_cparams("parallel"),
        name="repack_w_in",
    )(w_in, w_in)


def _proj_body(x_ref, w_ref, o_ref):
    o_ref[...] = _dot(x_ref[...].astype(BF16), w_ref[...]).astype(BF16)


def _in_proj(x2d, w_cat):
    return pl.pallas_call(
        _proj_body,
        grid=(PROJ_COLS // PROJ_TN, N_TOK // PROJ_TM),
        in_specs=[
            pl.BlockSpec((PROJ_TM, D_MODEL), lambda j, i: (i, 0)),
            pl.BlockSpec((D_MODEL, PROJ_TN), lambda j, i: (0, j)),
        ],
        out_specs=pl.BlockSpec((PROJ_TM, PROJ_TN), lambda j, i: (i, j)),
        out_shape=jax.ShapeDtypeStruct((N_TOK, PROJ_COLS), BF16),
        compiler_params=_cparams("parallel", "parallel"),
        name="in_proj",
    )(x2d, w_cat)


def _gla_body(q_ref, k_ref, v_ref, r_ref, low_ref, wg_ref, bg_ref, ng_ref, o_ref, st_ref, gk_ref):
    w_hi, w_lo = _split_bf16(wg_ref[...])
    low = low_ref[...]
    lin = _dot(low, w_hi) + _dot(low, w_lo) + bg_ref[...]
    gk_ref[...] = (jnp.minimum(lin, 0.0) - jnp.log1p(jnp.exp(-jnp.abs(lin)))) * (1.0 / GLA_NORMALIZER)
    st_ref[...] = jnp.zeros_like(st_ref)

    ri = lax.broadcasted_iota(I32, (GLA_GROUP, GLA_GROUP), 0)
    ci = lax.broadcasted_iota(I32, (GLA_GROUP, GLA_GROUP), 1)
    same_chunk = lax.shift_right_logical(ri, GLA_CHUNK.bit_length() - 1) == lax.shift_right_logical(
        ci, GLA_CHUNK.bit_length() - 1)
    causal = same_chunk & (ri >= ci)
    sums = jnp.concatenate([jnp.where(causal, 1.0, 0.0), jnp.where(same_chunk, 1.0, 0.0)], axis=0).astype(BF16)
    gain = ng_ref[...]

    def group(c, carry):
        rows = pl.ds(pl.multiple_of(c * GLA_GROUP, GLA_GROUP), GLA_GROUP)
        g_hi, g_lo = _split_bf16(gk_ref[rows, :])
        bb = _dot(sums, g_hi) + _dot(sums, g_lo)
        b = bb[0:GLA_GROUP]
        b_end = bb[GLA_GROUP:2 * GLA_GROUP]
        q = q_ref[rows, :].astype(F32) * (GLA_DK ** -0.5)
        k = k_ref[rows, :].astype(F32)
        v = v_ref[rows, :]
        q_e = (q * jnp.exp(b)).astype(BF16)
        k_e = (k * jnp.exp(-b)).astype(BF16)
        k_d = (k * jnp.exp(b_end - b)).astype(BF16)
        att = jnp.where(causal, _dot_nt(q_e, k_e), 0.0)
        o = _dot(att.astype(BF16), v)
        st = st_ref[...]
        inter = []
        for j in range(GLA_GROUP // GLA_CHUNK):
            cr = slice(j * GLA_CHUNK, (j + 1) * GLA_CHUNK)
            inter.append(_dot_nt(q_e[cr], st.astype(BF16)))
            st = st * jnp.exp(b_end[j * GLA_CHUNK:j * GLA_CHUNK + 1, :]) + _dot_tn(v[cr], k_d[cr])
        st_ref[...] = st
        o = o + jnp.concatenate(inter, axis=0)
        o = o * lax.rsqrt(jnp.mean(o * o, axis=-1, keepdims=True) + LN_EPS) * gain
        r = r_ref[rows, :].astype(F32)
        o_ref[rows, :] = (o * (r * _sigmoid(r))).astype(BF16)
        return carry

    lax.fori_loop(0, SEQ // GLA_GROUP, group, 0, unroll=GLA_UNROLL)


def _gla(proj, w_gk2_pad, b_gk, norm_g):
    kb, vb = GLA_DK, GLA_DV
    return pl.pallas_call(
        _gla_body,
        grid=(BATCH, GLA_HEADS),
        in_specs=[
            pl.BlockSpec((SEQ, kb), lambda b, h: (b, COL_GQ // kb + h)),
            pl.BlockSpec((SEQ, kb), lambda b, h: (b, COL_GK // kb + h)),
            pl.BlockSpec((SEQ, vb), lambda b, h: (b, COL_GV // vb + h)),
            pl.BlockSpec((SEQ, vb), lambda b, h: (b, COL_GR // vb + h)),
            pl.BlockSpec((SEQ, LANES), lambda b, h: (b, COL_LOW // LANES)),
            pl.BlockSpec((LANES, kb), lambda b, h: (0, h)),
            pl.BlockSpec((1, kb), lambda b, h: (0, h)),
            pl.BlockSpec((1, vb), lambda b, h: (0, 0)),
        ],
        out_specs=pl.BlockSpec((SEQ, vb), lambda b, h: (b, h)),
        out_shape=jax.ShapeDtypeStruct((N_TOK, GLA_VDIM), BF16),
        scratch_shapes=[pltpu.VMEM((vb, kb), F32), pltpu.VMEM((SEQ, kb), F32)],
        compiler_params=_cparams("parallel", "parallel"),
        name="gla",
    )(proj, proj, proj, proj, proj, w_gk2_pad, b_gk, norm_g)


def _moba_body(q_ref, k_ref, v_ref, c_ref, s_ref, o_ref, qs_ref, ks_ref, vt_ref, bias_ref,
               sc_a_ref, sc_b_ref, pr_a_ref, pr_b_ref):
    cos_t = c_ref[0]
    sin_t = s_ref[0]
    lane = lax.broadcasted_iota(I32, (SEQ, MOBA_DH), 1)
    half = ROT_DIM // 2

    def rope(x):
        partner = jnp.where(lane < half, pltpu.roll(x, MOBA_DH - half, 1), pltpu.roll(x, half, 1))
        return x * cos_t + partner * sin_t

    q = rope(q_ref[...].astype(F32))
    k = rope(k_ref[...].astype(F32))
    q_hi, q_lo = _split_bf16(q)
    qs_ref[...] = (q * (MOBA_DH ** -0.5 * LOG2_E)).astype(BF16)
    ks_ref[...] = k.astype(BF16)
    vt_ref[...] = v_ref[...].astype(F32).T.astype(BF16)

    k_mean = jnp.concatenate(
        [jnp.mean(k[j * MOBA_BLOCK:(j + 1) * MOBA_BLOCK], axis=0, keepdims=True) for j in range(N_KBLK)], axis=0)
    m_hi, m_lo = _split_bf16(k_mean)
    s_blk = _dot_nt(m_hi, q_hi) + _dot_nt(m_hi, q_lo) + _dot_nt(m_lo, q_hi)
    blk = lax.broadcasted_iota(I32, (N_KBLK, SEQ), 0)
    q_blk = lax.shift_right_logical(lax.broadcasted_iota(I32, (N_KBLK, SEQ), 1), MOBA_BLOCK.bit_length() - 1)
    past = blk < q_blk
    s_blk = jnp.where(past, s_blk, _NEG_INF)
    beaten = jnp.zeros((N_KBLK, SEQ), I32)
    for j in range(N_KBLK):
        row = s_blk[j:j + 1, :]
        beaten += jnp.where((row > s_blk) | ((row == s_blk) & (j < blk)), 1, 0)
    bias_ref[...] = jnp.where(past & (beaten < MOBA_TOPK), 0.0, _NEG_INF)

    kr = lax.broadcasted_iota(I32, (MOBA_BLOCK, MOBA_BLOCK), 0)
    qc = lax.broadcasted_iota(I32, (MOBA_BLOCK, MOBA_BLOCK), 1)
    own_bias = jnp.where(kr <= qc, 0.0, _NEG_INF)

    sc_bufs = (sc_a_ref, sc_b_ref)
    pr_bufs = (pr_a_ref, pr_b_ref)
    zeros_row = jnp.zeros((1, MOBA_BLOCK), F32)
    future_row = jnp.full((1, MOBA_BLOCK), _NEG_INF, F32)
    for pair in range(N_KBLK // 2):
        q_blocks = (2 * pair, 2 * pair + 1)
        cols = slice(q_blocks[0] * MOBA_BLOCK, (q_blocks[1] + 1) * MOBA_BLOCK)
        n_kblk = q_blocks[1] + 1
        n_keys = n_kblk * MOBA_BLOCK
        sc, pr = sc_bufs[pair % 2], pr_bufs[pair % 2]
        q_pair = qs_ref[cols, :]

        def query_bias(j):
            halves = []
            for qb in q_blocks:
                q_cols = slice(qb * MOBA_BLOCK, (qb + 1) * MOBA_BLOCK)
                halves.append(bias_ref[j:j + 1, q_cols] if j < qb else zeros_row if j == qb else future_row)
            return jnp.concatenate(halves, axis=1)

        biases = [query_bias(j) for j in range(n_kblk)]
        col_max = []
        for j in range(n_kblk):
            rows = slice(j * MOBA_BLOCK, (j + 1) * MOBA_BLOCK)
            s = _dot_nt(ks_ref[rows, :], q_pair)
            if j in q_blocks:
                h = q_blocks.index(j)
                own = s[:, h * MOBA_BLOCK:(h + 1) * MOBA_BLOCK] + own_bias
                s = jnp.concatenate([own, s[:, MOBA_BLOCK:]] if h == 0 else [s[:, :MOBA_BLOCK], own], axis=1)
            sc[rows, :] = s
            col_max.append(jnp.max(s, axis=0, keepdims=True) + biases[j])
        m = functools.reduce(jnp.maximum, col_max)
        denom = jnp.zeros((1, 2 * MOBA_BLOCK), F32)
        for j in range(n_kblk):
            rows = slice(j * MOBA_BLOCK, (j + 1) * MOBA_BLOCK)
            p = jnp.exp2(sc[rows, :] - (m - biases[j]))
            denom = denom + jnp.sum(p, axis=0, keepdims=True)
            pr[rows, :] = p.astype(BF16)
        o_t = _dot(vt_ref[:, 0:n_keys], pr[0:n_keys, :]) * (1.0 / denom)
        o_ref[cols, :] = o_t.T.astype(BF16)


def _moba(proj, cos_t, sin_t):
    dh = MOBA_DH
    return pl.pallas_call(
        _moba_body,
        grid=(BATCH, MOBA_HEADS),
        in_specs=[
            pl.BlockSpec((SEQ, dh), lambda b, h: (b, COL_MQ // dh + h)),
            pl.BlockSpec((SEQ, dh), lambda b, h: (b, COL_MK // dh + h)),
            pl.BlockSpec((SEQ, dh), lambda b, h: (b, COL_MV // dh + h)),
            pl.BlockSpec((1, SEQ, dh), lambda b, h: (b, 0, 0)),
            pl.BlockSpec((1, SEQ, dh), lambda b, h: (b, 0, 0)),
        ],
        out_specs=pl.BlockSpec((SEQ, dh), lambda b, h: (b, h)),
        out_shape=jax.ShapeDtypeStruct((N_TOK, MOBA_DIM), BF16),
        scratch_shapes=[
            pltpu.VMEM((SEQ, dh), BF16),
            pltpu.VMEM((SEQ, dh), BF16),
            pltpu.VMEM((dh, SEQ), BF16),
            pltpu.VMEM((N_KBLK, SEQ), F32),
            pltpu.VMEM((SEQ, 2 * MOBA_BLOCK), F32),
            pltpu.VMEM((SEQ, 2 * MOBA_BLOCK), F32),
            pltpu.VMEM((SEQ, 2 * MOBA_BLOCK), BF16),
            pltpu.VMEM((SEQ, 2 * MOBA_BLOCK), BF16),
        ],
        compiler_params=_cparams("parallel", "parallel"),
        name="moba",
    )(proj, proj, proj, cos_t, sin_t)


def _layer_norm(z, g, b):
    mu = jnp.mean(z, axis=-1, keepdims=True)
    zc = z - mu
    var = jnp.mean(zc * zc, axis=-1, keepdims=True)
    return zc * lax.rsqrt(var + LN_EPS) * g + b


def _mix_body(gla_ref, moba_ref, ga_ref, gb_ref, x_ref, wgo_ref, wmo_ref, wo_ref, g_ref, b_ref,
              h_ref, hb_ref, hp_ref):
    y_gla = _dot(gla_ref[...], wgo_ref[...])
    y_moba = _dot(moba_ref[...], wmo_ref[...])
    merged = _sigmoid(ga_ref[...].astype(F32)) * y_gla + _sigmoid(gb_ref[...].astype(F32)) * y_moba
    mix = _dot(merged.astype(BF16), wo_ref[...])
    h = _layer_norm(ALPHA * x_ref[...] + mix, g_ref[...], b_ref[...])
    h_ref[...] = h
    hb_ref[...] = h.astype(BF16)
    packed = _pack_bf16_pair(h)
    for c in range(PACKED_CHUNKS):
        hp_ref[:, c, :] = packed[:, c * LANES:(c + 1) * LANES]


def _mix(gla_out, moba_out, proj, x2d, w_gla_o, w_moba_o, w_out, ln_g, ln_b):
    d = D_MODEL
    row = lambda i: (i, 0)
    full = lambda i: (0, 0)
    return pl.pallas_call(
        _mix_body,
        grid=(N_TOK // MIX_TM,),
        in_specs=[
            pl.BlockSpec((MIX_TM, d), row),
            pl.BlockSpec((MIX_TM, d), row),
            pl.BlockSpec((MIX_TM, d), lambda i: (i, COL_GA // d)),
            pl.BlockSpec((MIX_TM, d), lambda i: (i, COL_GB // d)),
            pl.BlockSpec((MIX_TM, d), row),
            pl.BlockSpec((d, d), full),
            pl.BlockSpec((d, d), full),
            pl.BlockSpec((d, d), full),
            pl.BlockSpec((1, d), full),
            pl.BlockSpec((1, d), full),
        ],
        out_specs=[
            pl.BlockSpec((MIX_TM, d), row),
            pl.BlockSpec((MIX_TM, d), row),
            pl.BlockSpec((MIX_TM, PACKED_CHUNKS, LANES), lambda i: (i, 0, 0)),
        ],
        out_shape=[
            jax.ShapeDtypeStruct((N_TOK, d), F32),
            jax.ShapeDtypeStruct((N_TOK, d), BF16),
            jax.ShapeDtypeStruct((N_TOK, PACKED_CHUNKS, LANES), U32),
        ],
        compiler_params=_cparams("parallel"),
        name="mix_ln1",
    )(gla_out, moba_out, proj, proj, x2d, w_gla_o, w_moba_o, w_out, ln_g, ln_b)


def _route_body(h_ref, wr_ref, br_ref, e_ref, w_ref, rk_ref, cnt_ref, carry_ref):
    tm = ROUTE_TM

    @pl.when(pl.program_id(0) == 0)
    def _():
        carry_ref[...] = jnp.zeros_like(carry_ref)

    scores = _sigmoid(_dot_nt(wr_ref[...], h_ref[...]))
    biased = scores + br_ref[...]
    row = lax.broadcasted_iota(I32, (N_EXPERTS, tm), 0).astype(F32)
    row_g = lax.broadcasted_iota(I32, (GROUP_SIZE, tm), 0).astype(F32)

    g_scores = []
    for g in range(N_GROUPS):
        grp = biased[g * GROUP_SIZE:(g + 1) * GROUP_SIZE]
        m1 = jnp.max(grp, axis=0, keepdims=True)
        first = jnp.min(jnp.where(grp == m1, row_g, float(GROUP_SIZE)), axis=0, keepdims=True)
        m2 = jnp.max(jnp.where(row_g == first, _NEG_INF, grp), axis=0, keepdims=True)
        g_scores.append(m1 + m2)
    g_score = jnp.concatenate(g_scores, axis=0)
    g_row = lax.broadcasted_iota(I32, (N_GROUPS, tm), 0)
    g_beaten = jnp.zeros((N_GROUPS, tm), I32)
    for g in range(N_GROUPS):
        r = g_score[g:g + 1, :]
        g_beaten += jnp.where((r > g_score) | ((r == g_score) & (g < g_row)), 1, 0)
    g_keep = g_beaten < TOPK_GROUPS
    masked = jnp.concatenate(
        [jnp.where(g_keep[g:g + 1, :], biased[g * GROUP_SIZE:(g + 1) * GROUP_SIZE], _NEG_INF)
         for g in range(N_GROUPS)], axis=0)

    onehot = jnp.zeros((N_EXPERTS, tm), F32)
    picks, pick_scores = [], []
    for _ in range(TOPK_EXPERTS):
        m = jnp.max(masked, axis=0, keepdims=True)
        idx = jnp.min(jnp.where(masked == m, row, float(N_EXPERTS)), axis=0, keepdims=True)
        hit = row == idx
        picks.append(idx)
        pick_scores.append(jnp.sum(jnp.where(hit, scores, 0.0), axis=0, keepdims=True))
        onehot = onehot + jnp.where(hit, 1.0, 0.0)
        masked = jnp.where(hit, _NEG_INF, masked)
    sel = jnp.concatenate(pick_scores, axis=0)
    e_ref[...] = jnp.concatenate(picks, axis=0).astype(I32)
    w_ref[...] = sel / jnp.sum(sel, axis=0, keepdims=True) * ROUTED_SCALE

    t_r = lax.broadcasted_iota(I32, (tm, tm), 0)
    t_c = lax.broadcasted_iota(I32, (tm, tm), 1)
    earlier = jnp.where(t_r < t_c, 1.0, 0.0).astype(BF16)
    seen = _dot(onehot.astype(BF16), earlier) + carry_ref[...]
    rk_ref[...] = jnp.concatenate(
        [jnp.sum(jnp.where(row == idx, seen, 0.0), axis=0, keepdims=True) for idx in picks], axis=0).astype(I32)
    carry_ref[...] += jnp.sum(onehot, axis=1, keepdims=True)
    cnt_ref[...] = carry_ref[...]


def _route(h_bf, w_router_t, b_router_col):
    k = TOPK_EXPERTS
    tok = lambda i: (0, i)
    return pl.pallas_call(
        _route_body,
        grid=(N_TOK // ROUTE_TM,),
        in_specs=[
            pl.BlockSpec((ROUTE_TM, D_MODEL), lambda i: (i, 0)),
            pl.BlockSpec((N_EXPERTS, D_MODEL), lambda i: (0, 0)),
            pl.BlockSpec((N_EXPERTS, 1), lambda i: (0, 0)),
        ],
        out_specs=[
            pl.BlockSpec((k, ROUTE_TM), tok),
            pl.BlockSpec((k, ROUTE_TM), tok),
            pl.BlockSpec((k, ROUTE_TM), tok),
            pl.BlockSpec((N_EXPERTS, 1), lambda i: (0, 0)),
        ],
        out_shape=[
            jax.ShapeDtypeStruct((k, N_TOK), I32),
            jax.ShapeDtypeStruct((k, N_TOK), F32),
            jax.ShapeDtypeStruct((k, N_TOK), I32),
            jax.ShapeDtypeStruct((N_EXPERTS, 1), F32),
        ],
        scratch_shapes=[pltpu.VMEM((N_EXPERTS, 1), F32)],
        compiler_params=_cparams("arbitrary"),
        name="route",
    )(h_bf, w_router_t, b_router_col)


def _dest_body(cnt_ref, e_ref, rk_ref, d_ref, te_ref, nu_ref, lt_ref):
    cnt = cnt_ref[...]
    tiles = jnp.floor((cnt + (ROW_TILE - 1)) * (1.0 / ROW_TILE))
    er = lax.broadcasted_iota(I32, (N_EXPERTS, N_EXPERTS), 0)
    ec = lax.broadcasted_iota(I32, (N_EXPERTS, N_EXPERTS), 1)
    before = jnp.where(ec < er, 1.0, 0.0).astype(BF16)
    tiles_b = jnp.broadcast_to(tiles, (N_EXPERTS, LANES)).astype(BF16)
    t_start = _dot(before, tiles_b)[:, 0:1]
    t_end = t_start + tiles
    p_start = t_start * float(ROW_TILE)
    lt_ref[...] = jnp.where(tiles > 0.0, (t_end - 1.0) * float(ROW_TILE), -1.0).astype(I32)

    row = lax.broadcasted_iota(I32, (N_EXPERTS, DEST_TM), 0)
    d_ref[...] = jnp.concatenate(
        [jnp.sum(jnp.where(row == e_ref[k:k + 1, :], p_start, 0.0), axis=0, keepdims=True)
         for k in range(TOPK_EXPERTS)], axis=0).astype(I32) + rk_ref[...]

    tile_id = lax.broadcasted_iota(I32, (N_EXPERTS, TILE_TABLE), 1).astype(F32)
    owner = jnp.sum(jnp.where(t_end <= tile_id, 1, 0), axis=0, keepdims=True)
    te_ref[...] = jnp.minimum(owner, N_EXPERTS - 1)
    nu_ref[...] = jnp.broadcast_to(t_end[N_EXPERTS - 1:N_EXPERTS, :], (1, LANES)).astype(I32)


def _dest(cnt, e_t, rk_t):
    k = TOPK_EXPERTS
    tok = lambda i: (0, i)
    return pl.pallas_call(
        _dest_body,
        grid=(N_TOK // DEST_TM,),
        in_specs=[
            pl.BlockSpec((N_EXPERTS, 1), lambda i: (0, 0)),
            pl.BlockSpec((k, DEST_TM), tok),
            pl.BlockSpec((k, DEST_TM), tok),
        ],
        out_specs=[
            pl.BlockSpec((k, DEST_TM), tok),
            pl.BlockSpec((1, TILE_TABLE), lambda i: (0, 0)),
            pl.BlockSpec((1, LANES), lambda i: (0, 0)),
            pl.BlockSpec((N_EXPERTS, 1), lambda i: (0, 0)),
        ],
        out_shape=[
            jax.ShapeDtypeStruct((k, N_TOK), I32),
            jax.ShapeDtypeStruct((1, TILE_TABLE), I32),
            jax.ShapeDtypeStruct((1, LANES), I32),
            jax.ShapeDtypeStruct((N_EXPERTS, 1), I32),
        ],
        compiler_params=_cparams("arbitrary"),
        name="dest",
    )(cnt, e_t, rk_t)


def _row_copy(src, dst, sem):
    return pltpu.make_async_copy(src, dst, sem)


def _dispatch_body(lt_ref, nu_ref, d_ref, h_ref, xs_ref, zero_ref, sem):
    @pl.when(pl.program_id(0) == 0)
    def _():
        zero_ref[...] = jnp.zeros_like(zero_ref)

        def zero_tile(first_row):
            r0 = pl.multiple_of(first_row, ROW_TILE)
            return pltpu.make_async_copy(zero_ref, xs_ref.at[pl.ds(r0, ROW_TILE)], sem)

        def z_start(e, carry):
            @pl.when(lt_ref[e] >= 0)
            def _():
                zero_tile(jnp.maximum(lt_ref[e], 0)).start()
            return carry

        def z_wait(e, carry):
            @pl.when(lt_ref[e] >= 0)
            def _():
                zero_tile(jnp.maximum(lt_ref[e], 0)).wait()
            return carry

        def t_start(t, carry):
            zero_tile(t * ROW_TILE).start()
            return carry

        def t_wait(t, carry):
            zero_tile(t * ROW_TILE).wait()
            return carry

        lax.fori_loop(0, N_EXPERTS, z_start, 0)
        lax.fori_loop(nu_ref[0], N_ROW_TILES, t_start, 0)
        lax.fori_loop(0, N_EXPERTS, z_wait, 0)
        lax.fori_loop(nu_ref[0], N_ROW_TILES, t_wait, 0)

    for t in range(DISP_TM):
        for k in range(TOPK_EXPERTS):
            _row_copy(h_ref.at[t], xs_ref.at[d_ref[k, t]], sem).start(priority=k % 2)
    for k in range(TOPK_EXPERTS):
        pltpu.make_async_copy(h_ref, xs_ref.at[pl.ds(0, DISP_TM)], sem).wait()


def _dispatch(last_tile_row, n_used, dest_t, h1):
    grid_spec = pltpu.PrefetchScalarGridSpec(
        num_scalar_prefetch=2,
        grid=(N_TOK // DISP_TM,),
        in_specs=[
            pl.BlockSpec((TOPK_EXPERTS, DISP_TM), lambda i, lt, nu: (0, i), memory_space=pltpu.SMEM),
            pl.BlockSpec((DISP_TM, PACKED_CHUNKS, LANES), lambda i, lt, nu: (i, 0, 0)),
        ],
        out_specs=pl.BlockSpec(memory_space=pl.ANY),
        scratch_shapes=[pltpu.VMEM((ROW_TILE, PACKED_CHUNKS, LANES), U32), pltpu.SemaphoreType.DMA],
    )
    return pl.pallas_call(
        _dispatch_body,
        grid_spec=grid_spec,
        out_shape=jax.ShapeDtypeStruct((N_ROWS, PACKED_CHUNKS, LANES), U32),
        compiler_params=_cparams("arbitrary"),
        name="dispatch",
    )(last_tile_row, n_used, dest_t, h1)


def _row_destinations(dest_flat):
    info = plsc.get_sparse_core_info()
    n_cores, lanes = info.num_cores, info.num_lanes
    n_workers = n_cores * info.num_subcores
    rows_per_worker = -(-N_ROWS // (n_workers * ROW_TILE)) * ROW_TILE
    assert rows_per_worker % lanes == 0
    mesh = plsc.VectorSubcoreMesh(core_axis_name="c", subcore_axis_name="s")

    @functools.partial(
        pl.kernel, mesh=mesh, out_type=jax.ShapeDtypeStruct((n_workers * rows_per_worker,), I32),
        scratch_types=[pltpu.VMEM((SLOT_CHUNK,), I32), pltpu.VMEM((rows_per_worker,), I32)],
        compiler_params=pltpu.CompilerParams(needs_layout_passes=False), name="row_destinations")
    def invert(dest_hbm, out_hbm, dest_v, table_v):
        first_row = (lax.axis_index("s") * n_cores + lax.axis_index("c")) * rows_per_worker
        lane = lax.iota(I32, lanes)
        tile_shift = ROW_TILE.bit_length() - 1

        def spare(i, carry):
            row = first_row + i * lanes + lane
            buf = lax.rem(lax.shift_right_logical(row, tile_shift), EXPERT_BUFS)
            table_v[pl.ds(i * lanes, lanes)] = TOPK_EXPERTS * N_TOK + buf * ROW_TILE + (row & (ROW_TILE - 1))
            return carry

        lax.fori_loop(0, rows_per_worker // lanes, spare, 0)

        def chunk(c, carry):
            pltpu.sync_copy(dest_hbm.at[pl.ds(c * SLOT_CHUNK, SLOT_CHUNK)], dest_v)

            def vec(i, inner):
                local = dest_v[pl.ds(i * lanes, lanes)] - first_row
                mine = (local >= 0) & (local < rows_per_worker)
                plsc.store_scatter(table_v, [jnp.where(mine, local, 0)], lane + (c * SLOT_CHUNK + i * lanes),
                                   mask=mine)
                return inner

            lax.fori_loop(0, SLOT_CHUNK // lanes, vec, 0)
            return carry

        lax.fori_loop(0, N_ASSIGN // SLOT_CHUNK, chunk, 0)
        pltpu.sync_copy(table_v, out_hbm.at[pl.ds(first_row, rows_per_worker)])

    return invert(dest_flat)


def _experts_body(te_ref, nu_ref, lt_ref, slot_ref, xs_ref, wg_ref, wu_ref, wd_ref, yk_ref,
                  wg_f, wu_f, wd_f, wg_b, wu_b, wd_b, y_buf, n_loaded, sem, w_sem):
    i = pl.program_id(0)
    n_used = nu_ref[0]

    def weight_copies(e, s):
        return (pltpu.make_async_copy(wg_ref.at[e], wg_f.at[s], w_sem.at[s]),
                pltpu.make_async_copy(wu_ref.at[e], wu_f.at[s], w_sem.at[s]),
                pltpu.make_async_copy(wd_ref.at[e], wd_f.at[s], w_sem.at[s]))

    def wait_tile(b):
        pltpu.make_async_copy(y_buf.at[b], yk_ref.at[pl.ds(0, ROW_TILE), :], sem.at[b]).wait()

    def send_tile(b):
        for r in range(ROW_TILE):
            dst = slot_ref[0, 0, r]
            _row_copy(y_buf.at[b, pl.ds(r, 1), :], yk_ref.at[pl.ds(dst, 1), :], sem.at[b]).start(priority=r % 2)

    @pl.when(i == 0)
    def _():
        y_buf[EXPERT_BUFS - 1] = jnp.zeros((ROW_TILE, PACKED_COLS), U32)
        n_loaded[0] = 0
        for c in weight_copies(te_ref[0], 0):
            c.start()

    @pl.when((i >= 2) & (i <= n_used))
    def _():
        wait_tile(lax.rem(i, EXPERT_BUFS))

    @pl.when(i < n_used)
    def _():
        e = te_ref[i]
        prev = te_ref[jnp.maximum(i - 1, 0)]

        @pl.when((i == 0) | (e != prev))
        def _():
            s = lax.rem(n_loaded[0], 2)
            for c in weight_copies(e, s):
                c.wait()
            wg_b[...] = wg_f[s].astype(BF16)
            wu_b[...] = wu_f[s].astype(BF16)
            wd_b[...] = wd_f[s].astype(BF16)
            n_loaded[0] = n_loaded[0] + 1
            nxt = lax.shift_right_logical(lt_ref[e], ROW_TILE.bit_length() - 1) + 1

            @pl.when(nxt < n_used)
            def _():
                for c in weight_copies(te_ref[nxt], 1 - s):
                    c.start()

        for phase in range(EXPERT_BUFS):
            @pl.when(lax.rem(i, EXPERT_BUFS) == phase)
            def _(phase=phase):
                send_tile((phase + EXPERT_BUFS - 1) % EXPERT_BUFS)
                packed = jnp.concatenate([xs_ref[:, c, :] for c in range(PACKED_CHUNKS)], axis=1)
                x_lo, x_hi = _unpack_bf16_pair(packed)
                x = jnp.concatenate([x_lo.astype(BF16), x_hi.astype(BF16)], axis=1)
                g = _dot(x, wg_b[...])
                u = _dot(x, wu_b[...])
                h = (g * _sigmoid(g)) * u
                y_buf[phase] = _pack_bf16_pair(_dot(h.astype(BF16), wd_b[...]))

    @pl.when(i == n_used)
    def _():
        send_tile(lax.rem(i + EXPERT_BUFS - 1, EXPERT_BUFS))
        wait_tile(lax.rem(i + EXPERT_BUFS - 2, EXPERT_BUFS))
        wait_tile(lax.rem(i + EXPERT_BUFS - 1, EXPERT_BUFS))


def _experts(tile_expert, n_used, last_tile_row, row_dst, xs, w_gate, w_up, w_down):
    def tile(i, te, nu, lt):
        return (jnp.minimum(i, nu[0] - 1), 0, 0)

    n_table_tiles = row_dst.shape[0] // ROW_TILE
    placeholder = n_table_tiles - 1
    assert placeholder * ROW_TILE >= N_ROWS and placeholder % EXPERT_BUFS == EXPERT_BUFS - 1

    def prev_tile(i, te, nu, lt):
        return (jnp.where(i == 0, placeholder, jnp.minimum(i, nu[0]) - 1), 0, 0)

    grid_spec = pltpu.PrefetchScalarGridSpec(
        num_scalar_prefetch=3,
        grid=(N_ROW_TILES + 1,),
        in_specs=[
            pl.BlockSpec((1, 1, ROW_TILE), prev_tile, memory_space=pltpu.SMEM),
            pl.BlockSpec((ROW_TILE, PACKED_CHUNKS, LANES), tile),
            pl.BlockSpec(memory_space=pl.ANY),
            pl.BlockSpec(memory_space=pl.ANY),
            pl.BlockSpec(memory_space=pl.ANY),
        ],
        out_specs=pl.BlockSpec(memory_space=pl.ANY),
        scratch_shapes=[
            pltpu.VMEM((2, D_MODEL, D_EXPERT), F32),
            pltpu.VMEM((2, D_MODEL, D_EXPERT), F32),
            pltpu.VMEM((2, D_EXPERT, D_MODEL), F32),
            pltpu.VMEM((D_MODEL, D_EXPERT), BF16),
            pltpu.VMEM((D_MODEL, D_EXPERT), BF16),
            pltpu.VMEM((D_EXPERT, D_MODEL), BF16),
            pltpu.VMEM((EXPERT_BUFS, ROW_TILE, PACKED_COLS), U32),
            pltpu.SMEM((1,), I32),
            pltpu.SemaphoreType.DMA((EXPERT_BUFS,)),
            pltpu.SemaphoreType.DMA((2,)),
        ],
    )
    return pl.pallas_call(
        _experts_body,
        grid_spec=grid_spec,
        out_shape=jax.ShapeDtypeStruct((YK_ROWS, PACKED_COLS), U32),
        compiler_params=_cparams("arbitrary"),
        name="experts",
    )(tile_expert, n_used, last_tile_row, row_dst.reshape(n_table_tiles, 1, ROW_TILE), xs, w_gate, w_up, w_down)


def _final_body(w_ref, yk_ref, h_ref, hb_ref, p_ref, wsg_ref, wsu_ref, wsd_ref, wpl_ref, wpg_ref,
                g_ref, b_ref, o_ref):
    hb = hb_ref[...]
    sg = _dot(hb, wsg_ref[...])
    shared = _dot(((sg * _sigmoid(sg)) * _dot(hb, wsu_ref[...])).astype(BF16), wsd_ref[...])
    ple = _sigmoid(_dot(hb, wpg_ref[...])) * _dot(p_ref[...].astype(BF16), wpl_ref[...])

    w_col = w_ref[...].T
    y_lo, y_hi = _unpack_bf16_pair(yk_ref[0])
    r_lo, r_hi = y_lo * w_col[:, 0:1], y_hi * w_col[:, 0:1]
    for k in range(1, TOPK_EXPERTS):
        y_lo, y_hi = _unpack_bf16_pair(yk_ref[k])
        r_lo, r_hi = r_lo + y_lo * w_col[:, k:k + 1], r_hi + y_hi * w_col[:, k:k + 1]
    routed = jnp.concatenate([r_lo, r_hi], axis=1)
    o_ref[...] = _layer_norm(ALPHA * h_ref[...] + (routed + shared) + ple, g_ref[...], b_ref[...])


def _final(w_t, yk, h1, h_bf, p2d, w_s_gate, w_s_up, w_s_down, w_ple, w_ple_gate, ln_g, ln_b):
    d = D_MODEL
    full = lambda i: (0, 0)
    return pl.pallas_call(
        _final_body,
        grid=(N_TOK // FINAL_TM,),
        in_specs=[
            pl.BlockSpec((TOPK_EXPERTS, FINAL_TM), lambda i: (0, i)),
            pl.BlockSpec((TOPK_EXPERTS, FINAL_TM, PACKED_COLS), lambda i: (0, i, 0)),
            pl.BlockSpec((FINAL_TM, d), lambda i: (i, 0)),
            pl.BlockSpec((FINAL_TM, d), lambda i: (i, 0)),
            pl.BlockSpec((FINAL_TM, PLE_DIM), lambda i: (i, 0)),
            pl.BlockSpec((d, D_SHARED), full),
            pl.BlockSpec((d, D_SHARED), full),
            pl.BlockSpec((D_SHARED, d), full),
            pl.BlockSpec((PLE_DIM, d), full),
            pl.BlockSpec((d, d), full),
            pl.BlockSpec((1, d), full),
            pl.BlockSpec((1, d), full),
        ],
        out_specs=pl.BlockSpec((FINAL_TM, d), lambda i: (i, 0)),
        out_shape=jax.ShapeDtypeStruct((N_TOK, d), F32),
        compiler_params=_cparams("parallel"),
        name="combine_ln2",
    )(w_t, yk.reshape(TOPK_EXPERTS + 1, N_TOK, PACKED_COLS), h1, h_bf, p2d, w_s_gate, w_s_up, w_s_down,
      w_ple, w_ple_gate, ln_g, ln_b)


def _rope_tables(positions):
    half = ROT_DIM // 2
    inv = ROPE_THETA ** (-jnp.arange(0, ROT_DIM, 2, dtype=F32) / ROT_DIM)
    per_row = LANES // half
    pos = jnp.repeat(positions.reshape(-1, per_row), half, axis=1).astype(F32)
    ang = pos * jnp.tile(inv, per_row)[None, :]
    cos, sin = lax.optimization_barrier((jnp.cos(ang), jnp.sin(ang)))
    cos = cos.reshape(positions.shape + (half,))
    sin = sin.reshape(positions.shape + (half,))
    rest = MOBA_DH - ROT_DIM
    ones = jnp.ones(cos.shape[:-1] + (rest,), F32)
    zeros = jnp.zeros(cos.shape[:-1] + (rest,), F32)
    return (jnp.concatenate([cos, cos, ones], axis=-1), jnp.concatenate([-sin, sin, zeros], axis=-1))


def _layer(h2d, p2d, cos_t, sin_t, w_cat, w_gk2, b_gk, norm_g, w_gla_o, w_moba_o, w_out, ln1_g, ln1_b,
           w_router, b_router, w_e_gate, w_e_up, w_e_down, w_s_gate, w_s_up, w_s_down, w_ple, w_ple_gate,
           ln2_g, ln2_b):
    w_gk2_pad = jnp.concatenate([w_gk2, jnp.zeros((LANES - GLA_RANK, GLA_KDIM), w_gk2.dtype)], axis=0)

    proj = _in_proj(h2d, w_cat)
    gla_out = _gla(proj, w_gk2_pad, b_gk[None, :], norm_g[None, :])
    moba_out = _moba(proj, cos_t, sin_t)
    h1, h_bf, h_packed = _mix(gla_out, moba_out, proj, h2d, w_gla_o.astype(BF16), w_moba_o.astype(BF16),
                              w_out.astype(BF16), ln1_g[None, :], ln1_b[None, :])
    e_t, w_t, rk_t, cnt = _route(h_bf, w_router.T.astype(BF16), b_router[:, None])
    dest_t, tile_expert, n_used, last_tile_row = _dest(cnt, e_t, rk_t)
    last_tile_row = last_tile_row.reshape(N_EXPERTS)
    n_used = n_used[0, 0:1]
    xs = _dispatch(last_tile_row, n_used, dest_t, h_packed)
    row_dst = _row_destinations(dest_t.reshape(N_ASSIGN))
    yk = _experts(tile_expert.reshape(TILE_TABLE), n_used, last_tile_row, row_dst, xs,
                  w_e_gate, w_e_up, w_e_down)
    return _final(w_t, yk, h1, h_bf, p2d, w_s_gate.astype(BF16), w_s_up.astype(BF16),
                  w_s_down.astype(BF16), w_ple.astype(BF16), w_ple_gate.astype(BF16),
                  ln2_g[None, :], ln2_b[None, :])


def kernel(x, p, positions, w_in, w_gla_gk2, b_gla_gk, gla_norm_g, w_gla_o, w_moba_o, w_out, ln1_g, ln1_b,
           w_router, b_router, w_e_gate, w_e_up, w_e_down, w_s_gate, w_s_up, w_s_down, w_ple, w_ple_gate,
           ln2_g, ln2_b):
    cos_t, sin_t = _rope_tables(positions)
    h = x.reshape(N_TOK, D_MODEL)
    for i in range(DEPTH):
        h = _layer(h, p[i].reshape(N_TOK, PLE_DIM), cos_t, sin_t, _repack_w_in(w_in, i), w_gla_gk2[i], b_gla_gk[i],
                   gla_norm_g[i], w_gla_o[i], w_moba_o[i], w_out[i], ln1_g[i], ln1_b[i], w_router[i],
                   b_router[i], w_e_gate[i], w_e_up[i], w_e_down[i], w_s_gate[i], w_s_up[i], w_s_down[i],
                   w_ple[i], w_ple_gate[i], ln2_g[i], ln2_b[i])
    return h.reshape(BATCH, SEQ, D_MODEL)
```

```python
import functools

import jax
import jax.numpy as jnp
from jax import lax
from jax.experimental import pallas as pl
from jax.experimental.pallas import tpu as pltpu
from jax.experimental.pallas import tpu_sc as plsc

F32 = jnp.float32
BF16 = jnp.bfloat16
I32 = jnp.int32
U32 = jnp.uint32

LANES = 128
SUBLANES = 8
VMEM_LIMIT_BYTES = 48 * 1024 * 1024

D_MODEL = 1024
BATCH = 8
SEQ = 2048
N_TOK = BATCH * SEQ
GLA_HEADS = 4
GLA_DK = 128
GLA_DV = 256
GLA_RANK = 16
GLA_NORMALIZER = 16.0
GLA_CHUNK = 64
GLA_GROUP = 4 * GLA_CHUNK
GLA_UNROLL = 4
MOBA_HEADS = 8
MOBA_DH = 128
MOBA_BLOCK = 256
MOBA_TOPK = 3
ROT_DIM = 32
ROPE_THETA = 500000.0
N_EXPERTS = 256
TOPK_EXPERTS = 8
N_GROUPS = 8
GROUP_SIZE = N_EXPERTS // N_GROUPS
TOPK_GROUPS = 4
D_EXPERT = 256
D_SHARED = 256
ROUTED_SCALE = 2.5
PLE_DIM = 256
LN_EPS = 1e-5
DEPTH = 1
ALPHA = (2.0 * DEPTH) ** 0.25
GLA_KDIM = GLA_HEADS * GLA_DK
GLA_VDIM = GLA_HEADS * GLA_DV
MOBA_DIM = MOBA_HEADS * MOBA_DH
N_KBLK = SEQ // MOBA_BLOCK

COL_GQ = 0
COL_GK = COL_GQ + GLA_KDIM
COL_GV = COL_GK + GLA_KDIM
COL_GR = COL_GV + GLA_VDIM
COL_MQ = COL_GR + GLA_VDIM
COL_MK = COL_MQ + MOBA_DIM
COL_MV = COL_MK + MOBA_DIM
COL_GA = COL_MV + MOBA_DIM
COL_GB = COL_GA + D_MODEL
COL_LOW = COL_GB + D_MODEL
PROJ_COLS = COL_LOW + LANES
LOW_SRC = 2 * GLA_KDIM + 2 * GLA_VDIM

ROW_TILE = 256
PACKED_COLS = D_MODEL // 2
PACKED_CHUNKS = PACKED_COLS // LANES
EXPERT_BUFS = 3
YK_ROWS = TOPK_EXPERTS * N_TOK + EXPERT_BUFS * ROW_TILE
SLOT_CHUNK = 4096
assert EXPERT_BUFS * ROW_TILE <= N_TOK
N_ASSIGN = N_TOK * TOPK_EXPERTS
N_ROW_TILES = (N_ASSIGN + N_EXPERTS * (ROW_TILE - 1) + ROW_TILE - 1) // ROW_TILE
N_ROWS = N_ROW_TILES * ROW_TILE
TILE_TABLE = -(-N_ROW_TILES // LANES) * LANES

PROJ_TM = 1024
REPACK_TN = 512
PROJ_TN = PROJ_COLS // 5
MIX_TM = 512
ROUTE_TM = 512
DEST_TM = 2048
DISP_TM = 256
FINAL_TM = 256

_NEG_INF = float("-inf")
LOG2_E = 1.4426950408889634


def _cparams(*sem):
    return pltpu.CompilerParams(dimension_semantics=sem, vmem_limit_bytes=VMEM_LIMIT_BYTES)


def _dot(a, b):
    return jnp.dot(a, b, preferred_element_type=F32)


def _dot_nt(a, b):
    return lax.dot_general(a, b, (((1,), (1,)), ((), ())), preferred_element_type=F32)


def _dot_tn(a, b):
    return lax.dot_general(a, b, (((0,), (0,)), ((), ())), preferred_element_type=F32)


def _split_bf16(x):
    hi = x.astype(BF16)
    lo = (x - hi.astype(F32)).astype(BF16)
    return hi, lo


def _sigmoid(x):
    return 1.0 / (1.0 + jnp.exp(-x))


def _pack_bf16_pair(x):
    lo = lax.bitcast_convert_type(x[:, :PACKED_COLS].astype(BF16).astype(F32), U32)
    hi = lax.bitcast_convert_type(x[:, PACKED_COLS:].astype(BF16).astype(F32), U32)
    return hi | lax.shift_right_logical(lo, jnp.uint32(16))


def _unpack_bf16_pair(w):
    lo = lax.bitcast_convert_type(lax.shift_left(w, jnp.uint32(16)), F32)
    hi = lax.bitcast_convert_type(w & jnp.uint32(0xFFFF0000), F32)
    return lo, hi


def _repack_body(a_ref, b_ref, o_ref):
    t = pl.program_id(0)
    a = a_ref[...].astype(BF16)

    @pl.when(t < LOW_SRC // REPACK_TN)
    def _():
        o_ref[...] = a

    @pl.when((t >= LOW_SRC // REPACK_TN) & (t < COL_LOW // REPACK_TN))
    def _():
        o_ref[...] = jnp.concatenate([a[:, GLA_RANK:], b_ref[:, :GLA_RANK].astype(BF16)], axis=1)

    @pl.when(t == COL_LOW // REPACK_TN)
    def _():
        o_ref[...] = jnp.concatenate([a[:, :GLA_RANK], jnp.zeros((D_MODEL, REPACK_TN - GLA_RANK), BF16)], axis=1)


def _repack_w_in(w_in, layer):
    assert LOW_SRC % REPACK_TN == 0 and COL_LOW % REPACK_TN == 0
    low_tile = LOW_SRC // REPACK_TN
    last_out = COL_LOW // REPACK_TN
    lanes_per_tile = REPACK_TN // LANES
    return pl.pallas_call(
        _repack_body,
        grid=(last_out + 1,),
        in_specs=[
            pl.BlockSpec((None, D_MODEL, REPACK_TN), lambda t: (layer, 0, jnp.where(t == last_out, low_tile, t))),
            pl.BlockSpec((None, D_MODEL, LANES), lambda t: (layer, 0, jnp.minimum(t + 1, last_out) * lanes_per_tile)),
        ],
        out_specs=pl.BlockSpec((D_MODEL, REPACK_TN), lambda t: (0, t)),
        out_shape=jax.ShapeDtypeStruct((D_MODEL, PROJ_COLS), BF16),
        compiler_params=_cparams("parallel"),
        name="repack_w_in",
    )(w_in, w_in)


def _proj_body(x_ref, w_ref, o_ref):
    o_ref[...] = _dot(x_ref[...].astype(BF16), w_ref[...]).astype(BF16)


def _in_proj(x2d, w_cat):
    return pl.pallas_call(
        _proj_body,
        grid=(PROJ_COLS // PROJ_TN, N_TOK // PROJ_TM),
        in_specs=[
            pl.BlockSpec((PROJ_TM, D_MODEL), lambda j, i: (i, 0)),
            pl.BlockSpec((D_MODEL, PROJ_TN), lambda j, i: (0, j)),
        ],
        out_specs=pl.BlockSpec((PROJ_TM, PROJ_TN), lambda j, i: (i, j)),
        out_shape=jax.ShapeDtypeStruct((N_TOK, PROJ_COLS), BF16),
        compiler_params=_cparams("parallel", "parallel"),
        name="in_proj",
    )(x2d, w_cat)


def _gla_body(q_ref, k_ref, v_ref, r_ref, low_ref, wg_ref, bg_ref, ng_ref, o_ref, st_ref, gk_ref):
    w_hi, w_lo = _split_bf16(wg_ref[...])
    low = low_ref[...]
    lin = _dot(low, w_hi) + _dot(low, w_lo) + bg_ref[...]
    gk_ref[...] = (jnp.minimum(lin, 0.0) - jnp.log1p(jnp.exp(-jnp.abs(lin)))) * (1.0 / GLA_NORMALIZER)
    st_ref[...] = jnp.zeros_like(st_ref)

    ri = lax.broadcasted_iota(I32, (GLA_GROUP, GLA_GROUP), 0)
    ci = lax.broadcasted_iota(I32, (GLA_GROUP, GLA_GROUP), 1)
    same_chunk = lax.shift_right_logical(ri, GLA_CHUNK.bit_length() - 1) == lax.shift_right_logical(
        ci, GLA_CHUNK.bit_length() - 1)
    causal = same_chunk & (ri >= ci)
    sums = jnp.concatenate([jnp.where(causal, 1.0, 0.0), jnp.where(same_chunk, 1.0, 0.0)], axis=0).astype(BF16)
    gain = ng_ref[...]

    def group(c, carry):
        rows = pl.ds(pl.multiple_of(c * GLA_GROUP, GLA_GROUP), GLA_GROUP)
        g_hi, g_lo = _split_bf16(gk_ref[rows, :])
        bb = _dot(sums, g_hi) + _dot(sums, g_lo)
        b = bb[0:GLA_GROUP]
        b_end = bb[GLA_GROUP:2 * GLA_GROUP]
        q = q_ref[rows, :].astype(F32) * (GLA_DK ** -0.5)
        k = k_ref[rows, :].astype(F32)
        v = v_ref[rows, :]
        q_e = (q * jnp.exp(b)).astype(BF16)
        k_e = (k * jnp.exp(-b)).astype(BF16)
        k_d = (k * jnp.exp(b_end - b)).astype(BF16)
        att = jnp.where(causal, _dot_nt(q_e, k_e), 0.0)
        o = _dot(att.astype(BF16), v)
        st = st_ref[...]
        inter = []
        for j in range(GLA_GROUP // GLA_CHUNK):
            cr = slice(j * GLA_CHUNK, (j + 1) * GLA_CHUNK)
            inter.append(_dot_nt(q_e[cr], st.astype(BF16)))
            st = st * jnp.exp(b_end[j * GLA_CHUNK:j * GLA_CHUNK + 1, :]) + _dot_tn(v[cr], k_d[cr])
        st_ref[...] = st
        o = o + jnp.concatenate(inter, axis=0)
        o = o * lax.rsqrt(jnp.mean(o * o, axis=-1, keepdims=True) + LN_EPS) * gain
        r = r_ref[rows, :].astype(F32)
        o_ref[rows, :] = (o * (r * _sigmoid(r))).astype(BF16)
        return carry

    lax.fori_loop(0, SEQ // GLA_GROUP, group, 0, unroll=GLA_UNROLL)


def _gla(proj, w_gk2_pad, b_gk, norm_g):
    kb, vb = GLA_DK, GLA_DV
    return pl.pallas_call(
        _gla_body,
        grid=(BATCH, GLA_HEADS),
        in_specs=[
            pl.BlockSpec((SEQ, kb), lambda b, h: (b, COL_GQ // kb + h)),
            pl.BlockSpec((SEQ, kb), lambda b, h: (b, COL_GK // kb + h)),
            pl.BlockSpec((SEQ, vb), lambda b, h: (b, COL_GV // vb + h)),
            pl.BlockSpec((SEQ, vb), lambda b, h: (b, COL_GR // vb + h)),
            pl.BlockSpec((SEQ, LANES), lambda b, h: (b, COL_LOW // LANES)),
            pl.BlockSpec((LANES, kb), lambda b, h: (0, h)),
            pl.BlockSpec((1, kb), lambda b, h: (0, h)),
            pl.BlockSpec((1, vb), lambda b, h: (0, 0)),
        ],
        out_specs=pl.BlockSpec((SEQ, vb), lambda b, h: (b, h)),
        out_shape=jax.ShapeDtypeStruct((N_TOK, GLA_VDIM), BF16),
        scratch_shapes=[pltpu.VMEM((vb, kb), F32), pltpu.VMEM((SEQ, kb), F32)],
        compiler_params=_cparams("parallel", "parallel"),
        name="gla",
    )(proj, proj, proj, proj, proj, w_gk2_pad, b_gk, norm_g)


def _moba_body(q_ref, k_ref, v_ref, c_ref, s_ref, o_ref, qs_ref, ks_ref, vt_ref, bias_ref,
               sc_a_ref, sc_b_ref, pr_a_ref, pr_b_ref):
    cos_t = c_ref[0]
    sin_t = s_ref[0]
    lane = lax.broadcasted_iota(I32, (SEQ, MOBA_DH), 1)
    half = ROT_DIM // 2

    def rope(x):
        partner = jnp.where(lane < half, pltpu.roll(x, MOBA_DH - half, 1), pltpu.roll(x, half, 1))
        return x * cos_t + partner * sin_t

    q = rope(q_ref[...].astype(F32))
    k = rope(k_ref[...].astype(F32))
    q_hi, q_lo = _split_bf16(q)
    qs_ref[...] = (q * (MOBA_DH ** -0.5 * LOG2_E)).astype(BF16)
    ks_ref[...] = k.astype(BF16)
    vt_ref[...] = v_ref[...].astype(F32).T.astype(BF16)

    k_mean = jnp.concatenate(
        [jnp.mean(k[j * MOBA_BLOCK:(j + 1) * MOBA_BLOCK], axis=0, keepdims=True) for j in range(N_KBLK)], axis=0)
    m_hi, m_lo = _split_bf16(k_mean)
    s_blk = _dot_nt(m_hi, q_hi) + _dot_nt(m_hi, q_lo) + _dot_nt(m_lo, q_hi)
    blk = lax.broadcasted_iota(I32, (N_KBLK, SEQ), 0)
    q_blk = lax.shift_right_logical(lax.broadcasted_iota(I32, (N_KBLK, SEQ), 1), MOBA_BLOCK.bit_length() - 1)
    past = blk < q_blk
    s_blk = jnp.where(past, s_blk, _NEG_INF)
    beaten = jnp.zeros((N_KBLK, SEQ), I32)
    for j in range(N_KBLK):
        row = s_blk[j:j + 1, :]
        beaten += jnp.where((row > s_blk) | ((row == s_blk) & (j < blk)), 1, 0)
    bias_ref[...] = jnp.where(past & (beaten < MOBA_TOPK), 0.0, _NEG_INF)

    kr = lax.broadcasted_iota(I32, (MOBA_BLOCK, MOBA_BLOCK), 0)
    qc = lax.broadcasted_iota(I32, (MOBA_BLOCK, MOBA_BLOCK), 1)
    own_bias = jnp.where(kr <= qc, 0.0, _NEG_INF)

    sc_bufs = (sc_a_ref, sc_b_ref)
    pr_bufs = (pr_a_ref, pr_b_ref)
    zeros_row = jnp.zeros((1, MOBA_BLOCK), F32)
    future_row = jnp.full((1, MOBA_BLOCK), _NEG_INF, F32)
    for pair in range(N_KBLK // 2):
        q_blocks = (2 * pair, 2 * pair + 1)
        cols = slice(q_blocks[0] * MOBA_BLOCK, (q_blocks[1] + 1) * MOBA_BLOCK)
        n_kblk = q_blocks[1] + 1
        n_keys = n_kblk * MOBA_BLOCK
        sc, pr = sc_bufs[pair % 2], pr_bufs[pair % 2]
        q_pair = qs_ref[cols, :]

        def query_bias(j):
            halves = []
            for qb in q_blocks:
                q_cols = slice(qb * MOBA_BLOCK, (qb + 1) * MOBA_BLOCK)
                halves.append(bias_ref[j:j + 1, q_cols] if j < qb else zeros_row if j == qb else future_row)
            return jnp.concatenate(halves, axis=1)

        biases = [query_bias(j) for j in range(n_kblk)]
        col_max = []
        for j in range(n_kblk):
            rows = slice(j * MOBA_BLOCK, (j + 1) * MOBA_BLOCK)
            s = _dot_nt(ks_ref[rows, :], q_pair)
            if j in q_blocks:
                h = q_blocks.index(j)
                own = s[:, h * MOBA_BLOCK:(h + 1) * MOBA_BLOCK] + own_bias
                s = jnp.concatenate([own, s[:, MOBA_BLOCK:]] if h == 0 else [s[:, :MOBA_BLOCK], own], axis=1)
            sc[rows, :] = s
            col_max.append(jnp.max(s, axis=0, keepdims=True) + biases[j])
        m = functools.reduce(jnp.maximum, col_max)
        denom = jnp.zeros((1, 2 * MOBA_BLOCK), F32)
        for j in range(n_kblk):
            rows = slice(j * MOBA_BLOCK, (j + 1) * MOBA_BLOCK)
            p = jnp.exp2(sc[rows, :] - (m - biases[j]))
            denom = denom + jnp.sum(p, axis=0, keepdims=True)
            pr[rows, :] = p.astype(BF16)
        o_t = _dot(vt_ref[:, 0:n_keys], pr[0:n_keys, :]) * (1.0 / denom)
        o_ref[cols, :] = o_t.T.astype(BF16)


def _moba(proj, cos_t, sin_t):
    dh = MOBA_DH
    return pl.pallas_call(
        _moba_body,
        grid=(BATCH, MOBA_HEADS),
        in_specs=[
            pl.BlockSpec((SEQ, dh), lambda b, h: (b, COL_MQ // dh + h)),
            pl.BlockSpec((SEQ, dh), lambda b, h: (b, COL_MK // dh + h)),
            pl.BlockSpec((SEQ, dh), lambda b, h: (b, COL_MV // dh + h)),
            pl.BlockSpec((1, SEQ, dh), lambda b, h: (b, 0, 0)),
            pl.BlockSpec((1, SEQ, dh), lambda b, h: (b, 0, 0)),
        ],
        out_specs=pl.BlockSpec((SEQ, dh), lambda b, h: (b, h)),
        out_shape=jax.ShapeDtypeStruct((N_TOK, MOBA_DIM), BF16),
        scratch_shapes=[
            pltpu.VMEM((SEQ, dh), BF16),
            pltpu.VMEM((SEQ, dh), BF16),
            pltpu.VMEM((dh, SEQ), BF16),
            pltpu.VMEM((N_KBLK, SEQ), F32),
            pltpu.VMEM((SEQ, 2 * MOBA_BLOCK), F32),
            pltpu.VMEM((SEQ, 2 * MOBA_BLOCK), F32),
            pltpu.VMEM((SEQ, 2 * MOBA_BLOCK), BF16),
            pltpu.VMEM((SEQ, 2 * MOBA_BLOCK), BF16),
        ],
        compiler_params=_cparams("parallel", "parallel"),
        name="moba",
    )(proj, proj, proj, cos_t, sin_t)


def _layer_norm(z, g, b):
    mu = jnp.mean(z, axis=-1, keepdims=True)
    zc = z - mu
    var = jnp.mean(zc * zc, axis=-1, keepdims=True)
    return zc * lax.rsqrt(var + LN_EPS) * g + b


def _mix_body(gla_ref, moba_ref, ga_ref, gb_ref, x_ref, wgo_ref, wmo_ref, wo_ref, g_ref, b_ref,
              h_ref, hb_ref, hp_ref):
    y_gla = _dot(gla_ref[...], wgo_ref[...])
    y_moba = _dot(moba_ref[...], wmo_ref[...])
    merged = _sigmoid(ga_ref[...].astype(F32)) * y_gla + _sigmoid(gb_ref[...].astype(F32)) * y_moba
    mix = _dot(merged.astype(BF16), wo_ref[...])
    h = _layer_norm(ALPHA * x_ref[...] + mix, g_ref[...], b_ref[...])
    h_ref[...] = h
    hb_ref[...] = h.astype(BF16)
    packed = _pack_bf16_pair(h)
    for c in range(PACKED_CHUNKS):
        hp_ref[:, c, :] = packed[:, c * LANES:(c + 1) * LANES]


def _mix(gla_out, moba_out, proj, x2d, w_gla_o, w_moba_o, w_out, ln_g, ln_b):
    d = D_MODEL
    row = lambda i: (i, 0)
    full = lambda i: (0, 0)
    return pl.pallas_call(
        _mix_body,
        grid=(N_TOK // MIX_TM,),
        in_specs=[
            pl.BlockSpec((MIX_TM, d), row),
            pl.BlockSpec((MIX_TM, d), row),
            pl.BlockSpec((MIX_TM, d), lambda i: (i, COL_GA // d)),
            pl.BlockSpec((MIX_TM, d), lambda i: (i, COL_GB // d)),
            pl.BlockSpec((MIX_TM, d), row),
            pl.BlockSpec((d, d), full),
            pl.BlockSpec((d, d), full),
            pl.BlockSpec((d, d), full),
            pl.BlockSpec((1, d), full),
            pl.BlockSpec((1, d), full),
        ],
        out_specs=[
            pl.BlockSpec((MIX_TM, d), row),
            pl.BlockSpec((MIX_TM, d), row),
            pl.BlockSpec((MIX_TM, PACKED_CHUNKS, LANES), lambda i: (i, 0, 0)),
        ],
        out_shape=[
            jax.ShapeDtypeStruct((N_TOK, d), F32),
            jax.ShapeDtypeStruct((N_TOK, d), BF16),
            jax.ShapeDtypeStruct((N_TOK, PACKED_CHUNKS, LANES), U32),
        ],
        compiler_params=_cparams("parallel"),
        name="mix_ln1",
    )(gla_out, moba_out, proj, proj, x2d, w_gla_o, w_moba_o, w_out, ln_g, ln_b)


def _route_body(h_ref, wr_ref, br_ref, e_ref, w_ref, rk_ref, cnt_ref, carry_ref):
    tm = ROUTE_TM

    @pl.when(pl.program_id(0) == 0)
    def _():
        carry_ref[...] = jnp.zeros_like(carry_ref)

    scores = _sigmoid(_dot_nt(wr_ref[...], h_ref[...]))
    biased = scores + br_ref[...]
    row = lax.broadcasted_iota(I32, (N_EXPERTS, tm), 0).astype(F32)
    row_g = lax.broadcasted_iota(I32, (GROUP_SIZE, tm), 0).astype(F32)

    g_scores = []
    for g in range(N_GROUPS):
        grp = biased[g * GROUP_SIZE:(g + 1) * GROUP_SIZE]
        m1 = jnp.max(grp, axis=0, keepdims=True)
        first = jnp.min(jnp.where(grp == m1, row_g, float(GROUP_SIZE)), axis=0, keepdims=True)
        m2 = jnp.max(jnp.where(row_g == first, _NEG_INF, grp), axis=0, keepdims=True)
        g_scores.append(m1 + m2)
    g_score = jnp.concatenate(g_scores, axis=0)
    g_row = lax.broadcasted_iota(I32, (N_GROUPS, tm), 0)
    g_beaten = jnp.zeros((N_GROUPS, tm), I32)
    for g in range(N_GROUPS):
        r = g_score[g:g + 1, :]
        g_beaten += jnp.where((r > g_score) | ((r == g_score) & (g < g_row)), 1, 0)
    g_keep = g_beaten < TOPK_GROUPS
    masked = jnp.concatenate(
        [jnp.where(g_keep[g:g + 1, :], biased[g * GROUP_SIZE:(g + 1) * GROUP_SIZE], _NEG_INF)
         for g in range(N_GROUPS)], axis=0)

    onehot = jnp.zeros((N_EXPERTS, tm), F32)
    picks, pick_scores = [], []
    for _ in range(TOPK_EXPERTS):
        m = jnp.max(masked, axis=0, keepdims=True)
        idx = jnp.min(jnp.where(masked == m, row, float(N_EXPERTS)), axis=0, keepdims=True)
        hit = row == idx
        picks.append(idx)
        pick_scores.append(jnp.sum(jnp.where(hit, scores, 0.0), axis=0, keepdims=True))
        onehot = onehot + jnp.where(hit, 1.0, 0.0)
        masked = jnp.where(hit, _NEG_INF, masked)
    sel = jnp.concatenate(pick_scores, axis=0)
    e_ref[...] = jnp.concatenate(picks, axis=0).astype(I32)
    w_ref[...] = sel / jnp.sum(sel, axis=0, keepdims=True) * ROUTED_SCALE

    t_r = lax.broadcasted_iota(I32, (tm, tm), 0)
    t_c = lax.broadcasted_iota(I32, (tm, tm), 1)
    earlier = jnp.where(t_r < t_c, 1.0, 0.0).astype(BF16)
    seen = _dot(onehot.astype(BF16), earlier) + carry_ref[...]
    rk_ref[...] = jnp.concatenate(
        [jnp.sum(jnp.where(row == idx, seen, 0.0), axis=0, keepdims=True) for idx in picks], axis=0).astype(I32)
    carry_ref[...] += jnp.sum(onehot, axis=1, keepdims=True)
    cnt_ref[...] = carry_ref[...]


def _route(h_bf, w_router_t, b_router_col):
    k = TOPK_EXPERTS
    tok = lambda i: (0, i)
    return pl.pallas_call(
        _route_body,
        grid=(N_TOK // ROUTE_TM,),
        in_specs=[
            pl.BlockSpec((ROUTE_TM, D_MODEL), lambda i: (i, 0)),
            pl.BlockSpec((N_EXPERTS, D_MODEL), lambda i: (0, 0)),
            pl.BlockSpec((N_EXPERTS, 1), lambda i: (0, 0)),
        ],
        out_specs=[
            pl.BlockSpec((k, ROUTE_TM), tok),
            pl.BlockSpec((k, ROUTE_TM), tok),
            pl.BlockSpec((k, ROUTE_TM), tok),
            pl.BlockSpec((N_EXPERTS, 1), lambda i: (0, 0)),
        ],
        out_shape=[
            jax.ShapeDtypeStruct((k, N_TOK), I32),
            jax.ShapeDtypeStruct((k, N_TOK), F32),
            jax.ShapeDtypeStruct((k, N_TOK), I32),
            jax.ShapeDtypeStruct((N_EXPERTS, 1), F32),
        ],
        scratch_shapes=[pltpu.VMEM((N_EXPERTS, 1), F32)],
        compiler_params=_cparams("arbitrary"),
        name="route",
    )(h_bf, w_router_t, b_router_col)


def _dest_body(cnt_ref, e_ref, rk_ref, d_ref, te_ref, nu_ref, lt_ref):
    cnt = cnt_ref[...]
    tiles = jnp.floor((cnt + (ROW_TILE - 1)) * (1.0 / ROW_TILE))
    er = lax.broadcasted_iota(I32, (N_EXPERTS, N_EXPERTS), 0)
    ec = lax.broadcasted_iota(I32, (N_EXPERTS, N_EXPERTS), 1)
    before = jnp.where(ec < er, 1.0, 0.0).astype(BF16)
    tiles_b = jnp.broadcast_to(tiles, (N_EXPERTS, LANES)).astype(BF16)
    t_start = _dot(before, tiles_b)[:, 0:1]
    t_end = t_start + tiles
    p_start = t_start * float(ROW_TILE)
    lt_ref[...] = jnp.where(tiles > 0.0, (t_end - 1.0) * float(ROW_TILE), -1.0).astype(I32)

    row = lax.broadcasted_iota(I32, (N_EXPERTS, DEST_TM), 0)
    d_ref[...] = jnp.concatenate(
        [jnp.sum(jnp.where(row == e_ref[k:k + 1, :], p_start, 0.0), axis=0, keepdims=True)
         for k in range(TOPK_EXPERTS)], axis=0).astype(I32) + rk_ref[...]

    tile_id = lax.broadcasted_iota(I32, (N_EXPERTS, TILE_TABLE), 1).astype(F32)
    owner = jnp.sum(jnp.where(t_end <= tile_id, 1, 0), axis=0, keepdims=True)
    te_ref[...] = jnp.minimum(owner, N_EXPERTS - 1)
    nu_ref[...] = jnp.broadcast_to(t_end[N_EXPERTS - 1:N_EXPERTS, :], (1, LANES)).astype(I32)


def _dest(cnt, e_t, rk_t):
    k = TOPK_EXPERTS
    tok = lambda i: (0, i)
    return pl.pallas_call(
        _dest_body,
        grid=(N_TOK // DEST_TM,),
        in_specs=[
            pl.BlockSpec((N_EXPERTS, 1), lambda i: (0, 0)),
            pl.BlockSpec((k, DEST_TM), tok),
            pl.BlockSpec((k, DEST_TM), tok),
        ],
        out_specs=[
            pl.BlockSpec((k, DEST_TM), tok),
            pl.BlockSpec((1, TILE_TABLE), lambda i: (0, 0)),
            pl.BlockSpec((1, LANES), lambda i: (0, 0)),
            pl.BlockSpec((N_EXPERTS, 1), lambda i: (0, 0)),
        ],
        out_shape=[
            jax.ShapeDtypeStruct((k, N_TOK), I32),
            jax.ShapeDtypeStruct((1, TILE_TABLE), I32),
            jax.ShapeDtypeStruct((1, LANES), I32),
            jax.ShapeDtypeStruct((N_EXPERTS, 1), I32),
        ],
        compiler_params=_cparams("arbitrary"),
        name="dest",
    )(cnt, e_t, rk_t)


def _row_copy(src, dst, sem):
    return pltpu.make_async_copy(src, dst, sem)


def _dispatch_body(lt_ref, nu_ref, d_ref, h_ref, xs_ref, zero_ref, sem):
    @pl.when(pl.program_id(0) == 0)
    def _():
        zero_ref[...] = jnp.zeros_like(zero_ref)

        def zero_tile(first_row):
            r0 = pl.multiple_of(first_row, ROW_TILE)
            return pltpu.make_async_copy(zero_ref, xs_ref.at[pl.ds(r0, ROW_TILE)], sem)

        def z_start(e, carry):
            @pl.when(lt_ref[e] >= 0)
            def _():
                zero_tile(jnp.maximum(lt_ref[e], 0)).start()
            return carry

        def z_wait(e, carry):
            @pl.when(lt_ref[e] >= 0)
            def _():
                zero_tile(jnp.maximum(lt_ref[e], 0)).wait()
            return carry

        def t_start(t, carry):
            zero_tile(t * ROW_TILE).start()
            return carry

        def t_wait(t, carry):
            zero_tile(t * ROW_TILE).wait()
            return carry

        lax.fori_loop(0, N_EXPERTS, z_start, 0)
        lax.fori_loop(nu_ref[0], N_ROW_TILES, t_start, 0)
        lax.fori_loop(0, N_EXPERTS, z_wait, 0)
        lax.fori_loop(nu_ref[0], N_ROW_TILES, t_wait, 0)

    for t in range(DISP_TM):
        for k in range(TOPK_EXPERTS):
            _row_copy(h_ref.at[t], xs_ref.at[d_ref[k, t]], sem).start(priority=k % 2)
    for k in range(TOPK_EXPERTS):
        pltpu.make_async_copy(h_ref, xs_ref.at[pl.ds(0, DISP_TM)], sem).wait()


def _dispatch(last_tile_row, n_used, dest_t, h1):
    grid_spec = pltpu.PrefetchScalarGridSpec(
        num_scalar_prefetch=2,
        grid=(N_TOK // DISP_TM,),
        in_specs=[
            pl.BlockSpec((TOPK_EXPERTS, DISP_TM), lambda i, lt, nu: (0, i), memory_space=pltpu.SMEM),
            pl.BlockSpec((DISP_TM, PACKED_CHUNKS, LANES), lambda i, lt, nu: (i, 0, 0)),
        ],
        out_specs=pl.BlockSpec(memory_space=pl.ANY),
        scratch_shapes=[pltpu.VMEM((ROW_TILE, PACKED_CHUNKS, LANES), U32), pltpu.SemaphoreType.DMA],
    )
    return pl.pallas_call(
        _dispatch_body,
        grid_spec=grid_spec,
        out_shape=jax.ShapeDtypeStruct((N_ROWS, PACKED_CHUNKS, LANES), U32),
        compiler_params=_cparams("arbitrary"),
        name="dispatch",
    )(last_tile_row, n_used, dest_t, h1)


def _row_destinations(dest_flat):
    info = plsc.get_sparse_core_info()
    n_cores, lanes = info.num_cores, info.num_lanes
    n_workers = n_cores * info.num_subcores
    rows_per_worker = -(-N_ROWS // (n_workers * ROW_TILE)) * ROW_TILE
    assert rows_per_worker % lanes == 0
    mesh = plsc.VectorSubcoreMesh(core_axis_name="c", subcore_axis_name="s")

    @functools.partial(
        pl.kernel, mesh=mesh, out_type=jax.ShapeDtypeStruct((n_workers * rows_per_worker,), I32),
        scratch_types=[pltpu.VMEM((SLOT_CHUNK,), I32), pltpu.VMEM((rows_per_worker,), I32)],
        compiler_params=pltpu.CompilerParams(needs_layout_passes=False), name="row_destinations")
    def invert(dest_hbm, out_hbm, dest_v, table_v):
        first_row = (lax.axis_index("s") * n_cores + lax.axis_index("c")) * rows_per_worker
        lane = lax.iota(I32, lanes)
        tile_shift = ROW_TILE.bit_length() - 1

        def spare(i, carry):
            row = first_row + i * lanes + lane
            buf = lax.rem(lax.shift_right_logical(row, tile_shift), EXPERT_BUFS)
            table_v[pl.ds(i * lanes, lanes)] = TOPK_EXPERTS * N_TOK + buf * ROW_TILE + (row & (ROW_TILE - 1))
            return carry

        lax.fori_loop(0, rows_per_worker // lanes, spare, 0)

        def chunk(c, carry):
            pltpu.sync_copy(dest_hbm.at[pl.ds(c * SLOT_CHUNK, SLOT_CHUNK)], dest_v)

            def vec(i, inner):
                local = dest_v[pl.ds(i * lanes, lanes)] - first_row
                mine = (local >= 0) & (local < rows_per_worker)
                plsc.store_scatter(table_v, [jnp.where(mine, local, 0)], lane + (c * SLOT_CHUNK + i * lanes),
                                   mask=mine)
                return inner

            lax.fori_loop(0, SLOT_CHUNK // lanes, vec, 0)
            return carry

        lax.fori_loop(0, N_ASSIGN // SLOT_CHUNK, chunk, 0)
        pltpu.sync_copy(table_v, out_hbm.at[pl.ds(first_row, rows_per_worker)])

    return invert(dest_flat)


def _experts_body(te_ref, nu_ref, lt_ref, slot_ref, xs_ref, wg_ref, wu_ref, wd_ref, yk_ref,
                  wg_f, wu_f, wd_f, wg_b, wu_b, wd_b, y_buf, n_loaded, sem, w_sem):
    i = pl.program_id(0)
    n_used = nu_ref[0]

    def weight_copies(e, s):
        return (pltpu.make_async_copy(wg_ref.at[e], wg_f.at[s], w_sem.at[s]),
                pltpu.make_async_copy(wu_ref.at[e], wu_f.at[s], w_sem.at[s]),
                pltpu.make_async_copy(wd_ref.at[e], wd_f.at[s], w_sem.at[s]))

    def wait_tile(b):
        pltpu.make_async_copy(y_buf.at[b], yk_ref.at[pl.ds(0, ROW_TILE), :], sem.at[b]).wait()

    def send_tile(b):
        for r in range(ROW_TILE):
            dst = slot_ref[0, 0, r]
            _row_copy(y_buf.at[b, pl.ds(r, 1), :], yk_ref.at[pl.ds(dst, 1), :], sem.at[b]).start(priority=r % 2)

    @pl.when(i == 0)
    def _():
        y_buf[EXPERT_BUFS - 1] = jnp.zeros((ROW_TILE, PACKED_COLS), U32)
        spare = [pltpu.make_async_copy(y_buf.at[EXPERT_BUFS - 1],
                                       yk_ref.at[pl.ds(TOPK_EXPERTS * N_TOK + b * ROW_TILE, ROW_TILE), :],
                                       sem.at[b]) for b in range(EXPERT_BUFS - 1)]
        for c in spare:
            c.start()
        for c in spare:
            c.wait()
        n_loaded[0] = 0
        for c in weight_copies(te_ref[0], 0):
            c.start()

    @pl.when((i >= 2) & (i <= n_used))
    def _():
        wait_tile(lax.rem(i, EXPERT_BUFS))

    @pl.when(i < n_used)
    def _():
        e = te_ref[i]
        prev = te_ref[jnp.maximum(i - 1, 0)]

        @pl.when((i == 0) | (e != prev))
        def _():
            s = lax.rem(n_loaded[0], 2)
            for c in weight_copies(e, s):
                c.wait()
            wg_b[...] = wg_f[s].astype(BF16)
            wu_b[...] = wu_f[s].astype(BF16)
            wd_b[...] = wd_f[s].astype(BF16)
            n_loaded[0] = n_loaded[0] + 1
            nxt = lax.shift_right_logical(lt_ref[e], ROW_TILE.bit_length() - 1) + 1

            @pl.when(nxt < n_used)
            def _():
                for c in weight_copies(te_ref[nxt], 1 - s):
                    c.start()

        for phase in range(EXPERT_BUFS):
            @pl.when(lax.rem(i, EXPERT_BUFS) == phase)
            def _(phase=phase):
                send_tile((phase + EXPERT_BUFS - 1) % EXPERT_BUFS)
                packed = jnp.concatenate([xs_ref[:, c, :] for c in range(PACKED_CHUNKS)], axis=1)
                x_lo, x_hi = _unpack_bf16_pair(packed)
                x = jnp.concatenate([x_lo.astype(BF16), x_hi.astype(BF16)], axis=1)
                g = _dot(x, wg_b[...])
                u = _dot(x, wu_b[...])
                h = (g * _sigmoid(g)) * u
                y_buf[phase] = _pack_bf16_pair(_dot(h.astype(BF16), wd_b[...]))

    @pl.when(i == n_used)
    def _():
        send_tile(lax.rem(i + EXPERT_BUFS - 1, EXPERT_BUFS))
        wait_tile(lax.rem(i + EXPERT_BUFS - 2, EXPERT_BUFS))
        wait_tile(lax.rem(i + EXPERT_BUFS - 1, EXPERT_BUFS))


def _experts(tile_expert, n_used, last_tile_row, row_dst, xs, w_gate, w_up, w_down):
    def tile(i, te, nu, lt):
        return (jnp.minimum(i, nu[0] - 1), 0, 0)

    n_table_tiles = row_dst.shape[0] // ROW_TILE
    placeholder = n_table_tiles - 1
    assert placeholder * ROW_TILE >= N_ROWS and placeholder % EXPERT_BUFS == EXPERT_BUFS - 1

    def prev_tile(i, te, nu, lt):
        return (jnp.where(i == 0, placeholder, jnp.minimum(i, nu[0]) - 1), 0, 0)

    grid_spec = pltpu.PrefetchScalarGridSpec(
        num_scalar_prefetch=3,
        grid=(N_ROW_TILES + 1,),
        in_specs=[
            pl.BlockSpec((1, 1, ROW_TILE), prev_tile, memory_space=pltpu.SMEM),
            pl.BlockSpec((ROW_TILE, PACKED_CHUNKS, LANES), tile),
            pl.BlockSpec(memory_space=pl.ANY),
            pl.BlockSpec(memory_space=pl.ANY),
            pl.BlockSpec(memory_space=pl.ANY),
        ],
        out_specs=pl.BlockSpec(memory_space=pl.ANY),
        scratch_shapes=[
            pltpu.VMEM((2, D_MODEL, D_EXPERT), F32),
            pltpu.VMEM((2, D_MODEL, D_EXPERT), F32),
            pltpu.VMEM((2, D_EXPERT, D_MODEL), F32),
            pltpu.VMEM((D_MODEL, D_EXPERT), BF16),
            pltpu.VMEM((D_MODEL, D_EXPERT), BF16),
            pltpu.VMEM((D_EXPERT, D_MODEL), BF16),
            pltpu.VMEM((EXPERT_BUFS, ROW_TILE, PACKED_COLS), U32),
            pltpu.SMEM((1,), I32),
            pltpu.SemaphoreType.DMA((EXPERT_BUFS,)),
            pltpu.SemaphoreType.DMA((2,)),
        ],
    )
    return pl.pallas_call(
        _experts_body,
        grid_spec=grid_spec,
        out_shape=jax.ShapeDtypeStruct((YK_ROWS, PACKED_COLS), U32),
        compiler_params=_cparams("arbitrary"),
        name="experts",
    )(tile_expert, n_used, last_tile_row, row_dst.reshape(n_table_tiles, 1, ROW_TILE), xs, w_gate, w_up, w_down)


def _final_body(w_ref, *refs):
    yk_refs = refs[:TOPK_EXPERTS]
    h_ref, hb_ref, p_ref, wsg_ref, wsu_ref, wsd_ref, wpl_ref, wpg_ref, g_ref, b_ref, o_ref = refs[TOPK_EXPERTS:]
    hb = hb_ref[...]
    sg = _dot(hb, wsg_ref[...])
    shared = _dot(((sg * _sigmoid(sg)) * _dot(hb, wsu_ref[...])).astype(BF16), wsd_ref[...])
    ple = _sigmoid(_dot(hb, wpg_ref[...])) * _dot(p_ref[...].astype(BF16), wpl_ref[...])

    w_col = w_ref[...].T
    y_lo, y_hi = _unpack_bf16_pair(yk_refs[0][...])
    r_lo, r_hi = y_lo * w_col[:, 0:1], y_hi * w_col[:, 0:1]
    for k in range(1, TOPK_EXPERTS):
        y_lo, y_hi = _unpack_bf16_pair(yk_refs[k][...])
        r_lo, r_hi = r_lo + y_lo * w_col[:, k:k + 1], r_hi + y_hi * w_col[:, k:k + 1]
    routed = jnp.concatenate([r_lo, r_hi], axis=1)
    o_ref[...] = _layer_norm(ALPHA * h_ref[...] + (routed + shared) + ple, g_ref[...], b_ref[...])


def _final(w_t, yk, h1, h_bf, p2d, w_s_gate, w_s_up, w_s_down, w_ple, w_ple_gate, ln_g, ln_b):
    d = D_MODEL
    n_steps = N_TOK // FINAL_TM
    full = lambda i: (0, 0)
    return pl.pallas_call(
        _final_body,
        grid=(n_steps,),
        in_specs=[
            pl.BlockSpec((TOPK_EXPERTS, FINAL_TM), lambda i: (0, i)),
            *[pl.BlockSpec((FINAL_TM, PACKED_COLS), lambda i, k=k: (k * n_steps + i, 0)) for k in range(TOPK_EXPERTS)],
            pl.BlockSpec((FINAL_TM, d), lambda i: (i, 0)),
            pl.BlockSpec((FINAL_TM, d), lambda i: (i, 0)),
            pl.BlockSpec((FINAL_TM, PLE_DIM), lambda i: (i, 0)),
            pl.BlockSpec((d, D_SHARED), full),
            pl.BlockSpec((d, D_SHARED), full),
            pl.BlockSpec((D_SHARED, d), full),
            pl.BlockSpec((PLE_DIM, d), full),
            pl.BlockSpec((d, d), full),
            pl.BlockSpec((1, d), full),
            pl.BlockSpec((1, d), full),
        ],
        out_specs=pl.BlockSpec((FINAL_TM, d), lambda i: (i, 0)),
        out_shape=jax.ShapeDtypeStruct((N_TOK, d), F32),
        compiler_params=_cparams("parallel"),
        name="combine_ln2",
    )(w_t, *([yk] * TOPK_EXPERTS), h1, h_bf, p2d, w_s_gate, w_s_up, w_s_down, w_ple, w_ple_gate, ln_g, ln_b)


def _rope_tables(positions):
    half = ROT_DIM // 2
    inv = ROPE_THETA ** (-jnp.arange(0, ROT_DIM, 2, dtype=F32) / ROT_DIM)
    per_row = LANES // half
    pos = jnp.repeat(positions.reshape(-1, per_row), half, axis=1).astype(F32)
    ang = pos * jnp.tile(inv, per_row)[None, :]
    cos, sin = lax.optimization_barrier((jnp.cos(ang), jnp.sin(ang)))
    cos = cos.reshape(positions.shape + (half,))
    sin = sin.reshape(positions.shape + (half,))
    rest = MOBA_DH - ROT_DIM
    ones = jnp.ones(cos.shape[:-1] + (rest,), F32)
    zeros = jnp.zeros(cos.shape[:-1] + (rest,), F32)
    return (jnp.concatenate([cos, cos, ones], axis=-1), jnp.concatenate([-sin, sin, zeros], axis=-1))


def _layer(h2d, p2d, cos_t, sin_t, w_cat, w_gk2, b_gk, norm_g, w_gla_o, w_moba_o, w_out, ln1_g, ln1_b,
           w_router, b_router, w_e_gate, w_e_up, w_e_down, w_s_gate, w_s_up, w_s_down, w_ple, w_ple_gate,
           ln2_g, ln2_b):
    w_gk2_pad = jnp.concatenate([w_gk2, jnp.zeros((LANES - GLA_RANK, GLA_KDIM), w_gk2.dtype)], axis=0)

    proj = _in_proj(h2d, w_cat)
    gla_out = _gla(proj, w_gk2_pad, b_gk[None, :], norm_g[None, :])
    moba_out = _moba(proj, cos_t, sin_t)
    h1, h_bf, h_packed = _mix(gla_out, moba_out, proj, h2d, w_gla_o.astype(BF16), w_moba_o.astype(BF16),
                              w_out.astype(BF16), ln1_g[None, :], ln1_b[None, :])
    e_t, w_t, rk_t, cnt = _route(h_bf, w_router.T.astype(BF16), b_router[:, None])
    dest_t, tile_expert, n_used, last_tile_row = _dest(cnt, e_t, rk_t)
    last_tile_row = last_tile_row.reshape(N_EXPERTS)
    n_used = n_used[0, 0:1]
    xs = _dispatch(last_tile_row, n_used, dest_t, h_packed)
    row_dst = _row_destinations(dest_t.reshape(N_ASSIGN))
    yk = _experts(tile_expert.reshape(TILE_TABLE), n_used, last_tile_row, row_dst, xs,
                  w_e_gate, w_e_up, w_e_down)
    return _final(w_t, yk, h1, h_bf, p2d, w_s_gate.astype(BF16), w_s_up.astype(BF16),
                  w_s_down.astype(BF16), w_ple.astype(BF16), w_ple_gate.astype(BF16),
                  ln2_g[None, :], ln2_b[None, :])


def kernel(x, p, positions, w_in, w_gla_gk2, b_gla_gk, gla_norm_g, w_gla_o, w_moba_o, w_out, ln1_g, ln1_b,
           w_router, b_router, w_e_gate, w_e_up, w_e_down, w_s_gate, w_s_up, w_s_down, w_ple, w_ple_gate,
           ln2_g, ln2_b):
    cos_t, sin_t = _rope_tables(positions)
    h = x.reshape(N_TOK, D_MODEL)
    for i in range(DEPTH):
        h = _layer(h, p[i].reshape(N_TOK, PLE_DIM), cos_t, sin_t, _repack_w_in(w_in, i), w_gla_gk2[i], b_gla_gk[i],
                   gla_norm_g[i], w_gla_o[i], w_moba_o[i], w_out[i], ln1_g[i], ln1_b[i], w_router[i],
                   b_router[i], w_e_gate[i], w_e_up[i], w_e_down[i], w_s_gate[i], w_s_up[i], w_s_down[i],
                   w_ple[i], w_ple_gate[i], ln2_g[i], ln2_b[i])
    return h.reshape(BATCH, SEQ, D_MODEL)
```

```python
import functools

import jax
import jax.numpy as jnp
from jax import lax
from jax.experimental import pallas as pl
from jax.experimental.pallas import tpu as pltpu
from jax.experimental.pallas import tpu_sc as plsc

F32 = jnp.float32
BF16 = jnp.bfloat16
I32 = jnp.int32
U32 = jnp.uint32

LANES = 128
SUBLANES = 8
VMEM_LIMIT_BYTES = 48 * 1024 * 1024

D_MODEL = 1024
BATCH = 8
SEQ = 2048
N_TOK = BATCH * SEQ
GLA_HEADS = 4
GLA_DK = 128
GLA_DV = 256
GLA_RANK = 16
GLA_NORMALIZER = 16.0
GLA_CHUNK = 64
GLA_GROUP = 2 * GLA_CHUNK
GLA_UNROLL = 8
MOBA_HEADS = 8
MOBA_DH = 128
MOBA_BLOCK = 256
MOBA_TOPK = 3
ROT_DIM = 32
ROPE_THETA = 500000.0
N_EXPERTS = 256
TOPK_EXPERTS = 8
N_GROUPS = 8
GROUP_SIZE = N_EXPERTS // N_GROUPS
TOPK_GROUPS = 4
D_EXPERT = 256
D_SHARED = 256
ROUTED_SCALE = 2.5
PLE_DIM = 256
LN_EPS = 1e-5
DEPTH = 1
ALPHA = (2.0 * DEPTH) ** 0.25
GLA_KDIM = GLA_HEADS * GLA_DK
GLA_VDIM = GLA_HEADS * GLA_DV
MOBA_DIM = MOBA_HEADS * MOBA_DH
N_KBLK = SEQ // MOBA_BLOCK

COL_GQ = 0
COL_GK = COL_GQ + GLA_KDIM
COL_GV = COL_GK + GLA_KDIM
COL_GR = COL_GV + GLA_VDIM
COL_MQ = COL_GR + GLA_VDIM
COL_MK = COL_MQ + MOBA_DIM
COL_MV = COL_MK + MOBA_DIM
COL_GA = COL_MV + MOBA_DIM
COL_GB = COL_GA + D_MODEL
COL_LOW = COL_GB + D_MODEL
PROJ_COLS = COL_LOW + LANES
LOW_SRC = 2 * GLA_KDIM + 2 * GLA_VDIM

ROW_TILE = 256
PACKED_COLS = D_MODEL // 2
PACKED_CHUNKS = PACKED_COLS // LANES
EXPERT_BUFS = 3
YK_ROWS = TOPK_EXPERTS * N_TOK + EXPERT_BUFS * ROW_TILE
SLOT_CHUNK = 4096
assert EXPERT_BUFS * ROW_TILE <= N_TOK
N_ASSIGN = N_TOK * TOPK_EXPERTS
N_ROW_TILES = (N_ASSIGN + N_EXPERTS * (ROW_TILE - 1) + ROW_TILE - 1) // ROW_TILE
N_ROWS = N_ROW_TILES * ROW_TILE
TILE_TABLE = -(-N_ROW_TILES // LANES) * LANES

PROJ_TM = 1024
REPACK_TN = 512
PROJ_TN = PROJ_COLS // 5
MIX_TM = 512
ROUTE_TM = 512
DEST_TM = 2048
DISP_TM = 256
FINAL_TM = 256

_NEG_INF = float("-inf")
LOG2_E = 1.4426950408889634


def _cparams(*sem):
    return pltpu.CompilerParams(dimension_semantics=sem, vmem_limit_bytes=VMEM_LIMIT_BYTES)


def _dot(a, b):
    return jnp.dot(a, b, preferred_element_type=F32)


def _dot_nt(a, b):
    return lax.dot_general(a, b, (((1,), (1,)), ((), ())), preferred_element_type=F32)


def _dot_tn(a, b):
    return lax.dot_general(a, b, (((0,), (0,)), ((), ())), preferred_element_type=F32)


def _split_bf16(x):
    hi = x.astype(BF16)
    lo = (x - hi.astype(F32)).astype(BF16)
    return hi, lo


def _sigmoid(x):
    return 1.0 / (1.0 + jnp.exp(-x))


def _pack_bf16_pair(x):
    lo = lax.bitcast_convert_type(x[:, :PACKED_COLS].astype(BF16).astype(F32), U32)
    hi = lax.bitcast_convert_type(x[:, PACKED_COLS:].astype(BF16).astype(F32), U32)
    return hi | lax.shift_right_logical(lo, jnp.uint32(16))


def _unpack_bf16_pair(w):
    lo = lax.bitcast_convert_type(lax.shift_left(w, jnp.uint32(16)), F32)
    hi = lax.bitcast_convert_type(w & jnp.uint32(0xFFFF0000), F32)
    return lo, hi


def _repack_body(a_ref, b_ref, o_ref):
    t = pl.program_id(0)
    a = a_ref[...].astype(BF16)

    @pl.when(t < LOW_SRC // REPACK_TN)
    def _():
        o_ref[...] = a

    @pl.when((t >= LOW_SRC // REPACK_TN) & (t < COL_LOW // REPACK_TN))
    def _():
        o_ref[...] = jnp.concatenate([a[:, GLA_RANK:], b_ref[:, :GLA_RANK].astype(BF16)], axis=1)

    @pl.when(t == COL_LOW // REPACK_TN)
    def _():
        o_ref[...] = jnp.concatenate([a[:, :GLA_RANK], jnp.zeros((D_MODEL, REPACK_TN - GLA_RANK), BF16)], axis=1)


def _repack_w_in(w_in, layer):
    assert LOW_SRC % REPACK_TN == 0 and COL_LOW % REPACK_TN == 0
    low_tile = LOW_SRC // REPACK_TN
    last_out = COL_LOW // REPACK_TN
    lanes_per_tile = REPACK_TN // LANES
    return pl.pallas_call(
        _repack_body,
        grid=(last_out + 1,),
        in_specs=[
            pl.BlockSpec((None, D_MODEL, REPACK_TN), lambda t: (layer, 0, jnp.where(t == last_out, low_tile, t))),
            pl.BlockSpec((None, D_MODEL, LANES), lambda t: (layer, 0, jnp.minimum(t + 1, last_out) * lanes_per_tile)),
        ],
        out_specs=pl.BlockSpec((D_MODEL, REPACK_TN), lambda t: (0, t)),
        out_shape=jax.ShapeDtypeStruct((D_MODEL, PROJ_COLS), BF16),
        compiler_params=_cparams("parallel"),
        name="repack_w_in",
    )(w_in, w_in)


def _proj_body(x_ref, w_ref, o_ref):
    o_ref[...] = _dot(x_ref[...].astype(BF16), w_ref[...]).astype(BF16)


def _in_proj(x2d, w_cat):
    return pl.pallas_call(
        _proj_body,
        grid=(PROJ_COLS // PROJ_TN, N_TOK // PROJ_TM),
        in_specs=[
            pl.BlockSpec((PROJ_TM, D_MODEL), lambda j, i: (i, 0)),
            pl.BlockSpec((D_MODEL, PROJ_TN), lambda j, i: (0, j)),
        ],
        out_specs=pl.BlockSpec((PROJ_TM, PROJ_TN), lambda j, i: (i, j)),
        out_shape=jax.ShapeDtypeStruct((N_TOK, PROJ_COLS), BF16),
        compiler_params=_cparams("parallel", "parallel"),
        name="in_proj",
    )(x2d, w_cat)


def _gla_body(q_ref, k_ref, v_ref, r_ref, low_ref, wg_ref, bg_ref, ng_ref, o_ref, st_ref, gk_ref):
    w_hi, w_lo = _split_bf16(wg_ref[...])
    low = low_ref[...]
    lin = _dot(low, w_hi) + _dot(low, w_lo) + bg_ref[...]
    gk_ref[...] = (jnp.minimum(lin, 0.0) - jnp.log1p(jnp.exp(-jnp.abs(lin)))) * (1.0 / GLA_NORMALIZER)
    st_ref[...] = jnp.zeros_like(st_ref)

    ri = lax.broadcasted_iota(I32, (GLA_GROUP, GLA_GROUP), 0)
    ci = lax.broadcasted_iota(I32, (GLA_GROUP, GLA_GROUP), 1)
    same_chunk = lax.shift_right_logical(ri, GLA_CHUNK.bit_length() - 1) == lax.shift_right_logical(
        ci, GLA_CHUNK.bit_length() - 1)
    causal = same_chunk & (ri >= ci)
    sums = jnp.concatenate([jnp.where(causal, 1.0, 0.0), jnp.where(same_chunk, 1.0, 0.0)], axis=0).astype(BF16)
    gain = ng_ref[...]

    def group(c, carry):
        rows = pl.ds(pl.multiple_of(c * GLA_GROUP, GLA_GROUP), GLA_GROUP)
        g_hi, g_lo = _split_bf16(gk_ref[rows, :])
        bb = _dot(sums, g_hi) + _dot(sums, g_lo)
        b = bb[0:GLA_GROUP]
        b_end = bb[GLA_GROUP:2 * GLA_GROUP]
        q = q_ref[rows, :].astype(F32) * (GLA_DK ** -0.5)
        k = k_ref[rows, :].astype(F32)
        v = v_ref[rows, :]
        q_e = (q * jnp.exp(b)).astype(BF16)
        k_e = (k * jnp.exp(-b)).astype(BF16)
        k_d = (k * jnp.exp(b_end - b)).astype(BF16)
        att = jnp.where(causal, _dot_nt(q_e, k_e), 0.0)
        o = _dot(att.astype(BF16), v)
        st = st_ref[...]
        inter = []
        for j in range(GLA_GROUP // GLA_CHUNK):
            cr = slice(j * GLA_CHUNK, (j + 1) * GLA_CHUNK)
            inter.append(_dot_nt(q_e[cr], st.astype(BF16)))
            st = st * jnp.exp(b_end[j * GLA_CHUNK:j * GLA_CHUNK + 1, :]) + _dot_tn(v[cr], k_d[cr])
        st_ref[...] = st
        o = o + jnp.concatenate(inter, axis=0)
        o = o * lax.rsqrt(jnp.mean(o * o, axis=-1, keepdims=True) + LN_EPS) * gain
        r = r_ref[rows, :].astype(F32)
        o_ref[rows, :] = (o * (r * _sigmoid(r))).astype(BF16)
        return carry

    lax.fori_loop(0, SEQ // GLA_GROUP, group, 0, unroll=GLA_UNROLL)


def _gla(proj, w_gk2_pad, b_gk, norm_g):
    kb, vb = GLA_DK, GLA_DV
    return pl.pallas_call(
        _gla_body,
        grid=(BATCH, GLA_HEADS),
        in_specs=[
            pl.BlockSpec((SEQ, kb), lambda b, h: (b, COL_GQ // kb + h)),
            pl.BlockSpec((SEQ, kb), lambda b, h: (b, COL_GK // kb + h)),
            pl.BlockSpec((SEQ, vb), lambda b, h: (b, COL_GV // vb + h)),
            pl.BlockSpec((SEQ, vb), lambda b, h: (b, COL_GR // vb + h)),
            pl.BlockSpec((SEQ, LANES), lambda b, h: (b, COL_LOW // LANES)),
            pl.BlockSpec((LANES, kb), lambda b, h: (0, h)),
            pl.BlockSpec((1, kb), lambda b, h: (0, h)),
            pl.BlockSpec((1, vb), lambda b, h: (0, 0)),
        ],
        out_specs=pl.BlockSpec((SEQ, vb), lambda b, h: (b, h)),
        out_shape=jax.ShapeDtypeStruct((N_TOK, GLA_VDIM), BF16),
        scratch_shapes=[pltpu.VMEM((vb, kb), F32), pltpu.VMEM((SEQ, kb), F32)],
        compiler_params=_cparams("parallel", "parallel"),
        name="gla",
    )(proj, proj, proj, proj, proj, w_gk2_pad, b_gk, norm_g)


def _moba_body(q_ref, k_ref, v_ref, c_ref, s_ref, o_ref, qs_ref, ks_ref, vt_ref, bias_ref,
               sc_a_ref, sc_b_ref, pr_a_ref, pr_b_ref):
    cos_t = c_ref[0]
    sin_t = s_ref[0]
    lane = lax.broadcasted_iota(I32, (SEQ, MOBA_DH), 1)
    half = ROT_DIM // 2

    def rope(x):
        partner = jnp.where(lane < half, pltpu.roll(x, MOBA_DH - half, 1), pltpu.roll(x, half, 1))
        return x * cos_t + partner * sin_t

    q = rope(q_ref[...].astype(F32))
    k = rope(k_ref[...].astype(F32))
    q_hi, q_lo = _split_bf16(q)
    qs_ref[...] = (q * (MOBA_DH ** -0.5 * LOG2_E)).astype(BF16)
    ks_ref[...] = k.astype(BF16)
    vt_ref[...] = v_ref[...].astype(F32).T.astype(BF16)

    k_mean = jnp.concatenate(
        [jnp.mean(k[j * MOBA_BLOCK:(j + 1) * MOBA_BLOCK], axis=0, keepdims=True) for j in range(N_KBLK)], axis=0)
    m_hi, m_lo = _split_bf16(k_mean)
    s_blk = _dot_nt(m_hi, q_hi) + _dot_nt(m_hi, q_lo) + _dot_nt(m_lo, q_hi)
    blk = lax.broadcasted_iota(I32, (N_KBLK, SEQ), 0)
    q_blk = lax.shift_right_logical(lax.broadcasted_iota(I32, (N_KBLK, SEQ), 1), MOBA_BLOCK.bit_length() - 1)
    past = blk < q_blk
    s_blk = jnp.where(past, s_blk, _NEG_INF)
    beaten = jnp.zeros((N_KBLK, SEQ), I32)
    for j in range(N_KBLK):
        row = s_blk[j:j + 1, :]
        beaten += jnp.where((row > s_blk) | ((row == s_blk) & (j < blk)), 1, 0)
    bias_ref[...] = jnp.where(past & (beaten < MOBA_TOPK), 0.0, _NEG_INF)

    kr = lax.broadcasted_iota(I32, (MOBA_BLOCK, MOBA_BLOCK), 0)
    qc = lax.broadcasted_iota(I32, (MOBA_BLOCK, MOBA_BLOCK), 1)
    own_bias = jnp.where(kr <= qc, 0.0, _NEG_INF)

    sc_bufs = (sc_a_ref, sc_b_ref)
    pr_bufs = (pr_a_ref, pr_b_ref)
    zeros_row = jnp.zeros((1, MOBA_BLOCK), F32)
    future_row = jnp.full((1, MOBA_BLOCK), _NEG_INF, F32)
    for pair in range(N_KBLK // 2):
        q_blocks = (2 * pair, 2 * pair + 1)
        cols = slice(q_blocks[0] * MOBA_BLOCK, (q_blocks[1] + 1) * MOBA_BLOCK)
        n_kblk = q_blocks[1] + 1
        n_keys = n_kblk * MOBA_BLOCK
        sc, pr = sc_bufs[pair % 2], pr_bufs[pair % 2]
        q_pair = qs_ref[cols, :]

        def query_bias(j):
            halves = []
            for qb in q_blocks:
                q_cols = slice(qb * MOBA_BLOCK, (qb + 1) * MOBA_BLOCK)
                halves.append(bias_ref[j:j + 1, q_cols] if j < qb else zeros_row if j == qb else future_row)
            return jnp.concatenate(halves, axis=1)

        biases = [query_bias(j) for j in range(n_kblk)]
        col_max = []
        for j in range(n_kblk):
            rows = slice(j * MOBA_BLOCK, (j + 1) * MOBA_BLOCK)
            s = _dot_nt(ks_ref[rows, :], q_pair)
            if j in q_blocks:
                h = q_blocks.index(j)
                own = s[:, h * MOBA_BLOCK:(h + 1) * MOBA_BLOCK] + own_bias
                s = jnp.concatenate([own, s[:, MOBA_BLOCK:]] if h == 0 else [s[:, :MOBA_BLOCK], own], axis=1)
            sc[rows, :] = s
            col_max.append(jnp.max(s, axis=0, keepdims=True) + biases[j])
        m = functools.reduce(jnp.maximum, col_max)
        denom = jnp.zeros((1, 2 * MOBA_BLOCK), F32)
        for j in range(n_kblk):
            rows = slice(j * MOBA_BLOCK, (j + 1) * MOBA_BLOCK)
            p = jnp.exp2(sc[rows, :] - (m - biases[j]))
            denom = denom + jnp.sum(p, axis=0, keepdims=True)
            pr[rows, :] = p.astype(BF16)
        o_t = _dot(vt_ref[:, 0:n_keys], pr[0:n_keys, :]) * (1.0 / denom)
        o_ref[cols, :] = o_t.T.astype(BF16)


def _moba(proj, cos_t, sin_t):
    dh = MOBA_DH
    return pl.pallas_call(
        _moba_body,
        grid=(BATCH, MOBA_HEADS),
        in_specs=[
            pl.BlockSpec((SEQ, dh), lambda b, h: (b, COL_MQ // dh + h)),
            pl.BlockSpec((SEQ, dh), lambda b, h: (b, COL_MK // dh + h)),
            pl.BlockSpec((SEQ, dh), lambda b, h: (b, COL_MV // dh + h)),
            pl.BlockSpec((1, SEQ, dh), lambda b, h: (b, 0, 0)),
            pl.BlockSpec((1, SEQ, dh), lambda b, h: (b, 0, 0)),
        ],
        out_specs=pl.BlockSpec((SEQ, dh), lambda b, h: (b, h)),
        out_shape=jax.ShapeDtypeStruct((N_TOK, MOBA_DIM), BF16),
        scratch_shapes=[
            pltpu.VMEM((SEQ, dh), BF16),
            pltpu.VMEM((SEQ, dh), BF16),
            pltpu.VMEM((dh, SEQ), BF16),
            pltpu.VMEM((N_KBLK, SEQ), F32),
            pltpu.VMEM((SEQ, 2 * MOBA_BLOCK), F32),
            pltpu.VMEM((SEQ, 2 * MOBA_BLOCK), F32),
            pltpu.VMEM((SEQ, 2 * MOBA_BLOCK), BF16),
            pltpu.VMEM((SEQ, 2 * MOBA_BLOCK), BF16),
        ],
        compiler_params=_cparams("parallel", "parallel"),
        name="moba",
    )(proj, proj, proj, cos_t, sin_t)


def _layer_norm(z, g, b):
    mu = jnp.mean(z, axis=-1, keepdims=True)
    zc = z - mu
    var = jnp.mean(zc * zc, axis=-1, keepdims=True)
    return zc * lax.rsqrt(var + LN_EPS) * g + b


def _mix_body(gla_ref, moba_ref, ga_ref, gb_ref, x_ref, wgo_ref, wmo_ref, wo_ref, g_ref, b_ref,
              h_ref, hb_ref, hp_ref):
    y_gla = _dot(gla_ref[...], wgo_ref[...])
    y_moba = _dot(moba_ref[...], wmo_ref[...])
    merged = _sigmoid(ga_ref[...].astype(F32)) * y_gla + _sigmoid(gb_ref[...].astype(F32)) * y_moba
    mix = _dot(merged.astype(BF16), wo_ref[...])
    h = _layer_norm(ALPHA * x_ref[...] + mix, g_ref[...], b_ref[...])
    h_ref[...] = h
    hb_ref[...] = h.astype(BF16)
    packed = _pack_bf16_pair(h)
    for c in range(PACKED_CHUNKS):
        hp_ref[:, c, :] = packed[:, c * LANES:(c + 1) * LANES]


def _mix(gla_out, moba_out, proj, x2d, w_gla_o, w_moba_o, w_out, ln_g, ln_b):
    d = D_MODEL
    row = lambda i: (i, 0)
    full = lambda i: (0, 0)
    return pl.pallas_call(
        _mix_body,
        grid=(N_TOK // MIX_TM,),
        in_specs=[
            pl.BlockSpec((MIX_TM, d), row),
            pl.BlockSpec((MIX_TM, d), row),
            pl.BlockSpec((MIX_TM, d), lambda i: (i, COL_GA // d)),
            pl.BlockSpec((MIX_TM, d), lambda i: (i, COL_GB // d)),
            pl.BlockSpec((MIX_TM, d), row),
            pl.BlockSpec((d, d), full),
            pl.BlockSpec((d, d), full),
            pl.BlockSpec((d, d), full),
            pl.BlockSpec((1, d), full),
            pl.BlockSpec((1, d), full),
        ],
        out_specs=[
            pl.BlockSpec((MIX_TM, d), row),
            pl.BlockSpec((MIX_TM, d), row),
            pl.BlockSpec((MIX_TM, PACKED_CHUNKS, LANES), lambda i: (i, 0, 0)),
        ],
        out_shape=[
            jax.ShapeDtypeStruct((N_TOK, d), F32),
            jax.ShapeDtypeStruct((N_TOK, d), BF16),
            jax.ShapeDtypeStruct((N_TOK, PACKED_CHUNKS, LANES), U32),
        ],
        compiler_params=_cparams("parallel"),
        name="mix_ln1",
    )(gla_out, moba_out, proj, proj, x2d, w_gla_o, w_moba_o, w_out, ln_g, ln_b)


def _route_body(h_ref, wr_ref, br_ref, e_ref, w_ref, rk_ref, cnt_ref, carry_ref):
    tm = ROUTE_TM

    @pl.when(pl.program_id(0) == 0)
    def _():
        carry_ref[...] = jnp.zeros_like(carry_ref)

    scores = _sigmoid(_dot_nt(wr_ref[...], h_ref[...]))
    biased = scores + br_ref[...]
    row = lax.broadcasted_iota(I32, (N_EXPERTS, tm), 0).astype(F32)
    row_g = lax.broadcasted_iota(I32, (GROUP_SIZE, tm), 0).astype(F32)

    g_scores = []
    for g in range(N_GROUPS):
        grp = biased[g * GROUP_SIZE:(g + 1) * GROUP_SIZE]
        m1 = jnp.max(grp, axis=0, keepdims=True)
        first = jnp.min(jnp.where(grp == m1, row_g, float(GROUP_SIZE)), axis=0, keepdims=True)
        m2 = jnp.max(jnp.where(row_g == first, _NEG_INF, grp), axis=0, keepdims=True)
        g_scores.append(m1 + m2)
    g_score = jnp.concatenate(g_scores, axis=0)
    g_row = lax.broadcasted_iota(I32, (N_GROUPS, tm), 0)
    g_beaten = jnp.zeros((N_GROUPS, tm), I32)
    for g in range(N_GROUPS):
        r = g_score[g:g + 1, :]
        g_beaten += jnp.where((r > g_score) | ((r == g_score) & (g < g_row)), 1, 0)
    g_keep = g_beaten < TOPK_GROUPS
    masked = jnp.concatenate(
        [jnp.where(g_keep[g:g + 1, :], biased[g * GROUP_SIZE:(g + 1) * GROUP_SIZE], _NEG_INF)
         for g in range(N_GROUPS)], axis=0)

    onehot = jnp.zeros((N_EXPERTS, tm), F32)
    picks, pick_scores = [], []
    for _ in range(TOPK_EXPERTS):
        m = jnp.max(masked, axis=0, keepdims=True)
        idx = jnp.min(jnp.where(masked == m, row, float(N_EXPERTS)), axis=0, keepdims=True)
        hit = row == idx
        picks.append(idx)
        pick_scores.append(jnp.sum(jnp.where(hit, scores, 0.0), axis=0, keepdims=True))
        onehot = onehot + jnp.where(hit, 1.0, 0.0)
        masked = jnp.where(hit, _NEG_INF, masked)
    sel = jnp.concatenate(pick_scores, axis=0)
    e_ref[...] = jnp.concatenate(picks, axis=0).astype(I32)
    w_ref[...] = sel / jnp.sum(sel, axis=0, keepdims=True) * ROUTED_SCALE

    t_r = lax.broadcasted_iota(I32, (tm, tm), 0)
    t_c = lax.broadcasted_iota(I32, (tm, tm), 1)
    earlier = jnp.where(t_r < t_c, 1.0, 0.0).astype(BF16)
    seen = _dot(onehot.astype(BF16), earlier) + carry_ref[...]
    rk_ref[...] = jnp.concatenate(
        [jnp.sum(jnp.where(row == idx, seen, 0.0), axis=0, keepdims=True) for idx in picks], axis=0).astype(I32)
    carry_ref[...] += jnp.sum(onehot, axis=1, keepdims=True)
    cnt_ref[...] = carry_ref[...]


def _route(h_bf, w_router_t, b_router_col):
    k = TOPK_EXPERTS
    tok = lambda i: (0, i)
    return pl.pallas_call(
        _route_body,
        grid=(N_TOK // ROUTE_TM,),
        in_specs=[
            pl.BlockSpec((ROUTE_TM, D_MODEL), lambda i: (i, 0)),
            pl.BlockSpec((N_EXPERTS, D_MODEL), lambda i: (0, 0)),
            pl.BlockSpec((N_EXPERTS, 1), lambda i: (0, 0)),
        ],
        out_specs=[
            pl.BlockSpec((k, ROUTE_TM), tok),
            pl.BlockSpec((k, ROUTE_TM), tok),
            pl.BlockSpec((k, ROUTE_TM), tok),
            pl.BlockSpec((N_EXPERTS, 1), lambda i: (0, 0)),
        ],
        out_shape=[
            jax.ShapeDtypeStruct((k, N_TOK), I32),
            jax.ShapeDtypeStruct((k, N_TOK), F32),
            jax.ShapeDtypeStruct((k, N_TOK), I32),
            jax.ShapeDtypeStruct((N_EXPERTS, 1), F32),
        ],
        scratch_shapes=[pltpu.VMEM((N_EXPERTS, 1), F32)],
        compiler_params=_cparams("arbitrary"),
        name="route",
    )(h_bf, w_router_t, b_router_col)


def _dest_body(cnt_ref, e_ref, rk_ref, d_ref, te_ref, nu_ref, lt_ref):
    cnt = cnt_ref[...]
    tiles = jnp.floor((cnt + (ROW_TILE - 1)) * (1.0 / ROW_TILE))
    er = lax.broadcasted_iota(I32, (N_EXPERTS, N_EXPERTS), 0)
    ec = lax.broadcasted_iota(I32, (N_EXPERTS, N_EXPERTS), 1)
    before = jnp.where(ec < er, 1.0, 0.0).astype(BF16)
    tiles_b = jnp.broadcast_to(tiles, (N_EXPERTS, LANES)).astype(BF16)
    t_start = _dot(before, tiles_b)[:, 0:1]
    t_end = t_start + tiles
    p_start = t_start * float(ROW_TILE)
    lt_ref[...] = jnp.where(tiles > 0.0, (t_end - 1.0) * float(ROW_TILE), -1.0).astype(I32)

    row = lax.broadcasted_iota(I32, (N_EXPERTS, DEST_TM), 0)
    d_ref[...] = jnp.concatenate(
        [jnp.sum(jnp.where(row == e_ref[k:k + 1, :], p_start, 0.0), axis=0, keepdims=True)
         for k in range(TOPK_EXPERTS)], axis=0).astype(I32) + rk_ref[...]

    tile_id = lax.broadcasted_iota(I32, (N_EXPERTS, TILE_TABLE), 1).astype(F32)
    owner = jnp.sum(jnp.where(t_end <= tile_id, 1, 0), axis=0, keepdims=True)
    te_ref[...] = jnp.minimum(owner, N_EXPERTS - 1)
    nu_ref[...] = jnp.broadcast_to(t_end[N_EXPERTS - 1:N_EXPERTS, :], (1, LANES)).astype(I32)


def _dest(cnt, e_t, rk_t):
    k = TOPK_EXPERTS
    tok = lambda i: (0, i)
    return pl.pallas_call(
        _dest_body,
        grid=(N_TOK // DEST_TM,),
        in_specs=[
            pl.BlockSpec((N_EXPERTS, 1), lambda i: (0, 0)),
            pl.BlockSpec((k, DEST_TM), tok),
            pl.BlockSpec((k, DEST_TM), tok),
        ],
        out_specs=[
            pl.BlockSpec((k, DEST_TM), tok),
            pl.BlockSpec((1, TILE_TABLE), lambda i: (0, 0)),
            pl.BlockSpec((1, LANES), lambda i: (0, 0)),
            pl.BlockSpec((N_EXPERTS, 1), lambda i: (0, 0)),
        ],
        out_shape=[
            jax.ShapeDtypeStruct((k, N_TOK), I32),
            jax.ShapeDtypeStruct((1, TILE_TABLE), I32),
            jax.ShapeDtypeStruct((1, LANES), I32),
            jax.ShapeDtypeStruct((N_EXPERTS, 1), I32),
        ],
        compiler_params=_cparams("arbitrary"),
        name="dest",
    )(cnt, e_t, rk_t)


def _row_copy(src, dst, sem):
    return pltpu.make_async_copy(src, dst, sem)


def _dispatch_body(lt_ref, nu_ref, d_ref, h_ref, xs_ref, zero_ref, sem):
    @pl.when(pl.program_id(0) == 0)
    def _():
        zero_ref[...] = jnp.zeros_like(zero_ref)

        def zero_tile(first_row):
            r0 = pl.multiple_of(first_row, ROW_TILE)
            return pltpu.make_async_copy(zero_ref, xs_ref.at[pl.ds(r0, ROW_TILE)], sem)

        def z_start(e, carry):
            @pl.when(lt_ref[e] >= 0)
            def _():
                zero_tile(jnp.maximum(lt_ref[e], 0)).start()
            return carry

        def z_wait(e, carry):
            @pl.when(lt_ref[e] >= 0)
            def _():
                zero_tile(jnp.maximum(lt_ref[e], 0)).wait()
            return carry

        def t_start(t, carry):
            zero_tile(t * ROW_TILE).start()
            return carry

        def t_wait(t, carry):
            zero_tile(t * ROW_TILE).wait()
            return carry

        lax.fori_loop(0, N_EXPERTS, z_start, 0)
        lax.fori_loop(nu_ref[0], N_ROW_TILES, t_start, 0)
        lax.fori_loop(0, N_EXPERTS, z_wait, 0)
        lax.fori_loop(nu_ref[0], N_ROW_TILES, t_wait, 0)

    for t in range(DISP_TM):
        for k in range(TOPK_EXPERTS):
            _row_copy(h_ref.at[t], xs_ref.at[d_ref[k, t]], sem).start(priority=k % 2)
    for k in range(TOPK_EXPERTS):
        pltpu.make_async_copy(h_ref, xs_ref.at[pl.ds(0, DISP_TM)], sem).wait()


def _dispatch(last_tile_row, n_used, dest_t, h1):
    grid_spec = pltpu.PrefetchScalarGridSpec(
        num_scalar_prefetch=2,
        grid=(N_TOK // DISP_TM,),
        in_specs=[
            pl.BlockSpec((TOPK_EXPERTS, DISP_TM), lambda i, lt, nu: (0, i), memory_space=pltpu.SMEM),
            pl.BlockSpec((DISP_TM, PACKED_CHUNKS, LANES), lambda i, lt, nu: (i, 0, 0)),
        ],
        out_specs=pl.BlockSpec(memory_space=pl.ANY),
        scratch_shapes=[pltpu.VMEM((ROW_TILE, PACKED_CHUNKS, LANES), U32), pltpu.SemaphoreType.DMA],
    )
    return pl.pallas_call(
        _dispatch_body,
        grid_spec=grid_spec,
        out_shape=jax.ShapeDtypeStruct((N_ROWS, PACKED_CHUNKS, LANES), U32),
        compiler_params=_cparams("arbitrary"),
        name="dispatch",
    )(last_tile_row, n_used, dest_t, h1)


def _row_destinations(dest_flat):
    info = plsc.get_sparse_core_info()
    n_cores, lanes = info.num_cores, info.num_lanes
    n_workers = n_cores * info.num_subcores
    rows_per_worker = -(-N_ROWS // (n_workers * ROW_TILE)) * ROW_TILE
    assert rows_per_worker % lanes == 0
    mesh = plsc.VectorSubcoreMesh(core_axis_name="c", subcore_axis_name="s")

    @functools.partial(
        pl.kernel, mesh=mesh, out_type=jax.ShapeDtypeStruct((n_workers * rows_per_worker,), I32),
        scratch_types=[pltpu.VMEM((SLOT_CHUNK,), I32), pltpu.VMEM((rows_per_worker,), I32)],
        compiler_params=pltpu.CompilerParams(needs_layout_passes=False), name="row_destinations")
    def invert(dest_hbm, out_hbm, dest_v, table_v):
        first_row = (lax.axis_index("s") * n_cores + lax.axis_index("c")) * rows_per_worker
        lane = lax.iota(I32, lanes)
        tile_shift = ROW_TILE.bit_length() - 1

        def spare(i, carry):
            row = first_row + i * lanes + lane
            buf = lax.rem(lax.shift_right_logical(row, tile_shift), EXPERT_BUFS)
            table_v[pl.ds(i * lanes, lanes)] = TOPK_EXPERTS * N_TOK + buf * ROW_TILE + (row & (ROW_TILE - 1))
            return carry

        lax.fori_loop(0, rows_per_worker // lanes, spare, 0)

        def chunk(c, carry):
            pltpu.sync_copy(dest_hbm.at[pl.ds(c * SLOT_CHUNK, SLOT_CHUNK)], dest_v)

            def vec(i, inner):
                local = dest_v[pl.ds(i * lanes, lanes)] - first_row
                mine = (local >= 0) & (local < rows_per_worker)
                plsc.store_scatter(table_v, [jnp.where(mine, local, 0)], lane + (c * SLOT_CHUNK + i * lanes),
                                   mask=mine)
                return inner

            lax.fori_loop(0, SLOT_CHUNK // lanes, vec, 0)
            return carry

        lax.fori_loop(0, N_ASSIGN // SLOT_CHUNK, chunk, 0)
        pltpu.sync_copy(table_v, out_hbm.at[pl.ds(first_row, rows_per_worker)])

    return invert(dest_flat)


def _experts_body(te_ref, nu_ref, lt_ref, slot_ref, xs_ref, wg_ref, wu_ref, wd_ref, yk_ref,
                  wg_f, wu_f, wd_f, wg_b, wu_b, wd_b, y_buf, n_loaded, sem, w_sem):
    i = pl.program_id(0)
    n_used = nu_ref[0]

    def weight_copies(e, s):
        return (pltpu.make_async_copy(wg_ref.at[e], wg_f.at[s], w_sem.at[s]),
                pltpu.make_async_copy(wu_ref.at[e], wu_f.at[s], w_sem.at[s]),
                pltpu.make_async_copy(wd_ref.at[e], wd_f.at[s], w_sem.at[s]))

    def wait_tile(b):
        pltpu.make_async_copy(y_buf.at[b], yk_ref.at[pl.ds(0, ROW_TILE), :], sem.at[b]).wait()

    def send_tile(b):
        for r in range(ROW_TILE):
            dst = slot_ref[0, 0, r]
            _row_copy(y_buf.at[b, pl.ds(r, 1), :], yk_ref.at[pl.ds(dst, 1), :], sem.at[b]).start(priority=r % 2)

    @pl.when(i == 0)
    def _():
        y_buf[EXPERT_BUFS - 1] = jnp.zeros((ROW_TILE, PACKED_COLS), U32)
        spare = [pltpu.make_async_copy(y_buf.at[EXPERT_BUFS - 1],
                                       yk_ref.at[pl.ds(TOPK_EXPERTS * N_TOK + b * ROW_TILE, ROW_TILE), :],
                                       sem.at[b]) for b in range(EXPERT_BUFS - 1)]
        for c in spare:
            c.start()
        for c in spare:
            c.wait()
        n_loaded[0] = 0
        for c in weight_copies(te_ref[0], 0):
            c.start()

    @pl.when((i >= 2) & (i <= n_used))
    def _():
        wait_tile(lax.rem(i, EXPERT_BUFS))

    @pl.when(i < n_used)
    def _():
        e = te_ref[i]
        prev = te_ref[jnp.maximum(i - 1, 0)]

        @pl.when((i == 0) | (e != prev))
        def _():
            s = lax.rem(n_loaded[0], 2)
            for c in weight_copies(e, s):
                c.wait()
            wg_b[...] = wg_f[s].astype(BF16)
            wu_b[...] = wu_f[s].astype(BF16)
            wd_b[...] = wd_f[s].astype(BF16)
            n_loaded[0] = n_loaded[0] + 1
            nxt = lax.shift_right_logical(lt_ref[e], ROW_TILE.bit_length() - 1) + 1

            @pl.when(nxt < n_used)
            def _():
                for c in weight_copies(te_ref[nxt], 1 - s):
                    c.start()

        for phase in range(EXPERT_BUFS):
            @pl.when(lax.rem(i, EXPERT_BUFS) == phase)
            def _(phase=phase):
                send_tile((phase + EXPERT_BUFS - 1) % EXPERT_BUFS)
                packed = jnp.concatenate([xs_ref[:, c, :] for c in range(PACKED_CHUNKS)], axis=1)
                x_lo, x_hi = _unpack_bf16_pair(packed)
                x = jnp.concatenate([x_lo.astype(BF16), x_hi.astype(BF16)], axis=1)
                g = _dot(x, wg_b[...])
                u = _dot(x, wu_b[...])
                h = (g * _sigmoid(g)) * u
                y_buf[phase] = _pack_bf16_pair(_dot(h.astype(BF16), wd_b[...]))

    @pl.when(i == n_used)
    def _():
        send_tile(lax.rem(i + EXPERT_BUFS - 1, EXPERT_BUFS))
        wait_tile(lax.rem(i + EXPERT_BUFS - 2, EXPERT_BUFS))
        wait_tile(lax.rem(i + EXPERT_BUFS - 1, EXPERT_BUFS))


def _experts(tile_expert, n_used, last_tile_row, row_dst, xs, w_gate, w_up, w_down):
    def tile(i, te, nu, lt):
        return (jnp.minimum(i, nu[0] - 1), 0, 0)

    n_table_tiles = row_dst.shape[0] // ROW_TILE
    placeholder = n_table_tiles - 1
    assert placeholder * ROW_TILE >= N_ROWS and placeholder % EXPERT_BUFS == EXPERT_BUFS - 1

    def prev_tile(i, te, nu, lt):
        return (jnp.where(i == 0, placeholder, jnp.minimum(i, nu[0]) - 1), 0, 0)

    grid_spec = pltpu.PrefetchScalarGridSpec(
        num_scalar_prefetch=3,
        grid=(N_ROW_TILES + 1,),
        in_specs=[
            pl.BlockSpec((1, 1, ROW_TILE), prev_tile, memory_space=pltpu.SMEM),
            pl.BlockSpec((ROW_TILE, PACKED_CHUNKS, LANES), tile),
            pl.BlockSpec(memory_space=pl.ANY),
            pl.BlockSpec(memory_space=pl.ANY),
            pl.BlockSpec(memory_space=pl.ANY),
        ],
        out_specs=pl.BlockSpec(memory_space=pl.ANY),
        scratch_shapes=[
            pltpu.VMEM((2, D_MODEL, D_EXPERT), F32),
            pltpu.VMEM((2, D_MODEL, D_EXPERT), F32),
            pltpu.VMEM((2, D_EXPERT, D_MODEL), F32),
            pltpu.VMEM((D_MODEL, D_EXPERT), BF16),
            pltpu.VMEM((D_MODEL, D_EXPERT), BF16),
            pltpu.VMEM((D_EXPERT, D_MODEL), BF16),
            pltpu.VMEM((EXPERT_BUFS, ROW_TILE, PACKED_COLS), U32),
            pltpu.SMEM((1,), I32),
            pltpu.SemaphoreType.DMA((EXPERT_BUFS,)),
            pltpu.SemaphoreType.DMA((2,)),
        ],
    )
    return pl.pallas_call(
        _experts_body,
        grid_spec=grid_spec,
        out_shape=jax.ShapeDtypeStruct((YK_ROWS, PACKED_COLS), U32),
        compiler_params=_cparams("arbitrary"),
        name="experts",
    )(tile_expert, n_used, last_tile_row, row_dst.reshape(n_table_tiles, 1, ROW_TILE), xs, w_gate, w_up, w_down)


def _final_body(w_ref, *refs):
    yk_refs = refs[:TOPK_EXPERTS]
    h_ref, hb_ref, p_ref, wsg_ref, wsu_ref, wsd_ref, wpl_ref, wpg_ref, g_ref, b_ref, o_ref = refs[TOPK_EXPERTS:]
    hb = hb_ref[...]
    sg = _dot(hb, wsg_ref[...])
    shared = _dot(((sg * _sigmoid(sg)) * _dot(hb, wsu_ref[...])).astype(BF16), wsd_ref[...])
    ple = _sigmoid(_dot(hb, wpg_ref[...])) * _dot(p_ref[...].astype(BF16), wpl_ref[...])

    w_col = w_ref[...].T
    y_lo, y_hi = _unpack_bf16_pair(yk_refs[0][...])
    r_lo, r_hi = y_lo * w_col[:, 0:1], y_hi * w_col[:, 0:1]
    for k in range(1, TOPK_EXPERTS):
        y_lo, y_hi = _unpack_bf16_pair(yk_refs[k][...])
        r_lo, r_hi = r_lo + y_lo * w_col[:, k:k + 1], r_hi + y_hi * w_col[:, k:k + 1]
    routed = jnp.concatenate([r_lo, r_hi], axis=1)
    o_ref[...] = _layer_norm(ALPHA * h_ref[...] + (routed + shared) + ple, g_ref[...], b_ref[...])


def _final(w_t, yk, h1, h_bf, p2d, w_s_gate, w_s_up, w_s_down, w_ple, w_ple_gate, ln_g, ln_b):
    d = D_MODEL
    n_steps = N_TOK // FINAL_TM
    full = lambda i: (0, 0)
    return pl.pallas_call(
        _final_body,
        grid=(n_steps,),
        in_specs=[
            pl.BlockSpec((TOPK_EXPERTS, FINAL_TM), lambda i: (0, i)),
            *[pl.BlockSpec((FINAL_TM, PACKED_COLS), lambda i, k=k: (k * n_steps + i, 0)) for k in range(TOPK_EXPERTS)],
            pl.BlockSpec((FINAL_TM, d), lambda i: (i, 0)),
            pl.BlockSpec((FINAL_TM, d), lambda i: (i, 0)),
            pl.BlockSpec((FINAL_TM, PLE_DIM), lambda i: (i, 0)),
            pl.BlockSpec((d, D_SHARED), full),
            pl.BlockSpec((d, D_SHARED), full),
            pl.BlockSpec((D_SHARED, d), full),
            pl.BlockSpec((PLE_DIM, d), full),
            pl.BlockSpec((d, d), full),
            pl.BlockSpec((1, d), full),
            pl.BlockSpec((1, d), full),
        ],
        out_specs=pl.BlockSpec((FINAL_TM, d), lambda i: (i, 0)),
        out_shape=jax.ShapeDtypeStruct((N_TOK, d), F32),
        compiler_params=_cparams("parallel"),
        name="combine_ln2",
    )(w_t, *([yk] * TOPK_EXPERTS), h1, h_bf, p2d, w_s_gate, w_s_up, w_s_down, w_ple, w_ple_gate, ln_g, ln_b)


def _rope_tables(positions):
    half = ROT_DIM // 2
    inv = ROPE_THETA ** (-jnp.arange(0, ROT_DIM, 2, dtype=F32) / ROT_DIM)
    per_row = LANES // half
    pos = jnp.repeat(positions.reshape(-1, per_row), half, axis=1).astype(F32)
    ang = pos * jnp.tile(inv, per_row)[None, :]
    cos, sin = lax.optimization_barrier((jnp.cos(ang), jnp.sin(ang)))
    cos = cos.reshape(positions.shape + (half,))
    sin = sin.reshape(positions.shape + (half,))
    rest = MOBA_DH - ROT_DIM
    ones = jnp.ones(cos.shape[:-1] + (rest,), F32)
    zeros = jnp.zeros(cos.shape[:-1] + (rest,), F32)
    return (jnp.concatenate([cos, cos, ones], axis=-1), jnp.concatenate([-sin, sin, zeros], axis=-1))


def _layer(h2d, p2d, cos_t, sin_t, w_cat, w_gk2, b_gk, norm_g, w_gla_o, w_moba_o, w_out, ln1_g, ln1_b,
           w_router, b_router, w_e_gate, w_e_up, w_e_down, w_s_gate, w_s_up, w_s_down, w_ple, w_ple_gate,
           ln2_g, ln2_b):
    w_gk2_pad = jnp.concatenate([w_gk2, jnp.zeros((LANES - GLA_RANK, GLA_KDIM), w_gk2.dtype)], axis=0)

    proj = _in_proj(h2d, w_cat)
    gla_out = _gla(proj, w_gk2_pad, b_gk[None, :], norm_g[None, :])
    moba_out = _moba(proj, cos_t, sin_t)
    h1, h_bf, h_packed = _mix(gla_out, moba_out, proj, h2d, w_gla_o.astype(BF16), w_moba_o.astype(BF16),
                              w_out.astype(BF16), ln1_g[None, :], ln1_b[None, :])
    e_t, w_t, rk_t, cnt = _route(h_bf, w_router.T.astype(BF16), b_router[:, None])
    dest_t, tile_expert, n_used, last_tile_row = _dest(cnt, e_t, rk_t)
    last_tile_row = last_tile_row.reshape(N_EXPERTS)
    n_used = n_used[0, 0:1]
    xs = _dispatch(last_tile_row, n_used, dest_t, h_packed)
    row_dst = _row_destinations(dest_t.reshape(N_ASSIGN))
    yk = _experts(tile_expert.reshape(TILE_TABLE), n_used, last_tile_row, row_dst, xs,
                  w_e_gate, w_e_up, w_e_down)
    return _final(w_t, yk, h1, h_bf, p2d, w_s_gate.astype(BF16), w_s_up.astype(BF16),
                  w_s_down.astype(BF16), w_ple.astype(BF16), w_ple_gate.astype(BF16),
                  ln2_g[None, :], ln2_b[None, :])


def kernel(x, p, positions, w_in, w_gla_gk2, b_gla_gk, gla_norm_g, w_gla_o, w_moba_o, w_out, ln1_g, ln1_b,
           w_router, b_router, w_e_gate, w_e_up, w_e_down, w_s_gate, w_s_up, w_s_down, w_ple, w_ple_gate,
           ln2_g, ln2_b):
    cos_t, sin_t = _rope_tables(positions)
    h = x.reshape(N_TOK, D_MODEL)
    for i in range(DEPTH):
        h = _layer(h, p[i].reshape(N_TOK, PLE_DIM), cos_t, sin_t, _repack_w_in(w_in, i), w_gla_gk2[i], b_gla_gk[i],
                   gla_norm_g[i], w_gla_o[i], w_moba_o[i], w_out[i], ln1_g[i], ln1_b[i], w_router[i],
                   b_router[i], w_e_gate[i], w_e_up[i], w_e_down[i], w_s_gate[i], w_s_up[i], w_s_down[i],
                   w_ple[i], w_ple_gate[i], ln2_g[i], ln2_b[i])
    return h.reshape(BATCH, SEQ, D_MODEL)
```

```python
import functools

import jax
import jax.numpy as jnp
from jax import lax
from jax.experimental import pallas as pl
from jax.experimental.pallas import tpu as pltpu
from jax.experimental.pallas import tpu_sc as plsc

F32 = jnp.float32
BF16 = jnp.bfloat16
I32 = jnp.int32
U32 = jnp.uint32

LANES = 128
SUBLANES = 8
VMEM_LIMIT_BYTES = 48 * 1024 * 1024

D_MODEL = 1024
BATCH = 8
SEQ = 2048
N_TOK = BATCH * SEQ
GLA_HEADS = 4
GLA_DK = 128
GLA_DV = 256
GLA_RANK = 16
GLA_NORMALIZER = 16.0
GLA_CHUNK = 64
GLA_GROUP = 2 * GLA_CHUNK
GLA_UNROLL = 8
MOBA_HEADS = 8
MOBA_DH = 128
MOBA_BLOCK = 256
MOBA_TOPK = 3
ROT_DIM = 32
ROPE_THETA = 500000.0
N_EXPERTS = 256
TOPK_EXPERTS = 8
N_GROUPS = 8
GROUP_SIZE = N_EXPERTS // N_GROUPS
TOPK_GROUPS = 4
D_EXPERT = 256
D_SHARED = 256
ROUTED_SCALE = 2.5
PLE_DIM = 256
LN_EPS = 1e-5
DEPTH = 1
ALPHA = (2.0 * DEPTH) ** 0.25
GLA_KDIM = GLA_HEADS * GLA_DK
GLA_VDIM = GLA_HEADS * GLA_DV
MOBA_DIM = MOBA_HEADS * MOBA_DH
N_KBLK = SEQ // MOBA_BLOCK

COL_GQ = 0
COL_GK = COL_GQ + GLA_KDIM
COL_GV = COL_GK + GLA_KDIM
COL_GR = COL_GV + GLA_VDIM
COL_MQ = COL_GR + GLA_VDIM
COL_MK = COL_MQ + MOBA_DIM
COL_MV = COL_MK + MOBA_DIM
COL_GA = COL_MV + MOBA_DIM
COL_GB = COL_GA + D_MODEL
COL_LOW = COL_GB + D_MODEL
PROJ_COLS = COL_LOW + LANES
LOW_SRC = 2 * GLA_KDIM + 2 * GLA_VDIM

ROW_TILE = 256
PACKED_COLS = D_MODEL // 2
PACKED_CHUNKS = PACKED_COLS // LANES
EXPERT_BUFS = 3
YK_ROWS = TOPK_EXPERTS * N_TOK + EXPERT_BUFS * ROW_TILE
SLOT_CHUNK = 4096
assert EXPERT_BUFS * ROW_TILE <= N_TOK
N_ASSIGN = N_TOK * TOPK_EXPERTS
N_ROW_TILES = (N_ASSIGN + N_EXPERTS * (ROW_TILE - 1) + ROW_TILE - 1) // ROW_TILE
N_ROWS = N_ROW_TILES * ROW_TILE
TILE_TABLE = -(-N_ROW_TILES // LANES) * LANES

PROJ_TM = 1024
REPACK_TN = 512
PROJ_TN = PROJ_COLS // 5
MIX_TM = 512
ROUTE_TM = 512
DEST_TM = 2048
DISP_TM = 256
FINAL_TM = 512

_NEG_INF = float("-inf")
LOG2_E = 1.4426950408889634


def _cparams(*sem):
    return pltpu.CompilerParams(dimension_semantics=sem, vmem_limit_bytes=VMEM_LIMIT_BYTES)


def _dot(a, b):
    return jnp.dot(a, b, preferred_element_type=F32)


def _dot_nt(a, b):
    return lax.dot_general(a, b, (((1,), (1,)), ((), ())), preferred_element_type=F32)


def _dot_tn(a, b):
    return lax.dot_general(a, b, (((0,), (0,)), ((), ())), preferred_element_type=F32)


def _split_bf16(x):
    hi = x.astype(BF16)
    lo = (x - hi.astype(F32)).astype(BF16)
    return hi, lo


def _sigmoid(x):
    return 1.0 / (1.0 + jnp.exp(-x))


def _pack_bf16_pair(x):
    lo = lax.bitcast_convert_type(x[:, :PACKED_COLS].astype(BF16).astype(F32), U32)
    hi = lax.bitcast_convert_type(x[:, PACKED_COLS:].astype(BF16).astype(F32), U32)
    return hi | lax.shift_right_logical(lo, jnp.uint32(16))


def _unpack_bf16_pair(w):
    lo = lax.bitcast_convert_type(lax.shift_left(w, jnp.uint32(16)), F32)
    hi = lax.bitcast_convert_type(w & jnp.uint32(0xFFFF0000), F32)
    return lo, hi


def _repack_body(a_ref, b_ref, o_ref):
    t = pl.program_id(0)
    a = a_ref[...].astype(BF16)

    @pl.when(t < LOW_SRC // REPACK_TN)
    def _():
        o_ref[...] = a

    @pl.when((t >= LOW_SRC // REPACK_TN) & (t < COL_LOW // REPACK_TN))
    def _():
        o_ref[...] = jnp.concatenate([a[:, GLA_RANK:], b_ref[:, :GLA_RANK].astype(BF16)], axis=1)

    @pl.when(t == COL_LOW // REPACK_TN)
    def _():
        o_ref[...] = jnp.concatenate([a[:, :GLA_RANK], jnp.zeros((D_MODEL, REPACK_TN - GLA_RANK), BF16)], axis=1)


def _repack_w_in(w_in, layer):
    assert LOW_SRC % REPACK_TN == 0 and COL_LOW % REPACK_TN == 0
    low_tile = LOW_SRC // REPACK_TN
    last_out = COL_LOW // REPACK_TN
    lanes_per_tile = REPACK_TN // LANES
    return pl.pallas_call(
        _repack_body,
        grid=(last_out + 1,),
        in_specs=[
            pl.BlockSpec((None, D_MODEL, REPACK_TN), lambda t: (layer, 0, jnp.where(t == last_out, low_tile, t))),
            pl.BlockSpec((None, D_MODEL, LANES), lambda t: (layer, 0, jnp.minimum(t + 1, last_out) * lanes_per_tile)),
        ],
        out_specs=pl.BlockSpec((D_MODEL, REPACK_TN), lambda t: (0, t)),
        out_shape=jax.ShapeDtypeStruct((D_MODEL, PROJ_COLS), BF16),
        compiler_params=_cparams("parallel"),
        name="repack_w_in",
    )(w_in, w_in)


def _proj_body(x_ref, w_ref, o_ref):
    o_ref[...] = _dot(x_ref[...].astype(BF16), w_ref[...]).astype(BF16)


def _in_proj(x2d, w_cat):
    return pl.pallas_call(
        _proj_body,
        grid=(PROJ_COLS // PROJ_TN, N_TOK // PROJ_TM),
        in_specs=[
            pl.BlockSpec((PROJ_TM, D_MODEL), lambda j, i: (i, 0)),
            pl.BlockSpec((D_MODEL, PROJ_TN), lambda j, i: (0, j)),
        ],
        out_specs=pl.BlockSpec((PROJ_TM, PROJ_TN), lambda j, i: (i, j)),
        out_shape=jax.ShapeDtypeStruct((N_TOK, PROJ_COLS), BF16),
        compiler_params=_cparams("parallel", "parallel"),
        name="in_proj",
    )(x2d, w_cat)


def _gla_body(q_ref, k_ref, v_ref, r_ref, low_ref, wg_ref, bg_ref, ng_ref, o_ref, st_ref, gk_ref):
    w_hi, w_lo = _split_bf16(wg_ref[...])
    low = low_ref[...]
    lin = _dot(low, w_hi) + _dot(low, w_lo) + bg_ref[...]
    gk_ref[...] = (jnp.minimum(lin, 0.0) - jnp.log1p(jnp.exp(-jnp.abs(lin)))) * (1.0 / GLA_NORMALIZER)
    st_ref[...] = jnp.zeros_like(st_ref)

    ri = lax.broadcasted_iota(I32, (GLA_GROUP, GLA_GROUP), 0)
    ci = lax.broadcasted_iota(I32, (GLA_GROUP, GLA_GROUP), 1)
    same_chunk = lax.shift_right_logical(ri, GLA_CHUNK.bit_length() - 1) == lax.shift_right_logical(
        ci, GLA_CHUNK.bit_length() - 1)
    causal = same_chunk & (ri >= ci)
    sums = jnp.concatenate([jnp.where(causal, 1.0, 0.0), jnp.where(same_chunk, 1.0, 0.0)], axis=0).astype(BF16)
    gain = ng_ref[...]

    def group(c, carry):
        rows = pl.ds(pl.multiple_of(c * GLA_GROUP, GLA_GROUP), GLA_GROUP)
        g_hi, g_lo = _split_bf16(gk_ref[rows, :])
        bb = _dot(sums, g_hi) + _dot(sums, g_lo)
        b = bb[0:GLA_GROUP]
        b_end = bb[GLA_GROUP:2 * GLA_GROUP]
        q = q_ref[rows, :].astype(F32) * (GLA_DK ** -0.5)
        k = k_ref[rows, :].astype(F32)
        v = v_ref[rows, :]
        q_e = (q * jnp.exp(b)).astype(BF16)
        k_e = (k * jnp.exp(-b)).astype(BF16)
        k_d = (k * jnp.exp(b_end - b)).astype(BF16)
        att = jnp.where(causal, _dot_nt(q_e, k_e), 0.0)
        o = _dot(att.astype(BF16), v)
        st = st_ref[...]
        inter = []
        for j in range(GLA_GROUP // GLA_CHUNK):
            cr = slice(j * GLA_CHUNK, (j + 1) * GLA_CHUNK)
            inter.append(_dot_nt(q_e[cr], st.astype(BF16)))
            st = st * jnp.exp(b_end[j * GLA_CHUNK:j * GLA_CHUNK + 1, :]) + _dot_tn(v[cr], k_d[cr])
        st_ref[...] = st
        o = o + jnp.concatenate(inter, axis=0)
        o = o * lax.rsqrt(jnp.mean(o * o, axis=-1, keepdims=True) + LN_EPS) * gain
        r = r_ref[rows, :].astype(F32)
        o_ref[rows, :] = (o * (r * _sigmoid(r))).astype(BF16)
        return carry

    lax.fori_loop(0, SEQ // GLA_GROUP, group, 0, unroll=GLA_UNROLL)


def _gla(proj, w_gk2_pad, b_gk, norm_g):
    kb, vb = GLA_DK, GLA_DV
    return pl.pallas_call(
        _gla_body,
        grid=(BATCH, GLA_HEADS),
        in_specs=[
            pl.BlockSpec((SEQ, kb), lambda b, h: (b, COL_GQ // kb + h)),
            pl.BlockSpec((SEQ, kb), lambda b, h: (b, COL_GK // kb + h)),
            pl.BlockSpec((SEQ, vb), lambda b, h: (b, COL_GV // vb + h)),
            pl.BlockSpec((SEQ, vb), lambda b, h: (b, COL_GR // vb + h)),
            pl.BlockSpec((SEQ, LANES), lambda b, h: (b, COL_LOW // LANES)),
            pl.BlockSpec((LANES, kb), lambda b, h: (0, h)),
            pl.BlockSpec((1, kb), lambda b, h: (0, h)),
            pl.BlockSpec((1, vb), lambda b, h: (0, 0)),
        ],
        out_specs=pl.BlockSpec((SEQ, vb), lambda b, h: (b, h)),
        out_shape=jax.ShapeDtypeStruct((N_TOK, GLA_VDIM), BF16),
        scratch_shapes=[pltpu.VMEM((vb, kb), F32), pltpu.VMEM((SEQ, kb), F32)],
        compiler_params=_cparams("parallel", "parallel"),
        name="gla",
    )(proj, proj, proj, proj, proj, w_gk2_pad, b_gk, norm_g)


def _moba_body(q_ref, k_ref, v_ref, c_ref, s_ref, o_ref, qs_ref, ks_ref, vt_ref, bias_ref,
               sc_a_ref, sc_b_ref, pr_a_ref, pr_b_ref):
    cos_t = c_ref[0]
    sin_t = s_ref[0]
    lane = lax.broadcasted_iota(I32, (SEQ, MOBA_DH), 1)
    half = ROT_DIM // 2

    def rope(x):
        partner = jnp.where(lane < half, pltpu.roll(x, MOBA_DH - half, 1), pltpu.roll(x, half, 1))
        return x * cos_t + partner * sin_t

    q = rope(q_ref[...].astype(F32))
    k = rope(k_ref[...].astype(F32))
    q_hi, q_lo = _split_bf16(q)
    qs_ref[...] = (q * (MOBA_DH ** -0.5 * LOG2_E)).astype(BF16)
    ks_ref[...] = k.astype(BF16)
    vt_ref[...] = v_ref[...].astype(F32).T.astype(BF16)

    k_mean = jnp.concatenate(
        [jnp.mean(k[j * MOBA_BLOCK:(j + 1) * MOBA_BLOCK], axis=0, keepdims=True) for j in range(N_KBLK)], axis=0)
    m_hi, m_lo = _split_bf16(k_mean)
    s_blk = _dot_nt(m_hi, q_hi) + _dot_nt(m_hi, q_lo) + _dot_nt(m_lo, q_hi)
    blk = lax.broadcasted_iota(I32, (N_KBLK, SEQ), 0)
    q_blk = lax.shift_right_logical(lax.broadcasted_iota(I32, (N_KBLK, SEQ), 1), MOBA_BLOCK.bit_length() - 1)
    past = blk < q_blk
    s_blk = jnp.where(past, s_blk, _NEG_INF)
    beaten = jnp.zeros((N_KBLK, SEQ), I32)
    for j in range(N_KBLK):
        row = s_blk[j:j + 1, :]
        beaten += jnp.where((row > s_blk) | ((row == s_blk) & (j < blk)), 1, 0)
    bias_ref[...] = jnp.where(past & (beaten < MOBA_TOPK), 0.0, _NEG_INF)

    kr = lax.broadcasted_iota(I32, (MOBA_BLOCK, MOBA_BLOCK), 0)
    qc = lax.broadcasted_iota(I32, (MOBA_BLOCK, MOBA_BLOCK), 1)
    own_bias = jnp.where(kr <= qc, 0.0, _NEG_INF)

    sc_bufs = (sc_a_ref, sc_b_ref)
    pr_bufs = (pr_a_ref, pr_b_ref)
    zeros_row = jnp.zeros((1, MOBA_BLOCK), F32)
    future_row = jnp.full((1, MOBA_BLOCK), _NEG_INF, F32)
    for pair in range(N_KBLK // 2):
        q_blocks = (2 * pair, 2 * pair + 1)
        cols = slice(q_blocks[0] * MOBA_BLOCK, (q_blocks[1] + 1) * MOBA_BLOCK)
        n_kblk = q_blocks[1] + 1
        n_keys = n_kblk * MOBA_BLOCK
        sc, pr = sc_bufs[pair % 2], pr_bufs[pair % 2]
        q_pair = qs_ref[cols, :]

        def query_bias(j):
            halves = []
            for qb in q_blocks:
                q_cols = slice(qb * MOBA_BLOCK, (qb + 1) * MOBA_BLOCK)
                halves.append(bias_ref[j:j + 1, q_cols] if j < qb else zeros_row if j == qb else future_row)
            return jnp.concatenate(halves, axis=1)

        biases = [query_bias(j) for j in range(n_kblk)]
        col_max = []
        for j in range(n_kblk):
            rows = slice(j * MOBA_BLOCK, (j + 1) * MOBA_BLOCK)
            s = _dot_nt(ks_ref[rows, :], q_pair)
            if j in q_blocks:
                h = q_blocks.index(j)
                own = s[:, h * MOBA_BLOCK:(h + 1) * MOBA_BLOCK] + own_bias
                s = jnp.concatenate([own, s[:, MOBA_BLOCK:]] if h == 0 else [s[:, :MOBA_BLOCK], own], axis=1)
            sc[rows, :] = s
            col_max.append(jnp.max(s, axis=0, keepdims=True) + biases[j])
        m = functools.reduce(jnp.maximum, col_max)
        denom = jnp.zeros((1, 2 * MOBA_BLOCK), F32)
        for j in range(n_kblk):
            rows = slice(j * MOBA_BLOCK, (j + 1) * MOBA_BLOCK)
            p = jnp.exp2(sc[rows, :] - (m - biases[j]))
            denom = denom + jnp.sum(p, axis=0, keepdims=True)
            pr[rows, :] = p.astype(BF16)
        o_t = _dot(vt_ref[:, 0:n_keys], pr[0:n_keys, :]) * (1.0 / denom)
        o_ref[cols, :] = o_t.T.astype(BF16)


def _moba(proj, cos_t, sin_t):
    dh = MOBA_DH
    return pl.pallas_call(
        _moba_body,
        grid=(BATCH, MOBA_HEADS),
        in_specs=[
            pl.BlockSpec((SEQ, dh), lambda b, h: (b, COL_MQ // dh + h)),
            pl.BlockSpec((SEQ, dh), lambda b, h: (b, COL_MK // dh + h)),
            pl.BlockSpec((SEQ, dh), lambda b, h: (b, COL_MV // dh + h)),
            pl.BlockSpec((1, SEQ, dh), lambda b, h: (b, 0, 0)),
            pl.BlockSpec((1, SEQ, dh), lambda b, h: (b, 0, 0)),
        ],
        out_specs=pl.BlockSpec((SEQ, dh), lambda b, h: (b, h)),
        out_shape=jax.ShapeDtypeStruct((N_TOK, MOBA_DIM), BF16),
        scratch_shapes=[
            pltpu.VMEM((SEQ, dh), BF16),
            pltpu.VMEM((SEQ, dh), BF16),
            pltpu.VMEM((dh, SEQ), BF16),
            pltpu.VMEM((N_KBLK, SEQ), F32),
            pltpu.VMEM((SEQ, 2 * MOBA_BLOCK), F32),
            pltpu.VMEM((SEQ, 2 * MOBA_BLOCK), F32),
            pltpu.VMEM((SEQ, 2 * MOBA_BLOCK), BF16),
            pltpu.VMEM((SEQ, 2 * MOBA_BLOCK), BF16),
        ],
        compiler_params=_cparams("parallel", "parallel"),
        name="moba",
    )(proj, proj, proj, cos_t, sin_t)


def _layer_norm(z, g, b):
    mu = jnp.mean(z, axis=-1, keepdims=True)
    zc = z - mu
    var = jnp.mean(zc * zc, axis=-1, keepdims=True)
    return zc * lax.rsqrt(var + LN_EPS) * g + b


def _mix_body(gla_ref, moba_ref, ga_ref, gb_ref, x_ref, wgo_ref, wmo_ref, wo_ref, g_ref, b_ref,
              h_ref, hb_ref, hp_ref):
    y_gla = _dot(gla_ref[...], wgo_ref[...])
    y_moba = _dot(moba_ref[...], wmo_ref[...])
    merged = _sigmoid(ga_ref[...].astype(F32)) * y_gla + _sigmoid(gb_ref[...].astype(F32)) * y_moba
    mix = _dot(merged.astype(BF16), wo_ref[...])
    h = _layer_norm(ALPHA * x_ref[...] + mix, g_ref[...], b_ref[...])
    h_ref[...] = h
    hb_ref[...] = h.astype(BF16)
    packed = _pack_bf16_pair(h)
    for c in range(PACKED_CHUNKS):
        hp_ref[:, c, :] = packed[:, c * LANES:(c + 1) * LANES]


def _mix(gla_out, moba_out, proj, x2d, w_gla_o, w_moba_o, w_out, ln_g, ln_b):
    d = D_MODEL
    row = lambda i: (i, 0)
    full = lambda i: (0, 0)
    return pl.pallas_call(
        _mix_body,
        grid=(N_TOK // MIX_TM,),
        in_specs=[
            pl.BlockSpec((MIX_TM, d), row),
            pl.BlockSpec((MIX_TM, d), row),
            pl.BlockSpec((MIX_TM, d), lambda i: (i, COL_GA // d)),
            pl.BlockSpec((MIX_TM, d), lambda i: (i, COL_GB // d)),
            pl.BlockSpec((MIX_TM, d), row),
            pl.BlockSpec((d, d), full),
            pl.BlockSpec((d, d), full),
            pl.BlockSpec((d, d), full),
            pl.BlockSpec((1, d), full),
            pl.BlockSpec((1, d), full),
        ],
        out_specs=[
            pl.BlockSpec((MIX_TM, d), row),
            pl.BlockSpec((MIX_TM, d), row),
            pl.BlockSpec((MIX_TM, PACKED_CHUNKS, LANES), lambda i: (i, 0, 0)),
        ],
        out_shape=[
            jax.ShapeDtypeStruct((N_TOK, d), F32),
            jax.ShapeDtypeStruct((N_TOK, d), BF16),
            jax.ShapeDtypeStruct((N_TOK, PACKED_CHUNKS, LANES), U32),
        ],
        compiler_params=_cparams("parallel"),
        name="mix_ln1",
    )(gla_out, moba_out, proj, proj, x2d, w_gla_o, w_moba_o, w_out, ln_g, ln_b)


def _route_body(h_ref, wr_ref, br_ref, e_ref, w_ref, rk_ref, cnt_ref, carry_ref):
    tm = ROUTE_TM

    @pl.when(pl.program_id(0) == 0)
    def _():
        carry_ref[...] = jnp.zeros_like(carry_ref)

    scores = _sigmoid(_dot_nt(wr_ref[...], h_ref[...]))
    biased = scores + br_ref[...]
    row = lax.broadcasted_iota(I32, (N_EXPERTS, tm), 0).astype(F32)
    row_g = lax.broadcasted_iota(I32, (GROUP_SIZE, tm), 0).astype(F32)

    g_scores = []
    for g in range(N_GROUPS):
        grp = biased[g * GROUP_SIZE:(g + 1) * GROUP_SIZE]
        m1 = jnp.max(grp, axis=0, keepdims=True)
        first = jnp.min(jnp.where(grp == m1, row_g, float(GROUP_SIZE)), axis=0, keepdims=True)
        m2 = jnp.max(jnp.where(row_g == first, _NEG_INF, grp), axis=0, keepdims=True)
        g_scores.append(m1 + m2)
    g_score = jnp.concatenate(g_scores, axis=0)
    g_row = lax.broadcasted_iota(I32, (N_GROUPS, tm), 0)
    g_beaten = jnp.zeros((N_GROUPS, tm), I32)
    for g in range(N_GROUPS):
        r = g_score[g:g + 1, :]
        g_beaten += jnp.where((r > g_score) | ((r == g_score) & (g < g_row)), 1, 0)
    g_keep = g_beaten < TOPK_GROUPS
    masked = jnp.concatenate(
        [jnp.where(g_keep[g:g + 1, :], biased[g * GROUP_SIZE:(g + 1) * GROUP_SIZE], _NEG_INF)
         for g in range(N_GROUPS)], axis=0)

    onehot = jnp.zeros((N_EXPERTS, tm), F32)
    picks, pick_scores = [], []
    for _ in range(TOPK_EXPERTS):
        m = jnp.max(masked, axis=0, keepdims=True)
        idx = jnp.min(jnp.where(masked == m, row, float(N_EXPERTS)), axis=0, keepdims=True)
        hit = row == idx
        picks.append(idx)
        pick_scores.append(jnp.sum(jnp.where(hit, scores, 0.0), axis=0, keepdims=True))
        onehot = onehot + jnp.where(hit, 1.0, 0.0)
        masked = jnp.where(hit, _NEG_INF, masked)
    sel = jnp.concatenate(pick_scores, axis=0)
    e_ref[...] = jnp.concatenate(picks, axis=0).astype(I32)
    w_ref[...] = sel / jnp.sum(sel, axis=0, keepdims=True) * ROUTED_SCALE

    t_r = lax.broadcasted_iota(I32, (tm, tm), 0)
    t_c = lax.broadcasted_iota(I32, (tm, tm), 1)
    earlier = jnp.where(t_r < t_c, 1.0, 0.0).astype(BF16)
    seen = _dot(onehot.astype(BF16), earlier) + carry_ref[...]
    rk_ref[...] = jnp.concatenate(
        [jnp.sum(jnp.where(row == idx, seen, 0.0), axis=0, keepdims=True) for idx in picks], axis=0).astype(I32)
    carry_ref[...] += jnp.sum(onehot, axis=1, keepdims=True)
    cnt_ref[...] = carry_ref[...]


def _route(h_bf, w_router_t, b_router_col):
    k = TOPK_EXPERTS
    tok = lambda i: (0, i)
    return pl.pallas_call(
        _route_body,
        grid=(N_TOK // ROUTE_TM,),
        in_specs=[
            pl.BlockSpec((ROUTE_TM, D_MODEL), lambda i: (i, 0)),
            pl.BlockSpec((N_EXPERTS, D_MODEL), lambda i: (0, 0)),
            pl.BlockSpec((N_EXPERTS, 1), lambda i: (0, 0)),
        ],
        out_specs=[
            pl.BlockSpec((k, ROUTE_TM), tok),
            pl.BlockSpec((k, ROUTE_TM), tok),
            pl.BlockSpec((k, ROUTE_TM), tok),
            pl.BlockSpec((N_EXPERTS, 1), lambda i: (0, 0)),
        ],
        out_shape=[
            jax.ShapeDtypeStruct((k, N_TOK), I32),
            jax.ShapeDtypeStruct((k, N_TOK), F32),
            jax.ShapeDtypeStruct((k, N_TOK), I32),
            jax.ShapeDtypeStruct((N_EXPERTS, 1), F32),
        ],
        scratch_shapes=[pltpu.VMEM((N_EXPERTS, 1), F32)],
        compiler_params=_cparams("arbitrary"),
        name="route",
    )(h_bf, w_router_t, b_router_col)


def _dest_body(cnt_ref, e_ref, rk_ref, d_ref, te_ref, nu_ref, lt_ref):
    cnt = cnt_ref[...]
    tiles = jnp.floor((cnt + (ROW_TILE - 1)) * (1.0 / ROW_TILE))
    er = lax.broadcasted_iota(I32, (N_EXPERTS, N_EXPERTS), 0)
    ec = lax.broadcasted_iota(I32, (N_EXPERTS, N_EXPERTS), 1)
    before = jnp.where(ec < er, 1.0, 0.0).astype(BF16)
    tiles_b = jnp.broadcast_to(tiles, (N_EXPERTS, LANES)).astype(BF16)
    t_start = _dot(before, tiles_b)[:, 0:1]
    t_end = t_start + tiles
    p_start = t_start * float(ROW_TILE)
    lt_ref[...] = jnp.where(tiles > 0.0, (t_end - 1.0) * float(ROW_TILE), -1.0).astype(I32)

    row = lax.broadcasted_iota(I32, (N_EXPERTS, DEST_TM), 0)
    d_ref[...] = jnp.concatenate(
        [jnp.sum(jnp.where(row == e_ref[k:k + 1, :], p_start, 0.0), axis=0, keepdims=True)
         for k in range(TOPK_EXPERTS)], axis=0).astype(I32) + rk_ref[...]

    tile_id = lax.broadcasted_iota(I32, (N_EXPERTS, TILE_TABLE), 1).astype(F32)
    owner = jnp.sum(jnp.where(t_end <= tile_id, 1, 0), axis=0, keepdims=True)
    te_ref[...] = jnp.minimum(owner, N_EXPERTS - 1)
    nu_ref[...] = jnp.broadcast_to(t_end[N_EXPERTS - 1:N_EXPERTS, :], (1, LANES)).astype(I32)


def _dest(cnt, e_t, rk_t):
    k = TOPK_EXPERTS
    tok = lambda i: (0, i)
    return pl.pallas_call(
        _dest_body,
        grid=(N_TOK // DEST_TM,),
        in_specs=[
            pl.BlockSpec((N_EXPERTS, 1), lambda i: (0, 0)),
            pl.BlockSpec((k, DEST_TM), tok),
            pl.BlockSpec((k, DEST_TM), tok),
        ],
        out_specs=[
            pl.BlockSpec((k, DEST_TM), tok),
            pl.BlockSpec((1, TILE_TABLE), lambda i: (0, 0)),
            pl.BlockSpec((1, LANES), lambda i: (0, 0)),
            pl.BlockSpec((N_EXPERTS, 1), lambda i: (0, 0)),
        ],
        out_shape=[
            jax.ShapeDtypeStruct((k, N_TOK), I32),
            jax.ShapeDtypeStruct((1, TILE_TABLE), I32),
            jax.ShapeDtypeStruct((1, LANES), I32),
            jax.ShapeDtypeStruct((N_EXPERTS, 1), I32),
        ],
        compiler_params=_cparams("arbitrary"),
        name="dest",
    )(cnt, e_t, rk_t)


def _row_copy(src, dst, sem):
    return pltpu.make_async_copy(src, dst, sem)


def _dispatch_body(lt_ref, nu_ref, d_ref, h_ref, xs_ref, zero_ref, sem):
    @pl.when(pl.program_id(0) == 0)
    def _():
        zero_ref[...] = jnp.zeros_like(zero_ref)

        def zero_tile(first_row):
            r0 = pl.multiple_of(first_row, ROW_TILE)
            return pltpu.make_async_copy(zero_ref, xs_ref.at[pl.ds(r0, ROW_TILE)], sem)

        def z_start(e, carry):
            @pl.when(lt_ref[e] >= 0)
            def _():
                zero_tile(jnp.maximum(lt_ref[e], 0)).start()
            return carry

        def z_wait(e, carry):
            @pl.when(lt_ref[e] >= 0)
            def _():
                zero_tile(jnp.maximum(lt_ref[e], 0)).wait()
            return carry

        def t_start(t, carry):
            zero_tile(t * ROW_TILE).start()
            return carry

        def t_wait(t, carry):
            zero_tile(t * ROW_TILE).wait()
            return carry

        lax.fori_loop(0, N_EXPERTS, z_start, 0)
        lax.fori_loop(nu_ref[0], N_ROW_TILES, t_start, 0)
        lax.fori_loop(0, N_EXPERTS, z_wait, 0)
        lax.fori_loop(nu_ref[0], N_ROW_TILES, t_wait, 0)

    for t in range(DISP_TM):
        for k in range(TOPK_EXPERTS):
            _row_copy(h_ref.at[t], xs_ref.at[d_ref[k, t]], sem).start(priority=k % 2)
    for k in range(TOPK_EXPERTS):
        pltpu.make_async_copy(h_ref, xs_ref.at[pl.ds(0, DISP_TM)], sem).wait()


def _dispatch(last_tile_row, n_used, dest_t, h1):
    grid_spec = pltpu.PrefetchScalarGridSpec(
        num_scalar_prefetch=2,
        grid=(N_TOK // DISP_TM,),
        in_specs=[
            pl.BlockSpec((TOPK_EXPERTS, DISP_TM), lambda i, lt, nu: (0, i), memory_space=pltpu.SMEM),
            pl.BlockSpec((DISP_TM, PACKED_CHUNKS, LANES), lambda i, lt, nu: (i, 0, 0)),
        ],
        out_specs=pl.BlockSpec(memory_space=pl.ANY),
        scratch_shapes=[pltpu.VMEM((ROW_TILE, PACKED_CHUNKS, LANES), U32), pltpu.SemaphoreType.DMA],
    )
    return pl.pallas_call(
        _dispatch_body,
        grid_spec=grid_spec,
        out_shape=jax.ShapeDtypeStruct((N_ROWS, PACKED_CHUNKS, LANES), U32),
        compiler_params=_cparams("arbitrary"),
        name="dispatch",
    )(last_tile_row, n_used, dest_t, h1)


def _row_destinations(dest_flat):
    info = plsc.get_sparse_core_info()
    n_cores, lanes = info.num_cores, info.num_lanes
    n_workers = n_cores * info.num_subcores
    rows_per_worker = -(-N_ROWS // (n_workers * ROW_TILE)) * ROW_TILE
    assert rows_per_worker % lanes == 0
    mesh = plsc.VectorSubcoreMesh(core_axis_name="c", subcore_axis_name="s")

    @functools.partial(
        pl.kernel, mesh=mesh, out_type=jax.ShapeDtypeStruct((n_workers * rows_per_worker,), I32),
        scratch_types=[pltpu.VMEM((SLOT_CHUNK,), I32), pltpu.VMEM((rows_per_worker,), I32)],
        compiler_params=pltpu.CompilerParams(needs_layout_passes=False), name="row_destinations")
    def invert(dest_hbm, out_hbm, dest_v, table_v):
        first_row = (lax.axis_index("s") * n_cores + lax.axis_index("c")) * rows_per_worker
        lane = lax.iota(I32, lanes)
        tile_shift = ROW_TILE.bit_length() - 1

        def spare(i, carry):
            row = first_row + i * lanes + lane
            buf = lax.rem(lax.shift_right_logical(row, tile_shift), EXPERT_BUFS)
            table_v[pl.ds(i * lanes, lanes)] = TOPK_EXPERTS * N_TOK + buf * ROW_TILE + (row & (ROW_TILE - 1))
            return carry

        lax.fori_loop(0, rows_per_worker // lanes, spare, 0)

        def chunk(c, carry):
            pltpu.sync_copy(dest_hbm.at[pl.ds(c * SLOT_CHUNK, SLOT_CHUNK)], dest_v)

            def vec(i, inner):
                local = dest_v[pl.ds(i * lanes, lanes)] - first_row
                mine = (local >= 0) & (local < rows_per_worker)
                plsc.store_scatter(table_v, [jnp.where(mine, local, 0)], lane + (c * SLOT_CHUNK + i * lanes),
                                   mask=mine)
                return inner

            lax.fori_loop(0, SLOT_CHUNK // lanes, vec, 0)
            return carry

        lax.fori_loop(0, N_ASSIGN // SLOT_CHUNK, chunk, 0)
        pltpu.sync_copy(table_v, out_hbm.at[pl.ds(first_row, rows_per_worker)])

    return invert(dest_flat)


def _experts_body(te_ref, nu_ref, lt_ref, slot_ref, xs_ref, wg_ref, wu_ref, wd_ref, yk_ref,
                  wg_f, wu_f, wd_f, wg_b, wu_b, wd_b, y_buf, n_loaded, sem, w_sem):
    i = pl.program_id(0)
    n_used = nu_ref[0]

    def weight_copies(e, s):
        return (pltpu.make_async_copy(wg_ref.at[e], wg_f.at[s], w_sem.at[s]),
                pltpu.make_async_copy(wu_ref.at[e], wu_f.at[s], w_sem.at[s]),
                pltpu.make_async_copy(wd_ref.at[e], wd_f.at[s], w_sem.at[s]))

    def wait_tile(b):
        pltpu.make_async_copy(y_buf.at[b], yk_ref.at[pl.ds(0, ROW_TILE), :], sem.at[b]).wait()

    def send_tile(b):
        for r in range(ROW_TILE):
            dst = slot_ref[0, 0, r]
            _row_copy(y_buf.at[b, pl.ds(r, 1), :], yk_ref.at[pl.ds(dst, 1), :], sem.at[b]).start(priority=r % 2)

    @pl.when(i == 0)
    def _():
        y_buf[EXPERT_BUFS - 1] = jnp.zeros((ROW_TILE, PACKED_COLS), U32)
        spare = [pltpu.make_async_copy(y_buf.at[EXPERT_BUFS - 1],
                                       yk_ref.at[pl.ds(TOPK_EXPERTS * N_TOK + b * ROW_TILE, ROW_TILE), :],
                                       sem.at[b]) for b in range(EXPERT_BUFS - 1)]
        for c in spare:
            c.start()
        for c in spare:
            c.wait()
        n_loaded[0] = 0
        for c in weight_copies(te_ref[0], 0):
            c.start()

    @pl.when((i >= 2) & (i <= n_used))
    def _():
        wait_tile(lax.rem(i, EXPERT_BUFS))

    @pl.when(i < n_used)
    def _():
        e = te_ref[i]
        prev = te_ref[jnp.maximum(i - 1, 0)]

        @pl.when((i == 0) | (e != prev))
        def _():
            s = lax.rem(n_loaded[0], 2)
            for c in weight_copies(e, s):
                c.wait()
            wg_b[...] = wg_f[s].astype(BF16)
            wu_b[...] = wu_f[s].astype(BF16)
            wd_b[...] = wd_f[s].astype(BF16)
            n_loaded[0] = n_loaded[0] + 1
            nxt = lax.shift_right_logical(lt_ref[e], ROW_TILE.bit_length() - 1) + 1

            @pl.when(nxt < n_used)
            def _():
                for c in weight_copies(te_ref[nxt], 1 - s):
                    c.start()

        for phase in range(EXPERT_BUFS):
            @pl.when(lax.rem(i, EXPERT_BUFS) == phase)
            def _(phase=phase):
                send_tile((phase + EXPERT_BUFS - 1) % EXPERT_BUFS)
                packed = jnp.concatenate([xs_ref[:, c, :] for c in range(PACKED_CHUNKS)], axis=1)
                x_lo, x_hi = _unpack_bf16_pair(packed)
                x = jnp.concatenate([x_lo.astype(BF16), x_hi.astype(BF16)], axis=1)
                g = _dot(x, wg_b[...])
                u = _dot(x, wu_b[...])
                h = (g * _sigmoid(g)) * u
                y_buf[phase] = _pack_bf16_pair(_dot(h.astype(BF16), wd_b[...]))

    @pl.when(i == n_used)
    def _():
        send_tile(lax.rem(i + EXPERT_BUFS - 1, EXPERT_BUFS))
        wait_tile(lax.rem(i + EXPERT_BUFS - 2, EXPERT_BUFS))
        wait_tile(lax.rem(i + EXPERT_BUFS - 1, EXPERT_BUFS))


def _experts(tile_expert, n_used, last_tile_row, row_dst, xs, w_gate, w_up, w_down):
    def tile(i, te, nu, lt):
        return (jnp.minimum(i, nu[0] - 1), 0, 0)

    n_table_tiles = row_dst.shape[0] // ROW_TILE
    placeholder = n_table_tiles - 1
    assert placeholder * ROW_TILE >= N_ROWS and placeholder % EXPERT_BUFS == EXPERT_BUFS - 1

    def prev_tile(i, te, nu, lt):
        return (jnp.where(i == 0, placeholder, jnp.minimum(i, nu[0]) - 1), 0, 0)

    grid_spec = pltpu.PrefetchScalarGridSpec(
        num_scalar_prefetch=3,
        grid=(N_ROW_TILES + 1,),
        in_specs=[
            pl.BlockSpec((1, 1, ROW_TILE), prev_tile, memory_space=pltpu.SMEM),
            pl.BlockSpec((ROW_TILE, PACKED_CHUNKS, LANES), tile),
            pl.BlockSpec(memory_space=pl.ANY),
            pl.BlockSpec(memory_space=pl.ANY),
            pl.BlockSpec(memory_space=pl.ANY),
        ],
        out_specs=pl.BlockSpec(memory_space=pl.ANY),
        scratch_shapes=[
            pltpu.VMEM((2, D_MODEL, D_EXPERT), F32),
            pltpu.VMEM((2, D_MODEL, D_EXPERT), F32),
            pltpu.VMEM((2, D_EXPERT, D_MODEL), F32),
            pltpu.VMEM((D_MODEL, D_EXPERT), BF16),
            pltpu.VMEM((D_MODEL, D_EXPERT), BF16),
            pltpu.VMEM((D_EXPERT, D_MODEL), BF16),
            pltpu.VMEM((EXPERT_BUFS, ROW_TILE, PACKED_COLS), U32),
            pltpu.SMEM((1,), I32),
            pltpu.SemaphoreType.DMA((EXPERT_BUFS,)),
            pltpu.SemaphoreType.DMA((2,)),
        ],
    )
    return pl.pallas_call(
        _experts_body,
        grid_spec=grid_spec,
        out_shape=jax.ShapeDtypeStruct((YK_ROWS, PACKED_COLS), U32),
        compiler_params=_cparams("arbitrary"),
        name="experts",
    )(tile_expert, n_used, last_tile_row, row_dst.reshape(n_table_tiles, 1, ROW_TILE), xs, w_gate, w_up, w_down)


def _final_body(w_ref, *refs):
    yk_refs = refs[:TOPK_EXPERTS]
    h_ref, hb_ref, p_ref, wsg_ref, wsu_ref, wsd_ref, wpl_ref, wpg_ref, g_ref, b_ref, o_ref = refs[TOPK_EXPERTS:]
    hb = hb_ref[...]
    sg = _dot(hb, wsg_ref[...])
    shared = _dot(((sg * _sigmoid(sg)) * _dot(hb, wsu_ref[...])).astype(BF16), wsd_ref[...])
    ple = _sigmoid(_dot(hb, wpg_ref[...])) * _dot(p_ref[...].astype(BF16), wpl_ref[...])

    w_col = w_ref[...].T
    y_lo, y_hi = _unpack_bf16_pair(yk_refs[0][...])
    r_lo, r_hi = y_lo * w_col[:, 0:1], y_hi * w_col[:, 0:1]
    for k in range(1, TOPK_EXPERTS):
        y_lo, y_hi = _unpack_bf16_pair(yk_refs[k][...])
        r_lo, r_hi = r_lo + y_lo * w_col[:, k:k + 1], r_hi + y_hi * w_col[:, k:k + 1]
    routed = jnp.concatenate([r_lo, r_hi], axis=1)
    o_ref[...] = _layer_norm(ALPHA * h_ref[...] + (routed + shared) + ple, g_ref[...], b_ref[...])


def _final(w_t, yk, h1, h_bf, p2d, w_s_gate, w_s_up, w_s_down, w_ple, w_ple_gate, ln_g, ln_b):
    d = D_MODEL
    n_steps = N_TOK // FINAL_TM
    full = lambda i: (0, 0)
    return pl.pallas_call(
        _final_body,
        grid=(n_steps,),
        in_specs=[
            pl.BlockSpec((TOPK_EXPERTS, FINAL_TM), lambda i: (0, i)),
            *[pl.BlockSpec((FINAL_TM, PACKED_COLS), lambda i, k=k: (k * n_steps + i, 0)) for k in range(TOPK_EXPERTS)],
            pl.BlockSpec((FINAL_TM, d), lambda i: (i, 0)),
            pl.BlockSpec((FINAL_TM, d), lambda i: (i, 0)),
            pl.BlockSpec((FINAL_TM, PLE_DIM), lambda i: (i, 0)),
            pl.BlockSpec((d, D_SHARED), full),
            pl.BlockSpec((d, D_SHARED), full),
            pl.BlockSpec((D_SHARED, d), full),
            pl.BlockSpec((PLE_DIM, d), full),
            pl.BlockSpec((d, d), full),
            pl.BlockSpec((1, d), full),
            pl.BlockSpec((1, d), full),
        ],
        out_specs=pl.BlockSpec((FINAL_TM, d), lambda i: (i, 0)),
        out_shape=jax.ShapeDtypeStruct((N_TOK, d), F32),
        compiler_params=_cparams("parallel"),
        name="combine_ln2",
    )(w_t, *([yk] * TOPK_EXPERTS), h1, h_bf, p2d, w_s_gate, w_s_up, w_s_down, w_ple, w_ple_gate, ln_g, ln_b)


def _rope_tables(positions):
    half = ROT_DIM // 2
    inv = ROPE_THETA ** (-jnp.arange(0, ROT_DIM, 2, dtype=F32) / ROT_DIM)
    per_row = LANES // half
    pos = jnp.repeat(positions.reshape(-1, per_row), half, axis=1).astype(F32)
    ang = pos * jnp.tile(inv, per_row)[None, :]
    cos, sin = lax.optimization_barrier((jnp.cos(ang), jnp.sin(ang)))
    cos = cos.reshape(positions.shape + (half,))
    sin = sin.reshape(positions.shape + (half,))
    rest = MOBA_DH - ROT_DIM
    ones = jnp.ones(cos.shape[:-1] + (rest,), F32)
    zeros = jnp.zeros(cos.shape[:-1] + (rest,), F32)
    return (jnp.concatenate([cos, cos, ones], axis=-1), jnp.concatenate([-sin, sin, zeros], axis=-1))


def _layer(h2d, p2d, cos_t, sin_t, w_cat, w_gk2, b_gk, norm_g, w_gla_o, w_moba_o, w_out, ln1_g, ln1_b,
           w_router, b_router, w_e_gate, w_e_up, w_e_down, w_s_gate, w_s_up, w_s_down, w_ple, w_ple_gate,
           ln2_g, ln2_b):
    w_gk2_pad = jnp.concatenate([w_gk2, jnp.zeros((LANES - GLA_RANK, GLA_KDIM), w_gk2.dtype)], axis=0)

    proj = _in_proj(h2d, w_cat)
    gla_out = _gla(proj, w_gk2_pad, b_gk[None, :], norm_g[None, :])
    moba_out = _moba(proj, cos_t, sin_t)
    h1, h_bf, h_packed = _mix(gla_out, moba_out, proj, h2d, w_gla_o.astype(BF16), w_moba_o.astype(BF16),
                              w_out.astype(BF16), ln1_g[None, :], ln1_b[None, :])
    e_t, w_t, rk_t, cnt = _route(h_bf, w_router.T.astype(BF16), b_router[:, None])
    dest_t, tile_expert, n_used, last_tile_row = _dest(cnt, e_t, rk_t)
    last_tile_row = last_tile_row.reshape(N_EXPERTS)
    n_used = n_used[0, 0:1]
    xs = _dispatch(last_tile_row, n_used, dest_t, h_packed)
    row_dst = _row_destinations(dest_t.reshape(N_ASSIGN))
    yk = _experts(tile_expert.reshape(TILE_TABLE), n_used, last_tile_row, row_dst, xs,
                  w_e_gate, w_e_up, w_e_down)
    return _final(w_t, yk, h1, h_bf, p2d, w_s_gate.astype(BF16), w_s_up.astype(BF16),
                  w_s_down.astype(BF16), w_ple.astype(BF16), w_ple_gate.astype(BF16),
                  ln2_g[None, :], ln2_b[None, :])


def kernel(x, p, positions, w_in, w_gla_gk2, b_gla_gk, gla_norm_g, w_gla_o, w_moba_o, w_out, ln1_g, ln1_b,
           w_router, b_router, w_e_gate, w_e_up, w_e_down, w_s_gate, w_s_up, w_s_down, w_ple, w_ple_gate,
           ln2_g, ln2_b):
    cos_t, sin_t = _rope_tables(positions)
    h = x.reshape(N_TOK, D_MODEL)
    for i in range(DEPTH):
        h = _layer(h, p[i].reshape(N_TOK, PLE_DIM), cos_t, sin_t, _repack_w_in(w_in, i), w_gla_gk2[i], b_gla_gk[i],
                   gla_norm_g[i], w_gla_o[i], w_moba_o[i], w_out[i], ln1_g[i], ln1_b[i], w_router[i],
                   b_router[i], w_e_gate[i], w_e_up[i], w_e_down[i], w_s_gate[i], w_s_up[i], w_s_down[i],
                   w_ple[i], w_ple_gate[i], ln2_g[i], ln2_b[i])
    return h.reshape(BATCH, SEQ, D_MODEL)
```

```python
import functools

import jax
import jax.numpy as jnp
from jax import lax
from jax.experimental import pallas as pl
from jax.experimental.pallas import tpu as pltpu
from jax.experimental.pallas import tpu_sc as plsc

F32 = jnp.float32
BF16 = jnp.bfloat16
I32 = jnp.int32
U32 = jnp.uint32

LANES = 128
SUBLANES = 8
VMEM_LIMIT_BYTES = 48 * 1024 * 1024

D_MODEL = 1024
BATCH = 8
SEQ = 2048
N_TOK = BATCH * SEQ
GLA_HEADS = 4
GLA_DK = 128
GLA_DV = 256
GLA_RANK = 16
GLA_NORMALIZER = 16.0
GLA_CHUNK = 64
GLA_GROUP = 2 * GLA_CHUNK
GLA_UNROLL = 8
MOBA_HEADS = 8
MOBA_DH = 128
MOBA_BLOCK = 256
MOBA_TOPK = 3
ROT_DIM = 32
ROPE_THETA = 500000.0
N_EXPERTS = 256
TOPK_EXPERTS = 8
N_GROUPS = 8
GROUP_SIZE = N_EXPERTS // N_GROUPS
TOPK_GROUPS = 4
D_EXPERT = 256
D_SHARED = 256
ROUTED_SCALE = 2.5
PLE_DIM = 256
LN_EPS = 1e-5
DEPTH = 1
ALPHA = (2.0 * DEPTH) ** 0.25
GLA_KDIM = GLA_HEADS * GLA_DK
GLA_VDIM = GLA_HEADS * GLA_DV
MOBA_DIM = MOBA_HEADS * MOBA_DH
N_KBLK = SEQ // MOBA_BLOCK

COL_GQ = 0
COL_GK = COL_GQ + GLA_KDIM
COL_GV = COL_GK + GLA_KDIM
COL_GR = COL_GV + GLA_VDIM
COL_MQ = COL_GR + GLA_VDIM
COL_MK = COL_MQ + MOBA_DIM
COL_MV = COL_MK + MOBA_DIM
COL_GA = COL_MV + MOBA_DIM
COL_GB = COL_GA + D_MODEL
COL_LOW = COL_GB + D_MODEL
PROJ_COLS = COL_LOW + LANES
LOW_SRC = 2 * GLA_KDIM + 2 * GLA_VDIM

ROW_TILE = 256
PACKED_COLS = D_MODEL // 2
PACKED_CHUNKS = PACKED_COLS // LANES
EXPERT_BUFS = 3
YK_ROWS = TOPK_EXPERTS * N_TOK + EXPERT_BUFS * ROW_TILE
SLOT_CHUNK = 4096
assert EXPERT_BUFS * ROW_TILE <= N_TOK
N_ASSIGN = N_TOK * TOPK_EXPERTS
N_ROW_TILES = (N_ASSIGN + N_EXPERTS * (ROW_TILE - 1) + ROW_TILE - 1) // ROW_TILE
N_ROWS = N_ROW_TILES * ROW_TILE
TILE_TABLE = -(-N_ROW_TILES // LANES) * LANES

PROJ_TM = 1024
REPACK_TN = 512
PROJ_TN = PROJ_COLS // 5
MIX_TM = 512
ROUTE_TM = 512
DEST_TM = 2048
DISP_TM = 256
FINAL_TM = 512

_NEG_INF = float("-inf")
LOG2_E = 1.4426950408889634


def _cparams(*sem):
    return pltpu.CompilerParams(dimension_semantics=sem, vmem_limit_bytes=VMEM_LIMIT_BYTES)


def _dot(a, b):
    return jnp.dot(a, b, preferred_element_type=F32)


def _dot_nt(a, b):
    return lax.dot_general(a, b, (((1,), (1,)), ((), ())), preferred_element_type=F32)


def _dot_tn(a, b):
    return lax.dot_general(a, b, (((0,), (0,)), ((), ())), preferred_element_type=F32)


def _split_bf16(x):
    hi = x.astype(BF16)
    lo = (x - hi.astype(F32)).astype(BF16)
    return hi, lo


def _sigmoid(x):
    return 1.0 / (1.0 + jnp.exp(-x))


def _pack_bf16_pair(x):
    lo = lax.bitcast_convert_type(x[:, :PACKED_COLS].astype(BF16).astype(F32), U32)
    hi = lax.bitcast_convert_type(x[:, PACKED_COLS:].astype(BF16).astype(F32), U32)
    return hi | lax.shift_right_logical(lo, jnp.uint32(16))


def _unpack_bf16_pair(w):
    lo = lax.bitcast_convert_type(lax.shift_left(w, jnp.uint32(16)), F32)
    hi = lax.bitcast_convert_type(w & jnp.uint32(0xFFFF0000), F32)
    return lo, hi


def _repack_body(a_ref, b_ref, o_ref):
    t = pl.program_id(0)
    a = a_ref[...].astype(BF16)

    @pl.when(t < LOW_SRC // REPACK_TN)
    def _():
        o_ref[...] = a

    @pl.when((t >= LOW_SRC // REPACK_TN) & (t < COL_LOW // REPACK_TN))
    def _():
        o_ref[...] = jnp.concatenate([a[GLA_RANK:], b_ref[...].astype(BF16)], axis=0)

    @pl.when(t == COL_LOW // REPACK_TN)
    def _():
        o_ref[...] = jnp.concatenate([a[:GLA_RANK], jnp.zeros((REPACK_TN - GLA_RANK, D_MODEL), BF16)], axis=0)


def _repack_w_in(w_in, layer):
    assert LOW_SRC % REPACK_TN == 0 and COL_LOW % REPACK_TN == 0 and REPACK_TN % GLA_RANK == 0
    w_t = jnp.swapaxes(w_in, 1, 2)
    low_tile = LOW_SRC // REPACK_TN
    last_out = COL_LOW // REPACK_TN
    ranks_per_tile = REPACK_TN // GLA_RANK
    return pl.pallas_call(
        _repack_body,
        grid=(last_out + 1,),
        in_specs=[
            pl.BlockSpec((None, REPACK_TN, D_MODEL), lambda t: (layer, jnp.where(t == last_out, low_tile, t), 0)),
            pl.BlockSpec((None, GLA_RANK, D_MODEL), lambda t: (layer, jnp.minimum(t + 1, last_out) * ranks_per_tile, 0)),
        ],
        out_specs=pl.BlockSpec((REPACK_TN, D_MODEL), lambda t: (t, 0)),
        out_shape=jax.ShapeDtypeStruct((PROJ_COLS, D_MODEL), BF16),
        compiler_params=_cparams("parallel"),
        name="repack_w_in",
    )(w_t, w_t)


def _proj_body(x_ref, w_ref, o_ref):
    o_ref[...] = _dot_nt(x_ref[...].astype(BF16), w_ref[...]).astype(BF16)


def _in_proj(x2d, w_cat):
    return pl.pallas_call(
        _proj_body,
        grid=(PROJ_COLS // PROJ_TN, N_TOK // PROJ_TM),
        in_specs=[
            pl.BlockSpec((PROJ_TM, D_MODEL), lambda j, i: (i, 0)),
            pl.BlockSpec((PROJ_TN, D_MODEL), lambda j, i: (j, 0)),
        ],
        out_specs=pl.BlockSpec((PROJ_TM, PROJ_TN), lambda j, i: (i, j)),
        out_shape=jax.ShapeDtypeStruct((N_TOK, PROJ_COLS), BF16),
        compiler_params=_cparams("parallel", "parallel"),
        name="in_proj",
    )(x2d, w_cat)


def _gla_body(q_ref, k_ref, v_ref, r_ref, low_ref, wg_ref, bg_ref, ng_ref, o_ref, st_ref, gk_ref):
    w_hi, w_lo = _split_bf16(wg_ref[...])
    low = low_ref[...]
    lin = _dot(low, w_hi) + _dot(low, w_lo) + bg_ref[...]
    gk_ref[...] = (jnp.minimum(lin, 0.0) - jnp.log1p(jnp.exp(-jnp.abs(lin)))) * (1.0 / GLA_NORMALIZER)
    st_ref[...] = jnp.zeros_like(st_ref)

    ri = lax.broadcasted_iota(I32, (GLA_GROUP, GLA_GROUP), 0)
    ci = lax.broadcasted_iota(I32, (GLA_GROUP, GLA_GROUP), 1)
    same_chunk = lax.shift_right_logical(ri, GLA_CHUNK.bit_length() - 1) == lax.shift_right_logical(
        ci, GLA_CHUNK.bit_length() - 1)
    causal = same_chunk & (ri >= ci)
    sums = jnp.concatenate([jnp.where(causal, 1.0, 0.0), jnp.where(same_chunk, 1.0, 0.0)], axis=0).astype(BF16)
    gain = ng_ref[...]

    def group(c, carry):
        rows = pl.ds(pl.multiple_of(c * GLA_GROUP, GLA_GROUP), GLA_GROUP)
        g_hi, g_lo = _split_bf16(gk_ref[rows, :])
        bb = _dot(sums, g_hi) + _dot(sums, g_lo)
        b = bb[0:GLA_GROUP]
        b_end = bb[GLA_GROUP:2 * GLA_GROUP]
        q = q_ref[rows, :].astype(F32) * (GLA_DK ** -0.5)
        k = k_ref[rows, :].astype(F32)
        v = v_ref[rows, :]
        q_e = (q * jnp.exp(b)).astype(BF16)
        k_e = (k * jnp.exp(-b)).astype(BF16)
        k_d = (k * jnp.exp(b_end - b)).astype(BF16)
        att = jnp.where(causal, _dot_nt(q_e, k_e), 0.0)
        o = _dot(att.astype(BF16), v)
        st = st_ref[...]
        inter = []
        for j in range(GLA_GROUP // GLA_CHUNK):
            cr = slice(j * GLA_CHUNK, (j + 1) * GLA_CHUNK)
            inter.append(_dot_nt(q_e[cr], st.astype(BF16)))
            st = st * jnp.exp(b_end[j * GLA_CHUNK:j * GLA_CHUNK + 1, :]) + _dot_tn(v[cr], k_d[cr])
        st_ref[...] = st
        o = o + jnp.concatenate(inter, axis=0)
        o = o * lax.rsqrt(jnp.mean(o * o, axis=-1, keepdims=True) + LN_EPS) * gain
        r = r_ref[rows, :].astype(F32)
        o_ref[rows, :] = (o * (r * _sigmoid(r))).astype(BF16)
        return carry

    lax.fori_loop(0, SEQ // GLA_GROUP, group, 0, unroll=GLA_UNROLL)


def _gla(proj, w_gk2_pad, b_gk, norm_g):
    kb, vb = GLA_DK, GLA_DV
    return pl.pallas_call(
        _gla_body,
        grid=(BATCH, GLA_HEADS),
        in_specs=[
            pl.BlockSpec((SEQ, kb), lambda b, h: (b, COL_GQ // kb + h)),
            pl.BlockSpec((SEQ, kb), lambda b, h: (b, COL_GK // kb + h)),
            pl.BlockSpec((SEQ, vb), lambda b, h: (b, COL_GV // vb + h)),
            pl.BlockSpec((SEQ, vb), lambda b, h: (b, COL_GR // vb + h)),
            pl.BlockSpec((SEQ, LANES), lambda b, h: (b, COL_LOW // LANES)),
            pl.BlockSpec((LANES, kb), lambda b, h: (0, h)),
            pl.BlockSpec((1, kb), lambda b, h: (0, h)),
            pl.BlockSpec((1, vb), lambda b, h: (0, 0)),
        ],
        out_specs=pl.BlockSpec((SEQ, vb), lambda b, h: (b, h)),
        out_shape=jax.ShapeDtypeStruct((N_TOK, GLA_VDIM), BF16),
        scratch_shapes=[pltpu.VMEM((vb, kb), F32), pltpu.VMEM((SEQ, kb), F32)],
        compiler_params=_cparams("parallel", "parallel"),
        name="gla",
    )(proj, proj, proj, proj, proj, w_gk2_pad, b_gk, norm_g)


def _moba_body(q_ref, k_ref, v_ref, c_ref, s_ref, o_ref, qs_ref, ks_ref, vt_ref, bias_ref,
               sc_a_ref, sc_b_ref, pr_a_ref, pr_b_ref):
    cos_t = c_ref[0]
    sin_t = s_ref[0]
    lane = lax.broadcasted_iota(I32, (SEQ, MOBA_DH), 1)
    half = ROT_DIM // 2

    def rope(x):
        partner = jnp.where(lane < half, pltpu.roll(x, MOBA_DH - half, 1), pltpu.roll(x, half, 1))
        return x * cos_t + partner * sin_t

    q = rope(q_ref[...].astype(F32))
    k = rope(k_ref[...].astype(F32))
    q_hi, q_lo = _split_bf16(q)
    qs_ref[...] = (q * (MOBA_DH ** -0.5 * LOG2_E)).astype(BF16)
    ks_ref[...] = k.astype(BF16)
    vt_ref[...] = v_ref[...].astype(F32).T.astype(BF16)

    k_mean = jnp.concatenate(
        [jnp.mean(k[j * MOBA_BLOCK:(j + 1) * MOBA_BLOCK], axis=0, keepdims=True) for j in range(N_KBLK)], axis=0)
    m_hi, m_lo = _split_bf16(k_mean)
    s_blk = _dot_nt(m_hi, q_hi) + _dot_nt(m_hi, q_lo) + _dot_nt(m_lo, q_hi)
    blk = lax.broadcasted_iota(I32, (N_KBLK, SEQ), 0)
    q_blk = lax.shift_right_logical(lax.broadcasted_iota(I32, (N_KBLK, SEQ), 1), MOBA_BLOCK.bit_length() - 1)
    past = blk < q_blk
    s_blk = jnp.where(past, s_blk, _NEG_INF)
    beaten = jnp.zeros((N_KBLK, SEQ), I32)
    for j in range(N_KBLK):
        row = s_blk[j:j + 1, :]
        beaten += jnp.where((row > s_blk) | ((row == s_blk) & (j < blk)), 1, 0)
    bias_ref[...] = jnp.where(past & (beaten < MOBA_TOPK), 0.0, _NEG_INF)

    kr = lax.broadcasted_iota(I32, (MOBA_BLOCK, MOBA_BLOCK), 0)
    qc = lax.broadcasted_iota(I32, (MOBA_BLOCK, MOBA_BLOCK), 1)
    own_bias = jnp.where(kr <= qc, 0.0, _NEG_INF)

    sc_bufs = (sc_a_ref, sc_b_ref)
    pr_bufs = (pr_a_ref, pr_b_ref)
    zeros_row = jnp.zeros((1, MOBA_BLOCK), F32)
    future_row = jnp.full((1, MOBA_BLOCK), _NEG_INF, F32)
    for pair in range(N_KBLK // 2):
        q_blocks = (2 * pair, 2 * pair + 1)
        cols = slice(q_blocks[0] * MOBA_BLOCK, (q_blocks[1] + 1) * MOBA_BLOCK)
        n_kblk = q_blocks[1] + 1
        n_keys = n_kblk * MOBA_BLOCK
        sc, pr = sc_bufs[pair % 2], pr_bufs[pair % 2]
        q_pair = qs_ref[cols, :]

        def query_bias(j):
            halves = []
            for qb in q_blocks:
                q_cols = slice(qb * MOBA_BLOCK, (qb + 1) * MOBA_BLOCK)
                halves.append(bias_ref[j:j + 1, q_cols] if j < qb else zeros_row if j == qb else future_row)
            return jnp.concatenate(halves, axis=1)

        biases = [query_bias(j) for j in range(n_kblk)]
        col_max = []
        for j in range(n_kblk):
            rows = slice(j * MOBA_BLOCK, (j + 1) * MOBA_BLOCK)
            s = _dot_nt(ks_ref[rows, :], q_pair)
            if j in q_blocks:
                h = q_blocks.index(j)
                own = s[:, h * MOBA_BLOCK:(h + 1) * MOBA_BLOCK] + own_bias
                s = jnp.concatenate([own, s[:, MOBA_BLOCK:]] if h == 0 else [s[:, :MOBA_BLOCK], own], axis=1)
            sc[rows, :] = s
            col_max.append(jnp.max(s, axis=0, keepdims=True) + biases[j])
        m = functools.reduce(jnp.maximum, col_max)
        denom = jnp.zeros((1, 2 * MOBA_BLOCK), F32)
        for j in range(n_kblk):
            rows = slice(j * MOBA_BLOCK, (j + 1) * MOBA_BLOCK)
            p = jnp.exp2(sc[rows, :] - (m - biases[j]))
            denom = denom + jnp.sum(p, axis=0, keepdims=True)
            pr[rows, :] = p.astype(BF16)
        o_t = _dot(vt_ref[:, 0:n_keys], pr[0:n_keys, :]) * (1.0 / denom)
        o_ref[cols, :] = o_t.T.astype(BF16)


def _moba(proj, cos_t, sin_t):
    dh = MOBA_DH
    return pl.pallas_call(
        _moba_body,
        grid=(BATCH, MOBA_HEADS),
        in_specs=[
            pl.BlockSpec((SEQ, dh), lambda b, h: (b, COL_MQ // dh + h)),
            pl.BlockSpec((SEQ, dh), lambda b, h: (b, COL_MK // dh + h)),
            pl.BlockSpec((SEQ, dh), lambda b, h: (b, COL_MV // dh + h)),
            pl.BlockSpec((1, SEQ, dh), lambda b, h: (b, 0, 0)),
            pl.BlockSpec((1, SEQ, dh), lambda b, h: (b, 0, 0)),
        ],
        out_specs=pl.BlockSpec((SEQ, dh), lambda b, h: (b, h)),
        out_shape=jax.ShapeDtypeStruct((N_TOK, MOBA_DIM), BF16),
        scratch_shapes=[
            pltpu.VMEM((SEQ, dh), BF16),
            pltpu.VMEM((SEQ, dh), BF16),
            pltpu.VMEM((dh, SEQ), BF16),
            pltpu.VMEM((N_KBLK, SEQ), F32),
            pltpu.VMEM((SEQ, 2 * MOBA_BLOCK), F32),
            pltpu.VMEM((SEQ, 2 * MOBA_BLOCK), F32),
            pltpu.VMEM((SEQ, 2 * MOBA_BLOCK), BF16),
            pltpu.VMEM((SEQ, 2 * MOBA_BLOCK), BF16),
        ],
        compiler_params=_cparams("parallel", "parallel"),
        name="moba",
    )(proj, proj, proj, cos_t, sin_t)


def _layer_norm(z, g, b):
    mu = jnp.mean(z, axis=-1, keepdims=True)
    zc = z - mu
    var = jnp.mean(zc * zc, axis=-1, keepdims=True)
    return zc * lax.rsqrt(var + LN_EPS) * g + b


def _mix_body(gla_ref, moba_ref, ga_ref, gb_ref, x_ref, wgo_ref, wmo_ref, wo_ref, g_ref, b_ref,
              h_ref, hb_ref, hp_ref):
    y_gla = _dot(gla_ref[...], wgo_ref[...])
    y_moba = _dot(moba_ref[...], wmo_ref[...])
    merged = _sigmoid(ga_ref[...].astype(F32)) * y_gla + _sigmoid(gb_ref[...].astype(F32)) * y_moba
    mix = _dot(merged.astype(BF16), wo_ref[...])
    h = _layer_norm(ALPHA * x_ref[...] + mix, g_ref[...], b_ref[...])
    h_ref[...] = h
    hb_ref[...] = h.astype(BF16)
    packed = _pack_bf16_pair(h)
    for c in range(PACKED_CHUNKS):
        hp_ref[:, c, :] = packed[:, c * LANES:(c + 1) * LANES]


def _mix(gla_out, moba_out, proj, x2d, w_gla_o, w_moba_o, w_out, ln_g, ln_b):
    d = D_MODEL
    row = lambda i: (i, 0)
    full = lambda i: (0, 0)
    return pl.pallas_call(
        _mix_body,
        grid=(N_TOK // MIX_TM,),
        in_specs=[
            pl.BlockSpec((MIX_TM, d), row),
            pl.BlockSpec((MIX_TM, d), row),
            pl.BlockSpec((MIX_TM, d), lambda i: (i, COL_GA // d)),
            pl.BlockSpec((MIX_TM, d), lambda i: (i, COL_GB // d)),
            pl.BlockSpec((MIX_TM, d), row),
            pl.BlockSpec((d, d), full),
            pl.BlockSpec((d, d), full),
            pl.BlockSpec((d, d), full),
            pl.BlockSpec((1, d), full),
            pl.BlockSpec((1, d), full),
        ],
        out_specs=[
            pl.BlockSpec((MIX_TM, d), row),
            pl.BlockSpec((MIX_TM, d), row),
            pl.BlockSpec((MIX_TM, PACKED_CHUNKS, LANES), lambda i: (i, 0, 0)),
        ],
        out_shape=[
            jax.ShapeDtypeStruct((N_TOK, d), F32),
            jax.ShapeDtypeStruct((N_TOK, d), BF16),
            jax.ShapeDtypeStruct((N_TOK, PACKED_CHUNKS, LANES), U32),
        ],
        compiler_params=_cparams("parallel"),
        name="mix_ln1",
    )(gla_out, moba_out, proj, proj, x2d, w_gla_o, w_moba_o, w_out, ln_g, ln_b)


def _route_body(h_ref, wr_ref, br_ref, e_ref, w_ref, rk_ref, cnt_ref, carry_ref):
    tm = ROUTE_TM

    @pl.when(pl.program_id(0) == 0)
    def _():
        carry_ref[...] = jnp.zeros_like(carry_ref)

    scores = _sigmoid(_dot_nt(wr_ref[...], h_ref[...]))
    biased = scores + br_ref[...]
    row = lax.broadcasted_iota(I32, (N_EXPERTS, tm), 0).astype(F32)
    row_g = lax.broadcasted_iota(I32, (GROUP_SIZE, tm), 0).astype(F32)

    g_scores = []
    for g in range(N_GROUPS):
        grp = biased[g * GROUP_SIZE:(g + 1) * GROUP_SIZE]
        m1 = jnp.max(grp, axis=0, keepdims=True)
        first = jnp.min(jnp.where(grp == m1, row_g, float(GROUP_SIZE)), axis=0, keepdims=True)
        m2 = jnp.max(jnp.where(row_g == first, _NEG_INF, grp), axis=0, keepdims=True)
        g_scores.append(m1 + m2)
    g_score = jnp.concatenate(g_scores, axis=0)
    g_row = lax.broadcasted_iota(I32, (N_GROUPS, tm), 0)
    g_beaten = jnp.zeros((N_GROUPS, tm), I32)
    for g in range(N_GROUPS):
        r = g_score[g:g + 1, :]
        g_beaten += jnp.where((r > g_score) | ((r == g_score) & (g < g_row)), 1, 0)
    g_keep = g_beaten < TOPK_GROUPS
    masked = jnp.concatenate(
        [jnp.where(g_keep[g:g + 1, :], biased[g * GROUP_SIZE:(g + 1) * GROUP_SIZE], _NEG_INF)
         for g in range(N_GROUPS)], axis=0)

    onehot = jnp.zeros((N_EXPERTS, tm), F32)
    picks, pick_scores = [], []
    for _ in range(TOPK_EXPERTS):
        m = jnp.max(masked, axis=0, keepdims=True)
        idx = jnp.min(jnp.where(masked == m, row, float(N_EXPERTS)), axis=0, keepdims=True)
        hit = row == idx
        picks.append(idx)
        pick_scores.append(jnp.sum(jnp.where(hit, scores, 0.0), axis=0, keepdims=True))
        onehot = onehot + jnp.where(hit, 1.0, 0.0)
        masked = jnp.where(hit, _NEG_INF, masked)
    sel = jnp.concatenate(pick_scores, axis=0)
    e_ref[...] = jnp.concatenate(picks, axis=0).astype(I32)
    w_ref[...] = sel / jnp.sum(sel, axis=0, keepdims=True) * ROUTED_SCALE

    t_r = lax.broadcasted_iota(I32, (tm, tm), 0)
    t_c = lax.broadcasted_iota(I32, (tm, tm), 1)
    earlier = jnp.where(t_r < t_c, 1.0, 0.0).astype(BF16)
    seen = _dot(onehot.astype(BF16), earlier) + carry_ref[...]
    rk_ref[...] = jnp.concatenate(
        [jnp.sum(jnp.where(row == idx, seen, 0.0), axis=0, keepdims=True) for idx in picks], axis=0).astype(I32)
    carry_ref[...] += jnp.sum(onehot, axis=1, keepdims=True)
    cnt_ref[...] = carry_ref[...]


def _route(h_bf, w_router_t, b_router_col):
    k = TOPK_EXPERTS
    tok = lambda i: (0, i)
    return pl.pallas_call(
        _route_body,
        grid=(N_TOK // ROUTE_TM,),
        in_specs=[
            pl.BlockSpec((ROUTE_TM, D_MODEL), lambda i: (i, 0)),
            pl.BlockSpec((N_EXPERTS, D_MODEL), lambda i: (0, 0)),
            pl.BlockSpec((N_EXPERTS, 1), lambda i: (0, 0)),
        ],
        out_specs=[
            pl.BlockSpec((k, ROUTE_TM), tok),
            pl.BlockSpec((k, ROUTE_TM), tok),
            pl.BlockSpec((k, ROUTE_TM), tok),
            pl.BlockSpec((N_EXPERTS, 1), lambda i: (0, 0)),
        ],
        out_shape=[
            jax.ShapeDtypeStruct((k, N_TOK), I32),
            jax.ShapeDtypeStruct((k, N_TOK), F32),
            jax.ShapeDtypeStruct((k, N_TOK), I32),
            jax.ShapeDtypeStruct((N_EXPERTS, 1), F32),
        ],
        scratch_shapes=[pltpu.VMEM((N_EXPERTS, 1), F32)],
        compiler_params=_cparams("arbitrary"),
        name="route",
    )(h_bf, w_router_t, b_router_col)


def _dest_body(cnt_ref, e_ref, rk_ref, d_ref, te_ref, nu_ref, lt_ref):
    cnt = cnt_ref[...]
    tiles = jnp.floor((cnt + (ROW_TILE - 1)) * (1.0 / ROW_TILE))
    er = lax.broadcasted_iota(I32, (N_EXPERTS, N_EXPERTS), 0)
    ec = lax.broadcasted_iota(I32, (N_EXPERTS, N_EXPERTS), 1)
    before = jnp.where(ec < er, 1.0, 0.0).astype(BF16)
    tiles_b = jnp.broadcast_to(tiles, (N_EXPERTS, LANES)).astype(BF16)
    t_start = _dot(before, tiles_b)[:, 0:1]
    t_end = t_start + tiles
    p_start = t_start * float(ROW_TILE)
    lt_ref[...] = jnp.where(tiles > 0.0, (t_end - 1.0) * float(ROW_TILE), -1.0).astype(I32)

    row = lax.broadcasted_iota(I32, (N_EXPERTS, DEST_TM), 0)
    d_ref[...] = jnp.concatenate(
        [jnp.sum(jnp.where(row == e_ref[k:k + 1, :], p_start, 0.0), axis=0, keepdims=True)
         for k in range(TOPK_EXPERTS)], axis=0).astype(I32) + rk_ref[...]

    tile_id = lax.broadcasted_iota(I32, (N_EXPERTS, TILE_TABLE), 1).astype(F32)
    owner = jnp.sum(jnp.where(t_end <= tile_id, 1, 0), axis=0, keepdims=True)
    te_ref[...] = jnp.minimum(owner, N_EXPERTS - 1)
    nu_ref[...] = jnp.broadcast_to(t_end[N_EXPERTS - 1:N_EXPERTS, :], (1, LANES)).astype(I32)


def _dest(cnt, e_t, rk_t):
    k = TOPK_EXPERTS
    tok = lambda i: (0, i)
    return pl.pallas_call(
        _dest_body,
        grid=(N_TOK // DEST_TM,),
        in_specs=[
            pl.BlockSpec((N_EXPERTS, 1), lambda i: (0, 0)),
            pl.BlockSpec((k, DEST_TM), tok),
            pl.BlockSpec((k, DEST_TM), tok),
        ],
        out_specs=[
            pl.BlockSpec((k, DEST_TM), tok),
            pl.BlockSpec((1, TILE_TABLE), lambda i: (0, 0)),
            pl.BlockSpec((1, LANES), lambda i: (0, 0)),
            pl.BlockSpec((N_EXPERTS, 1), lambda i: (0, 0)),
        ],
        out_shape=[
            jax.ShapeDtypeStruct((k, N_TOK), I32),
            jax.ShapeDtypeStruct((1, TILE_TABLE), I32),
            jax.ShapeDtypeStruct((1, LANES), I32),
            jax.ShapeDtypeStruct((N_EXPERTS, 1), I32),
        ],
        compiler_params=_cparams("arbitrary"),
        name="dest",
    )(cnt, e_t, rk_t)


def _row_copy(src, dst, sem):
    return pltpu.make_async_copy(src, dst, sem)


def _dispatch_body(lt_ref, nu_ref, d_ref, h_ref, xs_ref, zero_ref, sem):
    @pl.when(pl.program_id(0) == 0)
    def _():
        zero_ref[...] = jnp.zeros_like(zero_ref)

        def zero_tile(first_row):
            r0 = pl.multiple_of(first_row, ROW_TILE)
            return pltpu.make_async_copy(zero_ref, xs_ref.at[pl.ds(r0, ROW_TILE)], sem)

        def z_start(e, carry):
            @pl.when(lt_ref[e] >= 0)
            def _():
                zero_tile(jnp.maximum(lt_ref[e], 0)).start()
            return carry

        def z_wait(e, carry):
            @pl.when(lt_ref[e] >= 0)
            def _():
                zero_tile(jnp.maximum(lt_ref[e], 0)).wait()
            return carry

        def t_start(t, carry):
            zero_tile(t * ROW_TILE).start()
            return carry

        def t_wait(t, carry):
            zero_tile(t * ROW_TILE).wait()
            return carry

        lax.fori_loop(0, N_EXPERTS, z_start, 0)
        lax.fori_loop(nu_ref[0], N_ROW_TILES, t_start, 0)
        lax.fori_loop(0, N_EXPERTS, z_wait, 0)
        lax.fori_loop(nu_ref[0], N_ROW_TILES, t_wait, 0)

    for t in range(DISP_TM):
        for k in range(TOPK_EXPERTS):
            _row_copy(h_ref.at[t], xs_ref.at[d_ref[k, t]], sem).start(priority=k % 2)
    for k in range(TOPK_EXPERTS):
        pltpu.make_async_copy(h_ref, xs_ref.at[pl.ds(0, DISP_TM)], sem).wait()


def _dispatch(last_tile_row, n_used, dest_t, h1):
    grid_spec = pltpu.PrefetchScalarGridSpec(
        num_scalar_prefetch=2,
        grid=(N_TOK // DISP_TM,),
        in_specs=[
            pl.BlockSpec((TOPK_EXPERTS, DISP_TM), lambda i, lt, nu: (0, i), memory_space=pltpu.SMEM),
            pl.BlockSpec((DISP_TM, PACKED_CHUNKS, LANES), lambda i, lt, nu: (i, 0, 0)),
        ],
        out_specs=pl.BlockSpec(memory_space=pl.ANY),
        scratch_shapes=[pltpu.VMEM((ROW_TILE, PACKED_CHUNKS, LANES), U32), pltpu.SemaphoreType.DMA],
    )
    return pl.pallas_call(
        _dispatch_body,
        grid_spec=grid_spec,
        out_shape=jax.ShapeDtypeStruct((N_ROWS, PACKED_CHUNKS, LANES), U32),
        compiler_params=_cparams("arbitrary"),
        name="dispatch",
    )(last_tile_row, n_used, dest_t, h1)


def _row_destinations(dest_flat):
    info = plsc.get_sparse_core_info()
    n_cores, lanes = info.num_cores, info.num_lanes
    n_workers = n_cores * info.num_subcores
    rows_per_worker = -(-N_ROWS // (n_workers * ROW_TILE)) * ROW_TILE
    assert rows_per_worker % lanes == 0
    mesh = plsc.VectorSubcoreMesh(core_axis_name="c", subcore_axis_name="s")

    @functools.partial(
        pl.kernel, mesh=mesh, out_type=jax.ShapeDtypeStruct((n_workers * rows_per_worker,), I32),
        scratch_types=[pltpu.VMEM((SLOT_CHUNK,), I32), pltpu.VMEM((rows_per_worker,), I32)],
        compiler_params=pltpu.CompilerParams(needs_layout_passes=False), name="row_destinations")
    def invert(dest_hbm, out_hbm, dest_v, table_v):
        first_row = (lax.axis_index("s") * n_cores + lax.axis_index("c")) * rows_per_worker
        lane = lax.iota(I32, lanes)
        tile_shift = ROW_TILE.bit_length() - 1

        def spare(i, carry):
            row = first_row + i * lanes + lane
            buf = lax.rem(lax.shift_right_logical(row, tile_shift), EXPERT_BUFS)
            table_v[pl.ds(i * lanes, lanes)] = TOPK_EXPERTS * N_TOK + buf * ROW_TILE + (row & (ROW_TILE - 1))
            return carry

        lax.fori_loop(0, rows_per_worker // lanes, spare, 0)

        def chunk(c, carry):
            pltpu.sync_copy(dest_hbm.at[pl.ds(c * SLOT_CHUNK, SLOT_CHUNK)], dest_v)

            def vec(i, inner):
                local = dest_v[pl.ds(i * lanes, lanes)] - first_row
                mine = (local >= 0) & (local < rows_per_worker)
                plsc.store_scatter(table_v, [jnp.where(mine, local, 0)], lane + (c * SLOT_CHUNK + i * lanes),
                                   mask=mine)
                return inner

            lax.fori_loop(0, SLOT_CHUNK // lanes, vec, 0)
            return carry

        lax.fori_loop(0, N_ASSIGN // SLOT_CHUNK, chunk, 0)
        pltpu.sync_copy(table_v, out_hbm.at[pl.ds(first_row, rows_per_worker)])

    return invert(dest_flat)


def _experts_body(te_ref, nu_ref, lt_ref, slot_ref, xs_ref, wg_ref, wu_ref, wd_ref, yk_ref,
                  wg_f, wu_f, wd_f, wg_b, wu_b, wd_b, y_buf, n_loaded, sem, w_sem):
    i = pl.program_id(0)
    n_used = nu_ref[0]

    def weight_copies(e, s):
        return (pltpu.make_async_copy(wg_ref.at[e], wg_f.at[s], w_sem.at[s]),
                pltpu.make_async_copy(wu_ref.at[e], wu_f.at[s], w_sem.at[s]),
                pltpu.make_async_copy(wd_ref.at[e], wd_f.at[s], w_sem.at[s]))

    def wait_tile(b):
        pltpu.make_async_copy(y_buf.at[b], yk_ref.at[pl.ds(0, ROW_TILE), :], sem.at[b]).wait()

    def send_tile(b):
        for r in range(ROW_TILE):
            dst = slot_ref[0, 0, r]
            _row_copy(y_buf.at[b, pl.ds(r, 1), :], yk_ref.at[pl.ds(dst, 1), :], sem.at[b]).start(priority=r % 2)

    @pl.when(i == 0)
    def _():
        y_buf[EXPERT_BUFS - 1] = jnp.zeros((ROW_TILE, PACKED_COLS), U32)
        spare = [pltpu.make_async_copy(y_buf.at[EXPERT_BUFS - 1],
                                       yk_ref.at[pl.ds(TOPK_EXPERTS * N_TOK + b * ROW_TILE, ROW_TILE), :],
                                       sem.at[b]) for b in range(EXPERT_BUFS - 1)]
        for c in spare:
            c.start()
        for c in spare:
            c.wait()
        n_loaded[0] = 0
        for c in weight_copies(te_ref[0], 0):
            c.start()

    @pl.when((i >= 2) & (i <= n_used))
    def _():
        wait_tile(lax.rem(i, EXPERT_BUFS))

    @pl.when(i < n_used)
    def _():
        e = te_ref[i]
        prev = te_ref[jnp.maximum(i - 1, 0)]

        @pl.when((i == 0) | (e != prev))
        def _():
            s = lax.rem(n_loaded[0], 2)
            for c in weight_copies(e, s):
                c.wait()
            wg_b[...] = wg_f[s].astype(BF16)
            wu_b[...] = wu_f[s].astype(BF16)
            wd_b[...] = wd_f[s].astype(BF16)
            n_loaded[0] = n_loaded[0] + 1
            nxt = lax.shift_right_logical(lt_ref[e], ROW_TILE.bit_length() - 1) + 1

            @pl.when(nxt < n_used)
            def _():
                for c in weight_copies(te_ref[nxt], 1 - s):
                    c.start()

        for phase in range(EXPERT_BUFS):
            @pl.when(lax.rem(i, EXPERT_BUFS) == phase)
            def _(phase=phase):
                send_tile((phase + EXPERT_BUFS - 1) % EXPERT_BUFS)
                packed = jnp.concatenate([xs_ref[:, c, :] for c in range(PACKED_CHUNKS)], axis=1)
                x_lo, x_hi = _unpack_bf16_pair(packed)
                x = jnp.concatenate([x_lo.astype(BF16), x_hi.astype(BF16)], axis=1)
                g = _dot(x, wg_b[...])
                u = _dot(x, wu_b[...])
                h = (g * _sigmoid(g)) * u
                y_buf[phase] = _pack_bf16_pair(_dot(h.astype(BF16), wd_b[...]))

    @pl.when(i == n_used)
    def _():
        send_tile(lax.rem(i + EXPERT_BUFS - 1, EXPERT_BUFS))
        wait_tile(lax.rem(i + EXPERT_BUFS - 2, EXPERT_BUFS))
        wait_tile(lax.rem(i + EXPERT_BUFS - 1, EXPERT_BUFS))


def _experts(tile_expert, n_used, last_tile_row, row_dst, xs, w_gate, w_up, w_down):
    def tile(i, te, nu, lt):
        return (jnp.minimum(i, nu[0] - 1), 0, 0)

    n_table_tiles = row_dst.shape[0] // ROW_TILE
    placeholder = n_table_tiles - 1
    assert placeholder * ROW_TILE >= N_ROWS and placeholder % EXPERT_BUFS == EXPERT_BUFS - 1

    def prev_tile(i, te, nu, lt):
        return (jnp.where(i == 0, placeholder, jnp.minimum(i, nu[0]) - 1), 0, 0)

    grid_spec = pltpu.PrefetchScalarGridSpec(
        num_scalar_prefetch=3,
        grid=(N_ROW_TILES + 1,),
        in_specs=[
            pl.BlockSpec((1, 1, ROW_TILE), prev_tile, memory_space=pltpu.SMEM),
            pl.BlockSpec((ROW_TILE, PACKED_CHUNKS, LANES), tile),
            pl.BlockSpec(memory_space=pl.ANY),
            pl.BlockSpec(memory_space=pl.ANY),
            pl.BlockSpec(memory_space=pl.ANY),
        ],
        out_specs=pl.BlockSpec(memory_space=pl.ANY),
        scratch_shapes=[
            pltpu.VMEM((2, D_MODEL, D_EXPERT), F32),
            pltpu.VMEM((2, D_MODEL, D_EXPERT), F32),
            pltpu.VMEM((2, D_EXPERT, D_MODEL), F32),
            pltpu.VMEM((D_MODEL, D_EXPERT), BF16),
            pltpu.VMEM((D_MODEL, D_EXPERT), BF16),
            pltpu.VMEM((D_EXPERT, D_MODEL), BF16),
            pltpu.VMEM((EXPERT_BUFS, ROW_TILE, PACKED_COLS), U32),
            pltpu.SMEM((1,), I32),
            pltpu.SemaphoreType.DMA((EXPERT_BUFS,)),
            pltpu.SemaphoreType.DMA((2,)),
        ],
    )
    return pl.pallas_call(
        _experts_body,
        grid_spec=grid_spec,
        out_shape=jax.ShapeDtypeStruct((YK_ROWS, PACKED_COLS), U32),
        compiler_params=_cparams("arbitrary"),
        name="experts",
    )(tile_expert, n_used, last_tile_row, row_dst.reshape(n_table_tiles, 1, ROW_TILE), xs, w_gate, w_up, w_down)


def _final_body(w_ref, *refs):
    yk_refs = refs[:TOPK_EXPERTS]
    h_ref, hb_ref, p_ref, wsg_ref, wsu_ref, wsd_ref, wpl_ref, wpg_ref, g_ref, b_ref, o_ref = refs[TOPK_EXPERTS:]
    hb = hb_ref[...]
    sg = _dot(hb, wsg_ref[...])
    shared = _dot(((sg * _sigmoid(sg)) * _dot(hb, wsu_ref[...])).astype(BF16), wsd_ref[...])
    ple = _sigmoid(_dot(hb, wpg_ref[...])) * _dot(p_ref[...].astype(BF16), wpl_ref[...])

    w_col = w_ref[...].T
    y_lo, y_hi = _unpack_bf16_pair(yk_refs[0][...])
    r_lo, r_hi = y_lo * w_col[:, 0:1], y_hi * w_col[:, 0:1]
    for k in range(1, TOPK_EXPERTS):
        y_lo, y_hi = _unpack_bf16_pair(yk_refs[k][...])
        r_lo, r_hi = r_lo + y_lo * w_col[:, k:k + 1], r_hi + y_hi * w_col[:, k:k + 1]
    routed = jnp.concatenate([r_lo, r_hi], axis=1)
    o_ref[...] = _layer_norm(ALPHA * h_ref[...] + (routed + shared) + ple, g_ref[...], b_ref[...])


def _final(w_t, yk, h1, h_bf, p2d, w_s_gate, w_s_up, w_s_down, w_ple, w_ple_gate, ln_g, ln_b):
    d = D_MODEL
    n_steps = N_TOK // FINAL_TM
    full = lambda i: (0, 0)
    return pl.pallas_call(
        _final_body,
        grid=(n_steps,),
        in_specs=[
            pl.BlockSpec((TOPK_EXPERTS, FINAL_TM), lambda i: (0, i)),
            *[pl.BlockSpec((FINAL_TM, PACKED_COLS), lambda i, k=k: (k * n_steps + i, 0)) for k in range(TOPK_EXPERTS)],
            pl.BlockSpec((FINAL_TM, d), lambda i: (i, 0)),
            pl.BlockSpec((FINAL_TM, d), lambda i: (i, 0)),
            pl.BlockSpec((FINAL_TM, PLE_DIM), lambda i: (i, 0)),
            pl.BlockSpec((d, D_SHARED), full),
            pl.BlockSpec((d, D_SHARED), full),
            pl.BlockSpec((D_SHARED, d), full),
            pl.BlockSpec((PLE_DIM, d), full),
            pl.BlockSpec((d, d), full),
            pl.BlockSpec((1, d), full),
            pl.BlockSpec((1, d), full),
        ],
        out_specs=pl.BlockSpec((FINAL_TM, d), lambda i: (i, 0)),
        out_shape=jax.ShapeDtypeStruct((N_TOK, d), F32),
        compiler_params=_cparams("parallel"),
        name="combine_ln2",
    )(w_t, *([yk] * TOPK_EXPERTS), h1, h_bf, p2d, w_s_gate, w_s_up, w_s_down, w_ple, w_ple_gate, ln_g, ln_b)


def _rope_tables(positions):
    half = ROT_DIM // 2
    inv = ROPE_THETA ** (-jnp.arange(0, ROT_DIM, 2, dtype=F32) / ROT_DIM)
    per_row = LANES // half
    pos = jnp.repeat(positions.reshape(-1, per_row), half, axis=1).astype(F32)
    ang = pos * jnp.tile(inv, per_row)[None, :]
    cos, sin = lax.optimization_barrier((jnp.cos(ang), jnp.sin(ang)))
    cos = cos.reshape(positions.shape + (half,))
    sin = sin.reshape(positions.shape + (half,))
    rest = MOBA_DH - ROT_DIM
    ones = jnp.ones(cos.shape[:-1] + (rest,), F32)
    zeros = jnp.zeros(cos.shape[:-1] + (rest,), F32)
    return (jnp.concatenate([cos, cos, ones], axis=-1), jnp.concatenate([-sin, sin, zeros], axis=-1))


def _layer(h2d, p2d, cos_t, sin_t, w_cat, w_gk2, b_gk, norm_g, w_gla_o, w_moba_o, w_out, ln1_g, ln1_b,
           w_router, b_router, w_e_gate, w_e_up, w_e_down, w_s_gate, w_s_up, w_s_down, w_ple, w_ple_gate,
           ln2_g, ln2_b):
    w_gk2_pad = jnp.concatenate([w_gk2, jnp.zeros((LANES - GLA_RANK, GLA_KDIM), w_gk2.dtype)], axis=0)

    proj = _in_proj(h2d, w_cat)
    gla_out = _gla(proj, w_gk2_pad, b_gk[None, :], norm_g[None, :])
    moba_out = _moba(proj, cos_t, sin_t)
    h1, h_bf, h_packed = _mix(gla_out, moba_out, proj, h2d, w_gla_o.astype(BF16), w_moba_o.astype(BF16),
                              w_out.astype(BF16), ln1_g[None, :], ln1_b[None, :])
    e_t, w_t, rk_t, cnt = _route(h_bf, w_router.T.astype(BF16), b_router[:, None])
    dest_t, tile_expert, n_used, last_tile_row = _dest(cnt, e_t, rk_t)
    last_tile_row = last_tile_row.reshape(N_EXPERTS)
    n_used = n_used[0, 0:1]
    xs = _dispatch(last_tile_row, n_used, dest_t, h_packed)
    row_dst = _row_destinations(dest_t.reshape(N_ASSIGN))
    yk = _experts(tile_expert.reshape(TILE_TABLE), n_used, last_tile_row, row_dst, xs,
                  w_e_gate, w_e_up, w_e_down)
    return _final(w_t, yk, h1, h_bf, p2d, w_s_gate.astype(BF16), w_s_up.astype(BF16),
                  w_s_down.astype(BF16), w_ple.astype(BF16), w_ple_gate.astype(BF16),
                  ln2_g[None, :], ln2_b[None, :])


def kernel(x, p, positions, w_in, w_gla_gk2, b_gla_gk, gla_norm_g, w_gla_o, w_moba_o, w_out, ln1_g, ln1_b,
           w_router, b_router, w_e_gate, w_e_up, w_e_down, w_s_gate, w_s_up, w_s_down, w_ple, w_ple_gate,
           ln2_g, ln2_b):
    cos_t, sin_t = _rope_tables(positions)
    h = x.reshape(N_TOK, D_MODEL)
    for i in range(DEPTH):
        h = _layer(h, p[i].reshape(N_TOK, PLE_DIM), cos_t, sin_t, _repack_w_in(w_in, i), w_gla_gk2[i], b_gla_gk[i],
                   gla_norm_g[i], w_gla_o[i], w_moba_o[i], w_out[i], ln1_g[i], ln1_b[i], w_router[i],
                   b_router[i], w_e_gate[i], w_e_up[i], w_e_down[i], w_s_gate[i], w_s_up[i], w_s_down[i],
                   w_ple[i], w_ple_gate[i], ln2_g[i], ln2_b[i])
    return h.reshape(BATCH, SEQ, D_MODEL)
```

```python
import functools

import jax
import jax.numpy as jnp
from jax import lax
from jax.experimental import pallas as pl
from jax.experimental.pallas import tpu as pltpu
from jax.experimental.pallas import tpu_sc as plsc

F32 = jnp.float32
BF16 = jnp.bfloat16
I32 = jnp.int32
U32 = jnp.uint32

LANES = 128
SUBLANES = 8
VMEM_LIMIT_BYTES = 48 * 1024 * 1024

D_MODEL = 1024
BATCH = 8
SEQ = 2048
N_TOK = BATCH * SEQ
GLA_HEADS = 4
GLA_DK = 128
GLA_DV = 256
GLA_RANK = 16
GLA_NORMALIZER = 16.0
GLA_CHUNK = 64
GLA_GROUP = 2 * GLA_CHUNK
GLA_UNROLL = 8
MOBA_HEADS = 8
MOBA_DH = 128
MOBA_BLOCK = 256
MOBA_TOPK = 3
ROT_DIM = 32
ROPE_THETA = 500000.0
N_EXPERTS = 256
TOPK_EXPERTS = 8
N_GROUPS = 8
GROUP_SIZE = N_EXPERTS // N_GROUPS
TOPK_GROUPS = 4
D_EXPERT = 256
D_SHARED = 256
ROUTED_SCALE = 2.5
PLE_DIM = 256
LN_EPS = 1e-5
DEPTH = 1
ALPHA = (2.0 * DEPTH) ** 0.25
GLA_KDIM = GLA_HEADS * GLA_DK
GLA_VDIM = GLA_HEADS * GLA_DV
MOBA_DIM = MOBA_HEADS * MOBA_DH
N_KBLK = SEQ // MOBA_BLOCK

COL_GQ = 0
COL_GK = COL_GQ + GLA_KDIM
COL_GV = COL_GK + GLA_KDIM
COL_GR = COL_GV + GLA_VDIM
COL_MQ = COL_GR + GLA_VDIM
COL_MK = COL_MQ + MOBA_DIM
COL_MV = COL_MK + MOBA_DIM
COL_GA = COL_MV + MOBA_DIM
COL_GB = COL_GA + D_MODEL
COL_LOW = COL_GB + D_MODEL
PROJ_COLS = COL_LOW + LANES
LOW_SRC = 2 * GLA_KDIM + 2 * GLA_VDIM

ROW_TILE = 256
PACKED_COLS = D_MODEL // 2
PACKED_CHUNKS = PACKED_COLS // LANES
EXPERT_BUFS = 3
YK_ROWS = TOPK_EXPERTS * N_TOK + EXPERT_BUFS * ROW_TILE
SLOT_CHUNK = 4096
assert EXPERT_BUFS * ROW_TILE <= N_TOK
N_ASSIGN = N_TOK * TOPK_EXPERTS
N_ROW_TILES = (N_ASSIGN + N_EXPERTS * (ROW_TILE - 1) + ROW_TILE - 1) // ROW_TILE
N_ROWS = N_ROW_TILES * ROW_TILE
TILE_TABLE = -(-N_ROW_TILES // LANES) * LANES

PROJ_TM = 1024
REPACK_TN = 512
PROJ_TN = PROJ_COLS // 5
MIX_TM = 512
ROUTE_TM = 512
DEST_TM = 2048
DISP_TM = 256
FINAL_TM = 512

_NEG_INF = float("-inf")
LOG2_E = 1.4426950408889634


def _cparams(*sem):
    return pltpu.CompilerParams(dimension_semantics=sem, vmem_limit_bytes=VMEM_LIMIT_BYTES)


def _dot(a, b):
    return jnp.dot(a, b, preferred_element_type=F32)


def _dot_nt(a, b):
    return lax.dot_general(a, b, (((1,), (1,)), ((), ())), preferred_element_type=F32)


def _dot_tn(a, b):
    return lax.dot_general(a, b, (((0,), (0,)), ((), ())), preferred_element_type=F32)


def _split_bf16(x):
    hi = x.astype(BF16)
    lo = (x - hi.astype(F32)).astype(BF16)
    return hi, lo


def _sigmoid(x):
    return 1.0 / (1.0 + jnp.exp(-x))


def _pack_bf16_pair(x):
    lo = lax.bitcast_convert_type(x[:, :PACKED_COLS].astype(BF16).astype(F32), U32)
    hi = lax.bitcast_convert_type(x[:, PACKED_COLS:].astype(BF16).astype(F32), U32)
    return hi | lax.shift_right_logical(lo, jnp.uint32(16))


def _unpack_bf16_pair(w):
    lo = lax.bitcast_convert_type(lax.shift_left(w, jnp.uint32(16)), F32)
    hi = lax.bitcast_convert_type(w & jnp.uint32(0xFFFF0000), F32)
    return lo, hi


def _repack_body(a_ref, b_ref, o_ref):
    t = pl.program_id(0)
    a = a_ref[...].astype(BF16)

    @pl.when(t < LOW_SRC // REPACK_TN)
    def _():
        o_ref[...] = a

    @pl.when((t >= LOW_SRC // REPACK_TN) & (t < COL_LOW // REPACK_TN))
    def _():
        o_ref[...] = jnp.concatenate([a[GLA_RANK:], b_ref[...].astype(BF16)], axis=0)

    @pl.when(t == COL_LOW // REPACK_TN)
    def _():
        o_ref[...] = jnp.concatenate([a[:GLA_RANK], jnp.zeros((REPACK_TN - GLA_RANK, D_MODEL), BF16)], axis=0)


def _repack_w_in(w_in, layer):
    assert LOW_SRC % REPACK_TN == 0 and COL_LOW % REPACK_TN == 0 and REPACK_TN % GLA_RANK == 0
    w_t = jnp.swapaxes(w_in, 1, 2)
    low_tile = LOW_SRC // REPACK_TN
    last_out = COL_LOW // REPACK_TN
    ranks_per_tile = REPACK_TN // GLA_RANK
    return pl.pallas_call(
        _repack_body,
        grid=(last_out + 1,),
        in_specs=[
            pl.BlockSpec((None, REPACK_TN, D_MODEL), lambda t: (layer, jnp.where(t == last_out, low_tile, t), 0)),
            pl.BlockSpec((None, GLA_RANK, D_MODEL), lambda t: (layer, jnp.minimum(t + 1, last_out) * ranks_per_tile, 0)),
        ],
        out_specs=pl.BlockSpec((REPACK_TN, D_MODEL), lambda t: (t, 0)),
        out_shape=jax.ShapeDtypeStruct((PROJ_COLS, D_MODEL), BF16),
        compiler_params=_cparams("parallel"),
        name="repack_w_in",
    )(w_t, w_t)


def _proj_body(x_ref, w_ref, o_ref):
    o_ref[...] = _dot_nt(x_ref[...].astype(BF16), w_ref[...]).astype(BF16)


def _in_proj(x2d, w_cat):
    return pl.pallas_call(
        _proj_body,
        grid=(PROJ_COLS // PROJ_TN, N_TOK // PROJ_TM),
        in_specs=[
            pl.BlockSpec((PROJ_TM, D_MODEL), lambda j, i: (i, 0)),
            pl.BlockSpec((PROJ_TN, D_MODEL), lambda j, i: (j, 0)),
        ],
        out_specs=pl.BlockSpec((PROJ_TM, PROJ_TN), lambda j, i: (i, j)),
        out_shape=jax.ShapeDtypeStruct((N_TOK, PROJ_COLS), BF16),
        compiler_params=_cparams("parallel", "parallel"),
        name="in_proj",
    )(x2d, w_cat)


def _gla_body(q_ref, k_ref, v_ref, r_ref, low_ref, wg_ref, bg_ref, ng_ref, o_ref, st_ref, gk_ref):
    w_hi, w_lo = _split_bf16(wg_ref[...])
    low = low_ref[...]
    lin = _dot(low, w_hi) + _dot(low, w_lo) + bg_ref[...]
    gk_ref[...] = (jnp.minimum(lin, 0.0) - jnp.log1p(jnp.exp(-jnp.abs(lin)))) * (1.0 / GLA_NORMALIZER)
    st_ref[...] = jnp.zeros_like(st_ref)

    ri = lax.broadcasted_iota(I32, (GLA_GROUP, GLA_GROUP), 0)
    ci = lax.broadcasted_iota(I32, (GLA_GROUP, GLA_GROUP), 1)
    same_chunk = lax.shift_right_logical(ri, GLA_CHUNK.bit_length() - 1) == lax.shift_right_logical(
        ci, GLA_CHUNK.bit_length() - 1)
    causal = same_chunk & (ri >= ci)
    sums = jnp.concatenate([jnp.where(causal, 1.0, 0.0), jnp.where(same_chunk, 1.0, 0.0)], axis=0).astype(BF16)
    gain = ng_ref[...]

    def group(c, carry):
        rows = pl.ds(pl.multiple_of(c * GLA_GROUP, GLA_GROUP), GLA_GROUP)
        g_hi, g_lo = _split_bf16(gk_ref[rows, :])
        bb = _dot(sums, g_hi) + _dot(sums, g_lo)
        b = bb[0:GLA_GROUP]
        b_end = bb[GLA_GROUP:2 * GLA_GROUP]
        q = q_ref[rows, :].astype(F32) * (GLA_DK ** -0.5)
        k = k_ref[rows, :].astype(F32)
        v = v_ref[rows, :]
        q_e = (q * jnp.exp(b)).astype(BF16)
        k_e = (k * jnp.exp(-b)).astype(BF16)
        k_d = (k * jnp.exp(b_end - b)).astype(BF16)
        att = jnp.where(causal, _dot_nt(q_e, k_e), 0.0)
        o = _dot(att.astype(BF16), v)
        st = st_ref[...]
        inter = []
        for j in range(GLA_GROUP // GLA_CHUNK):
            cr = slice(j * GLA_CHUNK, (j + 1) * GLA_CHUNK)
            inter.append(_dot_nt(q_e[cr], st.astype(BF16)))
            st = st * jnp.exp(b_end[j * GLA_CHUNK:j * GLA_CHUNK + 1, :]) + _dot_tn(v[cr], k_d[cr])
        st_ref[...] = st
        o = o + jnp.concatenate(inter, axis=0)
        o = o * lax.rsqrt(jnp.mean(o * o, axis=-1, keepdims=True) + LN_EPS) * gain
        r = r_ref[rows, :].astype(F32)
        o_ref[rows, :] = (o * (r * _sigmoid(r))).astype(BF16)
        return carry

    lax.fori_loop(0, SEQ // GLA_GROUP, group, 0, unroll=GLA_UNROLL)


def _gla(proj, w_gk2_pad, b_gk, norm_g):
    kb, vb = GLA_DK, GLA_DV
    return pl.pallas_call(
        _gla_body,
        grid=(BATCH, GLA_HEADS),
        in_specs=[
            pl.BlockSpec((SEQ, kb), lambda b, h: (b, COL_GQ // kb + h)),
            pl.BlockSpec((SEQ, kb), lambda b, h: (b, COL_GK // kb + h)),
            pl.BlockSpec((SEQ, vb), lambda b, h: (b, COL_GV // vb + h)),
            pl.BlockSpec((SEQ, vb), lambda b, h: (b, COL_GR // vb + h)),
            pl.BlockSpec((SEQ, LANES), lambda b, h: (b, COL_LOW // LANES)),
            pl.BlockSpec((LANES, kb), lambda b, h: (0, h)),
            pl.BlockSpec((1, kb), lambda b, h: (0, h)),
            pl.BlockSpec((1, vb), lambda b, h: (0, 0)),
        ],
        out_specs=pl.BlockSpec((SEQ, vb), lambda b, h: (b, h)),
        out_shape=jax.ShapeDtypeStruct((N_TOK, GLA_VDIM), BF16),
        scratch_shapes=[pltpu.VMEM((vb, kb), F32), pltpu.VMEM((SEQ, kb), F32)],
        compiler_params=_cparams("parallel", "parallel"),
        name="gla",
    )(proj, proj, proj, proj, proj, w_gk2_pad, b_gk, norm_g)


def _moba_body(q_ref, k_ref, v_ref, c_ref, s_ref, o_ref, qs_ref, ks_ref, vt_ref, bias_ref,
               sc_a_ref, sc_b_ref, pr_a_ref, pr_b_ref):
    cos_t = c_ref[0]
    sin_t = s_ref[0]
    lane = lax.broadcasted_iota(I32, (SEQ, MOBA_DH), 1)
    half = ROT_DIM // 2

    def rope(x):
        partner = jnp.where(lane < half, pltpu.roll(x, MOBA_DH - half, 1), pltpu.roll(x, half, 1))
        return x * cos_t + partner * sin_t

    q = rope(q_ref[...].astype(F32))
    k = rope(k_ref[...].astype(F32))
    q_hi, q_lo = _split_bf16(q)
    qs_ref[...] = (q * (MOBA_DH ** -0.5 * LOG2_E)).astype(BF16)
    ks_ref[...] = k.astype(BF16)
    vt_ref[...] = v_ref[...].astype(F32).T.astype(BF16)

    k_mean = jnp.concatenate(
        [jnp.mean(k[j * MOBA_BLOCK:(j + 1) * MOBA_BLOCK], axis=0, keepdims=True) for j in range(N_KBLK)], axis=0)
    m_hi, m_lo = _split_bf16(k_mean)
    s_blk = _dot_nt(m_hi, q_hi) + _dot_nt(m_hi, q_lo) + _dot_nt(m_lo, q_hi)
    blk = lax.broadcasted_iota(I32, (N_KBLK, SEQ), 0)
    q_blk = lax.shift_right_logical(lax.broadcasted_iota(I32, (N_KBLK, SEQ), 1), MOBA_BLOCK.bit_length() - 1)
    past = blk < q_blk
    s_blk = jnp.where(past, s_blk, _NEG_INF)
    beaten = jnp.zeros((N_KBLK, SEQ), I32)
    for j in range(N_KBLK):
        row = s_blk[j:j + 1, :]
        beaten += jnp.where((row > s_blk) | ((row == s_blk) & (j < blk)), 1, 0)
    bias_ref[...] = jnp.where(past & (beaten < MOBA_TOPK), 0.0, _NEG_INF)

    kr = lax.broadcasted_iota(I32, (MOBA_BLOCK, MOBA_BLOCK), 0)
    qc = lax.broadcasted_iota(I32, (MOBA_BLOCK, MOBA_BLOCK), 1)
    own_bias = jnp.where(kr <= qc, 0.0, _NEG_INF)

    sc_bufs = (sc_a_ref, sc_b_ref)
    pr_bufs = (pr_a_ref, pr_b_ref)
    zeros_row = jnp.zeros((1, MOBA_BLOCK), F32)
    future_row = jnp.full((1, MOBA_BLOCK), _NEG_INF, F32)
    for pair in range(N_KBLK // 2):
        q_blocks = (2 * pair, 2 * pair + 1)
        cols = slice(q_blocks[0] * MOBA_BLOCK, (q_blocks[1] + 1) * MOBA_BLOCK)
        n_kblk = q_blocks[1] + 1
        n_keys = n_kblk * MOBA_BLOCK
        sc, pr = sc_bufs[pair % 2], pr_bufs[pair % 2]
        q_pair = qs_ref[cols, :]

        def query_bias(j):
            halves = []
            for qb in q_blocks:
                q_cols = slice(qb * MOBA_BLOCK, (qb + 1) * MOBA_BLOCK)
                halves.append(bias_ref[j:j + 1, q_cols] if j < qb else zeros_row if j == qb else future_row)
            return jnp.concatenate(halves, axis=1)

        biases = [query_bias(j) for j in range(n_kblk)]
        col_max = []
        for j in range(n_kblk):
            rows = slice(j * MOBA_BLOCK, (j + 1) * MOBA_BLOCK)
            s = _dot_nt(ks_ref[rows, :], q_pair)
            if j in q_blocks:
                h = q_blocks.index(j)
                own = s[:, h * MOBA_BLOCK:(h + 1) * MOBA_BLOCK] + own_bias
                s = jnp.concatenate([own, s[:, MOBA_BLOCK:]] if h == 0 else [s[:, :MOBA_BLOCK], own], axis=1)
            sc[rows, :] = s
            col_max.append(jnp.max(s, axis=0, keepdims=True) + biases[j])
        m = functools.reduce(jnp.maximum, col_max)
        denom = jnp.zeros((1, 2 * MOBA_BLOCK), F32)
        for j in range(n_kblk):
            rows = slice(j * MOBA_BLOCK, (j + 1) * MOBA_BLOCK)
            p = jnp.exp2(sc[rows, :] - (m - biases[j]))
            denom = denom + jnp.sum(p, axis=0, keepdims=True)
            pr[rows, :] = p.astype(BF16)
        o_t = _dot(vt_ref[:, 0:n_keys], pr[0:n_keys, :]) * (1.0 / denom)
        o_ref[cols, :] = o_t.T.astype(BF16)


def _moba(proj, cos_t, sin_t):
    dh = MOBA_DH
    return pl.pallas_call(
        _moba_body,
        grid=(BATCH, MOBA_HEADS),
        in_specs=[
            pl.BlockSpec((SEQ, dh), lambda b, h: (b, COL_MQ // dh + h)),
            pl.BlockSpec((SEQ, dh), lambda b, h: (b, COL_MK // dh + h)),
            pl.BlockSpec((SEQ, dh), lambda b, h: (b, COL_MV // dh + h)),
            pl.BlockSpec((1, SEQ, dh), lambda b, h: (b, 0, 0)),
            pl.BlockSpec((1, SEQ, dh), lambda b, h: (b, 0, 0)),
        ],
        out_specs=pl.BlockSpec((SEQ, dh), lambda b, h: (b, h)),
        out_shape=jax.ShapeDtypeStruct((N_TOK, MOBA_DIM), BF16),
        scratch_shapes=[
            pltpu.VMEM((SEQ, dh), BF16),
            pltpu.VMEM((SEQ, dh), BF16),
            pltpu.VMEM((dh, SEQ), BF16),
            pltpu.VMEM((N_KBLK, SEQ), F32),
            pltpu.VMEM((SEQ, 2 * MOBA_BLOCK), F32),
            pltpu.VMEM((SEQ, 2 * MOBA_BLOCK), F32),
            pltpu.VMEM((SEQ, 2 * MOBA_BLOCK), BF16),
            pltpu.VMEM((SEQ, 2 * MOBA_BLOCK), BF16),
        ],
        compiler_params=_cparams("parallel", "parallel"),
        name="moba",
    )(proj, proj, proj, cos_t, sin_t)


def _layer_norm(z, g, b):
    mu = jnp.mean(z, axis=-1, keepdims=True)
    zc = z - mu
    var = jnp.mean(zc * zc, axis=-1, keepdims=True)
    return zc * lax.rsqrt(var + LN_EPS) * g + b


def _mix_body(gla_ref, moba_ref, ga_ref, gb_ref, x_ref, wgo_ref, wmo_ref, wo_ref, g_ref, b_ref,
              h_ref, hb_ref, hp_ref):
    y_gla = _dot(gla_ref[...], wgo_ref[...])
    y_moba = _dot(moba_ref[...], wmo_ref[...])
    merged = _sigmoid(ga_ref[...].astype(F32)) * y_gla + _sigmoid(gb_ref[...].astype(F32)) * y_moba
    mix = _dot(merged.astype(BF16), wo_ref[...])
    h = _layer_norm(ALPHA * x_ref[...] + mix, g_ref[...], b_ref[...])
    h_ref[...] = h
    hb_ref[...] = h.astype(BF16)
    packed = _pack_bf16_pair(h)
    for c in range(PACKED_CHUNKS):
        hp_ref[:, c, :] = packed[:, c * LANES:(c + 1) * LANES]


def _mix(gla_out, moba_out, proj, x2d, w_gla_o, w_moba_o, w_out, ln_g, ln_b):
    d = D_MODEL
    row = lambda i: (i, 0)
    full = lambda i: (0, 0)
    return pl.pallas_call(
        _mix_body,
        grid=(N_TOK // MIX_TM,),
        in_specs=[
            pl.BlockSpec((MIX_TM, d), row),
            pl.BlockSpec((MIX_TM, d), row),
            pl.BlockSpec((MIX_TM, d), lambda i: (i, COL_GA // d)),
            pl.BlockSpec((MIX_TM, d), lambda i: (i, COL_GB // d)),
            pl.BlockSpec((MIX_TM, d), row),
            pl.BlockSpec((d, d), full),
            pl.BlockSpec((d, d), full),
            pl.BlockSpec((d, d), full),
            pl.BlockSpec((1, d), full),
            pl.BlockSpec((1, d), full),
        ],
        out_specs=[
            pl.BlockSpec((MIX_TM, d), row),
            pl.BlockSpec((MIX_TM, d), row),
            pl.BlockSpec((MIX_TM, PACKED_CHUNKS, LANES), lambda i: (i, 0, 0)),
        ],
        out_shape=[
            jax.ShapeDtypeStruct((N_TOK, d), F32),
            jax.ShapeDtypeStruct((N_TOK, d), BF16),
            jax.ShapeDtypeStruct((N_TOK, PACKED_CHUNKS, LANES), U32),
        ],
        compiler_params=_cparams("parallel"),
        name="mix_ln1",
    )(gla_out, moba_out, proj, proj, x2d, w_gla_o, w_moba_o, w_out, ln_g, ln_b)


def _route_body(h_ref, wr_ref, br_ref, e_ref, w_ref, rk_ref, cnt_ref, carry_ref):
    tm = ROUTE_TM

    @pl.when(pl.program_id(0) == 0)
    def _():
        carry_ref[...] = jnp.zeros_like(carry_ref)

    scores = _sigmoid(_dot_nt(wr_ref[...], h_ref[...]))
    biased = scores + br_ref[...]
    row = lax.broadcasted_iota(I32, (N_EXPERTS, tm), 0).astype(F32)
    row_g = lax.broadcasted_iota(I32, (GROUP_SIZE, tm), 0).astype(F32)

    g_scores = []
    for g in range(N_GROUPS):
        grp = biased[g * GROUP_SIZE:(g + 1) * GROUP_SIZE]
        m1 = jnp.max(grp, axis=0, keepdims=True)
        first = jnp.min(jnp.where(grp == m1, row_g, float(GROUP_SIZE)), axis=0, keepdims=True)
        m2 = jnp.max(jnp.where(row_g == first, _NEG_INF, grp), axis=0, keepdims=True)
        g_scores.append(m1 + m2)
    g_score = jnp.concatenate(g_scores, axis=0)
    g_row = lax.broadcasted_iota(I32, (N_GROUPS, tm), 0)
    g_beaten = jnp.zeros((N_GROUPS, tm), I32)
    for g in range(N_GROUPS):
        r = g_score[g:g + 1, :]
        g_beaten += jnp.where((r > g_score) | ((r == g_score) & (g < g_row)), 1, 0)
    g_keep = g_beaten < TOPK_GROUPS
    masked = jnp.concatenate(
        [jnp.where(g_keep[g:g + 1, :], biased[g * GROUP_SIZE:(g + 1) * GROUP_SIZE], _NEG_INF)
         for g in range(N_GROUPS)], axis=0)

    onehot = jnp.zeros((N_EXPERTS, tm), F32)
    picks, pick_scores = [], []
    for _ in range(TOPK_EXPERTS):
        m = jnp.max(masked, axis=0, keepdims=True)
        idx = jnp.min(jnp.where(masked == m, row, float(N_EXPERTS)), axis=0, keepdims=True)
        hit = row == idx
        picks.append(idx)
        pick_scores.append(jnp.sum(jnp.where(hit, scores, 0.0), axis=0, keepdims=True))
        onehot = onehot + jnp.where(hit, 1.0, 0.0)
        masked = jnp.where(hit, _NEG_INF, masked)
    sel = jnp.concatenate(pick_scores, axis=0)
    e_ref[...] = jnp.concatenate(picks, axis=0).astype(I32)
    w_ref[...] = sel / jnp.sum(sel, axis=0, keepdims=True) * ROUTED_SCALE

    t_r = lax.broadcasted_iota(I32, (tm, tm), 0)
    t_c = lax.broadcasted_iota(I32, (tm, tm), 1)
    earlier = jnp.where(t_r < t_c, 1.0, 0.0).astype(BF16)
    seen = _dot(onehot.astype(BF16), earlier) + carry_ref[...]
    rk_ref[...] = jnp.concatenate(
        [jnp.sum(jnp.where(row == idx, seen, 0.0), axis=0, keepdims=True) for idx in picks], axis=0).astype(I32)
    carry_ref[...] += jnp.sum(onehot, axis=1, keepdims=True)
    cnt_ref[...] = carry_ref[...]


def _route(h_bf, w_router_t, b_router_col):
    k = TOPK_EXPERTS
    tok = lambda i: (0, i)
    return pl.pallas_call(
        _route_body,
        grid=(N_TOK // ROUTE_TM,),
        in_specs=[
            pl.BlockSpec((ROUTE_TM, D_MODEL), lambda i: (i, 0)),
            pl.BlockSpec((N_EXPERTS, D_MODEL), lambda i: (0, 0)),
            pl.BlockSpec((N_EXPERTS, 1), lambda i: (0, 0)),
        ],
        out_specs=[
            pl.BlockSpec((k, ROUTE_TM), tok),
            pl.BlockSpec((k, ROUTE_TM), tok),
            pl.BlockSpec((k, ROUTE_TM), tok),
            pl.BlockSpec((N_EXPERTS, 1), lambda i: (0, 0)),
        ],
        out_shape=[
            jax.ShapeDtypeStruct((k, N_TOK), I32),
            jax.ShapeDtypeStruct((k, N_TOK), F32),
            jax.ShapeDtypeStruct((k, N_TOK), I32),
            jax.ShapeDtypeStruct((N_EXPERTS, 1), F32),
        ],
        scratch_shapes=[pltpu.VMEM((N_EXPERTS, 1), F32)],
        compiler_params=_cparams("arbitrary"),
        name="route",
    )(h_bf, w_router_t, b_router_col)


def _dest_body(cnt_ref, e_ref, rk_ref, d_ref, te_ref, nu_ref, lt_ref):
    cnt = cnt_ref[...]
    tiles = jnp.floor((cnt + (ROW_TILE - 1)) * (1.0 / ROW_TILE))
    er = lax.broadcasted_iota(I32, (N_EXPERTS, N_EXPERTS), 0)
    ec = lax.broadcasted_iota(I32, (N_EXPERTS, N_EXPERTS), 1)
    before = jnp.where(ec < er, 1.0, 0.0).astype(BF16)
    tiles_b = jnp.broadcast_to(tiles, (N_EXPERTS, LANES)).astype(BF16)
    t_start = _dot(before, tiles_b)[:, 0:1]
    t_end = t_start + tiles
    p_start = t_start * float(ROW_TILE)
    lt_ref[...] = jnp.where(tiles > 0.0, (t_end - 1.0) * float(ROW_TILE), -1.0).astype(I32)

    row = lax.broadcasted_iota(I32, (N_EXPERTS, DEST_TM), 0)
    d_ref[...] = jnp.concatenate(
        [jnp.sum(jnp.where(row == e_ref[k:k + 1, :], p_start, 0.0), axis=0, keepdims=True)
         for k in range(TOPK_EXPERTS)], axis=0).astype(I32) + rk_ref[...]

    tile_id = lax.broadcasted_iota(I32, (N_EXPERTS, TILE_TABLE), 1).astype(F32)
    owner = jnp.sum(jnp.where(t_end <= tile_id, 1, 0), axis=0, keepdims=True)
    te_ref[...] = jnp.minimum(owner, N_EXPERTS - 1)
    nu_ref[...] = jnp.broadcast_to(t_end[N_EXPERTS - 1:N_EXPERTS, :], (1, LANES)).astype(I32)


def _dest(cnt, e_t, rk_t):
    k = TOPK_EXPERTS
    tok = lambda i: (0, i)
    return pl.pallas_call(
        _dest_body,
        grid=(N_TOK // DEST_TM,),
        in_specs=[
            pl.BlockSpec((N_EXPERTS, 1), lambda i: (0, 0)),
            pl.BlockSpec((k, DEST_TM), tok),
            pl.BlockSpec((k, DEST_TM), tok),
        ],
        out_specs=[
            pl.BlockSpec((k, DEST_TM), tok),
            pl.BlockSpec((1, TILE_TABLE), lambda i: (0, 0)),
            pl.BlockSpec((1, LANES), lambda i: (0, 0)),
            pl.BlockSpec((N_EXPERTS, 1), lambda i: (0, 0)),
        ],
        out_shape=[
            jax.ShapeDtypeStruct((k, N_TOK), I32),
            jax.ShapeDtypeStruct((1, TILE_TABLE), I32),
            jax.ShapeDtypeStruct((1, LANES), I32),
            jax.ShapeDtypeStruct((N_EXPERTS, 1), I32),
        ],
        compiler_params=_cparams("arbitrary"),
        name="dest",
    )(cnt, e_t, rk_t)


def _row_copy(src, dst, sem):
    return pltpu.make_async_copy(src, dst, sem)


def _dispatch_body(lt_ref, nu_ref, d_ref, h_ref, xs_ref, zero_ref, sem):
    @pl.when(pl.program_id(0) == 0)
    def _():
        zero_ref[...] = jnp.zeros_like(zero_ref)

        def zero_tile(first_row):
            r0 = pl.multiple_of(first_row, ROW_TILE)
            return pltpu.make_async_copy(zero_ref, xs_ref.at[pl.ds(r0, ROW_TILE)], sem)

        def z_start(e, carry):
            @pl.when(lt_ref[e] >= 0)
            def _():
                zero_tile(jnp.maximum(lt_ref[e], 0)).start()
            return carry

        def z_wait(e, carry):
            @pl.when(lt_ref[e] >= 0)
            def _():
                zero_tile(jnp.maximum(lt_ref[e], 0)).wait()
            return carry

        def t_start(t, carry):
            zero_tile(t * ROW_TILE).start()
            return carry

        def t_wait(t, carry):
            zero_tile(t * ROW_TILE).wait()
            return carry

        lax.fori_loop(0, N_EXPERTS, z_start, 0)
        lax.fori_loop(nu_ref[0], N_ROW_TILES, t_start, 0)
        lax.fori_loop(0, N_EXPERTS, z_wait, 0)
        lax.fori_loop(nu_ref[0], N_ROW_TILES, t_wait, 0)

    for t in range(DISP_TM):
        for k in range(TOPK_EXPERTS):
            _row_copy(h_ref.at[t], xs_ref.at[d_ref[k, t]], sem).start(priority=k % 2)
    for k in range(TOPK_EXPERTS):
        pltpu.make_async_copy(h_ref, xs_ref.at[pl.ds(0, DISP_TM)], sem).wait()


def _dispatch(last_tile_row, n_used, dest_t, h1):
    grid_spec = pltpu.PrefetchScalarGridSpec(
        num_scalar_prefetch=2,
        grid=(N_TOK // DISP_TM,),
        in_specs=[
            pl.BlockSpec((TOPK_EXPERTS, DISP_TM), lambda i, lt, nu: (0, i), memory_space=pltpu.SMEM),
            pl.BlockSpec((DISP_TM, PACKED_CHUNKS, LANES), lambda i, lt, nu: (i, 0, 0)),
        ],
        out_specs=pl.BlockSpec(memory_space=pl.ANY),
        scratch_shapes=[pltpu.VMEM((ROW_TILE, PACKED_CHUNKS, LANES), U32), pltpu.SemaphoreType.DMA],
    )
    return pl.pallas_call(
        _dispatch_body,
        grid_spec=grid_spec,
        out_shape=jax.ShapeDtypeStruct((N_ROWS, PACKED_CHUNKS, LANES), U32),
        compiler_params=_cparams("arbitrary"),
        name="dispatch",
    )(last_tile_row, n_used, dest_t, h1)


def _row_destinations(dest_flat):
    info = plsc.get_sparse_core_info()
    n_cores, lanes = info.num_cores, info.num_lanes
    n_workers = n_cores * info.num_subcores
    rows_per_worker = -(-N_ROWS // (n_workers * ROW_TILE)) * ROW_TILE
    assert rows_per_worker % lanes == 0
    mesh = plsc.VectorSubcoreMesh(core_axis_name="c", subcore_axis_name="s")

    @functools.partial(
        pl.kernel, mesh=mesh, out_type=jax.ShapeDtypeStruct((n_workers * rows_per_worker,), I32),
        scratch_types=[pltpu.VMEM((SLOT_CHUNK,), I32), pltpu.VMEM((rows_per_worker,), I32)],
        compiler_params=pltpu.CompilerParams(needs_layout_passes=False), name="row_destinations")
    def invert(dest_hbm, out_hbm, dest_v, table_v):
        first_row = (lax.axis_index("s") * n_cores + lax.axis_index("c")) * rows_per_worker
        lane = lax.iota(I32, lanes)
        tile_shift = ROW_TILE.bit_length() - 1

        def spare(i, carry):
            row = first_row + i * lanes + lane
            buf = lax.rem(lax.shift_right_logical(row, tile_shift), EXPERT_BUFS)
            table_v[pl.ds(i * lanes, lanes)] = TOPK_EXPERTS * N_TOK + buf * ROW_TILE + (row & (ROW_TILE - 1))
            return carry

        lax.fori_loop(0, rows_per_worker // lanes, spare, 0)

        def chunk(c, carry):
            pltpu.sync_copy(dest_hbm.at[pl.ds(c * SLOT_CHUNK, SLOT_CHUNK)], dest_v)

            def vec(i, inner):
                local = dest_v[pl.ds(i * lanes, lanes)] - first_row
                mine = (local >= 0) & (local < rows_per_worker)
                plsc.store_scatter(table_v, [jnp.where(mine, local, 0)], lane + (c * SLOT_CHUNK + i * lanes),
                                   mask=mine)
                return inner

            lax.fori_loop(0, SLOT_CHUNK // lanes, vec, 0)
            return carry

        lax.fori_loop(0, N_ASSIGN // SLOT_CHUNK, chunk, 0)
        pltpu.sync_copy(table_v, out_hbm.at[pl.ds(first_row, rows_per_worker)])

    return invert(dest_flat)


def _experts_body(te_ref, nu_ref, lt_ref, slot_ref, xs_ref, wg_ref, wu_ref, wd_ref, yk_ref,
                  wg_f, wu_f, wd_f, wg_b, wu_b, wd_b, y_buf, n_loaded, sem, w_sem):
    i = pl.program_id(0)
    n_used = nu_ref[0]

    def weight_copies(e, s):
        return (pltpu.make_async_copy(wg_ref.at[e], wg_f.at[s], w_sem.at[s]),
                pltpu.make_async_copy(wu_ref.at[e], wu_f.at[s], w_sem.at[s]),
                pltpu.make_async_copy(wd_ref.at[e], wd_f.at[s], w_sem.at[s]))

    def wait_tile(b):
        pltpu.make_async_copy(y_buf.at[b], yk_ref.at[pl.ds(0, ROW_TILE), :], sem.at[b]).wait()

    def send_tile(b):
        for r in range(ROW_TILE):
            dst = slot_ref[0, 0, r]
            _row_copy(y_buf.at[b, pl.ds(r, 1), :], yk_ref.at[pl.ds(dst, 1), :], sem.at[b]).start(priority=r % 2)

    @pl.when(i == 0)
    def _():
        y_buf[EXPERT_BUFS - 1] = jnp.zeros((ROW_TILE, PACKED_COLS), U32)
        spare = [pltpu.make_async_copy(y_buf.at[EXPERT_BUFS - 1],
                                       yk_ref.at[pl.ds(TOPK_EXPERTS * N_TOK + b * ROW_TILE, ROW_TILE), :],
                                       sem.at[b]) for b in range(EXPERT_BUFS - 1)]
        for c in spare:
            c.start()
        for c in spare:
            c.wait()
        n_loaded[0] = 0
        for c in weight_copies(te_ref[0], 0):
            c.start()

    @pl.when((i >= 2) & (i <= n_used))
    def _():
        wait_tile(lax.rem(i, EXPERT_BUFS))

    @pl.when(i < n_used)
    def _():
        e = te_ref[i]
        prev = te_ref[jnp.maximum(i - 1, 0)]

        @pl.when((i == 0) | (e != prev))
        def _():
            s = lax.rem(n_loaded[0], 2)
            for c in weight_copies(e, s):
                c.wait()
            wg_b[...] = wg_f[s].astype(BF16)
            wu_b[...] = wu_f[s].astype(BF16)
            wd_b[...] = wd_f[s].astype(BF16)
            n_loaded[0] = n_loaded[0] + 1
            nxt = lax.shift_right_logical(lt_ref[e], ROW_TILE.bit_length() - 1) + 1

            @pl.when(nxt < n_used)
            def _():
                for c in weight_copies(te_ref[nxt], 1 - s):
                    c.start()

        for phase in range(EXPERT_BUFS):
            @pl.when(lax.rem(i, EXPERT_BUFS) == phase)
            def _(phase=phase):
                send_tile((phase + EXPERT_BUFS - 1) % EXPERT_BUFS)
                packed = jnp.concatenate([xs_ref[:, c, :] for c in range(PACKED_CHUNKS)], axis=1)
                x_lo, x_hi = _unpack_bf16_pair(packed)
                x = jnp.concatenate([x_lo.astype(BF16), x_hi.astype(BF16)], axis=1)
                g = _dot(x, wg_b[...])
                u = _dot(x, wu_b[...])
                h = (g * _sigmoid(g)) * u
                y_buf[phase] = _pack_bf16_pair(_dot(h.astype(BF16), wd_b[...]))

    @pl.when(i == n_used)
    def _():
        send_tile(lax.rem(i + EXPERT_BUFS - 1, EXPERT_BUFS))
        wait_tile(lax.rem(i + EXPERT_BUFS - 2, EXPERT_BUFS))
        wait_tile(lax.rem(i + EXPERT_BUFS - 1, EXPERT_BUFS))


def _experts(tile_expert, n_used, last_tile_row, row_dst, xs, w_gate, w_up, w_down):
    def tile(i, te, nu, lt):
        return (jnp.minimum(i, nu[0] - 1), 0, 0)

    n_table_tiles = row_dst.shape[0] // ROW_TILE
    placeholder = n_table_tiles - 1
    assert placeholder * ROW_TILE >= N_ROWS and placeholder % EXPERT_BUFS == EXPERT_BUFS - 1

    def prev_tile(i, te, nu, lt):
        return (jnp.where(i == 0, placeholder, jnp.minimum(i, nu[0]) - 1), 0, 0)

    grid_spec = pltpu.PrefetchScalarGridSpec(
        num_scalar_prefetch=3,
        grid=(n_used[0] + 1,),
        in_specs=[
            pl.BlockSpec((1, 1, ROW_TILE), prev_tile, memory_space=pltpu.SMEM),
            pl.BlockSpec((ROW_TILE, PACKED_CHUNKS, LANES), tile),
            pl.BlockSpec(memory_space=pl.ANY),
            pl.BlockSpec(memory_space=pl.ANY),
            pl.BlockSpec(memory_space=pl.ANY),
        ],
        out_specs=pl.BlockSpec(memory_space=pl.ANY),
        scratch_shapes=[
            pltpu.VMEM((2, D_MODEL, D_EXPERT), F32),
            pltpu.VMEM((2, D_MODEL, D_EXPERT), F32),
            pltpu.VMEM((2, D_EXPERT, D_MODEL), F32),
            pltpu.VMEM((D_MODEL, D_EXPERT), BF16),
            pltpu.VMEM((D_MODEL, D_EXPERT), BF16),
            pltpu.VMEM((D_EXPERT, D_MODEL), BF16),
            pltpu.VMEM((EXPERT_BUFS, ROW_TILE, PACKED_COLS), U32),
            pltpu.SMEM((1,), I32),
            pltpu.SemaphoreType.DMA((EXPERT_BUFS,)),
            pltpu.SemaphoreType.DMA((2,)),
        ],
    )
    return pl.pallas_call(
        _experts_body,
        grid_spec=grid_spec,
        out_shape=jax.ShapeDtypeStruct((YK_ROWS, PACKED_COLS), U32),
        compiler_params=_cparams("arbitrary"),
        name="experts",
    )(tile_expert, n_used, last_tile_row, row_dst.reshape(n_table_tiles, 1, ROW_TILE), xs, w_gate, w_up, w_down)


def _final_body(w_ref, *refs):
    yk_refs = refs[:TOPK_EXPERTS]
    h_ref, hb_ref, p_ref, wsg_ref, wsu_ref, wsd_ref, wpl_ref, wpg_ref, g_ref, b_ref, o_ref = refs[TOPK_EXPERTS:]
    hb = hb_ref[...]
    sg = _dot(hb, wsg_ref[...])
    shared = _dot(((sg * _sigmoid(sg)) * _dot(hb, wsu_ref[...])).astype(BF16), wsd_ref[...])
    ple = _sigmoid(_dot(hb, wpg_ref[...])) * _dot(p_ref[...].astype(BF16), wpl_ref[...])

    w_col = w_ref[...].T
    y_lo, y_hi = _unpack_bf16_pair(yk_refs[0][...])
    r_lo, r_hi = y_lo * w_col[:, 0:1], y_hi * w_col[:, 0:1]
    for k in range(1, TOPK_EXPERTS):
        y_lo, y_hi = _unpack_bf16_pair(yk_refs[k][...])
        r_lo, r_hi = r_lo + y_lo * w_col[:, k:k + 1], r_hi + y_hi * w_col[:, k:k + 1]
    routed = jnp.concatenate([r_lo, r_hi], axis=1)
    o_ref[...] = _layer_norm(ALPHA * h_ref[...] + (routed + shared) + ple, g_ref[...], b_ref[...])


def _final(w_t, yk, h1, h_bf, p2d, w_s_gate, w_s_up, w_s_down, w_ple, w_ple_gate, ln_g, ln_b):
    d = D_MODEL
    n_steps = N_TOK // FINAL_TM
    full = lambda i: (0, 0)
    return pl.pallas_call(
        _final_body,
        grid=(n_steps,),
        in_specs=[
            pl.BlockSpec((TOPK_EXPERTS, FINAL_TM), lambda i: (0, i)),
            *[pl.BlockSpec((FINAL_TM, PACKED_COLS), lambda i, k=k: (k * n_steps + i, 0)) for k in range(TOPK_EXPERTS)],
            pl.BlockSpec((FINAL_TM, d), lambda i: (i, 0)),
            pl.BlockSpec((FINAL_TM, d), lambda i: (i, 0)),
            pl.BlockSpec((FINAL_TM, PLE_DIM), lambda i: (i, 0)),
            pl.BlockSpec((d, D_SHARED), full),
            pl.BlockSpec((d, D_SHARED), full),
            pl.BlockSpec((D_SHARED, d), full),
            pl.BlockSpec((PLE_DIM, d), full),
            pl.BlockSpec((d, d), full),
            pl.BlockSpec((1, d), full),
            pl.BlockSpec((1, d), full),
        ],
        out_specs=pl.BlockSpec((FINAL_TM, d), lambda i: (i, 0)),
        out_shape=jax.ShapeDtypeStruct((N_TOK, d), F32),
        compiler_params=_cparams("parallel"),
        name="combine_ln2",
    )(w_t, *([yk] * TOPK_EXPERTS), h1, h_bf, p2d, w_s_gate, w_s_up, w_s_down, w_ple, w_ple_gate, ln_g, ln_b)


def _rope_tables(positions):
    half = ROT_DIM // 2
    inv = ROPE_THETA ** (-jnp.arange(0, ROT_DIM, 2, dtype=F32) / ROT_DIM)
    per_row = LANES // half
    pos = jnp.repeat(positions.reshape(-1, per_row), half, axis=1).astype(F32)
    ang = pos * jnp.tile(inv, per_row)[None, :]
    cos, sin = lax.optimization_barrier((jnp.cos(ang), jnp.sin(ang)))
    cos = cos.reshape(positions.shape + (half,))
    sin = sin.reshape(positions.shape + (half,))
    rest = MOBA_DH - ROT_DIM
    ones = jnp.ones(cos.shape[:-1] + (rest,), F32)
    zeros = jnp.zeros(cos.shape[:-1] + (rest,), F32)
    return (jnp.concatenate([cos, cos, ones], axis=-1), jnp.concatenate([-sin, sin, zeros], axis=-1))


def _layer(h2d, p2d, cos_t, sin_t, w_cat, w_gk2, b_gk, norm_g, w_gla_o, w_moba_o, w_out, ln1_g, ln1_b,
           w_router, b_router, w_e_gate, w_e_up, w_e_down, w_s_gate, w_s_up, w_s_down, w_ple, w_ple_gate,
           ln2_g, ln2_b):
    w_gk2_pad = jnp.concatenate([w_gk2, jnp.zeros((LANES - GLA_RANK, GLA_KDIM), w_gk2.dtype)], axis=0)

    proj = _in_proj(h2d, w_cat)
    gla_out = _gla(proj, w_gk2_pad, b_gk[None, :], norm_g[None, :])
    moba_out = _moba(proj, cos_t, sin_t)
    h1, h_bf, h_packed = _mix(gla_out, moba_out, proj, h2d, w_gla_o.astype(BF16), w_moba_o.astype(BF16),
                              w_out.astype(BF16), ln1_g[None, :], ln1_b[None, :])
    e_t, w_t, rk_t, cnt = _route(h_bf, w_router.T.astype(BF16), b_router[:, None])
    dest_t, tile_expert, n_used, last_tile_row = _dest(cnt, e_t, rk_t)
    last_tile_row = last_tile_row.reshape(N_EXPERTS)
    n_used = n_used[0, 0:1]
    xs = _dispatch(last_tile_row, n_used, dest_t, h_packed)
    row_dst = _row_destinations(dest_t.reshape(N_ASSIGN))
    yk = _experts(tile_expert.reshape(TILE_TABLE), n_used, last_tile_row, row_dst, xs,
                  w_e_gate, w_e_up, w_e_down)
    return _final(w_t, yk, h1, h_bf, p2d, w_s_gate.astype(BF16), w_s_up.astype(BF16),
                  w_s_down.astype(BF16), w_ple.astype(BF16), w_ple_gate.astype(BF16),
                  ln2_g[None, :], ln2_b[None, :])


def kernel(x, p, positions, w_in, w_gla_gk2, b_gla_gk, gla_norm_g, w_gla_o, w_moba_o, w_out, ln1_g, ln1_b,
           w_router, b_router, w_e_gate, w_e_up, w_e_down, w_s_gate, w_s_up, w_s_down, w_ple, w_ple_gate,
           ln2_g, ln2_b):
    cos_t, sin_t = _rope_tables(positions)
    h = x.reshape(N_TOK, D_MODEL)
    for i in range(DEPTH):
        h = _layer(h, p[i].reshape(N_TOK, PLE_DIM), cos_t, sin_t, _repack_w_in(w_in, i), w_gla_gk2[i], b_gla_gk[i],
                   gla_norm_g[i], w_gla_o[i], w_moba_o[i], w_out[i], ln1_g[i], ln1_b[i], w_router[i],
                   b_router[i], w_e_gate[i], w_e_up[i], w_e_down[i], w_s_gate[i], w_s_up[i], w_s_down[i],
                   w_ple[i], w_ple_gate[i], ln2_g[i], ln2_b[i])
    return h.reshape(BATCH, SEQ, D_MODEL)
```

```python
import functools

import jax
import jax.numpy as jnp
from jax import lax
from jax.experimental import pallas as pl
from jax.experimental.pallas import tpu as pltpu
from jax.experimental.pallas import tpu_sc as plsc

F32 = jnp.float32
BF16 = jnp.bfloat16
I32 = jnp.int32
U32 = jnp.uint32

LANES = 128
SUBLANES = 8
VMEM_LIMIT_BYTES = 48 * 1024 * 1024

D_MODEL = 1024
BATCH = 8
SEQ = 2048
N_TOK = BATCH * SEQ
GLA_HEADS = 4
GLA_DK = 128
GLA_DV = 256
GLA_RANK = 16
GLA_NORMALIZER = 16.0
GLA_CHUNK = 64
GLA_GROUP = 2 * GLA_CHUNK
GLA_UNROLL = 8
MOBA_HEADS = 8
MOBA_DH = 128
MOBA_BLOCK = 256
MOBA_TOPK = 3
ROT_DIM = 32
ROPE_THETA = 500000.0
N_EXPERTS = 256
TOPK_EXPERTS = 8
N_GROUPS = 8
GROUP_SIZE = N_EXPERTS // N_GROUPS
TOPK_GROUPS = 4
D_EXPERT = 256
D_SHARED = 256
ROUTED_SCALE = 2.5
PLE_DIM = 256
LN_EPS = 1e-5
DEPTH = 1
ALPHA = (2.0 * DEPTH) ** 0.25
GLA_KDIM = GLA_HEADS * GLA_DK
GLA_VDIM = GLA_HEADS * GLA_DV
MOBA_DIM = MOBA_HEADS * MOBA_DH
N_KBLK = SEQ // MOBA_BLOCK

COL_GQ = 0
COL_GK = COL_GQ + GLA_KDIM
COL_GV = COL_GK + GLA_KDIM
COL_GR = COL_GV + GLA_VDIM
COL_MQ = COL_GR + GLA_VDIM
COL_MK = COL_MQ + MOBA_DIM
COL_MV = COL_MK + MOBA_DIM
COL_GA = COL_MV + MOBA_DIM
COL_GB = COL_GA + D_MODEL
COL_LOW = COL_GB + D_MODEL
PROJ_COLS = COL_LOW + LANES
LOW_SRC = 2 * GLA_KDIM + 2 * GLA_VDIM

ROW_TILE = 256
PACKED_COLS = D_MODEL // 2
PACKED_CHUNKS = PACKED_COLS // LANES
EXPERT_BUFS = 3
YK_ROWS = TOPK_EXPERTS * N_TOK + EXPERT_BUFS * ROW_TILE
SLOT_CHUNK = 4096
assert EXPERT_BUFS * ROW_TILE <= N_TOK
N_ASSIGN = N_TOK * TOPK_EXPERTS
N_ROW_TILES = (N_ASSIGN + N_EXPERTS * (ROW_TILE - 1) + ROW_TILE - 1) // ROW_TILE
N_ROWS = N_ROW_TILES * ROW_TILE
TILE_TABLE = -(-N_ROW_TILES // LANES) * LANES

PROJ_TM = 1024
REPACK_TN = 512
PROJ_TN = PROJ_COLS // 5
MIX_TM = 512
ROUTE_TM = 512
DEST_TM = 2048
DISP_TM = 256
FINAL_TM = 512

_NEG_INF = float("-inf")
LOG2_E = 1.4426950408889634


def _cparams(*sem):
    return pltpu.CompilerParams(dimension_semantics=sem, vmem_limit_bytes=VMEM_LIMIT_BYTES)


def _dot(a, b):
    return jnp.dot(a, b, preferred_element_type=F32)


def _dot_nt(a, b):
    return lax.dot_general(a, b, (((1,), (1,)), ((), ())), preferred_element_type=F32)


def _dot_tn(a, b):
    return lax.dot_general(a, b, (((0,), (0,)), ((), ())), preferred_element_type=F32)


def _split_bf16(x):
    hi = x.astype(BF16)
    lo = (x - hi.astype(F32)).astype(BF16)
    return hi, lo


def _sigmoid(x):
    return 1.0 / (1.0 + jnp.exp(-x))


def _pack_bf16_pair(x):
    lo = lax.bitcast_convert_type(x[:, :PACKED_COLS].astype(BF16).astype(F32), U32)
    hi = lax.bitcast_convert_type(x[:, PACKED_COLS:].astype(BF16).astype(F32), U32)
    return hi | lax.shift_right_logical(lo, jnp.uint32(16))


def _unpack_bf16_pair(w):
    lo = lax.bitcast_convert_type(lax.shift_left(w, jnp.uint32(16)), F32)
    hi = lax.bitcast_convert_type(w & jnp.uint32(0xFFFF0000), F32)
    return lo, hi


def _repack_body(a_ref, b_ref, o_ref):
    t = pl.program_id(0)
    a = a_ref[...].astype(BF16)

    @pl.when(t < LOW_SRC // REPACK_TN)
    def _():
        o_ref[...] = a

    @pl.when((t >= LOW_SRC // REPACK_TN) & (t < COL_LOW // REPACK_TN))
    def _():
        o_ref[...] = jnp.concatenate([a[GLA_RANK:], b_ref[...].astype(BF16)], axis=0)

    @pl.when(t == COL_LOW // REPACK_TN)
    def _():
        o_ref[...] = jnp.concatenate([a[:GLA_RANK], jnp.zeros((REPACK_TN - GLA_RANK, D_MODEL), BF16)], axis=0)


def _repack_w_in(w_in, layer):
    assert LOW_SRC % REPACK_TN == 0 and COL_LOW % REPACK_TN == 0 and REPACK_TN % GLA_RANK == 0
    w_t = jnp.swapaxes(w_in, 1, 2)
    low_tile = LOW_SRC // REPACK_TN
    last_out = COL_LOW // REPACK_TN
    ranks_per_tile = REPACK_TN // GLA_RANK
    return pl.pallas_call(
        _repack_body,
        grid=(last_out + 1,),
        in_specs=[
            pl.BlockSpec((None, REPACK_TN, D_MODEL), lambda t: (layer, jnp.where(t == last_out, low_tile, t), 0)),
            pl.BlockSpec((None, GLA_RANK, D_MODEL), lambda t: (layer, jnp.minimum(t + 1, last_out) * ranks_per_tile, 0)),
        ],
        out_specs=pl.BlockSpec((REPACK_TN, D_MODEL), lambda t: (t, 0)),
        out_shape=jax.ShapeDtypeStruct((PROJ_COLS, D_MODEL), BF16),
        compiler_params=_cparams("parallel"),
        name="repack_w_in",
    )(w_t, w_t)


def _proj_body(x_ref, w_ref, o_ref):
    o_ref[...] = _dot_nt(x_ref[...].astype(BF16), w_ref[...]).astype(BF16)


def _in_proj(x2d, w_cat):
    return pl.pallas_call(
        _proj_body,
        grid=(PROJ_COLS // PROJ_TN, N_TOK // PROJ_TM),
        in_specs=[
            pl.BlockSpec((PROJ_TM, D_MODEL), lambda j, i: (i, 0)),
            pl.BlockSpec((PROJ_TN, D_MODEL), lambda j, i: (j, 0)),
        ],
        out_specs=pl.BlockSpec((PROJ_TM, PROJ_TN), lambda j, i: (i, j)),
        out_shape=jax.ShapeDtypeStruct((N_TOK, PROJ_COLS), BF16),
        compiler_params=_cparams("parallel", "parallel"),
        name="in_proj",
    )(x2d, w_cat)


def _gla_body(q_ref, k_ref, v_ref, r_ref, low_ref, wg_ref, bg_ref, ng_ref, o_ref, st_ref, gk_ref):
    w_hi, w_lo = _split_bf16(wg_ref[...])
    low = low_ref[...]
    lin = _dot(low, w_hi) + _dot(low, w_lo) + bg_ref[...]
    gk_ref[...] = (jnp.minimum(lin, 0.0) - jnp.log1p(jnp.exp(-jnp.abs(lin)))) * (1.0 / GLA_NORMALIZER)
    st_ref[...] = jnp.zeros_like(st_ref)

    ri = lax.broadcasted_iota(I32, (GLA_GROUP, GLA_GROUP), 0)
    ci = lax.broadcasted_iota(I32, (GLA_GROUP, GLA_GROUP), 1)
    same_chunk = lax.shift_right_logical(ri, GLA_CHUNK.bit_length() - 1) == lax.shift_right_logical(
        ci, GLA_CHUNK.bit_length() - 1)
    causal = same_chunk & (ri >= ci)
    sums = jnp.concatenate([jnp.where(causal, 1.0, 0.0), jnp.where(same_chunk, 1.0, 0.0)], axis=0).astype(BF16)
    gain = ng_ref[...]

    def group(c, carry):
        rows = pl.ds(pl.multiple_of(c * GLA_GROUP, GLA_GROUP), GLA_GROUP)
        g_hi, g_lo = _split_bf16(gk_ref[rows, :])
        bb = _dot(sums, g_hi) + _dot(sums, g_lo)
        b = bb[0:GLA_GROUP]
        b_end = bb[GLA_GROUP:2 * GLA_GROUP]
        q = q_ref[rows, :].astype(F32) * (GLA_DK ** -0.5)
        k = k_ref[rows, :].astype(F32)
        v = v_ref[rows, :]
        q_e = (q * jnp.exp(b)).astype(BF16)
        k_e = (k * jnp.exp(-b)).astype(BF16)
        k_d = (k * jnp.exp(b_end - b)).astype(BF16)
        att = jnp.where(causal, _dot_nt(q_e, k_e), 0.0)
        o = _dot(att.astype(BF16), v)
        st = st_ref[...]
        inter = []
        for j in range(GLA_GROUP // GLA_CHUNK):
            cr = slice(j * GLA_CHUNK, (j + 1) * GLA_CHUNK)
            inter.append(_dot_nt(q_e[cr], st.astype(BF16)))
            st = st * jnp.exp(b_end[j * GLA_CHUNK:j * GLA_CHUNK + 1, :]) + _dot_tn(v[cr], k_d[cr])
        st_ref[...] = st
        o = o + jnp.concatenate(inter, axis=0)
        o = o * lax.rsqrt(jnp.mean(o * o, axis=-1, keepdims=True) + LN_EPS) * gain
        r = r_ref[rows, :].astype(F32)
        o_ref[rows, :] = (o * (r * _sigmoid(r))).astype(BF16)
        return carry

    lax.fori_loop(0, SEQ // GLA_GROUP, group, 0, unroll=GLA_UNROLL)


def _gla(proj, w_gk2_pad, b_gk, norm_g):
    kb, vb = GLA_DK, GLA_DV
    return pl.pallas_call(
        _gla_body,
        grid=(BATCH, GLA_HEADS),
        in_specs=[
            pl.BlockSpec((SEQ, kb), lambda b, h: (b, COL_GQ // kb + h)),
            pl.BlockSpec((SEQ, kb), lambda b, h: (b, COL_GK // kb + h)),
            pl.BlockSpec((SEQ, vb), lambda b, h: (b, COL_GV // vb + h)),
            pl.BlockSpec((SEQ, vb), lambda b, h: (b, COL_GR // vb + h)),
            pl.BlockSpec((SEQ, LANES), lambda b, h: (b, COL_LOW // LANES)),
            pl.BlockSpec((LANES, kb), lambda b, h: (0, h)),
            pl.BlockSpec((1, kb), lambda b, h: (0, h)),
            pl.BlockSpec((1, vb), lambda b, h: (0, 0)),
        ],
        out_specs=pl.BlockSpec((SEQ, vb), lambda b, h: (b, h)),
        out_shape=jax.ShapeDtypeStruct((N_TOK, GLA_VDIM), BF16),
        scratch_shapes=[pltpu.VMEM((vb, kb), F32), pltpu.VMEM((SEQ, kb), F32)],
        compiler_params=_cparams("parallel", "parallel"),
        name="gla",
    )(proj, proj, proj, proj, proj, w_gk2_pad, b_gk, norm_g)


def _moba_body(q_ref, k_ref, v_ref, c_ref, s_ref, o_ref, qs_ref, ks_ref, vt_ref, bias_ref,
               sc_a_ref, sc_b_ref, pr_a_ref, pr_b_ref):
    cos_t = c_ref[0]
    sin_t = s_ref[0]
    lane = lax.broadcasted_iota(I32, (SEQ, MOBA_DH), 1)
    half = ROT_DIM // 2

    def rope(x):
        partner = jnp.where(lane < half, pltpu.roll(x, MOBA_DH - half, 1), pltpu.roll(x, half, 1))
        return x * cos_t + partner * sin_t

    q = rope(q_ref[...].astype(F32))
    k = rope(k_ref[...].astype(F32))
    q_hi, q_lo = _split_bf16(q)
    qs_ref[...] = (q * (MOBA_DH ** -0.5 * LOG2_E)).astype(BF16)
    ks_ref[...] = k.astype(BF16)
    vt_ref[...] = v_ref[...].astype(F32).T.astype(BF16)

    k_mean = jnp.concatenate(
        [jnp.mean(k[j * MOBA_BLOCK:(j + 1) * MOBA_BLOCK], axis=0, keepdims=True) for j in range(N_KBLK)], axis=0)
    m_hi, m_lo = _split_bf16(k_mean)
    s_blk = _dot_nt(m_hi, q_hi) + _dot_nt(m_hi, q_lo) + _dot_nt(m_lo, q_hi)
    blk = lax.broadcasted_iota(I32, (N_KBLK, SEQ), 0)
    q_blk = lax.shift_right_logical(lax.broadcasted_iota(I32, (N_KBLK, SEQ), 1), MOBA_BLOCK.bit_length() - 1)
    past = blk < q_blk
    s_blk = jnp.where(past, s_blk, _NEG_INF)
    beaten = jnp.zeros((N_KBLK, SEQ), I32)
    for j in range(N_KBLK):
        row = s_blk[j:j + 1, :]
        beaten += jnp.where((row > s_blk) | ((row == s_blk) & (j < blk)), 1, 0)
    bias_ref[...] = jnp.where(past & (beaten < MOBA_TOPK), 0.0, _NEG_INF)

    kr = lax.broadcasted_iota(I32, (MOBA_BLOCK, MOBA_BLOCK), 0)
    qc = lax.broadcasted_iota(I32, (MOBA_BLOCK, MOBA_BLOCK), 1)
    own_bias = jnp.where(kr <= qc, 0.0, _NEG_INF)

    sc_bufs = (sc_a_ref, sc_b_ref)
    pr_bufs = (pr_a_ref, pr_b_ref)
    zeros_row = jnp.zeros((1, MOBA_BLOCK), F32)
    future_row = jnp.full((1, MOBA_BLOCK), _NEG_INF, F32)
    for pair in range(N_KBLK // 2):
        q_blocks = (2 * pair, 2 * pair + 1)
        cols = slice(q_blocks[0] * MOBA_BLOCK, (q_blocks[1] + 1) * MOBA_BLOCK)
        n_kblk = q_blocks[1] + 1
        n_keys = n_kblk * MOBA_BLOCK
        sc, pr = sc_bufs[pair % 2], pr_bufs[pair % 2]
        q_pair = qs_ref[cols, :]

        def query_bias(j):
            halves = []
            for qb in q_blocks:
                q_cols = slice(qb * MOBA_BLOCK, (qb + 1) * MOBA_BLOCK)
                halves.append(bias_ref[j:j + 1, q_cols] if j < qb else zeros_row if j == qb else future_row)
            return jnp.concatenate(halves, axis=1)

        biases = [query_bias(j) for j in range(n_kblk)]
        col_max = []
        for j in range(n_kblk):
            rows = slice(j * MOBA_BLOCK, (j + 1) * MOBA_BLOCK)
            s = _dot_nt(ks_ref[rows, :], q_pair)
            if j in q_blocks:
                h = q_blocks.index(j)
                own = s[:, h * MOBA_BLOCK:(h + 1) * MOBA_BLOCK] + own_bias
                s = jnp.concatenate([own, s[:, MOBA_BLOCK:]] if h == 0 else [s[:, :MOBA_BLOCK], own], axis=1)
            sc[rows, :] = s
            col_max.append(jnp.max(s, axis=0, keepdims=True) + biases[j])
        m = functools.reduce(jnp.maximum, col_max)
        denom = jnp.zeros((1, 2 * MOBA_BLOCK), F32)
        for j in range(n_kblk):
            rows = slice(j * MOBA_BLOCK, (j + 1) * MOBA_BLOCK)
            p = jnp.exp2(sc[rows, :] - (m - biases[j]))
            denom = denom + jnp.sum(p, axis=0, keepdims=True)
            pr[rows, :] = p.astype(BF16)
        o_t = _dot(vt_ref[:, 0:n_keys], pr[0:n_keys, :]) * (1.0 / denom)
        o_ref[cols, :] = o_t.T.astype(BF16)


def _moba(proj, cos_t, sin_t):
    dh = MOBA_DH
    return pl.pallas_call(
        _moba_body,
        grid=(BATCH, MOBA_HEADS),
        in_specs=[
            pl.BlockSpec((SEQ, dh), lambda b, h: (b, COL_MQ // dh + h)),
            pl.BlockSpec((SEQ, dh), lambda b, h: (b, COL_MK // dh + h)),
            pl.BlockSpec((SEQ, dh), lambda b, h: (b, COL_MV // dh + h)),
            pl.BlockSpec((1, SEQ, dh), lambda b, h: (b, 0, 0)),
            pl.BlockSpec((1, SEQ, dh), lambda b, h: (b, 0, 0)),
        ],
        out_specs=pl.BlockSpec((SEQ, dh), lambda b, h: (b, h)),
        out_shape=jax.ShapeDtypeStruct((N_TOK, MOBA_DIM), BF16),
        scratch_shapes=[
            pltpu.VMEM((SEQ, dh), BF16),
            pltpu.VMEM((SEQ, dh), BF16),
            pltpu.VMEM((dh, SEQ), BF16),
            pltpu.VMEM((N_KBLK, SEQ), F32),
            pltpu.VMEM((SEQ, 2 * MOBA_BLOCK), F32),
            pltpu.VMEM((SEQ, 2 * MOBA_BLOCK), F32),
            pltpu.VMEM((SEQ, 2 * MOBA_BLOCK), BF16),
            pltpu.VMEM((SEQ, 2 * MOBA_BLOCK), BF16),
        ],
        compiler_params=_cparams("parallel", "parallel"),
        name="moba",
    )(proj, proj, proj, cos_t, sin_t)


def _layer_norm(z, g, b):
    mu = jnp.mean(z, axis=-1, keepdims=True)
    zc = z - mu
    var = jnp.mean(zc * zc, axis=-1, keepdims=True)
    return zc * lax.rsqrt(var + LN_EPS) * g + b


def _mix_body(gla_ref, moba_ref, ga_ref, gb_ref, x_ref, wgo_ref, wmo_ref, wo_ref, g_ref, b_ref,
              h_ref, hb_ref, hp_ref):
    y_gla = _dot(gla_ref[...], wgo_ref[...])
    y_moba = _dot(moba_ref[...], wmo_ref[...])
    merged = _sigmoid(ga_ref[...].astype(F32)) * y_gla + _sigmoid(gb_ref[...].astype(F32)) * y_moba
    mix = _dot(merged.astype(BF16), wo_ref[...])
    h = _layer_norm(ALPHA * x_ref[...] + mix, g_ref[...], b_ref[...])
    h_ref[...] = h
    hb_ref[...] = h.astype(BF16)
    packed = _pack_bf16_pair(h)
    for c in range(PACKED_CHUNKS):
        hp_ref[:, c, :] = packed[:, c * LANES:(c + 1) * LANES]


def _mix(gla_out, moba_out, proj, x2d, w_gla_o, w_moba_o, w_out, ln_g, ln_b):
    d = D_MODEL
    row = lambda i: (i, 0)
    full = lambda i: (0, 0)
    return pl.pallas_call(
        _mix_body,
        grid=(N_TOK // MIX_TM,),
        in_specs=[
            pl.BlockSpec((MIX_TM, d), row),
            pl.BlockSpec((MIX_TM, d), row),
            pl.BlockSpec((MIX_TM, d), lambda i: (i, COL_GA // d)),
            pl.BlockSpec((MIX_TM, d), lambda i: (i, COL_GB // d)),
            pl.BlockSpec((MIX_TM, d), row),
            pl.BlockSpec((d, d), full),
            pl.BlockSpec((d, d), full),
            pl.BlockSpec((d, d), full),
            pl.BlockSpec((1, d), full),
            pl.BlockSpec((1, d), full),
        ],
        out_specs=[
            pl.BlockSpec((MIX_TM, d), row),
            pl.BlockSpec((MIX_TM, d), row),
            pl.BlockSpec((MIX_TM, PACKED_CHUNKS, LANES), lambda i: (i, 0, 0)),
        ],
        out_shape=[
            jax.ShapeDtypeStruct((N_TOK, d), F32),
            jax.ShapeDtypeStruct((N_TOK, d), BF16),
            jax.ShapeDtypeStruct((N_TOK, PACKED_CHUNKS, LANES), U32),
        ],
        compiler_params=_cparams("parallel"),
        name="mix_ln1",
    )(gla_out, moba_out, proj, proj, x2d, w_gla_o, w_moba_o, w_out, ln_g, ln_b)


def _route_body(h_ref, wr_ref, br_ref, e_ref, w_ref, rk_ref, cnt_ref, carry_ref):
    tm = ROUTE_TM

    @pl.when(pl.program_id(0) == 0)
    def _():
        carry_ref[...] = jnp.zeros_like(carry_ref)

    scores = _sigmoid(_dot_nt(wr_ref[...], h_ref[...]))
    biased = scores + br_ref[...]
    row = lax.broadcasted_iota(I32, (N_EXPERTS, tm), 0).astype(F32)
    row_g = lax.broadcasted_iota(I32, (GROUP_SIZE, tm), 0).astype(F32)

    g_scores = []
    for g in range(N_GROUPS):
        grp = biased[g * GROUP_SIZE:(g + 1) * GROUP_SIZE]
        m1 = jnp.max(grp, axis=0, keepdims=True)
        first = jnp.min(jnp.where(grp == m1, row_g, float(GROUP_SIZE)), axis=0, keepdims=True)
        m2 = jnp.max(jnp.where(row_g == first, _NEG_INF, grp), axis=0, keepdims=True)
        g_scores.append(m1 + m2)
    g_score = jnp.concatenate(g_scores, axis=0)
    g_row = lax.broadcasted_iota(I32, (N_GROUPS, tm), 0)
    g_beaten = jnp.zeros((N_GROUPS, tm), I32)
    for g in range(N_GROUPS):
        r = g_score[g:g + 1, :]
        g_beaten += jnp.where((r > g_score) | ((r == g_score) & (g < g_row)), 1, 0)
    g_keep = g_beaten < TOPK_GROUPS
    masked = jnp.concatenate(
        [jnp.where(g_keep[g:g + 1, :], biased[g * GROUP_SIZE:(g + 1) * GROUP_SIZE], _NEG_INF)
         for g in range(N_GROUPS)], axis=0)

    onehot = jnp.zeros((N_EXPERTS, tm), F32)
    picks, pick_scores = [], []
    for _ in range(TOPK_EXPERTS):
        m = jnp.max(masked, axis=0, keepdims=True)
        idx = jnp.min(jnp.where(masked == m, row, float(N_EXPERTS)), axis=0, keepdims=True)
        hit = row == idx
        picks.append(idx)
        pick_scores.append(jnp.sum(jnp.where(hit, scores, 0.0), axis=0, keepdims=True))
        onehot = onehot + jnp.where(hit, 1.0, 0.0)
        masked = jnp.where(hit, _NEG_INF, masked)
    sel = jnp.concatenate(pick_scores, axis=0)
    e_ref[...] = jnp.concatenate(picks, axis=0).astype(I32)
    w_ref[...] = sel / jnp.sum(sel, axis=0, keepdims=True) * ROUTED_SCALE

    t_r = lax.broadcasted_iota(I32, (tm, tm), 0)
    t_c = lax.broadcasted_iota(I32, (tm, tm), 1)
    earlier = jnp.where(t_r < t_c, 1.0, 0.0).astype(BF16)
    seen = _dot(onehot.astype(BF16), earlier) + carry_ref[...]
    rk_ref[...] = jnp.concatenate(
        [jnp.sum(jnp.where(row == idx, seen, 0.0), axis=0, keepdims=True) for idx in picks], axis=0).astype(I32)
    carry_ref[...] += jnp.sum(onehot, axis=1, keepdims=True)
    cnt_ref[...] = carry_ref[...]


def _route(h_bf, w_router_t, b_router_col):
    k = TOPK_EXPERTS
    tok = lambda i: (0, i)
    return pl.pallas_call(
        _route_body,
        grid=(N_TOK // ROUTE_TM,),
        in_specs=[
            pl.BlockSpec((ROUTE_TM, D_MODEL), lambda i: (i, 0)),
            pl.BlockSpec((N_EXPERTS, D_MODEL), lambda i: (0, 0)),
            pl.BlockSpec((N_EXPERTS, 1), lambda i: (0, 0)),
        ],
        out_specs=[
            pl.BlockSpec((k, ROUTE_TM), tok),
            pl.BlockSpec((k, ROUTE_TM), tok),
            pl.BlockSpec((k, ROUTE_TM), tok),
            pl.BlockSpec((N_EXPERTS, 1), lambda i: (0, 0)),
        ],
        out_shape=[
            jax.ShapeDtypeStruct((k, N_TOK), I32),
            jax.ShapeDtypeStruct((k, N_TOK), F32),
            jax.ShapeDtypeStruct((k, N_TOK), I32),
            jax.ShapeDtypeStruct((N_EXPERTS, 1), F32),
        ],
        scratch_shapes=[pltpu.VMEM((N_EXPERTS, 1), F32)],
        compiler_params=_cparams("arbitrary"),
        name="route",
    )(h_bf, w_router_t, b_router_col)


def _dest_body(cnt_ref, e_ref, rk_ref, d_ref, te_ref, nu_ref, lt_ref, pad_ref):
    cnt = cnt_ref[...]
    tiles = jnp.floor((cnt + (ROW_TILE - 1)) * (1.0 / ROW_TILE))
    er = lax.broadcasted_iota(I32, (N_EXPERTS, N_EXPERTS), 0)
    ec = lax.broadcasted_iota(I32, (N_EXPERTS, N_EXPERTS), 1)
    before = jnp.where(ec < er, 1.0, 0.0).astype(BF16)
    tiles_b = jnp.broadcast_to(tiles, (N_EXPERTS, LANES)).astype(BF16)
    t_start = _dot(before, tiles_b)[:, 0:1]
    t_end = t_start + tiles
    p_start = t_start * float(ROW_TILE)
    lt_ref[...] = jnp.where(tiles > 0.0, (t_end - 1.0) * float(ROW_TILE), -1.0).astype(I32)
    pad_ref[...] = (p_start + cnt).astype(I32)

    row = lax.broadcasted_iota(I32, (N_EXPERTS, DEST_TM), 0)
    d_ref[...] = jnp.concatenate(
        [jnp.sum(jnp.where(row == e_ref[k:k + 1, :], p_start, 0.0), axis=0, keepdims=True)
         for k in range(TOPK_EXPERTS)], axis=0).astype(I32) + rk_ref[...]

    tile_id = lax.broadcasted_iota(I32, (N_EXPERTS, TILE_TABLE), 1).astype(F32)
    owner = jnp.sum(jnp.where(t_end <= tile_id, 1, 0), axis=0, keepdims=True)
    te_ref[...] = jnp.minimum(owner, N_EXPERTS - 1)
    nu_ref[...] = jnp.broadcast_to(t_end[N_EXPERTS - 1:N_EXPERTS, :], (1, LANES)).astype(I32)


def _dest(cnt, e_t, rk_t):
    k = TOPK_EXPERTS
    tok = lambda i: (0, i)
    return pl.pallas_call(
        _dest_body,
        grid=(N_TOK // DEST_TM,),
        in_specs=[
            pl.BlockSpec((N_EXPERTS, 1), lambda i: (0, 0)),
            pl.BlockSpec((k, DEST_TM), tok),
            pl.BlockSpec((k, DEST_TM), tok),
        ],
        out_specs=[
            pl.BlockSpec((k, DEST_TM), tok),
            pl.BlockSpec((1, TILE_TABLE), lambda i: (0, 0)),
            pl.BlockSpec((1, LANES), lambda i: (0, 0)),
            pl.BlockSpec((N_EXPERTS, 1), lambda i: (0, 0)),
            pl.BlockSpec((N_EXPERTS, 1), lambda i: (0, 0)),
        ],
        out_shape=[
            jax.ShapeDtypeStruct((k, N_TOK), I32),
            jax.ShapeDtypeStruct((1, TILE_TABLE), I32),
            jax.ShapeDtypeStruct((1, LANES), I32),
            jax.ShapeDtypeStruct((N_EXPERTS, 1), I32),
            jax.ShapeDtypeStruct((N_EXPERTS, 1), I32),
        ],
        compiler_params=_cparams("arbitrary"),
        name="dest",
    )(cnt, e_t, rk_t)


def _row_copy(src, dst, sem):
    return pltpu.make_async_copy(src, dst, sem)


def _dispatch_body(lt_ref, pad_ref, nu_ref, d_ref, h_ref, xs_ref, zero_ref, sem, zero_sem):
    step = pl.program_id(0)
    n_steps = N_TOK // DISP_TM
    experts_per_step = N_EXPERTS // n_steps
    assert experts_per_step * n_steps == N_EXPERTS
    tail_per_step = -(-(N_ROW_TILES - N_ASSIGN // ROW_TILE) // n_steps)

    @pl.when(step == 0)
    def _():
        zero_ref[...] = jnp.zeros_like(zero_ref)

    def zero_fills(act):
        for j in range(experts_per_step):
            e = step * experts_per_step + j
            row = pad_ref[e]
            n_pad = jnp.where(lt_ref[e] >= 0, lt_ref[e] + ROW_TILE - row, 0)
            for bit in reversed(range(ROW_TILE.bit_length() - 1)):
                run = 1 << bit
                has_run = (n_pad & run) != 0

                @pl.when(has_run)
                def _(row=row, run=run):
                    act(pltpu.make_async_copy(zero_ref.at[pl.ds(0, run)], xs_ref.at[pl.ds(row, run)], zero_sem))

                row = row + jnp.where(has_run, run, 0)
        for j in range(tail_per_step):
            tile = nu_ref[0] + step * tail_per_step + j

            @pl.when(tile < N_ROW_TILES)
            def _(tile=tile):
                r0 = pl.multiple_of(tile * ROW_TILE, ROW_TILE)
                act(pltpu.make_async_copy(zero_ref, xs_ref.at[pl.ds(r0, ROW_TILE)], zero_sem))

    zero_fills(lambda c: c.start())
    for t in range(DISP_TM):
        for k in range(TOPK_EXPERTS):
            _row_copy(h_ref.at[t], xs_ref.at[d_ref[k, t]], sem).start(priority=k % 2)
    for k in range(TOPK_EXPERTS):
        pltpu.make_async_copy(h_ref, xs_ref.at[pl.ds(0, DISP_TM)], sem).wait()
    zero_fills(lambda c: c.wait())


def _dispatch(last_tile_row, pad_row, n_used, dest_t, h1):
    grid_spec = pltpu.PrefetchScalarGridSpec(
        num_scalar_prefetch=3,
        grid=(N_TOK // DISP_TM,),
        in_specs=[
            pl.BlockSpec((TOPK_EXPERTS, DISP_TM), lambda i, lt, pad, nu: (0, i), memory_space=pltpu.SMEM),
            pl.BlockSpec((DISP_TM, PACKED_CHUNKS, LANES), lambda i, lt, pad, nu: (i, 0, 0)),
        ],
        out_specs=pl.BlockSpec(memory_space=pl.ANY),
        scratch_shapes=[pltpu.VMEM((ROW_TILE, PACKED_CHUNKS, LANES), U32), pltpu.SemaphoreType.DMA,
                        pltpu.SemaphoreType.DMA],
    )
    return pl.pallas_call(
        _dispatch_body,
        grid_spec=grid_spec,
        out_shape=jax.ShapeDtypeStruct((N_ROWS, PACKED_CHUNKS, LANES), U32),
        compiler_params=_cparams("arbitrary"),
        name="dispatch",
    )(last_tile_row, pad_row, n_used, dest_t, h1)


def _row_destinations(dest_flat):
    info = plsc.get_sparse_core_info()
    n_cores, lanes = info.num_cores, info.num_lanes
    n_workers = n_cores * info.num_subcores
    rows_per_worker = -(-N_ROWS // (n_workers * ROW_TILE)) * ROW_TILE
    assert rows_per_worker % lanes == 0
    mesh = plsc.VectorSubcoreMesh(core_axis_name="c", subcore_axis_name="s")

    @functools.partial(
        pl.kernel, mesh=mesh, out_type=jax.ShapeDtypeStruct((n_workers * rows_per_worker,), I32),
        scratch_types=[pltpu.VMEM((SLOT_CHUNK,), I32), pltpu.VMEM((rows_per_worker,), I32)],
        compiler_params=pltpu.CompilerParams(needs_layout_passes=False), name="row_destinations")
    def invert(dest_hbm, out_hbm, dest_v, table_v):
        first_row = (lax.axis_index("s") * n_cores + lax.axis_index("c")) * rows_per_worker
        lane = lax.iota(I32, lanes)
        tile_shift = ROW_TILE.bit_length() - 1

        def spare(i, carry):
            row = first_row + i * lanes + lane
            buf = lax.rem(lax.shift_right_logical(row, tile_shift), EXPERT_BUFS)
            table_v[pl.ds(i * lanes, lanes)] = TOPK_EXPERTS * N_TOK + buf * ROW_TILE + (row & (ROW_TILE - 1))
            return carry

        lax.fori_loop(0, rows_per_worker // lanes, spare, 0)

        def chunk(c, carry):
            pltpu.sync_copy(dest_hbm.at[pl.ds(c * SLOT_CHUNK, SLOT_CHUNK)], dest_v)

            def vec(i, inner):
                local = dest_v[pl.ds(i * lanes, lanes)] - first_row
                mine = (local >= 0) & (local < rows_per_worker)
                plsc.store_scatter(table_v, [jnp.where(mine, local, 0)], lane + (c * SLOT_CHUNK + i * lanes),
                                   mask=mine)
                return inner

            lax.fori_loop(0, SLOT_CHUNK // lanes, vec, 0)
            return carry

        lax.fori_loop(0, N_ASSIGN // SLOT_CHUNK, chunk, 0)
        pltpu.sync_copy(table_v, out_hbm.at[pl.ds(first_row, rows_per_worker)])

    return invert(dest_flat)


def _experts_body(te_ref, nu_ref, lt_ref, slot_ref, xs_ref, wg_ref, wu_ref, wd_ref, yk_ref,
                  wg_f, wu_f, wd_f, wg_b, wu_b, wd_b, y_buf, n_loaded, sem, w_sem):
    i = pl.program_id(0)
    n_used = nu_ref[0]

    def weight_copies(e, s):
        return (pltpu.make_async_copy(wg_ref.at[e], wg_f.at[s], w_sem.at[s]),
                pltpu.make_async_copy(wu_ref.at[e], wu_f.at[s], w_sem.at[s]),
                pltpu.make_async_copy(wd_ref.at[e], wd_f.at[s], w_sem.at[s]))

    def wait_tile(b):
        pltpu.make_async_copy(y_buf.at[b], yk_ref.at[pl.ds(0, ROW_TILE), :], sem.at[b]).wait()

    def send_tile(b):
        for r in range(ROW_TILE):
            dst = slot_ref[0, 0, r]
            _row_copy(y_buf.at[b, pl.ds(r, 1), :], yk_ref.at[pl.ds(dst, 1), :], sem.at[b]).start(priority=r % 2)

    @pl.when(i == 0)
    def _():
        y_buf[EXPERT_BUFS - 1] = jnp.zeros((ROW_TILE, PACKED_COLS), U32)
        spare = [pltpu.make_async_copy(y_buf.at[EXPERT_BUFS - 1],
                                       yk_ref.at[pl.ds(TOPK_EXPERTS * N_TOK + b * ROW_TILE, ROW_TILE), :],
                                       sem.at[b]) for b in range(EXPERT_BUFS - 1)]
        for c in spare:
            c.start()
        for c in spare:
            c.wait()
        n_loaded[0] = 0
        for c in weight_copies(te_ref[0], 0):
            c.start()

    @pl.when((i >= 2) & (i <= n_used))
    def _():
        wait_tile(lax.rem(i, EXPERT_BUFS))

    @pl.when(i < n_used)
    def _():
        e = te_ref[i]
        prev = te_ref[jnp.maximum(i - 1, 0)]

        @pl.when((i == 0) | (e != prev))
        def _():
            s = lax.rem(n_loaded[0], 2)
            for c in weight_copies(e, s):
                c.wait()
            wg_b[...] = wg_f[s].astype(BF16)
            wu_b[...] = wu_f[s].astype(BF16)
            wd_b[...] = wd_f[s].astype(BF16)
            n_loaded[0] = n_loaded[0] + 1
            nxt = lax.shift_right_logical(lt_ref[e], ROW_TILE.bit_length() - 1) + 1

            @pl.when(nxt < n_used)
            def _():
                for c in weight_copies(te_ref[nxt], 1 - s):
                    c.start()

        for phase in range(EXPERT_BUFS):
            @pl.when(lax.rem(i, EXPERT_BUFS) == phase)
            def _(phase=phase):
                send_tile((phase + EXPERT_BUFS - 1) % EXPERT_BUFS)
                packed = jnp.concatenate([xs_ref[:, c, :] for c in range(PACKED_CHUNKS)], axis=1)
                x_lo, x_hi = _unpack_bf16_pair(packed)
                x = jnp.concatenate([x_lo.astype(BF16), x_hi.astype(BF16)], axis=1)
                g = _dot(x, wg_b[...])
                u = _dot(x, wu_b[...])
                h = (g * _sigmoid(g)) * u
                y_buf[phase] = _pack_bf16_pair(_dot(h.astype(BF16), wd_b[...]))

    @pl.when(i == n_used)
    def _():
        send_tile(lax.rem(i + EXPERT_BUFS - 1, EXPERT_BUFS))
        wait_tile(lax.rem(i + EXPERT_BUFS - 2, EXPERT_BUFS))
        wait_tile(lax.rem(i + EXPERT_BUFS - 1, EXPERT_BUFS))


def _experts(tile_expert, n_used, last_tile_row, row_dst, xs, w_gate, w_up, w_down):
    def tile(i, te, nu, lt):
        return (jnp.minimum(i, nu[0] - 1), 0, 0)

    n_table_tiles = row_dst.shape[0] // ROW_TILE
    placeholder = n_table_tiles - 1
    assert placeholder * ROW_TILE >= N_ROWS and placeholder % EXPERT_BUFS == EXPERT_BUFS - 1

    def prev_tile(i, te, nu, lt):
        return (jnp.where(i == 0, placeholder, jnp.minimum(i, nu[0]) - 1), 0, 0)

    grid_spec = pltpu.PrefetchScalarGridSpec(
        num_scalar_prefetch=3,
        grid=(n_used[0] + 1,),
        in_specs=[
            pl.BlockSpec((1, 1, ROW_TILE), prev_tile, memory_space=pltpu.SMEM),
            pl.BlockSpec((ROW_TILE, PACKED_CHUNKS, LANES), tile),
            pl.BlockSpec(memory_space=pl.ANY),
            pl.BlockSpec(memory_space=pl.ANY),
            pl.BlockSpec(memory_space=pl.ANY),
        ],
        out_specs=pl.BlockSpec(memory_space=pl.ANY),
        scratch_shapes=[
            pltpu.VMEM((2, D_MODEL, D_EXPERT), F32),
            pltpu.VMEM((2, D_MODEL, D_EXPERT), F32),
            pltpu.VMEM((2, D_EXPERT, D_MODEL), F32),
            pltpu.VMEM((D_MODEL, D_EXPERT), BF16),
            pltpu.VMEM((D_MODEL, D_EXPERT), BF16),
            pltpu.VMEM((D_EXPERT, D_MODEL), BF16),
            pltpu.VMEM((EXPERT_BUFS, ROW_TILE, PACKED_COLS), U32),
            pltpu.SMEM((1,), I32),
            pltpu.SemaphoreType.DMA((EXPERT_BUFS,)),
            pltpu.SemaphoreType.DMA((2,)),
        ],
    )
    return pl.pallas_call(
        _experts_body,
        grid_spec=grid_spec,
        out_shape=jax.ShapeDtypeStruct((YK_ROWS, PACKED_COLS), U32),
        compiler_params=_cparams("arbitrary"),
        name="experts",
    )(tile_expert, n_used, last_tile_row, row_dst.reshape(n_table_tiles, 1, ROW_TILE), xs, w_gate, w_up, w_down)


def _final_body(w_ref, *refs):
    yk_refs = refs[:TOPK_EXPERTS]
    h_ref, hb_ref, p_ref, wsg_ref, wsu_ref, wsd_ref, wpl_ref, wpg_ref, g_ref, b_ref, o_ref = refs[TOPK_EXPERTS:]
    hb = hb_ref[...]
    sg = _dot(hb, wsg_ref[...])
    shared = _dot(((sg * _sigmoid(sg)) * _dot(hb, wsu_ref[...])).astype(BF16), wsd_ref[...])
    ple = _sigmoid(_dot(hb, wpg_ref[...])) * _dot(p_ref[...].astype(BF16), wpl_ref[...])

    w_col = w_ref[...].T
    y_lo, y_hi = _unpack_bf16_pair(yk_refs[0][...])
    r_lo, r_hi = y_lo * w_col[:, 0:1], y_hi * w_col[:, 0:1]
    for k in range(1, TOPK_EXPERTS):
        y_lo, y_hi = _unpack_bf16_pair(yk_refs[k][...])
        r_lo, r_hi = r_lo + y_lo * w_col[:, k:k + 1], r_hi + y_hi * w_col[:, k:k + 1]
    routed = jnp.concatenate([r_lo, r_hi], axis=1)
    o_ref[...] = _layer_norm(ALPHA * h_ref[...] + (routed + shared) + ple, g_ref[...], b_ref[...])


def _final(w_t, yk, h1, h_bf, p2d, w_s_gate, w_s_up, w_s_down, w_ple, w_ple_gate, ln_g, ln_b):
    d = D_MODEL
    n_steps = N_TOK // FINAL_TM
    full = lambda i: (0, 0)
    return pl.pallas_call(
        _final_body,
        grid=(n_steps,),
        in_specs=[
            pl.BlockSpec((TOPK_EXPERTS, FINAL_TM), lambda i: (0, i)),
            *[pl.BlockSpec((FINAL_TM, PACKED_COLS), lambda i, k=k: (k * n_steps + i, 0)) for k in range(TOPK_EXPERTS)],
            pl.BlockSpec((FINAL_TM, d), lambda i: (i, 0)),
            pl.BlockSpec((FINAL_TM, d), lambda i: (i, 0)),
            pl.BlockSpec((FINAL_TM, PLE_DIM), lambda i: (i, 0)),
            pl.BlockSpec((d, D_SHARED), full),
            pl.BlockSpec((d, D_SHARED), full),
            pl.BlockSpec((D_SHARED, d), full),
            pl.BlockSpec((PLE_DIM, d), full),
            pl.BlockSpec((d, d), full),
            pl.BlockSpec((1, d), full),
            pl.BlockSpec((1, d), full),
        ],
        out_specs=pl.BlockSpec((FINAL_TM, d), lambda i: (i, 0)),
        out_shape=jax.ShapeDtypeStruct((N_TOK, d), F32),
        compiler_params=_cparams("parallel"),
        name="combine_ln2",
    )(w_t, *([yk] * TOPK_EXPERTS), h1, h_bf, p2d, w_s_gate, w_s_up, w_s_down, w_ple, w_ple_gate, ln_g, ln_b)


def _rope_tables(positions):
    half = ROT_DIM // 2
    inv = ROPE_THETA ** (-jnp.arange(0, ROT_DIM, 2, dtype=F32) / ROT_DIM)
    per_row = LANES // half
    pos = jnp.repeat(positions.reshape(-1, per_row), half, axis=1).astype(F32)
    ang = pos * jnp.tile(inv, per_row)[None, :]
    cos, sin = lax.optimization_barrier((jnp.cos(ang), jnp.sin(ang)))
    cos = cos.reshape(positions.shape + (half,))
    sin = sin.reshape(positions.shape + (half,))
    rest = MOBA_DH - ROT_DIM
    ones = jnp.ones(cos.shape[:-1] + (rest,), F32)
    zeros = jnp.zeros(cos.shape[:-1] + (rest,), F32)
    return (jnp.concatenate([cos, cos, ones], axis=-1), jnp.concatenate([-sin, sin, zeros], axis=-1))


def _layer(h2d, p2d, cos_t, sin_t, w_cat, w_gk2, b_gk, norm_g, w_gla_o, w_moba_o, w_out, ln1_g, ln1_b,
           w_router, b_router, w_e_gate, w_e_up, w_e_down, w_s_gate, w_s_up, w_s_down, w_ple, w_ple_gate,
           ln2_g, ln2_b):
    w_gk2_pad = jnp.concatenate([w_gk2, jnp.zeros((LANES - GLA_RANK, GLA_KDIM), w_gk2.dtype)], axis=0)

    proj = _in_proj(h2d, w_cat)
    gla_out = _gla(proj, w_gk2_pad, b_gk[None, :], norm_g[None, :])
    moba_out = _moba(proj, cos_t, sin_t)
    h1, h_bf, h_packed = _mix(gla_out, moba_out, proj, h2d, w_gla_o.astype(BF16), w_moba_o.astype(BF16),
                              w_out.astype(BF16), ln1_g[None, :], ln1_b[None, :])
    e_t, w_t, rk_t, cnt = _route(h_bf, w_router.T.astype(BF16), b_router[:, None])
    dest_t, tile_expert, n_used, last_tile_row, pad_row = _dest(cnt, e_t, rk_t)
    last_tile_row = last_tile_row.reshape(N_EXPERTS)
    n_used = n_used[0, 0:1]
    xs = _dispatch(last_tile_row, pad_row.reshape(N_EXPERTS), n_used, dest_t, h_packed)
    row_dst = _row_destinations(dest_t.reshape(N_ASSIGN))
    yk = _experts(tile_expert.reshape(TILE_TABLE), n_used, last_tile_row, row_dst, xs,
                  w_e_gate, w_e_up, w_e_down)
    return _final(w_t, yk, h1, h_bf, p2d, w_s_gate.astype(BF16), w_s_up.astype(BF16),
                  w_s_down.astype(BF16), w_ple.astype(BF16), w_ple_gate.astype(BF16),
                  ln2_g[None, :], ln2_b[None, :])


def kernel(x, p, positions, w_in, w_gla_gk2, b_gla_gk, gla_norm_g, w_gla_o, w_moba_o, w_out, ln1_g, ln1_b,
           w_router, b_router, w_e_gate, w_e_up, w_e_down, w_s_gate, w_s_up, w_s_down, w_ple, w_ple_gate,
           ln2_g, ln2_b):
    cos_t, sin_t = _rope_tables(positions)
    h = x.reshape(N_TOK, D_MODEL)
    for i in range(DEPTH):
        h = _layer(h, p[i].reshape(N_TOK, PLE_DIM), cos_t, sin_t, _repack_w_in(w_in, i), w_gla_gk2[i], b_gla_gk[i],
                   gla_norm_g[i], w_gla_o[i], w_moba_o[i], w_out[i], ln1_g[i], ln1_b[i], w_router[i],
                   b_router[i], w_e_gate[i], w_e_up[i], w_e_down[i], w_s_gate[i], w_s_up[i], w_s_down[i],
                   w_ple[i], w_ple_gate[i], ln2_g[i], ln2_b[i])
    return h.reshape(BATCH, SEQ, D_MODEL)
```

```python
import functools

import jax
import jax.numpy as jnp
from jax import lax
from jax.experimental import pallas as pl
from jax.experimental.pallas import tpu as pltpu
from jax.experimental.pallas import tpu_sc as plsc

F32 = jnp.float32
BF16 = jnp.bfloat16
I32 = jnp.int32
U32 = jnp.uint32

LANES = 128
SUBLANES = 8
VMEM_LIMIT_BYTES = 48 * 1024 * 1024

D_MODEL = 1024
BATCH = 8
SEQ = 2048
N_TOK = BATCH * SEQ
GLA_HEADS = 4
GLA_DK = 128
GLA_DV = 256
GLA_RANK = 16
GLA_NORMALIZER = 16.0
GLA_CHUNK = 64
GLA_GROUP = 2 * GLA_CHUNK
GLA_UNROLL = 8
MOBA_HEADS = 8
MOBA_DH = 128
MOBA_BLOCK = 256
MOBA_TOPK = 3
ROT_DIM = 32
ROPE_THETA = 500000.0
N_EXPERTS = 256
TOPK_EXPERTS = 8
N_GROUPS = 8
GROUP_SIZE = N_EXPERTS // N_GROUPS
TOPK_GROUPS = 4
D_EXPERT = 256
D_SHARED = 256
ROUTED_SCALE = 2.5
PLE_DIM = 256
LN_EPS = 1e-5
DEPTH = 1
ALPHA = (2.0 * DEPTH) ** 0.25
GLA_KDIM = GLA_HEADS * GLA_DK
GLA_VDIM = GLA_HEADS * GLA_DV
MOBA_DIM = MOBA_HEADS * MOBA_DH
N_KBLK = SEQ // MOBA_BLOCK

COL_GQ = 0
COL_GK = COL_GQ + GLA_KDIM
COL_GV = COL_GK + GLA_KDIM
COL_GR = COL_GV + GLA_VDIM
COL_MQ = COL_GR + GLA_VDIM
COL_MK = COL_MQ + MOBA_DIM
COL_MV = COL_MK + MOBA_DIM
COL_GA = COL_MV + MOBA_DIM
COL_GB = COL_GA + D_MODEL
COL_LOW = COL_GB + D_MODEL
PROJ_COLS = COL_LOW + LANES
LOW_SRC = 2 * GLA_KDIM + 2 * GLA_VDIM

ROW_TILE = 256
PACKED_COLS = D_MODEL // 2
PACKED_CHUNKS = PACKED_COLS // LANES
EXPERT_BUFS = 3
YK_ROWS = TOPK_EXPERTS * N_TOK + EXPERT_BUFS * ROW_TILE
SLOT_CHUNK = 4096
assert EXPERT_BUFS * ROW_TILE <= N_TOK
N_ASSIGN = N_TOK * TOPK_EXPERTS
N_ROW_TILES = (N_ASSIGN + N_EXPERTS * (ROW_TILE - 1) + ROW_TILE - 1) // ROW_TILE
N_ROWS = N_ROW_TILES * ROW_TILE
TILE_TABLE = -(-N_ROW_TILES // LANES) * LANES

PROJ_TM = 1024
REPACK_TN = 512
PROJ_TN = PROJ_COLS // 5
MIX_TM = 512
MIX_SUB = 256
ROUTE_TM = 512
DEST_TM = 2048
DISP_TM = 256
FINAL_TM = 512

_NEG_INF = float("-inf")
LOG2_E = 1.4426950408889634


def _cparams(*sem):
    return pltpu.CompilerParams(dimension_semantics=sem, vmem_limit_bytes=VMEM_LIMIT_BYTES)


def _dot(a, b):
    return jnp.dot(a, b, preferred_element_type=F32)


def _dot_nt(a, b):
    return lax.dot_general(a, b, (((1,), (1,)), ((), ())), preferred_element_type=F32)


def _dot_tn(a, b):
    return lax.dot_general(a, b, (((0,), (0,)), ((), ())), preferred_element_type=F32)


def _split_bf16(x):
    hi = x.astype(BF16)
    lo = (x - hi.astype(F32)).astype(BF16)
    return hi, lo


def _sigmoid(x):
    return 1.0 / (1.0 + jnp.exp(-x))


def _pack_bf16_pair(x):
    lo = lax.bitcast_convert_type(x[:, :PACKED_COLS].astype(BF16).astype(F32), U32)
    hi = lax.bitcast_convert_type(x[:, PACKED_COLS:].astype(BF16).astype(F32), U32)
    return hi | lax.shift_right_logical(lo, jnp.uint32(16))


def _unpack_bf16_pair(w):
    lo = lax.bitcast_convert_type(lax.shift_left(w, jnp.uint32(16)), F32)
    hi = lax.bitcast_convert_type(w & jnp.uint32(0xFFFF0000), F32)
    return lo, hi


def _repack_body(a_ref, b_ref, o_ref):
    t = pl.program_id(0)
    a = a_ref[...].astype(BF16)

    @pl.when(t < LOW_SRC // REPACK_TN)
    def _():
        o_ref[...] = a

    @pl.when((t >= LOW_SRC // REPACK_TN) & (t < COL_LOW // REPACK_TN))
    def _():
        o_ref[...] = jnp.concatenate([a[GLA_RANK:], b_ref[...].astype(BF16)], axis=0)

    @pl.when(t == COL_LOW // REPACK_TN)
    def _():
        o_ref[...] = jnp.concatenate([a[:GLA_RANK], jnp.zeros((REPACK_TN - GLA_RANK, D_MODEL), BF16)], axis=0)


def _repack_w_in(w_in, layer):
    assert LOW_SRC % REPACK_TN == 0 and COL_LOW % REPACK_TN == 0 and REPACK_TN % GLA_RANK == 0
    w_t = jnp.swapaxes(w_in, 1, 2)
    low_tile = LOW_SRC // REPACK_TN
    last_out = COL_LOW // REPACK_TN
    ranks_per_tile = REPACK_TN // GLA_RANK
    return pl.pallas_call(
        _repack_body,
        grid=(last_out + 1,),
        in_specs=[
            pl.BlockSpec((None, REPACK_TN, D_MODEL), lambda t: (layer, jnp.where(t == last_out, low_tile, t), 0)),
            pl.BlockSpec((None, GLA_RANK, D_MODEL), lambda t: (layer, jnp.minimum(t + 1, last_out) * ranks_per_tile, 0)),
        ],
        out_specs=pl.BlockSpec((REPACK_TN, D_MODEL), lambda t: (t, 0)),
        out_shape=jax.ShapeDtypeStruct((PROJ_COLS, D_MODEL), BF16),
        compiler_params=_cparams("parallel"),
        name="repack_w_in",
    )(w_t, w_t)


def _proj_body(x_ref, w_ref, o_ref):
    o_ref[...] = _dot_nt(x_ref[...].astype(BF16), w_ref[...]).astype(BF16)


def _in_proj(x2d, w_cat):
    return pl.pallas_call(
        _proj_body,
        grid=(PROJ_COLS // PROJ_TN, N_TOK // PROJ_TM),
        in_specs=[
            pl.BlockSpec((PROJ_TM, D_MODEL), lambda j, i: (i, 0)),
            pl.BlockSpec((PROJ_TN, D_MODEL), lambda j, i: (j, 0)),
        ],
        out_specs=pl.BlockSpec((PROJ_TM, PROJ_TN), lambda j, i: (i, j)),
        out_shape=jax.ShapeDtypeStruct((N_TOK, PROJ_COLS), BF16),
        compiler_params=_cparams("parallel", "parallel"),
        name="in_proj",
    )(x2d, w_cat)


def _gla_body(q_ref, k_ref, v_ref, r_ref, low_ref, wg_ref, bg_ref, ng_ref, o_ref, st_ref, gk_ref):
    w_hi, w_lo = _split_bf16(wg_ref[...])
    low = low_ref[...]
    lin = _dot(low, w_hi) + _dot(low, w_lo) + bg_ref[...]
    gk_ref[...] = (jnp.minimum(lin, 0.0) - jnp.log1p(jnp.exp(-jnp.abs(lin)))) * (1.0 / GLA_NORMALIZER)
    st_ref[...] = jnp.zeros_like(st_ref)

    ri = lax.broadcasted_iota(I32, (GLA_GROUP, GLA_GROUP), 0)
    ci = lax.broadcasted_iota(I32, (GLA_GROUP, GLA_GROUP), 1)
    same_chunk = lax.shift_right_logical(ri, GLA_CHUNK.bit_length() - 1) == lax.shift_right_logical(
        ci, GLA_CHUNK.bit_length() - 1)
    causal = same_chunk & (ri >= ci)
    sums = jnp.concatenate([jnp.where(causal, 1.0, 0.0), jnp.where(same_chunk, 1.0, 0.0)], axis=0).astype(BF16)
    gain = ng_ref[...]

    def group(c, carry):
        rows = pl.ds(pl.multiple_of(c * GLA_GROUP, GLA_GROUP), GLA_GROUP)
        g_hi, g_lo = _split_bf16(gk_ref[rows, :])
        bb = _dot(sums, g_hi) + _dot(sums, g_lo)
        b = bb[0:GLA_GROUP]
        b_end = bb[GLA_GROUP:2 * GLA_GROUP]
        q = q_ref[rows, :].astype(F32) * (GLA_DK ** -0.5)
        k = k_ref[rows, :].astype(F32)
        v = v_ref[rows, :]
        q_e = (q * jnp.exp(b)).astype(BF16)
        k_e = (k * jnp.exp(-b)).astype(BF16)
        k_d = (k * jnp.exp(b_end - b)).astype(BF16)
        att = jnp.where(causal, _dot_nt(q_e, k_e), 0.0)
        o = _dot(att.astype(BF16), v)
        st = st_ref[...]
        inter = []
        for j in range(GLA_GROUP // GLA_CHUNK):
            cr = slice(j * GLA_CHUNK, (j + 1) * GLA_CHUNK)
            inter.append(_dot_nt(q_e[cr], st.astype(BF16)))
            st = st * jnp.exp(b_end[j * GLA_CHUNK:j * GLA_CHUNK + 1, :]) + _dot_tn(v[cr], k_d[cr])
        st_ref[...] = st
        o = o + jnp.concatenate(inter, axis=0)
        o = o * lax.rsqrt(jnp.mean(o * o, axis=-1, keepdims=True) + LN_EPS) * gain
        r = r_ref[rows, :].astype(F32)
        o_ref[rows, :] = (o * (r * _sigmoid(r))).astype(BF16)
        return carry

    lax.fori_loop(0, SEQ // GLA_GROUP, group, 0, unroll=GLA_UNROLL)


def _gla(proj, w_gk2_pad, b_gk, norm_g):
    kb, vb = GLA_DK, GLA_DV
    return pl.pallas_call(
        _gla_body,
        grid=(BATCH, GLA_HEADS),
        in_specs=[
            pl.BlockSpec((SEQ, kb), lambda b, h: (b, COL_GQ // kb + h)),
            pl.BlockSpec((SEQ, kb), lambda b, h: (b, COL_GK // kb + h)),
            pl.BlockSpec((SEQ, vb), lambda b, h: (b, COL_GV // vb + h)),
            pl.BlockSpec((SEQ, vb), lambda b, h: (b, COL_GR // vb + h)),
            pl.BlockSpec((SEQ, LANES), lambda b, h: (b, COL_LOW // LANES)),
            pl.BlockSpec((LANES, kb), lambda b, h: (0, h)),
            pl.BlockSpec((1, kb), lambda b, h: (0, h)),
            pl.BlockSpec((1, vb), lambda b, h: (0, 0)),
        ],
        out_specs=pl.BlockSpec((SEQ, vb), lambda b, h: (b, h)),
        out_shape=jax.ShapeDtypeStruct((N_TOK, GLA_VDIM), BF16),
        scratch_shapes=[pltpu.VMEM((vb, kb), F32), pltpu.VMEM((SEQ, kb), F32)],
        compiler_params=_cparams("parallel", "parallel"),
        name="gla",
    )(proj, proj, proj, proj, proj, w_gk2_pad, b_gk, norm_g)


def _moba_body(q_ref, k_ref, v_ref, c_ref, s_ref, o_ref, qs_ref, ks_ref, vt_ref, bias_ref,
               sc_a_ref, sc_b_ref, pr_a_ref, pr_b_ref):
    cos_t = c_ref[0]
    sin_t = s_ref[0]
    lane = lax.broadcasted_iota(I32, (SEQ, MOBA_DH), 1)
    half = ROT_DIM // 2

    def rope(x):
        partner = jnp.where(lane < half, pltpu.roll(x, MOBA_DH - half, 1), pltpu.roll(x, half, 1))
        return x * cos_t + partner * sin_t

    q = rope(q_ref[...].astype(F32))
    k = rope(k_ref[...].astype(F32))
    q_hi, q_lo = _split_bf16(q)
    qs_ref[...] = (q * (MOBA_DH ** -0.5 * LOG2_E)).astype(BF16)
    ks_ref[...] = k.astype(BF16)
    vt_ref[...] = v_ref[...].astype(F32).T.astype(BF16)

    k_mean = jnp.concatenate(
        [jnp.mean(k[j * MOBA_BLOCK:(j + 1) * MOBA_BLOCK], axis=0, keepdims=True) for j in range(N_KBLK)], axis=0)
    m_hi, m_lo = _split_bf16(k_mean)
    s_blk = _dot_nt(m_hi, q_hi) + _dot_nt(m_hi, q_lo) + _dot_nt(m_lo, q_hi)
    blk = lax.broadcasted_iota(I32, (N_KBLK, SEQ), 0)
    q_blk = lax.shift_right_logical(lax.broadcasted_iota(I32, (N_KBLK, SEQ), 1), MOBA_BLOCK.bit_length() - 1)
    past = blk < q_blk
    s_blk = jnp.where(past, s_blk, _NEG_INF)
    beaten = jnp.zeros((N_KBLK, SEQ), I32)
    for j in range(N_KBLK):
        row = s_blk[j:j + 1, :]
        beaten += jnp.where((row > s_blk) | ((row == s_blk) & (j < blk)), 1, 0)
    bias_ref[...] = jnp.where(past & (beaten < MOBA_TOPK), 0.0, _NEG_INF)

    kr = lax.broadcasted_iota(I32, (MOBA_BLOCK, MOBA_BLOCK), 0)
    qc = lax.broadcasted_iota(I32, (MOBA_BLOCK, MOBA_BLOCK), 1)
    own_bias = jnp.where(kr <= qc, 0.0, _NEG_INF)

    sc_bufs = (sc_a_ref, sc_b_ref)
    pr_bufs = (pr_a_ref, pr_b_ref)
    zeros_row = jnp.zeros((1, MOBA_BLOCK), F32)
    future_row = jnp.full((1, MOBA_BLOCK), _NEG_INF, F32)
    for pair in range(N_KBLK // 2):
        q_blocks = (2 * pair, 2 * pair + 1)
        cols = slice(q_blocks[0] * MOBA_BLOCK, (q_blocks[1] + 1) * MOBA_BLOCK)
        n_kblk = q_blocks[1] + 1
        n_keys = n_kblk * MOBA_BLOCK
        sc, pr = sc_bufs[pair % 2], pr_bufs[pair % 2]
        q_pair = qs_ref[cols, :]

        def query_bias(j):
            halves = []
            for qb in q_blocks:
                q_cols = slice(qb * MOBA_BLOCK, (qb + 1) * MOBA_BLOCK)
                halves.append(bias_ref[j:j + 1, q_cols] if j < qb else zeros_row if j == qb else future_row)
            return jnp.concatenate(halves, axis=1)

        biases = [query_bias(j) for j in range(n_kblk)]
        col_max = []
        for j in range(n_kblk):
            rows = slice(j * MOBA_BLOCK, (j + 1) * MOBA_BLOCK)
            s = _dot_nt(ks_ref[rows, :], q_pair)
            if j in q_blocks:
                h = q_blocks.index(j)
                own = s[:, h * MOBA_BLOCK:(h + 1) * MOBA_BLOCK] + own_bias
                s = jnp.concatenate([own, s[:, MOBA_BLOCK:]] if h == 0 else [s[:, :MOBA_BLOCK], own], axis=1)
            sc[rows, :] = s
            col_max.append(jnp.max(s, axis=0, keepdims=True) + biases[j])
        m = functools.reduce(jnp.maximum, col_max)
        denom = jnp.zeros((1, 2 * MOBA_BLOCK), F32)
        for j in range(n_kblk):
            rows = slice(j * MOBA_BLOCK, (j + 1) * MOBA_BLOCK)
            p = jnp.exp2(sc[rows, :] - (m - biases[j]))
            denom = denom + jnp.sum(p, axis=0, keepdims=True)
            pr[rows, :] = p.astype(BF16)
        o_t = _dot(vt_ref[:, 0:n_keys], pr[0:n_keys, :]) * (1.0 / denom)
        o_ref[cols, :] = o_t.T.astype(BF16)


def _moba(proj, cos_t, sin_t):
    dh = MOBA_DH
    return pl.pallas_call(
        _moba_body,
        grid=(BATCH, MOBA_HEADS),
        in_specs=[
            pl.BlockSpec((SEQ, dh), lambda b, h: (b, COL_MQ // dh + h)),
            pl.BlockSpec((SEQ, dh), lambda b, h: (b, COL_MK // dh + h)),
            pl.BlockSpec((SEQ, dh), lambda b, h: (b, COL_MV // dh + h)),
            pl.BlockSpec((1, SEQ, dh), lambda b, h: (b, 0, 0)),
            pl.BlockSpec((1, SEQ, dh), lambda b, h: (b, 0, 0)),
        ],
        out_specs=pl.BlockSpec((SEQ, dh), lambda b, h: (b, h)),
        out_shape=jax.ShapeDtypeStruct((N_TOK, MOBA_DIM), BF16),
        scratch_shapes=[
            pltpu.VMEM((SEQ, dh), BF16),
            pltpu.VMEM((SEQ, dh), BF16),
            pltpu.VMEM((dh, SEQ), BF16),
            pltpu.VMEM((N_KBLK, SEQ), F32),
            pltpu.VMEM((SEQ, 2 * MOBA_BLOCK), F32),
            pltpu.VMEM((SEQ, 2 * MOBA_BLOCK), F32),
            pltpu.VMEM((SEQ, 2 * MOBA_BLOCK), BF16),
            pltpu.VMEM((SEQ, 2 * MOBA_BLOCK), BF16),
        ],
        compiler_params=_cparams("parallel", "parallel"),
        name="moba",
    )(proj, proj, proj, cos_t, sin_t)


def _layer_norm(z, g, b):
    mu = jnp.mean(z, axis=-1, keepdims=True)
    zc = z - mu
    var = jnp.mean(zc * zc, axis=-1, keepdims=True)
    return zc * lax.rsqrt(var + LN_EPS) * g + b


def _mix_body(gla_ref, moba_ref, ga_ref, gb_ref, x_ref, wgo_ref, wmo_ref, wo_ref, g_ref, b_ref,
              h_ref, hb_ref, hp_ref):
    for s in range(MIX_TM // MIX_SUB):
        rows = slice(s * MIX_SUB, (s + 1) * MIX_SUB)
        y_gla = _dot(gla_ref[rows, :], wgo_ref[...])
        y_moba = _dot(moba_ref[rows, :], wmo_ref[...])
        merged = (_sigmoid(ga_ref[rows, :].astype(F32)) * y_gla
                  + _sigmoid(gb_ref[rows, :].astype(F32)) * y_moba)
        mix = _dot(merged.astype(BF16), wo_ref[...])
        h = _layer_norm(ALPHA * x_ref[rows, :] + mix, g_ref[...], b_ref[...])
        h_ref[rows, :] = h
        hb_ref[rows, :] = h.astype(BF16)
        packed = _pack_bf16_pair(h)
        for c in range(PACKED_CHUNKS):
            hp_ref[pl.ds(s * MIX_SUB * PACKED_CHUNKS + c, MIX_SUB, stride=PACKED_CHUNKS), :] = (
                packed[:, c * LANES:(c + 1) * LANES])


def _mix(gla_out, moba_out, proj, x2d, w_gla_o, w_moba_o, w_out, ln_g, ln_b):
    d = D_MODEL
    row = lambda i: (i, 0)
    full = lambda i: (0, 0)
    return pl.pallas_call(
        _mix_body,
        grid=(N_TOK // MIX_TM,),
        in_specs=[
            pl.BlockSpec((MIX_TM, d), row),
            pl.BlockSpec((MIX_TM, d), row),
            pl.BlockSpec((MIX_TM, d), lambda i: (i, COL_GA // d)),
            pl.BlockSpec((MIX_TM, d), lambda i: (i, COL_GB // d)),
            pl.BlockSpec((MIX_TM, d), row),
            pl.BlockSpec((d, d), full),
            pl.BlockSpec((d, d), full),
            pl.BlockSpec((d, d), full),
            pl.BlockSpec((1, d), full),
            pl.BlockSpec((1, d), full),
        ],
        out_specs=[
            pl.BlockSpec((MIX_TM, d), row),
            pl.BlockSpec((MIX_TM, d), row),
            pl.BlockSpec((MIX_TM * PACKED_CHUNKS, LANES), row),
        ],
        out_shape=[
            jax.ShapeDtypeStruct((N_TOK, d), F32),
            jax.ShapeDtypeStruct((N_TOK, d), BF16),
            jax.ShapeDtypeStruct((N_TOK * PACKED_CHUNKS, LANES), U32),
        ],
        compiler_params=_cparams("parallel"),
        name="mix_ln1",
    )(gla_out, moba_out, proj, proj, x2d, w_gla_o, w_moba_o, w_out, ln_g, ln_b)


def _route_body(h_ref, wr_ref, br_ref, e_ref, w_ref, rk_ref, cnt_ref, carry_ref):
    tm = ROUTE_TM

    @pl.when(pl.program_id(0) == 0)
    def _():
        carry_ref[...] = jnp.zeros_like(carry_ref)

    scores = _sigmoid(_dot_nt(wr_ref[...], h_ref[...]))
    biased = scores + br_ref[...]
    row = lax.broadcasted_iota(I32, (N_EXPERTS, tm), 0).astype(F32)
    row_g = lax.broadcasted_iota(I32, (GROUP_SIZE, tm), 0).astype(F32)

    g_scores = []
    for g in range(N_GROUPS):
        grp = biased[g * GROUP_SIZE:(g + 1) * GROUP_SIZE]
        m1 = jnp.max(grp, axis=0, keepdims=True)
        first = jnp.min(jnp.where(grp == m1, row_g, float(GROUP_SIZE)), axis=0, keepdims=True)
        m2 = jnp.max(jnp.where(row_g == first, _NEG_INF, grp), axis=0, keepdims=True)
        g_scores.append(m1 + m2)
    g_score = jnp.concatenate(g_scores, axis=0)
    g_row = lax.broadcasted_iota(I32, (N_GROUPS, tm), 0)
    g_beaten = jnp.zeros((N_GROUPS, tm), I32)
    for g in range(N_GROUPS):
        r = g_score[g:g + 1, :]
        g_beaten += jnp.where((r > g_score) | ((r == g_score) & (g < g_row)), 1, 0)
    g_keep = g_beaten < TOPK_GROUPS
    masked = jnp.concatenate(
        [jnp.where(g_keep[g:g + 1, :], biased[g * GROUP_SIZE:(g + 1) * GROUP_SIZE], _NEG_INF)
         for g in range(N_GROUPS)], axis=0)

    onehot = jnp.zeros((N_EXPERTS, tm), F32)
    picks, pick_scores = [], []
    for _ in range(TOPK_EXPERTS):
        m = jnp.max(masked, axis=0, keepdims=True)
        idx = jnp.min(jnp.where(masked == m, row, float(N_EXPERTS)), axis=0, keepdims=True)
        hit = row == idx
        picks.append(idx)
        pick_scores.append(jnp.sum(jnp.where(hit, scores, 0.0), axis=0, keepdims=True))
        onehot = onehot + jnp.where(hit, 1.0, 0.0)
        masked = jnp.where(hit, _NEG_INF, masked)
    sel = jnp.concatenate(pick_scores, axis=0)
    e_ref[...] = jnp.concatenate(picks, axis=0).astype(I32)
    w_ref[...] = sel / jnp.sum(sel, axis=0, keepdims=True) * ROUTED_SCALE

    t_r = lax.broadcasted_iota(I32, (tm, tm), 0)
    t_c = lax.broadcasted_iota(I32, (tm, tm), 1)
    earlier = jnp.where(t_r < t_c, 1.0, 0.0).astype(BF16)
    seen = _dot(onehot.astype(BF16), earlier) + carry_ref[...]
    rk_ref[...] = jnp.concatenate(
        [jnp.sum(jnp.where(row == idx, seen, 0.0), axis=0, keepdims=True) for idx in picks], axis=0).astype(I32)
    carry_ref[...] += jnp.sum(onehot, axis=1, keepdims=True)
    cnt_ref[...] = carry_ref[...]


def _route(h_bf, w_router_t, b_router_col):
    k = TOPK_EXPERTS
    tok = lambda i: (0, i)
    return pl.pallas_call(
        _route_body,
        grid=(N_TOK // ROUTE_TM,),
        in_specs=[
            pl.BlockSpec((ROUTE_TM, D_MODEL), lambda i: (i, 0)),
            pl.BlockSpec((N_EXPERTS, D_MODEL), lambda i: (0, 0)),
            pl.BlockSpec((N_EXPERTS, 1), lambda i: (0, 0)),
        ],
        out_specs=[
            pl.BlockSpec((k, ROUTE_TM), tok),
            pl.BlockSpec((k, ROUTE_TM), tok),
            pl.BlockSpec((k, ROUTE_TM), tok),
            pl.BlockSpec((N_EXPERTS, 1), lambda i: (0, 0)),
        ],
        out_shape=[
            jax.ShapeDtypeStruct((k, N_TOK), I32),
            jax.ShapeDtypeStruct((k, N_TOK), F32),
            jax.ShapeDtypeStruct((k, N_TOK), I32),
            jax.ShapeDtypeStruct((N_EXPERTS, 1), F32),
        ],
        scratch_shapes=[pltpu.VMEM((N_EXPERTS, 1), F32)],
        compiler_params=_cparams("arbitrary"),
        name="route",
    )(h_bf, w_router_t, b_router_col)


def _dest_body(cnt_ref, e_ref, rk_ref, d_ref, te_ref, nu_ref, lt_ref, pad_ref):
    cnt = cnt_ref[...]
    tiles = jnp.floor((cnt + (ROW_TILE - 1)) * (1.0 / ROW_TILE))
    er = lax.broadcasted_iota(I32, (N_EXPERTS, N_EXPERTS), 0)
    ec = lax.broadcasted_iota(I32, (N_EXPERTS, N_EXPERTS), 1)
    before = jnp.where(ec < er, 1.0, 0.0).astype(BF16)
    tiles_b = jnp.broadcast_to(tiles, (N_EXPERTS, LANES)).astype(BF16)
    t_start = _dot(before, tiles_b)[:, 0:1]
    t_end = t_start + tiles
    p_start = t_start * float(ROW_TILE)
    lt_ref[...] = jnp.where(tiles > 0.0, (t_end - 1.0) * float(ROW_TILE), -1.0).astype(I32)
    pad_ref[...] = (p_start + cnt).astype(I32)

    row = lax.broadcasted_iota(I32, (N_EXPERTS, DEST_TM), 0)
    d_ref[...] = jnp.concatenate(
        [jnp.sum(jnp.where(row == e_ref[k:k + 1, :], p_start, 0.0), axis=0, keepdims=True)
         for k in range(TOPK_EXPERTS)], axis=0).astype(I32) + rk_ref[...]

    tile_id = lax.broadcasted_iota(I32, (N_EXPERTS, TILE_TABLE), 1).astype(F32)
    owner = jnp.sum(jnp.where(t_end <= tile_id, 1, 0), axis=0, keepdims=True)
    te_ref[...] = jnp.minimum(owner, N_EXPERTS - 1)
    nu_ref[...] = jnp.broadcast_to(t_end[N_EXPERTS - 1:N_EXPERTS, :], (1, LANES)).astype(I32)


def _dest(cnt, e_t, rk_t):
    k = TOPK_EXPERTS
    tok = lambda i: (0, i)
    return pl.pallas_call(
        _dest_body,
        grid=(N_TOK // DEST_TM,),
        in_specs=[
            pl.BlockSpec((N_EXPERTS, 1), lambda i: (0, 0)),
            pl.BlockSpec((k, DEST_TM), tok),
            pl.BlockSpec((k, DEST_TM), tok),
        ],
        out_specs=[
            pl.BlockSpec((k, DEST_TM), tok),
            pl.BlockSpec((1, TILE_TABLE), lambda i: (0, 0)),
            pl.BlockSpec((1, LANES), lambda i: (0, 0)),
            pl.BlockSpec((N_EXPERTS, 1), lambda i: (0, 0)),
            pl.BlockSpec((N_EXPERTS, 1), lambda i: (0, 0)),
        ],
        out_shape=[
            jax.ShapeDtypeStruct((k, N_TOK), I32),
            jax.ShapeDtypeStruct((1, TILE_TABLE), I32),
            jax.ShapeDtypeStruct((1, LANES), I32),
            jax.ShapeDtypeStruct((N_EXPERTS, 1), I32),
            jax.ShapeDtypeStruct((N_EXPERTS, 1), I32),
        ],
        compiler_params=_cparams("arbitrary"),
        name="dest",
    )(cnt, e_t, rk_t)


def _row_copy(src, dst, sem):
    return pltpu.make_async_copy(src, dst, sem)


def _packed_rows(ref, first, n):
    start = first * PACKED_CHUNKS
    if not isinstance(first, int):
        start = pl.multiple_of(start, PACKED_CHUNKS)
    return ref.at[pl.ds(start, n * PACKED_CHUNKS), :]


def _dispatch_body(lt_ref, pad_ref, nu_ref, d_ref, h_ref, xs_ref, zero_ref, sem, zero_sem):
    step = pl.program_id(0)
    n_steps = N_TOK // DISP_TM
    experts_per_step = N_EXPERTS // n_steps
    assert experts_per_step * n_steps == N_EXPERTS
    tail_per_step = -(-(N_ROW_TILES - N_ASSIGN // ROW_TILE) // n_steps)

    @pl.when(step == 0)
    def _():
        zero_ref[...] = jnp.zeros_like(zero_ref)

    def zero_fills(act):
        for j in range(experts_per_step):
            e = step * experts_per_step + j
            row = pad_ref[e]
            n_pad = jnp.where(lt_ref[e] >= 0, lt_ref[e] + ROW_TILE - row, 0)
            for bit in reversed(range(ROW_TILE.bit_length() - 1)):
                run = 1 << bit
                has_run = (n_pad & run) != 0

                @pl.when(has_run)
                def _(row=row, run=run):
                    act(pltpu.make_async_copy(_packed_rows(zero_ref, 0, run), _packed_rows(xs_ref, row, run),
                                              zero_sem))

                row = row + jnp.where(has_run, run, 0)
        for j in range(tail_per_step):
            tile = nu_ref[0] + step * tail_per_step + j

            @pl.when(tile < N_ROW_TILES)
            def _(tile=tile):
                act(pltpu.make_async_copy(zero_ref, _packed_rows(xs_ref, tile * ROW_TILE, ROW_TILE), zero_sem))

    zero_fills(lambda c: c.start())
    for t in range(DISP_TM):
        for k in range(TOPK_EXPERTS):
            _row_copy(_packed_rows(h_ref, t, 1), _packed_rows(xs_ref, d_ref[k, t], 1), sem).start(priority=k % 2)
    for k in range(TOPK_EXPERTS):
        pltpu.make_async_copy(h_ref, _packed_rows(xs_ref, 0, DISP_TM), sem).wait()
    zero_fills(lambda c: c.wait())


def _dispatch(last_tile_row, pad_row, n_used, dest_t, h1):
    grid_spec = pltpu.PrefetchScalarGridSpec(
        num_scalar_prefetch=3,
        grid=(N_TOK // DISP_TM,),
        in_specs=[
            pl.BlockSpec((TOPK_EXPERTS, DISP_TM), lambda i, lt, pad, nu: (0, i), memory_space=pltpu.SMEM),
            pl.BlockSpec((DISP_TM * PACKED_CHUNKS, LANES), lambda i, lt, pad, nu: (i, 0)),
        ],
        out_specs=pl.BlockSpec(memory_space=pl.ANY),
        scratch_shapes=[pltpu.VMEM((ROW_TILE * PACKED_CHUNKS, LANES), U32), pltpu.SemaphoreType.DMA,
                        pltpu.SemaphoreType.DMA],
    )
    return pl.pallas_call(
        _dispatch_body,
        grid_spec=grid_spec,
        out_shape=jax.ShapeDtypeStruct((N_ROWS * PACKED_CHUNKS, LANES), U32),
        compiler_params=_cparams("arbitrary"),
        name="dispatch",
    )(last_tile_row, pad_row, n_used, dest_t, h1)


def _row_destinations(dest_flat):
    info = plsc.get_sparse_core_info()
    n_cores, lanes = info.num_cores, info.num_lanes
    n_workers = n_cores * info.num_subcores
    rows_per_worker = -(-N_ROWS // (n_workers * ROW_TILE)) * ROW_TILE
    assert rows_per_worker % lanes == 0
    mesh = plsc.VectorSubcoreMesh(core_axis_name="c", subcore_axis_name="s")

    @functools.partial(
        pl.kernel, mesh=mesh, out_type=jax.ShapeDtypeStruct((n_workers * rows_per_worker,), I32),
        scratch_types=[pltpu.VMEM((SLOT_CHUNK,), I32), pltpu.VMEM((rows_per_worker,), I32)],
        compiler_params=pltpu.CompilerParams(needs_layout_passes=False), name="row_destinations")
    def invert(dest_hbm, out_hbm, dest_v, table_v):
        first_row = (lax.axis_index("s") * n_cores + lax.axis_index("c")) * rows_per_worker
        lane = lax.iota(I32, lanes)
        tile_shift = ROW_TILE.bit_length() - 1

        def spare(i, carry):
            row = first_row + i * lanes + lane
            buf = lax.rem(lax.shift_right_logical(row, tile_shift), EXPERT_BUFS)
            table_v[pl.ds(i * lanes, lanes)] = TOPK_EXPERTS * N_TOK + buf * ROW_TILE + (row & (ROW_TILE - 1))
            return carry

        lax.fori_loop(0, rows_per_worker // lanes, spare, 0)

        def chunk(c, carry):
            pltpu.sync_copy(dest_hbm.at[pl.ds(c * SLOT_CHUNK, SLOT_CHUNK)], dest_v)

            def vec(i, inner):
                local = dest_v[pl.ds(i * lanes, lanes)] - first_row
                mine = (local >= 0) & (local < rows_per_worker)
                plsc.store_scatter(table_v, [jnp.where(mine, local, 0)], lane + (c * SLOT_CHUNK + i * lanes),
                                   mask=mine)
                return inner

            lax.fori_loop(0, SLOT_CHUNK // lanes, vec, 0)
            return carry

        lax.fori_loop(0, N_ASSIGN // SLOT_CHUNK, chunk, 0)
        pltpu.sync_copy(table_v, out_hbm.at[pl.ds(first_row, rows_per_worker)])

    return invert(dest_flat)


def _experts_body(te_ref, nu_ref, lt_ref, slot_ref, xs_ref, wg_ref, wu_ref, wd_ref, yk_ref,
                  wg_f, wu_f, wd_f, wg_b, wu_b, wd_b, y_buf, n_loaded, sem, w_sem):
    i = pl.program_id(0)
    n_used = nu_ref[0]

    def weight_copies(e, s):
        return (pltpu.make_async_copy(wg_ref.at[e], wg_f.at[s], w_sem.at[s]),
                pltpu.make_async_copy(wu_ref.at[e], wu_f.at[s], w_sem.at[s]),
                pltpu.make_async_copy(wd_ref.at[e], wd_f.at[s], w_sem.at[s]))

    def wait_tile(b):
        pltpu.make_async_copy(y_buf.at[b], yk_ref.at[pl.ds(0, ROW_TILE), :], sem.at[b]).wait()

    def send_tile(b):
        for r in range(ROW_TILE):
            dst = slot_ref[0, 0, r]
            _row_copy(y_buf.at[b, pl.ds(r, 1), :], yk_ref.at[pl.ds(dst, 1), :], sem.at[b]).start(priority=r % 2)

    @pl.when(i == 0)
    def _():
        y_buf[EXPERT_BUFS - 1] = jnp.zeros((ROW_TILE, PACKED_COLS), U32)
        spare = [pltpu.make_async_copy(y_buf.at[EXPERT_BUFS - 1],
                                       yk_ref.at[pl.ds(TOPK_EXPERTS * N_TOK + b * ROW_TILE, ROW_TILE), :],
                                       sem.at[b]) for b in range(EXPERT_BUFS - 1)]
        for c in spare:
            c.start()
        for c in spare:
            c.wait()
        n_loaded[0] = 0
        for c in weight_copies(te_ref[0], 0):
            c.start()

    @pl.when((i >= 2) & (i <= n_used))
    def _():
        wait_tile(lax.rem(i, EXPERT_BUFS))

    @pl.when(i < n_used)
    def _():
        e = te_ref[i]
        prev = te_ref[jnp.maximum(i - 1, 0)]

        @pl.when((i == 0) | (e != prev))
        def _():
            s = lax.rem(n_loaded[0], 2)
            for c in weight_copies(e, s):
                c.wait()
            wg_b[...] = wg_f[s].astype(BF16)
            wu_b[...] = wu_f[s].astype(BF16)
            wd_b[...] = wd_f[s].astype(BF16)
            n_loaded[0] = n_loaded[0] + 1
            nxt = lax.shift_right_logical(lt_ref[e], ROW_TILE.bit_length() - 1) + 1

            @pl.when(nxt < n_used)
            def _():
                for c in weight_copies(te_ref[nxt], 1 - s):
                    c.start()

        for phase in range(EXPERT_BUFS):
            @pl.when(lax.rem(i, EXPERT_BUFS) == phase)
            def _(phase=phase):
                send_tile((phase + EXPERT_BUFS - 1) % EXPERT_BUFS)
                packed = jnp.concatenate([xs_ref[pl.ds(c, ROW_TILE, stride=PACKED_CHUNKS), :]
                                          for c in range(PACKED_CHUNKS)], axis=1)
                x_lo, x_hi = _unpack_bf16_pair(packed)
                x = jnp.concatenate([x_lo.astype(BF16), x_hi.astype(BF16)], axis=1)
                g = _dot(x, wg_b[...])
                u = _dot(x, wu_b[...])
                h = (g * _sigmoid(g)) * u
                y_buf[phase] = _pack_bf16_pair(_dot(h.astype(BF16), wd_b[...]))

    @pl.when(i == n_used)
    def _():
        send_tile(lax.rem(i + EXPERT_BUFS - 1, EXPERT_BUFS))
        wait_tile(lax.rem(i + EXPERT_BUFS - 2, EXPERT_BUFS))
        wait_tile(lax.rem(i + EXPERT_BUFS - 1, EXPERT_BUFS))


def _experts(tile_expert, n_used, last_tile_row, row_dst, xs, w_gate, w_up, w_down):
    def tile(i, te, nu, lt):
        return (jnp.minimum(i, nu[0] - 1), 0)

    n_table_tiles = row_dst.shape[0] // ROW_TILE
    placeholder = n_table_tiles - 1
    assert placeholder * ROW_TILE >= N_ROWS and placeholder % EXPERT_BUFS == EXPERT_BUFS - 1

    def prev_tile(i, te, nu, lt):
        return (jnp.where(i == 0, placeholder, jnp.minimum(i, nu[0]) - 1), 0, 0)

    grid_spec = pltpu.PrefetchScalarGridSpec(
        num_scalar_prefetch=3,
        grid=(n_used[0] + 1,),
        in_specs=[
            pl.BlockSpec((1, 1, ROW_TILE), prev_tile, memory_space=pltpu.SMEM),
            pl.BlockSpec((ROW_TILE * PACKED_CHUNKS, LANES), tile),
            pl.BlockSpec(memory_space=pl.ANY),
            pl.BlockSpec(memory_space=pl.ANY),
            pl.BlockSpec(memory_space=pl.ANY),
        ],
        out_specs=pl.BlockSpec(memory_space=pl.ANY),
        scratch_shapes=[
            pltpu.VMEM((2, D_MODEL, D_EXPERT), F32),
            pltpu.VMEM((2, D_MODEL, D_EXPERT), F32),
            pltpu.VMEM((2, D_EXPERT, D_MODEL), F32),
            pltpu.VMEM((D_MODEL, D_EXPERT), BF16),
            pltpu.VMEM((D_MODEL, D_EXPERT), BF16),
            pltpu.VMEM((D_EXPERT, D_MODEL), BF16),
            pltpu.VMEM((EXPERT_BUFS, ROW_TILE, PACKED_COLS), U32),
            pltpu.SMEM((1,), I32),
            pltpu.SemaphoreType.DMA((EXPERT_BUFS,)),
            pltpu.SemaphoreType.DMA((2,)),
        ],
    )
    return pl.pallas_call(
        _experts_body,
        grid_spec=grid_spec,
        out_shape=jax.ShapeDtypeStruct((YK_ROWS, PACKED_COLS), U32),
        compiler_params=_cparams("arbitrary"),
        name="experts",
    )(tile_expert, n_used, last_tile_row, row_dst.reshape(n_table_tiles, 1, ROW_TILE), xs, w_gate, w_up, w_down)


def _final_body(w_ref, *refs):
    yk_refs = refs[:TOPK_EXPERTS]
    h_ref, hb_ref, p_ref, wsg_ref, wsu_ref, wsd_ref, wpl_ref, wpg_ref, g_ref, b_ref, o_ref = refs[TOPK_EXPERTS:]
    hb = hb_ref[...]
    sg = _dot(hb, wsg_ref[...])
    shared = _dot(((sg * _sigmoid(sg)) * _dot(hb, wsu_ref[...])).astype(BF16), wsd_ref[...])
    ple = _sigmoid(_dot(hb, wpg_ref[...])) * _dot(p_ref[...].astype(BF16), wpl_ref[...])

    w_col = w_ref[...].T
    y_lo, y_hi = _unpack_bf16_pair(yk_refs[0][...])
    r_lo, r_hi = y_lo * w_col[:, 0:1], y_hi * w_col[:, 0:1]
    for k in range(1, TOPK_EXPERTS):
        y_lo, y_hi = _unpack_bf16_pair(yk_refs[k][...])
        r_lo, r_hi = r_lo + y_lo * w_col[:, k:k + 1], r_hi + y_hi * w_col[:, k:k + 1]
    routed = jnp.concatenate([r_lo, r_hi], axis=1)
    o_ref[...] = _layer_norm(ALPHA * h_ref[...] + (routed + shared) + ple, g_ref[...], b_ref[...])


def _final(w_t, yk, h1, h_bf, p2d, w_s_gate, w_s_up, w_s_down, w_ple, w_ple_gate, ln_g, ln_b):
    d = D_MODEL
    n_steps = N_TOK // FINAL_TM
    full = lambda i: (0, 0)
    return pl.pallas_call(
        _final_body,
        grid=(n_steps,),
        in_specs=[
            pl.BlockSpec((TOPK_EXPERTS, FINAL_TM), lambda i: (0, i)),
            *[pl.BlockSpec((FINAL_TM, PACKED_COLS), lambda i, k=k: (k * n_steps + i, 0)) for k in range(TOPK_EXPERTS)],
            pl.BlockSpec((FINAL_TM, d), lambda i: (i, 0)),
            pl.BlockSpec((FINAL_TM, d), lambda i: (i, 0)),
            pl.BlockSpec((FINAL_TM, PLE_DIM), lambda i: (i, 0)),
            pl.BlockSpec((d, D_SHARED), full),
            pl.BlockSpec((d, D_SHARED), full),
            pl.BlockSpec((D_SHARED, d), full),
            pl.BlockSpec((PLE_DIM, d), full),
            pl.BlockSpec((d, d), full),
            pl.BlockSpec((1, d), full),
            pl.BlockSpec((1, d), full),
        ],
        out_specs=pl.BlockSpec((FINAL_TM, d), lambda i: (i, 0)),
        out_shape=jax.ShapeDtypeStruct((N_TOK, d), F32),
        compiler_params=_cparams("parallel"),
        name="combine_ln2",
    )(w_t, *([yk] * TOPK_EXPERTS), h1, h_bf, p2d, w_s_gate, w_s_up, w_s_down, w_ple, w_ple_gate, ln_g, ln_b)


def _rope_tables(positions):
    half = ROT_DIM // 2
    inv = ROPE_THETA ** (-jnp.arange(0, ROT_DIM, 2, dtype=F32) / ROT_DIM)
    per_row = LANES // half
    pos = jnp.repeat(positions.reshape(-1, per_row), half, axis=1).astype(F32)
    ang = pos * jnp.tile(inv, per_row)[None, :]
    cos, sin = lax.optimization_barrier((jnp.cos(ang), jnp.sin(ang)))
    cos = cos.reshape(positions.shape + (half,))
    sin = sin.reshape(positions.shape + (half,))
    rest = MOBA_DH - ROT_DIM
    ones = jnp.ones(cos.shape[:-1] + (rest,), F32)
    zeros = jnp.zeros(cos.shape[:-1] + (rest,), F32)
    return (jnp.concatenate([cos, cos, ones], axis=-1), jnp.concatenate([-sin, sin, zeros], axis=-1))


def _layer(h2d, p2d, cos_t, sin_t, w_cat, w_gk2, b_gk, norm_g, w_gla_o, w_moba_o, w_out, ln1_g, ln1_b,
           w_router, b_router, w_e_gate, w_e_up, w_e_down, w_s_gate, w_s_up, w_s_down, w_ple, w_ple_gate,
           ln2_g, ln2_b):
    w_gk2_pad = jnp.concatenate([w_gk2, jnp.zeros((LANES - GLA_RANK, GLA_KDIM), w_gk2.dtype)], axis=0)

    proj = _in_proj(h2d, w_cat)
    gla_out = _gla(proj, w_gk2_pad, b_gk[None, :], norm_g[None, :])
    moba_out = _moba(proj, cos_t, sin_t)
    h1, h_bf, h_packed = _mix(gla_out, moba_out, proj, h2d, w_gla_o.astype(BF16), w_moba_o.astype(BF16),
                              w_out.astype(BF16), ln1_g[None, :], ln1_b[None, :])
    e_t, w_t, rk_t, cnt = _route(h_bf, w_router.T.astype(BF16), b_router[:, None])
    dest_t, tile_expert, n_used, last_tile_row, pad_row = _dest(cnt, e_t, rk_t)
    last_tile_row = last_tile_row.reshape(N_EXPERTS)
    n_used = n_used[0, 0:1]
    xs = _dispatch(last_tile_row, pad_row.reshape(N_EXPERTS), n_used, dest_t, h_packed)
    row_dst = _row_destinations(dest_t.reshape(N_ASSIGN))
    yk = _experts(tile_expert.reshape(TILE_TABLE), n_used, last_tile_row, row_dst, xs,
                  w_e_gate, w_e_up, w_e_down)
    return _final(w_t, yk, h1, h_bf, p2d, w_s_gate.astype(BF16), w_s_up.astype(BF16),
                  w_s_down.astype(BF16), w_ple.astype(BF16), w_ple_gate.astype(BF16),
                  ln2_g[None, :], ln2_b[None, :])


def kernel(x, p, positions, w_in, w_gla_gk2, b_gla_gk, gla_norm_g, w_gla_o, w_moba_o, w_out, ln1_g, ln1_b,
           w_router, b_router, w_e_gate, w_e_up, w_e_down, w_s_gate, w_s_up, w_s_down, w_ple, w_ple_gate,
           ln2_g, ln2_b):
    cos_t, sin_t = _rope_tables(positions)
    h = x.reshape(N_TOK, D_MODEL)
    for i in range(DEPTH):
        h = _layer(h, p[i].reshape(N_TOK, PLE_DIM), cos_t, sin_t, _repack_w_in(w_in, i), w_gla_gk2[i], b_gla_gk[i],
                   gla_norm_g[i], w_gla_o[i], w_moba_o[i], w_out[i], ln1_g[i], ln1_b[i], w_router[i],
                   b_router[i], w_e_gate[i], w_e_up[i], w_e_down[i], w_s_gate[i], w_s_up[i], w_s_down[i],
                   w_ple[i], w_ple_gate[i], ln2_g[i], ln2_b[i])
    return h.reshape(BATCH, SEQ, D_MODEL)
```

```python
import functools

import jax
import jax.numpy as jnp
from jax import lax
from jax.experimental import pallas as pl
from jax.experimental.pallas import tpu as pltpu
from jax.experimental.pallas import tpu_sc as plsc

F32 = jnp.float32
BF16 = jnp.bfloat16
I32 = jnp.int32
U32 = jnp.uint32

LANES = 128
SUBLANES = 8
VMEM_LIMIT_BYTES = 48 * 1024 * 1024

D_MODEL = 1024
BATCH = 8
SEQ = 2048
N_TOK = BATCH * SEQ
GLA_HEADS = 4
GLA_DK = 128
GLA_DV = 256
GLA_RANK = 16
GLA_NORMALIZER = 16.0
GLA_CHUNK = 64
GLA_GROUP = 2 * GLA_CHUNK
GLA_UNROLL = 8
MOBA_HEADS = 8
MOBA_DH = 128
MOBA_BLOCK = 256
MOBA_TOPK = 3
ROT_DIM = 32
ROPE_THETA = 500000.0
N_EXPERTS = 256
TOPK_EXPERTS = 8
N_GROUPS = 8
GROUP_SIZE = N_EXPERTS // N_GROUPS
TOPK_GROUPS = 4
D_EXPERT = 256
D_SHARED = 256
ROUTED_SCALE = 2.5
PLE_DIM = 256
LN_EPS = 1e-5
DEPTH = 1
ALPHA = (2.0 * DEPTH) ** 0.25
GLA_KDIM = GLA_HEADS * GLA_DK
GLA_VDIM = GLA_HEADS * GLA_DV
MOBA_DIM = MOBA_HEADS * MOBA_DH
N_KBLK = SEQ // MOBA_BLOCK

COL_GQ = 0
COL_GK = COL_GQ + GLA_KDIM
COL_GV = COL_GK + GLA_KDIM
COL_GR = COL_GV + GLA_VDIM
COL_MQ = COL_GR + GLA_VDIM
COL_MK = COL_MQ + MOBA_DIM
COL_MV = COL_MK + MOBA_DIM
COL_GA = COL_MV + MOBA_DIM
COL_GB = COL_GA + D_MODEL
COL_LOW = COL_GB + D_MODEL
PROJ_COLS = COL_LOW + LANES
LOW_SRC = 2 * GLA_KDIM + 2 * GLA_VDIM

ROW_TILE = 256
PACKED_COLS = D_MODEL // 2
PACKED_CHUNKS = PACKED_COLS // LANES
EXPERT_BUFS = 3
YK_ROWS = TOPK_EXPERTS * N_TOK + EXPERT_BUFS * ROW_TILE
SLOT_CHUNK = 4096
assert EXPERT_BUFS * ROW_TILE <= N_TOK
N_ASSIGN = N_TOK * TOPK_EXPERTS
N_ROW_TILES = (N_ASSIGN + N_EXPERTS * (ROW_TILE - 1) + ROW_TILE - 1) // ROW_TILE
N_ROWS = N_ROW_TILES * ROW_TILE
TILE_TABLE = -(-N_ROW_TILES // LANES) * LANES

PROJ_TM = 1024
REPACK_TN = 512
PROJ_TN = PROJ_COLS // 5
MIX_TM = 512
MIX_SUB = 256
ROUTE_TM = 512
DEST_TM = 2048
DISP_TM = 256
FINAL_TM = 512

_NEG_INF = float("-inf")
LOG2_E = 1.4426950408889634


def _cparams(*sem):
    return pltpu.CompilerParams(dimension_semantics=sem, vmem_limit_bytes=VMEM_LIMIT_BYTES)


def _dot(a, b):
    return jnp.dot(a, b, preferred_element_type=F32)


def _dot_nt(a, b):
    return lax.dot_general(a, b, (((1,), (1,)), ((), ())), preferred_element_type=F32)


def _dot_tn(a, b):
    return lax.dot_general(a, b, (((0,), (0,)), ((), ())), preferred_element_type=F32)


def _split_bf16(x):
    hi = x.astype(BF16)
    lo = (x - hi.astype(F32)).astype(BF16)
    return hi, lo


def _sigmoid(x):
    return 1.0 / (1.0 + jnp.exp(-x))


def _pack_bf16_pair(x):
    lo = lax.bitcast_convert_type(x[:, :PACKED_COLS].astype(BF16).astype(F32), U32)
    hi = lax.bitcast_convert_type(x[:, PACKED_COLS:].astype(BF16).astype(F32), U32)
    return hi | lax.shift_right_logical(lo, jnp.uint32(16))


def _unpack_bf16_pair(w):
    lo = lax.bitcast_convert_type(lax.shift_left(w, jnp.uint32(16)), F32)
    hi = lax.bitcast_convert_type(w & jnp.uint32(0xFFFF0000), F32)
    return lo, hi


def _repack_body(a_ref, b_ref, o_ref):
    t = pl.program_id(0)
    a = a_ref[...].astype(BF16)

    @pl.when(t < LOW_SRC // REPACK_TN)
    def _():
        o_ref[...] = a

    @pl.when((t >= LOW_SRC // REPACK_TN) & (t < COL_LOW // REPACK_TN))
    def _():
        o_ref[...] = jnp.concatenate([a[GLA_RANK:], b_ref[...].astype(BF16)], axis=0)

    @pl.when(t == COL_LOW // REPACK_TN)
    def _():
        o_ref[...] = jnp.concatenate([a[:GLA_RANK], jnp.zeros((REPACK_TN - GLA_RANK, D_MODEL), BF16)], axis=0)


def _repack_w_in(w_in, layer):
    assert LOW_SRC % REPACK_TN == 0 and COL_LOW % REPACK_TN == 0 and REPACK_TN % GLA_RANK == 0
    w_t = jnp.swapaxes(w_in, 1, 2)
    low_tile = LOW_SRC // REPACK_TN
    last_out = COL_LOW // REPACK_TN
    ranks_per_tile = REPACK_TN // GLA_RANK
    return pl.pallas_call(
        _repack_body,
        grid=(last_out + 1,),
        in_specs=[
            pl.BlockSpec((None, REPACK_TN, D_MODEL), lambda t: (layer, jnp.where(t == last_out, low_tile, t), 0)),
            pl.BlockSpec((None, GLA_RANK, D_MODEL), lambda t: (layer, jnp.minimum(t + 1, last_out) * ranks_per_tile, 0)),
        ],
        out_specs=pl.BlockSpec((REPACK_TN, D_MODEL), lambda t: (t, 0)),
        out_shape=jax.ShapeDtypeStruct((PROJ_COLS, D_MODEL), BF16),
        compiler_params=_cparams("parallel"),
        name="repack_w_in",
    )(w_t, w_t)


def _proj_body(x_ref, w_ref, o_ref):
    o_ref[...] = _dot_nt(x_ref[...].astype(BF16), w_ref[...]).astype(BF16)


def _in_proj(x2d, w_cat):
    return pl.pallas_call(
        _proj_body,
        grid=(PROJ_COLS // PROJ_TN, N_TOK // PROJ_TM),
        in_specs=[
            pl.BlockSpec((PROJ_TM, D_MODEL), lambda j, i: (i, 0)),
            pl.BlockSpec((PROJ_TN, D_MODEL), lambda j, i: (j, 0)),
        ],
        out_specs=pl.BlockSpec((PROJ_TM, PROJ_TN), lambda j, i: (i, j)),
        out_shape=jax.ShapeDtypeStruct((N_TOK, PROJ_COLS), BF16),
        compiler_params=_cparams("parallel", "parallel"),
        name="in_proj",
    )(x2d, w_cat)


def _gla_body(q_ref, k_ref, v_ref, r_ref, low_ref, wg_ref, bg_ref, ng_ref, o_ref, st_ref, gk_ref):
    w_hi, w_lo = _split_bf16(wg_ref[...])
    low = low_ref[...]
    lin = _dot(low, w_hi) + _dot(low, w_lo) + bg_ref[...]
    gk_ref[...] = (jnp.minimum(lin, 0.0) - jnp.log1p(jnp.exp(-jnp.abs(lin)))) * (1.0 / GLA_NORMALIZER)
    st_ref[...] = jnp.zeros_like(st_ref)

    ri = lax.broadcasted_iota(I32, (GLA_GROUP, GLA_GROUP), 0)
    ci = lax.broadcasted_iota(I32, (GLA_GROUP, GLA_GROUP), 1)
    same_chunk = lax.shift_right_logical(ri, GLA_CHUNK.bit_length() - 1) == lax.shift_right_logical(
        ci, GLA_CHUNK.bit_length() - 1)
    causal = same_chunk & (ri >= ci)
    sums = jnp.concatenate([jnp.where(causal, 1.0, 0.0), jnp.where(same_chunk, 1.0, 0.0)], axis=0).astype(BF16)
    gain = ng_ref[...]

    def group(c, carry):
        rows = pl.ds(pl.multiple_of(c * GLA_GROUP, GLA_GROUP), GLA_GROUP)
        g_hi, g_lo = _split_bf16(gk_ref[rows, :])
        bb = _dot(sums, g_hi) + _dot(sums, g_lo)
        b = bb[0:GLA_GROUP]
        b_end = bb[GLA_GROUP:2 * GLA_GROUP]
        q = q_ref[rows, :].astype(F32) * (GLA_DK ** -0.5)
        k = k_ref[rows, :].astype(F32)
        v = v_ref[rows, :]
        q_e = (q * jnp.exp(b)).astype(BF16)
        k_e = (k * jnp.exp(-b)).astype(BF16)
        k_d = (k * jnp.exp(b_end - b)).astype(BF16)
        att = jnp.where(causal, _dot_nt(q_e, k_e), 0.0)
        o = _dot(att.astype(BF16), v)
        st = st_ref[...]
        inter = []
        for j in range(GLA_GROUP // GLA_CHUNK):
            cr = slice(j * GLA_CHUNK, (j + 1) * GLA_CHUNK)
            inter.append(_dot_nt(q_e[cr], st.astype(BF16)))
            st = st * jnp.exp(b_end[j * GLA_CHUNK:j * GLA_CHUNK + 1, :]) + _dot_tn(v[cr], k_d[cr])
        st_ref[...] = st
        o = o + jnp.concatenate(inter, axis=0)
        o = o * lax.rsqrt(jnp.mean(o * o, axis=-1, keepdims=True) + LN_EPS) * gain
        r = r_ref[rows, :].astype(F32)
        o_ref[rows, :] = (o * (r * _sigmoid(r))).astype(BF16)
        return carry

    lax.fori_loop(0, SEQ // GLA_GROUP, group, 0, unroll=GLA_UNROLL)


def _gla(proj, w_gk2_pad, b_gk, norm_g):
    kb, vb = GLA_DK, GLA_DV
    return pl.pallas_call(
        _gla_body,
        grid=(BATCH, GLA_HEADS),
        in_specs=[
            pl.BlockSpec((SEQ, kb), lambda b, h: (b, COL_GQ // kb + h)),
            pl.BlockSpec((SEQ, kb), lambda b, h: (b, COL_GK // kb + h)),
            pl.BlockSpec((SEQ, vb), lambda b, h: (b, COL_GV // vb + h)),
            pl.BlockSpec((SEQ, vb), lambda b, h: (b, COL_GR // vb + h)),
            pl.BlockSpec((SEQ, LANES), lambda b, h: (b, COL_LOW // LANES)),
            pl.BlockSpec((LANES, kb), lambda b, h: (0, h)),
            pl.BlockSpec((1, kb), lambda b, h: (0, h)),
            pl.BlockSpec((1, vb), lambda b, h: (0, 0)),
        ],
        out_specs=pl.BlockSpec((SEQ, vb), lambda b, h: (b, h)),
        out_shape=jax.ShapeDtypeStruct((N_TOK, GLA_VDIM), BF16),
        scratch_shapes=[pltpu.VMEM((vb, kb), F32), pltpu.VMEM((SEQ, kb), F32)],
        compiler_params=_cparams("parallel", "parallel"),
        name="gla",
    )(proj, proj, proj, proj, proj, w_gk2_pad, b_gk, norm_g)


def _moba_body(q_ref, k_ref, v_ref, c_ref, s_ref, o_ref, qs_ref, ks_ref, vt_ref, bias_ref,
               sc_a_ref, sc_b_ref, pr_a_ref, pr_b_ref):
    cos_t = c_ref[0]
    sin_t = s_ref[0]
    lane = lax.broadcasted_iota(I32, (SEQ, MOBA_DH), 1)
    half = ROT_DIM // 2

    def rope(x):
        partner = jnp.where(lane < half, pltpu.roll(x, MOBA_DH - half, 1), pltpu.roll(x, half, 1))
        return x * cos_t + partner * sin_t

    q = rope(q_ref[...].astype(F32))
    k = rope(k_ref[...].astype(F32))
    q_hi, q_lo = _split_bf16(q)
    qs_ref[...] = (q * (MOBA_DH ** -0.5 * LOG2_E)).astype(BF16)
    ks_ref[...] = k.astype(BF16)
    vt_ref[...] = v_ref[...].astype(F32).T.astype(BF16)

    k_mean = jnp.concatenate(
        [jnp.mean(k[j * MOBA_BLOCK:(j + 1) * MOBA_BLOCK], axis=0, keepdims=True) for j in range(N_KBLK)], axis=0)
    m_hi, m_lo = _split_bf16(k_mean)
    s_blk = _dot_nt(m_hi, q_hi) + _dot_nt(m_hi, q_lo) + _dot_nt(m_lo, q_hi)
    blk = lax.broadcasted_iota(I32, (N_KBLK, SEQ), 0)
    q_blk = lax.shift_right_logical(lax.broadcasted_iota(I32, (N_KBLK, SEQ), 1), MOBA_BLOCK.bit_length() - 1)
    past = blk < q_blk
    s_blk = jnp.where(past, s_blk, _NEG_INF)
    beaten = jnp.zeros((N_KBLK, SEQ), I32)
    for j in range(N_KBLK):
        row = s_blk[j:j + 1, :]
        beaten += jnp.where((row > s_blk) | ((row == s_blk) & (j < blk)), 1, 0)
    bias_ref[...] = jnp.where(past & (beaten < MOBA_TOPK), 0.0, _NEG_INF)

    kr = lax.broadcasted_iota(I32, (MOBA_BLOCK, MOBA_BLOCK), 0)
    qc = lax.broadcasted_iota(I32, (MOBA_BLOCK, MOBA_BLOCK), 1)
    own_bias = jnp.where(kr <= qc, 0.0, _NEG_INF)

    sc_bufs = (sc_a_ref, sc_b_ref)
    pr_bufs = (pr_a_ref, pr_b_ref)
    zeros_row = jnp.zeros((1, MOBA_BLOCK), F32)
    future_row = jnp.full((1, MOBA_BLOCK), _NEG_INF, F32)
    for pair in range(N_KBLK // 2):
        q_blocks = (2 * pair, 2 * pair + 1)
        cols = slice(q_blocks[0] * MOBA_BLOCK, (q_blocks[1] + 1) * MOBA_BLOCK)
        n_kblk = q_blocks[1] + 1
        n_keys = n_kblk * MOBA_BLOCK
        sc, pr = sc_bufs[pair % 2], pr_bufs[pair % 2]
        q_pair = qs_ref[cols, :]

        def query_bias(j):
            halves = []
            for qb in q_blocks:
                q_cols = slice(qb * MOBA_BLOCK, (qb + 1) * MOBA_BLOCK)
                halves.append(bias_ref[j:j + 1, q_cols] if j < qb else zeros_row if j == qb else future_row)
            return jnp.concatenate(halves, axis=1)

        biases = [query_bias(j) for j in range(n_kblk)]
        col_max = []
        for j in range(n_kblk):
            rows = slice(j * MOBA_BLOCK, (j + 1) * MOBA_BLOCK)
            s = _dot_nt(ks_ref[rows, :], q_pair)
            if j in q_blocks:
                h = q_blocks.index(j)
                own = s[:, h * MOBA_BLOCK:(h + 1) * MOBA_BLOCK] + own_bias
                s = jnp.concatenate([own, s[:, MOBA_BLOCK:]] if h == 0 else [s[:, :MOBA_BLOCK], own], axis=1)
            sc[rows, :] = s
            col_max.append(jnp.max(s, axis=0, keepdims=True) + biases[j])
        m = functools.reduce(jnp.maximum, col_max)
        denom = jnp.zeros((1, 2 * MOBA_BLOCK), F32)
        for j in range(n_kblk):
            rows = slice(j * MOBA_BLOCK, (j + 1) * MOBA_BLOCK)
            p = jnp.exp2(sc[rows, :] - (m - biases[j]))
            denom = denom + jnp.sum(p, axis=0, keepdims=True)
            pr[rows, :] = p.astype(BF16)
        o_t = _dot(vt_ref[:, 0:n_keys], pr[0:n_keys, :]) * (1.0 / denom)
        o_ref[cols, :] = o_t.T.astype(BF16)


def _moba(proj, cos_t, sin_t):
    dh = MOBA_DH
    return pl.pallas_call(
        _moba_body,
        grid=(BATCH, MOBA_HEADS),
        in_specs=[
            pl.BlockSpec((SEQ, dh), lambda b, h: (b, COL_MQ // dh + h)),
            pl.BlockSpec((SEQ, dh), lambda b, h: (b, COL_MK // dh + h)),
            pl.BlockSpec((SEQ, dh), lambda b, h: (b, COL_MV // dh + h)),
            pl.BlockSpec((1, SEQ, dh), lambda b, h: (b, 0, 0)),
            pl.BlockSpec((1, SEQ, dh), lambda b, h: (b, 0, 0)),
        ],
        out_specs=pl.BlockSpec((SEQ, dh), lambda b, h: (b, h)),
        out_shape=jax.ShapeDtypeStruct((N_TOK, MOBA_DIM), BF16),
        scratch_shapes=[
            pltpu.VMEM((SEQ, dh), BF16),
            pltpu.VMEM((SEQ, dh), BF16),
            pltpu.VMEM((dh, SEQ), BF16),
            pltpu.VMEM((N_KBLK, SEQ), F32),
            pltpu.VMEM((SEQ, 2 * MOBA_BLOCK), F32),
            pltpu.VMEM((SEQ, 2 * MOBA_BLOCK), F32),
            pltpu.VMEM((SEQ, 2 * MOBA_BLOCK), BF16),
            pltpu.VMEM((SEQ, 2 * MOBA_BLOCK), BF16),
        ],
        compiler_params=_cparams("parallel", "parallel"),
        name="moba",
    )(proj, proj, proj, cos_t, sin_t)


def _layer_norm(z, g, b):
    mu = jnp.mean(z, axis=-1, keepdims=True)
    zc = z - mu
    var = jnp.mean(zc * zc, axis=-1, keepdims=True)
    return zc * lax.rsqrt(var + LN_EPS) * g + b


def _mix_body(gla_ref, moba_ref, ga_ref, gb_ref, x_ref, wgo_ref, wmo_ref, wo_ref, g_ref, b_ref,
              h_ref, hb_ref, hp_ref):
    for s in range(MIX_TM // MIX_SUB):
        rows = slice(s * MIX_SUB, (s + 1) * MIX_SUB)
        y_gla = _dot(gla_ref[rows, :], wgo_ref[...])
        y_moba = _dot(moba_ref[rows, :], wmo_ref[...])
        merged = (_sigmoid(ga_ref[rows, :].astype(F32)) * y_gla
                  + _sigmoid(gb_ref[rows, :].astype(F32)) * y_moba)
        mix = _dot(merged.astype(BF16), wo_ref[...])
        h = _layer_norm(ALPHA * x_ref[rows, :] + mix, g_ref[...], b_ref[...])
        h_ref[rows, :] = h
        hb_ref[rows, :] = h.astype(BF16)
        packed = _pack_bf16_pair(h)
        for c in range(PACKED_CHUNKS):
            hp_ref[pl.ds(s * MIX_SUB * PACKED_CHUNKS + c, MIX_SUB, stride=PACKED_CHUNKS), :] = (
                packed[:, c * LANES:(c + 1) * LANES])


def _mix(gla_out, moba_out, proj, x2d, w_gla_o, w_moba_o, w_out, ln_g, ln_b):
    d = D_MODEL
    row = lambda i: (i, 0)
    full = lambda i: (0, 0)
    return pl.pallas_call(
        _mix_body,
        grid=(N_TOK // MIX_TM,),
        in_specs=[
            pl.BlockSpec((MIX_TM, d), row),
            pl.BlockSpec((MIX_TM, d), row),
            pl.BlockSpec((MIX_TM, d), lambda i: (i, COL_GA // d)),
            pl.BlockSpec((MIX_TM, d), lambda i: (i, COL_GB // d)),
            pl.BlockSpec((MIX_TM, d), row),
            pl.BlockSpec((d, d), full),
            pl.BlockSpec((d, d), full),
            pl.BlockSpec((d, d), full),
            pl.BlockSpec((1, d), full),
            pl.BlockSpec((1, d), full),
        ],
        out_specs=[
            pl.BlockSpec((MIX_TM, d), row),
            pl.BlockSpec((MIX_TM, d), row),
            pl.BlockSpec((MIX_TM * PACKED_CHUNKS, LANES), row),
        ],
        out_shape=[
            jax.ShapeDtypeStruct((N_TOK, d), F32),
            jax.ShapeDtypeStruct((N_TOK, d), BF16),
            jax.ShapeDtypeStruct((N_TOK * PACKED_CHUNKS, LANES), U32),
        ],
        compiler_params=_cparams("parallel"),
        name="mix_ln1",
    )(gla_out, moba_out, proj, proj, x2d, w_gla_o, w_moba_o, w_out, ln_g, ln_b)


def _route_body(h_ref, wr_ref, br_ref, e_ref, w_ref, rk_ref, cnt_ref, carry_ref):
    tm = ROUTE_TM

    @pl.when(pl.program_id(0) == 0)
    def _():
        carry_ref[...] = jnp.zeros_like(carry_ref)

    scores = _sigmoid(_dot_nt(wr_ref[...], h_ref[...]))
    biased = scores + br_ref[...]
    row = lax.broadcasted_iota(I32, (N_EXPERTS, tm), 0).astype(F32)
    row_g = lax.broadcasted_iota(I32, (GROUP_SIZE, tm), 0).astype(F32)

    g_scores = []
    for g in range(N_GROUPS):
        grp = biased[g * GROUP_SIZE:(g + 1) * GROUP_SIZE]
        m1 = jnp.max(grp, axis=0, keepdims=True)
        first = jnp.min(jnp.where(grp == m1, row_g, float(GROUP_SIZE)), axis=0, keepdims=True)
        m2 = jnp.max(jnp.where(row_g == first, _NEG_INF, grp), axis=0, keepdims=True)
        g_scores.append(m1 + m2)
    g_score = jnp.concatenate(g_scores, axis=0)
    g_row = lax.broadcasted_iota(I32, (N_GROUPS, tm), 0)
    g_beaten = jnp.zeros((N_GROUPS, tm), I32)
    for g in range(N_GROUPS):
        r = g_score[g:g + 1, :]
        g_beaten += jnp.where((r > g_score) | ((r == g_score) & (g < g_row)), 1, 0)
    g_keep = g_beaten < TOPK_GROUPS
    masked = jnp.concatenate(
        [jnp.where(g_keep[g:g + 1, :], biased[g * GROUP_SIZE:(g + 1) * GROUP_SIZE], _NEG_INF)
         for g in range(N_GROUPS)], axis=0)

    onehot = jnp.zeros((N_EXPERTS, tm), F32)
    picks, pick_scores = [], []
    for _ in range(TOPK_EXPERTS):
        m = jnp.max(masked, axis=0, keepdims=True)
        idx = jnp.min(jnp.where(masked == m, row, float(N_EXPERTS)), axis=0, keepdims=True)
        hit = row == idx
        picks.append(idx)
        pick_scores.append(jnp.sum(jnp.where(hit, scores, 0.0), axis=0, keepdims=True))
        onehot = onehot + jnp.where(hit, 1.0, 0.0)
        masked = jnp.where(hit, _NEG_INF, masked)
    sel = jnp.concatenate(pick_scores, axis=0)
    e_ref[...] = jnp.concatenate(picks, axis=0).astype(I32)
    w_ref[...] = sel / jnp.sum(sel, axis=0, keepdims=True) * ROUTED_SCALE

    t_r = lax.broadcasted_iota(I32, (tm, tm), 0)
    t_c = lax.broadcasted_iota(I32, (tm, tm), 1)
    earlier = jnp.where(t_r < t_c, 1.0, 0.0).astype(BF16)
    seen = _dot(onehot.astype(BF16), earlier) + carry_ref[...]
    rk_ref[...] = jnp.concatenate(
        [jnp.sum(jnp.where(row == idx, seen, 0.0), axis=0, keepdims=True) for idx in picks], axis=0).astype(I32)
    carry_ref[...] += jnp.sum(onehot, axis=1, keepdims=True)
    cnt_ref[...] = carry_ref[...]


def _route(h_bf, w_router_t, b_router_col):
    k = TOPK_EXPERTS
    tok = lambda i: (0, i)
    return pl.pallas_call(
        _route_body,
        grid=(N_TOK // ROUTE_TM,),
        in_specs=[
            pl.BlockSpec((ROUTE_TM, D_MODEL), lambda i: (i, 0)),
            pl.BlockSpec((N_EXPERTS, D_MODEL), lambda i: (0, 0)),
            pl.BlockSpec((N_EXPERTS, 1), lambda i: (0, 0)),
        ],
        out_specs=[
            pl.BlockSpec((k, ROUTE_TM), tok),
            pl.BlockSpec((k, ROUTE_TM), tok),
            pl.BlockSpec((k, ROUTE_TM), tok),
            pl.BlockSpec((N_EXPERTS, 1), lambda i: (0, 0)),
        ],
        out_shape=[
            jax.ShapeDtypeStruct((k, N_TOK), I32),
            jax.ShapeDtypeStruct((k, N_TOK), F32),
            jax.ShapeDtypeStruct((k, N_TOK), I32),
            jax.ShapeDtypeStruct((N_EXPERTS, 1), F32),
        ],
        scratch_shapes=[pltpu.VMEM((N_EXPERTS, 1), F32)],
        compiler_params=_cparams("arbitrary"),
        name="route",
    )(h_bf, w_router_t, b_router_col)


def _dest_body(cnt_ref, e_ref, rk_ref, d_ref, te_ref, nu_ref, lt_ref, pad_ref):
    cnt = cnt_ref[...]
    tiles = jnp.floor((cnt + (ROW_TILE - 1)) * (1.0 / ROW_TILE))
    er = lax.broadcasted_iota(I32, (N_EXPERTS, N_EXPERTS), 0)
    ec = lax.broadcasted_iota(I32, (N_EXPERTS, N_EXPERTS), 1)
    before = jnp.where(ec < er, 1.0, 0.0).astype(BF16)
    tiles_b = jnp.broadcast_to(tiles, (N_EXPERTS, LANES)).astype(BF16)
    t_start = _dot(before, tiles_b)[:, 0:1]
    t_end = t_start + tiles
    p_start = t_start * float(ROW_TILE)
    lt_ref[...] = jnp.where(tiles > 0.0, (t_end - 1.0) * float(ROW_TILE), -1.0).astype(I32)
    pad_ref[...] = (p_start + cnt).astype(I32)

    row = lax.broadcasted_iota(I32, (N_EXPERTS, DEST_TM), 0)
    d_ref[...] = jnp.concatenate(
        [jnp.sum(jnp.where(row == e_ref[k:k + 1, :], p_start, 0.0), axis=0, keepdims=True)
         for k in range(TOPK_EXPERTS)], axis=0).astype(I32) + rk_ref[...]

    tile_id = lax.broadcasted_iota(I32, (N_EXPERTS, TILE_TABLE), 1).astype(F32)
    owner = jnp.sum(jnp.where(t_end <= tile_id, 1, 0), axis=0, keepdims=True)
    te_ref[...] = jnp.minimum(owner, N_EXPERTS - 1)
    nu_ref[...] = jnp.broadcast_to(t_end[N_EXPERTS - 1:N_EXPERTS, :], (1, LANES)).astype(I32)


def _dest(cnt, e_t, rk_t):
    k = TOPK_EXPERTS
    tok = lambda i: (0, i)
    return pl.pallas_call(
        _dest_body,
        grid=(N_TOK // DEST_TM,),
        in_specs=[
            pl.BlockSpec((N_EXPERTS, 1), lambda i: (0, 0)),
            pl.BlockSpec((k, DEST_TM), tok),
            pl.BlockSpec((k, DEST_TM), tok),
        ],
        out_specs=[
            pl.BlockSpec((k, DEST_TM), tok),
            pl.BlockSpec((1, TILE_TABLE), lambda i: (0, 0)),
            pl.BlockSpec((1, LANES), lambda i: (0, 0)),
            pl.BlockSpec((N_EXPERTS, 1), lambda i: (0, 0)),
            pl.BlockSpec((N_EXPERTS, 1), lambda i: (0, 0)),
        ],
        out_shape=[
            jax.ShapeDtypeStruct((k, N_TOK), I32),
            jax.ShapeDtypeStruct((1, TILE_TABLE), I32),
            jax.ShapeDtypeStruct((1, LANES), I32),
            jax.ShapeDtypeStruct((N_EXPERTS, 1), I32),
            jax.ShapeDtypeStruct((N_EXPERTS, 1), I32),
        ],
        compiler_params=_cparams("arbitrary"),
        name="dest",
    )(cnt, e_t, rk_t)


def _row_copy(src, dst, sem):
    return pltpu.make_async_copy(src, dst, sem)


def _packed_rows(ref, first, n):
    start = first * PACKED_CHUNKS
    if not isinstance(first, int):
        start = pl.multiple_of(start, PACKED_CHUNKS)
    return ref.at[pl.ds(start, n * PACKED_CHUNKS), :]


def _dispatch_body(lt_ref, pad_ref, nu_ref, d_ref, h_ref, xs_ref, zero_ref, sem, zero_sem):
    step = pl.program_id(0)
    n_steps = N_TOK // DISP_TM
    experts_per_step = N_EXPERTS // n_steps
    assert experts_per_step * n_steps == N_EXPERTS
    tail_per_step = -(-(N_ROW_TILES - N_ASSIGN // ROW_TILE) // n_steps)

    @pl.when(step == 0)
    def _():
        zero_ref[...] = jnp.zeros_like(zero_ref)

    def zero_fills(act):
        for j in range(experts_per_step):
            e = step * experts_per_step + j
            row = pad_ref[e]
            n_pad = jnp.where(lt_ref[e] >= 0, lt_ref[e] + ROW_TILE - row, 0)
            for bit in reversed(range(ROW_TILE.bit_length() - 1)):
                run = 1 << bit
                has_run = (n_pad & run) != 0

                @pl.when(has_run)
                def _(row=row, run=run):
                    act(pltpu.make_async_copy(_packed_rows(zero_ref, 0, run), _packed_rows(xs_ref, row, run),
                                              zero_sem))

                row = row + jnp.where(has_run, run, 0)
        for j in range(tail_per_step):
            tile = nu_ref[0] + step * tail_per_step + j

            @pl.when(tile < N_ROW_TILES)
            def _(tile=tile):
                act(pltpu.make_async_copy(zero_ref, _packed_rows(xs_ref, tile * ROW_TILE, ROW_TILE), zero_sem))

    zero_fills(lambda c: c.start())
    for t in range(DISP_TM):
        for k in range(TOPK_EXPERTS):
            _row_copy(_packed_rows(h_ref, t, 1), _packed_rows(xs_ref, d_ref[k, t], 1), sem).start(priority=k % 2)
    for k in range(TOPK_EXPERTS):
        pltpu.make_async_copy(h_ref, _packed_rows(xs_ref, 0, DISP_TM), sem).wait()
    zero_fills(lambda c: c.wait())


def _dispatch(last_tile_row, pad_row, n_used, dest_t, h1):
    grid_spec = pltpu.PrefetchScalarGridSpec(
        num_scalar_prefetch=3,
        grid=(N_TOK // DISP_TM,),
        in_specs=[
            pl.BlockSpec((TOPK_EXPERTS, DISP_TM), lambda i, lt, pad, nu: (0, i), memory_space=pltpu.SMEM),
            pl.BlockSpec((DISP_TM * PACKED_CHUNKS, LANES), lambda i, lt, pad, nu: (i, 0)),
        ],
        out_specs=pl.BlockSpec(memory_space=pl.ANY),
        scratch_shapes=[pltpu.VMEM((ROW_TILE * PACKED_CHUNKS, LANES), U32), pltpu.SemaphoreType.DMA,
                        pltpu.SemaphoreType.DMA],
    )
    return pl.pallas_call(
        _dispatch_body,
        grid_spec=grid_spec,
        out_shape=jax.ShapeDtypeStruct((N_ROWS * PACKED_CHUNKS, LANES), U32),
        compiler_params=_cparams("arbitrary"),
        name="dispatch",
    )(last_tile_row, pad_row, n_used, dest_t, h1)


def _row_destinations(dest_flat):
    info = plsc.get_sparse_core_info()
    n_cores, lanes = info.num_cores, info.num_lanes
    n_workers = n_cores * info.num_subcores
    rows_per_worker = -(-N_ROWS // (n_workers * ROW_TILE)) * ROW_TILE
    assert rows_per_worker % lanes == 0
    mesh = plsc.VectorSubcoreMesh(core_axis_name="c", subcore_axis_name="s")

    @functools.partial(
        pl.kernel, mesh=mesh, out_type=jax.ShapeDtypeStruct((n_workers * rows_per_worker,), I32),
        scratch_types=[pltpu.VMEM((SLOT_CHUNK,), I32), pltpu.VMEM((rows_per_worker,), I32)],
        compiler_params=pltpu.CompilerParams(needs_layout_passes=False), name="row_destinations")
    def invert(dest_hbm, out_hbm, dest_v, table_v):
        first_row = (lax.axis_index("s") * n_cores + lax.axis_index("c")) * rows_per_worker
        lane = lax.iota(I32, lanes)
        tile_shift = ROW_TILE.bit_length() - 1

        def spare(i, carry):
            row = first_row + i * lanes + lane
            buf = lax.rem(lax.shift_right_logical(row, tile_shift), EXPERT_BUFS)
            table_v[pl.ds(i * lanes, lanes)] = TOPK_EXPERTS * N_TOK + buf * ROW_TILE + (row & (ROW_TILE - 1))
            return carry

        lax.fori_loop(0, rows_per_worker // lanes, spare, 0)

        def chunk(c, carry):
            pltpu.sync_copy(dest_hbm.at[pl.ds(c * SLOT_CHUNK, SLOT_CHUNK)], dest_v)

            def vec(i, inner):
                local = dest_v[pl.ds(i * lanes, lanes)] - first_row
                mine = (local >= 0) & (local < rows_per_worker)
                plsc.store_scatter(table_v, [jnp.where(mine, local, 0)], lane + (c * SLOT_CHUNK + i * lanes),
                                   mask=mine)
                return inner

            lax.fori_loop(0, SLOT_CHUNK // lanes, vec, 0)
            return carry

        lax.fori_loop(0, N_ASSIGN // SLOT_CHUNK, chunk, 0)
        pltpu.sync_copy(table_v, out_hbm.at[pl.ds(first_row, rows_per_worker)])

    return invert(dest_flat)


def _experts_body(te_ref, nu_ref, lt_ref, slot_ref, xs_ref, wg_ref, wu_ref, wd_ref, yk_ref,
                  wg_f, wu_f, wd_f, wg_b, wu_b, wd_b, y_buf, n_loaded, sem, w_sem):
    i = pl.program_id(0)
    n_used = nu_ref[0]

    def weight_copies(e, s):
        return (pltpu.make_async_copy(wg_ref.at[e], wg_f.at[s], w_sem.at[s]),
                pltpu.make_async_copy(wu_ref.at[e], wu_f.at[s], w_sem.at[s]),
                pltpu.make_async_copy(wd_ref.at[e], wd_f.at[s], w_sem.at[s]))

    def wait_tile(b):
        pltpu.make_async_copy(y_buf.at[b], _packed_rows(yk_ref, 0, ROW_TILE), sem.at[b]).wait()

    def send_tile(b):
        for r in range(ROW_TILE):
            dst = slot_ref[0, 0, r]
            _row_copy(_packed_rows(y_buf.at[b], r, 1), _packed_rows(yk_ref, dst, 1),
                      sem.at[b]).start(priority=r % 2)

    @pl.when(i == 0)
    def _():
        y_buf[EXPERT_BUFS - 1] = jnp.zeros((ROW_TILE * PACKED_CHUNKS, LANES), U32)
        spare = [pltpu.make_async_copy(y_buf.at[EXPERT_BUFS - 1],
                                       _packed_rows(yk_ref, TOPK_EXPERTS * N_TOK + b * ROW_TILE, ROW_TILE),
                                       sem.at[b]) for b in range(EXPERT_BUFS - 1)]
        for c in spare:
            c.start()
        for c in spare:
            c.wait()
        n_loaded[0] = 0
        for c in weight_copies(te_ref[0], 0):
            c.start()

    @pl.when((i >= 2) & (i <= n_used))
    def _():
        wait_tile(lax.rem(i, EXPERT_BUFS))

    @pl.when(i < n_used)
    def _():
        e = te_ref[i]
        prev = te_ref[jnp.maximum(i - 1, 0)]

        @pl.when((i == 0) | (e != prev))
        def _():
            s = lax.rem(n_loaded[0], 2)
            for c in weight_copies(e, s):
                c.wait()
            wg_b[...] = wg_f[s].astype(BF16)
            wu_b[...] = wu_f[s].astype(BF16)
            wd_b[...] = wd_f[s].astype(BF16)
            n_loaded[0] = n_loaded[0] + 1
            nxt = lax.shift_right_logical(lt_ref[e], ROW_TILE.bit_length() - 1) + 1

            @pl.when(nxt < n_used)
            def _():
                for c in weight_copies(te_ref[nxt], 1 - s):
                    c.start()

        for phase in range(EXPERT_BUFS):
            @pl.when(lax.rem(i, EXPERT_BUFS) == phase)
            def _(phase=phase):
                send_tile((phase + EXPERT_BUFS - 1) % EXPERT_BUFS)
                packed = jnp.concatenate([xs_ref[pl.ds(c, ROW_TILE, stride=PACKED_CHUNKS), :]
                                          for c in range(PACKED_CHUNKS)], axis=1)
                x_lo, x_hi = _unpack_bf16_pair(packed)
                x = jnp.concatenate([x_lo.astype(BF16), x_hi.astype(BF16)], axis=1)
                g = _dot(x, wg_b[...])
                u = _dot(x, wu_b[...])
                h = (g * _sigmoid(g)) * u
                y = _pack_bf16_pair(_dot(h.astype(BF16), wd_b[...]))
                for c in range(PACKED_CHUNKS):
                    y_buf[phase, pl.ds(c, ROW_TILE, stride=PACKED_CHUNKS), :] = y[:, c * LANES:(c + 1) * LANES]

    @pl.when(i == n_used)
    def _():
        send_tile(lax.rem(i + EXPERT_BUFS - 1, EXPERT_BUFS))
        wait_tile(lax.rem(i + EXPERT_BUFS - 2, EXPERT_BUFS))
        wait_tile(lax.rem(i + EXPERT_BUFS - 1, EXPERT_BUFS))


def _experts(tile_expert, n_used, last_tile_row, row_dst, xs, w_gate, w_up, w_down):
    def tile(i, te, nu, lt):
        return (jnp.minimum(i, nu[0] - 1), 0)

    n_table_tiles = row_dst.shape[0] // ROW_TILE
    placeholder = n_table_tiles - 1
    assert placeholder * ROW_TILE >= N_ROWS and placeholder % EXPERT_BUFS == EXPERT_BUFS - 1

    def prev_tile(i, te, nu, lt):
        return (jnp.where(i == 0, placeholder, jnp.minimum(i, nu[0]) - 1), 0, 0)

    grid_spec = pltpu.PrefetchScalarGridSpec(
        num_scalar_prefetch=3,
        grid=(n_used[0] + 1,),
        in_specs=[
            pl.BlockSpec((1, 1, ROW_TILE), prev_tile, memory_space=pltpu.SMEM),
            pl.BlockSpec((ROW_TILE * PACKED_CHUNKS, LANES), tile),
            pl.BlockSpec(memory_space=pl.ANY),
            pl.BlockSpec(memory_space=pl.ANY),
            pl.BlockSpec(memory_space=pl.ANY),
        ],
        out_specs=pl.BlockSpec(memory_space=pl.ANY),
        scratch_shapes=[
            pltpu.VMEM((2, D_MODEL, D_EXPERT), F32),
            pltpu.VMEM((2, D_MODEL, D_EXPERT), F32),
            pltpu.VMEM((2, D_EXPERT, D_MODEL), F32),
            pltpu.VMEM((D_MODEL, D_EXPERT), BF16),
            pltpu.VMEM((D_MODEL, D_EXPERT), BF16),
            pltpu.VMEM((D_EXPERT, D_MODEL), BF16),
            pltpu.VMEM((EXPERT_BUFS, ROW_TILE * PACKED_CHUNKS, LANES), U32),
            pltpu.SMEM((1,), I32),
            pltpu.SemaphoreType.DMA((EXPERT_BUFS,)),
            pltpu.SemaphoreType.DMA((2,)),
        ],
    )
    return pl.pallas_call(
        _experts_body,
        grid_spec=grid_spec,
        out_shape=jax.ShapeDtypeStruct((YK_ROWS * PACKED_CHUNKS, LANES), U32),
        compiler_params=_cparams("arbitrary"),
        name="experts",
    )(tile_expert, n_used, last_tile_row, row_dst.reshape(n_table_tiles, 1, ROW_TILE), xs, w_gate, w_up, w_down)


def _final_body(w_ref, *refs):
    yk_refs = refs[:TOPK_EXPERTS]
    h_ref, hb_ref, p_ref, wsg_ref, wsu_ref, wsd_ref, wpl_ref, wpg_ref, g_ref, b_ref, o_ref = refs[TOPK_EXPERTS:]
    hb = hb_ref[...]
    sg = _dot(hb, wsg_ref[...])
    shared = _dot(((sg * _sigmoid(sg)) * _dot(hb, wsu_ref[...])).astype(BF16), wsd_ref[...])
    ple = _sigmoid(_dot(hb, wpg_ref[...])) * _dot(p_ref[...].astype(BF16), wpl_ref[...])

    def plane(k):
        return jnp.concatenate([yk_refs[k][pl.ds(c, FINAL_TM, stride=PACKED_CHUNKS), :]
                                for c in range(PACKED_CHUNKS)], axis=1)

    w_col = w_ref[...].T
    y_lo, y_hi = _unpack_bf16_pair(plane(0))
    r_lo, r_hi = y_lo * w_col[:, 0:1], y_hi * w_col[:, 0:1]
    for k in range(1, TOPK_EXPERTS):
        y_lo, y_hi = _unpack_bf16_pair(plane(k))
        r_lo, r_hi = r_lo + y_lo * w_col[:, k:k + 1], r_hi + y_hi * w_col[:, k:k + 1]
    routed = jnp.concatenate([r_lo, r_hi], axis=1)
    o_ref[...] = _layer_norm(ALPHA * h_ref[...] + (routed + shared) + ple, g_ref[...], b_ref[...])


def _final(w_t, yk, h1, h_bf, p2d, w_s_gate, w_s_up, w_s_down, w_ple, w_ple_gate, ln_g, ln_b):
    d = D_MODEL
    n_steps = N_TOK // FINAL_TM
    full = lambda i: (0, 0)
    return pl.pallas_call(
        _final_body,
        grid=(n_steps,),
        in_specs=[
            pl.BlockSpec((TOPK_EXPERTS, FINAL_TM), lambda i: (0, i)),
            *[pl.BlockSpec((FINAL_TM * PACKED_CHUNKS, LANES), lambda i, k=k: (k * n_steps + i, 0))
              for k in range(TOPK_EXPERTS)],
            pl.BlockSpec((FINAL_TM, d), lambda i: (i, 0)),
            pl.BlockSpec((FINAL_TM, d), lambda i: (i, 0)),
            pl.BlockSpec((FINAL_TM, PLE_DIM), lambda i: (i, 0)),
            pl.BlockSpec((d, D_SHARED), full),
            pl.BlockSpec((d, D_SHARED), full),
            pl.BlockSpec((D_SHARED, d), full),
            pl.BlockSpec((PLE_DIM, d), full),
            pl.BlockSpec((d, d), full),
            pl.BlockSpec((1, d), full),
            pl.BlockSpec((1, d), full),
        ],
        out_specs=pl.BlockSpec((FINAL_TM, d), lambda i: (i, 0)),
        out_shape=jax.ShapeDtypeStruct((N_TOK, d), F32),
        compiler_params=_cparams("parallel"),
        name="combine_ln2",
    )(w_t, *([yk] * TOPK_EXPERTS), h1, h_bf, p2d, w_s_gate, w_s_up, w_s_down, w_ple, w_ple_gate, ln_g, ln_b)


def _rope_tables(positions):
    half = ROT_DIM // 2
    inv = ROPE_THETA ** (-jnp.arange(0, ROT_DIM, 2, dtype=F32) / ROT_DIM)
    per_row = LANES // half
    pos = jnp.repeat(positions.reshape(-1, per_row), half, axis=1).astype(F32)
    ang = pos * jnp.tile(inv, per_row)[None, :]
    cos, sin = lax.optimization_barrier((jnp.cos(ang), jnp.sin(ang)))
    cos = cos.reshape(positions.shape + (half,))
    sin = sin.reshape(positions.shape + (half,))
    rest = MOBA_DH - ROT_DIM
    ones = jnp.ones(cos.shape[:-1] + (rest,), F32)
    zeros = jnp.zeros(cos.shape[:-1] + (rest,), F32)
    return (jnp.concatenate([cos, cos, ones], axis=-1), jnp.concatenate([-sin, sin, zeros], axis=-1))


def _layer(h2d, p2d, cos_t, sin_t, w_cat, w_gk2, b_gk, norm_g, w_gla_o, w_moba_o, w_out, ln1_g, ln1_b,
           w_router, b_router, w_e_gate, w_e_up, w_e_down, w_s_gate, w_s_up, w_s_down, w_ple, w_ple_gate,
           ln2_g, ln2_b):
    w_gk2_pad = jnp.concatenate([w_gk2, jnp.zeros((LANES - GLA_RANK, GLA_KDIM), w_gk2.dtype)], axis=0)

    proj = _in_proj(h2d, w_cat)
    gla_out = _gla(proj, w_gk2_pad, b_gk[None, :], norm_g[None, :])
    moba_out = _moba(proj, cos_t, sin_t)
    h1, h_bf, h_packed = _mix(gla_out, moba_out, proj, h2d, w_gla_o.astype(BF16), w_moba_o.astype(BF16),
                              w_out.astype(BF16), ln1_g[None, :], ln1_b[None, :])
    e_t, w_t, rk_t, cnt = _route(h_bf, w_router.T.astype(BF16), b_router[:, None])
    dest_t, tile_expert, n_used, last_tile_row, pad_row = _dest(cnt, e_t, rk_t)
    last_tile_row = last_tile_row.reshape(N_EXPERTS)
    n_used = n_used[0, 0:1]
    xs = _dispatch(last_tile_row, pad_row.reshape(N_EXPERTS), n_used, dest_t, h_packed)
    row_dst = _row_destinations(dest_t.reshape(N_ASSIGN))
    yk = _experts(tile_expert.reshape(TILE_TABLE), n_used, last_tile_row, row_dst, xs,
                  w_e_gate, w_e_up, w_e_down)
    return _final(w_t, yk, h1, h_bf, p2d, w_s_gate.astype(BF16), w_s_up.astype(BF16),
                  w_s_down.astype(BF16), w_ple.astype(BF16), w_ple_gate.astype(BF16),
                  ln2_g[None, :], ln2_b[None, :])


def kernel(x, p, positions, w_in, w_gla_gk2, b_gla_gk, gla_norm_g, w_gla_o, w_moba_o, w_out, ln1_g, ln1_b,
           w_router, b_router, w_e_gate, w_e_up, w_e_down, w_s_gate, w_s_up, w_s_down, w_ple, w_ple_gate,
           ln2_g, ln2_b):
    cos_t, sin_t = _rope_tables(positions)
    h = x.reshape(N_TOK, D_MODEL)
    for i in range(DEPTH):
        h = _layer(h, p[i].reshape(N_TOK, PLE_DIM), cos_t, sin_t, _repack_w_in(w_in, i), w_gla_gk2[i], b_gla_gk[i],
                   gla_norm_g[i], w_gla_o[i], w_moba_o[i], w_out[i], ln1_g[i], ln1_b[i], w_router[i],
                   b_router[i], w_e_gate[i], w_e_up[i], w_e_down[i], w_s_gate[i], w_s_up[i], w_s_down[i],
                   w_ple[i], w_ple_gate[i], ln2_g[i], ln2_b[i])
    return h.reshape(BATCH, SEQ, D_MODEL)
```

```python
import functools

import jax
import jax.numpy as jnp
from jax import lax
from jax.experimental import pallas as pl
from jax.experimental.pallas import tpu as pltpu
from jax.experimental.pallas import tpu_sc as plsc

F32 = jnp.float32
BF16 = jnp.bfloat16
I32 = jnp.int32
U32 = jnp.uint32

LANES = 128
SUBLANES = 8
VMEM_LIMIT_BYTES = 48 * 1024 * 1024

D_MODEL = 1024
BATCH = 8
SEQ = 2048
N_TOK = BATCH * SEQ
GLA_HEADS = 4
GLA_DK = 128
GLA_DV = 256
GLA_RANK = 16
GLA_NORMALIZER = 16.0
GLA_CHUNK = 64
GLA_GROUP = 2 * GLA_CHUNK
GLA_UNROLL = 8
MOBA_HEADS = 8
MOBA_DH = 128
MOBA_BLOCK = 256
MOBA_TOPK = 3
ROT_DIM = 32
ROPE_THETA = 500000.0
N_EXPERTS = 256
TOPK_EXPERTS = 8
N_GROUPS = 8
GROUP_SIZE = N_EXPERTS // N_GROUPS
TOPK_GROUPS = 4
D_EXPERT = 256
D_SHARED = 256
ROUTED_SCALE = 2.5
PLE_DIM = 256
LN_EPS = 1e-5
DEPTH = 1
ALPHA = (2.0 * DEPTH) ** 0.25
GLA_KDIM = GLA_HEADS * GLA_DK
GLA_VDIM = GLA_HEADS * GLA_DV
MOBA_DIM = MOBA_HEADS * MOBA_DH
N_KBLK = SEQ // MOBA_BLOCK

COL_GQ = 0
COL_GK = COL_GQ + GLA_KDIM
COL_GV = COL_GK + GLA_KDIM
COL_GR = COL_GV + GLA_VDIM
COL_MQ = COL_GR + GLA_VDIM
COL_MK = COL_MQ + MOBA_DIM
COL_MV = COL_MK + MOBA_DIM
COL_GA = COL_MV + MOBA_DIM
COL_GB = COL_GA + D_MODEL
COL_LOW = COL_GB + D_MODEL
PROJ_COLS = COL_LOW + LANES
LOW_SRC = 2 * GLA_KDIM + 2 * GLA_VDIM

ROW_TILE = 256
PACKED_COLS = D_MODEL // 2
PACKED_CHUNKS = PACKED_COLS // LANES
EXPERT_BUFS = 3
YK_ROWS = TOPK_EXPERTS * N_TOK + EXPERT_BUFS * ROW_TILE
SLOT_CHUNK = 4096
assert EXPERT_BUFS * ROW_TILE <= N_TOK
N_ASSIGN = N_TOK * TOPK_EXPERTS
N_ROW_TILES = (N_ASSIGN + N_EXPERTS * (ROW_TILE - 1) + ROW_TILE - 1) // ROW_TILE
N_ROWS = N_ROW_TILES * ROW_TILE
TILE_TABLE = -(-N_ROW_TILES // LANES) * LANES

PROJ_TM = 1024
REPACK_TN = 512
PROJ_TN = PROJ_COLS // 5
MIX_TM = 512
MIX_SUB = 256
ROUTE_TM = 512
DEST_TM = 2048
DISP_TM = 512
FINAL_TM = 512

_NEG_INF = float("-inf")
LOG2_E = 1.4426950408889634


def _cparams(*sem):
    return pltpu.CompilerParams(dimension_semantics=sem, vmem_limit_bytes=VMEM_LIMIT_BYTES)


def _dot(a, b):
    return jnp.dot(a, b, preferred_element_type=F32)


def _dot_nt(a, b):
    return lax.dot_general(a, b, (((1,), (1,)), ((), ())), preferred_element_type=F32)


def _dot_tn(a, b):
    return lax.dot_general(a, b, (((0,), (0,)), ((), ())), preferred_element_type=F32)


def _split_bf16(x):
    hi = x.astype(BF16)
    lo = (x - hi.astype(F32)).astype(BF16)
    return hi, lo


def _sigmoid(x):
    return 1.0 / (1.0 + jnp.exp(-x))


def _pack_bf16_pair(x):
    lo = lax.bitcast_convert_type(x[:, :PACKED_COLS].astype(BF16).astype(F32), U32)
    hi = lax.bitcast_convert_type(x[:, PACKED_COLS:].astype(BF16).astype(F32), U32)
    return hi | lax.shift_right_logical(lo, jnp.uint32(16))


def _unpack_bf16_pair(w):
    lo = lax.bitcast_convert_type(lax.shift_left(w, jnp.uint32(16)), F32)
    hi = lax.bitcast_convert_type(w & jnp.uint32(0xFFFF0000), F32)
    return lo, hi


def _repack_body(a_ref, b_ref, o_ref):
    t = pl.program_id(0)
    a = a_ref[...].astype(BF16)

    @pl.when(t < LOW_SRC // REPACK_TN)
    def _():
        o_ref[...] = a

    @pl.when((t >= LOW_SRC // REPACK_TN) & (t < COL_LOW // REPACK_TN))
    def _():
        o_ref[...] = jnp.concatenate([a[GLA_RANK:], b_ref[...].astype(BF16)], axis=0)

    @pl.when(t == COL_LOW // REPACK_TN)
    def _():
        o_ref[...] = jnp.concatenate([a[:GLA_RANK], jnp.zeros((REPACK_TN - GLA_RANK, D_MODEL), BF16)], axis=0)


def _repack_w_in(w_in, layer):
    assert LOW_SRC % REPACK_TN == 0 and COL_LOW % REPACK_TN == 0 and REPACK_TN % GLA_RANK == 0
    w_t = jnp.swapaxes(w_in, 1, 2)
    low_tile = LOW_SRC // REPACK_TN
    last_out = COL_LOW // REPACK_TN
    ranks_per_tile = REPACK_TN // GLA_RANK
    return pl.pallas_call(
        _repack_body,
        grid=(last_out + 1,),
        in_specs=[
            pl.BlockSpec((None, REPACK_TN, D_MODEL), lambda t: (layer, jnp.where(t == last_out, low_tile, t), 0)),
            pl.BlockSpec((None, GLA_RANK, D_MODEL), lambda t: (layer, jnp.minimum(t + 1, last_out) * ranks_per_tile, 0)),
        ],
        out_specs=pl.BlockSpec((REPACK_TN, D_MODEL), lambda t: (t, 0)),
        out_shape=jax.ShapeDtypeStruct((PROJ_COLS, D_MODEL), BF16),
        compiler_params=_cparams("parallel"),
        name="repack_w_in",
    )(w_t, w_t)


def _proj_body(x_ref, w_ref, o_ref):
    o_ref[...] = _dot_nt(x_ref[...].astype(BF16), w_ref[...]).astype(BF16)


def _in_proj(x2d, w_cat):
    return pl.pallas_call(
        _proj_body,
        grid=(PROJ_COLS // PROJ_TN, N_TOK // PROJ_TM),
        in_specs=[
            pl.BlockSpec((PROJ_TM, D_MODEL), lambda j, i: (i, 0)),
            pl.BlockSpec((PROJ_TN, D_MODEL), lambda j, i: (j, 0)),
        ],
        out_specs=pl.BlockSpec((PROJ_TM, PROJ_TN), lambda j, i: (i, j)),
        out_shape=jax.ShapeDtypeStruct((N_TOK, PROJ_COLS), BF16),
        compiler_params=_cparams("parallel", "parallel"),
        name="in_proj",
    )(x2d, w_cat)


def _gla_body(q_ref, k_ref, v_ref, r_ref, low_ref, wg_ref, bg_ref, ng_ref, o_ref, st_ref, gk_ref):
    w_hi, w_lo = _split_bf16(wg_ref[...])
    low = low_ref[...]
    lin = _dot(low, w_hi) + _dot(low, w_lo) + bg_ref[...]
    gk_ref[...] = (jnp.minimum(lin, 0.0) - jnp.log1p(jnp.exp(-jnp.abs(lin)))) * (1.0 / GLA_NORMALIZER)
    st_ref[...] = jnp.zeros_like(st_ref)

    ri = lax.broadcasted_iota(I32, (GLA_GROUP, GLA_GROUP), 0)
    ci = lax.broadcasted_iota(I32, (GLA_GROUP, GLA_GROUP), 1)
    same_chunk = lax.shift_right_logical(ri, GLA_CHUNK.bit_length() - 1) == lax.shift_right_logical(
        ci, GLA_CHUNK.bit_length() - 1)
    causal = same_chunk & (ri >= ci)
    sums = jnp.concatenate([jnp.where(causal, 1.0, 0.0), jnp.where(same_chunk, 1.0, 0.0)], axis=0).astype(BF16)
    gain = ng_ref[...]

    def group(c, carry):
        rows = pl.ds(pl.multiple_of(c * GLA_GROUP, GLA_GROUP), GLA_GROUP)
        g_hi, g_lo = _split_bf16(gk_ref[rows, :])
        bb = _dot(sums, g_hi) + _dot(sums, g_lo)
        b = bb[0:GLA_GROUP]
        b_end = bb[GLA_GROUP:2 * GLA_GROUP]
        q = q_ref[rows, :].astype(F32) * (GLA_DK ** -0.5)
        k = k_ref[rows, :].astype(F32)
        v = v_ref[rows, :]
        q_e = (q * jnp.exp(b)).astype(BF16)
        k_e = (k * jnp.exp(-b)).astype(BF16)
        k_d = (k * jnp.exp(b_end - b)).astype(BF16)
        att = jnp.where(causal, _dot_nt(q_e, k_e), 0.0)
        o = _dot(att.astype(BF16), v)
        st = st_ref[...]
        inter = []
        for j in range(GLA_GROUP // GLA_CHUNK):
            cr = slice(j * GLA_CHUNK, (j + 1) * GLA_CHUNK)
            inter.append(_dot_nt(q_e[cr], st.astype(BF16)))
            st = st * jnp.exp(b_end[j * GLA_CHUNK:j * GLA_CHUNK + 1, :]) + _dot_tn(v[cr], k_d[cr])
        st_ref[...] = st
        o = o + jnp.concatenate(inter, axis=0)
        o = o * lax.rsqrt(jnp.mean(o * o, axis=-1, keepdims=True) + LN_EPS) * gain
        r = r_ref[rows, :].astype(F32)
        o_ref[rows, :] = (o * (r * _sigmoid(r))).astype(BF16)
        return carry

    lax.fori_loop(0, SEQ // GLA_GROUP, group, 0, unroll=GLA_UNROLL)


def _gla(proj, w_gk2_pad, b_gk, norm_g):
    kb, vb = GLA_DK, GLA_DV
    return pl.pallas_call(
        _gla_body,
        grid=(BATCH, GLA_HEADS),
        in_specs=[
            pl.BlockSpec((SEQ, kb), lambda b, h: (b, COL_GQ // kb + h)),
            pl.BlockSpec((SEQ, kb), lambda b, h: (b, COL_GK // kb + h)),
            pl.BlockSpec((SEQ, vb), lambda b, h: (b, COL_GV // vb + h)),
            pl.BlockSpec((SEQ, vb), lambda b, h: (b, COL_GR // vb + h)),
            pl.BlockSpec((SEQ, LANES), lambda b, h: (b, COL_LOW // LANES)),
            pl.BlockSpec((LANES, kb), lambda b, h: (0, h)),
            pl.BlockSpec((1, kb), lambda b, h: (0, h)),
            pl.BlockSpec((1, vb), lambda b, h: (0, 0)),
        ],
        out_specs=pl.BlockSpec((SEQ, vb), lambda b, h: (b, h)),
        out_shape=jax.ShapeDtypeStruct((N_TOK, GLA_VDIM), BF16),
        scratch_shapes=[pltpu.VMEM((vb, kb), F32), pltpu.VMEM((SEQ, kb), F32)],
        compiler_params=_cparams("parallel", "parallel"),
        name="gla",
    )(proj, proj, proj, proj, proj, w_gk2_pad, b_gk, norm_g)


def _moba_body(q_ref, k_ref, v_ref, c_ref, s_ref, o_ref, qs_ref, ks_ref, vt_ref, bias_ref,
               sc_a_ref, sc_b_ref, pr_a_ref, pr_b_ref):
    cos_t = c_ref[0]
    sin_t = s_ref[0]
    lane = lax.broadcasted_iota(I32, (SEQ, MOBA_DH), 1)
    half = ROT_DIM // 2

    def rope(x):
        partner = jnp.where(lane < half, pltpu.roll(x, MOBA_DH - half, 1), pltpu.roll(x, half, 1))
        return x * cos_t + partner * sin_t

    q = rope(q_ref[...].astype(F32))
    k = rope(k_ref[...].astype(F32))
    q_hi, q_lo = _split_bf16(q)
    qs_ref[...] = (q * (MOBA_DH ** -0.5 * LOG2_E)).astype(BF16)
    ks_ref[...] = k.astype(BF16)
    vt_ref[...] = v_ref[...].astype(F32).T.astype(BF16)

    k_mean = jnp.concatenate(
        [jnp.mean(k[j * MOBA_BLOCK:(j + 1) * MOBA_BLOCK], axis=0, keepdims=True) for j in range(N_KBLK)], axis=0)
    m_hi, m_lo = _split_bf16(k_mean)
    s_blk = _dot_nt(m_hi, q_hi) + _dot_nt(m_hi, q_lo) + _dot_nt(m_lo, q_hi)
    blk = lax.broadcasted_iota(I32, (N_KBLK, SEQ), 0)
    q_blk = lax.shift_right_logical(lax.broadcasted_iota(I32, (N_KBLK, SEQ), 1), MOBA_BLOCK.bit_length() - 1)
    past = blk < q_blk
    s_blk = jnp.where(past, s_blk, _NEG_INF)
    beaten = jnp.zeros((N_KBLK, SEQ), I32)
    for j in range(N_KBLK):
        row = s_blk[j:j + 1, :]
        beaten += jnp.where((row > s_blk) | ((row == s_blk) & (j < blk)), 1, 0)
    bias_ref[...] = jnp.where(past & (beaten < MOBA_TOPK), 0.0, _NEG_INF)

    kr = lax.broadcasted_iota(I32, (MOBA_BLOCK, MOBA_BLOCK), 0)
    qc = lax.broadcasted_iota(I32, (MOBA_BLOCK, MOBA_BLOCK), 1)
    own_bias = jnp.where(kr <= qc, 0.0, _NEG_INF)

    sc_bufs = (sc_a_ref, sc_b_ref)
    pr_bufs = (pr_a_ref, pr_b_ref)
    zeros_row = jnp.zeros((1, MOBA_BLOCK), F32)
    future_row = jnp.full((1, MOBA_BLOCK), _NEG_INF, F32)
    for pair in range(N_KBLK // 2):
        q_blocks = (2 * pair, 2 * pair + 1)
        cols = slice(q_blocks[0] * MOBA_BLOCK, (q_blocks[1] + 1) * MOBA_BLOCK)
        n_kblk = q_blocks[1] + 1
        n_keys = n_kblk * MOBA_BLOCK
        sc, pr = sc_bufs[pair % 2], pr_bufs[pair % 2]
        q_pair = qs_ref[cols, :]

        def query_bias(j):
            halves = []
            for qb in q_blocks:
                q_cols = slice(qb * MOBA_BLOCK, (qb + 1) * MOBA_BLOCK)
                halves.append(bias_ref[j:j + 1, q_cols] if j < qb else zeros_row if j == qb else future_row)
            return jnp.concatenate(halves, axis=1)

        biases = [query_bias(j) for j in range(n_kblk)]
        col_max = []
        for j in range(n_kblk):
            rows = slice(j * MOBA_BLOCK, (j + 1) * MOBA_BLOCK)
            s = _dot_nt(ks_ref[rows, :], q_pair)
            if j in q_blocks:
                h = q_blocks.index(j)
                own = s[:, h * MOBA_BLOCK:(h + 1) * MOBA_BLOCK] + own_bias
                s = jnp.concatenate([own, s[:, MOBA_BLOCK:]] if h == 0 else [s[:, :MOBA_BLOCK], own], axis=1)
            sc[rows, :] = s
            col_max.append(jnp.max(s, axis=0, keepdims=True) + biases[j])
        m = functools.reduce(jnp.maximum, col_max)
        denom = jnp.zeros((1, 2 * MOBA_BLOCK), F32)
        for j in range(n_kblk):
            rows = slice(j * MOBA_BLOCK, (j + 1) * MOBA_BLOCK)
            p = jnp.exp2(sc[rows, :] - (m - biases[j]))
            denom = denom + jnp.sum(p, axis=0, keepdims=True)
            pr[rows, :] = p.astype(BF16)
        o_t = _dot(vt_ref[:, 0:n_keys], pr[0:n_keys, :]) * (1.0 / denom)
        o_ref[cols, :] = o_t.T.astype(BF16)


def _moba(proj, cos_t, sin_t):
    dh = MOBA_DH
    return pl.pallas_call(
        _moba_body,
        grid=(BATCH, MOBA_HEADS),
        in_specs=[
            pl.BlockSpec((SEQ, dh), lambda b, h: (b, COL_MQ // dh + h)),
            pl.BlockSpec((SEQ, dh), lambda b, h: (b, COL_MK // dh + h)),
            pl.BlockSpec((SEQ, dh), lambda b, h: (b, COL_MV // dh + h)),
            pl.BlockSpec((1, SEQ, dh), lambda b, h: (b, 0, 0)),
            pl.BlockSpec((1, SEQ, dh), lambda b, h: (b, 0, 0)),
        ],
        out_specs=pl.BlockSpec((SEQ, dh), lambda b, h: (b, h)),
        out_shape=jax.ShapeDtypeStruct((N_TOK, MOBA_DIM), BF16),
        scratch_shapes=[
            pltpu.VMEM((SEQ, dh), BF16),
            pltpu.VMEM((SEQ, dh), BF16),
            pltpu.VMEM((dh, SEQ), BF16),
            pltpu.VMEM((N_KBLK, SEQ), F32),
            pltpu.VMEM((SEQ, 2 * MOBA_BLOCK), F32),
            pltpu.VMEM((SEQ, 2 * MOBA_BLOCK), F32),
            pltpu.VMEM((SEQ, 2 * MOBA_BLOCK), BF16),
            pltpu.VMEM((SEQ, 2 * MOBA_BLOCK), BF16),
        ],
        compiler_params=_cparams("parallel", "parallel"),
        name="moba",
    )(proj, proj, proj, cos_t, sin_t)


def _layer_norm(z, g, b):
    mu = jnp.mean(z, axis=-1, keepdims=True)
    zc = z - mu
    var = jnp.mean(zc * zc, axis=-1, keepdims=True)
    return zc * lax.rsqrt(var + LN_EPS) * g + b


def _mix_body(gla_ref, moba_ref, ga_ref, gb_ref, x_ref, wgo_ref, wmo_ref, wo_ref, g_ref, b_ref,
              h_ref, hb_ref, hp_ref):
    for s in range(MIX_TM // MIX_SUB):
        rows = slice(s * MIX_SUB, (s + 1) * MIX_SUB)
        y_gla = _dot(gla_ref[rows, :], wgo_ref[...])
        y_moba = _dot(moba_ref[rows, :], wmo_ref[...])
        merged = (_sigmoid(ga_ref[rows, :].astype(F32)) * y_gla
                  + _sigmoid(gb_ref[rows, :].astype(F32)) * y_moba)
        mix = _dot(merged.astype(BF16), wo_ref[...])
        h = _layer_norm(ALPHA * x_ref[rows, :] + mix, g_ref[...], b_ref[...])
        h_ref[rows, :] = h
        hb_ref[rows, :] = h.astype(BF16)
        packed = _pack_bf16_pair(h)
        for c in range(PACKED_CHUNKS):
            hp_ref[pl.ds(s * MIX_SUB * PACKED_CHUNKS + c, MIX_SUB, stride=PACKED_CHUNKS), :] = (
                packed[:, c * LANES:(c + 1) * LANES])


def _mix(gla_out, moba_out, proj, x2d, w_gla_o, w_moba_o, w_out, ln_g, ln_b):
    d = D_MODEL
    row = lambda i: (i, 0)
    full = lambda i: (0, 0)
    return pl.pallas_call(
        _mix_body,
        grid=(N_TOK // MIX_TM,),
        in_specs=[
            pl.BlockSpec((MIX_TM, d), row),
            pl.BlockSpec((MIX_TM, d), row),
            pl.BlockSpec((MIX_TM, d), lambda i: (i, COL_GA // d)),
            pl.BlockSpec((MIX_TM, d), lambda i: (i, COL_GB // d)),
            pl.BlockSpec((MIX_TM, d), row),
            pl.BlockSpec((d, d), full),
            pl.BlockSpec((d, d), full),
            pl.BlockSpec((d, d), full),
            pl.BlockSpec((1, d), full),
            pl.BlockSpec((1, d), full),
        ],
        out_specs=[
            pl.BlockSpec((MIX_TM, d), row),
            pl.BlockSpec((MIX_TM, d), row),
            pl.BlockSpec((MIX_TM * PACKED_CHUNKS, LANES), row),
        ],
        out_shape=[
            jax.ShapeDtypeStruct((N_TOK, d), F32),
            jax.ShapeDtypeStruct((N_TOK, d), BF16),
            jax.ShapeDtypeStruct((N_TOK * PACKED_CHUNKS, LANES), U32),
        ],
        compiler_params=_cparams("parallel"),
        name="mix_ln1",
    )(gla_out, moba_out, proj, proj, x2d, w_gla_o, w_moba_o, w_out, ln_g, ln_b)


def _route_body(h_ref, wr_ref, br_ref, e_ref, w_ref, rk_ref, cnt_ref, carry_ref):
    tm = ROUTE_TM

    @pl.when(pl.program_id(0) == 0)
    def _():
        carry_ref[...] = jnp.zeros_like(carry_ref)

    scores = _sigmoid(_dot_nt(wr_ref[...], h_ref[...]))
    biased = scores + br_ref[...]
    row = lax.broadcasted_iota(I32, (N_EXPERTS, tm), 0).astype(F32)
    row_g = lax.broadcasted_iota(I32, (GROUP_SIZE, tm), 0).astype(F32)

    g_scores = []
    for g in range(N_GROUPS):
        grp = biased[g * GROUP_SIZE:(g + 1) * GROUP_SIZE]
        m1 = jnp.max(grp, axis=0, keepdims=True)
        first = jnp.min(jnp.where(grp == m1, row_g, float(GROUP_SIZE)), axis=0, keepdims=True)
        m2 = jnp.max(jnp.where(row_g == first, _NEG_INF, grp), axis=0, keepdims=True)
        g_scores.append(m1 + m2)
    g_score = jnp.concatenate(g_scores, axis=0)
    g_row = lax.broadcasted_iota(I32, (N_GROUPS, tm), 0)
    g_beaten = jnp.zeros((N_GROUPS, tm), I32)
    for g in range(N_GROUPS):
        r = g_score[g:g + 1, :]
        g_beaten += jnp.where((r > g_score) | ((r == g_score) & (g < g_row)), 1, 0)
    g_keep = g_beaten < TOPK_GROUPS
    masked = jnp.concatenate(
        [jnp.where(g_keep[g:g + 1, :], biased[g * GROUP_SIZE:(g + 1) * GROUP_SIZE], _NEG_INF)
         for g in range(N_GROUPS)], axis=0)

    onehot = jnp.zeros((N_EXPERTS, tm), F32)
    picks, pick_scores = [], []
    for _ in range(TOPK_EXPERTS):
        m = jnp.max(masked, axis=0, keepdims=True)
        idx = jnp.min(jnp.where(masked == m, row, float(N_EXPERTS)), axis=0, keepdims=True)
        hit = row == idx
        picks.append(idx)
        pick_scores.append(jnp.sum(jnp.where(hit, scores, 0.0), axis=0, keepdims=True))
        onehot = onehot + jnp.where(hit, 1.0, 0.0)
        masked = jnp.where(hit, _NEG_INF, masked)
    sel = jnp.concatenate(pick_scores, axis=0)
    e_ref[...] = jnp.concatenate(picks, axis=0).astype(I32)
    w_ref[...] = sel / jnp.sum(sel, axis=0, keepdims=True) * ROUTED_SCALE

    t_r = lax.broadcasted_iota(I32, (tm, tm), 0)
    t_c = lax.broadcasted_iota(I32, (tm, tm), 1)
    earlier = jnp.where(t_r < t_c, 1.0, 0.0).astype(BF16)
    seen = _dot(onehot.astype(BF16), earlier) + carry_ref[...]
    rk_ref[...] = jnp.concatenate(
        [jnp.sum(jnp.where(row == idx, seen, 0.0), axis=0, keepdims=True) for idx in picks], axis=0).astype(I32)
    carry_ref[...] += jnp.sum(onehot, axis=1, keepdims=True)
    cnt_ref[...] = carry_ref[...]


def _route(h_bf, w_router_t, b_router_col):
    k = TOPK_EXPERTS
    tok = lambda i: (0, i)
    return pl.pallas_call(
        _route_body,
        grid=(N_TOK // ROUTE_TM,),
        in_specs=[
            pl.BlockSpec((ROUTE_TM, D_MODEL), lambda i: (i, 0)),
            pl.BlockSpec((N_EXPERTS, D_MODEL), lambda i: (0, 0)),
            pl.BlockSpec((N_EXPERTS, 1), lambda i: (0, 0)),
        ],
        out_specs=[
            pl.BlockSpec((k, ROUTE_TM), tok),
            pl.BlockSpec((k, ROUTE_TM), tok),
            pl.BlockSpec((k, ROUTE_TM), tok),
            pl.BlockSpec((N_EXPERTS, 1), lambda i: (0, 0)),
        ],
        out_shape=[
            jax.ShapeDtypeStruct((k, N_TOK), I32),
            jax.ShapeDtypeStruct((k, N_TOK), F32),
            jax.ShapeDtypeStruct((k, N_TOK), I32),
            jax.ShapeDtypeStruct((N_EXPERTS, 1), F32),
        ],
        scratch_shapes=[pltpu.VMEM((N_EXPERTS, 1), F32)],
        compiler_params=_cparams("arbitrary"),
        name="route",
    )(h_bf, w_router_t, b_router_col)


def _dest_body(cnt_ref, e_ref, rk_ref, d_ref, te_ref, nu_ref, lt_ref, pad_ref):
    cnt = cnt_ref[...]
    tiles = jnp.floor((cnt + (ROW_TILE - 1)) * (1.0 / ROW_TILE))
    er = lax.broadcasted_iota(I32, (N_EXPERTS, N_EXPERTS), 0)
    ec = lax.broadcasted_iota(I32, (N_EXPERTS, N_EXPERTS), 1)
    before = jnp.where(ec < er, 1.0, 0.0).astype(BF16)
    tiles_b = jnp.broadcast_to(tiles, (N_EXPERTS, LANES)).astype(BF16)
    t_start = _dot(before, tiles_b)[:, 0:1]
    t_end = t_start + tiles
    p_start = t_start * float(ROW_TILE)
    lt_ref[...] = jnp.where(tiles > 0.0, (t_end - 1.0) * float(ROW_TILE), -1.0).astype(I32)
    pad_ref[...] = (p_start + cnt).astype(I32)

    row = lax.broadcasted_iota(I32, (N_EXPERTS, DEST_TM), 0)
    d_ref[...] = jnp.concatenate(
        [jnp.sum(jnp.where(row == e_ref[k:k + 1, :], p_start, 0.0), axis=0, keepdims=True)
         for k in range(TOPK_EXPERTS)], axis=0).astype(I32) + rk_ref[...]

    tile_id = lax.broadcasted_iota(I32, (N_EXPERTS, TILE_TABLE), 1).astype(F32)
    owner = jnp.sum(jnp.where(t_end <= tile_id, 1, 0), axis=0, keepdims=True)
    te_ref[...] = jnp.minimum(owner, N_EXPERTS - 1)
    nu_ref[...] = jnp.broadcast_to(t_end[N_EXPERTS - 1:N_EXPERTS, :], (1, LANES)).astype(I32)


def _dest(cnt, e_t, rk_t):
    k = TOPK_EXPERTS
    tok = lambda i: (0, i)
    return pl.pallas_call(
        _dest_body,
        grid=(N_TOK // DEST_TM,),
        in_specs=[
            pl.BlockSpec((N_EXPERTS, 1), lambda i: (0, 0)),
            pl.BlockSpec((k, DEST_TM), tok),
            pl.BlockSpec((k, DEST_TM), tok),
        ],
        out_specs=[
            pl.BlockSpec((k, DEST_TM), tok),
            pl.BlockSpec((1, TILE_TABLE), lambda i: (0, 0)),
            pl.BlockSpec((1, LANES), lambda i: (0, 0)),
            pl.BlockSpec((N_EXPERTS, 1), lambda i: (0, 0)),
            pl.BlockSpec((N_EXPERTS, 1), lambda i: (0, 0)),
        ],
        out_shape=[
            jax.ShapeDtypeStruct((k, N_TOK), I32),
            jax.ShapeDtypeStruct((1, TILE_TABLE), I32),
            jax.ShapeDtypeStruct((1, LANES), I32),
            jax.ShapeDtypeStruct((N_EXPERTS, 1), I32),
            jax.ShapeDtypeStruct((N_EXPERTS, 1), I32),
        ],
        compiler_params=_cparams("arbitrary"),
        name="dest",
    )(cnt, e_t, rk_t)


def _row_copy(src, dst, sem):
    return pltpu.make_async_copy(src, dst, sem)


def _packed_rows(ref, first, n):
    start = first * PACKED_CHUNKS
    if not isinstance(first, int):
        start = pl.multiple_of(start, PACKED_CHUNKS)
    return ref.at[pl.ds(start, n * PACKED_CHUNKS), :]


def _dispatch_body(lt_ref, pad_ref, nu_ref, d_ref, h_ref, xs_ref, zero_ref, sem, zero_sem):
    step = pl.program_id(0)
    n_steps = N_TOK // DISP_TM
    experts_per_step = N_EXPERTS // n_steps
    assert experts_per_step * n_steps == N_EXPERTS
    tail_per_step = -(-(N_ROW_TILES - N_ASSIGN // ROW_TILE) // n_steps)

    @pl.when(step == 0)
    def _():
        zero_ref[...] = jnp.zeros_like(zero_ref)

    def zero_fills(act):
        for j in range(experts_per_step):
            e = step * experts_per_step + j
            row = pad_ref[e]
            n_pad = jnp.where(lt_ref[e] >= 0, lt_ref[e] + ROW_TILE - row, 0)
            for bit in reversed(range(ROW_TILE.bit_length() - 1)):
                run = 1 << bit
                has_run = (n_pad & run) != 0

                @pl.when(has_run)
                def _(row=row, run=run):
                    act(pltpu.make_async_copy(_packed_rows(zero_ref, 0, run), _packed_rows(xs_ref, row, run),
                                              zero_sem))

                row = row + jnp.where(has_run, run, 0)
        for j in range(tail_per_step):
            tile = nu_ref[0] + step * tail_per_step + j

            @pl.when(tile < N_ROW_TILES)
            def _(tile=tile):
                act(pltpu.make_async_copy(zero_ref, _packed_rows(xs_ref, tile * ROW_TILE, ROW_TILE), zero_sem))

    zero_fills(lambda c: c.start())
    for t in range(DISP_TM):
        for k in range(TOPK_EXPERTS):
            _row_copy(_packed_rows(h_ref, t, 1), _packed_rows(xs_ref, d_ref[k, t], 1), sem).start(priority=k % 2)
    for k in range(TOPK_EXPERTS):
        pltpu.make_async_copy(h_ref, _packed_rows(xs_ref, 0, DISP_TM), sem).wait()
    zero_fills(lambda c: c.wait())


def _dispatch(last_tile_row, pad_row, n_used, dest_t, h1):
    grid_spec = pltpu.PrefetchScalarGridSpec(
        num_scalar_prefetch=3,
        grid=(N_TOK // DISP_TM,),
        in_specs=[
            pl.BlockSpec((TOPK_EXPERTS, DISP_TM), lambda i, lt, pad, nu: (0, i), memory_space=pltpu.SMEM),
            pl.BlockSpec((DISP_TM * PACKED_CHUNKS, LANES), lambda i, lt, pad, nu: (i, 0)),
        ],
        out_specs=pl.BlockSpec(memory_space=pl.ANY),
        scratch_shapes=[pltpu.VMEM((ROW_TILE * PACKED_CHUNKS, LANES), U32), pltpu.SemaphoreType.DMA,
                        pltpu.SemaphoreType.DMA],
    )
    return pl.pallas_call(
        _dispatch_body,
        grid_spec=grid_spec,
        out_shape=jax.ShapeDtypeStruct((N_ROWS * PACKED_CHUNKS, LANES), U32),
        compiler_params=_cparams("arbitrary"),
        name="dispatch",
    )(last_tile_row, pad_row, n_used, dest_t, h1)


def _row_destinations(dest_flat):
    info = plsc.get_sparse_core_info()
    n_cores, lanes = info.num_cores, info.num_lanes
    n_workers = n_cores * info.num_subcores
    rows_per_worker = -(-N_ROWS // (n_workers * ROW_TILE)) * ROW_TILE
    assert rows_per_worker % lanes == 0
    mesh = plsc.VectorSubcoreMesh(core_axis_name="c", subcore_axis_name="s")

    @functools.partial(
        pl.kernel, mesh=mesh, out_type=jax.ShapeDtypeStruct((n_workers * rows_per_worker,), I32),
        scratch_types=[pltpu.VMEM((SLOT_CHUNK,), I32), pltpu.VMEM((rows_per_worker,), I32)],
        compiler_params=pltpu.CompilerParams(needs_layout_passes=False), name="row_destinations")
    def invert(dest_hbm, out_hbm, dest_v, table_v):
        first_row = (lax.axis_index("s") * n_cores + lax.axis_index("c")) * rows_per_worker
        lane = lax.iota(I32, lanes)
        tile_shift = ROW_TILE.bit_length() - 1

        def spare(i, carry):
            row = first_row + i * lanes + lane
            buf = lax.rem(lax.shift_right_logical(row, tile_shift), EXPERT_BUFS)
            table_v[pl.ds(i * lanes, lanes)] = TOPK_EXPERTS * N_TOK + buf * ROW_TILE + (row & (ROW_TILE - 1))
            return carry

        lax.fori_loop(0, rows_per_worker // lanes, spare, 0)

        def chunk(c, carry):
            pltpu.sync_copy(dest_hbm.at[pl.ds(c * SLOT_CHUNK, SLOT_CHUNK)], dest_v)

            def vec(i, inner):
                local = dest_v[pl.ds(i * lanes, lanes)] - first_row
                mine = (local >= 0) & (local < rows_per_worker)
                plsc.store_scatter(table_v, [jnp.where(mine, local, 0)], lane + (c * SLOT_CHUNK + i * lanes),
                                   mask=mine)
                return inner

            lax.fori_loop(0, SLOT_CHUNK // lanes, vec, 0)
            return carry

        lax.fori_loop(0, N_ASSIGN // SLOT_CHUNK, chunk, 0)
        pltpu.sync_copy(table_v, out_hbm.at[pl.ds(first_row, rows_per_worker)])

    return invert(dest_flat)


def _experts_body(te_ref, nu_ref, lt_ref, slot_ref, xs_ref, wg_ref, wu_ref, wd_ref, yk_ref,
                  wg_f, wu_f, wd_f, wg_b, wu_b, wd_b, y_buf, n_loaded, sem, w_sem):
    i = pl.program_id(0)
    n_used = nu_ref[0]

    def weight_copies(e, s):
        return (pltpu.make_async_copy(wg_ref.at[e], wg_f.at[s], w_sem.at[s]),
                pltpu.make_async_copy(wu_ref.at[e], wu_f.at[s], w_sem.at[s]),
                pltpu.make_async_copy(wd_ref.at[e], wd_f.at[s], w_sem.at[s]))

    def wait_tile(b):
        pltpu.make_async_copy(y_buf.at[b], _packed_rows(yk_ref, 0, ROW_TILE), sem.at[b]).wait()

    def send_tile(b):
        for r in range(ROW_TILE):
            dst = slot_ref[0, 0, r]
            _row_copy(_packed_rows(y_buf.at[b], r, 1), _packed_rows(yk_ref, dst, 1),
                      sem.at[b]).start(priority=r % 2)

    @pl.when(i == 0)
    def _():
        y_buf[EXPERT_BUFS - 1] = jnp.zeros((ROW_TILE * PACKED_CHUNKS, LANES), U32)
        spare = [pltpu.make_async_copy(y_buf.at[EXPERT_BUFS - 1],
                                       _packed_rows(yk_ref, TOPK_EXPERTS * N_TOK + b * ROW_TILE, ROW_TILE),
                                       sem.at[b]) for b in range(EXPERT_BUFS - 1)]
        for c in spare:
            c.start()
        for c in spare:
            c.wait()
        n_loaded[0] = 0
        for c in weight_copies(te_ref[0], 0):
            c.start()

    @pl.when((i >= 2) & (i <= n_used))
    def _():
        wait_tile(lax.rem(i, EXPERT_BUFS))

    @pl.when(i < n_used)
    def _():
        e = te_ref[i]
        prev = te_ref[jnp.maximum(i - 1, 0)]

        @pl.when((i == 0) | (e != prev))
        def _():
            s = lax.rem(n_loaded[0], 2)
            for c in weight_copies(e, s):
                c.wait()
            wg_b[...] = wg_f[s].astype(BF16)
            wu_b[...] = wu_f[s].astype(BF16)
            wd_b[...] = wd_f[s].astype(BF16)
            n_loaded[0] = n_loaded[0] + 1
            nxt = lax.shift_right_logical(lt_ref[e], ROW_TILE.bit_length() - 1) + 1

            @pl.when(nxt < n_used)
            def _():
                for c in weight_copies(te_ref[nxt], 1 - s):
                    c.start()

        for phase in range(EXPERT_BUFS):
            @pl.when(lax.rem(i, EXPERT_BUFS) == phase)
            def _(phase=phase):
                send_tile((phase + EXPERT_BUFS - 1) % EXPERT_BUFS)
                packed = jnp.concatenate([xs_ref[pl.ds(c, ROW_TILE, stride=PACKED_CHUNKS), :]
                                          for c in range(PACKED_CHUNKS)], axis=1)
                x_lo, x_hi = _unpack_bf16_pair(packed)
                x = jnp.concatenate([x_lo.astype(BF16), x_hi.astype(BF16)], axis=1)
                g = _dot(x, wg_b[...])
                u = _dot(x, wu_b[...])
                h = (g * _sigmoid(g)) * u
                y = _pack_bf16_pair(_dot(h.astype(BF16), wd_b[...]))
                for c in range(PACKED_CHUNKS):
                    y_buf[phase, pl.ds(c, ROW_TILE, stride=PACKED_CHUNKS), :] = y[:, c * LANES:(c + 1) * LANES]

    @pl.when(i == n_used)
    def _():
        send_tile(lax.rem(i + EXPERT_BUFS - 1, EXPERT_BUFS))
        wait_tile(lax.rem(i + EXPERT_BUFS - 2, EXPERT_BUFS))
        wait_tile(lax.rem(i + EXPERT_BUFS - 1, EXPERT_BUFS))


def _experts(tile_expert, n_used, last_tile_row, row_dst, xs, w_gate, w_up, w_down):
    def tile(i, te, nu, lt):
        return (jnp.minimum(i, nu[0] - 1), 0)

    n_table_tiles = row_dst.shape[0] // ROW_TILE
    placeholder = n_table_tiles - 1
    assert placeholder * ROW_TILE >= N_ROWS and placeholder % EXPERT_BUFS == EXPERT_BUFS - 1

    def prev_tile(i, te, nu, lt):
        return (jnp.where(i == 0, placeholder, jnp.minimum(i, nu[0]) - 1), 0, 0)

    grid_spec = pltpu.PrefetchScalarGridSpec(
        num_scalar_prefetch=3,
        grid=(n_used[0] + 1,),
        in_specs=[
            pl.BlockSpec((1, 1, ROW_TILE), prev_tile, memory_space=pltpu.SMEM),
            pl.BlockSpec((ROW_TILE * PACKED_CHUNKS, LANES), tile),
            pl.BlockSpec(memory_space=pl.ANY),
            pl.BlockSpec(memory_space=pl.ANY),
            pl.BlockSpec(memory_space=pl.ANY),
        ],
        out_specs=pl.BlockSpec(memory_space=pl.ANY),
        scratch_shapes=[
            pltpu.VMEM((2, D_MODEL, D_EXPERT), F32),
            pltpu.VMEM((2, D_MODEL, D_EXPERT), F32),
            pltpu.VMEM((2, D_EXPERT, D_MODEL), F32),
            pltpu.VMEM((D_MODEL, D_EXPERT), BF16),
            pltpu.VMEM((D_MODEL, D_EXPERT), BF16),
            pltpu.VMEM((D_EXPERT, D_MODEL), BF16),
            pltpu.VMEM((EXPERT_BUFS, ROW_TILE * PACKED_CHUNKS, LANES), U32),
            pltpu.SMEM((1,), I32),
            pltpu.SemaphoreType.DMA((EXPERT_BUFS,)),
            pltpu.SemaphoreType.DMA((2,)),
        ],
    )
    return pl.pallas_call(
        _experts_body,
        grid_spec=grid_spec,
        out_shape=jax.ShapeDtypeStruct((YK_ROWS * PACKED_CHUNKS, LANES), U32),
        compiler_params=_cparams("arbitrary"),
        name="experts",
    )(tile_expert, n_used, last_tile_row, row_dst.reshape(n_table_tiles, 1, ROW_TILE), xs, w_gate, w_up, w_down)


def _final_body(w_ref, *refs):
    yk_refs = refs[:TOPK_EXPERTS]
    h_ref, hb_ref, p_ref, wsg_ref, wsu_ref, wsd_ref, wpl_ref, wpg_ref, g_ref, b_ref, o_ref = refs[TOPK_EXPERTS:]
    hb = hb_ref[...]
    sg = _dot(hb, wsg_ref[...])
    shared = _dot(((sg * _sigmoid(sg)) * _dot(hb, wsu_ref[...])).astype(BF16), wsd_ref[...])
    ple = _sigmoid(_dot(hb, wpg_ref[...])) * _dot(p_ref[...].astype(BF16), wpl_ref[...])

    def plane(k):
        return jnp.concatenate([yk_refs[k][pl.ds(c, FINAL_TM, stride=PACKED_CHUNKS), :]
                                for c in range(PACKED_CHUNKS)], axis=1)

    w_col = w_ref[...].T
    y_lo, y_hi = _unpack_bf16_pair(plane(0))
    r_lo, r_hi = y_lo * w_col[:, 0:1], y_hi * w_col[:, 0:1]
    for k in range(1, TOPK_EXPERTS):
        y_lo, y_hi = _unpack_bf16_pair(plane(k))
        r_lo, r_hi = r_lo + y_lo * w_col[:, k:k + 1], r_hi + y_hi * w_col[:, k:k + 1]
    routed = jnp.concatenate([r_lo, r_hi], axis=1)
    o_ref[...] = _layer_norm(ALPHA * h_ref[...] + (routed + shared) + ple, g_ref[...], b_ref[...])


def _final(w_t, yk, h1, h_bf, p2d, w_s_gate, w_s_up, w_s_down, w_ple, w_ple_gate, ln_g, ln_b):
    d = D_MODEL
    n_steps = N_TOK // FINAL_TM
    full = lambda i: (0, 0)
    return pl.pallas_call(
        _final_body,
        grid=(n_steps,),
        in_specs=[
            pl.BlockSpec((TOPK_EXPERTS, FINAL_TM), lambda i: (0, i)),
            *[pl.BlockSpec((FINAL_TM * PACKED_CHUNKS, LANES), lambda i, k=k: (k * n_steps + i, 0))
              for k in range(TOPK_EXPERTS)],
            pl.BlockSpec((FINAL_TM, d), lambda i: (i, 0)),
            pl.BlockSpec((FINAL_TM, d), lambda i: (i, 0)),
            pl.BlockSpec((FINAL_TM, PLE_DIM), lambda i: (i, 0)),
            pl.BlockSpec((d, D_SHARED), full),
            pl.BlockSpec((d, D_SHARED), full),
            pl.BlockSpec((D_SHARED, d), full),
            pl.BlockSpec((PLE_DIM, d), full),
            pl.BlockSpec((d, d), full),
            pl.BlockSpec((1, d), full),
            pl.BlockSpec((1, d), full),
        ],
        out_specs=pl.BlockSpec((FINAL_TM, d), lambda i: (i, 0)),
        out_shape=jax.ShapeDtypeStruct((N_TOK, d), F32),
        compiler_params=_cparams("parallel"),
        name="combine_ln2",
    )(w_t, *([yk] * TOPK_EXPERTS), h1, h_bf, p2d, w_s_gate, w_s_up, w_s_down, w_ple, w_ple_gate, ln_g, ln_b)


def _rope_tables(positions):
    half = ROT_DIM // 2
    inv = ROPE_THETA ** (-jnp.arange(0, ROT_DIM, 2, dtype=F32) / ROT_DIM)
    per_row = LANES // half
    pos = jnp.repeat(positions.reshape(-1, per_row), half, axis=1).astype(F32)
    ang = pos * jnp.tile(inv, per_row)[None, :]
    cos, sin = lax.optimization_barrier((jnp.cos(ang), jnp.sin(ang)))
    cos = cos.reshape(positions.shape + (half,))
    sin = sin.reshape(positions.shape + (half,))
    rest = MOBA_DH - ROT_DIM
    ones = jnp.ones(cos.shape[:-1] + (rest,), F32)
    zeros = jnp.zeros(cos.shape[:-1] + (rest,), F32)
    return (jnp.concatenate([cos, cos, ones], axis=-1), jnp.concatenate([-sin, sin, zeros], axis=-1))


def _layer(h2d, p2d, cos_t, sin_t, w_cat, w_gk2, b_gk, norm_g, w_gla_o, w_moba_o, w_out, ln1_g, ln1_b,
           w_router, b_router, w_e_gate, w_e_up, w_e_down, w_s_gate, w_s_up, w_s_down, w_ple, w_ple_gate,
           ln2_g, ln2_b):
    w_gk2_pad = jnp.concatenate([w_gk2, jnp.zeros((LANES - GLA_RANK, GLA_KDIM), w_gk2.dtype)], axis=0)

    proj = _in_proj(h2d, w_cat)
    gla_out = _gla(proj, w_gk2_pad, b_gk[None, :], norm_g[None, :])
    moba_out = _moba(proj, cos_t, sin_t)
    h1, h_bf, h_packed = _mix(gla_out, moba_out, proj, h2d, w_gla_o.astype(BF16), w_moba_o.astype(BF16),
                              w_out.astype(BF16), ln1_g[None, :], ln1_b[None, :])
    e_t, w_t, rk_t, cnt = _route(h_bf, w_router.T.astype(BF16), b_router[:, None])
    dest_t, tile_expert, n_used, last_tile_row, pad_row = _dest(cnt, e_t, rk_t)
    last_tile_row = last_tile_row.reshape(N_EXPERTS)
    n_used = n_used[0, 0:1]
    xs = _dispatch(last_tile_row, pad_row.reshape(N_EXPERTS), n_used, dest_t, h_packed)
    row_dst = _row_destinations(dest_t.reshape(N_ASSIGN))
    yk = _experts(tile_expert.reshape(TILE_TABLE), n_used, last_tile_row, row_dst, xs,
                  w_e_gate, w_e_up, w_e_down)
    return _final(w_t, yk, h1, h_bf, p2d, w_s_gate.astype(BF16), w_s_up.astype(BF16),
                  w_s_down.astype(BF16), w_ple.astype(BF16), w_ple_gate.astype(BF16),
                  ln2_g[None, :], ln2_b[None, :])


def kernel(x, p, positions, w_in, w_gla_gk2, b_gla_gk, gla_norm_g, w_gla_o, w_moba_o, w_out, ln1_g, ln1_b,
           w_router, b_router, w_e_gate, w_e_up, w_e_down, w_s_gate, w_s_up, w_s_down, w_ple, w_ple_gate,
           ln2_g, ln2_b):
    cos_t, sin_t = _rope_tables(positions)
    h = x.reshape(N_TOK, D_MODEL)
    for i in range(DEPTH):
        h = _layer(h, p[i].reshape(N_TOK, PLE_DIM), cos_t, sin_t, _repack_w_in(w_in, i), w_gla_gk2[i], b_gla_gk[i],
                   gla_norm_g[i], w_gla_o[i], w_moba_o[i], w_out[i], ln1_g[i], ln1_b[i], w_router[i],
                   b_router[i], w_e_gate[i], w_e_up[i], w_e_down[i], w_s_gate[i], w_s_up[i], w_s_down[i],
                   w_ple[i], w_ple_gate[i], ln2_g[i], ln2_b[i])
    return h.reshape(BATCH, SEQ, D_MODEL)
```

```python
import functools

import jax
import jax.numpy as jnp
from jax import lax
from jax.experimental import pallas as pl
from jax.experimental.pallas import tpu as pltpu
from jax.experimental.pallas import tpu_sc as plsc

F32 = jnp.float32
BF16 = jnp.bfloat16
I32 = jnp.int32
U32 = jnp.uint32

LANES = 128
SUBLANES = 8
VMEM_LIMIT_BYTES = 48 * 1024 * 1024

D_MODEL = 1024
BATCH = 8
SEQ = 2048
N_TOK = BATCH * SEQ
GLA_HEADS = 4
GLA_DK = 128
GLA_DV = 256
GLA_RANK = 16
GLA_NORMALIZER = 16.0
GLA_CHUNK = 64
GLA_GROUP = 2 * GLA_CHUNK
GLA_UNROLL = 8
MOBA_HEADS = 8
MOBA_DH = 128
MOBA_BLOCK = 256
MOBA_TOPK = 3
ROT_DIM = 32
ROPE_THETA = 500000.0
N_EXPERTS = 256
TOPK_EXPERTS = 8
N_GROUPS = 8
GROUP_SIZE = N_EXPERTS // N_GROUPS
TOPK_GROUPS = 4
D_EXPERT = 256
D_SHARED = 256
ROUTED_SCALE = 2.5
PLE_DIM = 256
LN_EPS = 1e-5
DEPTH = 1
ALPHA = (2.0 * DEPTH) ** 0.25
GLA_KDIM = GLA_HEADS * GLA_DK
GLA_VDIM = GLA_HEADS * GLA_DV
MOBA_DIM = MOBA_HEADS * MOBA_DH
N_KBLK = SEQ // MOBA_BLOCK

COL_GQ = 0
COL_GK = COL_GQ + GLA_KDIM
COL_GV = COL_GK + GLA_KDIM
COL_GR = COL_GV + GLA_VDIM
COL_MQ = COL_GR + GLA_VDIM
COL_MK = COL_MQ + MOBA_DIM
COL_MV = COL_MK + MOBA_DIM
COL_GA = COL_MV + MOBA_DIM
COL_GB = COL_GA + D_MODEL
COL_LOW = COL_GB + D_MODEL
PROJ_COLS = COL_LOW + LANES
LOW_SRC = 2 * GLA_KDIM + 2 * GLA_VDIM

ROW_TILE = 256
PACKED_COLS = D_MODEL // 2
PACKED_CHUNKS = PACKED_COLS // LANES
EXPERT_BUFS = 3
YK_ROWS = TOPK_EXPERTS * N_TOK + EXPERT_BUFS * ROW_TILE
SLOT_CHUNK = 4096
assert EXPERT_BUFS * ROW_TILE <= N_TOK
N_ASSIGN = N_TOK * TOPK_EXPERTS
N_ROW_TILES = (N_ASSIGN + N_EXPERTS * (ROW_TILE - 1) + ROW_TILE - 1) // ROW_TILE
N_ROWS = N_ROW_TILES * ROW_TILE
TILE_TABLE = -(-N_ROW_TILES // LANES) * LANES

PROJ_TM = 1024
REPACK_TN = 512
PROJ_TN = PROJ_COLS // 5
MIX_TM = 512
MIX_SUB = 256
ROUTE_TM = 512
DEST_TM = 2048
DISP_TM = 1024
FINAL_TM = 512

_NEG_INF = float("-inf")
LOG2_E = 1.4426950408889634


def _cparams(*sem):
    return pltpu.CompilerParams(dimension_semantics=sem, vmem_limit_bytes=VMEM_LIMIT_BYTES)


def _dot(a, b):
    return jnp.dot(a, b, preferred_element_type=F32)


def _dot_nt(a, b):
    return lax.dot_general(a, b, (((1,), (1,)), ((), ())), preferred_element_type=F32)


def _dot_tn(a, b):
    return lax.dot_general(a, b, (((0,), (0,)), ((), ())), preferred_element_type=F32)


def _split_bf16(x):
    hi = x.astype(BF16)
    lo = (x - hi.astype(F32)).astype(BF16)
    return hi, lo


def _sigmoid(x):
    return 1.0 / (1.0 + jnp.exp(-x))


def _pack_bf16_pair(x):
    lo = lax.bitcast_convert_type(x[:, :PACKED_COLS].astype(BF16).astype(F32), U32)
    hi = lax.bitcast_convert_type(x[:, PACKED_COLS:].astype(BF16).astype(F32), U32)
    return hi | lax.shift_right_logical(lo, jnp.uint32(16))


def _unpack_bf16_pair(w):
    lo = lax.bitcast_convert_type(lax.shift_left(w, jnp.uint32(16)), F32)
    hi = lax.bitcast_convert_type(w & jnp.uint32(0xFFFF0000), F32)
    return lo, hi


def _repack_body(a_ref, b_ref, o_ref):
    t = pl.program_id(0)
    a = a_ref[...].astype(BF16)

    @pl.when(t < LOW_SRC // REPACK_TN)
    def _():
        o_ref[...] = a

    @pl.when((t >= LOW_SRC // REPACK_TN) & (t < COL_LOW // REPACK_TN))
    def _():
        o_ref[...] = jnp.concatenate([a[GLA_RANK:], b_ref[...].astype(BF16)], axis=0)

    @pl.when(t == COL_LOW // REPACK_TN)
    def _():
        o_ref[...] = jnp.concatenate([a[:GLA_RANK], jnp.zeros((REPACK_TN - GLA_RANK, D_MODEL), BF16)], axis=0)


def _repack_w_in(w_in, layer):
    assert LOW_SRC % REPACK_TN == 0 and COL_LOW % REPACK_TN == 0 and REPACK_TN % GLA_RANK == 0
    w_t = jnp.swapaxes(w_in, 1, 2)
    low_tile = LOW_SRC // REPACK_TN
    last_out = COL_LOW // REPACK_TN
    ranks_per_tile = REPACK_TN // GLA_RANK
    return pl.pallas_call(
        _repack_body,
        grid=(last_out + 1,),
        in_specs=[
            pl.BlockSpec((None, REPACK_TN, D_MODEL), lambda t: (layer, jnp.where(t == last_out, low_tile, t), 0)),
            pl.BlockSpec((None, GLA_RANK, D_MODEL), lambda t: (layer, jnp.minimum(t + 1, last_out) * ranks_per_tile, 0)),
        ],
        out_specs=pl.BlockSpec((REPACK_TN, D_MODEL), lambda t: (t, 0)),
        out_shape=jax.ShapeDtypeStruct((PROJ_COLS, D_MODEL), BF16),
        compiler_params=_cparams("parallel"),
        name="repack_w_in",
    )(w_t, w_t)


def _proj_body(x_ref, w_ref, o_ref):
    o_ref[...] = _dot_nt(x_ref[...].astype(BF16), w_ref[...]).astype(BF16)


def _in_proj(x2d, w_cat):
    return pl.pallas_call(
        _proj_body,
        grid=(PROJ_COLS // PROJ_TN, N_TOK // PROJ_TM),
        in_specs=[
            pl.BlockSpec((PROJ_TM, D_MODEL), lambda j, i: (i, 0)),
            pl.BlockSpec((PROJ_TN, D_MODEL), lambda j, i: (j, 0)),
        ],
        out_specs=pl.BlockSpec((PROJ_TM, PROJ_TN), lambda j, i: (i, j)),
        out_shape=jax.ShapeDtypeStruct((N_TOK, PROJ_COLS), BF16),
        compiler_params=_cparams("parallel", "parallel"),
        name="in_proj",
    )(x2d, w_cat)


def _gla_body(q_ref, k_ref, v_ref, r_ref, low_ref, wg_ref, bg_ref, ng_ref, o_ref, st_ref, gk_ref):
    w_hi, w_lo = _split_bf16(wg_ref[...])
    low = low_ref[...]
    lin = _dot(low, w_hi) + _dot(low, w_lo) + bg_ref[...]
    gk_ref[...] = (jnp.minimum(lin, 0.0) - jnp.log1p(jnp.exp(-jnp.abs(lin)))) * (1.0 / GLA_NORMALIZER)
    st_ref[...] = jnp.zeros_like(st_ref)

    ri = lax.broadcasted_iota(I32, (GLA_GROUP, GLA_GROUP), 0)
    ci = lax.broadcasted_iota(I32, (GLA_GROUP, GLA_GROUP), 1)
    same_chunk = lax.shift_right_logical(ri, GLA_CHUNK.bit_length() - 1) == lax.shift_right_logical(
        ci, GLA_CHUNK.bit_length() - 1)
    causal = same_chunk & (ri >= ci)
    sums = jnp.concatenate([jnp.where(causal, 1.0, 0.0), jnp.where(same_chunk, 1.0, 0.0)], axis=0).astype(BF16)
    gain = ng_ref[...]

    def group(c, carry):
        rows = pl.ds(pl.multiple_of(c * GLA_GROUP, GLA_GROUP), GLA_GROUP)
        g_hi, g_lo = _split_bf16(gk_ref[rows, :])
        bb = _dot(sums, g_hi) + _dot(sums, g_lo)
        b = bb[0:GLA_GROUP]
        b_end = bb[GLA_GROUP:2 * GLA_GROUP]
        q = q_ref[rows, :].astype(F32) * (GLA_DK ** -0.5)
        k = k_ref[rows, :].astype(F32)
        v = v_ref[rows, :]
        q_e = (q * jnp.exp(b)).astype(BF16)
        k_e = (k * jnp.exp(-b)).astype(BF16)
        k_d = (k * jnp.exp(b_end - b)).astype(BF16)
        att = jnp.where(causal, _dot_nt(q_e, k_e), 0.0)
        o = _dot(att.astype(BF16), v)
        st = st_ref[...]
        inter = []
        for j in range(GLA_GROUP // GLA_CHUNK):
            cr = slice(j * GLA_CHUNK, (j + 1) * GLA_CHUNK)
            inter.append(_dot_nt(q_e[cr], st.astype(BF16)))
            st = st * jnp.exp(b_end[j * GLA_CHUNK:j * GLA_CHUNK + 1, :]) + _dot_tn(v[cr], k_d[cr])
        st_ref[...] = st
        o = o + jnp.concatenate(inter, axis=0)
        o = o * lax.rsqrt(jnp.mean(o * o, axis=-1, keepdims=True) + LN_EPS) * gain
        r = r_ref[rows, :].astype(F32)
        o_ref[rows, :] = (o * (r * _sigmoid(r))).astype(BF16)
        return carry

    lax.fori_loop(0, SEQ // GLA_GROUP, group, 0, unroll=GLA_UNROLL)


def _gla(proj, w_gk2_pad, b_gk, norm_g):
    kb, vb = GLA_DK, GLA_DV
    return pl.pallas_call(
        _gla_body,
        grid=(BATCH, GLA_HEADS),
        in_specs=[
            pl.BlockSpec((SEQ, kb), lambda b, h: (b, COL_GQ // kb + h)),
            pl.BlockSpec((SEQ, kb), lambda b, h: (b, COL_GK // kb + h)),
            pl.BlockSpec((SEQ, vb), lambda b, h: (b, COL_GV // vb + h)),
            pl.BlockSpec((SEQ, vb), lambda b, h: (b, COL_GR // vb + h)),
            pl.BlockSpec((SEQ, LANES), lambda b, h: (b, COL_LOW // LANES)),
            pl.BlockSpec((LANES, kb), lambda b, h: (0, h)),
            pl.BlockSpec((1, kb), lambda b, h: (0, h)),
            pl.BlockSpec((1, vb), lambda b, h: (0, 0)),
        ],
        out_specs=pl.BlockSpec((SEQ, vb), lambda b, h: (b, h)),
        out_shape=jax.ShapeDtypeStruct((N_TOK, GLA_VDIM), BF16),
        scratch_shapes=[pltpu.VMEM((vb, kb), F32), pltpu.VMEM((SEQ, kb), F32)],
        compiler_params=_cparams("parallel", "parallel"),
        name="gla",
    )(proj, proj, proj, proj, proj, w_gk2_pad, b_gk, norm_g)


def _moba_body(q_ref, k_ref, v_ref, c_ref, s_ref, o_ref, qs_ref, ks_ref, vt_ref, bias_ref,
               sc_a_ref, sc_b_ref, pr_a_ref, pr_b_ref):
    cos_t = c_ref[0]
    sin_t = s_ref[0]
    lane = lax.broadcasted_iota(I32, (SEQ, MOBA_DH), 1)
    half = ROT_DIM // 2

    def rope(x):
        partner = jnp.where(lane < half, pltpu.roll(x, MOBA_DH - half, 1), pltpu.roll(x, half, 1))
        return x * cos_t + partner * sin_t

    q = rope(q_ref[...].astype(F32))
    k = rope(k_ref[...].astype(F32))
    q_hi, q_lo = _split_bf16(q)
    qs_ref[...] = (q * (MOBA_DH ** -0.5 * LOG2_E)).astype(BF16)
    ks_ref[...] = k.astype(BF16)
    vt_ref[...] = v_ref[...].astype(F32).T.astype(BF16)

    k_mean = jnp.concatenate(
        [jnp.mean(k[j * MOBA_BLOCK:(j + 1) * MOBA_BLOCK], axis=0, keepdims=True) for j in range(N_KBLK)], axis=0)
    m_hi, m_lo = _split_bf16(k_mean)
    s_blk = _dot_nt(m_hi, q_hi) + _dot_nt(m_hi, q_lo) + _dot_nt(m_lo, q_hi)
    blk = lax.broadcasted_iota(I32, (N_KBLK, SEQ), 0)
    q_blk = lax.shift_right_logical(lax.broadcasted_iota(I32, (N_KBLK, SEQ), 1), MOBA_BLOCK.bit_length() - 1)
    past = blk < q_blk
    s_blk = jnp.where(past, s_blk, _NEG_INF)
    beaten = jnp.zeros((N_KBLK, SEQ), I32)
    for j in range(N_KBLK):
        row = s_blk[j:j + 1, :]
        beaten += jnp.where((row > s_blk) | ((row == s_blk) & (j < blk)), 1, 0)
    bias_ref[...] = jnp.where(past & (beaten < MOBA_TOPK), 0.0, _NEG_INF)

    kr = lax.broadcasted_iota(I32, (MOBA_BLOCK, MOBA_BLOCK), 0)
    qc = lax.broadcasted_iota(I32, (MOBA_BLOCK, MOBA_BLOCK), 1)
    own_bias = jnp.where(kr <= qc, 0.0, _NEG_INF)

    sc_bufs = (sc_a_ref, sc_b_ref)
    pr_bufs = (pr_a_ref, pr_b_ref)
    zeros_row = jnp.zeros((1, MOBA_BLOCK), F32)
    future_row = jnp.full((1, MOBA_BLOCK), _NEG_INF, F32)
    for pair in range(N_KBLK // 2):
        q_blocks = (2 * pair, 2 * pair + 1)
        cols = slice(q_blocks[0] * MOBA_BLOCK, (q_blocks[1] + 1) * MOBA_BLOCK)
        n_kblk = q_blocks[1] + 1
        n_keys = n_kblk * MOBA_BLOCK
        sc, pr = sc_bufs[pair % 2], pr_bufs[pair % 2]
        q_pair = qs_ref[cols, :]

        def query_bias(j):
            halves = []
            for qb in q_blocks:
                q_cols = slice(qb * MOBA_BLOCK, (qb + 1) * MOBA_BLOCK)
                halves.append(bias_ref[j:j + 1, q_cols] if j < qb else zeros_row if j == qb else future_row)
            return jnp.concatenate(halves, axis=1)

        biases = [query_bias(j) for j in range(n_kblk)]
        col_max = []
        for j in range(n_kblk):
            rows = slice(j * MOBA_BLOCK, (j + 1) * MOBA_BLOCK)
            s = _dot_nt(ks_ref[rows, :], q_pair)
            if j in q_blocks:
                h = q_blocks.index(j)
                own = s[:, h * MOBA_BLOCK:(h + 1) * MOBA_BLOCK] + own_bias
                s = jnp.concatenate([own, s[:, MOBA_BLOCK:]] if h == 0 else [s[:, :MOBA_BLOCK], own], axis=1)
            sc[rows, :] = s
            col_max.append(jnp.max(s, axis=0, keepdims=True) + biases[j])
        m = functools.reduce(jnp.maximum, col_max)
        denom = jnp.zeros((1, 2 * MOBA_BLOCK), F32)
        for j in range(n_kblk):
            rows = slice(j * MOBA_BLOCK, (j + 1) * MOBA_BLOCK)
            p = jnp.exp2(sc[rows, :] - (m - biases[j]))
            denom = denom + jnp.sum(p, axis=0, keepdims=True)
            pr[rows, :] = p.astype(BF16)
        o_t = _dot(vt_ref[:, 0:n_keys], pr[0:n_keys, :]) * (1.0 / denom)
        o_ref[cols, :] = o_t.T.astype(BF16)


def _moba(proj, cos_t, sin_t):
    dh = MOBA_DH
    return pl.pallas_call(
        _moba_body,
        grid=(BATCH, MOBA_HEADS),
        in_specs=[
            pl.BlockSpec((SEQ, dh), lambda b, h: (b, COL_MQ // dh + h)),
            pl.BlockSpec((SEQ, dh), lambda b, h: (b, COL_MK // dh + h)),
            pl.BlockSpec((SEQ, dh), lambda b, h: (b, COL_MV // dh + h)),
            pl.BlockSpec((1, SEQ, dh), lambda b, h: (b, 0, 0)),
            pl.BlockSpec((1, SEQ, dh), lambda b, h: (b, 0, 0)),
        ],
        out_specs=pl.BlockSpec((SEQ, dh), lambda b, h: (b, h)),
        out_shape=jax.ShapeDtypeStruct((N_TOK, MOBA_DIM), BF16),
        scratch_shapes=[
            pltpu.VMEM((SEQ, dh), BF16),
            pltpu.VMEM((SEQ, dh), BF16),
            pltpu.VMEM((dh, SEQ), BF16),
            pltpu.VMEM((N_KBLK, SEQ), F32),
            pltpu.VMEM((SEQ, 2 * MOBA_BLOCK), F32),
            pltpu.VMEM((SEQ, 2 * MOBA_BLOCK), F32),
            pltpu.VMEM((SEQ, 2 * MOBA_BLOCK), BF16),
            pltpu.VMEM((SEQ, 2 * MOBA_BLOCK), BF16),
        ],
        compiler_params=_cparams("parallel", "parallel"),
        name="moba",
    )(proj, proj, proj, cos_t, sin_t)


def _layer_norm(z, g, b):
    mu = jnp.mean(z, axis=-1, keepdims=True)
    zc = z - mu
    var = jnp.mean(zc * zc, axis=-1, keepdims=True)
    return zc * lax.rsqrt(var + LN_EPS) * g + b


def _mix_body(gla_ref, moba_ref, ga_ref, gb_ref, x_ref, wgo_ref, wmo_ref, wo_ref, g_ref, b_ref,
              h_ref, hb_ref, hp_ref):
    for s in range(MIX_TM // MIX_SUB):
        rows = slice(s * MIX_SUB, (s + 1) * MIX_SUB)
        y_gla = _dot(gla_ref[rows, :], wgo_ref[...])
        y_moba = _dot(moba_ref[rows, :], wmo_ref[...])
        merged = (_sigmoid(ga_ref[rows, :].astype(F32)) * y_gla
                  + _sigmoid(gb_ref[rows, :].astype(F32)) * y_moba)
        mix = _dot(merged.astype(BF16), wo_ref[...])
        h = _layer_norm(ALPHA * x_ref[rows, :] + mix, g_ref[...], b_ref[...])
        h_ref[rows, :] = h
        hb_ref[rows, :] = h.astype(BF16)
        packed = _pack_bf16_pair(h)
        for c in range(PACKED_CHUNKS):
            hp_ref[pl.ds(s * MIX_SUB * PACKED_CHUNKS + c, MIX_SUB, stride=PACKED_CHUNKS), :] = (
                packed[:, c * LANES:(c + 1) * LANES])


def _mix(gla_out, moba_out, proj, x2d, w_gla_o, w_moba_o, w_out, ln_g, ln_b):
    d = D_MODEL
    row = lambda i: (i, 0)
    full = lambda i: (0, 0)
    return pl.pallas_call(
        _mix_body,
        grid=(N_TOK // MIX_TM,),
        in_specs=[
            pl.BlockSpec((MIX_TM, d), row),
            pl.BlockSpec((MIX_TM, d), row),
            pl.BlockSpec((MIX_TM, d), lambda i: (i, COL_GA // d)),
            pl.BlockSpec((MIX_TM, d), lambda i: (i, COL_GB // d)),
            pl.BlockSpec((MIX_TM, d), row),
            pl.BlockSpec((d, d), full),
            pl.BlockSpec((d, d), full),
            pl.BlockSpec((d, d), full),
            pl.BlockSpec((1, d), full),
            pl.BlockSpec((1, d), full),
        ],
        out_specs=[
            pl.BlockSpec((MIX_TM, d), row),
            pl.BlockSpec((MIX_TM, d), row),
            pl.BlockSpec((MIX_TM * PACKED_CHUNKS, LANES), row),
        ],
        out_shape=[
            jax.ShapeDtypeStruct((N_TOK, d), F32),
            jax.ShapeDtypeStruct((N_TOK, d), BF16),
            jax.ShapeDtypeStruct((N_TOK * PACKED_CHUNKS, LANES), U32),
        ],
        compiler_params=_cparams("parallel"),
        name="mix_ln1",
    )(gla_out, moba_out, proj, proj, x2d, w_gla_o, w_moba_o, w_out, ln_g, ln_b)


def _route_body(h_ref, wr_ref, br_ref, e_ref, w_ref, rk_ref, cnt_ref, carry_ref):
    tm = ROUTE_TM

    @pl.when(pl.program_id(0) == 0)
    def _():
        carry_ref[...] = jnp.zeros_like(carry_ref)

    scores = _sigmoid(_dot_nt(wr_ref[...], h_ref[...]))
    biased = scores + br_ref[...]
    row = lax.broadcasted_iota(I32, (N_EXPERTS, tm), 0).astype(F32)
    row_g = lax.broadcasted_iota(I32, (GROUP_SIZE, tm), 0).astype(F32)

    g_scores = []
    for g in range(N_GROUPS):
        grp = biased[g * GROUP_SIZE:(g + 1) * GROUP_SIZE]
        m1 = jnp.max(grp, axis=0, keepdims=True)
        first = jnp.min(jnp.where(grp == m1, row_g, float(GROUP_SIZE)), axis=0, keepdims=True)
        m2 = jnp.max(jnp.where(row_g == first, _NEG_INF, grp), axis=0, keepdims=True)
        g_scores.append(m1 + m2)
    g_score = jnp.concatenate(g_scores, axis=0)
    g_row = lax.broadcasted_iota(I32, (N_GROUPS, tm), 0)
    g_beaten = jnp.zeros((N_GROUPS, tm), I32)
    for g in range(N_GROUPS):
        r = g_score[g:g + 1, :]
        g_beaten += jnp.where((r > g_score) | ((r == g_score) & (g < g_row)), 1, 0)
    g_keep = g_beaten < TOPK_GROUPS
    masked = jnp.concatenate(
        [jnp.where(g_keep[g:g + 1, :], biased[g * GROUP_SIZE:(g + 1) * GROUP_SIZE], _NEG_INF)
         for g in range(N_GROUPS)], axis=0)

    onehot = jnp.zeros((N_EXPERTS, tm), F32)
    picks, pick_scores = [], []
    for _ in range(TOPK_EXPERTS):
        m = jnp.max(masked, axis=0, keepdims=True)
        idx = jnp.min(jnp.where(masked == m, row, float(N_EXPERTS)), axis=0, keepdims=True)
        hit = row == idx
        picks.append(idx)
        pick_scores.append(jnp.sum(jnp.where(hit, scores, 0.0), axis=0, keepdims=True))
        onehot = onehot + jnp.where(hit, 1.0, 0.0)
        masked = jnp.where(hit, _NEG_INF, masked)
    sel = jnp.concatenate(pick_scores, axis=0)
    e_ref[...] = jnp.concatenate(picks, axis=0).astype(I32)
    w_ref[...] = sel / jnp.sum(sel, axis=0, keepdims=True) * ROUTED_SCALE

    t_r = lax.broadcasted_iota(I32, (tm, tm), 0)
    t_c = lax.broadcasted_iota(I32, (tm, tm), 1)
    earlier = jnp.where(t_r < t_c, 1.0, 0.0).astype(BF16)
    seen = _dot(onehot.astype(BF16), earlier) + carry_ref[...]
    rk_ref[...] = jnp.concatenate(
        [jnp.sum(jnp.where(row == idx, seen, 0.0), axis=0, keepdims=True) for idx in picks], axis=0).astype(I32)
    carry_ref[...] += jnp.sum(onehot, axis=1, keepdims=True)
    cnt_ref[...] = carry_ref[...]


def _route(h_bf, w_router_t, b_router_col):
    k = TOPK_EXPERTS
    tok = lambda i: (0, i)
    return pl.pallas_call(
        _route_body,
        grid=(N_TOK // ROUTE_TM,),
        in_specs=[
            pl.BlockSpec((ROUTE_TM, D_MODEL), lambda i: (i, 0)),
            pl.BlockSpec((N_EXPERTS, D_MODEL), lambda i: (0, 0)),
            pl.BlockSpec((N_EXPERTS, 1), lambda i: (0, 0)),
        ],
        out_specs=[
            pl.BlockSpec((k, ROUTE_TM), tok),
            pl.BlockSpec((k, ROUTE_TM), tok),
            pl.BlockSpec((k, ROUTE_TM), tok),
            pl.BlockSpec((N_EXPERTS, 1), lambda i: (0, 0)),
        ],
        out_shape=[
            jax.ShapeDtypeStruct((k, N_TOK), I32),
            jax.ShapeDtypeStruct((k, N_TOK), F32),
            jax.ShapeDtypeStruct((k, N_TOK), I32),
            jax.ShapeDtypeStruct((N_EXPERTS, 1), F32),
        ],
        scratch_shapes=[pltpu.VMEM((N_EXPERTS, 1), F32)],
        compiler_params=_cparams("arbitrary"),
        name="route",
    )(h_bf, w_router_t, b_router_col)


def _dest_body(cnt_ref, e_ref, rk_ref, d_ref, te_ref, nu_ref, lt_ref, pad_ref):
    cnt = cnt_ref[...]
    tiles = jnp.floor((cnt + (ROW_TILE - 1)) * (1.0 / ROW_TILE))
    er = lax.broadcasted_iota(I32, (N_EXPERTS, N_EXPERTS), 0)
    ec = lax.broadcasted_iota(I32, (N_EXPERTS, N_EXPERTS), 1)
    before = jnp.where(ec < er, 1.0, 0.0).astype(BF16)
    tiles_b = jnp.broadcast_to(tiles, (N_EXPERTS, LANES)).astype(BF16)
    t_start = _dot(before, tiles_b)[:, 0:1]
    t_end = t_start + tiles
    p_start = t_start * float(ROW_TILE)
    lt_ref[...] = jnp.where(tiles > 0.0, (t_end - 1.0) * float(ROW_TILE), -1.0).astype(I32)
    pad_ref[...] = (p_start + cnt).astype(I32)

    row = lax.broadcasted_iota(I32, (N_EXPERTS, DEST_TM), 0)
    d_ref[...] = jnp.concatenate(
        [jnp.sum(jnp.where(row == e_ref[k:k + 1, :], p_start, 0.0), axis=0, keepdims=True)
         for k in range(TOPK_EXPERTS)], axis=0).astype(I32) + rk_ref[...]

    tile_id = lax.broadcasted_iota(I32, (N_EXPERTS, TILE_TABLE), 1).astype(F32)
    owner = jnp.sum(jnp.where(t_end <= tile_id, 1, 0), axis=0, keepdims=True)
    te_ref[...] = jnp.minimum(owner, N_EXPERTS - 1)
    nu_ref[...] = jnp.broadcast_to(t_end[N_EXPERTS - 1:N_EXPERTS, :], (1, LANES)).astype(I32)


def _dest(cnt, e_t, rk_t):
    k = TOPK_EXPERTS
    tok = lambda i: (0, i)
    return pl.pallas_call(
        _dest_body,
        grid=(N_TOK // DEST_TM,),
        in_specs=[
            pl.BlockSpec((N_EXPERTS, 1), lambda i: (0, 0)),
            pl.BlockSpec((k, DEST_TM), tok),
            pl.BlockSpec((k, DEST_TM), tok),
        ],
        out_specs=[
            pl.BlockSpec((k, DEST_TM), tok),
            pl.BlockSpec((1, TILE_TABLE), lambda i: (0, 0)),
            pl.BlockSpec((1, LANES), lambda i: (0, 0)),
            pl.BlockSpec((N_EXPERTS, 1), lambda i: (0, 0)),
            pl.BlockSpec((N_EXPERTS, 1), lambda i: (0, 0)),
        ],
        out_shape=[
            jax.ShapeDtypeStruct((k, N_TOK), I32),
            jax.ShapeDtypeStruct((1, TILE_TABLE), I32),
            jax.ShapeDtypeStruct((1, LANES), I32),
            jax.ShapeDtypeStruct((N_EXPERTS, 1), I32),
            jax.ShapeDtypeStruct((N_EXPERTS, 1), I32),
        ],
        compiler_params=_cparams("arbitrary"),
        name="dest",
    )(cnt, e_t, rk_t)


def _row_copy(src, dst, sem):
    return pltpu.make_async_copy(src, dst, sem)


def _packed_rows(ref, first, n):
    start = first * PACKED_CHUNKS
    if not isinstance(first, int):
        start = pl.multiple_of(start, PACKED_CHUNKS)
    return ref.at[pl.ds(start, n * PACKED_CHUNKS), :]


def _dispatch_body(lt_ref, pad_ref, nu_ref, d_ref, h_ref, xs_ref, zero_ref, sem, zero_sem):
    step = pl.program_id(0)
    n_steps = N_TOK // DISP_TM
    experts_per_step = N_EXPERTS // n_steps
    assert experts_per_step * n_steps == N_EXPERTS
    tail_per_step = -(-(N_ROW_TILES - N_ASSIGN // ROW_TILE) // n_steps)

    @pl.when(step == 0)
    def _():
        zero_ref[...] = jnp.zeros_like(zero_ref)

    def zero_fills(act):
        for j in range(experts_per_step):
            e = step * experts_per_step + j
            row = pad_ref[e]
            n_pad = jnp.where(lt_ref[e] >= 0, lt_ref[e] + ROW_TILE - row, 0)
            for bit in reversed(range(ROW_TILE.bit_length() - 1)):
                run = 1 << bit
                has_run = (n_pad & run) != 0

                @pl.when(has_run)
                def _(row=row, run=run):
                    act(pltpu.make_async_copy(_packed_rows(zero_ref, 0, run), _packed_rows(xs_ref, row, run),
                                              zero_sem))

                row = row + jnp.where(has_run, run, 0)
        for j in range(tail_per_step):
            tile = nu_ref[0] + step * tail_per_step + j

            @pl.when(tile < N_ROW_TILES)
            def _(tile=tile):
                act(pltpu.make_async_copy(zero_ref, _packed_rows(xs_ref, tile * ROW_TILE, ROW_TILE), zero_sem))

    zero_fills(lambda c: c.start())
    for t in range(DISP_TM):
        for k in range(TOPK_EXPERTS):
            _row_copy(_packed_rows(h_ref, t, 1), _packed_rows(xs_ref, d_ref[k, t], 1), sem).start(priority=k % 2)
    for k in range(TOPK_EXPERTS):
        pltpu.make_async_copy(h_ref, _packed_rows(xs_ref, 0, DISP_TM), sem).wait()
    zero_fills(lambda c: c.wait())


def _dispatch(last_tile_row, pad_row, n_used, dest_t, h1):
    grid_spec = pltpu.PrefetchScalarGridSpec(
        num_scalar_prefetch=3,
        grid=(N_TOK // DISP_TM,),
        in_specs=[
            pl.BlockSpec((TOPK_EXPERTS, DISP_TM), lambda i, lt, pad, nu: (0, i), memory_space=pltpu.SMEM),
            pl.BlockSpec((DISP_TM * PACKED_CHUNKS, LANES), lambda i, lt, pad, nu: (i, 0)),
        ],
        out_specs=pl.BlockSpec(memory_space=pl.ANY),
        scratch_shapes=[pltpu.VMEM((ROW_TILE * PACKED_CHUNKS, LANES), U32), pltpu.SemaphoreType.DMA,
                        pltpu.SemaphoreType.DMA],
    )
    return pl.pallas_call(
        _dispatch_body,
        grid_spec=grid_spec,
        out_shape=jax.ShapeDtypeStruct((N_ROWS * PACKED_CHUNKS, LANES), U32),
        compiler_params=_cparams("arbitrary"),
        name="dispatch",
    )(last_tile_row, pad_row, n_used, dest_t, h1)


def _row_destinations(dest_flat):
    info = plsc.get_sparse_core_info()
    n_cores, lanes = info.num_cores, info.num_lanes
    n_workers = n_cores * info.num_subcores
    rows_per_worker = -(-N_ROWS // (n_workers * ROW_TILE)) * ROW_TILE
    assert rows_per_worker % lanes == 0
    mesh = plsc.VectorSubcoreMesh(core_axis_name="c", subcore_axis_name="s")

    @functools.partial(
        pl.kernel, mesh=mesh, out_type=jax.ShapeDtypeStruct((n_workers * rows_per_worker,), I32),
        scratch_types=[pltpu.VMEM((SLOT_CHUNK,), I32), pltpu.VMEM((rows_per_worker,), I32)],
        compiler_params=pltpu.CompilerParams(needs_layout_passes=False), name="row_destinations")
    def invert(dest_hbm, out_hbm, dest_v, table_v):
        first_row = (lax.axis_index("s") * n_cores + lax.axis_index("c")) * rows_per_worker
        lane = lax.iota(I32, lanes)
        tile_shift = ROW_TILE.bit_length() - 1

        def spare(i, carry):
            row = first_row + i * lanes + lane
            buf = lax.rem(lax.shift_right_logical(row, tile_shift), EXPERT_BUFS)
            table_v[pl.ds(i * lanes, lanes)] = TOPK_EXPERTS * N_TOK + buf * ROW_TILE + (row & (ROW_TILE - 1))
            return carry

        lax.fori_loop(0, rows_per_worker // lanes, spare, 0)

        def chunk(c, carry):
            pltpu.sync_copy(dest_hbm.at[pl.ds(c * SLOT_CHUNK, SLOT_CHUNK)], dest_v)

            def vec(i, inner):
                local = dest_v[pl.ds(i * lanes, lanes)] - first_row
                mine = (local >= 0) & (local < rows_per_worker)
                plsc.store_scatter(table_v, [jnp.where(mine, local, 0)], lane + (c * SLOT_CHUNK + i * lanes),
                                   mask=mine)
                return inner

            lax.fori_loop(0, SLOT_CHUNK // lanes, vec, 0)
            return carry

        lax.fori_loop(0, N_ASSIGN // SLOT_CHUNK, chunk, 0)
        pltpu.sync_copy(table_v, out_hbm.at[pl.ds(first_row, rows_per_worker)])

    return invert(dest_flat)


def _experts_body(te_ref, nu_ref, lt_ref, slot_ref, xs_ref, wg_ref, wu_ref, wd_ref, yk_ref,
                  wg_f, wu_f, wd_f, wg_b, wu_b, wd_b, y_buf, n_loaded, sem, w_sem):
    i = pl.program_id(0)
    n_used = nu_ref[0]

    def weight_copies(e, s):
        return (pltpu.make_async_copy(wg_ref.at[e], wg_f.at[s], w_sem.at[s]),
                pltpu.make_async_copy(wu_ref.at[e], wu_f.at[s], w_sem.at[s]),
                pltpu.make_async_copy(wd_ref.at[e], wd_f.at[s], w_sem.at[s]))

    def wait_tile(b):
        pltpu.make_async_copy(y_buf.at[b], _packed_rows(yk_ref, 0, ROW_TILE), sem.at[b]).wait()

    def send_tile(b):
        for r in range(ROW_TILE):
            dst = slot_ref[0, 0, r]
            _row_copy(_packed_rows(y_buf.at[b], r, 1), _packed_rows(yk_ref, dst, 1),
                      sem.at[b]).start(priority=r % 2)

    @pl.when(i == 0)
    def _():
        y_buf[EXPERT_BUFS - 1] = jnp.zeros((ROW_TILE * PACKED_CHUNKS, LANES), U32)
        spare = [pltpu.make_async_copy(y_buf.at[EXPERT_BUFS - 1],
                                       _packed_rows(yk_ref, TOPK_EXPERTS * N_TOK + b * ROW_TILE, ROW_TILE),
                                       sem.at[b]) for b in range(EXPERT_BUFS - 1)]
        for c in spare:
            c.start()
        for c in spare:
            c.wait()
        n_loaded[0] = 0
        for c in weight_copies(te_ref[0], 0):
            c.start()

    @pl.when((i >= 2) & (i <= n_used))
    def _():
        wait_tile(lax.rem(i, EXPERT_BUFS))

    @pl.when(i < n_used)
    def _():
        e = te_ref[i]
        prev = te_ref[jnp.maximum(i - 1, 0)]

        @pl.when((i == 0) | (e != prev))
        def _():
            s = lax.rem(n_loaded[0], 2)
            for c in weight_copies(e, s):
                c.wait()
            wg_b[...] = wg_f[s].astype(BF16)
            wu_b[...] = wu_f[s].astype(BF16)
            wd_b[...] = wd_f[s].astype(BF16)
            n_loaded[0] = n_loaded[0] + 1
            nxt = lax.shift_right_logical(lt_ref[e], ROW_TILE.bit_length() - 1) + 1

            @pl.when(nxt < n_used)
            def _():
                for c in weight_copies(te_ref[nxt], 1 - s):
                    c.start()

        for phase in range(EXPERT_BUFS):
            @pl.when(lax.rem(i, EXPERT_BUFS) == phase)
            def _(phase=phase):
                send_tile((phase + EXPERT_BUFS - 1) % EXPERT_BUFS)
                packed = jnp.concatenate([xs_ref[pl.ds(c, ROW_TILE, stride=PACKED_CHUNKS), :]
                                          for c in range(PACKED_CHUNKS)], axis=1)
                x_lo, x_hi = _unpack_bf16_pair(packed)
                x = jnp.concatenate([x_lo.astype(BF16), x_hi.astype(BF16)], axis=1)
                g = _dot(x, wg_b[...])
                u = _dot(x, wu_b[...])
                h = (g * _sigmoid(g)) * u
                y = _pack_bf16_pair(_dot(h.astype(BF16), wd_b[...]))
                for c in range(PACKED_CHUNKS):
                    y_buf[phase, pl.ds(c, ROW_TILE, stride=PACKED_CHUNKS), :] = y[:, c * LANES:(c + 1) * LANES]

    @pl.when(i == n_used)
    def _():
        send_tile(lax.rem(i + EXPERT_BUFS - 1, EXPERT_BUFS))
        wait_tile(lax.rem(i + EXPERT_BUFS - 2, EXPERT_BUFS))
        wait_tile(lax.rem(i + EXPERT_BUFS - 1, EXPERT_BUFS))


def _experts(tile_expert, n_used, last_tile_row, row_dst, xs, w_gate, w_up, w_down):
    def tile(i, te, nu, lt):
        return (jnp.minimum(i, nu[0] - 1), 0)

    n_table_tiles = row_dst.shape[0] // ROW_TILE
    placeholder = n_table_tiles - 1
    assert placeholder * ROW_TILE >= N_ROWS and placeholder % EXPERT_BUFS == EXPERT_BUFS - 1

    def prev_tile(i, te, nu, lt):
        return (jnp.where(i == 0, placeholder, jnp.minimum(i, nu[0]) - 1), 0, 0)

    grid_spec = pltpu.PrefetchScalarGridSpec(
        num_scalar_prefetch=3,
        grid=(n_used[0] + 1,),
        in_specs=[
            pl.BlockSpec((1, 1, ROW_TILE), prev_tile, memory_space=pltpu.SMEM),
            pl.BlockSpec((ROW_TILE * PACKED_CHUNKS, LANES), tile),
            pl.BlockSpec(memory_space=pl.ANY),
            pl.BlockSpec(memory_space=pl.ANY),
            pl.BlockSpec(memory_space=pl.ANY),
        ],
        out_specs=pl.BlockSpec(memory_space=pl.ANY),
        scratch_shapes=[
            pltpu.VMEM((2, D_MODEL, D_EXPERT), F32),
            pltpu.VMEM((2, D_MODEL, D_EXPERT), F32),
            pltpu.VMEM((2, D_EXPERT, D_MODEL), F32),
            pltpu.VMEM((D_MODEL, D_EXPERT), BF16),
            pltpu.VMEM((D_MODEL, D_EXPERT), BF16),
            pltpu.VMEM((D_EXPERT, D_MODEL), BF16),
            pltpu.VMEM((EXPERT_BUFS, ROW_TILE * PACKED_CHUNKS, LANES), U32),
            pltpu.SMEM((1,), I32),
            pltpu.SemaphoreType.DMA((EXPERT_BUFS,)),
            pltpu.SemaphoreType.DMA((2,)),
        ],
    )
    return pl.pallas_call(
        _experts_body,
        grid_spec=grid_spec,
        out_shape=jax.ShapeDtypeStruct((YK_ROWS * PACKED_CHUNKS, LANES), U32),
        compiler_params=_cparams("arbitrary"),
        name="experts",
    )(tile_expert, n_used, last_tile_row, row_dst.reshape(n_table_tiles, 1, ROW_TILE), xs, w_gate, w_up, w_down)


def _final_body(w_ref, *refs):
    yk_refs = refs[:TOPK_EXPERTS]
    h_ref, hb_ref, p_ref, wsg_ref, wsu_ref, wsd_ref, wpl_ref, wpg_ref, g_ref, b_ref, o_ref = refs[TOPK_EXPERTS:]
    hb = hb_ref[...]
    sg = _dot(hb, wsg_ref[...])
    shared = _dot(((sg * _sigmoid(sg)) * _dot(hb, wsu_ref[...])).astype(BF16), wsd_ref[...])
    ple = _sigmoid(_dot(hb, wpg_ref[...])) * _dot(p_ref[...].astype(BF16), wpl_ref[...])

    def plane(k):
        return jnp.concatenate([yk_refs[k][pl.ds(c, FINAL_TM, stride=PACKED_CHUNKS), :]
                                for c in range(PACKED_CHUNKS)], axis=1)

    w_col = w_ref[...].T
    y_lo, y_hi = _unpack_bf16_pair(plane(0))
    r_lo, r_hi = y_lo * w_col[:, 0:1], y_hi * w_col[:, 0:1]
    for k in range(1, TOPK_EXPERTS):
        y_lo, y_hi = _unpack_bf16_pair(plane(k))
        r_lo, r_hi = r_lo + y_lo * w_col[:, k:k + 1], r_hi + y_hi * w_col[:, k:k + 1]
    routed = jnp.concatenate([r_lo, r_hi], axis=1)
    o_ref[...] = _layer_norm(ALPHA * h_ref[...] + (routed + shared) + ple, g_ref[...], b_ref[...])


def _final(w_t, yk, h1, h_bf, p2d, w_s_gate, w_s_up, w_s_down, w_ple, w_ple_gate, ln_g, ln_b):
    d = D_MODEL
    n_steps = N_TOK // FINAL_TM
    full = lambda i: (0, 0)
    return pl.pallas_call(
        _final_body,
        grid=(n_steps,),
        in_specs=[
            pl.BlockSpec((TOPK_EXPERTS, FINAL_TM), lambda i: (0, i)),
            *[pl.BlockSpec((FINAL_TM * PACKED_CHUNKS, LANES), lambda i, k=k: (k * n_steps + i, 0))
              for k in range(TOPK_EXPERTS)],
            pl.BlockSpec((FINAL_TM, d), lambda i: (i, 0)),
            pl.BlockSpec((FINAL_TM, d), lambda i: (i, 0)),
            pl.BlockSpec((FINAL_TM, PLE_DIM), lambda i: (i, 0)),
            pl.BlockSpec((d, D_SHARED), full),
            pl.BlockSpec((d, D_SHARED), full),
            pl.BlockSpec((D_SHARED, d), full),
            pl.BlockSpec((PLE_DIM, d), full),
            pl.BlockSpec((d, d), full),
            pl.BlockSpec((1, d), full),
            pl.BlockSpec((1, d), full),
        ],
        out_specs=pl.BlockSpec((FINAL_TM, d), lambda i: (i, 0)),
        out_shape=jax.ShapeDtypeStruct((N_TOK, d), F32),
        compiler_params=_cparams("parallel"),
        name="combine_ln2",
    )(w_t, *([yk] * TOPK_EXPERTS), h1, h_bf, p2d, w_s_gate, w_s_up, w_s_down, w_ple, w_ple_gate, ln_g, ln_b)


def _rope_tables(positions):
    half = ROT_DIM // 2
    inv = ROPE_THETA ** (-jnp.arange(0, ROT_DIM, 2, dtype=F32) / ROT_DIM)
    per_row = LANES // half
    pos = jnp.repeat(positions.reshape(-1, per_row), half, axis=1).astype(F32)
    ang = pos * jnp.tile(inv, per_row)[None, :]
    cos, sin = lax.optimization_barrier((jnp.cos(ang), jnp.sin(ang)))
    cos = cos.reshape(positions.shape + (half,))
    sin = sin.reshape(positions.shape + (half,))
    rest = MOBA_DH - ROT_DIM
    ones = jnp.ones(cos.shape[:-1] + (rest,), F32)
    zeros = jnp.zeros(cos.shape[:-1] + (rest,), F32)
    return (jnp.concatenate([cos, cos, ones], axis=-1), jnp.concatenate([-sin, sin, zeros], axis=-1))


def _layer(h2d, p2d, cos_t, sin_t, w_cat, w_gk2, b_gk, norm_g, w_gla_o, w_moba_o, w_out, ln1_g, ln1_b,
           w_router, b_router, w_e_gate, w_e_up, w_e_down, w_s_gate, w_s_up, w_s_down, w_ple, w_ple_gate,
           ln2_g, ln2_b):
    w_gk2_pad = jnp.concatenate([w_gk2, jnp.zeros((LANES - GLA_RANK, GLA_KDIM), w_gk2.dtype)], axis=0)

    proj = _in_proj(h2d, w_cat)
    gla_out = _gla(proj, w_gk2_pad, b_gk[None, :], norm_g[None, :])
    moba_out = _moba(proj, cos_t, sin_t)
    h1, h_bf, h_packed = _mix(gla_out, moba_out, proj, h2d, w_gla_o.astype(BF16), w_moba_o.astype(BF16),
                              w_out.astype(BF16), ln1_g[None, :], ln1_b[None, :])
    e_t, w_t, rk_t, cnt = _route(h_bf, w_router.T.astype(BF16), b_router[:, None])
    dest_t, tile_expert, n_used, last_tile_row, pad_row = _dest(cnt, e_t, rk_t)
    last_tile_row = last_tile_row.reshape(N_EXPERTS)
    n_used = n_used[0, 0:1]
    xs = _dispatch(last_tile_row, pad_row.reshape(N_EXPERTS), n_used, dest_t, h_packed)
    row_dst = _row_destinations(dest_t.reshape(N_ASSIGN))
    yk = _experts(tile_expert.reshape(TILE_TABLE), n_used, last_tile_row, row_dst, xs,
                  w_e_gate, w_e_up, w_e_down)
    return _final(w_t, yk, h1, h_bf, p2d, w_s_gate.astype(BF16), w_s_up.astype(BF16),
                  w_s_down.astype(BF16), w_ple.astype(BF16), w_ple_gate.astype(BF16),
                  ln2_g[None, :], ln2_b[None, :])


def kernel(x, p, positions, w_in, w_gla_gk2, b_gla_gk, gla_norm_g, w_gla_o, w_moba_o, w_out, ln1_g, ln1_b,
           w_router, b_router, w_e_gate, w_e_up, w_e_down, w_s_gate, w_s_up, w_s_down, w_ple, w_ple_gate,
           ln2_g, ln2_b):
    cos_t, sin_t = _rope_tables(positions)
    h = x.reshape(N_TOK, D_MODEL)
    for i in range(DEPTH):
        h = _layer(h, p[i].reshape(N_TOK, PLE_DIM), cos_t, sin_t, _repack_w_in(w_in, i), w_gla_gk2[i], b_gla_gk[i],
                   gla_norm_g[i], w_gla_o[i], w_moba_o[i], w_out[i], ln1_g[i], ln1_b[i], w_router[i],
                   b_router[i], w_e_gate[i], w_e_up[i], w_e_down[i], w_s_gate[i], w_s_up[i], w_s_down[i],
                   w_ple[i], w_ple_gate[i], ln2_g[i], ln2_b[i])
    return h.reshape(BATCH, SEQ, D_MODEL)
```
